```python
import jax, jax.numpy as jnp
from jax import lax
import numpy as np

D_MODEL = 1024
BATCH = 16
SEQ = 256
DEPTH = 4
DEC_BATCH = 4
DEC_SEQ = 1024
PAST_LEN = 256

GRID_W = 64
EPS = 1e-6
N_MIXERS = 3
MLSTM_HEADS = 4
MLSTM_INNER = 2 * D_MODEL
MLSTM_HEAD_DIM = MLSTM_INNER // MLSTM_HEADS
MLSTM_CHUNK = 64
F_BIAS_LO = 3.0
F_BIAS_HI = 6.0
FNET_GROUPS = 4
CONV_WIDTH = 31
CONV_DIM = D_MODEL
MOE_GROUPS = 4
MOE_EXPERTS_PER_GROUP = 8
MOE_EXPERTS = MOE_GROUPS * MOE_EXPERTS_PER_GROUP
MOE_TOP_K = 2
MOE_HIDDEN = D_MODEL // 2
MOE_BLOCK = 128
N_MLSTM_LAYERS = (DEPTH + 2) // 3
N_FOURIER_LAYERS = (DEPTH + 1) // 3
N_CONV_LAYERS = DEPTH // 3

kernel_name = 'hybrid_mlstm_fnet_conformer_hmoe_step'


def rms_norm(x, w):
    xf = x.astype(jnp.float32)
    y = xf * lax.rsqrt(jnp.mean(xf * xf, axis=-1, keepdims=True) + EPS)
    return (y * w.astype(jnp.float32)).astype(x.dtype)


def layer_norm(x, w, b):
    xf = x.astype(jnp.float32)
    mu = jnp.mean(xf, axis=-1, keepdims=True)
    var = jnp.mean(jnp.square(xf - mu), axis=-1, keepdims=True)
    y = (xf - mu) * lax.rsqrt(var + EPS)
    return (y * w.astype(jnp.float32) + b.astype(jnp.float32)).astype(x.dtype)


def ada_params(cvec, w, b):
    return jnp.split(jax.nn.silu(cvec) @ w + b, 6, axis=-1)


def modulate(x, g, shift, scale):
    return rms_norm(x, g) * (1 + scale) + shift


def mlstm_scan(q, k, v, log_i, log_f, C0, n0, m0):
    B, H, S, _ = q.shape
    L = MLSTM_CHUNK
    nc = S // L

    def chunks(t):
        return jnp.moveaxis(t.reshape(B, H, nc, L, *t.shape[3:]), 2, 0)

    lower = jnp.tril(jnp.ones((L, L), dtype=bool))

    def step(carry, xs):
        C, n, m = carry
        qc, kc, vc, li, lf = xs
        b = jnp.cumsum(lf, axis=-1)
        log_d = jnp.where(lower, b[..., :, None] - b[..., None, :] + li[..., None, :], -jnp.inf)
        log_prev = b + m[..., None]
        m_t = jnp.maximum(log_prev, jnp.max(log_d, axis=-1))
        w_intra = jnp.exp(log_d - m_t[..., None])
        w_prev = jnp.exp(log_prev - m_t)
        s = jnp.einsum('bhtd,bhsd->bhts', qc, kc) * w_intra
        num = jnp.einsum('bhts,bhsv->bhtv', s, vc) + w_prev[..., None] * jnp.einsum('bhtd,bhdv->bhtv', qc, C)
        den = jnp.sum(s, axis=-1) + w_prev * jnp.einsum('bhtd,bhd->bht', qc, n)
        h = num / jnp.maximum(jnp.abs(den), jnp.exp(-m_t))[..., None]
        b_end = b[..., -1]
        log_end = b_end[..., None] - b + li
        m_new = jnp.maximum(b_end + m, jnp.max(log_end, axis=-1))
        w_end = jnp.exp(log_end - m_new[..., None])
        decay = jnp.exp(b_end + m - m_new)
        kw = kc * w_end[..., None]
        C_new = decay[..., None, None] * C + jnp.einsum('bhsd,bhsv->bhdv', kw, vc)
        n_new = decay[..., None] * n + jnp.sum(kw, axis=2)
        return (C_new, n_new, m_new), h

    carry, h = lax.scan(step, (C0, n0, m0), (chunks(q), chunks(k), chunks(v), chunks(log_i), chunks(log_f)))
    h = jnp.moveaxis(h, 0, 2).reshape(B, H, S, v.shape[-1])
    return h, carry


def mlstm_mixer(u, w_in, b_gate, head_norm_w, w_out, C0, n0, m0):
    B, S, _ = u.shape
    DI, H, DH = MLSTM_INNER, MLSTM_HEADS, MLSTM_HEAD_DIM
    proj = u @ w_in
    q, k, v, o_pre, g_pre = jnp.split(proj, [DI, 2 * DI, 3 * DI, 4 * DI], axis=-1)

    def heads(t):
        return t.reshape(B, S, H, DH).transpose(0, 2, 1, 3).astype(jnp.float32)

    q = heads(q) * (DH ** -0.5)
    k = heads(k)
    v = heads(v)
    g = (g_pre.astype(jnp.float32) + b_gate.astype(jnp.float32)).reshape(B, S, 4, H).transpose(2, 0, 3, 1)
    i_fwd, f_fwd, i_bwd, f_bwd = g[0], g[1], g[2], g[3]
    C0 = C0.astype(jnp.float32)
    n0 = n0.astype(jnp.float32)
    m0 = m0.astype(jnp.float32)
    h_f, (Cf, nf, mf) = mlstm_scan(q, k, v, i_fwd, jax.nn.log_sigmoid(f_fwd), C0[:, 0], n0[:, 0], m0[:, 0])

    def flip(t):
        return jnp.flip(t, axis=2)

    h_b, (Cb, nb, mb) = mlstm_scan(flip(q), flip(k), flip(v), flip(i_bwd), flip(jax.nn.log_sigmoid(f_bwd)),
                                   C0[:, 1], n0[:, 1], m0[:, 1])
    h = h_f + flip(h_b)
    h = h * lax.rsqrt(jnp.mean(h * h, axis=-1, keepdims=True) + EPS)
    h = h.transpose(0, 2, 1, 3).reshape(B, S, DI) * head_norm_w.astype(jnp.float32)
    h = h.astype(u.dtype) * jax.nn.sigmoid(o_pre)
    state = (jnp.stack([Cf, Cb], axis=1), jnp.stack([nf, nb], axis=1), jnp.stack([mf, mb], axis=1))
    return h @ w_out, state


def fourier_mixer(u, w_out):
    B, S, D = u.shape
    ug = u.astype(jnp.float32).reshape(B, S, FNET_GROUPS, D // FNET_GROUPS)
    y = jnp.fft.fftn(ug, axes=(1, 3), norm='ortho').real
    return y.reshape(B, S, D).astype(u.dtype) @ w_out


def conformer_conv(u, w_pw1, b_pw1, w_dw, b_dw, ln_w, ln_b, w_pw2, b_pw2):
    h = u @ w_pw1 + b_pw1
    h = h[..., :CONV_DIM] * jax.nn.sigmoid(h[..., CONV_DIM:])
    h = lax.conv_general_dilated(h, w_dw[:, None, :].astype(h.dtype), window_strides=(1,),
                                 padding=[(CONV_WIDTH // 2, CONV_WIDTH // 2)],
                                 dimension_numbers=('NWC', 'WIO', 'NWC'),
                                 feature_group_count=CONV_DIM) + b_dw
    h = jax.nn.silu(layer_norm(h, ln_w, ln_b))
    return h @ w_pw2 + b_pw2


def grouped_experts(xf, expert_idx, gates, w13, w2):
    T, D = xf.shape
    TK = T * MOE_TOP_K
    flat_e = expert_idx.reshape(TK)
    order = jnp.argsort(flat_e)
    e_sorted = flat_e[order]
    tok_sorted = order // MOE_TOP_K
    gate_sorted = gates.reshape(TK)[order]
    counts = jnp.bincount(flat_e, length=MOE_EXPERTS)
    padded = (counts + MOE_BLOCK - 1) // MOE_BLOCK * MOE_BLOCK
    pad_end = jnp.cumsum(padded)
    pad_start = pad_end - padded
    start = jnp.cumsum(counts) - counts
    dest = pad_start[e_sorted] + jnp.arange(TK) - start[e_sorted]
    nb = -(-TK // MOE_BLOCK) + MOE_EXPERTS
    tok_buf = jnp.full((nb * MOE_BLOCK,), T, jnp.int32).at[dest].set(tok_sorted)
    x_pad = jnp.concatenate([xf, jnp.zeros((1, D), xf.dtype)], axis=0)
    block_e = jnp.minimum(jnp.searchsorted(pad_end, jnp.arange(nb) * MOE_BLOCK, side='right'), MOE_EXPERTS - 1)

    def run_block(args):
        tok_b, e = args
        hb = x_pad[tok_b] @ w13[e]
        return (jax.nn.silu(hb[:, :MOE_HIDDEN]) * hb[:, MOE_HIDDEN:]) @ w2[e]

    out = lax.map(run_block, (tok_buf.reshape(nb, MOE_BLOCK), block_e)).reshape(nb * MOE_BLOCK, D)
    contrib = out[dest] * gate_sorted[:, None].astype(xf.dtype)
    return jax.ops.segment_sum(contrib, tok_sorted, num_segments=T)


def hier_moe(u, w_group, w_expert, w13, w2):
    B, S, D = u.shape
    xf = u.reshape(B * S, D)
    T = xf.shape[0]
    g_logits = (xf @ w_group).astype(jnp.float32)
    g_prob = jax.nn.softmax(g_logits, axis=-1)
    g_sel = jnp.argmax(g_logits, axis=-1)
    g_p = jnp.take_along_axis(g_prob, g_sel[:, None], axis=-1)
    e_logits = (xf @ w_expert).astype(jnp.float32).reshape(T, MOE_GROUPS, MOE_EXPERTS_PER_GROUP)
    e_sel = jnp.take_along_axis(e_logits, g_sel[:, None, None], axis=1)[:, 0]
    top_val, top_idx = lax.top_k(e_sel, MOE_TOP_K)
    gates = jax.nn.softmax(top_val, axis=-1) * g_p
    expert_idx = g_sel[:, None].astype(jnp.int32) * MOE_EXPERTS_PER_GROUP + top_idx.astype(jnp.int32)
    return grouped_experts(xf, expert_idx, gates, w13, w2).reshape(B, S, D)


def setup_inputs(seed: int = 0) -> dict:
    key = jax.random.key(seed)
    ks = iter(jax.random.split(key, 32))

    def nrm(shape, scale):
        return scale * jax.random.normal(next(ks), shape, jnp.float32)

    D, H, DH, DI = D_MODEL, MLSTM_HEADS, MLSTM_HEAD_DIM, MLSTM_INNER
    f_bias = jnp.linspace(F_BIAS_LO, F_BIAS_HI, H, dtype=jnp.float32)
    zero_b = jnp.zeros((H,), jnp.float32)
    gate_base = jnp.concatenate([zero_b, f_bias, zero_b, f_bias])
    return {
        'x_prompt': nrm((BATCH, SEQ, D), 1.0),
        'x_sample': nrm((DEC_BATCH, DEC_SEQ, D), 1.0),
        'state_C': nrm((DEC_BATCH, N_MLSTM_LAYERS, 2, H, DH, DH), 0.1),
        'state_n': nrm((DEC_BATCH, N_MLSTM_LAYERS, 2, H, DH), 0.1),
        'state_m': nrm((DEC_BATCH, N_MLSTM_LAYERS, 2, H), 1.0),
        'c': nrm((DEC_BATCH, D), 1.0),
        'c_ctx': nrm((D,), 1.0),
        'ada_w': nrm((DEPTH, D, 6 * D), 0.5 * D ** -0.5),
        'ada_b': nrm((DEPTH, 6 * D), 0.01),
        'norm1_w': 1.0 + nrm((DEPTH, D), 0.01),
        'norm2_w': 1.0 + nrm((DEPTH, D), 0.01),
        'm_w_in': nrm((N_MLSTM_LAYERS, D, 4 * DI + 4 * H), D ** -0.5),
        'm_b_gate': gate_base + nrm((N_MLSTM_LAYERS, 4 * H), 0.1),
        'm_head_norm_w': 1.0 + nrm((N_MLSTM_LAYERS, DI), 0.01),
        'm_w_out': nrm((N_MLSTM_LAYERS, DI, D), DI ** -0.5),
        'f_w_out': nrm((N_FOURIER_LAYERS, D, D), D ** -0.5),
        'cv_w_pw1': nrm((N_CONV_LAYERS, D, 2 * CONV_DIM), D ** -0.5),
        'cv_b_pw1': nrm((N_CONV_LAYERS, 2 * CONV_DIM), 0.01),
        'cv_w_dw': nrm((N_CONV_LAYERS, CONV_WIDTH, CONV_DIM), CONV_WIDTH ** -0.5),
        'cv_b_dw': nrm((N_CONV_LAYERS, CONV_DIM), 0.01),
        'cv_ln_w': 1.0 + nrm((N_CONV_LAYERS, CONV_DIM), 0.01),
        'cv_ln_b': nrm((N_CONV_LAYERS, CONV_DIM), 0.01),
        'cv_w_pw2': nrm((N_CONV_LAYERS, CONV_DIM, D), CONV_DIM ** -0.5),
        'cv_b_pw2': nrm((N_CONV_LAYERS, D), 0.01),
        'r_w_group': nrm((DEPTH, D, MOE_GROUPS), D ** -0.5),
        'r_w_expert': nrm((DEPTH, D, MOE_EXPERTS), D ** -0.5),
        'e_w13': nrm((DEPTH, MOE_EXPERTS, D, 2 * MOE_HIDDEN), D ** -0.5),
        'e_w2': nrm((DEPTH, MOE_EXPERTS, MOE_HIDDEN, D), MOE_HIDDEN ** -0.5),
        'final_norm_w': 1.0 + nrm((D,), 0.01),
    }


def reference(x_prompt, x_sample, state_C, state_n, state_m, c, c_ctx, ada_w, ada_b, norm1_w, norm2_w,
              m_w_in, m_b_gate, m_head_norm_w, m_w_out, f_w_out, cv_w_pw1, cv_b_pw1, cv_w_dw, cv_b_dw,
              cv_ln_w, cv_ln_b, cv_w_pw2, cv_b_pw2, r_w_group, r_w_expert, e_w13, e_w2, final_norm_w):
    rows = x_sample.shape[1] // GRID_W
    assert rows * GRID_W == x_sample.shape[1]
    xp, xs = x_prompt, x_sample
    nbp = xp.shape[0]
    H, DH = MLSTM_HEADS, MLSTM_HEAD_DIM
    new_C, new_n, new_m = [], [], []
    for l in range(DEPTH):
        j, kind = l // N_MIXERS, l % N_MIXERS
        mc = ada_params(c_ctx[None, None, :], ada_w[l], ada_b[l])
        ml = ada_params(c[:, None, :], ada_w[l], ada_b[l])
        up = modulate(xp, norm1_w[l], mc[0], mc[1])
        us = modulate(xs, norm1_w[l], ml[0], ml[1])
        if kind == 0:
            w = (m_w_in[j], m_b_gate[j], m_head_norm_w[j], m_w_out[j])
            op, (Cp, npv, mp) = mlstm_mixer(up, *w,
                                            jnp.zeros((nbp, 2, H, DH, DH), jnp.float32),
                                            jnp.zeros((nbp, 2, H, DH), jnp.float32),
                                            jnp.zeros((nbp, 2, H), jnp.float32))
            os_, _ = mlstm_mixer(us, *w, state_C[:, j], state_n[:, j], state_m[:, j])
            new_C.append(Cp)
            new_n.append(npv)
            new_m.append(mp)
        elif kind == 1:
            op = fourier_mixer(up, f_w_out[j])
            os_ = fourier_mixer(us, f_w_out[j])
        else:
            w = (cv_w_pw1[j], cv_b_pw1[j], cv_w_dw[j], cv_b_dw[j], cv_ln_w[j], cv_ln_b[j], cv_w_pw2[j], cv_b_pw2[j])
            op = conformer_conv(up, *w)
            os_ = conformer_conv(us, *w)
        xp = xp + mc[2] * op
        xs = xs + ml[2] * os_
        up = modulate(xp, norm2_w[l], mc[3], mc[4])
        us = modulate(xs, norm2_w[l], ml[3], ml[4])
        moe_w = (r_w_group[l], r_w_expert[l], e_w13[l], e_w2[l])
        xp = xp + mc[5] * hier_moe(up, *moe_w)
        xs = xs + ml[5] * hier_moe(us, *moe_w)
    y_prompt = rms_norm(xp, final_norm_w)
    y_sample = rms_norm(xs, final_norm_w)
    new_state_C = jnp.stack(new_C, axis=1)
    new_state_n = jnp.stack(new_n, axis=1)
    new_state_m = jnp.stack(new_m, axis=1)
    return (y_prompt, y_sample, new_state_C, new_state_n, new_state_m)
```

```python
import functools
import math

import numpy as np
import jax
import jax.numpy as jnp
from jax import lax
from jax.experimental import pallas as pl
from jax.experimental.pallas import tpu as pltpu

F32 = jnp.float32
BF16 = jnp.bfloat16
EPS = 1e-6
GRID_W = 64
N_MIXERS = 3
FNET_GROUPS = 4
CONV_WIDTH = 31
MOE_GROUPS = 4
MOE_TOP_K = 2

LANES = 128
SUBLANES = 8
MLSTM_L = 256
MOE_BM = 256
CONV_HALO = 16
VMEM_LIMIT = 56 * 1024 * 1024


def _cparams(sem, vmem=VMEM_LIMIT):
    return pltpu.CompilerParams(dimension_semantics=sem, vmem_limit_bytes=vmem)


def _bdot(a, b):
    return jnp.dot(a.astype(BF16), b.astype(BF16), preferred_element_type=F32)


def _rms(x, w):
    return x * lax.rsqrt(jnp.mean(x * x, axis=-1, keepdims=True) + EPS) * w


def _modulate(x, w, shift, scale):
    return _rms(x, w) * (1.0 + scale) + shift


def _sigmoid(x):
    return 1.0 / (1.0 + jnp.exp(-x))


def _log_sigmoid(x):
    return jnp.minimum(x, 0.0) - jnp.log(1.0 + jnp.exp(-jnp.abs(x)))


def _split3(x):
    hi = x.astype(BF16)
    r1 = x - hi.astype(F32)
    mid = r1.astype(BF16)
    lo = (r1 - mid.astype(F32)).astype(BF16)
    return hi, mid, lo


class _Layout:
    def __init__(self, nbp, sp, nbs, ss, d):
        self.nbp, self.sp, self.nbs, self.ss, self.d = nbp, sp, nbs, ss, d
        self.tp, self.ts = nbp * sp, nbs * ss
        self.t = self.tp + self.ts
        assert self.tp % ss == 0, "latent sequences must start on a block boundary of their own length"
        self.ngp = -(-(1 + nbs) // SUBLANES) * SUBLANES

    def group(self, row0):
        return jnp.where(row0 < self.tp, 0, 1 + (row0 - self.tp) // self.ss)

    def row_tile(self, want):
        tm = math.gcd(math.gcd(self.tp, self.ss), want)
        assert tm % SUBLANES == 0
        return tm


def _ada_kernel(cv_ref, w_ref, b_ref, o_ref):
    s = cv_ref[...]
    s = s * _sigmoid(s)
    o_ref[...] = _bdot(s, w_ref[...]) + b_ref[...]


def _ada_call(cv, ada_w, ada_b):
    depth, d, n = ada_w.shape
    ngp = cv.shape[0]
    tn = min(n, 2048)
    return pl.pallas_call(
        _ada_kernel,
        out_shape=jax.ShapeDtypeStruct((depth, ngp, n), F32),
        grid=(depth, n // tn),
        in_specs=[
            pl.BlockSpec((ngp, d), lambda l, j: (0, 0)),
            pl.BlockSpec((None, d, tn), lambda l, j: (l, 0, j)),
            pl.BlockSpec((None, 1, tn), lambda l, j: (l, 0, j)),
        ],
        out_specs=pl.BlockSpec((None, ngp, tn), lambda l, j: (l, 0, j)),
        compiler_params=_cparams(("parallel", "parallel")),
        name="ada_mods",
    )(cv, ada_w, ada_b.reshape(depth, 1, n))


def _mod_spec(lay, l, chunk, ngrid):
    if ngrid == 1:
        return pl.BlockSpec((None, lay.ngp, lay.d), lambda i: (l, 0, chunk))
    if ngrid == 2:
        return pl.BlockSpec((None, lay.ngp, lay.d), lambda i, j: (l, 0, chunk))
    return pl.BlockSpec((None, lay.ngp, lay.d), lambda i, j, k: (l, 0, chunk))


def _mod_rows(x_ref, w_ref, sh_ref, sc_ref, grp, dst_ref, rows, chunk=256):
    w = w_ref[...]
    sh = sh_ref[pl.ds(grp, 1), :]
    sc = sc_ref[pl.ds(grp, 1), :]
    chunk = min(chunk, rows)
    for r in range(0, rows, chunk):
        dst_ref[r:r + chunk, :] = _modulate(x_ref[r:r + chunk, :], w, sh, sc).astype(dst_ref.dtype)


def _proj_kernel(x_ref, nw_ref, sh_ref, sc_ref, w_ref, wg_ref, o_ref, og_ref, u_scr, *, lay, tm):
    i = pl.program_id(0)
    j = pl.program_id(1)

    @pl.when(j == 0)
    def _():
        _mod_rows(x_ref, nw_ref, sh_ref, sc_ref, lay.group(i * tm), u_scr, tm)
        og_ref[...] = jnp.dot(u_scr[...], wg_ref[...], preferred_element_type=F32)

    o_ref[...] = jnp.dot(u_scr[...], w_ref[...].astype(BF16), preferred_element_type=F32).astype(BF16)


def _proj_call(lay, x, nw, mods, l, w_in, jl, wg, nmain):
    t, d = x.shape
    tm = lay.row_tile(1024)
    tn = 1024
    kern = functools.partial(_proj_kernel, lay=lay, tm=tm)
    return pl.pallas_call(
        kern,
        out_shape=(jax.ShapeDtypeStruct((t, nmain), BF16), jax.ShapeDtypeStruct((t, LANES), F32)),
        grid=(t // tm, nmain // tn),
        in_specs=[
            pl.BlockSpec((tm, d), lambda i, j: (i, 0)),
            pl.BlockSpec((1, d), lambda i, j: (0, 0)),
            _mod_spec(lay, l, 0, 2),
            _mod_spec(lay, l, 1, 2),
            pl.BlockSpec((None, d, tn), lambda i, j: (jl, 0, j)),
            pl.BlockSpec((d, LANES), lambda i, j: (0, 0)),
        ],
        out_specs=(pl.BlockSpec((tm, tn), lambda i, j: (i, j)), pl.BlockSpec((tm, LANES), lambda i, j: (i, 0))),
        scratch_shapes=[pltpu.VMEM((tm, d), BF16)],
        compiler_params=_cparams(("parallel", "arbitrary")),
        name="mlstm_proj",
    )(x, nw, mods, mods, w_in, wg)


def _gate_prep_kernel(g_ref, b_ref, tri_ref, o_ref, *, nh):
    g = g_ref[...] + b_ref[...]
    lane = lax.broadcasted_iota(jnp.int32, g.shape, 1)
    is_f = ((lane >= nh) & (lane < 2 * nh)) | ((lane >= 3 * nh) & (lane < 4 * nh))
    lf = jnp.where(is_f, _log_sigmoid(g), 0.0)
    tri = tri_ref[...]
    hi, mid, lo = _split3(lf)
    prefix = (jnp.dot(tri, hi, preferred_element_type=F32) + jnp.dot(tri, mid, preferred_element_type=F32)
              + jnp.dot(tri, lo, preferred_element_type=F32))
    total = jnp.sum(lf, axis=0, keepdims=True)
    suffix = total - prefix + lf
    b = jnp.where(lane < 2 * nh, prefix, suffix)
    a = g - pltpu.roll(b, LANES - nh, 1)
    is_a = (lane < nh) | ((lane >= 2 * nh) & (lane < 3 * nh))
    o_ref[...] = jnp.where(is_a, a, b)


def _gate_prep_call(gates, bias, tri, nh):
    t = gates.shape[0]
    l = tri.shape[0]
    return pl.pallas_call(
        functools.partial(_gate_prep_kernel, nh=nh),
        out_shape=jax.ShapeDtypeStruct((t, LANES), F32),
        grid=(t // l,),
        in_specs=[
            pl.BlockSpec((l, LANES), lambda i: (i, 0)),
            pl.BlockSpec((1, LANES), lambda i: (0, 0)),
            pl.BlockSpec((l, l), lambda i: (0, 0)),
        ],
        out_specs=pl.BlockSpec((l, LANES), lambda i: (i, 0)),
        compiler_params=_cparams(("parallel",)),
        name="mlstm_gate_prep",
    )(gates, bias, tri)


def _col(tile, c):
    lane = lax.broadcasted_iota(jnp.int32, tile.shape, 1)
    return jnp.sum(jnp.where(lane == c, tile, 0.0), axis=-1, keepdims=True)


def _dir_masks(l):
    r = lax.broadcasted_iota(jnp.int32, (l, l), 0)
    c = lax.broadcasted_iota(jnp.int32, (l, l), 1)
    return c <= r, c >= r


def _head_epilogue(h, hw, o):
    hn = h * lax.rsqrt(jnp.mean(h * h, axis=-1, keepdims=True) + EPS) * hw
    return (hn * _sigmoid(o.astype(F32))).astype(BF16)


def _mlstm_single_kernel(q_ref, k_ref, v_ref, o_ref, gp_ref, gpt_ref, hw_ref, out_ref, *, nh, scale):
    h = pl.program_id(1)
    l = q_ref.shape[0]
    gp = gp_ref[...]
    qk = lax.dot_general(q_ref[...], k_ref[...], (((1,), (1,)), ((), ())), preferred_element_type=F32)
    masks = _dir_masks(l)
    p = None
    for d in range(2):
        a_r = gpt_ref[pl.ds(2 * nh * d + h, 1), :]
        b_c = _col(gp, 2 * nh * d + nh + h)
        g = jnp.where(masks[d], a_r, -jnp.inf)
        m = jnp.maximum(jnp.max(g, axis=-1, keepdims=True), 0.0)
        s = qk * jnp.exp(g - m) * scale
        den = jnp.sum(s, axis=-1, keepdims=True)
        inv = 1.0 / jnp.maximum(jnp.abs(den), jnp.exp(-(b_c + m)))
        p = s * inv if p is None else p + s * inv
    hh = jnp.dot(p.astype(BF16), v_ref[...], preferred_element_type=F32)
    out_ref[...] = _head_epilogue(hh, hw_ref[...], o_ref[...])


def _mlstm_multi_kernel(q_ref, k_ref, v_ref, o_ref, gp_ref, gpt_ref, hw_ref, c0_ref, n0_ref, m0_ref,
                        out_ref, hacc, cst, *, nh, nc, l, scale):
    h = pl.program_id(1)
    masks = _dir_masks(l)
    for d in range(2):
        cst[...] = c0_ref[d]
        n = n0_ref[pl.ds(d * nh + h, 1), :]
        m = m0_ref[pl.ds(d * nh + h, 1), 0:1]
        order = list(range(nc)) if d == 0 else list(range(nc - 1, -1, -1))
        for step, c in enumerate(order):
            r0 = c * l
            gp = gp_ref[r0:r0 + l, :]
            a_c = _col(gp, 2 * nh * d + h)
            b_c = _col(gp, 2 * nh * d + nh + h)
            a_r = gpt_ref[pl.ds(2 * nh * d + h, 1), r0:r0 + l]
            q = q_ref[r0:r0 + l, :]
            k = k_ref[r0:r0 + l, :]
            v = v_ref[r0:r0 + l, :]
            qk = lax.dot_general(q, k, (((1,), (1,)), ((), ())), preferred_element_type=F32)
            g = jnp.where(masks[d], a_r, -jnp.inf)
            mt = jnp.maximum(jnp.max(g, axis=-1, keepdims=True), m)
            s = qk * jnp.exp(g - mt) * scale
            w_prev = jnp.exp(m - mt) * scale
            qn = jnp.sum(q.astype(F32) * n, axis=-1, keepdims=True)
            den = jnp.sum(s, axis=-1, keepdims=True) + w_prev * qn
            inv = 1.0 / jnp.maximum(jnp.abs(den), jnp.exp(-(b_c + mt)))
            hc = jnp.dot((s * inv).astype(BF16), v, preferred_element_type=F32)
            hc = hc + (w_prev * inv) * jnp.dot(q, cst[...].astype(BF16), preferred_element_type=F32)
            if d == 0:
                hacc[r0:r0 + l, :] = hc
            else:
                hacc[r0:r0 + l, :] += hc
            if step + 1 < nc:
                m_last = jnp.max(mt, axis=0, keepdims=True)
                b_end = b_c[l - 1:l, :] if d == 0 else b_c[0:1, :]
                decay = jnp.exp(m - m_last)
                kw = k.astype(F32) * jnp.exp(a_c - m_last)
                cst[...] = decay * cst[...] + lax.dot_general(
                    kw.astype(BF16), v, (((0,), (0,)), ((), ())), preferred_element_type=F32)
                n = decay * n + jnp.sum(kw, axis=0, keepdims=True)
                m = b_end + m_last
    hw = hw_ref[...]
    for c in range(nc):
        r0 = c * l
        out_ref[r0:r0 + l, :] = _head_epilogue(hacc[r0:r0 + l, :], hw, o_ref[r0:r0 + l, :])


def _mlstm_call(lay, hg, qkvo, gp, gpt, hw, nh, dh, prompt, state=None):
    nb, s = (lay.nbp, lay.sp) if prompt else (lay.nbs, lay.ss)
    rb0 = 0 if prompt else lay.tp // s
    scale = dh ** -0.5
    common_in = [
        pl.BlockSpec((s, dh), lambda b, h: (rb0 + b, h)),
        pl.BlockSpec((s, dh), lambda b, h: (rb0 + b, nh + h)),
        pl.BlockSpec((s, dh), lambda b, h: (rb0 + b, 2 * nh + h)),
        pl.BlockSpec((s, dh), lambda b, h: (rb0 + b, 3 * nh + h)),
        pl.BlockSpec((s, LANES), lambda b, h: (rb0 + b, 0)),
        pl.BlockSpec((4 * nh, s), lambda b, h: (0, rb0 + b)),
        pl.BlockSpec((1, dh), lambda b, h: (0, h)),
    ]
    hg_spec = pl.BlockSpec((s, dh), lambda b, h: (rb0 + b, h))
    any_spec = pl.BlockSpec(memory_space=pl.ANY)
    if prompt:
        assert s == MLSTM_L
        kern = functools.partial(_mlstm_single_kernel, nh=nh, scale=scale)

        def body(hg_any, *refs):
            kern(*refs)

        return pl.pallas_call(
            body,
            out_shape=jax.ShapeDtypeStruct(hg.shape, hg.dtype),
            grid=(nb, nh),
            in_specs=[any_spec] + common_in,
            out_specs=hg_spec,
            input_output_aliases={0: 0},
            compiler_params=_cparams(("parallel", "parallel")),
            name="mlstm_prompt",
        )(hg, qkvo, qkvo, qkvo, qkvo, gp, gpt, hw)
    c0, n0, m0 = state
    nc = s // MLSTM_L
    kern = functools.partial(_mlstm_multi_kernel, nh=nh, nc=nc, l=MLSTM_L, scale=scale)

    def body(hg_any, *refs):
        kern(*refs)

    jl = c0[1]
    return pl.pallas_call(
        body,
        out_shape=jax.ShapeDtypeStruct(hg.shape, hg.dtype),
        grid=(nb, nh),
        in_specs=[any_spec] + common_in + [
            pl.BlockSpec((None, None, 2, None, dh, dh), lambda b, h: (b, jl, 0, h, 0, 0)),
            pl.BlockSpec((None, 2 * nh, dh), lambda b, h: (b, 0, 0)),
            pl.BlockSpec((None, 2 * nh, LANES), lambda b, h: (b, 0, 0)),
        ],
        out_specs=hg_spec,
        scratch_shapes=[pltpu.VMEM((s, dh), F32), pltpu.VMEM((dh, dh), F32)],
        input_output_aliases={0: 0},
        compiler_params=_cparams(("parallel", "parallel")),
        name="mlstm_latent",
    )(hg, qkvo, qkvo, qkvo, qkvo, gp, gpt, hw, c0[0], n0, m0)


def _state_kernel(*refs, nl, nh):
    ins, (c_ref, n_ref, m_ref) = refs[:3 * nl], refs[3 * nl:]
    lyr = pl.program_id(0)
    h = pl.program_id(2)

    @pl.when(h == 0)
    def _():
        m_ref[...] = jnp.zeros(m_ref.shape, F32)

    for jl in range(nl):
        k_ref, v_ref, gp_ref = ins[3 * jl:3 * jl + 3]

        @pl.when(lyr == jl)
        def _(k_ref=k_ref, v_ref=v_ref, gp_ref=gp_ref):
            l = k_ref.shape[0]
            gp = gp_ref[...]
            k = k_ref[...].astype(F32)
            v = v_ref[...]
            for d in range(2):
                a_c = _col(gp, 2 * nh * d + h)
                b_c = _col(gp, 2 * nh * d + nh + h)
                m_last = jnp.maximum(jnp.max(a_c, axis=0, keepdims=True), 0.0)
                b_end = b_c[l - 1:l, :] if d == 0 else b_c[0:1, :]
                kw = k * jnp.exp(a_c - m_last)
                c_ref[d] = lax.dot_general(kw.astype(BF16), v, (((0,), (0,)), ((), ())), preferred_element_type=F32)
                n_ref[d, pl.ds(h, 1), :] = jnp.sum(kw, axis=0, keepdims=True)
                sub = lax.broadcasted_iota(jnp.int32, m_ref.shape, 0)
                lane = lax.broadcasted_iota(jnp.int32, m_ref.shape, 1)
                m_ref[...] = jnp.where((sub == d) & (lane == h), b_end + m_last, m_ref[...])


def _state_call(lay, qkvos, gps, nh, dh):
    nl = len(qkvos)
    nbp, s = lay.nbp, lay.sp
    assert s == MLSTM_L

    def pick(jl, first, last):
        def f(lyr, idx):
            return jnp.where(lyr == jl, idx, jnp.where(lyr < jl, first, last))
        return f

    in_specs, args = [], []
    for jl in range(nl):
        for part in (1, 2):
            pb = pick(jl, 0, nbp - 1)
            ph = pick(jl, part * nh, part * nh + nh - 1)
            in_specs.append(pl.BlockSpec(
                (s, dh), lambda lyr, b, h, pb=pb, ph=ph, part=part: (pb(lyr, b), ph(lyr, part * nh + h))))
            args.append(qkvos[jl])
        pb = pick(jl, 0, nbp - 1)
        in_specs.append(pl.BlockSpec((s, LANES), lambda lyr, b, h, pb=pb: (pb(lyr, b), 0)))
        args.append(gps[jl])
    return pl.pallas_call(
        functools.partial(_state_kernel, nl=nl, nh=nh),
        out_shape=(
            jax.ShapeDtypeStruct((nbp, nl, 2, nh, dh, dh), F32),
            jax.ShapeDtypeStruct((nbp, nl, 2, nh, dh), F32),
            jax.ShapeDtypeStruct((nbp, nl, 2, nh), F32),
        ),
        grid=(nl, nbp, nh),
        in_specs=in_specs,
        out_specs=(
            pl.BlockSpec((None, None, 2, None, dh, dh), lambda lyr, b, h: (b, lyr, 0, h, 0, 0)),
            pl.BlockSpec((None, None, 2, nh, dh), lambda lyr, b, h: (b, lyr, 0, 0, 0)),
            pl.BlockSpec((None, None, 2, nh), lambda lyr, b, h: (b, lyr, 0, 0)),
        ),
        compiler_params=_cparams(("arbitrary", "arbitrary", "arbitrary")),
        name="mlstm_prompt_state",
    )(*args)


def _mm_res_kernel(a_ref, w_ref, x_ref, g_ref, o_ref, *, lay, tm):
    grp = lay.group(pl.program_id(0) * tm)
    acc = jnp.dot(a_ref[...], w_ref[...].astype(BF16), preferred_element_type=F32)
    o_ref[...] = x_ref[...] + g_ref[pl.ds(grp, 1), :] * acc


def _mm_res_call(lay, a, w3, jl, x, mods, l, chunk):
    t, kdim = a.shape
    d = x.shape[1]
    tm = lay.row_tile(1024)
    tn = 512
    nj = d // tn
    return pl.pallas_call(
        functools.partial(_mm_res_kernel, lay=lay, tm=tm),
        out_shape=jax.ShapeDtypeStruct(x.shape, F32),
        grid=(t // tm, nj),
        in_specs=[
            pl.BlockSpec((tm, kdim), lambda i, j: (i, 0)),
            pl.BlockSpec((None, kdim, tn), lambda i, j: (jl, 0, j)),
            pl.BlockSpec((tm, tn), lambda i, j: (i, j)),
            pl.BlockSpec((None, lay.ngp, tn), lambda i, j: (l, 0, chunk * nj + j)),
        ],
        out_specs=pl.BlockSpec((tm, tn), lambda i, j: (i, j)),
        compiler_params=_cparams(("parallel", "parallel")),
        name="mm_residual",
    )(a, w3, x, mods)


def _fnet_kernel(x_ref, nw_ref, sh_ref, sc_ref, gt_ref, wc_ref, ds_ref, wo_ref, o_ref, u_scr, ab_scr,
                 *, lay, row_base, groups, norm):
    s, d = x_ref.shape
    cg = d // groups
    grp = lay.group(row_base + pl.program_id(0) * s)
    _mod_rows(x_ref, nw_ref, sh_ref, sc_ref, grp, u_scr, s)
    wc = wc_ref[...]
    rc = min(s, 256)
    for g in range(groups):
        for r in range(0, s, rc):
            ab = jnp.dot(u_scr[r:r + rc, g * cg:(g + 1) * cg], wc, preferred_element_type=F32)
            ab_scr[r:r + rc, g * cg:(g + 1) * cg] = ab[:, :cg].astype(BF16)
            ab_scr[s + r:s + r + rc, g * cg:(g + 1) * cg] = ab[:, cg:].astype(BF16)
    gate = gt_ref[pl.ds(grp, 1), :]
    wo = wo_ref[...]
    for r in range(0, s, rc):
        y = jnp.dot(ds_ref[r:r + rc, :], ab_scr[...], preferred_element_type=F32) * norm
        o_ref[r:r + rc, :] = x_ref[r:r + rc, :] + gate * jnp.dot(y.astype(BF16), wo, preferred_element_type=F32)


def _dft_mats(s, cg):
    kc = np.arange(cg)
    ang_c = 2.0 * np.pi * np.outer(kc, kc) / cg
    wc = np.concatenate([np.cos(ang_c), np.sin(ang_c)], axis=1)
    ks = np.arange(s)
    ang_s = 2.0 * np.pi * np.outer(ks, ks) / s
    ds = np.concatenate([np.cos(ang_s), -np.sin(ang_s)], axis=1)
    return jnp.asarray(wc, dtype=BF16), jnp.asarray(ds, dtype=BF16)


def _fnet_call(lay, x, nw, mods, l, wo_bf, prompt):
    nb, s = (lay.nbp, lay.sp) if prompt else (lay.nbs, lay.ss)
    rb0 = 0 if prompt else lay.tp // s
    d = lay.d
    cg = d // FNET_GROUPS
    wc, ds = _dft_mats(s, cg)
    kern = functools.partial(_fnet_kernel, lay=lay, row_base=rb0 * s, groups=FNET_GROUPS,
                             norm=1.0 / math.sqrt(s * cg))
    return pl.pallas_call(
        kern,
        out_shape=jax.ShapeDtypeStruct(x.shape, F32),
        grid=(nb,),
        in_specs=[
            pl.BlockSpec((s, d), lambda b: (rb0 + b, 0)),
            pl.BlockSpec((1, d), lambda b: (0, 0)),
            _mod_spec(lay, l, 0, 1),
            _mod_spec(lay, l, 1, 1),
            _mod_spec(lay, l, 2, 1),
            pl.BlockSpec((cg, 2 * cg), lambda b: (0, 0)),
            pl.BlockSpec((s, 2 * s), lambda b: (0, 0)),
            pl.BlockSpec((d, d), lambda b: (0, 0)),
        ],
        out_specs=pl.BlockSpec((s, d), lambda b: (rb0 + b, 0)),
        scratch_shapes=[pltpu.VMEM((s, d), BF16), pltpu.VMEM((2 * s, d), BF16)],
        input_output_aliases={0: 0},
        compiler_params=_cparams(("parallel",)),
        name="fnet_prompt" if prompt else "fnet_latent",
    )(x, nw, mods, mods, mods, wc, ds, wo_bf)


def _glu_kernel(x_ref, nw_ref, sh_ref, sc_ref, wa_ref, wg_ref, ba_ref, bg_ref, o_ref, u_scr, *, lay, tm):
    i = pl.program_id(0)

    @pl.when(pl.program_id(1) == 0)
    def _():
        _mod_rows(x_ref, nw_ref, sh_ref, sc_ref, lay.group(i * tm), u_scr, tm)

    u = u_scr[...]
    a = jnp.dot(u, wa_ref[...], preferred_element_type=F32) + ba_ref[...]
    g = jnp.dot(u, wg_ref[...], preferred_element_type=F32) + bg_ref[...]
    o_ref[...] = a * _sigmoid(g)


def _glu_call(lay, x, nw, mods, l, w_bf, bias):
    t, d = x.shape
    cd = w_bf.shape[1] // 2
    tm = lay.row_tile(1024)
    tn = 512
    nj = cd // tn
    return pl.pallas_call(
        functools.partial(_glu_kernel, lay=lay, tm=tm),
        out_shape=jax.ShapeDtypeStruct((t, cd), F32),
        grid=(t // tm, nj),
        in_specs=[
            pl.BlockSpec((tm, d), lambda i, j: (i, 0)),
            pl.BlockSpec((1, d), lambda i, j: (0, 0)),
            _mod_spec(lay, l, 0, 2),
            _mod_spec(lay, l, 1, 2),
            pl.BlockSpec((d, tn), lambda i, j: (0, j)),
            pl.BlockSpec((d, tn), lambda i, j: (0, nj + j)),
            pl.BlockSpec((1, tn), lambda i, j: (0, j)),
            pl.BlockSpec((1, tn), lambda i, j: (0, nj + j)),
        ],
        out_specs=pl.BlockSpec((tm, tn), lambda i, j: (i, j)),
        scratch_shapes=[pltpu.VMEM((tm, d), BF16)],
        compiler_params=_cparams(("parallel", "arbitrary")),
        name="conv_glu",
    )(x, nw, mods, mods, w_bf, w_bf, bias, bias)


def _conv_kernel(c_ref, p_ref, n_ref, wd_ref, bd_ref, lw_ref, lb_ref, w2_ref, b2_ref, x_ref, gt_ref, o_ref,
                 pad, act, *, lay, rb, width):
    i = pl.program_id(0)
    row0 = i * rb
    grp = lay.group(row0)
    seq = jnp.where(row0 < lay.tp, lay.sp, lay.ss)
    pos = jnp.where(row0 < lay.tp, row0 % lay.sp, (row0 - lay.tp) % lay.ss)
    has_prev = (pos != 0).astype(F32)
    has_next = (pos + rb != seq).astype(F32)
    hl = CONV_HALO
    half = width // 2
    pad[0:hl, :] = p_ref[...] * has_prev
    pad[hl:hl + rb, :] = c_ref[...]
    pad[hl + rb:hl + rb + hl, :] = n_ref[...] * has_next
    sub = 16
    bd = bd_ref[...]
    lw = lw_ref[...]
    lb = lb_ref[...]
    for r in range(0, rb, sub):
        acc = jnp.zeros((sub, c_ref.shape[1]), F32) + bd
        for k in range(width):
            off = hl - half + k + r
            acc = acc + pad[off:off + sub, :] * wd_ref[k:k + 1, :]
        mu = jnp.mean(acc, axis=-1, keepdims=True)
        cen = acc - mu
        var = jnp.mean(cen * cen, axis=-1, keepdims=True)
        y = cen * lax.rsqrt(var + EPS) * lw + lb
        act[r:r + sub, :] = (y * _sigmoid(y)).astype(BF16)
    out = jnp.dot(act[...], w2_ref[...], preferred_element_type=F32) + b2_ref[...]
    o_ref[...] = x_ref[...] + gt_ref[pl.ds(grp, 1), :] * out


def _conv_call(lay, glu, wd, bd, lw, lb, w2_bf, b2, x, mods, l):
    t, cd = glu.shape
    d = x.shape[1]
    rb = lay.row_tile(256)
    hl = CONV_HALO
    assert CONV_WIDTH // 2 <= hl and rb % hl == 0
    nhb = t // hl
    per = rb // hl
    wd_p = jnp.zeros((-(-CONV_WIDTH // SUBLANES) * SUBLANES, cd), F32).at[:CONV_WIDTH].set(wd)
    row = lambda a: a.reshape(1, -1)
    return pl.pallas_call(
        functools.partial(_conv_kernel, lay=lay, rb=rb, width=CONV_WIDTH),
        out_shape=jax.ShapeDtypeStruct(x.shape, F32),
        grid=(t // rb,),
        in_specs=[
            pl.BlockSpec((rb, cd), lambda i: (i, 0)),
            pl.BlockSpec((hl, cd), lambda i: (jnp.maximum(i * per - 1, 0), 0)),
            pl.BlockSpec((hl, cd), lambda i: (jnp.minimum((i + 1) * per, nhb - 1), 0)),
            pl.BlockSpec(wd_p.shape, lambda i: (0, 0)),
            pl.BlockSpec((1, cd), lambda i: (0, 0)),
            pl.BlockSpec((1, cd), lambda i: (0, 0)),
            pl.BlockSpec((1, cd), lambda i: (0, 0)),
            pl.BlockSpec((cd, d), lambda i: (0, 0)),
            pl.BlockSpec((1, d), lambda i: (0, 0)),
            pl.BlockSpec((rb, d), lambda i: (i, 0)),
            _mod_spec(lay, l, 2, 1),
        ],
        out_specs=pl.BlockSpec((rb, d), lambda i: (i, 0)),
        scratch_shapes=[pltpu.VMEM((rb + 2 * hl, cd), F32), pltpu.VMEM((rb, cd), BF16)],
        compiler_params=_cparams(("parallel",)),
        name="conv_dw_ln_pw2",
    )(glu, glu, glu, wd_p, row(bd), row(lw), row(lb), w2_bf, row(b2), x, mods)


def _router_kernel(x_ref, nw_ref, sh_ref, sc_ref, wr_ref, tri_ref, o_ref, cnt_ref, u_scr, *, lay, tm, ng, ne):
    i = pl.program_id(0)

    @pl.when(i == 0)
    def _():
        cnt_ref[...] = jnp.zeros(cnt_ref.shape, F32)

    _mod_rows(x_ref, nw_ref, sh_ref, sc_ref, lay.group(i * tm), u_scr, tm)
    logits = jnp.dot(u_scr[...], wr_ref[...], preferred_element_type=F32)
    lane = lax.broadcasted_iota(jnp.int32, logits.shape, 1)
    big = jnp.int32(4 * LANES)
    neg = -jnp.inf
    epg = ne // ng

    gl = jnp.where(lane < ng, logits, neg)
    gmax = jnp.max(gl, axis=-1, keepdims=True)
    gidx = jnp.min(jnp.where(gl == gmax, lane, big), axis=-1, keepdims=True)
    g_p = 1.0 / jnp.sum(jnp.where(lane < ng, jnp.exp(logits - gmax), 0.0), axis=-1, keepdims=True)

    lo = ng + gidx * epg
    el = jnp.where((lane >= lo) & (lane < lo + epg), logits, neg)
    v1 = jnp.max(el, axis=-1, keepdims=True)
    i1 = jnp.min(jnp.where(el == v1, lane, big), axis=-1, keepdims=True)
    el2 = jnp.where(lane == i1, neg, el)
    v2 = jnp.max(el2, axis=-1, keepdims=True)
    i2 = jnp.min(jnp.where(el2 == v2, lane, big), axis=-1, keepdims=True)
    e1 = i1 - ng
    e2 = i2 - ng
    tt = jnp.exp(v2 - v1)
    p1 = 1.0 / (1.0 + tt)
    gate1 = p1 * g_p
    gate2 = (tt * p1) * g_p

    oh1 = lane == e1
    oh2 = lane == e2
    oh = jnp.where(oh1 | oh2, 1.0, 0.0)
    carry = cnt_ref[0:1, :]
    prefix = jnp.dot(tri_ref[...], oh.astype(BF16), preferred_element_type=F32) + carry
    rank1 = jnp.sum(jnp.where(oh1, prefix, 0.0), axis=-1, keepdims=True)
    rank2 = jnp.sum(jnp.where(oh2, prefix, 0.0), axis=-1, keepdims=True)
    cnt_ref[0:1, :] = carry + jnp.sum(oh, axis=0, keepdims=True)

    out = jnp.where(lane == 0, e1.astype(F32), 0.0)
    out = jnp.where(lane == 1, e2.astype(F32), out)
    out = jnp.where(lane == 2, gate1, out)
    out = jnp.where(lane == 3, gate2, out)
    out = jnp.where(lane == 4, rank1, out)
    out = jnp.where(lane == 5, rank2, out)
    o_ref[...] = out


def _router_call(lay, x, nw, mods, l, wr_bf, tri, ng, ne):
    t, d = x.shape
    tm = tri.shape[0]
    return pl.pallas_call(
        functools.partial(_router_kernel, lay=lay, tm=tm, ng=ng, ne=ne),
        out_shape=(jax.ShapeDtypeStruct((t, LANES), F32), jax.ShapeDtypeStruct((SUBLANES, LANES), F32)),
        grid=(t // tm,),
        in_specs=[
            pl.BlockSpec((tm, d), lambda i: (i, 0)),
            pl.BlockSpec((1, d), lambda i: (0, 0)),
            _mod_spec(lay, l, 3, 1),
            _mod_spec(lay, l, 4, 1),
            pl.BlockSpec((d, LANES), lambda i: (0, 0)),
            pl.BlockSpec((tm, tm), lambda i: (0, 0)),
        ],
        out_specs=(pl.BlockSpec((tm, LANES), lambda i: (i, 0)), pl.BlockSpec((SUBLANES, LANES), lambda i: (0, 0))),
        scratch_shapes=[pltpu.VMEM((tm, d), BF16)],
        compiler_params=_cparams(("arbitrary",)),
        name="moe_router",
    )(x, nw, mods, mods, wr_bf, tri)


def _dispatch_kernel(dest_ref, x_ref, nw_ref, sh_ref, sc_ref, xs_in, xs_out, u_scr, sem, *, lay, tm, topk):
    i = pl.program_id(0)
    _mod_rows(x_ref, nw_ref, sh_ref, sc_ref, lay.group(i * tm), u_scr, tm)

    def row_copy(r, kk):
        dst = dest_ref[0, 0, r * topk + kk]
        return pltpu.make_async_copy(u_scr.at[pl.ds(r, 1), :], xs_out.at[pl.ds(dst, 1), :], sem)

    def start(r, c):
        for kk in range(topk):
            row_copy(r, kk).start()
        return c

    def wait(r, c):
        for kk in range(topk):
            row_copy(r, kk).wait()
        return c

    lax.fori_loop(0, tm, start, 0)
    lax.fori_loop(0, tm, wait, 0)


def _dispatch_call(lay, x, nw, mods, l, dest, xs_zero):
    t, d = x.shape
    tm = lay.row_tile(256)
    topk = dest.shape[1]
    dest3 = dest.reshape(t // tm, 1, tm * topk)
    return pl.pallas_call(
        functools.partial(_dispatch_kernel, lay=lay, tm=tm, topk=topk),
        out_shape=jax.ShapeDtypeStruct(xs_zero.shape, xs_zero.dtype),
        grid=(t // tm,),
        in_specs=[
            pl.BlockSpec((1, 1, tm * topk), lambda i: (i, 0, 0), memory_space=pltpu.SMEM),
            pl.BlockSpec((tm, d), lambda i: (i, 0)),
            pl.BlockSpec((1, d), lambda i: (0, 0)),
            _mod_spec(lay, l, 3, 1),
            _mod_spec(lay, l, 4, 1),
            pl.BlockSpec(memory_space=pl.ANY),
        ],
        out_specs=pl.BlockSpec(memory_space=pl.ANY),
        scratch_shapes=[pltpu.VMEM((tm, d), F32), pltpu.SemaphoreType.DMA(())],
        input_output_aliases={5: 0},
        compiler_params=_cparams(("arbitrary",)),
        name="moe_dispatch",
    )(dest3, x, nw, mods, mods, xs_zero)


def _expert_kernel(be_ref, nu_ref, x_ref, w13_ref, w2_ref, o_ref, *, hid):
    @pl.when(pl.program_id(0) < nu_ref[0])
    def _():
        hb = jnp.dot(x_ref[...].astype(BF16), w13_ref[...].astype(BF16), preferred_element_type=F32)
        a = hb[:, :hid]
        act = (a * _sigmoid(a)) * hb[:, hid:]
        o_ref[...] = jnp.dot(act.astype(BF16), w2_ref[...].astype(BF16), preferred_element_type=F32)

    @pl.when(pl.program_id(0) >= nu_ref[0])
    def _():
        o_ref[...] = jnp.zeros(o_ref.shape, F32)


def _expert_call(xs, block_e, n_used, w13, w2, l):
    r, d = xs.shape
    nb = r // MOE_BM
    hid = w2.shape[2]
    clamp = lambda b, nu: jnp.minimum(b, jnp.maximum(nu[0], 1) - 1)
    grid_spec = pltpu.PrefetchScalarGridSpec(
        num_scalar_prefetch=2,
        grid=(nb,),
        in_specs=[
            pl.BlockSpec((MOE_BM, d), lambda b, be, nu: (clamp(b, nu), 0)),
            pl.BlockSpec((None, None, d, 2 * hid), lambda b, be, nu: (l, be[b], 0, 0)),
            pl.BlockSpec((None, None, hid, d), lambda b, be, nu: (l, be[b], 0, 0)),
        ],
        out_specs=pl.BlockSpec((MOE_BM, d), lambda b, be, nu: (b, 0)),
    )
    return pl.pallas_call(
        functools.partial(_expert_kernel, hid=hid),
        out_shape=jax.ShapeDtypeStruct((r, d), F32),
        grid_spec=grid_spec,
        compiler_params=_cparams(("arbitrary",)),
        name="moe_experts",
    )(block_e, n_used, xs, w13, w2)


def _combine_kernel(dest_ref, x_ref, rt_ref, gt_ref, ys_ref, o_ref, rows, sem, *, lay, tm, topk):
    i = pl.program_id(0)
    grp = lay.group(i * tm)

    def row_copy(r, kk):
        src = dest_ref[0, 0, r * topk + kk]
        return pltpu.make_async_copy(ys_ref.at[pl.ds(src, 1), :], rows.at[kk, pl.ds(r, 1), :], sem)

    def start(r, c):
        for kk in range(topk):
            row_copy(r, kk).start()
        return c

    def wait(r, c):
        for kk in range(topk):
            row_copy(r, kk).wait()
        return c

    lax.fori_loop(0, tm, start, 0)
    lax.fori_loop(0, tm, wait, 0)
    rt = rt_ref[...]
    mix = None
    for kk in range(topk):
        term = rt[:, topk + kk:topk + kk + 1] * rows[kk]
        mix = term if mix is None else mix + term
    o_ref[...] = x_ref[...] + gt_ref[pl.ds(grp, 1), :] * mix


def _combine_call(lay, x, route, mods, l, dest, ys):
    t, d = x.shape
    tm = lay.row_tile(256)
    topk = dest.shape[1]
    dest3 = dest.reshape(t // tm, 1, tm * topk)
    return pl.pallas_call(
        functools.partial(_combine_kernel, lay=lay, tm=tm, topk=topk),
        out_shape=jax.ShapeDtypeStruct(x.shape, F32),
        grid=(t // tm,),
        in_specs=[
            pl.BlockSpec((1, 1, tm * topk), lambda i: (i, 0, 0), memory_space=pltpu.SMEM),
            pl.BlockSpec((tm, d), lambda i: (i, 0)),
            pl.BlockSpec((tm, LANES), lambda i: (i, 0)),
            _mod_spec(lay, l, 5, 1),
            pl.BlockSpec(memory_space=pl.ANY),
        ],
        out_specs=pl.BlockSpec((tm, d), lambda i: (i, 0)),
        scratch_shapes=[pltpu.VMEM((topk, tm, d), F32), pltpu.SemaphoreType.DMA(())],
        compiler_params=_cparams(("arbitrary",)),
        name="moe_combine",
    )(dest3, x, route, mods, ys)


def _moe_layer(lay, x, nw, mods, l, wr_bf, tri_r, w13, w2, ng, ne):
    t, d = x.shape
    route, counts = _router_call(lay, x, nw, mods, l, wr_bf, tri_r, ng, ne)
    e_idx = route[:, 0:MOE_TOP_K].astype(jnp.int32)
    rank = route[:, 2 * MOE_TOP_K:3 * MOE_TOP_K].astype(jnp.int32)
    cnt = counts[0, :ne].astype(jnp.int32)
    padded = (cnt + MOE_BM - 1) // MOE_BM * MOE_BM
    pad_end = jnp.cumsum(padded)
    pad_start = pad_end - padded
    dest = pad_start[e_idx] + rank
    nb = -(-(t * MOE_TOP_K) // MOE_BM) + ne
    block_e = jnp.minimum(jnp.searchsorted(pad_end, jnp.arange(nb, dtype=jnp.int32) * MOE_BM, side="right"),
                          ne - 1).astype(jnp.int32)
    n_used = (pad_end[-1:] // MOE_BM).astype(jnp.int32)
    xs = _dispatch_call(lay, x, nw, mods, l, dest, jnp.zeros((nb * MOE_BM, d), F32))
    ys = _expert_call(xs, block_e, n_used, w13, w2, l)
    return _combine_call(lay, x, route, mods, l, dest, ys)


def _final_kernel(x_ref, w_ref, o_ref):
    o_ref[...] = _rms(x_ref[...], w_ref[...])


def _final_call(x, w, row0, nrows):
    d = x.shape[1]
    tm = math.gcd(512, math.gcd(row0, nrows) if row0 else nrows)
    rb0 = row0 // tm
    return pl.pallas_call(
        _final_kernel,
        out_shape=jax.ShapeDtypeStruct((nrows, d), F32),
        grid=(nrows // tm,),
        in_specs=[pl.BlockSpec((tm, d), lambda i: (rb0 + i, 0)), pl.BlockSpec((1, d), lambda i: (0, 0))],
        out_specs=pl.BlockSpec((tm, d), lambda i: (i, 0)),
        compiler_params=_cparams(("parallel",)),
        name="final_norm",
    )(x, w)


def _lower_tri(n, strict):
    r = np.arange(n)
    m = (r[None, :] < r[:, None]) if strict else (r[None, :] <= r[:, None])
    return jnp.asarray(m.astype(np.float32), dtype=BF16)


def kernel(x_prompt, x_sample, state_C, state_n, state_m, c, c_ctx, ada_w, ada_b, norm1_w, norm2_w, m_w_in, m_b_gate, m_head_norm_w, m_w_out, f_w_out, cv_w_pw1, cv_b_pw1, cv_w_dw, cv_b_dw, cv_ln_w, cv_ln_b, cv_w_pw2, cv_b_pw2, r_w_group, r_w_expert, e_w13, e_w2, final_norm_w):
    nbp, sp, d = x_prompt.shape
    nbs, ss, _ = x_sample.shape
    assert ss % GRID_W == 0
    lay = _Layout(nbp, sp, nbs, ss, d)
    depth = ada_w.shape[0]
    nh, dh = state_C.shape[3], state_C.shape[4]
    di = nh * dh
    ng = r_w_group.shape[2]
    ne = r_w_expert.shape[2]
    assert ng == MOE_GROUPS and ng + ne <= LANES and 4 * nh <= LANES

    x = jnp.concatenate([x_prompt.reshape(lay.tp, d), x_sample.reshape(lay.ts, d)], axis=0)
    cv = jnp.zeros((lay.ngp, d), F32).at[0].set(c_ctx).at[1:1 + nbs].set(c)
    mods = _ada_call(cv, ada_w, ada_b)

    tri_l = _lower_tri(MLSTM_L, strict=False)
    tri_r = _lower_tri(512, strict=True)
    row = lambda a: a.reshape(1, -1)

    qkvos, gps = [], []
    for l in range(depth):
        j, kind = l // N_MIXERS, l % N_MIXERS
        nw1 = row(norm1_w[l])
        if kind == 0:
            wg = jnp.zeros((d, LANES), F32).at[:, :4 * nh].set(m_w_in[j][:, 4 * di:]).astype(BF16)
            bg = jnp.zeros((1, LANES), F32).at[0, :4 * nh].set(m_b_gate[j])
            qkvo, gates = _proj_call(lay, x, nw1, mods, l, m_w_in, j, wg, 4 * di)
            gp = _gate_prep_call(gates, bg, tri_l, nh)
            gpt = gp[:, :4 * nh].T
            hw = row(m_head_norm_w[j])
            hg = jnp.zeros((lay.t, di), BF16)
            hg = _mlstm_call(lay, hg, qkvo, gp, gpt, hw, nh, dh, prompt=True)
            n0 = state_n[:, j].reshape(nbs, 2 * nh, dh)
            m0 = jnp.broadcast_to(state_m[:, j].reshape(nbs, 2 * nh, 1), (nbs, 2 * nh, LANES))
            hg = _mlstm_call(lay, hg, qkvo, gp, gpt, hw, nh, dh, prompt=False, state=((state_C, j), n0, m0))
            x = _mm_res_call(lay, hg, m_w_out, j, x, mods, l, 2)
            qkvos.append(qkvo)
            gps.append(gp)
        elif kind == 1:
            wo = f_w_out[j].astype(BF16)
            x = _fnet_call(lay, x, nw1, mods, l, wo, prompt=True)
            x = _fnet_call(lay, x, nw1, mods, l, wo, prompt=False)
        else:
            glu = _glu_call(lay, x, nw1, mods, l, cv_w_pw1[j].astype(BF16), row(cv_b_pw1[j]))
            x = _conv_call(lay, glu, cv_w_dw[j], cv_b_dw[j], cv_ln_w[j], cv_ln_b[j], cv_w_pw2[j].astype(BF16),
                           cv_b_pw2[j], x, mods, l)
        wr = jnp.zeros((d, LANES), F32).at[:, :ng].set(r_w_group[l]).at[:, ng:ng + ne].set(r_w_expert[l])
        x = _moe_layer(lay, x, row(norm2_w[l]), mods, l, wr.astype(BF16), tri_r, e_w13, e_w2, ng, ne)

    fw = row(final_norm_w)
    y_prompt = _final_call(x, fw, 0, lay.tp).reshape(nbp, sp, d)
    y_sample = _final_call(x, fw, lay.tp, lay.ts).reshape(nbs, ss, d)
    new_c, new_n, new_m = _state_call(lay, qkvos, gps, nh, dh)
    return (y_prompt, y_sample, new_c, new_n, new_m)
```

```python
import functools
import math

import numpy as np
import jax
import jax.numpy as jnp
from jax import lax
from jax.experimental import pallas as pl
from jax.experimental.pallas import tpu as pltpu

F32 = jnp.float32
BF16 = jnp.bfloat16
EPS = 1e-6
GRID_W = 64
N_MIXERS = 3
FNET_GROUPS = 4
CONV_WIDTH = 31
MOE_GROUPS = 4
MOE_TOP_K = 2

LANES = 128
SUBLANES = 8
MLSTM_L = 256
MOE_BM = 256
CONV_HALO = 16
VMEM_LIMIT = 56 * 1024 * 1024


def _cparams(sem, vmem=VMEM_LIMIT):
    return pltpu.CompilerParams(dimension_semantics=sem, vmem_limit_bytes=vmem)


def _bdot(a, b):
    return jnp.dot(a.astype(BF16), b.astype(BF16), preferred_element_type=F32)


def _rms(x, w):
    return x * lax.rsqrt(jnp.mean(x * x, axis=-1, keepdims=True) + EPS) * w


def _modulate(x, w, shift, scale):
    return _rms(x, w) * (1.0 + scale) + shift


def _sigmoid(x):
    return 1.0 / (1.0 + jnp.exp(-x))


def _log_sigmoid(x):
    return jnp.minimum(x, 0.0) - jnp.log(1.0 + jnp.exp(-jnp.abs(x)))


def _split3(x):
    hi = x.astype(BF16)
    r1 = x - hi.astype(F32)
    mid = r1.astype(BF16)
    lo = (r1 - mid.astype(F32)).astype(BF16)
    return hi, mid, lo


class _Layout:
    def __init__(self, nbp, sp, nbs, ss, d):
        self.nbp, self.sp, self.nbs, self.ss, self.d = nbp, sp, nbs, ss, d
        self.tp, self.ts = nbp * sp, nbs * ss
        self.t = self.tp + self.ts
        assert self.tp % ss == 0, "latent sequences must start on a block boundary of their own length"
        self.ngp = -(-(1 + nbs) // SUBLANES) * SUBLANES

    def group(self, row0):
        return jnp.where(row0 < self.tp, 0, 1 + (row0 - self.tp) // self.ss)

    def row_tile(self, want):
        tm = math.gcd(math.gcd(self.tp, self.ss), want)
        assert tm % SUBLANES == 0
        return tm


def _ada_kernel(cv_ref, w_ref, b_ref, o_ref):
    s = cv_ref[...]
    s = s * _sigmoid(s)
    o_ref[...] = _bdot(s, w_ref[...]) + b_ref[...]


def _ada_call(cv, ada_w, ada_b):
    depth, d, n = ada_w.shape
    ngp = cv.shape[0]
    tn = min(n, 2048)
    return pl.pallas_call(
        _ada_kernel,
        out_shape=jax.ShapeDtypeStruct((depth, ngp, n), F32),
        grid=(depth, n // tn),
        in_specs=[
            pl.BlockSpec((ngp, d), lambda l, j: (0, 0)),
            pl.BlockSpec((None, d, tn), lambda l, j: (l, 0, j)),
            pl.BlockSpec((None, 1, tn), lambda l, j: (l, 0, j)),
        ],
        out_specs=pl.BlockSpec((None, ngp, tn), lambda l, j: (l, 0, j)),
        compiler_params=_cparams(("parallel", "parallel")),
        name="ada_mods",
    )(cv, ada_w, ada_b.reshape(depth, 1, n))


def _mod_spec(lay, l, chunk, ngrid):
    if ngrid == 1:
        return pl.BlockSpec((None, lay.ngp, lay.d), lambda i: (l, 0, chunk))
    if ngrid == 2:
        return pl.BlockSpec((None, lay.ngp, lay.d), lambda i, j: (l, 0, chunk))
    return pl.BlockSpec((None, lay.ngp, lay.d), lambda i, j, k: (l, 0, chunk))


def _mod_rows(x_ref, w_ref, sh_ref, sc_ref, grp, dst_ref, rows, chunk=256):
    w = w_ref[...]
    sh = sh_ref[pl.ds(grp, 1), :]
    sc = sc_ref[pl.ds(grp, 1), :]
    chunk = min(chunk, rows)
    for r in range(0, rows, chunk):
        dst_ref[r:r + chunk, :] = _modulate(x_ref[r:r + chunk, :], w, sh, sc).astype(dst_ref.dtype)


def _proj_kernel(x_ref, nw_ref, sh_ref, sc_ref, w_ref, wg_ref, o_ref, og_ref, u_scr, *, lay, tm):
    i = pl.program_id(0)
    j = pl.program_id(1)

    @pl.when(j == 0)
    def _():
        _mod_rows(x_ref, nw_ref, sh_ref, sc_ref, lay.group(i * tm), u_scr, tm)
        og_ref[...] = jnp.dot(u_scr[...], wg_ref[...], preferred_element_type=F32)

    o_ref[...] = jnp.dot(u_scr[...], w_ref[...].astype(BF16), preferred_element_type=F32).astype(BF16)


def _proj_call(lay, x, nw, mods, l, w_in, jl, wg, nmain):
    t, d = x.shape
    tm = lay.row_tile(1024)
    tn = 1024
    kern = functools.partial(_proj_kernel, lay=lay, tm=tm)
    return pl.pallas_call(
        kern,
        out_shape=(jax.ShapeDtypeStruct((t, nmain), BF16), jax.ShapeDtypeStruct((t, LANES), F32)),
        grid=(t // tm, nmain // tn),
        in_specs=[
            pl.BlockSpec((tm, d), lambda i, j: (i, 0)),
            pl.BlockSpec((1, d), lambda i, j: (0, 0)),
            _mod_spec(lay, l, 0, 2),
            _mod_spec(lay, l, 1, 2),
            pl.BlockSpec((None, d, tn), lambda i, j: (jl, 0, j)),
            pl.BlockSpec((d, LANES), lambda i, j: (0, 0)),
        ],
        out_specs=(pl.BlockSpec((tm, tn), lambda i, j: (i, j)), pl.BlockSpec((tm, LANES), lambda i, j: (i, 0))),
        scratch_shapes=[pltpu.VMEM((tm, d), BF16)],
        compiler_params=_cparams(("parallel", "arbitrary")),
        name="mlstm_proj",
    )(x, nw, mods, mods, w_in, wg)


def _gate_prep_kernel(g_ref, b_ref, tri_ref, o_ref, *, nh):
    g = g_ref[...] + b_ref[...]
    lane = lax.broadcasted_iota(jnp.int32, g.shape, 1)
    is_f = ((lane >= nh) & (lane < 2 * nh)) | ((lane >= 3 * nh) & (lane < 4 * nh))
    lf = jnp.where(is_f, _log_sigmoid(g), 0.0)
    tri = tri_ref[...]
    hi, mid, lo = _split3(lf)
    prefix = (jnp.dot(tri, hi, preferred_element_type=F32) + jnp.dot(tri, mid, preferred_element_type=F32)
              + jnp.dot(tri, lo, preferred_element_type=F32))
    total = jnp.sum(lf, axis=0, keepdims=True)
    suffix = total - prefix + lf
    b = jnp.where(lane < 2 * nh, prefix, suffix)
    a = g - pltpu.roll(b, LANES - nh, 1)
    is_a = (lane < nh) | ((lane >= 2 * nh) & (lane < 3 * nh))
    o_ref[...] = jnp.where(is_a, a, b)


def _gate_prep_call(gates, bias, tri, nh):
    t = gates.shape[0]
    l = tri.shape[0]
    return pl.pallas_call(
        functools.partial(_gate_prep_kernel, nh=nh),
        out_shape=jax.ShapeDtypeStruct((t, LANES), F32),
        grid=(t // l,),
        in_specs=[
            pl.BlockSpec((l, LANES), lambda i: (i, 0)),
            pl.BlockSpec((1, LANES), lambda i: (0, 0)),
            pl.BlockSpec((l, l), lambda i: (0, 0)),
        ],
        out_specs=pl.BlockSpec((l, LANES), lambda i: (i, 0)),
        compiler_params=_cparams(("parallel",)),
        name="mlstm_gate_prep",
    )(gates, bias, tri)


def _col(tile, c):
    lane = lax.broadcasted_iota(jnp.int32, tile.shape, 1)
    return jnp.sum(jnp.where(lane == c, tile, 0.0), axis=-1, keepdims=True)


def _dir_masks(l):
    r = lax.broadcasted_iota(jnp.int32, (l, l), 0)
    c = lax.broadcasted_iota(jnp.int32, (l, l), 1)
    return c <= r, c >= r


def _head_epilogue(h, hw, o):
    hn = h * lax.rsqrt(jnp.mean(h * h, axis=-1, keepdims=True) + EPS) * hw
    return (hn * _sigmoid(o.astype(F32))).astype(BF16)


def _mlstm_single_kernel(q_ref, k_ref, v_ref, o_ref, gp_ref, gpt_ref, hw_ref, out_ref, *, nh, scale):
    h = pl.program_id(1)
    l = q_ref.shape[0]
    gp = gp_ref[...]
    qk = lax.dot_general(q_ref[...], k_ref[...], (((1,), (1,)), ((), ())), preferred_element_type=F32)
    masks = _dir_masks(l)
    p = None
    for d in range(2):
        a_r = gpt_ref[pl.ds(2 * nh * d + h, 1), :]
        b_c = _col(gp, 2 * nh * d + nh + h)
        g = jnp.where(masks[d], a_r, -jnp.inf)
        m = jnp.maximum(jnp.max(g, axis=-1, keepdims=True), 0.0)
        s = qk * jnp.exp(g - m) * scale
        den = jnp.sum(s, axis=-1, keepdims=True)
        inv = 1.0 / jnp.maximum(jnp.abs(den), jnp.exp(-(b_c + m)))
        p = s * inv if p is None else p + s * inv
    hh = jnp.dot(p.astype(BF16), v_ref[...], preferred_element_type=F32)
    out_ref[...] = _head_epilogue(hh, hw_ref[...], o_ref[...])


def _mlstm_multi_kernel(q_ref, k_ref, v_ref, o_ref, gp_ref, gpt_ref, hw_ref, c0_ref, n0_ref, m0_ref,
                        out_ref, hacc, cst, *, nh, nc, l, scale):
    h = pl.program_id(1)
    masks = _dir_masks(l)
    for d in range(2):
        cst[...] = c0_ref[d]
        n = n0_ref[pl.ds(d * nh + h, 1), :]
        m = m0_ref[pl.ds(d * nh + h, 1), 0:1]
        order = list(range(nc)) if d == 0 else list(range(nc - 1, -1, -1))
        for step, c in enumerate(order):
            r0 = c * l
            gp = gp_ref[r0:r0 + l, :]
            a_c = _col(gp, 2 * nh * d + h)
            b_c = _col(gp, 2 * nh * d + nh + h)
            a_r = gpt_ref[pl.ds(2 * nh * d + h, 1), r0:r0 + l]
            q = q_ref[r0:r0 + l, :]
            k = k_ref[r0:r0 + l, :]
            v = v_ref[r0:r0 + l, :]
            qk = lax.dot_general(q, k, (((1,), (1,)), ((), ())), preferred_element_type=F32)
            g = jnp.where(masks[d], a_r, -jnp.inf)
            mt = jnp.maximum(jnp.max(g, axis=-1, keepdims=True), m)
            s = qk * jnp.exp(g - mt) * scale
            w_prev = jnp.exp(m - mt) * scale
            qn = jnp.sum(q.astype(F32) * n, axis=-1, keepdims=True)
            den = jnp.sum(s, axis=-1, keepdims=True) + w_prev * qn
            inv = 1.0 / jnp.maximum(jnp.abs(den), jnp.exp(-(b_c + mt)))
            hc = jnp.dot((s * inv).astype(BF16), v, preferred_element_type=F32)
            hc = hc + (w_prev * inv) * jnp.dot(q, cst[...].astype(BF16), preferred_element_type=F32)
            if d == 0:
                hacc[r0:r0 + l, :] = hc
            else:
                hacc[r0:r0 + l, :] += hc
            if step + 1 < nc:
                m_last = jnp.max(mt, axis=0, keepdims=True)
                b_end = b_c[l - 1:l, :] if d == 0 else b_c[0:1, :]
                decay = jnp.exp(m - m_last)
                kw = k.astype(F32) * jnp.exp(a_c - m_last)
                cst[...] = decay * cst[...] + lax.dot_general(
                    kw.astype(BF16), v, (((0,), (0,)), ((), ())), preferred_element_type=F32)
                n = decay * n + jnp.sum(kw, axis=0, keepdims=True)
                m = b_end + m_last
    hw = hw_ref[...]
    for c in range(nc):
        r0 = c * l
        out_ref[r0:r0 + l, :] = _head_epilogue(hacc[r0:r0 + l, :], hw, o_ref[r0:r0 + l, :])


def _mlstm_call(lay, hg, qkvo, gp, gpt, hw, nh, dh, prompt, state=None):
    nb, s = (lay.nbp, lay.sp) if prompt else (lay.nbs, lay.ss)
    rb0 = 0 if prompt else lay.tp // s
    scale = dh ** -0.5
    common_in = [
        pl.BlockSpec((s, dh), lambda b, h: (rb0 + b, h)),
        pl.BlockSpec((s, dh), lambda b, h: (rb0 + b, nh + h)),
        pl.BlockSpec((s, dh), lambda b, h: (rb0 + b, 2 * nh + h)),
        pl.BlockSpec((s, dh), lambda b, h: (rb0 + b, 3 * nh + h)),
        pl.BlockSpec((s, LANES), lambda b, h: (rb0 + b, 0)),
        pl.BlockSpec((4 * nh, s), lambda b, h: (0, rb0 + b)),
        pl.BlockSpec((1, dh), lambda b, h: (0, h)),
    ]
    hg_spec = pl.BlockSpec((s, dh), lambda b, h: (rb0 + b, h))
    any_spec = pl.BlockSpec(memory_space=pl.ANY)
    if prompt:
        assert s == MLSTM_L
        kern = functools.partial(_mlstm_single_kernel, nh=nh, scale=scale)

        def body(hg_any, *refs):
            kern(*refs)

        return pl.pallas_call(
            body,
            out_shape=jax.ShapeDtypeStruct(hg.shape, hg.dtype),
            grid=(nb, nh),
            in_specs=[any_spec] + common_in,
            out_specs=hg_spec,
            input_output_aliases={0: 0},
            compiler_params=_cparams(("parallel", "parallel")),
            name="mlstm_prompt",
        )(hg, qkvo, qkvo, qkvo, qkvo, gp, gpt, hw)
    c0, n0, m0 = state
    nc = s // MLSTM_L
    kern = functools.partial(_mlstm_multi_kernel, nh=nh, nc=nc, l=MLSTM_L, scale=scale)

    def body(hg_any, *refs):
        kern(*refs)

    jl = c0[1]
    return pl.pallas_call(
        body,
        out_shape=jax.ShapeDtypeStruct(hg.shape, hg.dtype),
        grid=(nb, nh),
        in_specs=[any_spec] + common_in + [
            pl.BlockSpec((None, None, 2, None, dh, dh), lambda b, h: (b, jl, 0, h, 0, 0)),
            pl.BlockSpec((None, 2 * nh, dh), lambda b, h: (b, 0, 0)),
            pl.BlockSpec((None, 2 * nh, LANES), lambda b, h: (b, 0, 0)),
        ],
        out_specs=hg_spec,
        scratch_shapes=[pltpu.VMEM((s, dh), F32), pltpu.VMEM((dh, dh), F32)],
        input_output_aliases={0: 0},
        compiler_params=_cparams(("parallel", "parallel")),
        name="mlstm_latent",
    )(hg, qkvo, qkvo, qkvo, qkvo, gp, gpt, hw, c0[0], n0, m0)


def _state_kernel(*refs, nl, nh):
    ins, (c_ref, n_ref, m_ref) = refs[:3 * nl], refs[3 * nl:]
    lyr = pl.program_id(0)
    h = pl.program_id(2)

    @pl.when(h == 0)
    def _():
        m_ref[...] = jnp.zeros(m_ref.shape, F32)

    for jl in range(nl):
        k_ref, v_ref, gp_ref = ins[3 * jl:3 * jl + 3]

        @pl.when(lyr == jl)
        def _(k_ref=k_ref, v_ref=v_ref, gp_ref=gp_ref):
            l = k_ref.shape[0]
            gp = gp_ref[...]
            k = k_ref[...].astype(F32)
            v = v_ref[...]
            for d in range(2):
                a_c = _col(gp, 2 * nh * d + h)
                b_c = _col(gp, 2 * nh * d + nh + h)
                m_last = jnp.maximum(jnp.max(a_c, axis=0, keepdims=True), 0.0)
                b_end = b_c[l - 1:l, :] if d == 0 else b_c[0:1, :]
                kw = k * jnp.exp(a_c - m_last)
                c_ref[d] = lax.dot_general(kw.astype(BF16), v, (((0,), (0,)), ((), ())), preferred_element_type=F32)
                n_ref[d, pl.ds(h, 1), :] = jnp.sum(kw, axis=0, keepdims=True)
                sub = lax.broadcasted_iota(jnp.int32, m_ref.shape, 0)
                lane = lax.broadcasted_iota(jnp.int32, m_ref.shape, 1)
                m_ref[...] = jnp.where((sub == d) & (lane == h), b_end + m_last, m_ref[...])


def _state_call(lay, qkvos, gps, nh, dh):
    nl = len(qkvos)
    nbp, s = lay.nbp, lay.sp
    assert s == MLSTM_L

    def pick(jl, first, last):
        def f(lyr, idx):
            return jnp.where(lyr == jl, idx, jnp.where(lyr < jl, first, last))
        return f

    in_specs, args = [], []
    for jl in range(nl):
        for part in (1, 2):
            pb = pick(jl, 0, nbp - 1)
            ph = pick(jl, part * nh, part * nh + nh - 1)
            in_specs.append(pl.BlockSpec(
                (s, dh), lambda lyr, b, h, pb=pb, ph=ph, part=part: (pb(lyr, b), ph(lyr, part * nh + h))))
            args.append(qkvos[jl])
        pb = pick(jl, 0, nbp - 1)
        in_specs.append(pl.BlockSpec((s, LANES), lambda lyr, b, h, pb=pb: (pb(lyr, b), 0)))
        args.append(gps[jl])
    return pl.pallas_call(
        functools.partial(_state_kernel, nl=nl, nh=nh),
        out_shape=(
            jax.ShapeDtypeStruct((nbp, nl, 2, nh, dh, dh), F32),
            jax.ShapeDtypeStruct((nbp, nl, 2, nh, dh), F32),
            jax.ShapeDtypeStruct((nbp, nl, 2, nh), F32),
        ),
        grid=(nl, nbp, nh),
        in_specs=in_specs,
        out_specs=(
            pl.BlockSpec((None, None, 2, None, dh, dh), lambda lyr, b, h: (b, lyr, 0, h, 0, 0)),
            pl.BlockSpec((None, None, 2, nh, dh), lambda lyr, b, h: (b, lyr, 0, 0, 0)),
            pl.BlockSpec((None, None, 2, nh), lambda lyr, b, h: (b, lyr, 0, 0)),
        ),
        compiler_params=_cparams(("arbitrary", "arbitrary", "arbitrary")),
        name="mlstm_prompt_state",
    )(*args)


def _mm_res_kernel(a_ref, w_ref, x_ref, g_ref, o_ref, *, lay, tm):
    grp = lay.group(pl.program_id(0) * tm)
    acc = jnp.dot(a_ref[...], w_ref[...].astype(BF16), preferred_element_type=F32)
    o_ref[...] = x_ref[...] + g_ref[pl.ds(grp, 1), :] * acc


def _mm_res_call(lay, a, w3, jl, x, mods, l, chunk):
    t, kdim = a.shape
    d = x.shape[1]
    tm = lay.row_tile(1024)
    tn = 512
    nj = d // tn
    return pl.pallas_call(
        functools.partial(_mm_res_kernel, lay=lay, tm=tm),
        out_shape=jax.ShapeDtypeStruct(x.shape, F32),
        grid=(t // tm, nj),
        in_specs=[
            pl.BlockSpec((tm, kdim), lambda i, j: (i, 0)),
            pl.BlockSpec((None, kdim, tn), lambda i, j: (jl, 0, j)),
            pl.BlockSpec((tm, tn), lambda i, j: (i, j)),
            pl.BlockSpec((None, lay.ngp, tn), lambda i, j: (l, 0, chunk * nj + j)),
        ],
        out_specs=pl.BlockSpec((tm, tn), lambda i, j: (i, j)),
        compiler_params=_cparams(("parallel", "parallel")),
        name="mm_residual",
    )(a, w3, x, mods)


def _fnet_kernel(x_ref, nw_ref, sh_ref, sc_ref, gt_ref, wc_ref, ds_ref, wo_ref, o_ref, u_scr, ab_scr,
                 *, lay, row_base, groups, norm):
    s, d = x_ref.shape
    cg = d // groups
    grp = lay.group(row_base + pl.program_id(0) * s)
    _mod_rows(x_ref, nw_ref, sh_ref, sc_ref, grp, u_scr, s)
    wc = wc_ref[...]
    rc = min(s, 256)
    for g in range(groups):
        for r in range(0, s, rc):
            ab = jnp.dot(u_scr[r:r + rc, g * cg:(g + 1) * cg], wc, preferred_element_type=F32)
            ab_scr[r:r + rc, g * cg:(g + 1) * cg] = ab[:, :cg].astype(BF16)
            ab_scr[s + r:s + r + rc, g * cg:(g + 1) * cg] = ab[:, cg:].astype(BF16)
    gate = gt_ref[pl.ds(grp, 1), :]
    wo = wo_ref[...]
    for r in range(0, s, rc):
        y = jnp.dot(ds_ref[r:r + rc, :], ab_scr[...], preferred_element_type=F32) * norm
        o_ref[r:r + rc, :] = x_ref[r:r + rc, :] + gate * jnp.dot(y.astype(BF16), wo, preferred_element_type=F32)


def _dft_mats(s, cg):
    kc = np.arange(cg)
    ang_c = 2.0 * np.pi * np.outer(kc, kc) / cg
    wc = np.concatenate([np.cos(ang_c), np.sin(ang_c)], axis=1)
    ks = np.arange(s)
    ang_s = 2.0 * np.pi * np.outer(ks, ks) / s
    ds = np.concatenate([np.cos(ang_s), -np.sin(ang_s)], axis=1)
    return jnp.asarray(wc, dtype=BF16), jnp.asarray(ds, dtype=BF16)


def _fnet_call(lay, x, nw, mods, l, wo_bf, prompt):
    nb, s = (lay.nbp, lay.sp) if prompt else (lay.nbs, lay.ss)
    rb0 = 0 if prompt else lay.tp // s
    d = lay.d
    cg = d // FNET_GROUPS
    wc, ds = _dft_mats(s, cg)
    kern = functools.partial(_fnet_kernel, lay=lay, row_base=rb0 * s, groups=FNET_GROUPS,
                             norm=1.0 / math.sqrt(s * cg))
    return pl.pallas_call(
        kern,
        out_shape=jax.ShapeDtypeStruct(x.shape, F32),
        grid=(nb,),
        in_specs=[
            pl.BlockSpec((s, d), lambda b: (rb0 + b, 0)),
            pl.BlockSpec((1, d), lambda b: (0, 0)),
            _mod_spec(lay, l, 0, 1),
            _mod_spec(lay, l, 1, 1),
            _mod_spec(lay, l, 2, 1),
            pl.BlockSpec((cg, 2 * cg), lambda b: (0, 0)),
            pl.BlockSpec((s, 2 * s), lambda b: (0, 0)),
            pl.BlockSpec((d, d), lambda b: (0, 0)),
        ],
        out_specs=pl.BlockSpec((s, d), lambda b: (rb0 + b, 0)),
        scratch_shapes=[pltpu.VMEM((s, d), BF16), pltpu.VMEM((2 * s, d), BF16)],
        input_output_aliases={0: 0},
        compiler_params=_cparams(("parallel",)),
        name="fnet_prompt" if prompt else "fnet_latent",
    )(x, nw, mods, mods, mods, wc, ds, wo_bf)


def _glu_kernel(x_ref, nw_ref, sh_ref, sc_ref, wa_ref, wg_ref, ba_ref, bg_ref, o_ref, u_scr, *, lay, tm):
    i = pl.program_id(0)

    @pl.when(pl.program_id(1) == 0)
    def _():
        _mod_rows(x_ref, nw_ref, sh_ref, sc_ref, lay.group(i * tm), u_scr, tm)

    u = u_scr[...]
    a = jnp.dot(u, wa_ref[...], preferred_element_type=F32) + ba_ref[...]
    g = jnp.dot(u, wg_ref[...], preferred_element_type=F32) + bg_ref[...]
    o_ref[...] = a * _sigmoid(g)


def _glu_call(lay, x, nw, mods, l, w_bf, bias):
    t, d = x.shape
    cd = w_bf.shape[1] // 2
    tm = lay.row_tile(1024)
    tn = 512
    nj = cd // tn
    return pl.pallas_call(
        functools.partial(_glu_kernel, lay=lay, tm=tm),
        out_shape=jax.ShapeDtypeStruct((t, cd), F32),
        grid=(t // tm, nj),
        in_specs=[
            pl.BlockSpec((tm, d), lambda i, j: (i, 0)),
            pl.BlockSpec((1, d), lambda i, j: (0, 0)),
            _mod_spec(lay, l, 0, 2),
            _mod_spec(lay, l, 1, 2),
            pl.BlockSpec((d, tn), lambda i, j: (0, j)),
            pl.BlockSpec((d, tn), lambda i, j: (0, nj + j)),
            pl.BlockSpec((1, tn), lambda i, j: (0, j)),
            pl.BlockSpec((1, tn), lambda i, j: (0, nj + j)),
        ],
        out_specs=pl.BlockSpec((tm, tn), lambda i, j: (i, j)),
        scratch_shapes=[pltpu.VMEM((tm, d), BF16)],
        compiler_params=_cparams(("parallel", "arbitrary")),
        name="conv_glu",
    )(x, nw, mods, mods, w_bf, w_bf, bias, bias)


def _conv_kernel(c_ref, p_ref, n_ref, wd_ref, bd_ref, lw_ref, lb_ref, w2_ref, b2_ref, x_ref, gt_ref, o_ref,
                 pad, act, *, lay, rb, width):
    i = pl.program_id(0)
    row0 = i * rb
    grp = lay.group(row0)
    seq = jnp.where(row0 < lay.tp, lay.sp, lay.ss)
    pos = jnp.where(row0 < lay.tp, row0 % lay.sp, (row0 - lay.tp) % lay.ss)
    has_prev = (pos != 0).astype(F32)
    has_next = (pos + rb != seq).astype(F32)
    hl = CONV_HALO
    half = width // 2
    pad[0:hl, :] = p_ref[...] * has_prev
    pad[hl:hl + rb, :] = c_ref[...]
    pad[hl + rb:hl + rb + hl, :] = n_ref[...] * has_next
    sub = 16
    bd = bd_ref[...]
    lw = lw_ref[...]
    lb = lb_ref[...]
    for r in range(0, rb, sub):
        acc = jnp.zeros((sub, c_ref.shape[1]), F32) + bd
        for k in range(width):
            off = hl - half + k + r
            acc = acc + pad[off:off + sub, :] * wd_ref[k:k + 1, :]
        mu = jnp.mean(acc, axis=-1, keepdims=True)
        cen = acc - mu
        var = jnp.mean(cen * cen, axis=-1, keepdims=True)
        y = cen * lax.rsqrt(var + EPS) * lw + lb
        act[r:r + sub, :] = (y * _sigmoid(y)).astype(BF16)
    out = jnp.dot(act[...], w2_ref[...], preferred_element_type=F32) + b2_ref[...]
    o_ref[...] = x_ref[...] + gt_ref[pl.ds(grp, 1), :] * out


def _conv_call(lay, glu, wd, bd, lw, lb, w2_bf, b2, x, mods, l):
    t, cd = glu.shape
    d = x.shape[1]
    rb = lay.row_tile(256)
    hl = CONV_HALO
    assert CONV_WIDTH // 2 <= hl and rb % hl == 0
    nhb = t // hl
    per = rb // hl
    wd_p = jnp.zeros((-(-CONV_WIDTH // SUBLANES) * SUBLANES, cd), F32).at[:CONV_WIDTH].set(wd)
    row = lambda a: a.reshape(1, -1)
    return pl.pallas_call(
        functools.partial(_conv_kernel, lay=lay, rb=rb, width=CONV_WIDTH),
        out_shape=jax.ShapeDtypeStruct(x.shape, F32),
        grid=(t // rb,),
        in_specs=[
            pl.BlockSpec((rb, cd), lambda i: (i, 0)),
            pl.BlockSpec((hl, cd), lambda i: (jnp.maximum(i * per - 1, 0), 0)),
            pl.BlockSpec((hl, cd), lambda i: (jnp.minimum((i + 1) * per, nhb - 1), 0)),
            pl.BlockSpec(wd_p.shape, lambda i: (0, 0)),
            pl.BlockSpec((1, cd), lambda i: (0, 0)),
            pl.BlockSpec((1, cd), lambda i: (0, 0)),
            pl.BlockSpec((1, cd), lambda i: (0, 0)),
            pl.BlockSpec((cd, d), lambda i: (0, 0)),
            pl.BlockSpec((1, d), lambda i: (0, 0)),
            pl.BlockSpec((rb, d), lambda i: (i, 0)),
            _mod_spec(lay, l, 2, 1),
        ],
        out_specs=pl.BlockSpec((rb, d), lambda i: (i, 0)),
        scratch_shapes=[pltpu.VMEM((rb + 2 * hl, cd), F32), pltpu.VMEM((rb, cd), BF16)],
        compiler_params=_cparams(("parallel",)),
        name="conv_dw_ln_pw2",
    )(glu, glu, glu, wd_p, row(bd), row(lw), row(lb), w2_bf, row(b2), x, mods)


def _router_kernel(x_ref, nw_ref, sh_ref, sc_ref, wr_ref, tri_ref, upper_ref, o_ref, cnt_ref, u_scr,
                   *, lay, tm, ng, ne):
    i = pl.program_id(0)
    _mod_rows(x_ref, nw_ref, sh_ref, sc_ref, lay.group(i * tm), u_scr, tm)
    logits = jnp.dot(u_scr[...], wr_ref[...], preferred_element_type=F32)
    lane = lax.broadcasted_iota(jnp.int32, logits.shape, 1)
    big = jnp.int32(4 * LANES)
    neg = -jnp.inf
    epg = ne // ng

    gl = jnp.where(lane < ng, logits, neg)
    gmax = jnp.max(gl, axis=-1, keepdims=True)
    gidx = jnp.min(jnp.where(gl == gmax, lane, big), axis=-1, keepdims=True)
    g_p = 1.0 / jnp.sum(jnp.where(lane < ng, jnp.exp(logits - gmax), 0.0), axis=-1, keepdims=True)

    lo = ng + gidx * epg
    el = jnp.where((lane >= lo) & (lane < lo + epg), logits, neg)
    v1 = jnp.max(el, axis=-1, keepdims=True)
    i1 = jnp.min(jnp.where(el == v1, lane, big), axis=-1, keepdims=True)
    el2 = jnp.where(lane == i1, neg, el)
    v2 = jnp.max(el2, axis=-1, keepdims=True)
    i2 = jnp.min(jnp.where(el2 == v2, lane, big), axis=-1, keepdims=True)
    e1 = i1 - ng
    e2 = i2 - ng
    tt = jnp.exp(v2 - v1)
    p1 = 1.0 / (1.0 + tt)
    gate1 = p1 * g_p
    gate2 = (tt * p1) * g_p

    oh1 = lane == e1
    oh2 = lane == e2
    oh = jnp.where(oh1 | oh2, 1.0, 0.0)
    groups = jnp.floor((jnp.sum(oh, axis=0, keepdims=True) + (SUBLANES - 1)) * (1.0 / SUBLANES))
    groups8 = jnp.broadcast_to(groups, (SUBLANES, LANES))
    start = SUBLANES * jnp.dot(groups8.astype(BF16), upper_ref[...], preferred_element_type=F32)[0:1, :]
    prefix = jnp.dot(tri_ref[...], oh.astype(BF16), preferred_element_type=F32) + start
    rank1 = jnp.sum(jnp.where(oh1, prefix, 0.0), axis=-1, keepdims=True)
    rank2 = jnp.sum(jnp.where(oh2, prefix, 0.0), axis=-1, keepdims=True)
    cnt_ref[...] = groups8

    out = jnp.where(lane == 0, e1.astype(F32), 0.0)
    out = jnp.where(lane == 1, e2.astype(F32), out)
    out = jnp.where(lane == 2, gate1, out)
    out = jnp.where(lane == 3, gate2, out)
    out = jnp.where(lane == 4, rank1, out)
    out = jnp.where(lane == 5, rank2, out)
    o_ref[...] = out


def _router_call(lay, x, nw, mods, l, wr_bf, tri, upper, ng, ne):
    t, d = x.shape
    tm = tri.shape[0]
    nt = t // tm
    return pl.pallas_call(
        functools.partial(_router_kernel, lay=lay, tm=tm, ng=ng, ne=ne),
        out_shape=(jax.ShapeDtypeStruct((t, LANES), F32), jax.ShapeDtypeStruct((nt, SUBLANES, LANES), F32)),
        grid=(nt,),
        in_specs=[
            pl.BlockSpec((tm, d), lambda i: (i, 0)),
            pl.BlockSpec((1, d), lambda i: (0, 0)),
            _mod_spec(lay, l, 3, 1),
            _mod_spec(lay, l, 4, 1),
            pl.BlockSpec((d, LANES), lambda i: (0, 0)),
            pl.BlockSpec((tm, tm), lambda i: (0, 0)),
            pl.BlockSpec((LANES, LANES), lambda i: (0, 0)),
        ],
        out_specs=(pl.BlockSpec((tm, LANES), lambda i: (i, 0)),
                   pl.BlockSpec((None, SUBLANES, LANES), lambda i: (i, 0, 0))),
        scratch_shapes=[pltpu.VMEM((tm, d), BF16)],
        compiler_params=_cparams(("parallel",)),
        name="moe_router",
    )(x, nw, mods, mods, wr_bf, tri, upper)


def _pack_halves(lo, hi):
    lo_bits = lax.shift_right_logical(pltpu.bitcast(lo, jnp.uint32), jnp.uint32(16))
    hi_bits = pltpu.bitcast(hi, jnp.uint32) & jnp.uint32(0xFFFF0000)
    return hi_bits | lo_bits


def _unpack_halves(w):
    lo = pltpu.bitcast(lax.shift_left(w, jnp.uint32(16)), F32)
    hi = pltpu.bitcast(w & jnp.uint32(0xFFFF0000), F32)
    return lo.astype(BF16), hi.astype(BF16)


def _round_bf16(x):
    return x.astype(BF16).astype(F32)


def _group_copy(src, src_g, dst, dst_g, sem):
    g8 = lambda g: pl.ds(pl.multiple_of(g * SUBLANES, SUBLANES), SUBLANES)
    return pltpu.make_async_copy(src.at[g8(src_g), :], dst.at[g8(dst_g), :], sem)


def _dispatch_kernel(gdst_ref, ngt_ref, x_ref, nw_ref, sh_ref, sc_ref, pos_ref, xs_in, xs_out, u_scr, loc, sem,
                     *, lay, tm, nl):
    i = pl.program_id(0)
    _mod_rows(x_ref, nw_ref, sh_ref, sc_ref, lay.group(i * tm), u_scr, tm)
    pos1 = pos_ref[0:1, :]
    pos2 = pos_ref[1:2, :]
    half = u_scr.shape[1] // 2
    rc = 256
    for r in range(0, nl, rc):
        p = lax.broadcasted_iota(jnp.int32, (rc, tm), 0) + r
        onehot = jnp.where((p == pos1) | (p == pos2), 1.0, 0.0).astype(BF16)
        rows = jnp.dot(onehot, u_scr[...], preferred_element_type=F32)
        loc[r:r + rc, :] = _pack_halves(rows[:, :half], rows[:, half:])
    nlg = nl // SUBLANES
    n = ngt_ref[i]

    def copy(g):
        return _group_copy(loc, g, xs_out, gdst_ref[i * nlg + g], sem)

    lax.fori_loop(0, n, lambda g, c: (copy(g).start(), c)[1], 0)
    lax.fori_loop(0, n, lambda g, c: (copy(g).wait(), c)[1], 0)


def _dispatch_call(lay, x, nw, mods, l, pos_rows, gdst, ngt, xs_zero, tm, nl):
    t, d = x.shape
    grid_spec = pltpu.PrefetchScalarGridSpec(
        num_scalar_prefetch=2,
        grid=(t // tm,),
        in_specs=[
            pl.BlockSpec((tm, d), lambda i, *_: (i, 0)),
            pl.BlockSpec((1, d), lambda i, *_: (0, 0)),
            pl.BlockSpec((None, lay.ngp, d), lambda i, *_: (l, 0, 3)),
            pl.BlockSpec((None, lay.ngp, d), lambda i, *_: (l, 0, 4)),
            pl.BlockSpec((None, SUBLANES, tm), lambda i, *_: (i, 0, 0)),
            pl.BlockSpec(memory_space=pl.ANY),
        ],
        out_specs=pl.BlockSpec(memory_space=pl.ANY),
        scratch_shapes=[pltpu.VMEM((tm, d), BF16), pltpu.VMEM((nl, d // 2), jnp.uint32), pltpu.SemaphoreType.DMA(())],
    )
    return pl.pallas_call(
        functools.partial(_dispatch_kernel, lay=lay, tm=tm, nl=nl),
        out_shape=jax.ShapeDtypeStruct(xs_zero.shape, xs_zero.dtype),
        grid_spec=grid_spec,
        input_output_aliases={7: 0},
        compiler_params=_cparams(("arbitrary",)),
        name="moe_dispatch",
    )(gdst, ngt, x, nw, mods, mods, pos_rows, xs_zero)


def _expert_kernel(be_ref, nu_ref, x_ref, w13_ref, w2_ref, o_ref, w13_bf, w2_bf, *, hid):
    b = pl.program_id(0)

    @pl.when(b < nu_ref[0])
    def _():
        @pl.when((b == 0) | (be_ref[b] != be_ref[jnp.maximum(b - 1, 0)]))
        def _():
            w13_bf[...] = w13_ref[...].astype(BF16)
            w2_bf[...] = w2_ref[...].astype(BF16)

        x_lo, x_hi = _unpack_halves(x_ref[...])
        half = x_lo.shape[1]
        hb = (jnp.dot(x_lo, w13_bf[:half, :], preferred_element_type=F32)
              + jnp.dot(x_hi, w13_bf[half:, :], preferred_element_type=F32))
        a = hb[:, :hid]
        act = (a * _sigmoid(a)) * hb[:, hid:]
        y = _round_bf16(jnp.dot(act.astype(BF16), w2_bf[...], preferred_element_type=F32))
        o_ref[...] = _pack_halves(y[:, :half], y[:, half:])

    @pl.when(b >= nu_ref[0])
    def _():
        o_ref[...] = jnp.zeros(o_ref.shape, o_ref.dtype)


def _expert_call(xs, block_e, n_used, w13, w2, l):
    r, half = xs.shape
    d = 2 * half
    nb = r // MOE_BM
    hid = w2.shape[2]
    clamp = lambda b, nu: jnp.minimum(b, jnp.maximum(nu[0], 1) - 1)
    grid_spec = pltpu.PrefetchScalarGridSpec(
        num_scalar_prefetch=2,
        grid=(nb,),
        in_specs=[
            pl.BlockSpec((MOE_BM, half), lambda b, be, nu: (clamp(b, nu), 0)),
            pl.BlockSpec((None, None, d, 2 * hid), lambda b, be, nu: (l, be[b], 0, 0)),
            pl.BlockSpec((None, None, hid, d), lambda b, be, nu: (l, be[b], 0, 0)),
        ],
        out_specs=pl.BlockSpec((MOE_BM, half), lambda b, be, nu: (b, 0)),
        scratch_shapes=[pltpu.VMEM((d, 2 * hid), BF16), pltpu.VMEM((hid, d), BF16)],
    )
    return pl.pallas_call(
        functools.partial(_expert_kernel, hid=hid),
        out_shape=jax.ShapeDtypeStruct((r, half), jnp.uint32),
        grid_spec=grid_spec,
        compiler_params=_cparams(("arbitrary",)),
        name="moe_experts",
    )(block_e, n_used, xs, w13, w2)


def _combine_kernel(gdst_ref, ngt_ref, x_ref, rt_ref, gt_ref, ys_ref, o_ref, loc, sem, *, lay, tm, nl):
    i = pl.program_id(0)

    @pl.when(i == 0)
    def _():
        loc[...] = jnp.zeros(loc.shape, loc.dtype)

    nlg = nl // SUBLANES
    n = ngt_ref[i]

    def copy(g):
        return _group_copy(ys_ref, gdst_ref[i * nlg + g], loc, g, sem)

    lax.fori_loop(0, n, lambda g, c: (copy(g).start(), c)[1], 0)
    lax.fori_loop(0, n, lambda g, c: (copy(g).wait(), c)[1], 0)

    rt = rt_ref[...]
    gate1, gate2 = rt[:, 2:3], rt[:, 3:4]
    pos1, pos2 = rt[:, 4:5].astype(jnp.int32), rt[:, 5:6].astype(jnp.int32)
    half = loc.shape[1]
    rc = 256
    mix_lo = jnp.zeros((tm, half), F32)
    mix_hi = jnp.zeros((tm, half), F32)
    for r in range(0, nl, rc):
        p = lax.broadcasted_iota(jnp.int32, (tm, rc), 1) + r
        wgt = (jnp.where(p == pos1, gate1, 0.0) + jnp.where(p == pos2, gate2, 0.0)).astype(BF16)
        y_lo, y_hi = _unpack_halves(loc[r:r + rc, :])
        mix_lo = mix_lo + jnp.dot(wgt, y_lo, preferred_element_type=F32)
        mix_hi = mix_hi + jnp.dot(wgt, y_hi, preferred_element_type=F32)
    gate = gt_ref[pl.ds(lay.group(i * tm), 1), :]
    o_ref[:, :half] = x_ref[:, :half] + gate[:, :half] * mix_lo
    o_ref[:, half:] = x_ref[:, half:] + gate[:, half:] * mix_hi


def _combine_call(lay, x, route, mods, l, gdst, ngt, ys, tm, nl):
    t, d = x.shape
    grid_spec = pltpu.PrefetchScalarGridSpec(
        num_scalar_prefetch=2,
        grid=(t // tm,),
        in_specs=[
            pl.BlockSpec((tm, d), lambda i, *_: (i, 0)),
            pl.BlockSpec((tm, LANES), lambda i, *_: (i, 0)),
            pl.BlockSpec((None, lay.ngp, d), lambda i, *_: (l, 0, 5)),
            pl.BlockSpec(memory_space=pl.ANY),
        ],
        out_specs=pl.BlockSpec((tm, d), lambda i, *_: (i, 0)),
        scratch_shapes=[pltpu.VMEM((nl, d // 2), jnp.uint32), pltpu.SemaphoreType.DMA(())],
    )
    return pl.pallas_call(
        functools.partial(_combine_kernel, lay=lay, tm=tm, nl=nl),
        out_shape=jax.ShapeDtypeStruct(x.shape, F32),
        grid_spec=grid_spec,
        compiler_params=_cparams(("arbitrary",)),
        name="moe_combine",
    )(gdst, ngt, x, route, mods, ys)


def _moe_layer(lay, x, nw, mods, l, wr_bf, tri_r, upper, w13, w2, ng, ne):
    t, d = x.shape
    tm = tri_r.shape[0]
    nt = t // tm
    bmg = MOE_BM // SUBLANES
    nl = -(-(MOE_TOP_K * tm + ne * (SUBLANES - 1)) // 256) * 256
    nlg = nl // SUBLANES
    route, counts = _router_call(lay, x, nw, mods, l, wr_bf, tri_r, upper, ng, ne)

    c8 = counts[:, 0, :ne].astype(jnp.int32)
    lend = jnp.cumsum(c8, axis=1)
    lstart = lend - c8
    ngt = lend[:, -1]
    tot = jnp.sum(c8, axis=0)
    padded = (tot + bmg - 1) // bmg * bmg
    gend = jnp.cumsum(padded)
    gbase = (gend - padded)[None, :] + jnp.cumsum(c8, axis=0) - c8
    nb = -(-(MOE_TOP_K * t + nt * ne * (SUBLANES - 1)) // MOE_BM) + ne
    g = jnp.arange(nlg, dtype=jnp.int32)
    e_of_g = jnp.minimum(jnp.sum((g[None, :, None] >= lend[:, None, :]).astype(jnp.int32), axis=-1), ne - 1)
    gdst = g[None, :] + jnp.take_along_axis(gbase - lstart, e_of_g, axis=1)
    gdst = jnp.clip(gdst, 0, nb * bmg - 1).reshape(nt * nlg).astype(jnp.int32)
    block_e = jnp.minimum(jnp.searchsorted(gend, jnp.arange(nb, dtype=jnp.int32) * bmg, side="right"),
                          ne - 1).astype(jnp.int32)
    n_used = (gend[-1:] // bmg).astype(jnp.int32)
    pos = route[:, 2 * MOE_TOP_K:3 * MOE_TOP_K].astype(jnp.int32).reshape(nt, tm, MOE_TOP_K)
    pos_rows = jnp.full((nt, SUBLANES, tm), -1, jnp.int32).at[:, :MOE_TOP_K, :].set(jnp.swapaxes(pos, 1, 2))

    xs = _dispatch_call(lay, x, nw, mods, l, pos_rows, gdst, ngt.astype(jnp.int32),
                        jnp.zeros((nb * MOE_BM, d // 2), jnp.uint32), tm, nl)
    ys = _expert_call(xs, block_e, n_used, w13, w2, l)
    return _combine_call(lay, x, route, mods, l, gdst, ngt.astype(jnp.int32), ys, tm, nl)


def _final_kernel(x_ref, w_ref, o_ref):
    o_ref[...] = _rms(x_ref[...], w_ref[...])


def _final_call(x, w, row0, nrows):
    d = x.shape[1]
    tm = math.gcd(512, math.gcd(row0, nrows) if row0 else nrows)
    rb0 = row0 // tm
    return pl.pallas_call(
        _final_kernel,
        out_shape=jax.ShapeDtypeStruct((nrows, d), F32),
        grid=(nrows // tm,),
        in_specs=[pl.BlockSpec((tm, d), lambda i: (rb0 + i, 0)), pl.BlockSpec((1, d), lambda i: (0, 0))],
        out_specs=pl.BlockSpec((tm, d), lambda i: (i, 0)),
        compiler_params=_cparams(("parallel",)),
        name="final_norm",
    )(x, w)


def _lower_tri(n, strict):
    r = np.arange(n)
    m = (r[None, :] < r[:, None]) if strict else (r[None, :] <= r[:, None])
    return jnp.asarray(m.astype(np.float32), dtype=BF16)


def kernel(x_prompt, x_sample, state_C, state_n, state_m, c, c_ctx, ada_w, ada_b, norm1_w, norm2_w, m_w_in, m_b_gate, m_head_norm_w, m_w_out, f_w_out, cv_w_pw1, cv_b_pw1, cv_w_dw, cv_b_dw, cv_ln_w, cv_ln_b, cv_w_pw2, cv_b_pw2, r_w_group, r_w_expert, e_w13, e_w2, final_norm_w):
    nbp, sp, d = x_prompt.shape
    nbs, ss, _ = x_sample.shape
    assert ss % GRID_W == 0
    lay = _Layout(nbp, sp, nbs, ss, d)
    depth = ada_w.shape[0]
    nh, dh = state_C.shape[3], state_C.shape[4]
    di = nh * dh
    ng = r_w_group.shape[2]
    ne = r_w_expert.shape[2]
    assert ng == MOE_GROUPS and ng + ne <= LANES and 4 * nh <= LANES

    x = jnp.concatenate([x_prompt.reshape(lay.tp, d), x_sample.reshape(lay.ts, d)], axis=0)
    cv = jnp.zeros((lay.ngp, d), F32).at[0].set(c_ctx).at[1:1 + nbs].set(c)
    mods = _ada_call(cv, ada_w, ada_b)

    tri_l = _lower_tri(MLSTM_L, strict=False)
    tri_r = _lower_tri(lay.row_tile(512), strict=True)
    upper = _lower_tri(LANES, strict=True).T
    assert MOE_TOP_K == 2
    row = lambda a: a.reshape(1, -1)

    qkvos, gps = [], []
    for l in range(depth):
        j, kind = l // N_MIXERS, l % N_MIXERS
        nw1 = row(norm1_w[l])
        if kind == 0:
            wg = jnp.zeros((d, LANES), F32).at[:, :4 * nh].set(m_w_in[j][:, 4 * di:]).astype(BF16)
            bg = jnp.zeros((1, LANES), F32).at[0, :4 * nh].set(m_b_gate[j])
            qkvo, gates = _proj_call(lay, x, nw1, mods, l, m_w_in, j, wg, 4 * di)
            gp = _gate_prep_call(gates, bg, tri_l, nh)
            gpt = gp[:, :4 * nh].T
            hw = row(m_head_norm_w[j])
            hg = jnp.zeros((lay.t, di), BF16)
            hg = _mlstm_call(lay, hg, qkvo, gp, gpt, hw, nh, dh, prompt=True)
            n0 = state_n[:, j].reshape(nbs, 2 * nh, dh)
            m0 = jnp.broadcast_to(state_m[:, j].reshape(nbs, 2 * nh, 1), (nbs, 2 * nh, LANES))
            hg = _mlstm_call(lay, hg, qkvo, gp, gpt, hw, nh, dh, prompt=False, state=((state_C, j), n0, m0))
            x = _mm_res_call(lay, hg, m_w_out, j, x, mods, l, 2)
            qkvos.append(qkvo)
            gps.append(gp)
        elif kind == 1:
            wo = f_w_out[j].astype(BF16)
            x = _fnet_call(lay, x, nw1, mods, l, wo, prompt=True)
            x = _fnet_call(lay, x, nw1, mods, l, wo, prompt=False)
        else:
            glu = _glu_call(lay, x, nw1, mods, l, cv_w_pw1[j].astype(BF16), row(cv_b_pw1[j]))
            x = _conv_call(lay, glu, cv_w_dw[j], cv_b_dw[j], cv_ln_w[j], cv_ln_b[j], cv_w_pw2[j].astype(BF16),
                           cv_b_pw2[j], x, mods, l)
        wr = jnp.zeros((d, LANES), F32).at[:, :ng].set(r_w_group[l]).at[:, ng:ng + ne].set(r_w_expert[l])
        x = _moe_layer(lay, x, row(norm2_w[l]), mods, l, wr.astype(BF16), tri_r, upper, e_w13, e_w2, ng, ne)

    fw = row(final_norm_w)
    y_prompt = _final_call(x, fw, 0, lay.tp).reshape(nbp, sp, d)
    y_sample = _final_call(x, fw, lay.tp, lay.ts).reshape(nbs, ss, d)
    new_c, new_n, new_m = _state_call(lay, qkvos, gps, nh, dh)
    return (y_prompt, y_sample, new_c, new_n, new_m)
```

```python
import functools
import math

import numpy as np
import jax
import jax.numpy as jnp
from jax import lax
from jax.experimental import pallas as pl
from jax.experimental.pallas import tpu as pltpu

F32 = jnp.float32
BF16 = jnp.bfloat16
EPS = 1e-6
GRID_W = 64
N_MIXERS = 3
FNET_GROUPS = 4
CONV_WIDTH = 31
MOE_GROUPS = 4
MOE_TOP_K = 2

LANES = 128
SUBLANES = 8
MLSTM_L = 256
MOE_BM = 256
CONV_HALO = 16
VMEM_LIMIT = 56 * 1024 * 1024


def _cparams(sem, vmem=VMEM_LIMIT):
    return pltpu.CompilerParams(dimension_semantics=sem, vmem_limit_bytes=vmem)


def _bdot(a, b):
    return jnp.dot(a.astype(BF16), b.astype(BF16), preferred_element_type=F32)


def _rms(x, w):
    return x * lax.rsqrt(jnp.mean(x * x, axis=-1, keepdims=True) + EPS) * w


def _modulate(x, w, shift, scale):
    return _rms(x, w) * (1.0 + scale) + shift


def _sigmoid(x):
    return 1.0 / (1.0 + jnp.exp(-x))


def _log_sigmoid(x):
    return jnp.minimum(x, 0.0) - jnp.log(1.0 + jnp.exp(-jnp.abs(x)))


def _split3(x):
    hi = x.astype(BF16)
    r1 = x - hi.astype(F32)
    mid = r1.astype(BF16)
    lo = (r1 - mid.astype(F32)).astype(BF16)
    return hi, mid, lo


class _Layout:
    def __init__(self, nbp, sp, nbs, ss, d):
        self.nbp, self.sp, self.nbs, self.ss, self.d = nbp, sp, nbs, ss, d
        self.tp, self.ts = nbp * sp, nbs * ss
        self.t = self.tp + self.ts
        assert self.tp % ss == 0, "latent sequences must start on a block boundary of their own length"
        self.ngp = -(-(1 + nbs) // SUBLANES) * SUBLANES

    def group(self, row0):
        return jnp.where(row0 < self.tp, 0, 1 + (row0 - self.tp) // self.ss)

    def row_tile(self, want):
        tm = math.gcd(math.gcd(self.tp, self.ss), want)
        assert tm % SUBLANES == 0
        return tm


def _ada_kernel(cv_ref, w_ref, b_ref, o_ref):
    s = cv_ref[...]
    s = s * _sigmoid(s)
    o_ref[...] = _bdot(s, w_ref[...]) + b_ref[...]


def _ada_call(cv, ada_w, ada_b):
    depth, d, n = ada_w.shape
    ngp = cv.shape[0]
    tn = min(n, 2048)
    return pl.pallas_call(
        _ada_kernel,
        out_shape=jax.ShapeDtypeStruct((depth, ngp, n), F32),
        grid=(depth, n // tn),
        in_specs=[
            pl.BlockSpec((ngp, d), lambda l, j: (0, 0)),
            pl.BlockSpec((None, d, tn), lambda l, j: (l, 0, j)),
            pl.BlockSpec((None, 1, tn), lambda l, j: (l, 0, j)),
        ],
        out_specs=pl.BlockSpec((None, ngp, tn), lambda l, j: (l, 0, j)),
        compiler_params=_cparams(("parallel", "parallel")),
        name="ada_mods",
    )(cv, ada_w, ada_b.reshape(depth, 1, n))


def _mod_spec(lay, l, chunk, ngrid):
    if ngrid == 1:
        return pl.BlockSpec((None, lay.ngp, lay.d), lambda i: (l, 0, chunk))
    if ngrid == 2:
        return pl.BlockSpec((None, lay.ngp, lay.d), lambda i, j: (l, 0, chunk))
    return pl.BlockSpec((None, lay.ngp, lay.d), lambda i, j, k: (l, 0, chunk))


def _mod_rows(x_ref, w_ref, sh_ref, sc_ref, grp, dst_ref, rows, chunk=256):
    w = w_ref[...]
    sh = sh_ref[pl.ds(grp, 1), :]
    sc = sc_ref[pl.ds(grp, 1), :]
    chunk = min(chunk, rows)
    for r in range(0, rows, chunk):
        dst_ref[r:r + chunk, :] = _modulate(x_ref[r:r + chunk, :], w, sh, sc).astype(dst_ref.dtype)


def _proj_kernel(x_ref, nw_ref, sh_ref, sc_ref, w_ref, wg_ref, o_ref, og_ref, u_scr, *, lay, tm):
    i = pl.program_id(0)
    j = pl.program_id(1)

    @pl.when(j == 0)
    def _():
        _mod_rows(x_ref, nw_ref, sh_ref, sc_ref, lay.group(i * tm), u_scr, tm)
        og_ref[...] = _dot_nt(u_scr[...], wg_ref[...])

    o_ref[...] = _dot_nt(u_scr[...], w_ref[...].astype(BF16)).astype(BF16)


def _dot_nt(a, b):
    return lax.dot_general(a, b, (((1,), (1,)), ((), ())), preferred_element_type=F32)


def _proj_call(lay, x, nw, mods, l, w_in_t, jl, wg, nmain):
    t, d = x.shape
    tm = lay.row_tile(1024)
    tn = 1024
    kern = functools.partial(_proj_kernel, lay=lay, tm=tm)
    return pl.pallas_call(
        kern,
        out_shape=(jax.ShapeDtypeStruct((t, nmain), BF16), jax.ShapeDtypeStruct((t, LANES), F32)),
        grid=(t // tm, nmain // tn),
        in_specs=[
            pl.BlockSpec((tm, d), lambda i, j: (i, 0)),
            pl.BlockSpec((1, d), lambda i, j: (0, 0)),
            _mod_spec(lay, l, 0, 2),
            _mod_spec(lay, l, 1, 2),
            pl.BlockSpec((None, tn, d), lambda i, j: (jl, j, 0)),
            pl.BlockSpec((LANES, d), lambda i, j: (0, 0)),
        ],
        out_specs=(pl.BlockSpec((tm, tn), lambda i, j: (i, j)), pl.BlockSpec((tm, LANES), lambda i, j: (i, 0))),
        scratch_shapes=[pltpu.VMEM((tm, d), BF16)],
        compiler_params=_cparams(("parallel", "arbitrary")),
        name="mlstm_proj",
    )(x, nw, mods, mods, w_in_t, wg)


def _gate_prep_kernel(g_ref, b_ref, tri_ref, o_ref, *, nh):
    g = g_ref[...] + b_ref[...]
    lane = lax.broadcasted_iota(jnp.int32, g.shape, 1)
    is_f = ((lane >= nh) & (lane < 2 * nh)) | ((lane >= 3 * nh) & (lane < 4 * nh))
    lf = jnp.where(is_f, _log_sigmoid(g), 0.0)
    tri = tri_ref[...]
    hi, mid, lo = _split3(lf)
    prefix = (jnp.dot(tri, hi, preferred_element_type=F32) + jnp.dot(tri, mid, preferred_element_type=F32)
              + jnp.dot(tri, lo, preferred_element_type=F32))
    total = jnp.sum(lf, axis=0, keepdims=True)
    suffix = total - prefix + lf
    b = jnp.where(lane < 2 * nh, prefix, suffix)
    a = g - pltpu.roll(b, LANES - nh, 1)
    is_a = (lane < nh) | ((lane >= 2 * nh) & (lane < 3 * nh))
    o_ref[...] = jnp.where(is_a, a, b)


def _gate_prep_call(gates, bias, tri, nh):
    t = gates.shape[0]
    l = tri.shape[0]
    return pl.pallas_call(
        functools.partial(_gate_prep_kernel, nh=nh),
        out_shape=jax.ShapeDtypeStruct((t, LANES), F32),
        grid=(t // l,),
        in_specs=[
            pl.BlockSpec((l, LANES), lambda i: (i, 0)),
            pl.BlockSpec((1, LANES), lambda i: (0, 0)),
            pl.BlockSpec((l, l), lambda i: (0, 0)),
        ],
        out_specs=pl.BlockSpec((l, LANES), lambda i: (i, 0)),
        compiler_params=_cparams(("parallel",)),
        name="mlstm_gate_prep",
    )(gates, bias, tri)


def _col(tile, c):
    lane = lax.broadcasted_iota(jnp.int32, tile.shape, 1)
    return jnp.sum(jnp.where(lane == c, tile, 0.0), axis=-1, keepdims=True)


def _dir_masks(l):
    r = lax.broadcasted_iota(jnp.int32, (l, l), 0)
    c = lax.broadcasted_iota(jnp.int32, (l, l), 1)
    return c <= r, c >= r


def _head_epilogue(h, hw, o):
    hn = h * lax.rsqrt(jnp.mean(h * h, axis=-1, keepdims=True) + EPS) * hw
    return (hn * _sigmoid(o.astype(F32))).astype(BF16)


def _mlstm_single_kernel(q_ref, k_ref, v_ref, o_ref, gp_ref, gpt_ref, hw_ref, out_ref, *, nh, scale):
    h = pl.program_id(1)
    l = q_ref.shape[0]
    gp = gp_ref[...]
    qk = lax.dot_general(q_ref[...], k_ref[...], (((1,), (1,)), ((), ())), preferred_element_type=F32)
    masks = _dir_masks(l)
    p = None
    for d in range(2):
        a_r = gpt_ref[pl.ds(2 * nh * d + h, 1), :]
        b_c = _col(gp, 2 * nh * d + nh + h)
        g = jnp.where(masks[d], a_r, -jnp.inf)
        m = jnp.maximum(jnp.max(g, axis=-1, keepdims=True), 0.0)
        s = qk * jnp.exp(g - m) * scale
        den = jnp.sum(s, axis=-1, keepdims=True)
        inv = 1.0 / jnp.maximum(jnp.abs(den), jnp.exp(-(b_c + m)))
        p = s * inv if p is None else p + s * inv
    hh = jnp.dot(p.astype(BF16), v_ref[...], preferred_element_type=F32)
    out_ref[...] = _head_epilogue(hh, hw_ref[...], o_ref[...])


def _mlstm_multi_kernel(q_ref, k_ref, v_ref, o_ref, gp_ref, gpt_ref, hw_ref, c0_ref, n0_ref, m0_ref,
                        out_ref, hacc, cst, *, nh, nc, l, scale):
    h = pl.program_id(1)
    masks = _dir_masks(l)
    for d in range(2):
        cst[...] = c0_ref[d]
        n = n0_ref[pl.ds(d * nh + h, 1), :]
        m = m0_ref[pl.ds(d * nh + h, 1), 0:1]
        order = list(range(nc)) if d == 0 else list(range(nc - 1, -1, -1))
        for step, c in enumerate(order):
            r0 = c * l
            gp = gp_ref[r0:r0 + l, :]
            a_c = _col(gp, 2 * nh * d + h)
            b_c = _col(gp, 2 * nh * d + nh + h)
            a_r = gpt_ref[pl.ds(2 * nh * d + h, 1), r0:r0 + l]
            q = q_ref[r0:r0 + l, :]
            k = k_ref[r0:r0 + l, :]
            v = v_ref[r0:r0 + l, :]
            qk = lax.dot_general(q, k, (((1,), (1,)), ((), ())), preferred_element_type=F32)
            g = jnp.where(masks[d], a_r, -jnp.inf)
            mt = jnp.maximum(jnp.max(g, axis=-1, keepdims=True), m)
            s = qk * jnp.exp(g - mt) * scale
            w_prev = jnp.exp(m - mt) * scale
            qn = jnp.sum(q.astype(F32) * n, axis=-1, keepdims=True)
            den = jnp.sum(s, axis=-1, keepdims=True) + w_prev * qn
            inv = 1.0 / jnp.maximum(jnp.abs(den), jnp.exp(-(b_c + mt)))
            hc = jnp.dot((s * inv).astype(BF16), v, preferred_element_type=F32)
            hc = hc + (w_prev * inv) * jnp.dot(q, cst[...].astype(BF16), preferred_element_type=F32)
            if d == 0:
                hacc[r0:r0 + l, :] = hc
            else:
                hacc[r0:r0 + l, :] += hc
            if step + 1 < nc:
                m_last = jnp.max(mt, axis=0, keepdims=True)
                b_end = b_c[l - 1:l, :] if d == 0 else b_c[0:1, :]
                decay = jnp.exp(m - m_last)
                kw = k.astype(F32) * jnp.exp(a_c - m_last)
                cst[...] = decay * cst[...] + lax.dot_general(
                    kw.astype(BF16), v, (((0,), (0,)), ((), ())), preferred_element_type=F32)
                n = decay * n + jnp.sum(kw, axis=0, keepdims=True)
                m = b_end + m_last
    hw = hw_ref[...]
    for c in range(nc):
        r0 = c * l
        out_ref[r0:r0 + l, :] = _head_epilogue(hacc[r0:r0 + l, :], hw, o_ref[r0:r0 + l, :])


def _mlstm_call(lay, hg, qkvo, gp, gpt, hw, nh, dh, prompt, state=None):
    nb, s = (lay.nbp, lay.sp) if prompt else (lay.nbs, lay.ss)
    rb0 = 0 if prompt else lay.tp // s
    scale = dh ** -0.5
    common_in = [
        pl.BlockSpec((s, dh), lambda b, h: (rb0 + b, h)),
        pl.BlockSpec((s, dh), lambda b, h: (rb0 + b, nh + h)),
        pl.BlockSpec((s, dh), lambda b, h: (rb0 + b, 2 * nh + h)),
        pl.BlockSpec((s, dh), lambda b, h: (rb0 + b, 3 * nh + h)),
        pl.BlockSpec((s, LANES), lambda b, h: (rb0 + b, 0)),
        pl.BlockSpec((4 * nh, s), lambda b, h: (0, rb0 + b)),
        pl.BlockSpec((1, dh), lambda b, h: (0, h)),
    ]
    hg_spec = pl.BlockSpec((s, dh), lambda b, h: (rb0 + b, h))
    any_spec = pl.BlockSpec(memory_space=pl.ANY)
    if prompt:
        assert s == MLSTM_L
        kern = functools.partial(_mlstm_single_kernel, nh=nh, scale=scale)

        def body(hg_any, *refs):
            kern(*refs)

        return pl.pallas_call(
            body,
            out_shape=jax.ShapeDtypeStruct(hg.shape, hg.dtype),
            grid=(nb, nh),
            in_specs=[any_spec] + common_in,
            out_specs=hg_spec,
            input_output_aliases={0: 0},
            compiler_params=_cparams(("parallel", "parallel")),
            name="mlstm_prompt",
        )(hg, qkvo, qkvo, qkvo, qkvo, gp, gpt, hw)
    c0, n0, m0 = state
    nc = s // MLSTM_L
    kern = functools.partial(_mlstm_multi_kernel, nh=nh, nc=nc, l=MLSTM_L, scale=scale)

    def body(hg_any, *refs):
        kern(*refs)

    jl = c0[1]
    return pl.pallas_call(
        body,
        out_shape=jax.ShapeDtypeStruct(hg.shape, hg.dtype),
        grid=(nb, nh),
        in_specs=[any_spec] + common_in + [
            pl.BlockSpec((None, None, 2, None, dh, dh), lambda b, h: (b, jl, 0, h, 0, 0)),
            pl.BlockSpec((None, 2 * nh, dh), lambda b, h: (b, 0, 0)),
            pl.BlockSpec((None, 2 * nh, LANES), lambda b, h: (b, 0, 0)),
        ],
        out_specs=hg_spec,
        scratch_shapes=[pltpu.VMEM((s, dh), F32), pltpu.VMEM((dh, dh), F32)],
        input_output_aliases={0: 0},
        compiler_params=_cparams(("parallel", "parallel")),
        name="mlstm_latent",
    )(hg, qkvo, qkvo, qkvo, qkvo, gp, gpt, hw, c0[0], n0, m0)


def _state_kernel(*refs, nl, nh):
    ins, (c_ref, n_ref, m_ref) = refs[:3 * nl], refs[3 * nl:]
    lyr = pl.program_id(0)
    h = pl.program_id(2)

    @pl.when(h == 0)
    def _():
        m_ref[...] = jnp.zeros(m_ref.shape, F32)

    for jl in range(nl):
        k_ref, v_ref, gp_ref = ins[3 * jl:3 * jl + 3]

        @pl.when(lyr == jl)
        def _(k_ref=k_ref, v_ref=v_ref, gp_ref=gp_ref):
            l = k_ref.shape[0]
            gp = gp_ref[...]
            k = k_ref[...].astype(F32)
            v = v_ref[...]
            for d in range(2):
                a_c = _col(gp, 2 * nh * d + h)
                b_c = _col(gp, 2 * nh * d + nh + h)
                m_last = jnp.maximum(jnp.max(a_c, axis=0, keepdims=True), 0.0)
                b_end = b_c[l - 1:l, :] if d == 0 else b_c[0:1, :]
                kw = k * jnp.exp(a_c - m_last)
                c_ref[d] = lax.dot_general(kw.astype(BF16), v, (((0,), (0,)), ((), ())), preferred_element_type=F32)
                n_ref[d, pl.ds(h, 1), :] = jnp.sum(kw, axis=0, keepdims=True)
                sub = lax.broadcasted_iota(jnp.int32, m_ref.shape, 0)
                lane = lax.broadcasted_iota(jnp.int32, m_ref.shape, 1)
                m_ref[...] = jnp.where((sub == d) & (lane == h), b_end + m_last, m_ref[...])


def _state_call(lay, qkvos, gps, nh, dh):
    nl = len(qkvos)
    nbp, s = lay.nbp, lay.sp
    assert s == MLSTM_L

    def pick(jl, first, last):
        def f(lyr, idx):
            return jnp.where(lyr == jl, idx, jnp.where(lyr < jl, first, last))
        return f

    in_specs, args = [], []
    for jl in range(nl):
        for part in (1, 2):
            pb = pick(jl, 0, nbp - 1)
            ph = pick(jl, part * nh, part * nh + nh - 1)
            in_specs.append(pl.BlockSpec(
                (s, dh), lambda lyr, b, h, pb=pb, ph=ph, part=part: (pb(lyr, b), ph(lyr, part * nh + h))))
            args.append(qkvos[jl])
        pb = pick(jl, 0, nbp - 1)
        in_specs.append(pl.BlockSpec((s, LANES), lambda lyr, b, h, pb=pb: (pb(lyr, b), 0)))
        args.append(gps[jl])
    return pl.pallas_call(
        functools.partial(_state_kernel, nl=nl, nh=nh),
        out_shape=(
            jax.ShapeDtypeStruct((nbp, nl, 2, nh, dh, dh), F32),
            jax.ShapeDtypeStruct((nbp, nl, 2, nh, dh), F32),
            jax.ShapeDtypeStruct((nbp, nl, 2, nh), F32),
        ),
        grid=(nl, nbp, nh),
        in_specs=in_specs,
        out_specs=(
            pl.BlockSpec((None, None, 2, None, dh, dh), lambda lyr, b, h: (b, lyr, 0, h, 0, 0)),
            pl.BlockSpec((None, None, 2, nh, dh), lambda lyr, b, h: (b, lyr, 0, 0, 0)),
            pl.BlockSpec((None, None, 2, nh), lambda lyr, b, h: (b, lyr, 0, 0)),
        ),
        compiler_params=_cparams(("arbitrary", "arbitrary", "arbitrary")),
        name="mlstm_prompt_state",
    )(*args)


def _mm_res_kernel(a_ref, w_ref, x_ref, g_ref, o_ref, *, lay, tm):
    grp = lay.group(pl.program_id(0) * tm)
    acc = jnp.dot(a_ref[...], w_ref[...].astype(BF16), preferred_element_type=F32)
    o_ref[...] = x_ref[...] + g_ref[pl.ds(grp, 1), :] * acc


def _mm_res_call(lay, a, w3, jl, x, mods, l, chunk):
    t, kdim = a.shape
    d = x.shape[1]
    tm = lay.row_tile(1024)
    tn = 512
    nj = d // tn
    return pl.pallas_call(
        functools.partial(_mm_res_kernel, lay=lay, tm=tm),
        out_shape=jax.ShapeDtypeStruct(x.shape, F32),
        grid=(t // tm, nj),
        in_specs=[
            pl.BlockSpec((tm, kdim), lambda i, j: (i, 0)),
            pl.BlockSpec((None, kdim, tn), lambda i, j: (jl, 0, j)),
            pl.BlockSpec((tm, tn), lambda i, j: (i, j)),
            pl.BlockSpec((None, lay.ngp, tn), lambda i, j: (l, 0, chunk * nj + j)),
        ],
        out_specs=pl.BlockSpec((tm, tn), lambda i, j: (i, j)),
        compiler_params=_cparams(("parallel", "parallel")),
        name="mm_residual",
    )(a, w3, x, mods)


def _fnet_kernel(x_ref, nw_ref, sh_ref, sc_ref, gt_ref, wc_ref, ds_ref, wo_ref, o_ref, u_scr, ab_scr,
                 *, lay, row_base, groups, norm):
    s, d = x_ref.shape
    cg = d // groups
    grp = lay.group(row_base + pl.program_id(0) * s)
    _mod_rows(x_ref, nw_ref, sh_ref, sc_ref, grp, u_scr, s)
    wc = wc_ref[...]
    rc = min(s, 256)
    for g in range(groups):
        for r in range(0, s, rc):
            ab = jnp.dot(u_scr[r:r + rc, g * cg:(g + 1) * cg], wc, preferred_element_type=F32)
            ab_scr[r:r + rc, g * cg:(g + 1) * cg] = ab[:, :cg].astype(BF16)
            ab_scr[s + r:s + r + rc, g * cg:(g + 1) * cg] = ab[:, cg:].astype(BF16)
    gate = gt_ref[pl.ds(grp, 1), :]
    wo = wo_ref[...]
    for r in range(0, s, rc):
        y = jnp.dot(ds_ref[r:r + rc, :], ab_scr[...], preferred_element_type=F32) * norm
        o_ref[r:r + rc, :] = x_ref[r:r + rc, :] + gate * jnp.dot(y.astype(BF16), wo, preferred_element_type=F32)


def _dft_mats(s, cg):
    kc = np.arange(cg)
    ang_c = 2.0 * np.pi * np.outer(kc, kc) / cg
    wc = np.concatenate([np.cos(ang_c), np.sin(ang_c)], axis=1)
    ks = np.arange(s)
    ang_s = 2.0 * np.pi * np.outer(ks, ks) / s
    ds = np.concatenate([np.cos(ang_s), -np.sin(ang_s)], axis=1)
    return jnp.asarray(wc, dtype=BF16), jnp.asarray(ds, dtype=BF16)


def _fnet_call(lay, x, nw, mods, l, wo_bf, prompt):
    nb, s = (lay.nbp, lay.sp) if prompt else (lay.nbs, lay.ss)
    rb0 = 0 if prompt else lay.tp // s
    d = lay.d
    cg = d // FNET_GROUPS
    wc, ds = _dft_mats(s, cg)
    kern = functools.partial(_fnet_kernel, lay=lay, row_base=rb0 * s, groups=FNET_GROUPS,
                             norm=1.0 / math.sqrt(s * cg))
    return pl.pallas_call(
        kern,
        out_shape=jax.ShapeDtypeStruct(x.shape, F32),
        grid=(nb,),
        in_specs=[
            pl.BlockSpec((s, d), lambda b: (rb0 + b, 0)),
            pl.BlockSpec((1, d), lambda b: (0, 0)),
            _mod_spec(lay, l, 0, 1),
            _mod_spec(lay, l, 1, 1),
            _mod_spec(lay, l, 2, 1),
            pl.BlockSpec((cg, 2 * cg), lambda b: (0, 0)),
            pl.BlockSpec((s, 2 * s), lambda b: (0, 0)),
            pl.BlockSpec((d, d), lambda b: (0, 0)),
        ],
        out_specs=pl.BlockSpec((s, d), lambda b: (rb0 + b, 0)),
        scratch_shapes=[pltpu.VMEM((s, d), BF16), pltpu.VMEM((2 * s, d), BF16)],
        input_output_aliases={0: 0},
        compiler_params=_cparams(("parallel",)),
        name="fnet_prompt" if prompt else "fnet_latent",
    )(x, nw, mods, mods, mods, wc, ds, wo_bf)


def _glu_kernel(x_ref, nw_ref, sh_ref, sc_ref, wa_ref, wg_ref, ba_ref, bg_ref, o_ref, u_scr, *, lay, tm):
    i = pl.program_id(0)

    @pl.when(pl.program_id(1) == 0)
    def _():
        _mod_rows(x_ref, nw_ref, sh_ref, sc_ref, lay.group(i * tm), u_scr, tm)

    u = u_scr[...]
    a = jnp.dot(u, wa_ref[...], preferred_element_type=F32) + ba_ref[...]
    g = jnp.dot(u, wg_ref[...], preferred_element_type=F32) + bg_ref[...]
    o_ref[...] = a * _sigmoid(g)


def _glu_call(lay, x, nw, mods, l, w_bf, bias):
    t, d = x.shape
    cd = w_bf.shape[1] // 2
    tm = lay.row_tile(1024)
    tn = 512
    nj = cd // tn
    return pl.pallas_call(
        functools.partial(_glu_kernel, lay=lay, tm=tm),
        out_shape=jax.ShapeDtypeStruct((t, cd), F32),
        grid=(t // tm, nj),
        in_specs=[
            pl.BlockSpec((tm, d), lambda i, j: (i, 0)),
            pl.BlockSpec((1, d), lambda i, j: (0, 0)),
            _mod_spec(lay, l, 0, 2),
            _mod_spec(lay, l, 1, 2),
            pl.BlockSpec((d, tn), lambda i, j: (0, j)),
            pl.BlockSpec((d, tn), lambda i, j: (0, nj + j)),
            pl.BlockSpec((1, tn), lambda i, j: (0, j)),
            pl.BlockSpec((1, tn), lambda i, j: (0, nj + j)),
        ],
        out_specs=pl.BlockSpec((tm, tn), lambda i, j: (i, j)),
        scratch_shapes=[pltpu.VMEM((tm, d), BF16)],
        compiler_params=_cparams(("parallel", "arbitrary")),
        name="conv_glu",
    )(x, nw, mods, mods, w_bf, w_bf, bias, bias)


def _conv_kernel(c_ref, p_ref, n_ref, wd_ref, bd_ref, lw_ref, lb_ref, w2_ref, b2_ref, x_ref, gt_ref, o_ref,
                 pad, act, *, lay, rb, width):
    i = pl.program_id(0)
    row0 = i * rb
    grp = lay.group(row0)
    seq = jnp.where(row0 < lay.tp, lay.sp, lay.ss)
    pos = jnp.where(row0 < lay.tp, row0 % lay.sp, (row0 - lay.tp) % lay.ss)
    has_prev = (pos != 0).astype(F32)
    has_next = (pos + rb != seq).astype(F32)
    hl = CONV_HALO
    half = width // 2
    cd = c_ref.shape[1]
    span = pad.shape[1]
    pad[0, 0:hl, :] = p_ref[...] * has_prev
    pad[0, hl:hl + rb, :] = c_ref[...]
    pad[0, hl + rb:hl + rb + hl, :] = n_ref[...] * has_next
    for s in range(1, SUBLANES):
        pad[s, 0:span - SUBLANES, :] = pad[0, s:s + span - SUBLANES, :]
    sub = 4 * SUBLANES
    bd = bd_ref[...]
    lw = lw_ref[...]
    lb = lb_ref[...]

    def block(blk, carry):
        r0 = pl.multiple_of(blk * sub, sub)
        acc = jnp.zeros((sub // SUBLANES, SUBLANES, cd), F32) + bd
        for k in range(width):
            q, s = divmod(hl - half + k, SUBLANES)
            win = pad[s, pl.ds(r0 + q * SUBLANES, sub), :]
            acc = acc + win.reshape(sub // SUBLANES, SUBLANES, cd) * wd_ref[k]
        acc = acc.reshape(sub, cd)
        mu = jnp.mean(acc, axis=-1, keepdims=True)
        cen = acc - mu
        var = jnp.mean(cen * cen, axis=-1, keepdims=True)
        y = cen * lax.rsqrt(var + EPS) * lw + lb
        act[pl.ds(r0, sub), :] = (y * _sigmoid(y)).astype(BF16)
        return carry

    lax.fori_loop(0, rb // sub, block, 0)
    out = jnp.dot(act[...], w2_ref[...], preferred_element_type=F32) + b2_ref[...]
    o_ref[...] = x_ref[...] + gt_ref[pl.ds(grp, 1), :] * out


def _conv_call(lay, glu, wd, bd, lw, lb, w2_bf, b2, x, mods, l):
    t, cd = glu.shape
    d = x.shape[1]
    rb = lay.row_tile(256)
    hl = CONV_HALO
    assert CONV_WIDTH // 2 <= hl and rb % hl == 0
    nhb = t // hl
    per = rb // hl
    wd_p = jnp.broadcast_to(wd[:, None, :], (CONV_WIDTH, SUBLANES, cd))
    row = lambda a: a.reshape(1, -1)
    return pl.pallas_call(
        functools.partial(_conv_kernel, lay=lay, rb=rb, width=CONV_WIDTH),
        out_shape=jax.ShapeDtypeStruct(x.shape, F32),
        grid=(t // rb,),
        in_specs=[
            pl.BlockSpec((rb, cd), lambda i: (i, 0)),
            pl.BlockSpec((hl, cd), lambda i: (jnp.maximum(i * per - 1, 0), 0)),
            pl.BlockSpec((hl, cd), lambda i: (jnp.minimum((i + 1) * per, nhb - 1), 0)),
            pl.BlockSpec(wd_p.shape, lambda i: (0, 0, 0)),
            pl.BlockSpec((1, cd), lambda i: (0, 0)),
            pl.BlockSpec((1, cd), lambda i: (0, 0)),
            pl.BlockSpec((1, cd), lambda i: (0, 0)),
            pl.BlockSpec((cd, d), lambda i: (0, 0)),
            pl.BlockSpec((1, d), lambda i: (0, 0)),
            pl.BlockSpec((rb, d), lambda i: (i, 0)),
            _mod_spec(lay, l, 2, 1),
        ],
        out_specs=pl.BlockSpec((rb, d), lambda i: (i, 0)),
        scratch_shapes=[pltpu.VMEM((SUBLANES, rb + 2 * hl, cd), F32), pltpu.VMEM((rb, cd), BF16)],
        compiler_params=_cparams(("parallel",)),
        name="conv_dw_ln_pw2",
    )(glu, glu, glu, wd_p, row(bd), row(lw), row(lb), w2_bf, row(b2), x, mods)


def _router_kernel(x_ref, nw_ref, sh_ref, sc_ref, wr_ref, tri_ref, upper_ref, o_ref, cnt_ref, u_scr,
                   *, lay, tm, ng, ne):
    i = pl.program_id(0)
    _mod_rows(x_ref, nw_ref, sh_ref, sc_ref, lay.group(i * tm), u_scr, tm)
    logits = jnp.dot(u_scr[...], wr_ref[...], preferred_element_type=F32)
    lane = lax.broadcasted_iota(jnp.int32, logits.shape, 1)
    big = jnp.int32(4 * LANES)
    neg = -jnp.inf
    epg = ne // ng

    gl = jnp.where(lane < ng, logits, neg)
    gmax = jnp.max(gl, axis=-1, keepdims=True)
    gidx = jnp.min(jnp.where(gl == gmax, lane, big), axis=-1, keepdims=True)
    g_p = 1.0 / jnp.sum(jnp.where(lane < ng, jnp.exp(logits - gmax), 0.0), axis=-1, keepdims=True)

    lo = ng + gidx * epg
    el = jnp.where((lane >= lo) & (lane < lo + epg), logits, neg)
    v1 = jnp.max(el, axis=-1, keepdims=True)
    i1 = jnp.min(jnp.where(el == v1, lane, big), axis=-1, keepdims=True)
    el2 = jnp.where(lane == i1, neg, el)
    v2 = jnp.max(el2, axis=-1, keepdims=True)
    i2 = jnp.min(jnp.where(el2 == v2, lane, big), axis=-1, keepdims=True)
    e1 = i1 - ng
    e2 = i2 - ng
    tt = jnp.exp(v2 - v1)
    p1 = 1.0 / (1.0 + tt)
    gate1 = p1 * g_p
    gate2 = (tt * p1) * g_p

    oh1 = lane == e1
    oh2 = lane == e2
    oh = jnp.where(oh1 | oh2, 1.0, 0.0)
    groups = jnp.floor((jnp.sum(oh, axis=0, keepdims=True) + (SUBLANES - 1)) * (1.0 / SUBLANES))
    groups8 = jnp.broadcast_to(groups, (SUBLANES, LANES))
    start = SUBLANES * jnp.dot(groups8.astype(BF16), upper_ref[...], preferred_element_type=F32)[0:1, :]
    prefix = jnp.dot(tri_ref[...], oh.astype(BF16), preferred_element_type=F32) + start
    rank1 = jnp.sum(jnp.where(oh1, prefix, 0.0), axis=-1, keepdims=True)
    rank2 = jnp.sum(jnp.where(oh2, prefix, 0.0), axis=-1, keepdims=True)
    cnt_ref[...] = groups8

    out = jnp.where(lane == 0, e1.astype(F32), 0.0)
    out = jnp.where(lane == 1, e2.astype(F32), out)
    out = jnp.where(lane == 2, gate1, out)
    out = jnp.where(lane == 3, gate2, out)
    out = jnp.where(lane == 4, rank1, out)
    out = jnp.where(lane == 5, rank2, out)
    o_ref[...] = out


def _router_call(lay, x, nw, mods, l, wr_bf, tri, upper, ng, ne):
    t, d = x.shape
    tm = tri.shape[0]
    nt = t // tm
    return pl.pallas_call(
        functools.partial(_router_kernel, lay=lay, tm=tm, ng=ng, ne=ne),
        out_shape=(jax.ShapeDtypeStruct((t, LANES), F32), jax.ShapeDtypeStruct((nt, SUBLANES, LANES), F32)),
        grid=(nt,),
        in_specs=[
            pl.BlockSpec((tm, d), lambda i: (i, 0)),
            pl.BlockSpec((1, d), lambda i: (0, 0)),
            _mod_spec(lay, l, 3, 1),
            _mod_spec(lay, l, 4, 1),
            pl.BlockSpec((d, LANES), lambda i: (0, 0)),
            pl.BlockSpec((tm, tm), lambda i: (0, 0)),
            pl.BlockSpec((LANES, LANES), lambda i: (0, 0)),
        ],
        out_specs=(pl.BlockSpec((tm, LANES), lambda i: (i, 0)),
                   pl.BlockSpec((None, SUBLANES, LANES), lambda i: (i, 0, 0))),
        scratch_shapes=[pltpu.VMEM((tm, d), BF16)],
        compiler_params=_cparams(("parallel",)),
        name="moe_router",
    )(x, nw, mods, mods, wr_bf, tri, upper)


def _pack_halves(lo, hi):
    lo_bits = lax.shift_right_logical(pltpu.bitcast(lo, jnp.uint32), jnp.uint32(16))
    hi_bits = pltpu.bitcast(hi, jnp.uint32) & jnp.uint32(0xFFFF0000)
    return hi_bits | lo_bits


def _unpack_halves(w):
    lo = pltpu.bitcast(lax.shift_left(w, jnp.uint32(16)), F32)
    hi = pltpu.bitcast(w & jnp.uint32(0xFFFF0000), F32)
    return lo.astype(BF16), hi.astype(BF16)


def _round_bf16(x):
    return x.astype(BF16).astype(F32)


def _group_copy(src, src_g, dst, dst_g, sem):
    g8 = lambda g: pl.ds(pl.multiple_of(g * SUBLANES, SUBLANES), SUBLANES)
    return pltpu.make_async_copy(src.at[g8(src_g), :], dst.at[g8(dst_g), :], sem)


def _dispatch_kernel(gdst_ref, ngt_ref, x_ref, nw_ref, sh_ref, sc_ref, pos_ref, xs_in, xs_out, u_scr, loc, sem,
                     *, lay, tm, nl):
    i = pl.program_id(0)
    _mod_rows(x_ref, nw_ref, sh_ref, sc_ref, lay.group(i * tm), u_scr, tm)
    pos1 = pos_ref[0:1, :]
    pos2 = pos_ref[1:2, :]
    half = u_scr.shape[1] // 2
    rc = 256
    for r in range(0, nl, rc):
        p = lax.broadcasted_iota(jnp.int32, (rc, tm), 0) + r
        onehot = jnp.where((p == pos1) | (p == pos2), 1.0, 0.0).astype(BF16)
        rows = jnp.dot(onehot, u_scr[...], preferred_element_type=F32)
        loc[r:r + rc, :] = _pack_halves(rows[:, :half], rows[:, half:])
    nlg = nl // SUBLANES
    n = ngt_ref[i]

    def copy(g):
        return _group_copy(loc, g, xs_out, gdst_ref[i * nlg + g], sem)

    lax.fori_loop(0, n, lambda g, c: (copy(g).start(), c)[1], 0)
    lax.fori_loop(0, n, lambda g, c: (copy(g).wait(), c)[1], 0)


def _dispatch_call(lay, x, nw, mods, l, pos_rows, gdst, ngt, xs_zero, tm, nl):
    t, d = x.shape
    grid_spec = pltpu.PrefetchScalarGridSpec(
        num_scalar_prefetch=2,
        grid=(t // tm,),
        in_specs=[
            pl.BlockSpec((tm, d), lambda i, *_: (i, 0)),
            pl.BlockSpec((1, d), lambda i, *_: (0, 0)),
            pl.BlockSpec((None, lay.ngp, d), lambda i, *_: (l, 0, 3)),
            pl.BlockSpec((None, lay.ngp, d), lambda i, *_: (l, 0, 4)),
            pl.BlockSpec((None, SUBLANES, tm), lambda i, *_: (i, 0, 0)),
            pl.BlockSpec(memory_space=pl.ANY),
        ],
        out_specs=pl.BlockSpec(memory_space=pl.ANY),
        scratch_shapes=[pltpu.VMEM((tm, d), BF16), pltpu.VMEM((nl, d // 2), jnp.uint32), pltpu.SemaphoreType.DMA(())],
    )
    return pl.pallas_call(
        functools.partial(_dispatch_kernel, lay=lay, tm=tm, nl=nl),
        out_shape=jax.ShapeDtypeStruct(xs_zero.shape, xs_zero.dtype),
        grid_spec=grid_spec,
        input_output_aliases={7: 0},
        compiler_params=_cparams(("arbitrary",)),
        name="moe_dispatch",
    )(gdst, ngt, x, nw, mods, mods, pos_rows, xs_zero)


def _expert_kernel(be_ref, nu_ref, first_ref, slot_ref, next_ref, x_ref, w13_hbm, w2_hbm, o_ref,
                   w13_f, w2_f, w13_bf, w2_bf, sem13, sem2, *, hid, l):
    b = pl.program_id(0)

    def fetch(e, s):
        return (pltpu.make_async_copy(w13_hbm.at[l, e], w13_f.at[s], sem13.at[s]),
                pltpu.make_async_copy(w2_hbm.at[l, e], w2_f.at[s], sem2.at[s]))

    @pl.when(b < nu_ref[0])
    def _():
        @pl.when(first_ref[b] == 1)
        def _():
            s = slot_ref[b]

            @pl.when(b == 0)
            def _():
                for c in fetch(be_ref[b], s):
                    c.start()

            for c in fetch(be_ref[b], s):
                c.wait()

            @pl.when(next_ref[b] >= 0)
            def _():
                for c in fetch(next_ref[b], 1 - s):
                    c.start()

            w13_bf[...] = w13_f[s].astype(BF16)
            w2_bf[...] = w2_f[s].astype(BF16)

        x_lo, x_hi = _unpack_halves(x_ref[...])
        half = x_lo.shape[1]
        hb = (jnp.dot(x_lo, w13_bf[:half, :], preferred_element_type=F32)
              + jnp.dot(x_hi, w13_bf[half:, :], preferred_element_type=F32))
        a = hb[:, :hid]
        act = (a * _sigmoid(a)) * hb[:, hid:]
        y = _round_bf16(jnp.dot(act.astype(BF16), w2_bf[...], preferred_element_type=F32))
        o_ref[...] = _pack_halves(y[:, :half], y[:, half:])

    @pl.when(b >= nu_ref[0])
    def _():
        o_ref[...] = jnp.zeros(o_ref.shape, o_ref.dtype)


def _expert_call(xs, block_e, n_used, w13, w2, l):
    r, half = xs.shape
    d = 2 * half
    nb = r // MOE_BM
    hid = w2.shape[2]
    idx = jnp.arange(nb, dtype=jnp.int32)
    used = idx < n_used[0]
    first = used & ((idx == 0) | (block_e != jnp.roll(block_e, 1)))
    slot = (jnp.cumsum(first.astype(jnp.int32)) - 1) % 2
    fpos = jnp.where(first, idx, nb)
    next_first = jnp.concatenate([lax.cummin(fpos, reverse=True)[1:], jnp.full((1,), nb, jnp.int32)])
    nxt = jnp.where(next_first < nb, block_e[jnp.minimum(next_first, nb - 1)], -1)
    clamp = lambda b, nu: jnp.minimum(b, jnp.maximum(nu[0], 1) - 1)
    grid_spec = pltpu.PrefetchScalarGridSpec(
        num_scalar_prefetch=5,
        grid=(nb,),
        in_specs=[
            pl.BlockSpec((MOE_BM, half), lambda b, be, nu, *_: (clamp(b, nu), 0)),
            pl.BlockSpec(memory_space=pl.ANY),
            pl.BlockSpec(memory_space=pl.ANY),
        ],
        out_specs=pl.BlockSpec((MOE_BM, half), lambda b, *_: (b, 0)),
        scratch_shapes=[
            pltpu.VMEM((2, d, 2 * hid), F32), pltpu.VMEM((2, hid, d), F32),
            pltpu.VMEM((d, 2 * hid), BF16), pltpu.VMEM((hid, d), BF16),
            pltpu.SemaphoreType.DMA((2,)), pltpu.SemaphoreType.DMA((2,)),
        ],
    )
    return pl.pallas_call(
        functools.partial(_expert_kernel, hid=hid, l=l),
        out_shape=jax.ShapeDtypeStruct((r, half), jnp.uint32),
        grid_spec=grid_spec,
        compiler_params=_cparams(("arbitrary",)),
        name="moe_experts",
    )(block_e, n_used, first.astype(jnp.int32), slot.astype(jnp.int32), nxt.astype(jnp.int32), xs, w13, w2)


def _combine_kernel(gdst_ref, ngt_ref, x_ref, rt_ref, gt_ref, ys_ref, o_ref, loc, sem, *, lay, tm, nl):
    i = pl.program_id(0)

    @pl.when(i == 0)
    def _():
        loc[...] = jnp.zeros(loc.shape, loc.dtype)

    nlg = nl // SUBLANES
    n = ngt_ref[i]

    def copy(g):
        return _group_copy(ys_ref, gdst_ref[i * nlg + g], loc, g, sem)

    lax.fori_loop(0, n, lambda g, c: (copy(g).start(), c)[1], 0)
    lax.fori_loop(0, n, lambda g, c: (copy(g).wait(), c)[1], 0)

    rt = rt_ref[...]
    gate1, gate2 = rt[:, 2:3], rt[:, 3:4]
    pos1, pos2 = rt[:, 4:5].astype(jnp.int32), rt[:, 5:6].astype(jnp.int32)
    half = loc.shape[1]
    rc = 256
    mix_lo = jnp.zeros((tm, half), F32)
    mix_hi = jnp.zeros((tm, half), F32)
    for r in range(0, nl, rc):
        p = lax.broadcasted_iota(jnp.int32, (tm, rc), 1) + r
        wgt = (jnp.where(p == pos1, gate1, 0.0) + jnp.where(p == pos2, gate2, 0.0)).astype(BF16)
        y_lo, y_hi = _unpack_halves(loc[r:r + rc, :])
        mix_lo = mix_lo + jnp.dot(wgt, y_lo, preferred_element_type=F32)
        mix_hi = mix_hi + jnp.dot(wgt, y_hi, preferred_element_type=F32)
    gate = gt_ref[pl.ds(lay.group(i * tm), 1), :]
    o_ref[:, :half] = x_ref[:, :half] + gate[:, :half] * mix_lo
    o_ref[:, half:] = x_ref[:, half:] + gate[:, half:] * mix_hi


def _combine_call(lay, x, route, mods, l, gdst, ngt, ys, tm, nl):
    t, d = x.shape
    grid_spec = pltpu.PrefetchScalarGridSpec(
        num_scalar_prefetch=2,
        grid=(t // tm,),
        in_specs=[
            pl.BlockSpec((tm, d), lambda i, *_: (i, 0)),
            pl.BlockSpec((tm, LANES), lambda i, *_: (i, 0)),
            pl.BlockSpec((None, lay.ngp, d), lambda i, *_: (l, 0, 5)),
            pl.BlockSpec(memory_space=pl.ANY),
        ],
        out_specs=pl.BlockSpec((tm, d), lambda i, *_: (i, 0)),
        scratch_shapes=[pltpu.VMEM((nl, d // 2), jnp.uint32), pltpu.SemaphoreType.DMA(())],
    )
    return pl.pallas_call(
        functools.partial(_combine_kernel, lay=lay, tm=tm, nl=nl),
        out_shape=jax.ShapeDtypeStruct(x.shape, F32),
        grid_spec=grid_spec,
        compiler_params=_cparams(("arbitrary",)),
        name="moe_combine",
    )(gdst, ngt, x, route, mods, ys)


def _moe_layer(lay, x, nw, mods, l, wr_bf, tri_r, upper, w13, w2, ng, ne):
    t, d = x.shape
    tm = tri_r.shape[0]
    nt = t // tm
    bmg = MOE_BM // SUBLANES
    nl = -(-(MOE_TOP_K * tm + ne * (SUBLANES - 1)) // 256) * 256
    nlg = nl // SUBLANES
    route, counts = _router_call(lay, x, nw, mods, l, wr_bf, tri_r, upper, ng, ne)

    c8 = counts[:, 0, :ne].astype(jnp.int32)
    lend = jnp.cumsum(c8, axis=1)
    lstart = lend - c8
    ngt = lend[:, -1]
    tot = jnp.sum(c8, axis=0)
    padded = (tot + bmg - 1) // bmg * bmg
    gend = jnp.cumsum(padded)
    gbase = (gend - padded)[None, :] + jnp.cumsum(c8, axis=0) - c8
    nb = -(-(MOE_TOP_K * t + nt * ne * (SUBLANES - 1)) // MOE_BM) + ne
    g = jnp.arange(nlg, dtype=jnp.int32)
    e_of_g = jnp.minimum(jnp.sum((g[None, :, None] >= lend[:, None, :]).astype(jnp.int32), axis=-1), ne - 1)
    gdst = g[None, :] + jnp.take_along_axis(gbase - lstart, e_of_g, axis=1)
    gdst = jnp.clip(gdst, 0, nb * bmg - 1).reshape(nt * nlg).astype(jnp.int32)
    block_e = jnp.minimum(jnp.searchsorted(gend, jnp.arange(nb, dtype=jnp.int32) * bmg, side="right"),
                          ne - 1).astype(jnp.int32)
    n_used = (gend[-1:] // bmg).astype(jnp.int32)
    pos = route[:, 2 * MOE_TOP_K:3 * MOE_TOP_K].astype(jnp.int32).reshape(nt, tm, MOE_TOP_K)
    pos_rows = jnp.full((nt, SUBLANES, tm), -1, jnp.int32).at[:, :MOE_TOP_K, :].set(jnp.swapaxes(pos, 1, 2))

    xs = _dispatch_call(lay, x, nw, mods, l, pos_rows, gdst, ngt.astype(jnp.int32),
                        jnp.zeros((nb * MOE_BM, d // 2), jnp.uint32), tm, nl)
    ys = _expert_call(xs, block_e, n_used, w13, w2, l)
    return _combine_call(lay, x, route, mods, l, gdst, ngt.astype(jnp.int32), ys, tm, nl)


def _final_kernel(x_ref, w_ref, o_ref):
    o_ref[...] = _rms(x_ref[...], w_ref[...])


def _final_call(x, w, row0, nrows):
    d = x.shape[1]
    tm = math.gcd(512, math.gcd(row0, nrows) if row0 else nrows)
    rb0 = row0 // tm
    return pl.pallas_call(
        _final_kernel,
        out_shape=jax.ShapeDtypeStruct((nrows, d), F32),
        grid=(nrows // tm,),
        in_specs=[pl.BlockSpec((tm, d), lambda i: (rb0 + i, 0)), pl.BlockSpec((1, d), lambda i: (0, 0))],
        out_specs=pl.BlockSpec((tm, d), lambda i: (i, 0)),
        compiler_params=_cparams(("parallel",)),
        name="final_norm",
    )(x, w)


def _lower_tri(n, strict):
    r = np.arange(n)
    m = (r[None, :] < r[:, None]) if strict else (r[None, :] <= r[:, None])
    return jnp.asarray(m.astype(np.float32), dtype=BF16)


def kernel(x_prompt, x_sample, state_C, state_n, state_m, c, c_ctx, ada_w, ada_b, norm1_w, norm2_w, m_w_in, m_b_gate, m_head_norm_w, m_w_out, f_w_out, cv_w_pw1, cv_b_pw1, cv_w_dw, cv_b_dw, cv_ln_w, cv_ln_b, cv_w_pw2, cv_b_pw2, r_w_group, r_w_expert, e_w13, e_w2, final_norm_w):
    nbp, sp, d = x_prompt.shape
    nbs, ss, _ = x_sample.shape
    assert ss % GRID_W == 0
    lay = _Layout(nbp, sp, nbs, ss, d)
    depth = ada_w.shape[0]
    nh, dh = state_C.shape[3], state_C.shape[4]
    di = nh * dh
    ng = r_w_group.shape[2]
    ne = r_w_expert.shape[2]
    assert ng == MOE_GROUPS and ng + ne <= LANES and 4 * nh <= LANES

    x = jnp.concatenate([x_prompt.reshape(lay.tp, d), x_sample.reshape(lay.ts, d)], axis=0)
    cv = jnp.zeros((lay.ngp, d), F32).at[0].set(c_ctx).at[1:1 + nbs].set(c)
    mods = _ada_call(cv, ada_w, ada_b)

    tri_l = _lower_tri(MLSTM_L, strict=False)
    tri_r = _lower_tri(lay.row_tile(512), strict=True)
    upper = _lower_tri(LANES, strict=True).T
    assert MOE_TOP_K == 2
    row = lambda a: a.reshape(1, -1)
    w_in_t = jnp.swapaxes(m_w_in, 1, 2)

    qkvos, gps = [], []
    for l in range(depth):
        j, kind = l // N_MIXERS, l % N_MIXERS
        nw1 = row(norm1_w[l])
        if kind == 0:
            wg = jnp.zeros((LANES, d), F32).at[:4 * nh].set(w_in_t[j, 4 * di:]).astype(BF16)
            bg = jnp.zeros((1, LANES), F32).at[0, :4 * nh].set(m_b_gate[j])
            qkvo, gates = _proj_call(lay, x, nw1, mods, l, w_in_t, j, wg, 4 * di)
            gp = _gate_prep_call(gates, bg, tri_l, nh)
            gpt = gp[:, :4 * nh].T
            hw = row(m_head_norm_w[j])
            hg = jnp.zeros((lay.t, di), BF16)
            hg = _mlstm_call(lay, hg, qkvo, gp, gpt, hw, nh, dh, prompt=True)
            n0 = state_n[:, j].reshape(nbs, 2 * nh, dh)
            m0 = jnp.broadcast_to(state_m[:, j].reshape(nbs, 2 * nh, 1), (nbs, 2 * nh, LANES))
            hg = _mlstm_call(lay, hg, qkvo, gp, gpt, hw, nh, dh, prompt=False, state=((state_C, j), n0, m0))
            x = _mm_res_call(lay, hg, m_w_out, j, x, mods, l, 2)
            qkvos.append(qkvo)
            gps.append(gp)
        elif kind == 1:
            wo = f_w_out[j].astype(BF16)
            x = _fnet_call(lay, x, nw1, mods, l, wo, prompt=True)
            x = _fnet_call(lay, x, nw1, mods, l, wo, prompt=False)
        else:
            glu = _glu_call(lay, x, nw1, mods, l, cv_w_pw1[j].astype(BF16), row(cv_b_pw1[j]))
            x = _conv_call(lay, glu, cv_w_dw[j], cv_b_dw[j], cv_ln_w[j], cv_ln_b[j], cv_w_pw2[j].astype(BF16),
                           cv_b_pw2[j], x, mods, l)
        wr = jnp.zeros((d, LANES), F32).at[:, :ng].set(r_w_group[l]).at[:, ng:ng + ne].set(r_w_expert[l])
        x = _moe_layer(lay, x, row(norm2_w[l]), mods, l, wr.astype(BF16), tri_r, upper, e_w13, e_w2, ng, ne)

    fw = row(final_norm_w)
    y_prompt = _final_call(x, fw, 0, lay.tp).reshape(nbp, sp, d)
    y_sample = _final_call(x, fw, lay.tp, lay.ts).reshape(nbs, ss, d)
    new_c, new_n, new_m = _state_call(lay, qkvos, gps, nh, dh)
    return (y_prompt, y_sample, new_c, new_n, new_m)
```

```python
import functools
import math

import numpy as np
import jax
import jax.numpy as jnp
from jax import lax
from jax.experimental import pallas as pl
from jax.experimental.pallas import tpu as pltpu

F32 = jnp.float32
BF16 = jnp.bfloat16
U32 = jnp.uint32
EPS = 1e-6
GRID_W = 64
N_MIXERS = 3
FNET_GROUPS = 4
CONV_WIDTH = 31
MOE_GROUPS = 4
MOE_TOP_K = 2

LANES = 128
SUBLANES = 8
MLSTM_L = 256
MOE_BM = 256
CONV_HALO = 16
VMEM_LIMIT = 56 * 1024 * 1024


def _cparams(sem, vmem=VMEM_LIMIT):
    return pltpu.CompilerParams(dimension_semantics=sem, vmem_limit_bytes=vmem)


def _dot(a, b):
    return jnp.dot(a, b, preferred_element_type=F32)


def _dot_nt(a, b):
    return lax.dot_general(a, b, (((1,), (1,)), ((), ())), preferred_element_type=F32)


def _rms(x, w):
    return x * lax.rsqrt(jnp.mean(x * x, axis=-1, keepdims=True) + EPS) * w


def _modulate(x, w, shift, scale):
    return _rms(x, w) * (1.0 + scale) + shift


def _sigmoid(x):
    return 1.0 / (1.0 + jnp.exp(-x))


def _log_sigmoid(x):
    return jnp.minimum(x, 0.0) - jnp.log(1.0 + jnp.exp(-jnp.abs(x)))


def _split2(x):
    hi = x.astype(BF16)
    return hi, (x - hi.astype(F32)).astype(BF16)


def _split3(x):
    hi = x.astype(BF16)
    r1 = x - hi.astype(F32)
    mid = r1.astype(BF16)
    return hi, mid, (r1 - mid.astype(F32)).astype(BF16)


class _Layout:
    def __init__(self, nbp, sp, nbs, ss, d):
        self.nbp, self.sp, self.nbs, self.ss, self.d = nbp, sp, nbs, ss, d
        self.tp, self.ts = nbp * sp, nbs * ss
        self.t = self.tp + self.ts
        assert self.tp % ss == 0, "latent sequences must start on a block boundary of their own length"
        self.ngp = -(-(1 + nbs) // SUBLANES) * SUBLANES

    def group(self, row0):
        return jnp.where(row0 < self.tp, 0, 1 + (row0 - self.tp) // self.ss)

    def row_tile(self, want):
        tm = math.gcd(math.gcd(self.tp, self.ss), want)
        assert tm % SUBLANES == 0
        return tm

    def mod_spec(self, l, chunk):
        return pl.BlockSpec((None, self.ngp, self.d), lambda *_: (l, 0, chunk))

    def row_spec(self):
        return pl.BlockSpec((1, self.d), lambda *_: (0, 0))


class _Mod:
    def __init__(self, lay, mods, nw, l, c_shift):
        self.args = (nw.reshape(1, -1), mods, mods)
        self.specs = [lay.row_spec(), lay.mod_spec(l, c_shift), lay.mod_spec(l, c_shift + 1)]


def _mod_value(x, nw_ref, sh_ref, sc_ref, grp):
    return _modulate(x, nw_ref[...], sh_ref[pl.ds(grp, 1), :], sc_ref[pl.ds(grp, 1), :])


def _ada_kernel(cv_ref, w_ref, b_ref, o_ref):
    s = cv_ref[...]
    s = s * _sigmoid(s)
    o_ref[...] = _dot(s.astype(BF16), w_ref[...].astype(BF16)) + b_ref[...]


def _ada_call(cv, ada_w, ada_b):
    depth, d, n = ada_w.shape
    ngp = cv.shape[0]
    tn = min(n, 2048)
    return pl.pallas_call(
        _ada_kernel,
        out_shape=jax.ShapeDtypeStruct((depth, ngp, n), F32),
        grid=(depth, n // tn),
        in_specs=[
            pl.BlockSpec((ngp, d), lambda l, j: (0, 0)),
            pl.BlockSpec((None, d, tn), lambda l, j: (l, 0, j)),
            pl.BlockSpec((None, 1, tn), lambda l, j: (l, 0, j)),
        ],
        out_specs=pl.BlockSpec((None, ngp, tn), lambda l, j: (l, 0, j)),
        compiler_params=_cparams(("parallel", "parallel")),
        name="ada_mods",
    )(cv, ada_w, ada_b.reshape(depth, 1, n))


def _modulate_kernel(x_ref, nw_ref, sh_ref, sc_ref, o_ref, *, lay, tm):
    grp = lay.group(pl.program_id(0) * tm)
    o_ref[...] = _mod_value(x_ref[...], nw_ref, sh_ref, sc_ref, grp).astype(BF16)


def _modulate_call(lay, x, mod):
    t, d = x.shape
    tm = lay.row_tile(256)
    return pl.pallas_call(
        functools.partial(_modulate_kernel, lay=lay, tm=tm),
        out_shape=jax.ShapeDtypeStruct((t, d), BF16),
        grid=(t // tm,),
        in_specs=[pl.BlockSpec((tm, d), lambda i: (i, 0))] + mod.specs,
        out_specs=pl.BlockSpec((tm, d), lambda i: (i, 0)),
        compiler_params=_cparams(("parallel",)),
        name="modulate",
    )(x, *mod.args)


def _gates_kernel(x_ref, nw_ref, sh_ref, sc_ref, wh_ref, wl_ref, b_ref, tri_ref, o_ref, *, lay, nh):
    l = x_ref.shape[0]
    u = _mod_value(x_ref[...], nw_ref, sh_ref, sc_ref, lay.group(pl.program_id(0) * l))
    u_hi, u_lo = _split2(u)
    wh = wh_ref[...]
    g = _dot_nt(u_hi, wh) + _dot_nt(u_lo, wh) + _dot_nt(u_hi, wl_ref[...]) + b_ref[...]
    lane = lax.broadcasted_iota(jnp.int32, g.shape, 1)
    is_f = ((lane >= nh) & (lane < 2 * nh)) | ((lane >= 3 * nh) & (lane < 4 * nh))
    lf = jnp.where(is_f, _log_sigmoid(g), 0.0)
    tri = tri_ref[...]
    hi, mid, lo = _split3(lf)
    prefix = _dot(tri, hi) + _dot(tri, mid) + _dot(tri, lo)
    suffix = jnp.sum(lf, axis=0, keepdims=True) - prefix + lf
    b = jnp.where(lane < 2 * nh, prefix, suffix)
    a = g - pltpu.roll(b, LANES - nh, 1)
    is_a = (lane < nh) | ((lane >= 2 * nh) & (lane < 3 * nh))
    o_ref[...] = jnp.where(is_a, a, b)


def _gates_call(lay, x, mod, wg, bias, tri, nh):
    t, d = x.shape
    l = tri.shape[0]
    wh, wl = _split2(wg)
    return pl.pallas_call(
        functools.partial(_gates_kernel, lay=lay, nh=nh),
        out_shape=jax.ShapeDtypeStruct((t, LANES), F32),
        grid=(t // l,),
        in_specs=[pl.BlockSpec((l, d), lambda i: (i, 0))] + mod.specs + [
            pl.BlockSpec((LANES, d), lambda i: (0, 0)),
            pl.BlockSpec((LANES, d), lambda i: (0, 0)),
            pl.BlockSpec((1, LANES), lambda i: (0, 0)),
            pl.BlockSpec((l, l), lambda i: (0, 0)),
        ],
        out_specs=pl.BlockSpec((l, LANES), lambda i: (i, 0)),
        compiler_params=_cparams(("parallel",)),
        name="mlstm_gates",
    )(x, *mod.args, wh, wl, bias, tri)


def _proj_kernel(u_ref, w_ref, o_ref, w_bf, *, transposed):
    @pl.when(pl.program_id(1) == 0)
    def _():
        w_bf[...] = w_ref[...].astype(BF16)

    if transposed:
        o_ref[...] = _dot_nt(w_bf[...], u_ref[...]).astype(BF16)
    else:
        o_ref[...] = _dot_nt(u_ref[...], w_bf[...]).astype(BF16)


def _proj_call(lay, u, w_in_t, jl, blocks, tn, transposed):
    t, d = u.shape
    tm = lay.row_tile(1024)
    nb = len(blocks)
    first, gap_at, gap = blocks[0], None, 0
    for idx in range(1, nb):
        if blocks[idx] != blocks[idx - 1] + 1:
            assert gap_at is None
            gap_at, gap = idx, blocks[idx] - blocks[idx - 1] - 1
    wblk = (lambda j: first + j) if gap_at is None else (lambda j: first + j + jnp.where(j >= gap_at, gap, 0))
    if transposed:
        out_shape = jax.ShapeDtypeStruct((nb * tn, t), BF16)
        out_spec = pl.BlockSpec((tn, tm), lambda j, i: (j, i))
    else:
        out_shape = jax.ShapeDtypeStruct((t, nb * tn), BF16)
        out_spec = pl.BlockSpec((tm, tn), lambda j, i: (i, j))
    return pl.pallas_call(
        functools.partial(_proj_kernel, transposed=transposed),
        out_shape=out_shape,
        grid=(nb, t // tm),
        in_specs=[
            pl.BlockSpec((tm, d), lambda j, i: (i, 0)),
            pl.BlockSpec((None, tn, d), lambda j, i: (jl, wblk(j), 0)),
        ],
        out_specs=out_spec,
        scratch_shapes=[pltpu.VMEM((tn, d), BF16)],
        compiler_params=_cparams(("parallel", "arbitrary")),
        name="mlstm_proj_t" if transposed else "mlstm_proj",
    )(u, w_in_t)


def _col(tile, c):
    lane = lax.broadcasted_iota(jnp.int32, tile.shape, 1)
    return jnp.sum(jnp.where(lane == c, tile, 0.0), axis=-1, keepdims=True)


def _dir_masks(l):
    r = lax.broadcasted_iota(jnp.int32, (l, l), 0)
    c = lax.broadcasted_iota(jnp.int32, (l, l), 1)
    return c <= r, c >= r


def _head_epilogue(h, hw, o):
    hn = h * lax.rsqrt(jnp.mean(h * h, axis=-1, keepdims=True) + EPS) * hw
    return (hn * _sigmoid(o.astype(F32))).astype(BF16)


def _row_times_kt(w_row, kt):
    hi, lo = _split2(w_row)
    sub = lax.broadcasted_iota(jnp.int32, (SUBLANES, w_row.shape[1]), 0)
    stacked = jnp.where(sub == 0, hi.astype(F32), jnp.where(sub == 1, lo.astype(F32), 0.0)).astype(BF16)
    res = _dot_nt(stacked, kt)
    return res[0:1, :] + res[1:2, :]


def _mlstm_single_kernel(q_ref, kt_ref, v_ref, o_ref, gp_ref, gpt_ref, hw_ref, out_ref, *, nh, scale):
    h = pl.program_id(1)
    l = q_ref.shape[0]
    gp = gp_ref[...]
    qk = _dot(q_ref[...], kt_ref[...])
    masks = _dir_masks(l)
    p = None
    for d in range(2):
        a_r = gpt_ref[pl.ds(2 * nh * d + h, 1), :]
        b_c = _col(gp, 2 * nh * d + nh + h)
        g = jnp.where(masks[d], a_r, -jnp.inf)
        m = jnp.maximum(jnp.max(g, axis=-1, keepdims=True), 0.0)
        s = qk * jnp.exp(g - m) * scale
        den = jnp.sum(s, axis=-1, keepdims=True)
        inv = 1.0 / jnp.maximum(jnp.abs(den), jnp.exp(-(b_c + m)))
        p = s * inv if p is None else p + s * inv
    hh = _dot(p.astype(BF16), v_ref[...])
    out_ref[...] = _head_epilogue(hh, hw_ref[...], o_ref[...])


def _mlstm_multi_kernel(q_ref, kt_ref, v_ref, o_ref, gp_ref, gpt_ref, hw_ref, c0_ref, n0_ref, m0_ref,
                        out_ref, hacc, cst, *, nh, nc, l, scale):
    h = pl.program_id(1)
    masks = _dir_masks(l)
    for d in range(2):
        cst[...] = c0_ref[d]
        n = n0_ref[pl.ds(d * nh + h, 1), :]
        m = m0_ref[pl.ds(d * nh + h, 1), 0:1]
        order = list(range(nc)) if d == 0 else list(range(nc - 1, -1, -1))
        for step, c in enumerate(order):
            r0 = c * l
            b_c = _col(gp_ref[r0:r0 + l, :], 2 * nh * d + nh + h)
            a_r = gpt_ref[pl.ds(2 * nh * d + h, 1), r0:r0 + l]
            q = q_ref[r0:r0 + l, :]
            kt = kt_ref[:, r0:r0 + l]
            v = v_ref[r0:r0 + l, :]
            qk = _dot(q, kt)
            g = jnp.where(masks[d], a_r, -jnp.inf)
            mt = jnp.maximum(jnp.max(g, axis=-1, keepdims=True), m)
            s = qk * jnp.exp(g - mt) * scale
            w_prev = jnp.exp(m - mt) * scale
            qn = jnp.sum(q.astype(F32) * n, axis=-1, keepdims=True)
            den = jnp.sum(s, axis=-1, keepdims=True) + w_prev * qn
            inv = 1.0 / jnp.maximum(jnp.abs(den), jnp.exp(-(b_c + mt)))
            hc = _dot((s * inv).astype(BF16), v) + (w_prev * inv) * _dot(q, cst[...].astype(BF16))
            if d == 0:
                hacc[r0:r0 + l, :] = hc
            else:
                hacc[r0:r0 + l, :] += hc
            if step + 1 < nc:
                m_last = jnp.max(mt, axis=0, keepdims=True)
                b_end = b_c[l - 1:l, :] if d == 0 else b_c[0:1, :]
                decay = jnp.exp(m - m_last)
                w_end = jnp.exp(a_r - m_last)
                cst[...] = decay * cst[...] + _dot((kt.astype(F32) * w_end).astype(BF16), v)
                n = decay * n + _row_times_kt(w_end, kt)
                m = b_end + m_last
    hw = hw_ref[...]
    for c in range(nc):
        r0 = c * l
        out_ref[r0:r0 + l, :] = _head_epilogue(hacc[r0:r0 + l, :], hw, o_ref[r0:r0 + l, :])


def _mlstm_call(lay, hg, qvo, kt, gp, gpt, hw, nh, dh, prompt, state=None):
    nb, s = (lay.nbp, lay.sp) if prompt else (lay.nbs, lay.ss)
    rb0 = 0 if prompt else lay.tp // s
    scale = dh ** -0.5
    common_in = [
        pl.BlockSpec((s, dh), lambda b, h: (rb0 + b, h)),
        pl.BlockSpec((dh, s), lambda b, h: (h, rb0 + b)),
        pl.BlockSpec((s, dh), lambda b, h: (rb0 + b, nh + h)),
        pl.BlockSpec((s, dh), lambda b, h: (rb0 + b, 2 * nh + h)),
        pl.BlockSpec((s, LANES), lambda b, h: (rb0 + b, 0)),
        pl.BlockSpec((4 * nh, s), lambda b, h: (0, rb0 + b)),
        pl.BlockSpec((1, dh), lambda b, h: (0, h)),
    ]
    hg_spec = pl.BlockSpec((s, dh), lambda b, h: (rb0 + b, h))
    any_spec = pl.BlockSpec(memory_space=pl.ANY)
    if prompt:
        assert s == MLSTM_L
        kern = functools.partial(_mlstm_single_kernel, nh=nh, scale=scale)

        def body(hg_any, *refs):
            kern(*refs)

        return pl.pallas_call(
            body,
            out_shape=jax.ShapeDtypeStruct(hg.shape, hg.dtype),
            grid=(nb, nh),
            in_specs=[any_spec] + common_in,
            out_specs=hg_spec,
            input_output_aliases={0: 0},
            compiler_params=_cparams(("parallel", "parallel")),
            name="mlstm_prompt",
        )(hg, qvo, kt, qvo, qvo, gp, gpt, hw)
    state_c, jl, n0, m0 = state
    nc = s // MLSTM_L
    kern = functools.partial(_mlstm_multi_kernel, nh=nh, nc=nc, l=MLSTM_L, scale=scale)

    def body(hg_any, *refs):
        kern(*refs)

    return pl.pallas_call(
        body,
        out_shape=jax.ShapeDtypeStruct(hg.shape, hg.dtype),
        grid=(nb, nh),
        in_specs=[any_spec] + common_in + [
            pl.BlockSpec((None, None, 2, None, dh, dh), lambda b, h: (b, jl, 0, h, 0, 0)),
            pl.BlockSpec((None, 2 * nh, dh), lambda b, h: (b, 0, 0)),
            pl.BlockSpec((None, 2 * nh, LANES), lambda b, h: (b, 0, 0)),
        ],
        out_specs=hg_spec,
        scratch_shapes=[pltpu.VMEM((s, dh), F32), pltpu.VMEM((dh, dh), F32)],
        input_output_aliases={0: 0},
        compiler_params=_cparams(("parallel", "parallel")),
        name="mlstm_latent",
    )(hg, qvo, kt, qvo, qvo, gp, gpt, hw, state_c, n0, m0)


def _state_kernel(*refs, nl, nh):
    ins, (c_ref, n_ref, m_ref) = refs[:3 * nl], refs[3 * nl:]
    lyr = pl.program_id(0)
    h = pl.program_id(2)

    @pl.when(h == 0)
    def _():
        m_ref[...] = jnp.zeros(m_ref.shape, F32)

    for jl in range(nl):
        kt_ref, v_ref, gpt_ref = ins[3 * jl:3 * jl + 3]

        @pl.when(lyr == jl)
        def _(kt_ref=kt_ref, v_ref=v_ref, gpt_ref=gpt_ref):
            l = v_ref.shape[0]
            kt = kt_ref[...]
            ktf = kt.astype(F32)
            v = v_ref[...]
            for d in range(2):
                a_r = gpt_ref[pl.ds(2 * nh * d + h, 1), :]
                b_r = gpt_ref[pl.ds(2 * nh * d + nh + h, 1), :]
                m_last = jnp.maximum(jnp.max(a_r, axis=-1, keepdims=True), 0.0)
                b_end = b_r[:, l - 1:l] if d == 0 else b_r[:, 0:1]
                w_end = jnp.exp(a_r - m_last)
                c_ref[d] = _dot((ktf * w_end).astype(BF16), v)
                n_ref[d, pl.ds(h, 1), :] = _row_times_kt(w_end, kt)
                sub = lax.broadcasted_iota(jnp.int32, m_ref.shape, 0)
                lane = lax.broadcasted_iota(jnp.int32, m_ref.shape, 1)
                m_ref[...] = jnp.where((sub == d) & (lane == h), b_end + m_last, m_ref[...])


def _state_call(lay, qvos, kts, gpts, nh, dh):
    nl = len(qvos)
    nbp, s = lay.nbp, lay.sp
    assert s == MLSTM_L

    def pick(jl, first, last):
        def f(lyr, idx):
            return jnp.where(lyr == jl, idx, jnp.where(lyr < jl, first, last))
        return f

    in_specs, args = [], []
    for jl in range(nl):
        pb = pick(jl, 0, nbp - 1)
        ph = pick(jl, 0, nh - 1)
        pv = pick(jl, nh, 2 * nh - 1)
        in_specs.append(pl.BlockSpec((dh, s), lambda lyr, b, h, pb=pb, ph=ph: (ph(lyr, h), pb(lyr, b))))
        in_specs.append(pl.BlockSpec((s, dh), lambda lyr, b, h, pb=pb, pv=pv: (pb(lyr, b), pv(lyr, nh + h))))
        in_specs.append(pl.BlockSpec((4 * nh, s), lambda lyr, b, h, pb=pb: (0, pb(lyr, b))))
        args += [kts[jl], qvos[jl], gpts[jl]]
    return pl.pallas_call(
        functools.partial(_state_kernel, nl=nl, nh=nh),
        out_shape=(
            jax.ShapeDtypeStruct((nbp, nl, 2, nh, dh, dh), F32),
            jax.ShapeDtypeStruct((nbp, nl, 2, nh, dh), F32),
            jax.ShapeDtypeStruct((nbp, nl, 2, nh), F32),
        ),
        grid=(nl, nbp, nh),
        in_specs=in_specs,
        out_specs=(
            pl.BlockSpec((None, None, 2, None, dh, dh), lambda lyr, b, h: (b, lyr, 0, h, 0, 0)),
            pl.BlockSpec((None, None, 2, nh, dh), lambda lyr, b, h: (b, lyr, 0, 0, 0)),
            pl.BlockSpec((None, None, 2, nh), lambda lyr, b, h: (b, lyr, 0, 0)),
        ),
        compiler_params=_cparams(("arbitrary", "arbitrary", "arbitrary")),
        name="mlstm_prompt_state",
    )(*args)


def _mm_res_kernel(a_ref, w_ref, x_ref, g_ref, nw_ref, sh_ref, sc_ref, o_ref, u_ref, *, lay, tm):
    grp = lay.group(pl.program_id(0) * tm)
    gate = g_ref[pl.ds(grp, 1), :]
    w = w_ref[...]
    rc = min(tm, 256)
    for r in range(0, tm, rc):
        xn = x_ref[r:r + rc, :] + gate * _dot(a_ref[r:r + rc, :], w)
        o_ref[r:r + rc, :] = xn
        u_ref[r:r + rc, :] = _mod_value(xn, nw_ref, sh_ref, sc_ref, grp).astype(BF16)


def _mm_res_call(lay, a, w_bf, x, mods, l, mod2):
    t, kdim = a.shape
    d = x.shape[1]
    tm = lay.row_tile(512)
    return pl.pallas_call(
        functools.partial(_mm_res_kernel, lay=lay, tm=tm),
        out_shape=(jax.ShapeDtypeStruct(x.shape, F32), jax.ShapeDtypeStruct(x.shape, BF16)),
        grid=(t // tm,),
        in_specs=[
            pl.BlockSpec((tm, kdim), lambda i: (i, 0)),
            pl.BlockSpec((kdim, d), lambda i: (0, 0)),
            pl.BlockSpec((tm, d), lambda i: (i, 0)),
            lay.mod_spec(l, 2),
        ] + mod2.specs,
        out_specs=(pl.BlockSpec((tm, d), lambda i: (i, 0)), pl.BlockSpec((tm, d), lambda i: (i, 0))),
        compiler_params=_cparams(("parallel",)),
        name="mm_residual",
    )(a, w_bf, x, mods, *mod2.args)


def _fnet_kernel(x_ref, u_ref, u2_any, gt_ref, wc_ref, ds_ref, wo_ref, nw_ref, sh_ref, sc_ref, o_ref, u2_ref,
                 ab_scr, *, lay, row_base, groups, norm):
    s, d = x_ref.shape
    cg = d // groups
    grp = lay.group(row_base + pl.program_id(0) * s)
    wc = wc_ref[...]
    rc = min(s, 256)
    for g in range(groups):
        for r in range(0, s, rc):
            ab = _dot(u_ref[r:r + rc, g * cg:(g + 1) * cg], wc)
            ab_scr[r:r + rc, g * cg:(g + 1) * cg] = ab[:, :cg].astype(BF16)
            ab_scr[s + r:s + r + rc, g * cg:(g + 1) * cg] = ab[:, cg:].astype(BF16)
    gate = gt_ref[pl.ds(grp, 1), :]
    wo = wo_ref[...]
    for r in range(0, s, rc):
        y = _dot(ds_ref[r:r + rc, :], ab_scr[...]) * norm
        xn = x_ref[r:r + rc, :] + gate * _dot(y.astype(BF16), wo)
        o_ref[r:r + rc, :] = xn
        u2_ref[r:r + rc, :] = _mod_value(xn, nw_ref, sh_ref, sc_ref, grp).astype(BF16)


def _dft_mats(s, cg):
    kc = np.arange(cg)
    ang_c = 2.0 * np.pi * np.outer(kc, kc) / cg
    wc = np.concatenate([np.cos(ang_c), np.sin(ang_c)], axis=1)
    ks = np.arange(s)
    ang_s = 2.0 * np.pi * np.outer(ks, ks) / s
    ds = np.concatenate([np.cos(ang_s), -np.sin(ang_s)], axis=1)
    return jnp.asarray(wc, dtype=BF16), jnp.asarray(ds, dtype=BF16)


def _fnet_call(lay, x, u1, u2, mods, l, wo_bf, mod2, prompt):
    nb, s = (lay.nbp, lay.sp) if prompt else (lay.nbs, lay.ss)
    rb0 = 0 if prompt else lay.tp // s
    d = lay.d
    cg = d // FNET_GROUPS
    wc, ds = _dft_mats(s, cg)
    kern = functools.partial(_fnet_kernel, lay=lay, row_base=rb0 * s, groups=FNET_GROUPS,
                             norm=1.0 / math.sqrt(s * cg))
    blk = pl.BlockSpec((s, d), lambda b: (rb0 + b, 0))
    return pl.pallas_call(
        kern,
        out_shape=(jax.ShapeDtypeStruct(x.shape, F32), jax.ShapeDtypeStruct(u2.shape, BF16)),
        grid=(nb,),
        in_specs=[
            blk,
            blk,
            pl.BlockSpec(memory_space=pl.ANY),
            lay.mod_spec(l, 2),
            pl.BlockSpec((cg, 2 * cg), lambda b: (0, 0)),
            pl.BlockSpec((s, 2 * s), lambda b: (0, 0)),
            pl.BlockSpec((d, d), lambda b: (0, 0)),
        ] + mod2.specs,
        out_specs=(blk, blk),
        scratch_shapes=[pltpu.VMEM((2 * s, d), BF16)],
        input_output_aliases={0: 0, 2: 1},
        compiler_params=_cparams(("parallel",)),
        name="fnet_prompt" if prompt else "fnet_latent",
    )(x, u1, u2, mods, wc, ds, wo_bf, *mod2.args)


def _glu_kernel(u_ref, wa_ref, wg_ref, ba_ref, bg_ref, o_ref):
    u = u_ref[...]
    a = _dot(u, wa_ref[...]) + ba_ref[...]
    g = _dot(u, wg_ref[...]) + bg_ref[...]
    o_ref[...] = a * _sigmoid(g)


def _glu_call(lay, u, w_bf, bias):
    t, d = u.shape
    cd = w_bf.shape[1] // 2
    tm = lay.row_tile(1024)
    tn = 512
    nj = cd // tn
    return pl.pallas_call(
        _glu_kernel,
        out_shape=jax.ShapeDtypeStruct((t, cd), F32),
        grid=(t // tm, nj),
        in_specs=[
            pl.BlockSpec((tm, d), lambda i, j: (i, 0)),
            pl.BlockSpec((d, tn), lambda i, j: (0, j)),
            pl.BlockSpec((d, tn), lambda i, j: (0, nj + j)),
            pl.BlockSpec((1, tn), lambda i, j: (0, j)),
            pl.BlockSpec((1, tn), lambda i, j: (0, nj + j)),
        ],
        out_specs=pl.BlockSpec((tm, tn), lambda i, j: (i, j)),
        compiler_params=_cparams(("parallel", "parallel")),
        name="conv_glu",
    )(u, w_bf, w_bf, bias, bias)


def _conv_kernel(c_ref, p_ref, n_ref, wd_ref, bd_ref, lw_ref, lb_ref, w2_ref, b2_ref, x_ref, gt_ref,
                 nw_ref, sh_ref, sc_ref, o_ref, u2_ref, pad, act, *, lay, rb, width):
    i = pl.program_id(0)
    row0 = i * rb
    grp = lay.group(row0)
    seq = jnp.where(row0 < lay.tp, lay.sp, lay.ss)
    pos = jnp.where(row0 < lay.tp, row0 % lay.sp, (row0 - lay.tp) % lay.ss)
    has_prev = (pos != 0).astype(F32)
    has_next = (pos + rb != seq).astype(F32)
    hl = CONV_HALO
    half = width // 2
    cd = c_ref.shape[1]
    span = pad.shape[1]
    pad[0, 0:hl, :] = p_ref[...] * has_prev
    pad[0, hl:hl + rb, :] = c_ref[...]
    pad[0, hl + rb:hl + rb + hl, :] = n_ref[...] * has_next
    for s in range(1, SUBLANES):
        pad[s, 0:span - SUBLANES, :] = pad[0, s:s + span - SUBLANES, :]
    sub = 4 * SUBLANES
    bd = bd_ref[...]
    lw = lw_ref[...]
    lb = lb_ref[...]

    def block(blk, carry):
        r0 = pl.multiple_of(blk * sub, sub)
        acc = jnp.zeros((sub // SUBLANES, SUBLANES, cd), F32) + bd
        for k in range(width):
            q, s = divmod(hl - half + k, SUBLANES)
            win = pad[s, pl.ds(r0 + q * SUBLANES, sub), :]
            acc = acc + win.reshape(sub // SUBLANES, SUBLANES, cd) * wd_ref[k]
        acc = acc.reshape(sub, cd)
        mu = jnp.mean(acc, axis=-1, keepdims=True)
        cen = acc - mu
        var = jnp.mean(cen * cen, axis=-1, keepdims=True)
        y = cen * lax.rsqrt(var + EPS) * lw + lb
        act[pl.ds(r0, sub), :] = (y * _sigmoid(y)).astype(BF16)
        return carry

    lax.fori_loop(0, rb // sub, block, 0)
    xn = x_ref[...] + gt_ref[pl.ds(grp, 1), :] * (_dot(act[...], w2_ref[...]) + b2_ref[...])
    o_ref[...] = xn
    u2_ref[...] = _mod_value(xn, nw_ref, sh_ref, sc_ref, grp).astype(BF16)


def _conv_call(lay, glu, wd, bd, lw, lb, w2_bf, b2, x, mods, l, mod2):
    t, cd = glu.shape
    d = x.shape[1]
    rb = lay.row_tile(256)
    hl = CONV_HALO
    assert CONV_WIDTH // 2 <= hl and rb % hl == 0
    nhb = t // hl
    per = rb // hl
    wd_p = jnp.broadcast_to(wd[:, None, :], (CONV_WIDTH, SUBLANES, cd))
    row = lambda a: a.reshape(1, -1)
    rows = pl.BlockSpec((rb, d), lambda i: (i, 0))
    return pl.pallas_call(
        functools.partial(_conv_kernel, lay=lay, rb=rb, width=CONV_WIDTH),
        out_shape=(jax.ShapeDtypeStruct(x.shape, F32), jax.ShapeDtypeStruct(x.shape, BF16)),
        grid=(t // rb,),
        in_specs=[
            pl.BlockSpec((rb, cd), lambda i: (i, 0)),
            pl.BlockSpec((hl, cd), lambda i: (jnp.maximum(i * per - 1, 0), 0)),
            pl.BlockSpec((hl, cd), lambda i: (jnp.minimum((i + 1) * per, nhb - 1), 0)),
            pl.BlockSpec(wd_p.shape, lambda i: (0, 0, 0)),
            pl.BlockSpec((1, cd), lambda i: (0, 0)),
            pl.BlockSpec((1, cd), lambda i: (0, 0)),
            pl.BlockSpec((1, cd), lambda i: (0, 0)),
            pl.BlockSpec((cd, d), lambda i: (0, 0)),
            pl.BlockSpec((1, d), lambda i: (0, 0)),
            rows,
            lay.mod_spec(l, 2),
        ] + mod2.specs,
        out_specs=(rows, rows),
        scratch_shapes=[pltpu.VMEM((SUBLANES, rb + 2 * hl, cd), F32), pltpu.VMEM((rb, cd), BF16)],
        compiler_params=_cparams(("parallel",)),
        name="conv_dw_ln_pw2",
    )(glu, glu, glu, wd_p, row(bd), row(lw), row(lb), w2_bf, row(b2), x, mods, *mod2.args)


def _router_kernel(u_ref, wr_ref, tri_ref, upper_ref, o_ref, cnt_ref, *, ng, ne):
    logits = _dot(u_ref[...], wr_ref[...])
    lane = lax.broadcasted_iota(jnp.int32, logits.shape, 1)
    big = jnp.int32(4 * LANES)
    neg = -jnp.inf
    epg = ne // ng

    gl = jnp.where(lane < ng, logits, neg)
    gmax = jnp.max(gl, axis=-1, keepdims=True)
    gidx = jnp.min(jnp.where(gl == gmax, lane, big), axis=-1, keepdims=True)
    g_p = 1.0 / jnp.sum(jnp.where(lane < ng, jnp.exp(logits - gmax), 0.0), axis=-1, keepdims=True)

    lo = ng + gidx * epg
    el = jnp.where((lane >= lo) & (lane < lo + epg), logits, neg)
    v1 = jnp.max(el, axis=-1, keepdims=True)
    i1 = jnp.min(jnp.where(el == v1, lane, big), axis=-1, keepdims=True)
    el2 = jnp.where(lane == i1, neg, el)
    v2 = jnp.max(el2, axis=-1, keepdims=True)
    i2 = jnp.min(jnp.where(el2 == v2, lane, big), axis=-1, keepdims=True)
    e1 = i1 - ng
    e2 = i2 - ng
    tt = jnp.exp(v2 - v1)
    p1 = 1.0 / (1.0 + tt)
    gate1 = p1 * g_p
    gate2 = (tt * p1) * g_p

    oh1 = lane == e1
    oh2 = lane == e2
    oh = jnp.where(oh1 | oh2, 1.0, 0.0)
    groups = jnp.floor((jnp.sum(oh, axis=0, keepdims=True) + (SUBLANES - 1)) * (1.0 / SUBLANES))
    groups8 = jnp.broadcast_to(groups, (SUBLANES, LANES))
    start = SUBLANES * _dot(groups8.astype(BF16), upper_ref[...])[0:1, :]
    prefix = _dot(tri_ref[...], oh.astype(BF16)) + start
    pos1 = jnp.sum(jnp.where(oh1, prefix, 0.0), axis=-1, keepdims=True)
    pos2 = jnp.sum(jnp.where(oh2, prefix, 0.0), axis=-1, keepdims=True)
    cnt_ref[...] = groups8

    out = jnp.where(lane == 0, e1.astype(F32), 0.0)
    out = jnp.where(lane == 1, e2.astype(F32), out)
    out = jnp.where(lane == 2, gate1, out)
    out = jnp.where(lane == 3, gate2, out)
    out = jnp.where(lane == 4, pos1, out)
    out = jnp.where(lane == 5, pos2, out)
    o_ref[...] = out


def _router_call(u, wr_bf, tri, upper, ng, ne):
    t, d = u.shape
    tm = tri.shape[0]
    nt = t // tm
    return pl.pallas_call(
        functools.partial(_router_kernel, ng=ng, ne=ne),
        out_shape=(jax.ShapeDtypeStruct((t, LANES), F32), jax.ShapeDtypeStruct((nt, SUBLANES, LANES), F32)),
        grid=(nt,),
        in_specs=[
            pl.BlockSpec((tm, d), lambda i: (i, 0)),
            pl.BlockSpec((d, LANES), lambda i: (0, 0)),
            pl.BlockSpec((tm, tm), lambda i: (0, 0)),
            pl.BlockSpec((LANES, LANES), lambda i: (0, 0)),
        ],
        out_specs=(pl.BlockSpec((tm, LANES), lambda i: (i, 0)),
                   pl.BlockSpec((None, SUBLANES, LANES), lambda i: (i, 0, 0))),
        compiler_params=_cparams(("parallel",)),
        name="moe_router",
    )(u, wr_bf, tri, upper)


def _pack_halves(lo, hi):
    lo_bits = lax.shift_right_logical(pltpu.bitcast(lo, U32), jnp.uint32(16))
    hi_bits = pltpu.bitcast(hi, U32) & jnp.uint32(0xFFFF0000)
    return hi_bits | lo_bits


def _unpack_halves(w):
    lo = pltpu.bitcast(lax.shift_left(w, jnp.uint32(16)), F32)
    hi = pltpu.bitcast(w & jnp.uint32(0xFFFF0000), F32)
    return lo.astype(BF16), hi.astype(BF16)


def _round_bf16(x):
    return x.astype(BF16).astype(F32)


def _group_copy(src, src_g, dst, dst_g, sem):
    g8 = lambda g: pl.ds(pl.multiple_of(g * SUBLANES, SUBLANES), SUBLANES)
    return pltpu.make_async_copy(src.at[g8(src_g), :], dst.at[g8(dst_g), :], sem)


def _for_groups(n, fn):
    def body(g, c):
        fn(g)
        return c
    lax.fori_loop(0, n, body, 0)


def _dispatch_kernel(gdst_ref, ngt_ref, u_ref, pos_ref, xs_in, xs_out, loc, sem, *, tm, nl):
    i = pl.program_id(0)
    nt = pl.num_programs(0)
    slot = i % 2
    nlg = nl // SUBLANES

    def copy(step, s, g):
        return _group_copy(loc.at[s], g, xs_out, gdst_ref[step * nlg + g], sem.at[s])

    @pl.when(i >= 2)
    def _():
        _for_groups(ngt_ref[i - 2], lambda g: copy(i - 2, slot, g).wait())

    pos1 = pos_ref[0:1, :]
    pos2 = pos_ref[1:2, :]
    half = u_ref.shape[1] // 2
    u = u_ref[...]
    rc = 256
    for r in range(0, nl, rc):
        p = lax.broadcasted_iota(jnp.int32, (rc, tm), 0) + r
        onehot = jnp.where((p == pos1) | (p == pos2), 1.0, 0.0).astype(BF16)
        rows = _dot(onehot, u)
        loc[slot, r:r + rc, :] = _pack_halves(rows[:, :half], rows[:, half:])
    _for_groups(ngt_ref[i], lambda g: copy(i, slot, g).start())

    @pl.when(i == nt - 1)
    def _():
        @pl.when(i >= 1)
        def _():
            _for_groups(ngt_ref[i - 1], lambda g: copy(i - 1, 1 - slot, g).wait())

        _for_groups(ngt_ref[i], lambda g: copy(i, slot, g).wait())


def _dispatch_call(u, pos_rows, gdst, ngt, xs_zero, tm, nl):
    t, d = u.shape
    grid_spec = pltpu.PrefetchScalarGridSpec(
        num_scalar_prefetch=2,
        grid=(t // tm,),
        in_specs=[
            pl.BlockSpec((tm, d), lambda i, *_: (i, 0)),
            pl.BlockSpec((None, SUBLANES, tm), lambda i, *_: (i, 0, 0)),
            pl.BlockSpec(memory_space=pl.ANY),
        ],
        out_specs=pl.BlockSpec(memory_space=pl.ANY),
        scratch_shapes=[pltpu.VMEM((2, nl, d // 2), U32), pltpu.SemaphoreType.DMA((2,))],
    )
    return pl.pallas_call(
        functools.partial(_dispatch_kernel, tm=tm, nl=nl),
        out_shape=jax.ShapeDtypeStruct(xs_zero.shape, xs_zero.dtype),
        grid_spec=grid_spec,
        input_output_aliases={4: 0},
        compiler_params=_cparams(("arbitrary",)),
        name="moe_dispatch",
    )(gdst, ngt, u, pos_rows, xs_zero)


def _expert_kernel(be_ref, nu_ref, first_ref, slot_ref, next_ref, x_ref, w13_hbm, w2_hbm, o_ref,
                   w13_f, w2_f, w13_bf, w2_bf, sem13, sem2, *, hid, l):
    b = pl.program_id(0)

    def fetch(e, s):
        return (pltpu.make_async_copy(w13_hbm.at[l, e], w13_f.at[s], sem13.at[s]),
                pltpu.make_async_copy(w2_hbm.at[l, e], w2_f.at[s], sem2.at[s]))

    @pl.when(b < nu_ref[0])
    def _():
        @pl.when(first_ref[b] == 1)
        def _():
            s = slot_ref[b]

            @pl.when(b == 0)
            def _():
                for c in fetch(be_ref[b], s):
                    c.start()

            for c in fetch(be_ref[b], s):
                c.wait()

            @pl.when(next_ref[b] >= 0)
            def _():
                for c in fetch(next_ref[b], 1 - s):
                    c.start()

            w13_bf[...] = w13_f[s].astype(BF16)
            w2_bf[...] = w2_f[s].astype(BF16)

        x_lo, x_hi = _unpack_halves(x_ref[...])
        half = x_lo.shape[1]
        hb = _dot(x_lo, w13_bf[:half, :]) + _dot(x_hi, w13_bf[half:, :])
        a = hb[:, :hid]
        act = (a * _sigmoid(a)) * hb[:, hid:]
        y = _round_bf16(_dot(act.astype(BF16), w2_bf[...]))
        o_ref[...] = _pack_halves(y[:, :half], y[:, half:])

    @pl.when(b >= nu_ref[0])
    def _():
        o_ref[...] = jnp.zeros(o_ref.shape, o_ref.dtype)


def _expert_call(xs, block_e, n_used, w13, w2, l):
    r, half = xs.shape
    d = 2 * half
    nb = r // MOE_BM
    hid = w2.shape[2]
    idx = jnp.arange(nb, dtype=jnp.int32)
    used = idx < n_used[0]
    first = used & ((idx == 0) | (block_e != jnp.roll(block_e, 1)))
    slot = (jnp.cumsum(first.astype(jnp.int32)) - 1) % 2
    fpos = jnp.where(first, idx, nb)
    next_first = jnp.concatenate([lax.cummin(fpos, reverse=True)[1:], jnp.full((1,), nb, jnp.int32)])
    nxt = jnp.where(next_first < nb, block_e[jnp.minimum(next_first, nb - 1)], -1)
    clamp = lambda b, nu: jnp.minimum(b, jnp.maximum(nu[0], 1) - 1)
    grid_spec = pltpu.PrefetchScalarGridSpec(
        num_scalar_prefetch=5,
        grid=(nb,),
        in_specs=[
            pl.BlockSpec((MOE_BM, half), lambda b, be, nu, *_: (clamp(b, nu), 0)),
            pl.BlockSpec(memory_space=pl.ANY),
            pl.BlockSpec(memory_space=pl.ANY),
        ],
        out_specs=pl.BlockSpec((MOE_BM, half), lambda b, *_: (b, 0)),
        scratch_shapes=[
            pltpu.VMEM((2, d, 2 * hid), F32), pltpu.VMEM((2, hid, d), F32),
            pltpu.VMEM((d, 2 * hid), BF16), pltpu.VMEM((hid, d), BF16),
            pltpu.SemaphoreType.DMA((2,)), pltpu.SemaphoreType.DMA((2,)),
        ],
    )
    return pl.pallas_call(
        functools.partial(_expert_kernel, hid=hid, l=l),
        out_shape=jax.ShapeDtypeStruct((r, half), U32),
        grid_spec=grid_spec,
        compiler_params=_cparams(("arbitrary",)),
        name="moe_experts",
    )(block_e, n_used, first.astype(jnp.int32), slot.astype(jnp.int32), nxt.astype(jnp.int32), xs, w13, w2)


def _combine_kernel(gdst_ref, ngt_ref, x_ref, rt_ref, gt_ref, nw_ref, sh_ref, sc_ref, ys_ref, o_ref, u_ref,
                    loc, sem, *, lay, tm, nl, final):
    i = pl.program_id(0)
    nt = pl.num_programs(0)
    slot = i % 2
    nlg = nl // SUBLANES

    def copy(step, s, g):
        return _group_copy(ys_ref, gdst_ref[step * nlg + g], loc.at[s], g, sem.at[s])

    @pl.when(i == 0)
    def _():
        loc[...] = jnp.zeros(loc.shape, loc.dtype)
        _for_groups(ngt_ref[0], lambda g: copy(0, 0, g).start())

    _for_groups(ngt_ref[i], lambda g: copy(i, slot, g).wait())

    @pl.when(i + 1 < nt)
    def _():
        _for_groups(ngt_ref[i + 1], lambda g: copy(i + 1, 1 - slot, g).start())

    rt = rt_ref[...]
    gate1, gate2 = rt[:, 2:3], rt[:, 3:4]
    pos1, pos2 = rt[:, 4:5].astype(jnp.int32), rt[:, 5:6].astype(jnp.int32)
    half = loc.shape[2]
    rc = 256
    mix_lo = jnp.zeros((tm, half), F32)
    mix_hi = jnp.zeros((tm, half), F32)
    for r in range(0, nl, rc):
        p = lax.broadcasted_iota(jnp.int32, (tm, rc), 1) + r
        wgt = (jnp.where(p == pos1, gate1, 0.0) + jnp.where(p == pos2, gate2, 0.0)).astype(BF16)
        y_lo, y_hi = _unpack_halves(loc[slot, r:r + rc, :])
        mix_lo = mix_lo + _dot(wgt, y_lo)
        mix_hi = mix_hi + _dot(wgt, y_hi)
    grp = lay.group(i * tm)
    gate = gt_ref[pl.ds(grp, 1), :]
    x_lo = x_ref[:, :half] + gate[:, :half] * mix_lo
    x_hi = x_ref[:, half:] + gate[:, half:] * mix_hi
    o_ref[:, :half] = x_lo
    o_ref[:, half:] = x_hi
    ms = (jnp.sum(x_lo * x_lo, axis=-1, keepdims=True) + jnp.sum(x_hi * x_hi, axis=-1, keepdims=True)) / (2 * half)
    inv = lax.rsqrt(ms + EPS)
    nw = nw_ref[...]
    if final:
        u_ref[:, :half] = x_lo * inv * nw[:, :half]
        u_ref[:, half:] = x_hi * inv * nw[:, half:]
    else:
        sh = sh_ref[pl.ds(grp, 1), :]
        sc = sc_ref[pl.ds(grp, 1), :]
        u_ref[:, :half] = (x_lo * inv * nw[:, :half] * (1.0 + sc[:, :half]) + sh[:, :half]).astype(BF16)
        u_ref[:, half:] = (x_hi * inv * nw[:, half:] * (1.0 + sc[:, half:]) + sh[:, half:]).astype(BF16)


def _combine_call(lay, x, route, mods, l, mod_next, gdst, ngt, ys, tm, nl, final):
    t, d = x.shape
    rows = pl.BlockSpec((tm, d), lambda i, *_: (i, 0))
    grid_spec = pltpu.PrefetchScalarGridSpec(
        num_scalar_prefetch=2,
        grid=(t // tm,),
        in_specs=[
            rows,
            pl.BlockSpec((tm, LANES), lambda i, *_: (i, 0)),
            lay.mod_spec(l, 5),
        ] + mod_next.specs + [pl.BlockSpec(memory_space=pl.ANY)],
        out_specs=(rows, rows),
        scratch_shapes=[pltpu.VMEM((2, nl, d // 2), U32), pltpu.SemaphoreType.DMA((2,))],
    )
    return pl.pallas_call(
        functools.partial(_combine_kernel, lay=lay, tm=tm, nl=nl, final=final),
        out_shape=(jax.ShapeDtypeStruct(x.shape, F32), jax.ShapeDtypeStruct(x.shape, F32 if final else BF16)),
        grid_spec=grid_spec,
        compiler_params=_cparams(("arbitrary",)),
        name="moe_combine",
    )(gdst, ngt, x, route, mods, *mod_next.args, ys)


def _moe_layer(lay, x, u2, mods, l, mod_next, final, wr_bf, tri_r, upper, w13, w2, ng, ne):
    t, d = x.shape
    tm = tri_r.shape[0]
    nt = t // tm
    bmg = MOE_BM // SUBLANES
    nl = -(-(MOE_TOP_K * tm + ne * (SUBLANES - 1)) // 256) * 256
    nlg = nl // SUBLANES
    route, counts = _router_call(u2, wr_bf, tri_r, upper, ng, ne)

    c8 = counts[:, 0, :ne].astype(jnp.int32)
    lend = jnp.cumsum(c8, axis=1)
    lstart = lend - c8
    ngt = lend[:, -1].astype(jnp.int32)
    tot = jnp.sum(c8, axis=0)
    padded = (tot + bmg - 1) // bmg * bmg
    gend = jnp.cumsum(padded)
    gbase = (gend - padded)[None, :] + jnp.cumsum(c8, axis=0) - c8
    nb = -(-(MOE_TOP_K * t + nt * ne * (SUBLANES - 1)) // MOE_BM) + ne
    g = jnp.arange(nlg, dtype=jnp.int32)
    e_of_g = jnp.minimum(jnp.sum((g[None, :, None] >= lend[:, None, :]).astype(jnp.int32), axis=-1), ne - 1)
    gdst = g[None, :] + jnp.take_along_axis(gbase - lstart, e_of_g, axis=1)
    gdst = jnp.clip(gdst, 0, nb * bmg - 1).reshape(nt * nlg).astype(jnp.int32)
    block_e = jnp.minimum(jnp.searchsorted(gend, jnp.arange(nb, dtype=jnp.int32) * bmg, side="right"),
                          ne - 1).astype(jnp.int32)
    n_used = (gend[-1:] // bmg).astype(jnp.int32)
    pos = route[:, 2 * MOE_TOP_K:3 * MOE_TOP_K].astype(jnp.int32).reshape(nt, tm, MOE_TOP_K)
    pos_rows = jnp.full((nt, SUBLANES, tm), -1, jnp.int32).at[:, :MOE_TOP_K, :].set(jnp.swapaxes(pos, 1, 2))

    xs = _dispatch_call(u2, pos_rows, gdst, ngt, jnp.zeros((nb * MOE_BM, d // 2), U32), tm, nl)
    ys = _expert_call(xs, block_e, n_used, w13, w2, l)
    return _combine_call(lay, x, route, mods, l, mod_next, gdst, ngt, ys, tm, nl, final)


def _lower_tri(n, strict):
    r = np.arange(n)
    m = (r[None, :] < r[:, None]) if strict else (r[None, :] <= r[:, None])
    return jnp.asarray(m.astype(np.float32), dtype=BF16)


def kernel(x_prompt, x_sample, state_C, state_n, state_m, c, c_ctx, ada_w, ada_b, norm1_w, norm2_w, m_w_in, m_b_gate, m_head_norm_w, m_w_out, f_w_out, cv_w_pw1, cv_b_pw1, cv_w_dw, cv_b_dw, cv_ln_w, cv_ln_b, cv_w_pw2, cv_b_pw2, r_w_group, r_w_expert, e_w13, e_w2, final_norm_w):
    nbp, sp, d = x_prompt.shape
    nbs, ss, _ = x_sample.shape
    assert ss % GRID_W == 0
    lay = _Layout(nbp, sp, nbs, ss, d)
    depth = ada_w.shape[0]
    nh, dh = state_C.shape[3], state_C.shape[4]
    di = nh * dh
    ng = r_w_group.shape[2]
    ne = r_w_expert.shape[2]
    assert ng == MOE_GROUPS and ng + ne <= LANES and 4 * nh <= LANES and MOE_TOP_K == 2

    x = jnp.concatenate([x_prompt.reshape(lay.tp, d), x_sample.reshape(lay.ts, d)], axis=0)
    cv = jnp.zeros((lay.ngp, d), F32).at[0].set(c_ctx).at[1:1 + nbs].set(c)
    mods = _ada_call(cv, ada_w, ada_b)

    tri_l = _lower_tri(MLSTM_L, strict=False)
    tri_r = _lower_tri(lay.row_tile(512), strict=True)
    upper = _lower_tri(LANES, strict=True).T
    row = lambda a: a.reshape(1, -1)
    w_in_t = jnp.swapaxes(m_w_in, 1, 2)
    tn = 1024
    assert di % tn == 0
    nkb = di // tn

    u1 = _modulate_call(lay, x, _Mod(lay, mods, norm1_w[0], 0, 0))
    y = None
    qvos, kts, gpts = [], [], []
    for l in range(depth):
        j, kind = l // N_MIXERS, l % N_MIXERS
        mod2 = _Mod(lay, mods, norm2_w[l], l, 3)
        if kind == 0:
            wg = jnp.zeros((LANES, d), F32).at[:4 * nh].set(w_in_t[j, 4 * di:])
            bg = jnp.zeros((1, LANES), F32).at[0, :4 * nh].set(m_b_gate[j])
            gp = _gates_call(lay, x, _Mod(lay, mods, norm1_w[l], l, 0), wg, bg, tri_l, nh)
            gpt = gp[:, :4 * nh].T
            qvo_blocks = list(range(nkb)) + list(range(2 * nkb, 4 * nkb))
            qvo = _proj_call(lay, u1, w_in_t, j, qvo_blocks, tn, transposed=False)
            kt = _proj_call(lay, u1, w_in_t, j, list(range(nkb, 2 * nkb)), tn, transposed=True)
            hw = row(m_head_norm_w[j])
            hg = jnp.zeros((lay.t, di), BF16)
            hg = _mlstm_call(lay, hg, qvo, kt, gp, gpt, hw, nh, dh, prompt=True)
            n0 = state_n[:, j].reshape(nbs, 2 * nh, dh)
            m0 = jnp.broadcast_to(state_m[:, j].reshape(nbs, 2 * nh, 1), (nbs, 2 * nh, LANES))
            hg = _mlstm_call(lay, hg, qvo, kt, gp, gpt, hw, nh, dh, prompt=False, state=(state_C, j, n0, m0))
            x, u2 = _mm_res_call(lay, hg, m_w_out[j].astype(BF16), x, mods, l, mod2)
            qvos.append(qvo)
            kts.append(kt)
            gpts.append(gpt)
        elif kind == 1:
            wo = f_w_out[j].astype(BF16)
            u2 = jnp.zeros((lay.t, d), BF16)
            x, u2 = _fnet_call(lay, x, u1, u2, mods, l, wo, mod2, prompt=True)
            x, u2 = _fnet_call(lay, x, u1, u2, mods, l, wo, mod2, prompt=False)
        else:
            glu = _glu_call(lay, u1, cv_w_pw1[j].astype(BF16), row(cv_b_pw1[j]))
            x, u2 = _conv_call(lay, glu, cv_w_dw[j], cv_b_dw[j], cv_ln_w[j], cv_ln_b[j], cv_w_pw2[j].astype(BF16),
                               cv_b_pw2[j], x, mods, l, mod2)
        wr = jnp.zeros((d, LANES), F32).at[:, :ng].set(r_w_group[l]).at[:, ng:ng + ne].set(r_w_expert[l])
        final = l + 1 == depth
        mod_next = _Mod(lay, mods, final_norm_w, l, 0) if final else _Mod(lay, mods, norm1_w[l + 1], l + 1, 0)
        x, nxt = _moe_layer(lay, x, u2, mods, l, mod_next, final, wr.astype(BF16), tri_r, upper, e_w13, e_w2, ng, ne)
        if final:
            y = nxt
        else:
            u1 = nxt

    y_prompt = y[:lay.tp].reshape(nbp, sp, d)
    y_sample = y[lay.tp:].reshape(nbs, ss, d)
    new_c, new_n, new_m = _state_call(lay, qvos, kts, gpts, nh, dh)
    return (y_prompt, y_sample, new_c, new_n, new_m)
```

```python
import functools
import math

import numpy as np
import jax
import jax.numpy as jnp
from jax import lax
from jax.experimental import pallas as pl
from jax.experimental.pallas import tpu as pltpu

F32 = jnp.float32
BF16 = jnp.bfloat16
U32 = jnp.uint32
EPS = 1e-6
GRID_W = 64
N_MIXERS = 3
FNET_GROUPS = 4
CONV_WIDTH = 31
MOE_GROUPS = 4
MOE_TOP_K = 2

LANES = 128
SUBLANES = 8
MLSTM_L = 256
MOE_BM = 256
CONV_HALO = 16
VMEM_LIMIT = 56 * 1024 * 1024


def _cparams(sem, vmem=VMEM_LIMIT):
    return pltpu.CompilerParams(dimension_semantics=sem, vmem_limit_bytes=vmem)


def _dot(a, b):
    return jnp.dot(a, b, preferred_element_type=F32)


def _dot_nt(a, b):
    return lax.dot_general(a, b, (((1,), (1,)), ((), ())), preferred_element_type=F32)


def _rms(x, w):
    return x * lax.rsqrt(jnp.mean(x * x, axis=-1, keepdims=True) + EPS) * w


def _modulate(x, w, shift, scale):
    return _rms(x, w) * (1.0 + scale) + shift


def _sigmoid(x):
    return 1.0 / (1.0 + jnp.exp(-x))


def _log_sigmoid(x):
    return jnp.minimum(x, 0.0) - jnp.log(1.0 + jnp.exp(-jnp.abs(x)))


def _split2(x):
    hi = x.astype(BF16)
    return hi, (x - hi.astype(F32)).astype(BF16)


def _split3(x):
    hi = x.astype(BF16)
    r1 = x - hi.astype(F32)
    mid = r1.astype(BF16)
    return hi, mid, (r1 - mid.astype(F32)).astype(BF16)


class _Layout:
    def __init__(self, nbp, sp, nbs, ss, d):
        self.nbp, self.sp, self.nbs, self.ss, self.d = nbp, sp, nbs, ss, d
        self.tp, self.ts = nbp * sp, nbs * ss
        self.t = self.tp + self.ts
        assert self.tp % ss == 0, "latent sequences must start on a block boundary of their own length"
        self.ngp = -(-(1 + nbs) // SUBLANES) * SUBLANES

    def group(self, row0):
        return jnp.where(row0 < self.tp, 0, 1 + (row0 - self.tp) // self.ss)

    def row_tile(self, want):
        tm = math.gcd(math.gcd(self.tp, self.ss), want)
        assert tm % SUBLANES == 0
        return tm

    def mod_spec(self, l, chunk):
        return pl.BlockSpec((None, self.ngp, self.d), lambda *_: (l, 0, chunk))

    def row_spec(self):
        return pl.BlockSpec((1, self.d), lambda *_: (0, 0))


class _Mod:
    def __init__(self, lay, mods, nw, l, c_shift):
        self.args = (nw.reshape(1, -1), mods, mods)
        self.specs = [lay.row_spec(), lay.mod_spec(l, c_shift), lay.mod_spec(l, c_shift + 1)]


def _mod_value(x, nw_ref, sh_ref, sc_ref, grp):
    return _modulate(x, nw_ref[...], sh_ref[pl.ds(grp, 1), :], sc_ref[pl.ds(grp, 1), :])


def _ada_kernel(cv_ref, w_ref, b_ref, o_ref):
    s = cv_ref[...]
    s = s * _sigmoid(s)
    o_ref[...] = _dot(s.astype(BF16), w_ref[...].astype(BF16)) + b_ref[...]


def _ada_call(cv, ada_w, ada_b):
    depth, d, n = ada_w.shape
    ngp = cv.shape[0]
    tn = min(n, 2048)
    return pl.pallas_call(
        _ada_kernel,
        out_shape=jax.ShapeDtypeStruct((depth, ngp, n), F32),
        grid=(depth, n // tn),
        in_specs=[
            pl.BlockSpec((ngp, d), lambda l, j: (0, 0)),
            pl.BlockSpec((None, d, tn), lambda l, j: (l, 0, j)),
            pl.BlockSpec((None, 1, tn), lambda l, j: (l, 0, j)),
        ],
        out_specs=pl.BlockSpec((None, ngp, tn), lambda l, j: (l, 0, j)),
        compiler_params=_cparams(("parallel", "parallel")),
        name="ada_mods",
    )(cv, ada_w, ada_b.reshape(depth, 1, n))


def _prep_kernel(xp_ref, xs_ref, nw_ref, sh_ref, sc_ref, x_ref, u_ref, *, lay, tm):
    i = pl.program_id(0)
    grp = lay.group(i * tm)
    for src, cond in ((xp_ref, i * tm < lay.tp), (xs_ref, i * tm >= lay.tp)):
        @pl.when(cond)
        def _(src=src):
            x = src[...]
            x_ref[...] = x
            u_ref[...] = _mod_value(x, nw_ref, sh_ref, sc_ref, grp).astype(BF16)


def _prep_call(lay, xp, xs, mod):
    d = lay.d
    tm = lay.row_tile(256)
    ntp = lay.tp // tm
    rows = pl.BlockSpec((tm, d), lambda i: (i, 0))
    return pl.pallas_call(
        functools.partial(_prep_kernel, lay=lay, tm=tm),
        out_shape=(jax.ShapeDtypeStruct((lay.t, d), F32), jax.ShapeDtypeStruct((lay.t, d), BF16)),
        grid=(lay.t // tm,),
        in_specs=[
            pl.BlockSpec((tm, d), lambda i: (jnp.minimum(i, ntp - 1), 0)),
            pl.BlockSpec((tm, d), lambda i: (jnp.maximum(i - ntp, 0), 0)),
        ] + mod.specs,
        out_specs=(rows, rows),
        compiler_params=_cparams(("arbitrary",)),
        name="prep_modulate",
    )(xp, xs, *mod.args)


def _gates_kernel(x_ref, nw_ref, sh_ref, sc_ref, wh_ref, wl_ref, b_ref, tri_ref, o_ref, *, lay, nh):
    l = x_ref.shape[0]
    u = _mod_value(x_ref[...], nw_ref, sh_ref, sc_ref, lay.group(pl.program_id(0) * l))
    u_hi, u_lo = _split2(u)
    wh = wh_ref[...]
    g = _dot_nt(u_hi, wh) + _dot_nt(u_lo, wh) + _dot_nt(u_hi, wl_ref[...]) + b_ref[...]
    lane = lax.broadcasted_iota(jnp.int32, g.shape, 1)
    is_f = ((lane >= nh) & (lane < 2 * nh)) | ((lane >= 3 * nh) & (lane < 4 * nh))
    lf = jnp.where(is_f, _log_sigmoid(g), 0.0)
    tri = tri_ref[...]
    hi, mid, lo = _split3(lf)
    prefix = _dot(tri, hi) + _dot(tri, mid) + _dot(tri, lo)
    suffix = jnp.sum(lf, axis=0, keepdims=True) - prefix + lf
    b = jnp.where(lane < 2 * nh, prefix, suffix)
    a = g - pltpu.roll(b, LANES - nh, 1)
    is_a = (lane < nh) | ((lane >= 2 * nh) & (lane < 3 * nh))
    o_ref[...] = jnp.where(is_a, a, b)


def _gates_call(lay, x, mod, wg, bias, tri, nh):
    t, d = x.shape
    l = tri.shape[0]
    wh, wl = _split2(wg)
    return pl.pallas_call(
        functools.partial(_gates_kernel, lay=lay, nh=nh),
        out_shape=jax.ShapeDtypeStruct((t, LANES), F32),
        grid=(t // l,),
        in_specs=[pl.BlockSpec((l, d), lambda i: (i, 0))] + mod.specs + [
            pl.BlockSpec((LANES, d), lambda i: (0, 0)),
            pl.BlockSpec((LANES, d), lambda i: (0, 0)),
            pl.BlockSpec((1, LANES), lambda i: (0, 0)),
            pl.BlockSpec((l, l), lambda i: (0, 0)),
        ],
        out_specs=pl.BlockSpec((l, LANES), lambda i: (i, 0)),
        compiler_params=_cparams(("parallel",)),
        name="mlstm_gates",
    )(x, *mod.args, wh, wl, bias, tri)


def _proj_kernel(u_ref, w_ref, o_ref, w_bf, *, transposed):
    @pl.when(pl.program_id(1) == 0)
    def _():
        w_bf[...] = w_ref[...].astype(BF16)

    if transposed:
        o_ref[...] = _dot_nt(w_bf[...], u_ref[...]).astype(BF16)
    else:
        o_ref[...] = _dot_nt(u_ref[...], w_bf[...]).astype(BF16)


def _proj_call(lay, u, w_in_t, jl, blocks, tn, transposed):
    t, d = u.shape
    tm = lay.row_tile(1024)
    nb = len(blocks)
    first, gap_at, gap = blocks[0], None, 0
    for idx in range(1, nb):
        if blocks[idx] != blocks[idx - 1] + 1:
            assert gap_at is None
            gap_at, gap = idx, blocks[idx] - blocks[idx - 1] - 1
    wblk = (lambda j: first + j) if gap_at is None else (lambda j: first + j + jnp.where(j >= gap_at, gap, 0))
    if transposed:
        out_shape = jax.ShapeDtypeStruct((nb * tn, t), BF16)
        out_spec = pl.BlockSpec((tn, tm), lambda j, i: (j, i))
    else:
        out_shape = jax.ShapeDtypeStruct((t, nb * tn), BF16)
        out_spec = pl.BlockSpec((tm, tn), lambda j, i: (i, j))
    return pl.pallas_call(
        functools.partial(_proj_kernel, transposed=transposed),
        out_shape=out_shape,
        grid=(nb, t // tm),
        in_specs=[
            pl.BlockSpec((tm, d), lambda j, i: (i, 0)),
            pl.BlockSpec((None, tn, d), lambda j, i: (jl, wblk(j), 0)),
        ],
        out_specs=out_spec,
        scratch_shapes=[pltpu.VMEM((tn, d), BF16)],
        compiler_params=_cparams(("parallel", "arbitrary")),
        name="mlstm_proj_t" if transposed else "mlstm_proj",
    )(u, w_in_t)


def _col(tile, c):
    lane = lax.broadcasted_iota(jnp.int32, tile.shape, 1)
    return jnp.sum(jnp.where(lane == c, tile, 0.0), axis=-1, keepdims=True)


def _dir_masks(l):
    r = lax.broadcasted_iota(jnp.int32, (l, l), 0)
    c = lax.broadcasted_iota(jnp.int32, (l, l), 1)
    return c <= r, c >= r


def _head_epilogue(h, hw, o):
    hn = h * lax.rsqrt(jnp.mean(h * h, axis=-1, keepdims=True) + EPS) * hw
    return (hn * _sigmoid(o.astype(F32))).astype(BF16)


def _row_times_kt(w_row, kt):
    hi, lo = _split2(w_row)
    sub = lax.broadcasted_iota(jnp.int32, (SUBLANES, w_row.shape[1]), 0)
    stacked = jnp.where(sub == 0, hi.astype(F32), jnp.where(sub == 1, lo.astype(F32), 0.0)).astype(BF16)
    res = _dot_nt(stacked, kt)
    return res[0:1, :] + res[1:2, :]


def _mlstm_single_kernel(q_ref, kt_ref, v_ref, o_ref, gp_ref, gpt_ref, hw_ref, out_ref, *, nh, scale):
    h = pl.program_id(1)
    l = q_ref.shape[0]
    gp = gp_ref[...]
    qk = _dot(q_ref[...], kt_ref[...])
    masks = _dir_masks(l)
    p = None
    for d in range(2):
        a_r = gpt_ref[pl.ds(2 * nh * d + h, 1), :]
        b_c = _col(gp, 2 * nh * d + nh + h)
        g = jnp.where(masks[d], a_r, -jnp.inf)
        m = jnp.maximum(jnp.max(g, axis=-1, keepdims=True), 0.0)
        s = qk * jnp.exp(g - m) * scale
        den = jnp.sum(s, axis=-1, keepdims=True)
        inv = 1.0 / jnp.maximum(jnp.abs(den), jnp.exp(-(b_c + m)))
        p = s * inv if p is None else p + s * inv
    hh = _dot(p.astype(BF16), v_ref[...])
    out_ref[...] = _head_epilogue(hh, hw_ref[...], o_ref[...])


def _mlstm_multi_kernel(q_ref, kt_ref, v_ref, o_ref, gp_ref, gpt_ref, hw_ref, c0_ref, n0_ref, m0_ref,
                        out_ref, hacc, cst, *, nh, nc, l, scale):
    h = pl.program_id(1)
    masks = _dir_masks(l)
    for d in range(2):
        cst[...] = c0_ref[d]
        n = n0_ref[pl.ds(d * nh + h, 1), :]
        m = m0_ref[pl.ds(d * nh + h, 1), 0:1]
        order = list(range(nc)) if d == 0 else list(range(nc - 1, -1, -1))
        for step, c in enumerate(order):
            r0 = c * l
            b_c = _col(gp_ref[r0:r0 + l, :], 2 * nh * d + nh + h)
            a_r = gpt_ref[pl.ds(2 * nh * d + h, 1), r0:r0 + l]
            q = q_ref[r0:r0 + l, :]
            kt = kt_ref[:, r0:r0 + l]
            v = v_ref[r0:r0 + l, :]
            qk = _dot(q, kt)
            g = jnp.where(masks[d], a_r, -jnp.inf)
            mt = jnp.maximum(jnp.max(g, axis=-1, keepdims=True), m)
            s = qk * jnp.exp(g - mt) * scale
            w_prev = jnp.exp(m - mt) * scale
            qn = jnp.sum(q.astype(F32) * n, axis=-1, keepdims=True)
            den = jnp.sum(s, axis=-1, keepdims=True) + w_prev * qn
            inv = 1.0 / jnp.maximum(jnp.abs(den), jnp.exp(-(b_c + mt)))
            hc = _dot((s * inv).astype(BF16), v) + (w_prev * inv) * _dot(q, cst[...].astype(BF16))
            if d == 0:
                hacc[r0:r0 + l, :] = hc
            else:
                hacc[r0:r0 + l, :] += hc
            if step + 1 < nc:
                m_last = jnp.max(mt, axis=0, keepdims=True)
                b_end = b_c[l - 1:l, :] if d == 0 else b_c[0:1, :]
                decay = jnp.exp(m - m_last)
                w_end = jnp.exp(a_r - m_last)
                cst[...] = decay * cst[...] + _dot((kt.astype(F32) * w_end).astype(BF16), v)
                n = decay * n + _row_times_kt(w_end, kt)
                m = b_end + m_last
    hw = hw_ref[...]
    for c in range(nc):
        r0 = c * l
        out_ref[r0:r0 + l, :] = _head_epilogue(hacc[r0:r0 + l, :], hw, o_ref[r0:r0 + l, :])


def _mlstm_call(lay, hg, qvo, kt, gp, gpt, hw, nh, dh, prompt, state=None):
    nb, s = (lay.nbp, lay.sp) if prompt else (lay.nbs, lay.ss)
    rb0 = 0 if prompt else lay.tp // s
    scale = dh ** -0.5
    common_in = [
        pl.BlockSpec((s, dh), lambda b, h: (rb0 + b, h)),
        pl.BlockSpec((dh, s), lambda b, h: (h, rb0 + b)),
        pl.BlockSpec((s, dh), lambda b, h: (rb0 + b, nh + h)),
        pl.BlockSpec((s, dh), lambda b, h: (rb0 + b, 2 * nh + h)),
        pl.BlockSpec((s, LANES), lambda b, h: (rb0 + b, 0)),
        pl.BlockSpec((4 * nh, s), lambda b, h: (0, rb0 + b)),
        pl.BlockSpec((1, dh), lambda b, h: (0, h)),
    ]
    hg_spec = pl.BlockSpec((s, dh), lambda b, h: (rb0 + b, h))
    any_spec = pl.BlockSpec(memory_space=pl.ANY)
    if prompt:
        assert s == MLSTM_L
        kern = functools.partial(_mlstm_single_kernel, nh=nh, scale=scale)

        def body(hg_any, *refs):
            kern(*refs)

        return pl.pallas_call(
            body,
            out_shape=jax.ShapeDtypeStruct(hg.shape, hg.dtype),
            grid=(nb, nh),
            in_specs=[any_spec] + common_in,
            out_specs=hg_spec,
            input_output_aliases={0: 0},
            compiler_params=_cparams(("parallel", "parallel")),
            name="mlstm_prompt",
        )(hg, qvo, kt, qvo, qvo, gp, gpt, hw)
    state_c, jl, n0, m0 = state
    nc = s // MLSTM_L
    kern = functools.partial(_mlstm_multi_kernel, nh=nh, nc=nc, l=MLSTM_L, scale=scale)

    def body(hg_any, *refs):
        kern(*refs)

    return pl.pallas_call(
        body,
        out_shape=jax.ShapeDtypeStruct(hg.shape, hg.dtype),
        grid=(nb, nh),
        in_specs=[any_spec] + common_in + [
            pl.BlockSpec((None, None, 2, None, dh, dh), lambda b, h: (b, jl, 0, h, 0, 0)),
            pl.BlockSpec((None, 2 * nh, dh), lambda b, h: (b, 0, 0)),
            pl.BlockSpec((None, 2 * nh, LANES), lambda b, h: (b, 0, 0)),
        ],
        out_specs=hg_spec,
        scratch_shapes=[pltpu.VMEM((s, dh), F32), pltpu.VMEM((dh, dh), F32)],
        input_output_aliases={0: 0},
        compiler_params=_cparams(("parallel", "parallel")),
        name="mlstm_latent",
    )(hg, qvo, kt, qvo, qvo, gp, gpt, hw, state_c, n0, m0)


def _state_kernel(*refs, nl, nh, dh):
    ins, (c_ref, n_ref, m_ref) = refs[:3 * nl], refs[3 * nl:]
    lyr = pl.program_id(0)
    for jl in range(nl):
        kt_ref, v_ref, gpt_ref = ins[3 * jl:3 * jl + 3]

        @pl.when(lyr == jl)
        def _(kt_ref=kt_ref, v_ref=v_ref, gpt_ref=gpt_ref):
            l = v_ref.shape[0]
            sub = lax.broadcasted_iota(jnp.int32, m_ref.shape, 0)
            lane = lax.broadcasted_iota(jnp.int32, m_ref.shape, 1)
            m_all = jnp.zeros(m_ref.shape, F32)
            for h in range(nh):
                kt = kt_ref[h * dh:(h + 1) * dh, :]
                ktf = kt.astype(F32)
                v = v_ref[:, h * dh:(h + 1) * dh]
                for d in range(2):
                    a_r = gpt_ref[2 * nh * d + h:2 * nh * d + h + 1, :]
                    b_r = gpt_ref[2 * nh * d + nh + h:2 * nh * d + nh + h + 1, :]
                    m_last = jnp.maximum(jnp.max(a_r, axis=-1, keepdims=True), 0.0)
                    b_end = b_r[:, l - 1:l] if d == 0 else b_r[:, 0:1]
                    w_end = jnp.exp(a_r - m_last)
                    c_ref[d, h] = _dot((ktf * w_end).astype(BF16), v)
                    n_ref[d, h:h + 1, :] = _row_times_kt(w_end, kt)
                    m_all = jnp.where((sub == d) & (lane == h), b_end + m_last, m_all)
            m_ref[...] = m_all


def _state_call(lay, qvos, kts, gpts, nh, dh):
    nl = len(qvos)
    nbp, s = lay.nbp, lay.sp
    di = nh * dh
    assert s == MLSTM_L

    def pick(jl):
        return lambda lyr, b: jnp.where(lyr == jl, b, jnp.where(lyr < jl, 0, nbp - 1))

    in_specs, args = [], []
    for jl in range(nl):
        pb = pick(jl)
        in_specs.append(pl.BlockSpec((di, s), lambda lyr, b, pb=pb: (0, pb(lyr, b))))
        in_specs.append(pl.BlockSpec((s, di), lambda lyr, b, pb=pb: (pb(lyr, b), 1)))
        in_specs.append(pl.BlockSpec((4 * nh, s), lambda lyr, b, pb=pb: (0, pb(lyr, b))))
        args += [kts[jl], qvos[jl], gpts[jl]]
    return pl.pallas_call(
        functools.partial(_state_kernel, nl=nl, nh=nh, dh=dh),
        out_shape=(
            jax.ShapeDtypeStruct((nbp, nl, 2, nh, dh, dh), F32),
            jax.ShapeDtypeStruct((nbp, nl, 2, nh, dh), F32),
            jax.ShapeDtypeStruct((nbp, nl, 2, nh), F32),
        ),
        grid=(nl, nbp),
        in_specs=in_specs,
        out_specs=(
            pl.BlockSpec((None, None, 2, nh, dh, dh), lambda lyr, b: (b, lyr, 0, 0, 0, 0)),
            pl.BlockSpec((None, None, 2, nh, dh), lambda lyr, b: (b, lyr, 0, 0, 0)),
            pl.BlockSpec((None, None, 2, nh), lambda lyr, b: (b, lyr, 0, 0)),
        ),
        compiler_params=_cparams(("arbitrary", "arbitrary")),
        name="mlstm_prompt_state",
    )(*args)


def _mm_res_kernel(a_ref, w_ref, x_ref, g_ref, nw_ref, sh_ref, sc_ref, o_ref, u_ref, *, lay, tm):
    grp = lay.group(pl.program_id(0) * tm)
    gate = g_ref[pl.ds(grp, 1), :]
    w = w_ref[...]
    rc = min(tm, 256)
    for r in range(0, tm, rc):
        xn = x_ref[r:r + rc, :] + gate * _dot(a_ref[r:r + rc, :], w)
        o_ref[r:r + rc, :] = xn
        u_ref[r:r + rc, :] = _mod_value(xn, nw_ref, sh_ref, sc_ref, grp).astype(BF16)


def _mm_res_call(lay, a, w_bf, x, mods, l, mod2):
    t, kdim = a.shape
    d = x.shape[1]
    tm = lay.row_tile(512)
    return pl.pallas_call(
        functools.partial(_mm_res_kernel, lay=lay, tm=tm),
        out_shape=(jax.ShapeDtypeStruct(x.shape, F32), jax.ShapeDtypeStruct(x.shape, BF16)),
        grid=(t // tm,),
        in_specs=[
            pl.BlockSpec((tm, kdim), lambda i: (i, 0)),
            pl.BlockSpec((kdim, d), lambda i: (0, 0)),
            pl.BlockSpec((tm, d), lambda i: (i, 0)),
            lay.mod_spec(l, 2),
        ] + mod2.specs,
        out_specs=(pl.BlockSpec((tm, d), lambda i: (i, 0)), pl.BlockSpec((tm, d), lambda i: (i, 0))),
        compiler_params=_cparams(("parallel",)),
        name="mm_residual",
    )(a, w_bf, x, mods, *mod2.args)


def _fnet_kernel(x_ref, u_ref, u2_any, gt_ref, wc_ref, ds_ref, wo_ref, nw_ref, sh_ref, sc_ref, o_ref, u2_ref,
                 ab_scr, *, lay, row_base, groups, norm):
    s, d = x_ref.shape
    cg = d // groups
    grp = lay.group(row_base + pl.program_id(0) * s)
    wc = wc_ref[...]
    rc = min(s, 256)
    for g in range(groups):
        for r in range(0, s, rc):
            ab = _dot(u_ref[r:r + rc, g * cg:(g + 1) * cg], wc)
            ab_scr[r:r + rc, g * cg:(g + 1) * cg] = ab[:, :cg].astype(BF16)
            ab_scr[s + r:s + r + rc, g * cg:(g + 1) * cg] = ab[:, cg:].astype(BF16)
    gate = gt_ref[pl.ds(grp, 1), :]
    wo = wo_ref[...]
    for r in range(0, s, rc):
        y = _dot(ds_ref[r:r + rc, :], ab_scr[...]) * norm
        xn = x_ref[r:r + rc, :] + gate * _dot(y.astype(BF16), wo)
        o_ref[r:r + rc, :] = xn
        u2_ref[r:r + rc, :] = _mod_value(xn, nw_ref, sh_ref, sc_ref, grp).astype(BF16)


def _dft_mats(s, cg):
    kc = np.arange(cg)
    ang_c = 2.0 * np.pi * np.outer(kc, kc) / cg
    wc = np.concatenate([np.cos(ang_c), np.sin(ang_c)], axis=1)
    ks = np.arange(s)
    ang_s = 2.0 * np.pi * np.outer(ks, ks) / s
    ds = np.concatenate([np.cos(ang_s), -np.sin(ang_s)], axis=1)
    return jnp.asarray(wc, dtype=BF16), jnp.asarray(ds, dtype=BF16)


def _fnet_call(lay, x, u1, u2, mods, l, wo_bf, mod2, prompt):
    nb, s = (lay.nbp, lay.sp) if prompt else (lay.nbs, lay.ss)
    rb0 = 0 if prompt else lay.tp // s
    d = lay.d
    cg = d // FNET_GROUPS
    wc, ds = _dft_mats(s, cg)
    kern = functools.partial(_fnet_kernel, lay=lay, row_base=rb0 * s, groups=FNET_GROUPS,
                             norm=1.0 / math.sqrt(s * cg))
    blk = pl.BlockSpec((s, d), lambda b: (rb0 + b, 0))
    return pl.pallas_call(
        kern,
        out_shape=(jax.ShapeDtypeStruct(x.shape, F32), jax.ShapeDtypeStruct(u2.shape, BF16)),
        grid=(nb,),
        in_specs=[
            blk,
            blk,
            pl.BlockSpec(memory_space=pl.ANY),
            lay.mod_spec(l, 2),
            pl.BlockSpec((cg, 2 * cg), lambda b: (0, 0)),
            pl.BlockSpec((s, 2 * s), lambda b: (0, 0)),
            pl.BlockSpec((d, d), lambda b: (0, 0)),
        ] + mod2.specs,
        out_specs=(blk, blk),
        scratch_shapes=[pltpu.VMEM((2 * s, d), BF16)],
        input_output_aliases={0: 0, 2: 1},
        compiler_params=_cparams(("parallel",)),
        name="fnet_prompt" if prompt else "fnet_latent",
    )(x, u1, u2, mods, wc, ds, wo_bf, *mod2.args)


def _glu_kernel(u_ref, wa_ref, wg_ref, ba_ref, bg_ref, o_ref):
    u = u_ref[...]
    a = _dot(u, wa_ref[...]) + ba_ref[...]
    g = _dot(u, wg_ref[...]) + bg_ref[...]
    o_ref[...] = a * _sigmoid(g)


def _glu_call(lay, u, w_bf, bias):
    t, d = u.shape
    cd = w_bf.shape[1] // 2
    tm = lay.row_tile(1024)
    tn = 512
    nj = cd // tn
    return pl.pallas_call(
        _glu_kernel,
        out_shape=jax.ShapeDtypeStruct((t, cd), F32),
        grid=(t // tm, nj),
        in_specs=[
            pl.BlockSpec((tm, d), lambda i, j: (i, 0)),
            pl.BlockSpec((d, tn), lambda i, j: (0, j)),
            pl.BlockSpec((d, tn), lambda i, j: (0, nj + j)),
            pl.BlockSpec((1, tn), lambda i, j: (0, j)),
            pl.BlockSpec((1, tn), lambda i, j: (0, nj + j)),
        ],
        out_specs=pl.BlockSpec((tm, tn), lambda i, j: (i, j)),
        compiler_params=_cparams(("parallel", "parallel")),
        name="conv_glu",
    )(u, w_bf, w_bf, bias, bias)


def _conv_kernel(c_ref, p_ref, n_ref, wd_ref, bd_ref, lw_ref, lb_ref, w2_ref, b2_ref, x_ref, gt_ref,
                 nw_ref, sh_ref, sc_ref, o_ref, u2_ref, pad, act, *, lay, rb, width):
    i = pl.program_id(0)
    row0 = i * rb
    grp = lay.group(row0)
    seq = jnp.where(row0 < lay.tp, lay.sp, lay.ss)
    pos = jnp.where(row0 < lay.tp, row0 % lay.sp, (row0 - lay.tp) % lay.ss)
    has_prev = (pos != 0).astype(F32)
    has_next = (pos + rb != seq).astype(F32)
    hl = CONV_HALO
    half = width // 2
    cd = c_ref.shape[1]
    span = pad.shape[1]
    pad[0, 0:hl, :] = p_ref[...] * has_prev
    pad[0, hl:hl + rb, :] = c_ref[...]
    pad[0, hl + rb:hl + rb + hl, :] = n_ref[...] * has_next
    for s in range(1, SUBLANES):
        pad[s, 0:span - SUBLANES, :] = pad[0, s:s + span - SUBLANES, :]
    sub = 4 * SUBLANES
    bd = bd_ref[...]
    lw = lw_ref[...]
    lb = lb_ref[...]

    def block(blk, carry):
        r0 = pl.multiple_of(blk * sub, sub)
        acc = jnp.zeros((sub // SUBLANES, SUBLANES, cd), F32) + bd
        for k in range(width):
            q, s = divmod(hl - half + k, SUBLANES)
            win = pad[s, pl.ds(r0 + q * SUBLANES, sub), :]
            acc = acc + win.reshape(sub // SUBLANES, SUBLANES, cd) * wd_ref[k]
        acc = acc.reshape(sub, cd)
        mu = jnp.mean(acc, axis=-1, keepdims=True)
        cen = acc - mu
        var = jnp.mean(cen * cen, axis=-1, keepdims=True)
        y = cen * lax.rsqrt(var + EPS) * lw + lb
        act[pl.ds(r0, sub), :] = (y * _sigmoid(y)).astype(BF16)
        return carry

    lax.fori_loop(0, rb // sub, block, 0)
    xn = x_ref[...] + gt_ref[pl.ds(grp, 1), :] * (_dot(act[...], w2_ref[...]) + b2_ref[...])
    o_ref[...] = xn
    u2_ref[...] = _mod_value(xn, nw_ref, sh_ref, sc_ref, grp).astype(BF16)


def _conv_call(lay, glu, wd, bd, lw, lb, w2_bf, b2, x, mods, l, mod2):
    t, cd = glu.shape
    d = x.shape[1]
    rb = lay.row_tile(256)
    hl = CONV_HALO
    assert CONV_WIDTH // 2 <= hl and rb % hl == 0
    nhb = t // hl
    per = rb // hl
    wd_p = jnp.broadcast_to(wd[:, None, :], (CONV_WIDTH, SUBLANES, cd))
    row = lambda a: a.reshape(1, -1)
    rows = pl.BlockSpec((rb, d), lambda i: (i, 0))
    return pl.pallas_call(
        functools.partial(_conv_kernel, lay=lay, rb=rb, width=CONV_WIDTH),
        out_shape=(jax.ShapeDtypeStruct(x.shape, F32), jax.ShapeDtypeStruct(x.shape, BF16)),
        grid=(t // rb,),
        in_specs=[
            pl.BlockSpec((rb, cd), lambda i: (i, 0)),
            pl.BlockSpec((hl, cd), lambda i: (jnp.maximum(i * per - 1, 0), 0)),
            pl.BlockSpec((hl, cd), lambda i: (jnp.minimum((i + 1) * per, nhb - 1), 0)),
            pl.BlockSpec(wd_p.shape, lambda i: (0, 0, 0)),
            pl.BlockSpec((1, cd), lambda i: (0, 0)),
            pl.BlockSpec((1, cd), lambda i: (0, 0)),
            pl.BlockSpec((1, cd), lambda i: (0, 0)),
            pl.BlockSpec((cd, d), lambda i: (0, 0)),
            pl.BlockSpec((1, d), lambda i: (0, 0)),
            rows,
            lay.mod_spec(l, 2),
        ] + mod2.specs,
        out_specs=(rows, rows),
        scratch_shapes=[pltpu.VMEM((SUBLANES, rb + 2 * hl, cd), F32), pltpu.VMEM((rb, cd), BF16)],
        compiler_params=_cparams(("parallel",)),
        name="conv_dw_ln_pw2",
    )(glu, glu, glu, wd_p, row(bd), row(lw), row(lb), w2_bf, row(b2), x, mods, *mod2.args)


def _router_kernel(u_ref, wr_ref, tri_ref, upper_ref, o_ref, cnt_ref, *, ng, ne):
    logits = _dot(u_ref[...], wr_ref[...])
    lane = lax.broadcasted_iota(jnp.int32, logits.shape, 1)
    big = jnp.int32(4 * LANES)
    neg = -jnp.inf
    epg = ne // ng

    gl = jnp.where(lane < ng, logits, neg)
    gmax = jnp.max(gl, axis=-1, keepdims=True)
    gidx = jnp.min(jnp.where(gl == gmax, lane, big), axis=-1, keepdims=True)
    g_p = 1.0 / jnp.sum(jnp.where(lane < ng, jnp.exp(logits - gmax), 0.0), axis=-1, keepdims=True)

    lo = ng + gidx * epg
    el = jnp.where((lane >= lo) & (lane < lo + epg), logits, neg)
    v1 = jnp.max(el, axis=-1, keepdims=True)
    i1 = jnp.min(jnp.where(el == v1, lane, big), axis=-1, keepdims=True)
    el2 = jnp.where(lane == i1, neg, el)
    v2 = jnp.max(el2, axis=-1, keepdims=True)
    i2 = jnp.min(jnp.where(el2 == v2, lane, big), axis=-1, keepdims=True)
    e1 = i1 - ng
    e2 = i2 - ng
    tt = jnp.exp(v2 - v1)
    p1 = 1.0 / (1.0 + tt)
    gate1 = p1 * g_p
    gate2 = (tt * p1) * g_p

    oh1 = lane == e1
    oh2 = lane == e2
    oh = jnp.where(oh1 | oh2, 1.0, 0.0)
    groups = jnp.floor((jnp.sum(oh, axis=0, keepdims=True) + (SUBLANES - 1)) * (1.0 / SUBLANES))
    groups8 = jnp.broadcast_to(groups, (SUBLANES, LANES))
    start = SUBLANES * _dot(groups8.astype(BF16), upper_ref[...])[0:1, :]
    prefix = _dot(tri_ref[...], oh.astype(BF16)) + start
    pos1 = jnp.sum(jnp.where(oh1, prefix, 0.0), axis=-1, keepdims=True)
    pos2 = jnp.sum(jnp.where(oh2, prefix, 0.0), axis=-1, keepdims=True)
    cnt_ref[...] = groups8

    out = jnp.where(lane == 0, e1.astype(F32), 0.0)
    out = jnp.where(lane == 1, e2.astype(F32), out)
    out = jnp.where(lane == 2, gate1, out)
    out = jnp.where(lane == 3, gate2, out)
    out = jnp.where(lane == 4, pos1, out)
    out = jnp.where(lane == 5, pos2, out)
    o_ref[...] = out


def _router_call(u, wr_bf, tri, upper, ng, ne):
    t, d = u.shape
    tm = tri.shape[0]
    nt = t // tm
    return pl.pallas_call(
        functools.partial(_router_kernel, ng=ng, ne=ne),
        out_shape=(jax.ShapeDtypeStruct((t, LANES), F32), jax.ShapeDtypeStruct((nt, SUBLANES, LANES), F32)),
        grid=(nt,),
        in_specs=[
            pl.BlockSpec((tm, d), lambda i: (i, 0)),
            pl.BlockSpec((d, LANES), lambda i: (0, 0)),
            pl.BlockSpec((tm, tm), lambda i: (0, 0)),
            pl.BlockSpec((LANES, LANES), lambda i: (0, 0)),
        ],
        out_specs=(pl.BlockSpec((tm, LANES), lambda i: (i, 0)),
                   pl.BlockSpec((None, SUBLANES, LANES), lambda i: (i, 0, 0))),
        compiler_params=_cparams(("parallel",)),
        name="moe_router",
    )(u, wr_bf, tri, upper)


def _pack_halves(lo, hi):
    lo_bits = lax.shift_right_logical(pltpu.bitcast(lo, U32), jnp.uint32(16))
    hi_bits = pltpu.bitcast(hi, U32) & jnp.uint32(0xFFFF0000)
    return hi_bits | lo_bits


def _unpack_halves(w):
    lo = pltpu.bitcast(lax.shift_left(w, jnp.uint32(16)), F32)
    hi = pltpu.bitcast(w & jnp.uint32(0xFFFF0000), F32)
    return lo.astype(BF16), hi.astype(BF16)


def _round_bf16(x):
    return x.astype(BF16).astype(F32)


def _group_copy(src, src_g, dst, dst_g, sem):
    g8 = lambda g: pl.ds(pl.multiple_of(g * SUBLANES, SUBLANES), SUBLANES)
    return pltpu.make_async_copy(src.at[g8(src_g), :], dst.at[g8(dst_g), :], sem)


def _for_groups(n, fn):
    def body(g, c):
        fn(g)
        return c
    lax.fori_loop(0, n, body, 0)


def _dispatch_kernel(gdst_ref, ngt_ref, pad0_ref, npad_ref, u_ref, pos_ref, xs_out, loc, zeros, sem, zsem,
                     *, tm, nl, ne):
    i = pl.program_id(0)
    nt = pl.num_programs(0)
    slot = i % 2
    nlg = nl // SUBLANES

    def copy(step, s, g):
        return _group_copy(loc.at[s], g, xs_out, gdst_ref[step * nlg + g], sem.at[s])

    def zero_copy(e, g):
        return _group_copy(zeros, 0, xs_out, pad0_ref[e] + g, zsem)

    @pl.when(i == 0)
    def _():
        zeros[...] = jnp.zeros(zeros.shape, zeros.dtype)
        for e in range(ne):
            _for_groups(npad_ref[e], lambda g, e=e: zero_copy(e, g).start())

    @pl.when(i >= 2)
    def _():
        _for_groups(ngt_ref[i - 2], lambda g: copy(i - 2, slot, g).wait())

    pos1 = pos_ref[0:1, :]
    pos2 = pos_ref[1:2, :]
    half = u_ref.shape[1] // 2
    u = u_ref[...]
    rc = 256
    for r in range(0, nl, rc):
        p = lax.broadcasted_iota(jnp.int32, (rc, tm), 0) + r
        onehot = jnp.where((p == pos1) | (p == pos2), 1.0, 0.0).astype(BF16)
        rows = _dot(onehot, u)
        loc[slot, r:r + rc, :] = _pack_halves(rows[:, :half], rows[:, half:])
    _for_groups(ngt_ref[i], lambda g: copy(i, slot, g).start())

    @pl.when(i == nt - 1)
    def _():
        @pl.when(i >= 1)
        def _():
            _for_groups(ngt_ref[i - 1], lambda g: copy(i - 1, 1 - slot, g).wait())

        _for_groups(ngt_ref[i], lambda g: copy(i, slot, g).wait())
        for e in range(ne):
            _for_groups(npad_ref[e], lambda g, e=e: zero_copy(e, g).wait())


def _dispatch_call(u, pos_rows, gdst, ngt, pad0, npad, nrows, tm, nl):
    t, d = u.shape
    ne = npad.shape[0]
    grid_spec = pltpu.PrefetchScalarGridSpec(
        num_scalar_prefetch=4,
        grid=(t // tm,),
        in_specs=[
            pl.BlockSpec((tm, d), lambda i, *_: (i, 0)),
            pl.BlockSpec((None, SUBLANES, tm), lambda i, *_: (i, 0, 0)),
        ],
        out_specs=pl.BlockSpec(memory_space=pl.ANY),
        scratch_shapes=[pltpu.VMEM((2, nl, d // 2), U32), pltpu.VMEM((SUBLANES, d // 2), U32),
                        pltpu.SemaphoreType.DMA((2,)), pltpu.SemaphoreType.DMA(())],
    )
    return pl.pallas_call(
        functools.partial(_dispatch_kernel, tm=tm, nl=nl, ne=ne),
        out_shape=jax.ShapeDtypeStruct((nrows, d // 2), U32),
        grid_spec=grid_spec,
        compiler_params=_cparams(("arbitrary",)),
        name="moe_dispatch",
    )(gdst, ngt, pad0, npad, u, pos_rows)


def _expert_kernel(be_ref, nu_ref, first_ref, slot_ref, next_ref, x_ref, w13_hbm, w2_hbm, o_ref,
                   w13_f, w2_f, w13_bf, w2_bf, sem13, sem2, *, hid, l):
    b = pl.program_id(0)

    def fetch(e, s):
        return (pltpu.make_async_copy(w13_hbm.at[l, e], w13_f.at[s], sem13.at[s]),
                pltpu.make_async_copy(w2_hbm.at[l, e], w2_f.at[s], sem2.at[s]))

    @pl.when(b < nu_ref[0])
    def _():
        @pl.when(first_ref[b] == 1)
        def _():
            s = slot_ref[b]

            @pl.when(b == 0)
            def _():
                for c in fetch(be_ref[b], s):
                    c.start()

            for c in fetch(be_ref[b], s):
                c.wait()

            @pl.when(next_ref[b] >= 0)
            def _():
                for c in fetch(next_ref[b], 1 - s):
                    c.start()

            w13_bf[...] = w13_f[s].astype(BF16)
            w2_bf[...] = w2_f[s].astype(BF16)

        x_lo, x_hi = _unpack_halves(x_ref[...])
        half = x_lo.shape[1]
        hb = _dot(x_lo, w13_bf[:half, :]) + _dot(x_hi, w13_bf[half:, :])
        a = hb[:, :hid]
        act = (a * _sigmoid(a)) * hb[:, hid:]
        y = _round_bf16(_dot(act.astype(BF16), w2_bf[...]))
        o_ref[...] = _pack_halves(y[:, :half], y[:, half:])

    @pl.when(b >= nu_ref[0])
    def _():
        o_ref[...] = jnp.zeros(o_ref.shape, o_ref.dtype)


def _expert_call(xs, block_e, n_used, w13, w2, l):
    r, half = xs.shape
    d = 2 * half
    nb = r // MOE_BM
    hid = w2.shape[2]
    ne = w13.shape[1]
    idx = jnp.arange(nb, dtype=jnp.int32)
    used = idx < n_used[0]
    first = used & ((idx == 0) | (block_e != jnp.roll(block_e, 1)))
    slot = (jnp.cumsum(first.astype(jnp.int32)) - 1) % 2
    eid = jnp.arange(ne, dtype=jnp.int32)
    owns = jnp.sum(jnp.where(used[:, None] & (block_e[:, None] == eid[None, :]), 1, 0), axis=0) > 0
    later = owns[None, :] & (eid[None, :] > eid[:, None])
    next_e = jnp.min(jnp.where(later, eid[None, :], ne), axis=1)
    next_e = jnp.where(next_e < ne, next_e, -1)
    nxt = jnp.sum(jnp.where(block_e[:, None] == eid[None, :], next_e[None, :], 0), axis=1)
    clamp = lambda b, nu: jnp.minimum(b, jnp.maximum(nu[0], 1) - 1)
    grid_spec = pltpu.PrefetchScalarGridSpec(
        num_scalar_prefetch=5,
        grid=(nb,),
        in_specs=[
            pl.BlockSpec((MOE_BM, half), lambda b, be, nu, *_: (clamp(b, nu), 0)),
            pl.BlockSpec(memory_space=pl.ANY),
            pl.BlockSpec(memory_space=pl.ANY),
        ],
        out_specs=pl.BlockSpec((MOE_BM, half), lambda b, *_: (b, 0)),
        scratch_shapes=[
            pltpu.VMEM((2, d, 2 * hid), F32), pltpu.VMEM((2, hid, d), F32),
            pltpu.VMEM((d, 2 * hid), BF16), pltpu.VMEM((hid, d), BF16),
            pltpu.SemaphoreType.DMA((2,)), pltpu.SemaphoreType.DMA((2,)),
        ],
    )
    return pl.pallas_call(
        functools.partial(_expert_kernel, hid=hid, l=l),
        out_shape=jax.ShapeDtypeStruct((r, half), U32),
        grid_spec=grid_spec,
        compiler_params=_cparams(("arbitrary",)),
        name="moe_experts",
    )(block_e, n_used, first.astype(jnp.int32), slot.astype(jnp.int32), nxt.astype(jnp.int32), xs, w13, w2)


def _combine_kernel(gdst_ref, ngt_ref, x_ref, rt_ref, gt_ref, nw_ref, sh_ref, sc_ref, ys_ref, out_a, out_b,
                    loc, sem, *, lay, tm, nl, final):
    i = pl.program_id(0)
    nt = pl.num_programs(0)
    slot = i % 2
    nlg = nl // SUBLANES

    def copy(step, s, g):
        return _group_copy(ys_ref, gdst_ref[step * nlg + g], loc.at[s], g, sem.at[s])

    @pl.when(i == 0)
    def _():
        loc[...] = jnp.zeros(loc.shape, loc.dtype)
        _for_groups(ngt_ref[0], lambda g: copy(0, 0, g).start())

    _for_groups(ngt_ref[i], lambda g: copy(i, slot, g).wait())

    @pl.when(i + 1 < nt)
    def _():
        _for_groups(ngt_ref[i + 1], lambda g: copy(i + 1, 1 - slot, g).start())

    rt = rt_ref[...]
    gate1, gate2 = rt[:, 2:3], rt[:, 3:4]
    pos1, pos2 = rt[:, 4:5].astype(jnp.int32), rt[:, 5:6].astype(jnp.int32)
    half = loc.shape[2]
    rc = 256
    mix_lo = jnp.zeros((tm, half), F32)
    mix_hi = jnp.zeros((tm, half), F32)
    for r in range(0, nl, rc):
        p = lax.broadcasted_iota(jnp.int32, (tm, rc), 1) + r
        wgt = (jnp.where(p == pos1, gate1, 0.0) + jnp.where(p == pos2, gate2, 0.0)).astype(BF16)
        y_lo, y_hi = _unpack_halves(loc[slot, r:r + rc, :])
        mix_lo = mix_lo + _dot(wgt, y_lo)
        mix_hi = mix_hi + _dot(wgt, y_hi)
    grp = lay.group(i * tm)
    gate = gt_ref[pl.ds(grp, 1), :]
    x_lo = x_ref[:, :half] + gate[:, :half] * mix_lo
    x_hi = x_ref[:, half:] + gate[:, half:] * mix_hi
    ms = (jnp.sum(x_lo * x_lo, axis=-1, keepdims=True) + jnp.sum(x_hi * x_hi, axis=-1, keepdims=True)) / (2 * half)
    inv = lax.rsqrt(ms + EPS)
    nw = nw_ref[...]
    if final:
        y_lo = x_lo * inv * nw[:, :half]
        y_hi = x_hi * inv * nw[:, half:]
        for ref, cond in ((out_a, i * tm < lay.tp), (out_b, i * tm >= lay.tp)):
            @pl.when(cond)
            def _(ref=ref):
                ref[:, :half] = y_lo
                ref[:, half:] = y_hi
    else:
        out_a[:, :half] = x_lo
        out_a[:, half:] = x_hi
        sh = sh_ref[pl.ds(grp, 1), :]
        sc = sc_ref[pl.ds(grp, 1), :]
        out_b[:, :half] = (x_lo * inv * nw[:, :half] * (1.0 + sc[:, :half]) + sh[:, :half]).astype(BF16)
        out_b[:, half:] = (x_hi * inv * nw[:, half:] * (1.0 + sc[:, half:]) + sh[:, half:]).astype(BF16)


def _combine_call(lay, x, route, mods, l, mod_next, gdst, ngt, ys, tm, nl, final):
    t, d = x.shape
    rows = pl.BlockSpec((tm, d), lambda i, *_: (i, 0))
    if final:
        ntp = lay.tp // tm
        out_specs = (pl.BlockSpec((tm, d), lambda i, *_: (jnp.minimum(i, ntp - 1), 0)),
                     pl.BlockSpec((tm, d), lambda i, *_: (jnp.maximum(i - ntp, 0), 0)))
        out_shape = (jax.ShapeDtypeStruct((lay.tp, d), F32), jax.ShapeDtypeStruct((lay.ts, d), F32))
    else:
        out_specs = (rows, rows)
        out_shape = (jax.ShapeDtypeStruct(x.shape, F32), jax.ShapeDtypeStruct(x.shape, BF16))
    grid_spec = pltpu.PrefetchScalarGridSpec(
        num_scalar_prefetch=2,
        grid=(t // tm,),
        in_specs=[
            rows,
            pl.BlockSpec((tm, LANES), lambda i, *_: (i, 0)),
            lay.mod_spec(l, 5),
        ] + mod_next.specs + [pl.BlockSpec(memory_space=pl.ANY)],
        out_specs=out_specs,
        scratch_shapes=[pltpu.VMEM((2, nl, d // 2), U32), pltpu.SemaphoreType.DMA((2,))],
    )
    return pl.pallas_call(
        functools.partial(_combine_kernel, lay=lay, tm=tm, nl=nl, final=final),
        out_shape=out_shape,
        grid_spec=grid_spec,
        compiler_params=_cparams(("arbitrary",)),
        name="moe_combine_final" if final else "moe_combine",
    )(gdst, ngt, x, route, mods, *mod_next.args, ys)


def _moe_layer(lay, x, u2, mods, l, mod_next, final, wr_bf, tri_r, upper, w13, w2, ng, ne):
    t, d = x.shape
    tm = tri_r.shape[0]
    nt = t // tm
    bmg = MOE_BM // SUBLANES
    nl = -(-(MOE_TOP_K * tm + ne * (SUBLANES - 1)) // 256) * 256
    nlg = nl // SUBLANES
    route, counts = _router_call(u2, wr_bf, tri_r, upper, ng, ne)

    c8 = counts[:, 0, :ne].astype(jnp.int32)
    lend = jnp.cumsum(c8, axis=1)
    lstart = lend - c8
    ngt = lend[:, -1].astype(jnp.int32)
    tot = jnp.sum(c8, axis=0)
    padded = (tot + bmg - 1) // bmg * bmg
    gend = jnp.cumsum(padded)
    gbase = (gend - padded)[None, :] + jnp.cumsum(c8, axis=0) - c8
    nb = -(-(MOE_TOP_K * t + nt * ne * (SUBLANES - 1)) // MOE_BM) + ne
    g = jnp.arange(nlg, dtype=jnp.int32)[None, :, None]
    owner = (g >= lstart[:, None, :]) & (g < lend[:, None, :])
    gdst = g[:, :, 0] + jnp.sum(jnp.where(owner, (gbase - lstart)[:, None, :], 0), axis=-1)
    gdst = gdst.reshape(nt * nlg).astype(jnp.int32)
    blk0 = jnp.arange(nb, dtype=jnp.int32)[:, None] * bmg
    block_e = jnp.minimum(jnp.sum((blk0 >= gend[None, :]).astype(jnp.int32), axis=-1), ne - 1)
    n_used = (gend[-1:] // bmg).astype(jnp.int32)
    pos = route[:, 2 * MOE_TOP_K:3 * MOE_TOP_K].astype(jnp.int32).reshape(nt, tm, MOE_TOP_K)
    pos_rows = jnp.full((nt, SUBLANES, tm), -1, jnp.int32).at[:, :MOE_TOP_K, :].set(jnp.swapaxes(pos, 1, 2))

    pad0 = jnp.concatenate([gend - padded + tot, gend[-1:]]).astype(jnp.int32)
    npad = jnp.concatenate([padded - tot, nb * bmg - gend[-1:]]).astype(jnp.int32)
    xs = _dispatch_call(u2, pos_rows, gdst, ngt, pad0, npad, nb * MOE_BM, tm, nl)
    ys = _expert_call(xs, block_e, n_used, w13, w2, l)
    return _combine_call(lay, x, route, mods, l, mod_next, gdst, ngt, ys, tm, nl, final)


def _lower_tri(n, strict):
    r = np.arange(n)
    m = (r[None, :] < r[:, None]) if strict else (r[None, :] <= r[:, None])
    return jnp.asarray(m.astype(np.float32), dtype=BF16)


def kernel(x_prompt, x_sample, state_C, state_n, state_m, c, c_ctx, ada_w, ada_b, norm1_w, norm2_w, m_w_in, m_b_gate, m_head_norm_w, m_w_out, f_w_out, cv_w_pw1, cv_b_pw1, cv_w_dw, cv_b_dw, cv_ln_w, cv_ln_b, cv_w_pw2, cv_b_pw2, r_w_group, r_w_expert, e_w13, e_w2, final_norm_w):
    nbp, sp, d = x_prompt.shape
    nbs, ss, _ = x_sample.shape
    assert ss % GRID_W == 0
    lay = _Layout(nbp, sp, nbs, ss, d)
    depth = ada_w.shape[0]
    nh, dh = state_C.shape[3], state_C.shape[4]
    di = nh * dh
    ng = r_w_group.shape[2]
    ne = r_w_expert.shape[2]
    assert ng == MOE_GROUPS and ng + ne <= LANES and 4 * nh <= LANES and MOE_TOP_K == 2

    cv = jnp.zeros((lay.ngp, d), F32).at[0].set(c_ctx).at[1:1 + nbs].set(c)
    mods = _ada_call(cv, ada_w, ada_b)

    tri_l = _lower_tri(MLSTM_L, strict=False)
    tri_r = _lower_tri(lay.row_tile(512), strict=True)
    upper = _lower_tri(LANES, strict=True).T
    row = lambda a: a.reshape(1, -1)
    w_in_t = jnp.swapaxes(m_w_in, 1, 2)
    tn = 1024
    assert di % tn == 0
    nkb = di // tn

    x, u1 = _prep_call(lay, x_prompt.reshape(lay.tp, d), x_sample.reshape(lay.ts, d),
                       _Mod(lay, mods, norm1_w[0], 0, 0))
    y = None
    qvos, kts, gpts = [], [], []
    for l in range(depth):
        j, kind = l // N_MIXERS, l % N_MIXERS
        mod2 = _Mod(lay, mods, norm2_w[l], l, 3)
        if kind == 0:
            wg = jnp.zeros((LANES, d), F32).at[:4 * nh].set(w_in_t[j, 4 * di:])
            bg = jnp.zeros((1, LANES), F32).at[0, :4 * nh].set(m_b_gate[j])
            gp = _gates_call(lay, x, _Mod(lay, mods, norm1_w[l], l, 0), wg, bg, tri_l, nh)
            gpt = gp[:, :4 * nh].T
            qvo_blocks = list(range(nkb)) + list(range(2 * nkb, 4 * nkb))
            qvo = _proj_call(lay, u1, w_in_t, j, qvo_blocks, tn, transposed=False)
            kt = _proj_call(lay, u1, w_in_t, j, list(range(nkb, 2 * nkb)), tn, transposed=True)
            hw = row(m_head_norm_w[j])
            hg = jnp.zeros((lay.t, di), BF16)
            hg = _mlstm_call(lay, hg, qvo, kt, gp, gpt, hw, nh, dh, prompt=True)
            n0 = state_n[:, j].reshape(nbs, 2 * nh, dh)
            m0 = jnp.broadcast_to(state_m[:, j].reshape(nbs, 2 * nh, 1), (nbs, 2 * nh, LANES))
            hg = _mlstm_call(lay, hg, qvo, kt, gp, gpt, hw, nh, dh, prompt=False, state=(state_C, j, n0, m0))
            x, u2 = _mm_res_call(lay, hg, m_w_out[j].astype(BF16), x, mods, l, mod2)
            qvos.append(qvo)
            kts.append(kt)
            gpts.append(gpt)
        elif kind == 1:
            wo = f_w_out[j].astype(BF16)
            u2 = jnp.zeros((lay.t, d), BF16)
            x, u2 = _fnet_call(lay, x, u1, u2, mods, l, wo, mod2, prompt=True)
            x, u2 = _fnet_call(lay, x, u1, u2, mods, l, wo, mod2, prompt=False)
        else:
            glu = _glu_call(lay, u1, cv_w_pw1[j].astype(BF16), row(cv_b_pw1[j]))
            x, u2 = _conv_call(lay, glu, cv_w_dw[j], cv_b_dw[j], cv_ln_w[j], cv_ln_b[j], cv_w_pw2[j].astype(BF16),
                               cv_b_pw2[j], x, mods, l, mod2)
        wr = jnp.zeros((d, LANES), F32).at[:, :ng].set(r_w_group[l]).at[:, ng:ng + ne].set(r_w_expert[l])
        final = l + 1 == depth
        mod_next = _Mod(lay, mods, final_norm_w, l, 0) if final else _Mod(lay, mods, norm1_w[l + 1], l + 1, 0)
        outs = _moe_layer(lay, x, u2, mods, l, mod_next, final, wr.astype(BF16), tri_r, upper, e_w13, e_w2, ng, ne)
        if final:
            y = outs
        else:
            x, u1 = outs

    y_prompt = y[0].reshape(nbp, sp, d)
    y_sample = y[1].reshape(nbs, ss, d)
    new_c, new_n, new_m = _state_call(lay, qvos, kts, gpts, nh, dh)
    return (y_prompt, y_sample, new_c, new_n, new_m)
```

```python
import functools
import math

import numpy as np
import jax
import jax.numpy as jnp
from jax import lax
from jax.experimental import pallas as pl
from jax.experimental.pallas import tpu as pltpu

F32 = jnp.float32
BF16 = jnp.bfloat16
U32 = jnp.uint32
EPS = 1e-6
GRID_W = 64
N_MIXERS = 3
FNET_GROUPS = 4
CONV_WIDTH = 31
MOE_GROUPS = 4
MOE_TOP_K = 2

LANES = 128
SUBLANES = 8
MLSTM_L = 256
MOE_BM = 256
CONV_HALO = 16
VMEM_LIMIT = 56 * 1024 * 1024


def _cparams(sem, vmem=VMEM_LIMIT):
    return pltpu.CompilerParams(dimension_semantics=sem, vmem_limit_bytes=vmem)


def _dot(a, b):
    return jnp.dot(a, b, preferred_element_type=F32)


def _dot_nt(a, b):
    return lax.dot_general(a, b, (((1,), (1,)), ((), ())), preferred_element_type=F32)


def _rms(x, w):
    return x * lax.rsqrt(jnp.mean(x * x, axis=-1, keepdims=True) + EPS) * w


def _modulate(x, w, shift, scale):
    return _rms(x, w) * (1.0 + scale) + shift


def _sigmoid(x):
    return 1.0 / (1.0 + jnp.exp(-x))


def _log_sigmoid(x):
    return jnp.minimum(x, 0.0) - jnp.log(1.0 + jnp.exp(-jnp.abs(x)))


def _split2(x):
    hi = x.astype(BF16)
    return hi, (x - hi.astype(F32)).astype(BF16)


def _split3(x):
    hi = x.astype(BF16)
    r1 = x - hi.astype(F32)
    mid = r1.astype(BF16)
    return hi, mid, (r1 - mid.astype(F32)).astype(BF16)


class _Layout:
    def __init__(self, nbp, sp, nbs, ss, d):
        self.nbp, self.sp, self.nbs, self.ss, self.d = nbp, sp, nbs, ss, d
        self.tp, self.ts = nbp * sp, nbs * ss
        self.t = self.tp + self.ts
        assert self.tp % ss == 0, "latent sequences must start on a block boundary of their own length"
        self.ngp = -(-(1 + nbs) // SUBLANES) * SUBLANES

    def group(self, row0):
        return jnp.where(row0 < self.tp, 0, 1 + (row0 - self.tp) // self.ss)

    def row_tile(self, want):
        tm = math.gcd(math.gcd(self.tp, self.ss), want)
        assert tm % SUBLANES == 0
        return tm

    def mod_spec(self, l, chunk):
        return pl.BlockSpec((None, self.ngp, self.d), lambda *_: (l, 0, chunk))

    def row_spec(self):
        return pl.BlockSpec((1, self.d), lambda *_: (0, 0))


class _Mod:
    def __init__(self, lay, mods, nw, l, c_shift):
        self.args = (nw.reshape(1, -1), mods, mods)
        self.specs = [lay.row_spec(), lay.mod_spec(l, c_shift), lay.mod_spec(l, c_shift + 1)]


def _mod_value(x, nw_ref, sh_ref, sc_ref, grp):
    return _modulate(x, nw_ref[...], sh_ref[pl.ds(grp, 1), :], sc_ref[pl.ds(grp, 1), :])


def _ada_kernel(cv_ref, w_ref, b_ref, o_ref):
    s = cv_ref[...]
    s = s * _sigmoid(s)
    o_ref[...] = _dot(s.astype(BF16), w_ref[...].astype(BF16)) + b_ref[...]


def _ada_call(cv, ada_w, ada_b):
    depth, d, n = ada_w.shape
    ngp = cv.shape[0]
    tn = min(n, 2048)
    return pl.pallas_call(
        _ada_kernel,
        out_shape=jax.ShapeDtypeStruct((depth, ngp, n), F32),
        grid=(depth, n // tn),
        in_specs=[
            pl.BlockSpec((ngp, d), lambda l, j: (0, 0)),
            pl.BlockSpec((None, d, tn), lambda l, j: (l, 0, j)),
            pl.BlockSpec((None, 1, tn), lambda l, j: (l, 0, j)),
        ],
        out_specs=pl.BlockSpec((None, ngp, tn), lambda l, j: (l, 0, j)),
        compiler_params=_cparams(("parallel", "parallel")),
        name="ada_mods",
    )(cv, ada_w, ada_b.reshape(depth, 1, n))


def _prep_kernel(xp_ref, xs_ref, nw_ref, sh_ref, sc_ref, x_ref, u_ref, *, lay, tm):
    i = pl.program_id(0)
    grp = lay.group(i * tm)
    for src, cond in ((xp_ref, i * tm < lay.tp), (xs_ref, i * tm >= lay.tp)):
        @pl.when(cond)
        def _(src=src):
            x = src[...]
            x_ref[...] = x
            u_ref[...] = _mod_value(x, nw_ref, sh_ref, sc_ref, grp).astype(BF16)


def _prep_call(lay, xp, xs, mod):
    d = lay.d
    tm = lay.row_tile(256)
    ntp = lay.tp // tm
    rows = pl.BlockSpec((tm, d), lambda i: (i, 0))
    return pl.pallas_call(
        functools.partial(_prep_kernel, lay=lay, tm=tm),
        out_shape=(jax.ShapeDtypeStruct((lay.t, d), F32), jax.ShapeDtypeStruct((lay.t, d), BF16)),
        grid=(lay.t // tm,),
        in_specs=[
            pl.BlockSpec((tm, d), lambda i: (jnp.minimum(i, ntp - 1), 0)),
            pl.BlockSpec((tm, d), lambda i: (jnp.maximum(i - ntp, 0), 0)),
        ] + mod.specs,
        out_specs=(rows, rows),
        compiler_params=_cparams(("arbitrary",)),
        name="prep_modulate",
    )(xp, xs, *mod.args)


def _gates_kernel(x_ref, nw_ref, sh_ref, sc_ref, wh_ref, wl_ref, b_ref, tri_ref, o_ref, *, lay, nh):
    l = x_ref.shape[0]
    u = _mod_value(x_ref[...], nw_ref, sh_ref, sc_ref, lay.group(pl.program_id(0) * l))
    u_hi, u_lo = _split2(u)
    wh = wh_ref[...]
    g = _dot_nt(u_hi, wh) + _dot_nt(u_lo, wh) + _dot_nt(u_hi, wl_ref[...]) + b_ref[...]
    lane = lax.broadcasted_iota(jnp.int32, g.shape, 1)
    is_f = ((lane >= nh) & (lane < 2 * nh)) | ((lane >= 3 * nh) & (lane < 4 * nh))
    lf = jnp.where(is_f, _log_sigmoid(g), 0.0)
    tri = tri_ref[...]
    hi, mid, lo = _split3(lf)
    prefix = _dot(tri, hi) + _dot(tri, mid) + _dot(tri, lo)
    suffix = jnp.sum(lf, axis=0, keepdims=True) - prefix + lf
    b = jnp.where(lane < 2 * nh, prefix, suffix)
    a = g - pltpu.roll(b, LANES - nh, 1)
    is_a = (lane < nh) | ((lane >= 2 * nh) & (lane < 3 * nh))
    o_ref[...] = jnp.where(is_a, a, b)


def _gates_call(lay, x, mod, wg, bias, tri, nh):
    t, d = x.shape
    l = tri.shape[0]
    wh, wl = _split2(wg)
    return pl.pallas_call(
        functools.partial(_gates_kernel, lay=lay, nh=nh),
        out_shape=jax.ShapeDtypeStruct((t, LANES), F32),
        grid=(t // l,),
        in_specs=[pl.BlockSpec((l, d), lambda i: (i, 0))] + mod.specs + [
            pl.BlockSpec((LANES, d), lambda i: (0, 0)),
            pl.BlockSpec((LANES, d), lambda i: (0, 0)),
            pl.BlockSpec((1, LANES), lambda i: (0, 0)),
            pl.BlockSpec((l, l), lambda i: (0, 0)),
        ],
        out_specs=pl.BlockSpec((l, LANES), lambda i: (i, 0)),
        compiler_params=_cparams(("parallel",)),
        name="mlstm_gates",
    )(x, *mod.args, wh, wl, bias, tri)


def _proj_kernel(u_ref, w_ref, o_ref, w_bf, *, transposed):
    @pl.when(pl.program_id(1) == 0)
    def _():
        w_bf[...] = w_ref[...].astype(BF16)

    if transposed:
        o_ref[...] = _dot_nt(w_bf[...], u_ref[...]).astype(BF16)
    else:
        o_ref[...] = _dot_nt(u_ref[...], w_bf[...]).astype(BF16)


def _proj_call(lay, u, w_in_t, jl, blocks, tn, transposed):
    t, d = u.shape
    tm = lay.row_tile(1024)
    nb = len(blocks)
    first, gap_at, gap = blocks[0], None, 0
    for idx in range(1, nb):
        if blocks[idx] != blocks[idx - 1] + 1:
            assert gap_at is None
            gap_at, gap = idx, blocks[idx] - blocks[idx - 1] - 1
    wblk = (lambda j: first + j) if gap_at is None else (lambda j: first + j + jnp.where(j >= gap_at, gap, 0))
    if transposed:
        out_shape = jax.ShapeDtypeStruct((nb * tn, t), BF16)
        out_spec = pl.BlockSpec((tn, tm), lambda j, i: (j, i))
    else:
        out_shape = jax.ShapeDtypeStruct((t, nb * tn), BF16)
        out_spec = pl.BlockSpec((tm, tn), lambda j, i: (i, j))
    return pl.pallas_call(
        functools.partial(_proj_kernel, transposed=transposed),
        out_shape=out_shape,
        grid=(nb, t // tm),
        in_specs=[
            pl.BlockSpec((tm, d), lambda j, i: (i, 0)),
            pl.BlockSpec((None, tn, d), lambda j, i: (jl, wblk(j), 0)),
        ],
        out_specs=out_spec,
        scratch_shapes=[pltpu.VMEM((tn, d), BF16)],
        compiler_params=_cparams(("parallel", "arbitrary")),
        name="mlstm_proj_t" if transposed else "mlstm_proj",
    )(u, w_in_t)


def _col(tile, c):
    lane = lax.broadcasted_iota(jnp.int32, tile.shape, 1)
    return jnp.sum(jnp.where(lane == c, tile, 0.0), axis=-1, keepdims=True)


def _dir_masks(l):
    r = lax.broadcasted_iota(jnp.int32, (l, l), 0)
    c = lax.broadcasted_iota(jnp.int32, (l, l), 1)
    return c <= r, c >= r


def _head_epilogue(h, hw, o):
    hn = h * lax.rsqrt(jnp.mean(h * h, axis=-1, keepdims=True) + EPS) * hw
    return (hn * _sigmoid(o.astype(F32))).astype(BF16)


def _row_times_kt(w_row, kt):
    hi, lo = _split2(w_row)
    sub = lax.broadcasted_iota(jnp.int32, (SUBLANES, w_row.shape[1]), 0)
    stacked = jnp.where(sub == 0, hi.astype(F32), jnp.where(sub == 1, lo.astype(F32), 0.0)).astype(BF16)
    res = _dot_nt(stacked, kt)
    return res[0:1, :] + res[1:2, :]


def _mlstm_single_kernel(q_ref, kt_ref, v_ref, o_ref, gp_ref, gpt_ref, hw_ref, out_ref, *, nh, dh, scale):
    l = q_ref.shape[0]
    gp = gp_ref[...]
    masks = _dir_masks(l)
    for h in range(nh):
        cols = slice(h * dh, (h + 1) * dh)
        qk = _dot(q_ref[:, cols], kt_ref[cols, :])
        p = None
        for d in range(2):
            a_r = gpt_ref[2 * nh * d + h:2 * nh * d + h + 1, :]
            b_c = gp[:, 2 * nh * d + nh + h:2 * nh * d + nh + h + 1]
            g = jnp.where(masks[d], a_r, -jnp.inf)
            m = jnp.maximum(jnp.max(g, axis=-1, keepdims=True), 0.0)
            s = qk * jnp.exp(g - m) * scale
            den = jnp.sum(s, axis=-1, keepdims=True)
            inv = 1.0 / jnp.maximum(jnp.abs(den), jnp.exp(-(b_c + m)))
            p = s * inv if p is None else p + s * inv
        hh = _dot(p.astype(BF16), v_ref[:, cols])
        out_ref[:, cols] = _head_epilogue(hh, hw_ref[:, cols], o_ref[:, cols])


def _mlstm_multi_kernel(q_ref, kt_ref, v_ref, o_ref, gp_ref, gpt_ref, hw_ref, c0_ref, n0_ref, m0_ref,
                        out_ref, cst, cbf, *, nh, nc, l, scale):
    h = pl.program_id(1)
    masks = _dir_masks(l)
    m_in = [[None] * nc for _ in range(2)]
    n_in = [[None] * nc for _ in range(2)]
    for d in range(2):
        cst[...] = c0_ref[d]
        n = n0_ref[pl.ds(d * nh + h, 1), :]
        m = m0_ref[pl.ds(d * nh + h, 1), 0:1]
        order = list(range(nc)) if d == 0 else list(range(nc - 1, -1, -1))
        for step, c in enumerate(order):
            m_in[d][c], n_in[d][c] = m, n
            cbf[d, c] = cst[...].astype(BF16)
            if step + 1 < nc:
                r0 = c * l
                a_r = gpt_ref[pl.ds(2 * nh * d + h, 1), r0:r0 + l]
                b_r = gpt_ref[pl.ds(2 * nh * d + nh + h, 1), r0:r0 + l]
                m_last = jnp.maximum(jnp.max(a_r, axis=-1, keepdims=True), m)
                b_end = b_r[:, l - 1:l] if d == 0 else b_r[:, 0:1]
                decay = jnp.exp(m - m_last)
                w_end = jnp.exp(a_r - m_last)
                kt = kt_ref[:, r0:r0 + l]
                cst[...] = decay * cst[...] + _dot((kt.astype(F32) * w_end).astype(BF16), v_ref[r0:r0 + l, :])
                n = decay * n + _row_times_kt(w_end, kt)
                m = b_end + m_last
    hw = hw_ref[...]
    for c in range(nc):
        r0 = c * l
        q = q_ref[r0:r0 + l, :]
        v = v_ref[r0:r0 + l, :]
        qk = _dot(q, kt_ref[:, r0:r0 + l])
        qf = q.astype(F32)
        gp = gp_ref[r0:r0 + l, :]
        p = inter = None
        for d in range(2):
            m, n = m_in[d][c], n_in[d][c]
            a_r = gpt_ref[pl.ds(2 * nh * d + h, 1), r0:r0 + l]
            b_c = _col(gp, 2 * nh * d + nh + h)
            g = jnp.where(masks[d], a_r, -jnp.inf)
            mt = jnp.maximum(jnp.max(g, axis=-1, keepdims=True), m)
            s = qk * jnp.exp(g - mt) * scale
            w_prev = jnp.exp(m - mt) * scale
            den = jnp.sum(s, axis=-1, keepdims=True) + w_prev * jnp.sum(qf * n, axis=-1, keepdims=True)
            inv = 1.0 / jnp.maximum(jnp.abs(den), jnp.exp(-(b_c + mt)))
            term = (w_prev * inv) * _dot(q, cbf[d, c])
            p = s * inv if p is None else p + s * inv
            inter = term if inter is None else inter + term
        hh = _dot(p.astype(BF16), v) + inter
        out_ref[r0:r0 + l, :] = _head_epilogue(hh, hw, o_ref[r0:r0 + l, :])


def _mlstm_call(lay, hg, qvo, kt, gp, gpt, hw, nh, dh, prompt, state=None):
    nb, s = (lay.nbp, lay.sp) if prompt else (lay.nbs, lay.ss)
    rb0 = 0 if prompt else lay.tp // s
    scale = dh ** -0.5
    common_in = [
        pl.BlockSpec((s, dh), lambda b, h: (rb0 + b, h)),
        pl.BlockSpec((dh, s), lambda b, h: (h, rb0 + b)),
        pl.BlockSpec((s, dh), lambda b, h: (rb0 + b, nh + h)),
        pl.BlockSpec((s, dh), lambda b, h: (rb0 + b, 2 * nh + h)),
        pl.BlockSpec((s, LANES), lambda b, h: (rb0 + b, 0)),
        pl.BlockSpec((4 * nh, s), lambda b, h: (0, rb0 + b)),
        pl.BlockSpec((1, dh), lambda b, h: (0, h)),
    ]
    hg_spec = pl.BlockSpec((s, dh), lambda b, h: (rb0 + b, h))
    any_spec = pl.BlockSpec(memory_space=pl.ANY)
    if prompt:
        assert s == MLSTM_L
        di = nh * dh
        kern = functools.partial(_mlstm_single_kernel, nh=nh, dh=dh, scale=scale)

        def body(hg_any, *refs):
            kern(*refs)

        return pl.pallas_call(
            body,
            out_shape=jax.ShapeDtypeStruct(hg.shape, hg.dtype),
            grid=(nb,),
            in_specs=[
                any_spec,
                pl.BlockSpec((s, di), lambda b: (rb0 + b, 0)),
                pl.BlockSpec((di, s), lambda b: (0, rb0 + b)),
                pl.BlockSpec((s, di), lambda b: (rb0 + b, 1)),
                pl.BlockSpec((s, di), lambda b: (rb0 + b, 2)),
                pl.BlockSpec((s, LANES), lambda b: (rb0 + b, 0)),
                pl.BlockSpec((4 * nh, s), lambda b: (0, rb0 + b)),
                pl.BlockSpec((1, di), lambda b: (0, 0)),
            ],
            out_specs=pl.BlockSpec((s, di), lambda b: (rb0 + b, 0)),
            input_output_aliases={0: 0},
            compiler_params=_cparams(("parallel",)),
            name="mlstm_prompt",
        )(hg, qvo, kt, qvo, qvo, gp, gpt, hw)
    state_c, jl, n0, m0 = state
    nc = s // MLSTM_L
    kern = functools.partial(_mlstm_multi_kernel, nh=nh, nc=nc, l=MLSTM_L, scale=scale)

    def body(hg_any, *refs):
        kern(*refs)

    return pl.pallas_call(
        body,
        out_shape=jax.ShapeDtypeStruct(hg.shape, hg.dtype),
        grid=(nb, nh),
        in_specs=[any_spec] + common_in + [
            pl.BlockSpec((None, None, 2, None, dh, dh), lambda b, h: (b, jl, 0, h, 0, 0)),
            pl.BlockSpec((None, 2 * nh, dh), lambda b, h: (b, 0, 0)),
            pl.BlockSpec((None, 2 * nh, LANES), lambda b, h: (b, 0, 0)),
        ],
        out_specs=hg_spec,
        scratch_shapes=[pltpu.VMEM((dh, dh), F32), pltpu.VMEM((2, nc, dh, dh), BF16)],
        input_output_aliases={0: 0},
        compiler_params=_cparams(("parallel", "parallel")),
        name="mlstm_latent",
    )(hg, qvo, kt, qvo, qvo, gp, gpt, hw, state_c, n0, m0)


def _state_kernel(*refs, nl, nh, dh):
    ins, (c_ref, n_ref, m_ref) = refs[:3 * nl], refs[3 * nl:]
    lyr = pl.program_id(0)
    for jl in range(nl):
        kt_ref, v_ref, gpt_ref = ins[3 * jl:3 * jl + 3]

        @pl.when(lyr == jl)
        def _(kt_ref=kt_ref, v_ref=v_ref, gpt_ref=gpt_ref):
            l = v_ref.shape[0]
            sub = lax.broadcasted_iota(jnp.int32, m_ref.shape, 0)
            lane = lax.broadcasted_iota(jnp.int32, m_ref.shape, 1)
            m_all = jnp.zeros(m_ref.shape, F32)
            for h in range(nh):
                kt = kt_ref[h * dh:(h + 1) * dh, :]
                ktf = kt.astype(F32)
                v = v_ref[:, h * dh:(h + 1) * dh]
                for d in range(2):
                    a_r = gpt_ref[2 * nh * d + h:2 * nh * d + h + 1, :]
                    b_r = gpt_ref[2 * nh * d + nh + h:2 * nh * d + nh + h + 1, :]
                    m_last = jnp.maximum(jnp.max(a_r, axis=-1, keepdims=True), 0.0)
                    b_end = b_r[:, l - 1:l] if d == 0 else b_r[:, 0:1]
                    w_end = jnp.exp(a_r - m_last)
                    c_ref[d, h] = _dot((ktf * w_end).astype(BF16), v)
                    n_ref[d, h:h + 1, :] = _row_times_kt(w_end, kt)
                    m_all = jnp.where((sub == d) & (lane == h), b_end + m_last, m_all)
            m_ref[...] = m_all


def _state_call(lay, qvos, kts, gpts, nh, dh):
    nl = len(qvos)
    nbp, s = lay.nbp, lay.sp
    di = nh * dh
    assert s == MLSTM_L

    def pick(jl):
        return lambda lyr, b: jnp.where(lyr == jl, b, jnp.where(lyr < jl, 0, nbp - 1))

    in_specs, args = [], []
    for jl in range(nl):
        pb = pick(jl)
        in_specs.append(pl.BlockSpec((di, s), lambda lyr, b, pb=pb: (0, pb(lyr, b))))
        in_specs.append(pl.BlockSpec((s, di), lambda lyr, b, pb=pb: (pb(lyr, b), 1)))
        in_specs.append(pl.BlockSpec((4 * nh, s), lambda lyr, b, pb=pb: (0, pb(lyr, b))))
        args += [kts[jl], qvos[jl], gpts[jl]]
    return pl.pallas_call(
        functools.partial(_state_kernel, nl=nl, nh=nh, dh=dh),
        out_shape=(
            jax.ShapeDtypeStruct((nbp, nl, 2, nh, dh, dh), F32),
            jax.ShapeDtypeStruct((nbp, nl, 2, nh, dh), F32),
            jax.ShapeDtypeStruct((nbp, nl, 2, nh), F32),
        ),
        grid=(nl, nbp),
        in_specs=in_specs,
        out_specs=(
            pl.BlockSpec((None, None, 2, nh, dh, dh), lambda lyr, b: (b, lyr, 0, 0, 0, 0)),
            pl.BlockSpec((None, None, 2, nh, dh), lambda lyr, b: (b, lyr, 0, 0, 0)),
            pl.BlockSpec((None, None, 2, nh), lambda lyr, b: (b, lyr, 0, 0)),
        ),
        compiler_params=_cparams(("arbitrary", "arbitrary")),
        name="mlstm_prompt_state",
    )(*args)


def _mm_res_kernel(a_ref, w_ref, x_ref, g_ref, nw_ref, sh_ref, sc_ref, o_ref, u_ref, *, lay, tm):
    grp = lay.group(pl.program_id(0) * tm)
    gate = g_ref[pl.ds(grp, 1), :]
    w = w_ref[...]
    rc = min(tm, 256)
    for r in range(0, tm, rc):
        xn = x_ref[r:r + rc, :] + gate * _dot(a_ref[r:r + rc, :], w)
        o_ref[r:r + rc, :] = xn
        u_ref[r:r + rc, :] = _mod_value(xn, nw_ref, sh_ref, sc_ref, grp).astype(BF16)


def _mm_res_call(lay, a, w_bf, x, mods, l, mod2):
    t, kdim = a.shape
    d = x.shape[1]
    tm = lay.row_tile(512)
    return pl.pallas_call(
        functools.partial(_mm_res_kernel, lay=lay, tm=tm),
        out_shape=(jax.ShapeDtypeStruct(x.shape, F32), jax.ShapeDtypeStruct(x.shape, BF16)),
        grid=(t // tm,),
        in_specs=[
            pl.BlockSpec((tm, kdim), lambda i: (i, 0)),
            pl.BlockSpec((kdim, d), lambda i: (0, 0)),
            pl.BlockSpec((tm, d), lambda i: (i, 0)),
            lay.mod_spec(l, 2),
        ] + mod2.specs,
        out_specs=(pl.BlockSpec((tm, d), lambda i: (i, 0)), pl.BlockSpec((tm, d), lambda i: (i, 0))),
        compiler_params=_cparams(("parallel",)),
        name="mm_residual",
    )(a, w_bf, x, mods, *mod2.args)


def _fnet_kernel(x_ref, u_ref, u2_any, gt_ref, wc_ref, ds_ref, wo_ref, nw_ref, sh_ref, sc_ref, o_ref, u2_ref,
                 ab_scr, *, lay, row_base, groups, norm):
    s, d = x_ref.shape
    cg = d // groups
    grp = lay.group(row_base + pl.program_id(0) * s)
    wc = wc_ref[...]
    rc = min(s, 256)
    for g in range(groups):
        for r in range(0, s, rc):
            ab = _dot(u_ref[r:r + rc, g * cg:(g + 1) * cg], wc)
            ab_scr[r:r + rc, g * cg:(g + 1) * cg] = ab[:, :cg].astype(BF16)
            ab_scr[s + r:s + r + rc, g * cg:(g + 1) * cg] = ab[:, cg:].astype(BF16)
    gate = gt_ref[pl.ds(grp, 1), :]
    wo = wo_ref[...]
    for r in range(0, s, rc):
        y = _dot(ds_ref[r:r + rc, :], ab_scr[...]) * norm
        xn = x_ref[r:r + rc, :] + gate * _dot(y.astype(BF16), wo)
        o_ref[r:r + rc, :] = xn
        u2_ref[r:r + rc, :] = _mod_value(xn, nw_ref, sh_ref, sc_ref, grp).astype(BF16)


def _dft_mats(s, cg):
    kc = np.arange(cg)
    ang_c = 2.0 * np.pi * np.outer(kc, kc) / cg
    wc = np.concatenate([np.cos(ang_c), np.sin(ang_c)], axis=1)
    ks = np.arange(s)
    ang_s = 2.0 * np.pi * np.outer(ks, ks) / s
    ds = np.concatenate([np.cos(ang_s), -np.sin(ang_s)], axis=1)
    return jnp.asarray(wc, dtype=BF16), jnp.asarray(ds, dtype=BF16)


def _fnet_call(lay, x, u1, u2, mods, l, wo_bf, mod2, prompt):
    nb, s = (lay.nbp, lay.sp) if prompt else (lay.nbs, lay.ss)
    rb0 = 0 if prompt else lay.tp // s
    d = lay.d
    cg = d // FNET_GROUPS
    wc, ds = _dft_mats(s, cg)
    kern = functools.partial(_fnet_kernel, lay=lay, row_base=rb0 * s, groups=FNET_GROUPS,
                             norm=1.0 / math.sqrt(s * cg))
    blk = pl.BlockSpec((s, d), lambda b: (rb0 + b, 0))
    return pl.pallas_call(
        kern,
        out_shape=(jax.ShapeDtypeStruct(x.shape, F32), jax.ShapeDtypeStruct(u2.shape, BF16)),
        grid=(nb,),
        in_specs=[
            blk,
            blk,
            pl.BlockSpec(memory_space=pl.ANY),
            lay.mod_spec(l, 2),
            pl.BlockSpec((cg, 2 * cg), lambda b: (0, 0)),
            pl.BlockSpec((s, 2 * s), lambda b: (0, 0)),
            pl.BlockSpec((d, d), lambda b: (0, 0)),
        ] + mod2.specs,
        out_specs=(blk, blk),
        scratch_shapes=[pltpu.VMEM((2 * s, d), BF16)],
        input_output_aliases={0: 0, 2: 1},
        compiler_params=_cparams(("parallel",)),
        name="fnet_prompt" if prompt else "fnet_latent",
    )(x, u1, u2, mods, wc, ds, wo_bf, *mod2.args)


def _glu_kernel(u_ref, wa_ref, wg_ref, ba_ref, bg_ref, o_ref):
    u = u_ref[...]
    a = _dot(u, wa_ref[...]) + ba_ref[...]
    g = _dot(u, wg_ref[...]) + bg_ref[...]
    o_ref[...] = a * _sigmoid(g)


def _glu_call(lay, u, w_bf, bias):
    t, d = u.shape
    cd = w_bf.shape[1] // 2
    tm = lay.row_tile(1024)
    tn = 512
    nj = cd // tn
    return pl.pallas_call(
        _glu_kernel,
        out_shape=jax.ShapeDtypeStruct((t, cd), F32),
        grid=(t // tm, nj),
        in_specs=[
            pl.BlockSpec((tm, d), lambda i, j: (i, 0)),
            pl.BlockSpec((d, tn), lambda i, j: (0, j)),
            pl.BlockSpec((d, tn), lambda i, j: (0, nj + j)),
            pl.BlockSpec((1, tn), lambda i, j: (0, j)),
            pl.BlockSpec((1, tn), lambda i, j: (0, nj + j)),
        ],
        out_specs=pl.BlockSpec((tm, tn), lambda i, j: (i, j)),
        compiler_params=_cparams(("parallel", "parallel")),
        name="conv_glu",
    )(u, w_bf, w_bf, bias, bias)


def _conv_kernel(c_ref, p_ref, n_ref, wd_ref, bd_ref, lw_ref, lb_ref, w2_ref, b2_ref, x_ref, gt_ref,
                 nw_ref, sh_ref, sc_ref, o_ref, u2_ref, pad, act, *, lay, rb, width):
    i = pl.program_id(0)
    row0 = i * rb
    grp = lay.group(row0)
    seq = jnp.where(row0 < lay.tp, lay.sp, lay.ss)
    pos = jnp.where(row0 < lay.tp, row0 % lay.sp, (row0 - lay.tp) % lay.ss)
    has_prev = (pos != 0).astype(F32)
    has_next = (pos + rb != seq).astype(F32)
    hl = CONV_HALO
    half = width // 2
    cd = c_ref.shape[1]
    span = pad.shape[1]
    pad[0, 0:hl, :] = p_ref[...] * has_prev
    pad[0, hl:hl + rb, :] = c_ref[...]
    pad[0, hl + rb:hl + rb + hl, :] = n_ref[...] * has_next
    for s in range(1, SUBLANES):
        pad[s, 0:span - SUBLANES, :] = pad[0, s:s + span - SUBLANES, :]
    sub = 4 * SUBLANES
    bd = bd_ref[...]
    lw = lw_ref[...]
    lb = lb_ref[...]

    def block(blk, carry):
        r0 = pl.multiple_of(blk * sub, sub)
        acc = jnp.zeros((sub // SUBLANES, SUBLANES, cd), F32) + bd
        for k in range(width):
            q, s = divmod(hl - half + k, SUBLANES)
            win = pad[s, pl.ds(r0 + q * SUBLANES, sub), :]
            acc = acc + win.reshape(sub // SUBLANES, SUBLANES, cd) * wd_ref[k]
        acc = acc.reshape(sub, cd)
        mu = jnp.mean(acc, axis=-1, keepdims=True)
        cen = acc - mu
        var = jnp.mean(cen * cen, axis=-1, keepdims=True)
        y = cen * lax.rsqrt(var + EPS) * lw + lb
        act[pl.ds(r0, sub), :] = (y * _sigmoid(y)).astype(BF16)
        return carry

    lax.fori_loop(0, rb // sub, block, 0)
    xn = x_ref[...] + gt_ref[pl.ds(grp, 1), :] * (_dot(act[...], w2_ref[...]) + b2_ref[...])
    o_ref[...] = xn
    u2_ref[...] = _mod_value(xn, nw_ref, sh_ref, sc_ref, grp).astype(BF16)


def _conv_call(lay, glu, wd, bd, lw, lb, w2_bf, b2, x, mods, l, mod2):
    t, cd = glu.shape
    d = x.shape[1]
    rb = lay.row_tile(256)
    hl = CONV_HALO
    assert CONV_WIDTH // 2 <= hl and rb % hl == 0
    nhb = t // hl
    per = rb // hl
    wd_p = jnp.broadcast_to(wd[:, None, :], (CONV_WIDTH, SUBLANES, cd))
    row = lambda a: a.reshape(1, -1)
    rows = pl.BlockSpec((rb, d), lambda i: (i, 0))
    return pl.pallas_call(
        functools.partial(_conv_kernel, lay=lay, rb=rb, width=CONV_WIDTH),
        out_shape=(jax.ShapeDtypeStruct(x.shape, F32), jax.ShapeDtypeStruct(x.shape, BF16)),
        grid=(t // rb,),
        in_specs=[
            pl.BlockSpec((rb, cd), lambda i: (i, 0)),
            pl.BlockSpec((hl, cd), lambda i: (jnp.maximum(i * per - 1, 0), 0)),
            pl.BlockSpec((hl, cd), lambda i: (jnp.minimum((i + 1) * per, nhb - 1), 0)),
            pl.BlockSpec(wd_p.shape, lambda i: (0, 0, 0)),
            pl.BlockSpec((1, cd), lambda i: (0, 0)),
            pl.BlockSpec((1, cd), lambda i: (0, 0)),
            pl.BlockSpec((1, cd), lambda i: (0, 0)),
            pl.BlockSpec((cd, d), lambda i: (0, 0)),
            pl.BlockSpec((1, d), lambda i: (0, 0)),
            rows,
            lay.mod_spec(l, 2),
        ] + mod2.specs,
        out_specs=(rows, rows),
        scratch_shapes=[pltpu.VMEM((SUBLANES, rb + 2 * hl, cd), F32), pltpu.VMEM((rb, cd), BF16)],
        compiler_params=_cparams(("parallel",)),
        name="conv_dw_ln_pw2",
    )(glu, glu, glu, wd_p, row(bd), row(lw), row(lb), w2_bf, row(b2), x, mods, *mod2.args)


def _router_kernel(u_ref, wr_ref, tri_ref, upper_ref, o_ref, cnt_ref, *, ng, ne):
    logits = _dot(u_ref[...], wr_ref[...])
    lane = lax.broadcasted_iota(jnp.int32, logits.shape, 1)
    big = jnp.int32(4 * LANES)
    neg = -jnp.inf
    epg = ne // ng

    gl = jnp.where(lane < ng, logits, neg)
    gmax = jnp.max(gl, axis=-1, keepdims=True)
    gidx = jnp.min(jnp.where(gl == gmax, lane, big), axis=-1, keepdims=True)
    g_p = 1.0 / jnp.sum(jnp.where(lane < ng, jnp.exp(logits - gmax), 0.0), axis=-1, keepdims=True)

    lo = ng + gidx * epg
    el = jnp.where((lane >= lo) & (lane < lo + epg), logits, neg)
    v1 = jnp.max(el, axis=-1, keepdims=True)
    i1 = jnp.min(jnp.where(el == v1, lane, big), axis=-1, keepdims=True)
    el2 = jnp.where(lane == i1, neg, el)
    v2 = jnp.max(el2, axis=-1, keepdims=True)
    i2 = jnp.min(jnp.where(el2 == v2, lane, big), axis=-1, keepdims=True)
    e1 = i1 - ng
    e2 = i2 - ng
    tt = jnp.exp(v2 - v1)
    p1 = 1.0 / (1.0 + tt)
    gate1 = p1 * g_p
    gate2 = (tt * p1) * g_p

    oh1 = lane == e1
    oh2 = lane == e2
    oh = jnp.where(oh1 | oh2, 1.0, 0.0)
    groups = jnp.floor((jnp.sum(oh, axis=0, keepdims=True) + (SUBLANES - 1)) * (1.0 / SUBLANES))
    groups8 = jnp.broadcast_to(groups, (SUBLANES, LANES))
    start = SUBLANES * _dot(groups8.astype(BF16), upper_ref[...])[0:1, :]
    prefix = _dot(tri_ref[...], oh.astype(BF16)) + start
    pos1 = jnp.sum(jnp.where(oh1, prefix, 0.0), axis=-1, keepdims=True)
    pos2 = jnp.sum(jnp.where(oh2, prefix, 0.0), axis=-1, keepdims=True)
    cnt_ref[...] = groups8

    out = jnp.where(lane == 0, e1.astype(F32), 0.0)
    out = jnp.where(lane == 1, e2.astype(F32), out)
    out = jnp.where(lane == 2, gate1, out)
    out = jnp.where(lane == 3, gate2, out)
    out = jnp.where(lane == 4, pos1, out)
    out = jnp.where(lane == 5, pos2, out)
    o_ref[...] = out


def _router_call(u, wr_bf, tri, upper, ng, ne):
    t, d = u.shape
    tm = tri.shape[0]
    nt = t // tm
    return pl.pallas_call(
        functools.partial(_router_kernel, ng=ng, ne=ne),
        out_shape=(jax.ShapeDtypeStruct((t, LANES), F32), jax.ShapeDtypeStruct((nt, SUBLANES, LANES), F32)),
        grid=(nt,),
        in_specs=[
            pl.BlockSpec((tm, d), lambda i: (i, 0)),
            pl.BlockSpec((d, LANES), lambda i: (0, 0)),
            pl.BlockSpec((tm, tm), lambda i: (0, 0)),
            pl.BlockSpec((LANES, LANES), lambda i: (0, 0)),
        ],
        out_specs=(pl.BlockSpec((tm, LANES), lambda i: (i, 0)),
                   pl.BlockSpec((None, SUBLANES, LANES), lambda i: (i, 0, 0))),
        compiler_params=_cparams(("parallel",)),
        name="moe_router",
    )(u, wr_bf, tri, upper)


def _pack_halves(lo, hi):
    lo_bits = lax.shift_right_logical(pltpu.bitcast(lo, U32), jnp.uint32(16))
    hi_bits = pltpu.bitcast(hi, U32) & jnp.uint32(0xFFFF0000)
    return hi_bits | lo_bits


def _unpack_halves(w):
    lo = pltpu.bitcast(lax.shift_left(w, jnp.uint32(16)), F32)
    hi = pltpu.bitcast(w & jnp.uint32(0xFFFF0000), F32)
    return lo.astype(BF16), hi.astype(BF16)


def _round_bf16(x):
    return x.astype(BF16).astype(F32)


def _group_copy(src, src_g, dst, dst_g, sem):
    g8 = lambda g: pl.ds(pl.multiple_of(g * SUBLANES, SUBLANES), SUBLANES)
    return pltpu.make_async_copy(src.at[g8(src_g), :], dst.at[g8(dst_g), :], sem)


def _for_groups(n, fn):
    def body(g, c):
        fn(g)
        return c
    lax.fori_loop(0, n, body, 0)


def _dispatch_kernel(gdst_ref, ngt_ref, pad0_ref, npad_ref, tail_ref, u_ref, pos_ref, xs_out, loc, zeros, sem,
                     zsem, *, tm, nl, ne):
    i = pl.program_id(0)
    nt = pl.num_programs(0)
    slot = i % 2
    nlg = nl // SUBLANES

    def copy(step, s, g):
        return _group_copy(loc.at[s], g, xs_out, gdst_ref[step * nlg + g], sem.at[s])

    def zero_copy(e, g):
        return _group_copy(zeros, 0, xs_out, pad0_ref[e] + g, zsem)

    def zero_block(t):
        first = pl.multiple_of(tail_ref[0] + t * MOE_BM, MOE_BM)
        return pltpu.make_async_copy(zeros, xs_out.at[pl.ds(first, MOE_BM), :], zsem)

    @pl.when(i == 0)
    def _():
        zeros[...] = jnp.zeros(zeros.shape, zeros.dtype)
        for e in range(ne):
            _for_groups(npad_ref[e], lambda g, e=e: zero_copy(e, g).start())
        _for_groups(tail_ref[1], lambda t: zero_block(t).start())

    @pl.when(i >= 2)
    def _():
        _for_groups(ngt_ref[i - 2], lambda g: copy(i - 2, slot, g).wait())

    pos1 = pos_ref[0:1, :]
    pos2 = pos_ref[1:2, :]
    half = u_ref.shape[1] // 2
    u = u_ref[...]
    rc = 256
    for r in range(0, nl, rc):
        p = lax.broadcasted_iota(jnp.int32, (rc, tm), 0) + r
        onehot = jnp.where((p == pos1) | (p == pos2), 1.0, 0.0).astype(BF16)
        rows = _dot(onehot, u)
        loc[slot, r:r + rc, :] = _pack_halves(rows[:, :half], rows[:, half:])
    _for_groups(ngt_ref[i], lambda g: copy(i, slot, g).start())

    @pl.when(i == nt - 1)
    def _():
        @pl.when(i >= 1)
        def _():
            _for_groups(ngt_ref[i - 1], lambda g: copy(i - 1, 1 - slot, g).wait())

        _for_groups(ngt_ref[i], lambda g: copy(i, slot, g).wait())
        for e in range(ne):
            _for_groups(npad_ref[e], lambda g, e=e: zero_copy(e, g).wait())
        _for_groups(tail_ref[1], lambda t: zero_block(t).wait())


def _dispatch_call(u, pos_rows, gdst, ngt, pad0, npad, tail, nrows, tm, nl):
    t, d = u.shape
    ne = npad.shape[0]
    grid_spec = pltpu.PrefetchScalarGridSpec(
        num_scalar_prefetch=5,
        grid=(t // tm,),
        in_specs=[
            pl.BlockSpec((tm, d), lambda i, *_: (i, 0)),
            pl.BlockSpec((None, SUBLANES, tm), lambda i, *_: (i, 0, 0)),
        ],
        out_specs=pl.BlockSpec(memory_space=pl.ANY),
        scratch_shapes=[pltpu.VMEM((2, nl, d // 2), U32), pltpu.VMEM((MOE_BM, d // 2), U32),
                        pltpu.SemaphoreType.DMA((2,)), pltpu.SemaphoreType.DMA(())],
    )
    return pl.pallas_call(
        functools.partial(_dispatch_kernel, tm=tm, nl=nl, ne=ne),
        out_shape=jax.ShapeDtypeStruct((nrows, d // 2), U32),
        grid_spec=grid_spec,
        compiler_params=_cparams(("arbitrary",)),
        name="moe_dispatch",
    )(gdst, ngt, pad0, npad, tail, u, pos_rows)


def _expert_kernel(row0_ref, nblk_ref, tail_ref, x_hbm, w13_ref, w2_ref, y_hbm, xbuf, ybuf, w13_bf, w2_bf,
                   xsem, ysem, *, hid):
    e = pl.program_id(0)
    n = nblk_ref[e]
    row0 = row0_ref[e]
    half = xbuf.shape[2]

    def rows(first_row):
        return pl.ds(pl.multiple_of(first_row, MOE_BM), MOE_BM)

    def x_copy(c, s):
        return pltpu.make_async_copy(x_hbm.at[rows(row0 + c * MOE_BM), :], xbuf.at[s], xsem.at[s])

    def y_copy(first_row, s):
        return pltpu.make_async_copy(ybuf.at[s], y_hbm.at[rows(first_row), :], ysem.at[s])

    @pl.when(n > 0)
    def _():
        x_copy(0, 0).start()
        w13_bf[...] = w13_ref[...].astype(BF16)
        w2_bf[...] = w2_ref[...].astype(BF16)

        def block(c, carry):
            s = c % 2
            x_copy(c, s).wait()

            @pl.when(c + 1 < n)
            def _():
                x_copy(c + 1, 1 - s).start()

            @pl.when(c >= 2)
            def _():
                y_copy(row0 + (c - 2) * MOE_BM, s).wait()

            x_lo, x_hi = _unpack_halves(xbuf[s])
            hb = _dot(x_lo, w13_bf[:half, :]) + _dot(x_hi, w13_bf[half:, :])
            a = hb[:, :hid]
            act = (a * _sigmoid(a)) * hb[:, hid:]
            y = _round_bf16(_dot(act.astype(BF16), w2_bf[...]))
            ybuf[s] = _pack_halves(y[:, :half], y[:, half:])
            y_copy(row0 + c * MOE_BM, s).start()
            return carry

        lax.fori_loop(0, n, block, 0)

        @pl.when(n >= 2)
        def _():
            y_copy(row0 + (n - 2) * MOE_BM, n % 2).wait()

        y_copy(row0 + (n - 1) * MOE_BM, (n - 1) % 2).wait()

    @pl.when(e == pl.num_programs(0) - 1)
    def _():
        ybuf[0] = jnp.zeros(ybuf.shape[1:], ybuf.dtype)
        _for_groups(tail_ref[1], lambda t: y_copy(tail_ref[0] + t * MOE_BM, 0).start())
        _for_groups(tail_ref[1], lambda t: y_copy(tail_ref[0] + t * MOE_BM, 0).wait())


def _expert_call(xs, row0, nblk, tail, w13, w2, l):
    r, half = xs.shape
    d = 2 * half
    ne = w13.shape[1]
    hid = w2.shape[2]
    grid_spec = pltpu.PrefetchScalarGridSpec(
        num_scalar_prefetch=3,
        grid=(ne,),
        in_specs=[
            pl.BlockSpec(memory_space=pl.ANY),
            pl.BlockSpec((None, None, d, 2 * hid), lambda e, *_: (l, e, 0, 0)),
            pl.BlockSpec((None, None, hid, d), lambda e, *_: (l, e, 0, 0)),
        ],
        out_specs=pl.BlockSpec(memory_space=pl.ANY),
        scratch_shapes=[
            pltpu.VMEM((2, MOE_BM, half), U32), pltpu.VMEM((2, MOE_BM, half), U32),
            pltpu.VMEM((d, 2 * hid), BF16), pltpu.VMEM((hid, d), BF16),
            pltpu.SemaphoreType.DMA((2,)), pltpu.SemaphoreType.DMA((2,)),
        ],
    )
    return pl.pallas_call(
        functools.partial(_expert_kernel, hid=hid),
        out_shape=jax.ShapeDtypeStruct((r, half), U32),
        grid_spec=grid_spec,
        compiler_params=_cparams(("arbitrary",)),
        name="moe_experts",
    )(row0, nblk, tail, xs, w13, w2)


def _combine_kernel(gdst_ref, ngt_ref, x_ref, rt_ref, gt_ref, nw_ref, sh_ref, sc_ref, ys_ref, out_a, out_b,
                    loc, sem, *, lay, tm, nl, final):
    i = pl.program_id(0)
    nt = pl.num_programs(0)
    slot = i % 2
    nlg = nl // SUBLANES

    def copy(step, s, g):
        return _group_copy(ys_ref, gdst_ref[step * nlg + g], loc.at[s], g, sem.at[s])

    @pl.when(i == 0)
    def _():
        loc[...] = jnp.zeros(loc.shape, loc.dtype)
        _for_groups(ngt_ref[0], lambda g: copy(0, 0, g).start())

    _for_groups(ngt_ref[i], lambda g: copy(i, slot, g).wait())

    @pl.when(i + 1 < nt)
    def _():
        _for_groups(ngt_ref[i + 1], lambda g: copy(i + 1, 1 - slot, g).start())

    rt = rt_ref[...]
    gate1, gate2 = rt[:, 2:3], rt[:, 3:4]
    pos1, pos2 = rt[:, 4:5].astype(jnp.int32), rt[:, 5:6].astype(jnp.int32)
    half = loc.shape[2]
    rc = 256
    mix_lo = jnp.zeros((tm, half), F32)
    mix_hi = jnp.zeros((tm, half), F32)
    for r in range(0, nl, rc):
        p = lax.broadcasted_iota(jnp.int32, (tm, rc), 1) + r
        wgt = (jnp.where(p == pos1, gate1, 0.0) + jnp.where(p == pos2, gate2, 0.0)).astype(BF16)
        y_lo, y_hi = _unpack_halves(loc[slot, r:r + rc, :])
        mix_lo = mix_lo + _dot(wgt, y_lo)
        mix_hi = mix_hi + _dot(wgt, y_hi)
    grp = lay.group(i * tm)
    gate = gt_ref[pl.ds(grp, 1), :]
    x_lo = x_ref[:, :half] + gate[:, :half] * mix_lo
    x_hi = x_ref[:, half:] + gate[:, half:] * mix_hi
    ms = (jnp.sum(x_lo * x_lo, axis=-1, keepdims=True) + jnp.sum(x_hi * x_hi, axis=-1, keepdims=True)) / (2 * half)
    inv = lax.rsqrt(ms + EPS)
    nw = nw_ref[...]
    if final:
        y_lo = x_lo * inv * nw[:, :half]
        y_hi = x_hi * inv * nw[:, half:]
        for ref, cond in ((out_a, i * tm < lay.tp), (out_b, i * tm >= lay.tp)):
            @pl.when(cond)
            def _(ref=ref):
                ref[:, :half] = y_lo
                ref[:, half:] = y_hi
    else:
        out_a[:, :half] = x_lo
        out_a[:, half:] = x_hi
        sh = sh_ref[pl.ds(grp, 1), :]
        sc = sc_ref[pl.ds(grp, 1), :]
        out_b[:, :half] = (x_lo * inv * nw[:, :half] * (1.0 + sc[:, :half]) + sh[:, :half]).astype(BF16)
        out_b[:, half:] = (x_hi * inv * nw[:, half:] * (1.0 + sc[:, half:]) + sh[:, half:]).astype(BF16)


def _combine_call(lay, x, route, mods, l, mod_next, gdst, ngt, ys, tm, nl, final):
    t, d = x.shape
    rows = pl.BlockSpec((tm, d), lambda i, *_: (i, 0))
    if final:
        ntp = lay.tp // tm
        out_specs = (pl.BlockSpec((tm, d), lambda i, *_: (jnp.minimum(i, ntp - 1), 0)),
                     pl.BlockSpec((tm, d), lambda i, *_: (jnp.maximum(i - ntp, 0), 0)))
        out_shape = (jax.ShapeDtypeStruct((lay.tp, d), F32), jax.ShapeDtypeStruct((lay.ts, d), F32))
    else:
        out_specs = (rows, rows)
        out_shape = (jax.ShapeDtypeStruct(x.shape, F32), jax.ShapeDtypeStruct(x.shape, BF16))
    grid_spec = pltpu.PrefetchScalarGridSpec(
        num_scalar_prefetch=2,
        grid=(t // tm,),
        in_specs=[
            rows,
            pl.BlockSpec((tm, LANES), lambda i, *_: (i, 0)),
            lay.mod_spec(l, 5),
        ] + mod_next.specs + [pl.BlockSpec(memory_space=pl.ANY)],
        out_specs=out_specs,
        scratch_shapes=[pltpu.VMEM((2, nl, d // 2), U32), pltpu.SemaphoreType.DMA((2,))],
    )
    return pl.pallas_call(
        functools.partial(_combine_kernel, lay=lay, tm=tm, nl=nl, final=final),
        out_shape=out_shape,
        grid_spec=grid_spec,
        compiler_params=_cparams(("arbitrary",)),
        name="moe_combine_final" if final else "moe_combine",
    )(gdst, ngt, x, route, mods, *mod_next.args, ys)


def _moe_layer(lay, x, u2, mods, l, mod_next, final, wr_bf, tri_r, upper, w13, w2, ng, ne):
    t, d = x.shape
    tm = tri_r.shape[0]
    nt = t // tm
    bmg = MOE_BM // SUBLANES
    nl = -(-(MOE_TOP_K * tm + ne * (SUBLANES - 1)) // 256) * 256
    nlg = nl // SUBLANES
    route, counts = _router_call(u2, wr_bf, tri_r, upper, ng, ne)

    c8 = counts[:, 0, :ne].astype(jnp.int32)
    lend = jnp.cumsum(c8, axis=1)
    lstart = lend - c8
    ngt = lend[:, -1].astype(jnp.int32)
    tot = jnp.sum(c8, axis=0)
    padded = (tot + bmg - 1) // bmg * bmg
    gend = jnp.cumsum(padded)
    gbase = (gend - padded)[None, :] + jnp.cumsum(c8, axis=0) - c8
    nb = -(-(MOE_TOP_K * t + nt * ne * (SUBLANES - 1)) // MOE_BM) + ne
    g = jnp.arange(nlg, dtype=jnp.int32)[None, :, None]
    owner = (g >= lstart[:, None, :]) & (g < lend[:, None, :])
    gdst = g[:, :, 0] + jnp.sum(jnp.where(owner, (gbase - lstart)[:, None, :], 0), axis=-1)
    gdst = gdst.reshape(nt * nlg).astype(jnp.int32)
    row0 = ((gend - padded) * SUBLANES).astype(jnp.int32)
    nblk = (padded // bmg).astype(jnp.int32)
    tail = jnp.stack([gend[-1] * SUBLANES, nb - gend[-1] // bmg]).astype(jnp.int32)
    pos = route[:, 2 * MOE_TOP_K:3 * MOE_TOP_K].astype(jnp.int32).reshape(nt, tm, MOE_TOP_K)
    pos_rows = jnp.full((nt, SUBLANES, tm), -1, jnp.int32).at[:, :MOE_TOP_K, :].set(jnp.swapaxes(pos, 1, 2))

    pad0 = (gend - padded + tot).astype(jnp.int32)
    npad = (padded - tot).astype(jnp.int32)
    xs = _dispatch_call(u2, pos_rows, gdst, ngt, pad0, npad, tail, nb * MOE_BM, tm, nl)
    ys = _expert_call(xs, row0, nblk, tail, w13, w2, l)
    return _combine_call(lay, x, route, mods, l, mod_next, gdst, ngt, ys, tm, nl, final)


def _lower_tri(n, strict):
    r = np.arange(n)
    m = (r[None, :] < r[:, None]) if strict else (r[None, :] <= r[:, None])
    return jnp.asarray(m.astype(np.float32), dtype=BF16)


def kernel(x_prompt, x_sample, state_C, state_n, state_m, c, c_ctx, ada_w, ada_b, norm1_w, norm2_w, m_w_in, m_b_gate, m_head_norm_w, m_w_out, f_w_out, cv_w_pw1, cv_b_pw1, cv_w_dw, cv_b_dw, cv_ln_w, cv_ln_b, cv_w_pw2, cv_b_pw2, r_w_group, r_w_expert, e_w13, e_w2, final_norm_w):
    nbp, sp, d = x_prompt.shape
    nbs, ss, _ = x_sample.shape
    assert ss % GRID_W == 0
    lay = _Layout(nbp, sp, nbs, ss, d)
    depth = ada_w.shape[0]
    nh, dh = state_C.shape[3], state_C.shape[4]
    di = nh * dh
    ng = r_w_group.shape[2]
    ne = r_w_expert.shape[2]
    assert ng == MOE_GROUPS and ng + ne <= LANES and 4 * nh <= LANES and MOE_TOP_K == 2

    cv = jnp.zeros((lay.ngp, d), F32).at[0].set(c_ctx).at[1:1 + nbs].set(c)
    mods = _ada_call(cv, ada_w, ada_b)

    tri_l = _lower_tri(MLSTM_L, strict=False)
    tri_r = _lower_tri(lay.row_tile(512), strict=True)
    upper = _lower_tri(LANES, strict=True).T
    row = lambda a: a.reshape(1, -1)
    w_in_t = jnp.swapaxes(m_w_in, 1, 2)
    tn = 1024
    assert di % tn == 0
    nkb = di // tn

    x, u1 = _prep_call(lay, x_prompt.reshape(lay.tp, d), x_sample.reshape(lay.ts, d),
                       _Mod(lay, mods, norm1_w[0], 0, 0))
    y = None
    qvos, kts, gpts = [], [], []
    for l in range(depth):
        j, kind = l // N_MIXERS, l % N_MIXERS
        mod2 = _Mod(lay, mods, norm2_w[l], l, 3)
        if kind == 0:
            wg = jnp.zeros((LANES, d), F32).at[:4 * nh].set(w_in_t[j, 4 * di:])
            bg = jnp.zeros((1, LANES), F32).at[0, :4 * nh].set(m_b_gate[j])
            gp = _gates_call(lay, x, _Mod(lay, mods, norm1_w[l], l, 0), wg, bg, tri_l, nh)
            gpt = gp[:, :4 * nh].T
            qvo_blocks = list(range(nkb)) + list(range(2 * nkb, 4 * nkb))
            qvo = _proj_call(lay, u1, w_in_t, j, qvo_blocks, tn, transposed=False)
            kt = _proj_call(lay, u1, w_in_t, j, list(range(nkb, 2 * nkb)), tn, transposed=True)
            hw = row(m_head_norm_w[j])
            hg = jnp.zeros((lay.t, di), BF16)
            hg = _mlstm_call(lay, hg, qvo, kt, gp, gpt, hw, nh, dh, prompt=True)
            n0 = state_n[:, j].reshape(nbs, 2 * nh, dh)
            m0 = jnp.broadcast_to(state_m[:, j].reshape(nbs, 2 * nh, 1), (nbs, 2 * nh, LANES))
            hg = _mlstm_call(lay, hg, qvo, kt, gp, gpt, hw, nh, dh, prompt=False, state=(state_C, j, n0, m0))
            x, u2 = _mm_res_call(lay, hg, m_w_out[j].astype(BF16), x, mods, l, mod2)
            qvos.append(qvo)
            kts.append(kt)
            gpts.append(gpt)
        elif kind == 1:
            wo = f_w_out[j].astype(BF16)
            u2 = jnp.zeros((lay.t, d), BF16)
            x, u2 = _fnet_call(lay, x, u1, u2, mods, l, wo, mod2, prompt=True)
            x, u2 = _fnet_call(lay, x, u1, u2, mods, l, wo, mod2, prompt=False)
        else:
            glu = _glu_call(lay, u1, cv_w_pw1[j].astype(BF16), row(cv_b_pw1[j]))
            x, u2 = _conv_call(lay, glu, cv_w_dw[j], cv_b_dw[j], cv_ln_w[j], cv_ln_b[j], cv_w_pw2[j].astype(BF16),
                               cv_b_pw2[j], x, mods, l, mod2)
        wr = jnp.zeros((d, LANES), F32).at[:, :ng].set(r_w_group[l]).at[:, ng:ng + ne].set(r_w_expert[l])
        final = l + 1 == depth
        mod_next = _Mod(lay, mods, final_norm_w, l, 0) if final else _Mod(lay, mods, norm1_w[l + 1], l + 1, 0)
        outs = _moe_layer(lay, x, u2, mods, l, mod_next, final, wr.astype(BF16), tri_r, upper, e_w13, e_w2, ng, ne)
        if final:
            y = outs
        else:
            x, u1 = outs

    y_prompt = y[0].reshape(nbp, sp, d)
    y_sample = y[1].reshape(nbs, ss, d)
    new_c, new_n, new_m = _state_call(lay, qvos, kts, gpts, nh, dh)
    return (y_prompt, y_sample, new_c, new_n, new_m)
```

```python
import functools
import math

import numpy as np
import jax
import jax.numpy as jnp
from jax import lax
from jax.experimental import pallas as pl
from jax.experimental.pallas import tpu as pltpu

F32 = jnp.float32
BF16 = jnp.bfloat16
U32 = jnp.uint32
EPS = 1e-6
GRID_W = 64
N_MIXERS = 3
FNET_GROUPS = 4
CONV_WIDTH = 31
MOE_GROUPS = 4
MOE_TOP_K = 2

LANES = 128
SUBLANES = 8
MLSTM_L = 256
MOE_BM = 256
CONV_HALO = 16
VMEM_LIMIT = 56 * 1024 * 1024


def _cparams(sem, vmem=VMEM_LIMIT):
    return pltpu.CompilerParams(dimension_semantics=sem, vmem_limit_bytes=vmem)


def _dot(a, b):
    return jnp.dot(a, b, preferred_element_type=F32)


def _dot_nt(a, b):
    return lax.dot_general(a, b, (((1,), (1,)), ((), ())), preferred_element_type=F32)


def _rms(x, w):
    return x * lax.rsqrt(jnp.mean(x * x, axis=-1, keepdims=True) + EPS) * w


def _modulate(x, w, shift, scale):
    return _rms(x, w) * (1.0 + scale) + shift


def _sigmoid(x):
    return 1.0 / (1.0 + jnp.exp(-x))


def _log_sigmoid(x):
    return jnp.minimum(x, 0.0) - jnp.log(1.0 + jnp.exp(-jnp.abs(x)))


def _split2(x):
    hi = x.astype(BF16)
    return hi, (x - hi.astype(F32)).astype(BF16)


def _split3(x):
    hi = x.astype(BF16)
    r1 = x - hi.astype(F32)
    mid = r1.astype(BF16)
    return hi, mid, (r1 - mid.astype(F32)).astype(BF16)


class _Layout:
    def __init__(self, nbp, sp, nbs, ss, d):
        self.nbp, self.sp, self.nbs, self.ss, self.d = nbp, sp, nbs, ss, d
        self.tp, self.ts = nbp * sp, nbs * ss
        self.t = self.tp + self.ts
        assert self.tp % ss == 0, "latent sequences must start on a block boundary of their own length"
        self.ngp = -(-(1 + nbs) // SUBLANES) * SUBLANES

    def group(self, row0):
        return jnp.where(row0 < self.tp, 0, 1 + (row0 - self.tp) // self.ss)

    def row_tile(self, want):
        tm = math.gcd(math.gcd(self.tp, self.ss), want)
        assert tm % SUBLANES == 0
        return tm

    def mod_spec(self, l, chunk):
        return pl.BlockSpec((None, self.ngp, self.d), lambda *_: (l, 0, chunk))

    def row_spec(self):
        return pl.BlockSpec((1, self.d), lambda *_: (0, 0))


class _Mod:
    def __init__(self, lay, mods, nw, l, c_shift):
        self.args = (nw.reshape(1, -1), mods, mods)
        self.specs = [lay.row_spec(), lay.mod_spec(l, c_shift), lay.mod_spec(l, c_shift + 1)]


def _mod_value(x, nw_ref, sh_ref, sc_ref, grp):
    return _modulate(x, nw_ref[...], sh_ref[pl.ds(grp, 1), :], sc_ref[pl.ds(grp, 1), :])


def _ada_kernel(cv_ref, w_ref, b_ref, o_ref):
    s = cv_ref[...]
    s = s * _sigmoid(s)
    o_ref[...] = _dot(s.astype(BF16), w_ref[...].astype(BF16)) + b_ref[...]


def _ada_call(cv, ada_w, ada_b):
    depth, d, n = ada_w.shape
    ngp = cv.shape[0]
    tn = min(n, 2048)
    return pl.pallas_call(
        _ada_kernel,
        out_shape=jax.ShapeDtypeStruct((depth, ngp, n), F32),
        grid=(depth, n // tn),
        in_specs=[
            pl.BlockSpec((ngp, d), lambda l, j: (0, 0)),
            pl.BlockSpec((None, d, tn), lambda l, j: (l, 0, j)),
            pl.BlockSpec((None, 1, tn), lambda l, j: (l, 0, j)),
        ],
        out_specs=pl.BlockSpec((None, ngp, tn), lambda l, j: (l, 0, j)),
        compiler_params=_cparams(("parallel", "parallel")),
        name="ada_mods",
    )(cv, ada_w, ada_b.reshape(depth, 1, n))


def _prep_kernel(xp_ref, xs_ref, nw_ref, sh_ref, sc_ref, x_ref, u_ref, *, lay, tm):
    i = pl.program_id(0)
    grp = lay.group(i * tm)
    for src, cond in ((xp_ref, i * tm < lay.tp), (xs_ref, i * tm >= lay.tp)):
        @pl.when(cond)
        def _(src=src):
            x = src[...]
            x_ref[...] = x
            u_ref[...] = _mod_value(x, nw_ref, sh_ref, sc_ref, grp).astype(BF16)


def _prep_call(lay, xp, xs, mod):
    d = lay.d
    tm = lay.row_tile(256)
    ntp = lay.tp // tm
    rows = pl.BlockSpec((tm, d), lambda i: (i, 0))
    return pl.pallas_call(
        functools.partial(_prep_kernel, lay=lay, tm=tm),
        out_shape=(jax.ShapeDtypeStruct((lay.t, d), F32), jax.ShapeDtypeStruct((lay.t, d), BF16)),
        grid=(lay.t // tm,),
        in_specs=[
            pl.BlockSpec((tm, d), lambda i: (jnp.minimum(i, ntp - 1), 0)),
            pl.BlockSpec((tm, d), lambda i: (jnp.maximum(i - ntp, 0), 0)),
        ] + mod.specs,
        out_specs=(rows, rows),
        compiler_params=_cparams(("arbitrary",)),
        name="prep_modulate",
    )(xp, xs, *mod.args)


def _gates_kernel(x_ref, nw_ref, sh_ref, sc_ref, wh_ref, wl_ref, b_ref, tri_ref, o_ref, *, lay, nh, tm):
    l = tri_ref.shape[0]
    grp = lay.group(pl.program_id(0) * tm)
    wh = wh_ref[...]
    wl = wl_ref[...]
    tri = tri_ref[...]
    lane = lax.broadcasted_iota(jnp.int32, (l, LANES), 1)
    is_f = ((lane >= nh) & (lane < 2 * nh)) | ((lane >= 3 * nh) & (lane < 4 * nh))
    is_a = (lane < nh) | ((lane >= 2 * nh) & (lane < 3 * nh))
    for r in range(0, tm, l):
        u = _mod_value(x_ref[r:r + l, :], nw_ref, sh_ref, sc_ref, grp)
        u_hi, u_lo = _split2(u)
        g = _dot_nt(u_hi, wh) + _dot_nt(u_lo, wh) + _dot_nt(u_hi, wl) + b_ref[...]
        lf = jnp.where(is_f, _log_sigmoid(g), 0.0)
        hi, mid, lo = _split3(lf)
        prefix = _dot(tri, hi) + _dot(tri, mid) + _dot(tri, lo)
        suffix = jnp.sum(lf, axis=0, keepdims=True) - prefix + lf
        b = jnp.where(lane < 2 * nh, prefix, suffix)
        a = g - pltpu.roll(b, LANES - nh, 1)
        o_ref[r:r + l, :] = jnp.where(is_a, a, b)


def _gates_call(lay, x, mod, wg, bias, tri, nh):
    t, d = x.shape
    l = tri.shape[0]
    tm = lay.row_tile(1024)
    assert tm % l == 0
    wh, wl = _split2(wg)
    return pl.pallas_call(
        functools.partial(_gates_kernel, lay=lay, nh=nh, tm=tm),
        out_shape=jax.ShapeDtypeStruct((t, LANES), F32),
        grid=(t // tm,),
        in_specs=[pl.BlockSpec((tm, d), lambda i: (i, 0))] + mod.specs + [
            pl.BlockSpec((LANES, d), lambda i: (0, 0)),
            pl.BlockSpec((LANES, d), lambda i: (0, 0)),
            pl.BlockSpec((1, LANES), lambda i: (0, 0)),
            pl.BlockSpec((l, l), lambda i: (0, 0)),
        ],
        out_specs=pl.BlockSpec((tm, LANES), lambda i: (i, 0)),
        compiler_params=_cparams(("parallel",)),
        name="mlstm_gates",
    )(x, *mod.args, wh, wl, bias, tri)


def _proj_kernel(u_ref, w_ref, o_ref, w_bf, *, transposed):
    @pl.when(pl.program_id(1) == 0)
    def _():
        w_bf[...] = w_ref[...].astype(BF16)

    if transposed:
        o_ref[...] = _dot_nt(w_bf[...], u_ref[...]).astype(BF16)
    else:
        o_ref[...] = _dot_nt(u_ref[...], w_bf[...]).astype(BF16)


def _proj_call(lay, u, w_in_t, jl, blocks, tn, transposed):
    t, d = u.shape
    tm = lay.row_tile(1024)
    nb = len(blocks)
    first, gap_at, gap = blocks[0], None, 0
    for idx in range(1, nb):
        if blocks[idx] != blocks[idx - 1] + 1:
            assert gap_at is None
            gap_at, gap = idx, blocks[idx] - blocks[idx - 1] - 1
    wblk = (lambda j: first + j) if gap_at is None else (lambda j: first + j + jnp.where(j >= gap_at, gap, 0))
    if transposed:
        out_shape = jax.ShapeDtypeStruct((nb * tn, t), BF16)
        out_spec = pl.BlockSpec((tn, tm), lambda j, i: (j, i))
    else:
        out_shape = jax.ShapeDtypeStruct((t, nb * tn), BF16)
        out_spec = pl.BlockSpec((tm, tn), lambda j, i: (i, j))
    return pl.pallas_call(
        functools.partial(_proj_kernel, transposed=transposed),
        out_shape=out_shape,
        grid=(nb, t // tm),
        in_specs=[
            pl.BlockSpec((tm, d), lambda j, i: (i, 0)),
            pl.BlockSpec((None, tn, d), lambda j, i: (jl, wblk(j), 0)),
        ],
        out_specs=out_spec,
        scratch_shapes=[pltpu.VMEM((tn, d), BF16)],
        compiler_params=_cparams(("parallel", "arbitrary")),
        name="mlstm_proj_t" if transposed else "mlstm_proj",
    )(u, w_in_t)


def _col(tile, c):
    lane = lax.broadcasted_iota(jnp.int32, tile.shape, 1)
    return jnp.sum(jnp.where(lane == c, tile, 0.0), axis=-1, keepdims=True)


def _dir_masks(l):
    r = lax.broadcasted_iota(jnp.int32, (l, l), 0)
    c = lax.broadcasted_iota(jnp.int32, (l, l), 1)
    return c <= r, c >= r


def _head_epilogue(h, hw, o):
    hn = h * lax.rsqrt(jnp.mean(h * h, axis=-1, keepdims=True) + EPS) * hw
    return (hn * _sigmoid(o.astype(F32))).astype(BF16)


def _row_times_kt(w_row, kt):
    hi, lo = _split2(w_row)
    sub = lax.broadcasted_iota(jnp.int32, (SUBLANES, w_row.shape[1]), 0)
    stacked = jnp.where(sub == 0, hi.astype(F32), jnp.where(sub == 1, lo.astype(F32), 0.0)).astype(BF16)
    res = _dot_nt(stacked, kt)
    return res[0:1, :] + res[1:2, :]


def _mlstm_single_kernel(q_ref, kt_ref, v_ref, o_ref, gp_ref, gpt_ref, hw_ref, out_ref, *, nh, dh, scale):
    l = q_ref.shape[0]
    gp = gp_ref[...]
    masks = _dir_masks(l)
    for h in range(nh):
        cols = slice(h * dh, (h + 1) * dh)
        qk = _dot(q_ref[:, cols], kt_ref[cols, :])
        p = None
        for d in range(2):
            a_r = gpt_ref[2 * nh * d + h:2 * nh * d + h + 1, :]
            b_c = gp[:, 2 * nh * d + nh + h:2 * nh * d + nh + h + 1]
            g = jnp.where(masks[d], a_r, -jnp.inf)
            m = jnp.maximum(jnp.max(g, axis=-1, keepdims=True), 0.0)
            s = qk * jnp.exp(g - m) * scale
            den = jnp.sum(s, axis=-1, keepdims=True)
            inv = 1.0 / jnp.maximum(jnp.abs(den), jnp.exp(-(b_c + m)))
            p = s * inv if p is None else p + s * inv
        hh = _dot(p.astype(BF16), v_ref[:, cols])
        out_ref[:, cols] = _head_epilogue(hh, hw_ref[:, cols], o_ref[:, cols])


def _mlstm_multi_kernel(q_ref, kt_ref, v_ref, o_ref, gp_ref, gpt_ref, hw_ref, c0_ref, n0_ref, m0_ref,
                        out_ref, cst, cbf, *, nh, nc, l, scale):
    h = pl.program_id(1)
    masks = _dir_masks(l)
    m_in = [[None] * nc for _ in range(2)]
    n_in = [[None] * nc for _ in range(2)]
    for d in range(2):
        cst[...] = c0_ref[d]
        n = n0_ref[pl.ds(d * nh + h, 1), :]
        m = m0_ref[pl.ds(d * nh + h, 1), 0:1]
        order = list(range(nc)) if d == 0 else list(range(nc - 1, -1, -1))
        for step, c in enumerate(order):
            m_in[d][c], n_in[d][c] = m, n
            cbf[d, c] = cst[...].astype(BF16)
            if step + 1 < nc:
                r0 = c * l
                a_r = gpt_ref[pl.ds(2 * nh * d + h, 1), r0:r0 + l]
                b_r = gpt_ref[pl.ds(2 * nh * d + nh + h, 1), r0:r0 + l]
                m_last = jnp.maximum(jnp.max(a_r, axis=-1, keepdims=True), m)
                b_end = b_r[:, l - 1:l] if d == 0 else b_r[:, 0:1]
                decay = jnp.exp(m - m_last)
                w_end = jnp.exp(a_r - m_last)
                kt = kt_ref[:, r0:r0 + l]
                cst[...] = decay * cst[...] + _dot((kt.astype(F32) * w_end).astype(BF16), v_ref[r0:r0 + l, :])
                n = decay * n + _row_times_kt(w_end, kt)
                m = b_end + m_last
    hw = hw_ref[...]
    for c in range(nc):
        r0 = c * l
        q = q_ref[r0:r0 + l, :]
        v = v_ref[r0:r0 + l, :]
        qk = _dot(q, kt_ref[:, r0:r0 + l])
        qf = q.astype(F32)
        gp = gp_ref[r0:r0 + l, :]
        p = inter = None
        for d in range(2):
            m, n = m_in[d][c], n_in[d][c]
            a_r = gpt_ref[pl.ds(2 * nh * d + h, 1), r0:r0 + l]
            b_c = _col(gp, 2 * nh * d + nh + h)
            g = jnp.where(masks[d], a_r, -jnp.inf)
            mt = jnp.maximum(jnp.max(g, axis=-1, keepdims=True), m)
            s = qk * jnp.exp(g - mt) * scale
            w_prev = jnp.exp(m - mt) * scale
            den = jnp.sum(s, axis=-1, keepdims=True) + w_prev * jnp.sum(qf * n, axis=-1, keepdims=True)
            inv = 1.0 / jnp.maximum(jnp.abs(den), jnp.exp(-(b_c + mt)))
            term = (w_prev * inv) * _dot(q, cbf[d, c])
            p = s * inv if p is None else p + s * inv
            inter = term if inter is None else inter + term
        hh = _dot(p.astype(BF16), v) + inter
        out_ref[r0:r0 + l, :] = _head_epilogue(hh, hw, o_ref[r0:r0 + l, :])


def _mlstm_call(lay, hg, qvo, kt, gp, gpt, hw, nh, dh, prompt, state=None):
    nb, s = (lay.nbp, lay.sp) if prompt else (lay.nbs, lay.ss)
    rb0 = 0 if prompt else lay.tp // s
    scale = dh ** -0.5
    common_in = [
        pl.BlockSpec((s, dh), lambda b, h: (rb0 + b, h)),
        pl.BlockSpec((dh, s), lambda b, h: (h, rb0 + b)),
        pl.BlockSpec((s, dh), lambda b, h: (rb0 + b, nh + h)),
        pl.BlockSpec((s, dh), lambda b, h: (rb0 + b, 2 * nh + h)),
        pl.BlockSpec((s, LANES), lambda b, h: (rb0 + b, 0)),
        pl.BlockSpec((4 * nh, s), lambda b, h: (0, rb0 + b)),
        pl.BlockSpec((1, dh), lambda b, h: (0, h)),
    ]
    hg_spec = pl.BlockSpec((s, dh), lambda b, h: (rb0 + b, h))
    any_spec = pl.BlockSpec(memory_space=pl.ANY)
    if prompt:
        assert s == MLSTM_L
        di = nh * dh
        kern = functools.partial(_mlstm_single_kernel, nh=nh, dh=dh, scale=scale)

        def body(hg_any, *refs):
            kern(*refs)

        return pl.pallas_call(
            body,
            out_shape=jax.ShapeDtypeStruct(hg.shape, hg.dtype),
            grid=(nb,),
            in_specs=[
                any_spec,
                pl.BlockSpec((s, di), lambda b: (rb0 + b, 0)),
                pl.BlockSpec((di, s), lambda b: (0, rb0 + b)),
                pl.BlockSpec((s, di), lambda b: (rb0 + b, 1)),
                pl.BlockSpec((s, di), lambda b: (rb0 + b, 2)),
                pl.BlockSpec((s, LANES), lambda b: (rb0 + b, 0)),
                pl.BlockSpec((4 * nh, s), lambda b: (0, rb0 + b)),
                pl.BlockSpec((1, di), lambda b: (0, 0)),
            ],
            out_specs=pl.BlockSpec((s, di), lambda b: (rb0 + b, 0)),
            input_output_aliases={0: 0},
            compiler_params=_cparams(("parallel",)),
            name="mlstm_prompt",
        )(hg, qvo, kt, qvo, qvo, gp, gpt, hw)
    state_c, jl, n0, m0 = state
    nc = s // MLSTM_L
    kern = functools.partial(_mlstm_multi_kernel, nh=nh, nc=nc, l=MLSTM_L, scale=scale)

    def body(hg_any, *refs):
        kern(*refs)

    return pl.pallas_call(
        body,
        out_shape=jax.ShapeDtypeStruct(hg.shape, hg.dtype),
        grid=(nb, nh),
        in_specs=[any_spec] + common_in + [
            pl.BlockSpec((None, None, 2, None, dh, dh), lambda b, h: (b, jl, 0, h, 0, 0)),
            pl.BlockSpec((None, 2 * nh, dh), lambda b, h: (b, 0, 0)),
            pl.BlockSpec((None, 2 * nh, LANES), lambda b, h: (b, 0, 0)),
        ],
        out_specs=hg_spec,
        scratch_shapes=[pltpu.VMEM((dh, dh), F32), pltpu.VMEM((2, nc, dh, dh), BF16)],
        input_output_aliases={0: 0},
        compiler_params=_cparams(("parallel", "parallel")),
        name="mlstm_latent",
    )(hg, qvo, kt, qvo, qvo, gp, gpt, hw, state_c, n0, m0)


def _state_kernel(*refs, nl, nh, dh):
    ins, (c_ref, n_ref, m_ref) = refs[:3 * nl], refs[3 * nl:]
    lyr = pl.program_id(0)
    for jl in range(nl):
        kt_ref, v_ref, gpt_ref = ins[3 * jl:3 * jl + 3]

        @pl.when(lyr == jl)
        def _(kt_ref=kt_ref, v_ref=v_ref, gpt_ref=gpt_ref):
            l = v_ref.shape[0]
            sub = lax.broadcasted_iota(jnp.int32, m_ref.shape, 0)
            lane = lax.broadcasted_iota(jnp.int32, m_ref.shape, 1)
            m_all = jnp.zeros(m_ref.shape, F32)
            for h in range(nh):
                kt = kt_ref[h * dh:(h + 1) * dh, :]
                ktf = kt.astype(F32)
                v = v_ref[:, h * dh:(h + 1) * dh]
                for d in range(2):
                    a_r = gpt_ref[2 * nh * d + h:2 * nh * d + h + 1, :]
                    b_r = gpt_ref[2 * nh * d + nh + h:2 * nh * d + nh + h + 1, :]
                    m_last = jnp.maximum(jnp.max(a_r, axis=-1, keepdims=True), 0.0)
                    b_end = b_r[:, l - 1:l] if d == 0 else b_r[:, 0:1]
                    w_end = jnp.exp(a_r - m_last)
                    c_ref[d, h] = _dot((ktf * w_end).astype(BF16), v)
                    n_ref[d, h:h + 1, :] = _row_times_kt(w_end, kt)
                    m_all = jnp.where((sub == d) & (lane == h), b_end + m_last, m_all)
            m_ref[...] = m_all


def _state_call(lay, qvos, kts, gpts, nh, dh):
    nl = len(qvos)
    nbp, s = lay.nbp, lay.sp
    di = nh * dh
    assert s == MLSTM_L

    def pick(jl):
        return lambda lyr, b: jnp.where(lyr == jl, b, jnp.where(lyr < jl, 0, nbp - 1))

    in_specs, args = [], []
    for jl in range(nl):
        pb = pick(jl)
        in_specs.append(pl.BlockSpec((di, s), lambda lyr, b, pb=pb: (0, pb(lyr, b))))
        in_specs.append(pl.BlockSpec((s, di), lambda lyr, b, pb=pb: (pb(lyr, b), 1)))
        in_specs.append(pl.BlockSpec((4 * nh, s), lambda lyr, b, pb=pb: (0, pb(lyr, b))))
        args += [kts[jl], qvos[jl], gpts[jl]]
    return pl.pallas_call(
        functools.partial(_state_kernel, nl=nl, nh=nh, dh=dh),
        out_shape=(
            jax.ShapeDtypeStruct((nbp, nl, 2, nh, dh, dh), F32),
            jax.ShapeDtypeStruct((nbp, nl, 2, nh, dh), F32),
            jax.ShapeDtypeStruct((nbp, nl, 2, nh), F32),
        ),
        grid=(nl, nbp),
        in_specs=in_specs,
        out_specs=(
            pl.BlockSpec((None, None, 2, nh, dh, dh), lambda lyr, b: (b, lyr, 0, 0, 0, 0)),
            pl.BlockSpec((None, None, 2, nh, dh), lambda lyr, b: (b, lyr, 0, 0, 0)),
            pl.BlockSpec((None, None, 2, nh), lambda lyr, b: (b, lyr, 0, 0)),
        ),
        compiler_params=_cparams(("arbitrary", "arbitrary")),
        name="mlstm_prompt_state",
    )(*args)


def _mm_res_kernel(a_ref, w_ref, x_ref, g_ref, nw_ref, sh_ref, sc_ref, o_ref, u_ref, *, lay, tm):
    grp = lay.group(pl.program_id(0) * tm)
    gate = g_ref[pl.ds(grp, 1), :]
    w = w_ref[...]
    rc = min(tm, 256)
    for r in range(0, tm, rc):
        xn = x_ref[r:r + rc, :] + gate * _dot(a_ref[r:r + rc, :], w)
        o_ref[r:r + rc, :] = xn
        u_ref[r:r + rc, :] = _mod_value(xn, nw_ref, sh_ref, sc_ref, grp).astype(BF16)


def _mm_res_call(lay, a, w_bf, x, mods, l, mod2):
    t, kdim = a.shape
    d = x.shape[1]
    tm = lay.row_tile(512)
    return pl.pallas_call(
        functools.partial(_mm_res_kernel, lay=lay, tm=tm),
        out_shape=(jax.ShapeDtypeStruct(x.shape, F32), jax.ShapeDtypeStruct(x.shape, BF16)),
        grid=(t // tm,),
        in_specs=[
            pl.BlockSpec((tm, kdim), lambda i: (i, 0)),
            pl.BlockSpec((kdim, d), lambda i: (0, 0)),
            pl.BlockSpec((tm, d), lambda i: (i, 0)),
            lay.mod_spec(l, 2),
        ] + mod2.specs,
        out_specs=(pl.BlockSpec((tm, d), lambda i: (i, 0)), pl.BlockSpec((tm, d), lambda i: (i, 0))),
        compiler_params=_cparams(("parallel",)),
        name="mm_residual",
    )(a, w_bf, x, mods, *mod2.args)


def _fnet_kernel(x_ref, u_ref, u2_any, gt_ref, wc_ref, ds_ref, wo_ref, nw_ref, sh_ref, sc_ref, o_ref, u2_ref,
                 ab_scr, *, lay, row_base, groups, norm):
    s, d = x_ref.shape
    cg = d // groups
    grp = lay.group(row_base + pl.program_id(0) * s)
    wc = wc_ref[...]
    rc = min(s, 256)
    for g in range(groups):
        for r in range(0, s, rc):
            ab = _dot(u_ref[r:r + rc, g * cg:(g + 1) * cg], wc)
            ab_scr[r:r + rc, g * cg:(g + 1) * cg] = ab[:, :cg].astype(BF16)
            ab_scr[s + r:s + r + rc, g * cg:(g + 1) * cg] = ab[:, cg:].astype(BF16)
    gate = gt_ref[pl.ds(grp, 1), :]
    wo = wo_ref[...]
    for r in range(0, s, rc):
        y = _dot(ds_ref[r:r + rc, :], ab_scr[...]) * norm
        xn = x_ref[r:r + rc, :] + gate * _dot(y.astype(BF16), wo)
        o_ref[r:r + rc, :] = xn
        u2_ref[r:r + rc, :] = _mod_value(xn, nw_ref, sh_ref, sc_ref, grp).astype(BF16)


def _dft_mats(s, cg):
    kc = np.arange(cg)
    ang_c = 2.0 * np.pi * np.outer(kc, kc) / cg
    wc = np.concatenate([np.cos(ang_c), np.sin(ang_c)], axis=1)
    ks = np.arange(s)
    ang_s = 2.0 * np.pi * np.outer(ks, ks) / s
    ds = np.concatenate([np.cos(ang_s), -np.sin(ang_s)], axis=1)
    return jnp.asarray(wc, dtype=BF16), jnp.asarray(ds, dtype=BF16)


def _fnet_call(lay, x, u1, u2, mods, l, wo_bf, mod2, prompt):
    nb, s = (lay.nbp, lay.sp) if prompt else (lay.nbs, lay.ss)
    rb0 = 0 if prompt else lay.tp // s
    d = lay.d
    cg = d // FNET_GROUPS
    wc, ds = _dft_mats(s, cg)
    kern = functools.partial(_fnet_kernel, lay=lay, row_base=rb0 * s, groups=FNET_GROUPS,
                             norm=1.0 / math.sqrt(s * cg))
    blk = pl.BlockSpec((s, d), lambda b: (rb0 + b, 0))
    return pl.pallas_call(
        kern,
        out_shape=(jax.ShapeDtypeStruct(x.shape, F32), jax.ShapeDtypeStruct(u2.shape, BF16)),
        grid=(nb,),
        in_specs=[
            blk,
            blk,
            pl.BlockSpec(memory_space=pl.ANY),
            lay.mod_spec(l, 2),
            pl.BlockSpec((cg, 2 * cg), lambda b: (0, 0)),
            pl.BlockSpec((s, 2 * s), lambda b: (0, 0)),
            pl.BlockSpec((d, d), lambda b: (0, 0)),
        ] + mod2.specs,
        out_specs=(blk, blk),
        scratch_shapes=[pltpu.VMEM((2 * s, d), BF16)],
        input_output_aliases={0: 0, 2: 1},
        compiler_params=_cparams(("parallel",)),
        name="fnet_prompt" if prompt else "fnet_latent",
    )(x, u1, u2, mods, wc, ds, wo_bf, *mod2.args)


def _glu_kernel(u_ref, wa_ref, wg_ref, ba_ref, bg_ref, o_ref):
    u = u_ref[...]
    a = _dot(u, wa_ref[...]) + ba_ref[...]
    g = _dot(u, wg_ref[...]) + bg_ref[...]
    o_ref[...] = a * _sigmoid(g)


def _glu_call(lay, u, w_bf, bias):
    t, d = u.shape
    cd = w_bf.shape[1] // 2
    tm = lay.row_tile(1024)
    tn = 512
    nj = cd // tn
    return pl.pallas_call(
        _glu_kernel,
        out_shape=jax.ShapeDtypeStruct((t, cd), F32),
        grid=(t // tm, nj),
        in_specs=[
            pl.BlockSpec((tm, d), lambda i, j: (i, 0)),
            pl.BlockSpec((d, tn), lambda i, j: (0, j)),
            pl.BlockSpec((d, tn), lambda i, j: (0, nj + j)),
            pl.BlockSpec((1, tn), lambda i, j: (0, j)),
            pl.BlockSpec((1, tn), lambda i, j: (0, nj + j)),
        ],
        out_specs=pl.BlockSpec((tm, tn), lambda i, j: (i, j)),
        compiler_params=_cparams(("parallel", "parallel")),
        name="conv_glu",
    )(u, w_bf, w_bf, bias, bias)


def _conv_kernel(c_ref, p_ref, n_ref, wd_ref, bd_ref, lw_ref, lb_ref, w2_ref, b2_ref, x_ref, gt_ref,
                 nw_ref, sh_ref, sc_ref, o_ref, u2_ref, pad, conv, act, *, lay, rb, width):
    i = pl.program_id(0)
    row0 = i * rb
    grp = lay.group(row0)
    seq = jnp.where(row0 < lay.tp, lay.sp, lay.ss)
    pos = jnp.where(row0 < lay.tp, row0 % lay.sp, (row0 - lay.tp) % lay.ss)
    has_prev = (pos != 0).astype(F32)
    has_next = (pos + rb != seq).astype(F32)
    hl = CONV_HALO
    half = width // 2
    cd = c_ref.shape[1]
    span = pad.shape[1]
    pad[0, 0:hl, :] = p_ref[...] * has_prev
    pad[0, hl:hl + rb, :] = c_ref[...]
    pad[0, hl + rb:hl + rb + hl, :] = n_ref[...] * has_next
    for s in range(1, SUBLANES):
        pad[s, 0:span - SUBLANES, :] = pad[0, s:s + span - SUBLANES, :]
    ngrp = 8
    sub = ngrp * SUBLANES
    lanes = 2 * LANES
    assert rb % sub == 0 and cd % lanes == 0

    def conv_block(blk, carry):
        r0 = pl.multiple_of(blk * sub, sub)
        for c0 in range(0, cd, lanes):
            bias = bd_ref[:, c0:c0 + lanes]
            accs = [jnp.zeros((SUBLANES, lanes), F32) + bias for _ in range(ngrp)]
            for k in sorted(range(width), key=lambda k: ((hl - half + k) % SUBLANES, k)):
                q, s = divmod(hl - half + k, SUBLANES)
                wk = wd_ref[k, :, c0:c0 + lanes]
                for gi in range(ngrp):
                    win = pad[s, pl.ds(r0 + (q + gi) * SUBLANES, SUBLANES), c0:c0 + lanes]
                    accs[gi] = accs[gi] + win * wk
            conv[pl.ds(r0, sub), c0:c0 + lanes] = jnp.concatenate(accs, axis=0)
        return carry

    lax.fori_loop(0, rb // sub, conv_block, 0)
    lw = lw_ref[...]
    lb = lb_ref[...]
    lsub = min(rb, 16 * SUBLANES)

    def ln_block(blk, carry):
        r0 = pl.multiple_of(blk * lsub, lsub)
        acc = conv[pl.ds(r0, lsub), :]
        mu = jnp.mean(acc, axis=-1, keepdims=True)
        cen = acc - mu
        var = jnp.mean(cen * cen, axis=-1, keepdims=True)
        y = cen * lax.rsqrt(var + EPS) * lw + lb
        act[pl.ds(r0, lsub), :] = (y * _sigmoid(y)).astype(BF16)
        return carry

    lax.fori_loop(0, rb // lsub, ln_block, 0)
    xn = x_ref[...] + gt_ref[pl.ds(grp, 1), :] * (_dot(act[...], w2_ref[...]) + b2_ref[...])
    o_ref[...] = xn
    u2_ref[...] = _mod_value(xn, nw_ref, sh_ref, sc_ref, grp).astype(BF16)


def _conv_call(lay, glu, wd, bd, lw, lb, w2_bf, b2, x, mods, l, mod2):
    t, cd = glu.shape
    d = x.shape[1]
    rb = lay.row_tile(256)
    hl = CONV_HALO
    assert CONV_WIDTH // 2 <= hl and rb % hl == 0
    nhb = t // hl
    per = rb // hl
    wd_p = jnp.broadcast_to(wd[:, None, :], (CONV_WIDTH, SUBLANES, cd))
    row = lambda a: a.reshape(1, -1)
    rows = pl.BlockSpec((rb, d), lambda i: (i, 0))
    return pl.pallas_call(
        functools.partial(_conv_kernel, lay=lay, rb=rb, width=CONV_WIDTH),
        out_shape=(jax.ShapeDtypeStruct(x.shape, F32), jax.ShapeDtypeStruct(x.shape, BF16)),
        grid=(t // rb,),
        in_specs=[
            pl.BlockSpec((rb, cd), lambda i: (i, 0)),
            pl.BlockSpec((hl, cd), lambda i: (jnp.maximum(i * per - 1, 0), 0)),
            pl.BlockSpec((hl, cd), lambda i: (jnp.minimum((i + 1) * per, nhb - 1), 0)),
            pl.BlockSpec(wd_p.shape, lambda i: (0, 0, 0)),
            pl.BlockSpec((1, cd), lambda i: (0, 0)),
            pl.BlockSpec((1, cd), lambda i: (0, 0)),
            pl.BlockSpec((1, cd), lambda i: (0, 0)),
            pl.BlockSpec((cd, d), lambda i: (0, 0)),
            pl.BlockSpec((1, d), lambda i: (0, 0)),
            rows,
            lay.mod_spec(l, 2),
        ] + mod2.specs,
        out_specs=(rows, rows),
        scratch_shapes=[pltpu.VMEM((SUBLANES, rb + 2 * hl, cd), F32), pltpu.VMEM((rb, cd), F32),
                        pltpu.VMEM((rb, cd), BF16)],
        compiler_params=_cparams(("parallel",)),
        name="conv_dw_ln_pw2",
    )(glu, glu, glu, wd_p, row(bd), row(lw), row(lb), w2_bf, row(b2), x, mods, *mod2.args)


def _router_kernel(u_ref, wr_ref, tri_ref, upper_ref, o_ref, cnt_ref, *, ng, ne):
    logits = _dot(u_ref[...], wr_ref[...])
    lane = lax.broadcasted_iota(jnp.int32, logits.shape, 1)
    big = jnp.int32(4 * LANES)
    neg = -jnp.inf
    epg = ne // ng

    gl = jnp.where(lane < ng, logits, neg)
    gmax = jnp.max(gl, axis=-1, keepdims=True)
    gidx = jnp.min(jnp.where(gl == gmax, lane, big), axis=-1, keepdims=True)
    g_p = 1.0 / jnp.sum(jnp.where(lane < ng, jnp.exp(logits - gmax), 0.0), axis=-1, keepdims=True)

    lo = ng + gidx * epg
    el = jnp.where((lane >= lo) & (lane < lo + epg), logits, neg)
    v1 = jnp.max(el, axis=-1, keepdims=True)
    i1 = jnp.min(jnp.where(el == v1, lane, big), axis=-1, keepdims=True)
    el2 = jnp.where(lane == i1, neg, el)
    v2 = jnp.max(el2, axis=-1, keepdims=True)
    i2 = jnp.min(jnp.where(el2 == v2, lane, big), axis=-1, keepdims=True)
    e1 = i1 - ng
    e2 = i2 - ng
    tt = jnp.exp(v2 - v1)
    p1 = 1.0 / (1.0 + tt)
    gate1 = p1 * g_p
    gate2 = (tt * p1) * g_p

    oh1 = lane == e1
    oh2 = lane == e2
    oh = jnp.where(oh1 | oh2, 1.0, 0.0)
    groups = jnp.floor((jnp.sum(oh, axis=0, keepdims=True) + (SUBLANES - 1)) * (1.0 / SUBLANES))
    groups8 = jnp.broadcast_to(groups, (SUBLANES, LANES))
    start = SUBLANES * _dot(groups8.astype(BF16), upper_ref[...])[0:1, :]
    prefix = _dot(tri_ref[...], oh.astype(BF16)) + start
    pos1 = jnp.sum(jnp.where(oh1, prefix, 0.0), axis=-1, keepdims=True)
    pos2 = jnp.sum(jnp.where(oh2, prefix, 0.0), axis=-1, keepdims=True)
    cnt_ref[...] = groups8

    out = jnp.where(lane == 0, e1.astype(F32), 0.0)
    out = jnp.where(lane == 1, e2.astype(F32), out)
    out = jnp.where(lane == 2, gate1, out)
    out = jnp.where(lane == 3, gate2, out)
    out = jnp.where(lane == 4, pos1, out)
    out = jnp.where(lane == 5, pos2, out)
    o_ref[...] = out


def _router_call(u, wr_bf, tri, upper, ng, ne):
    t, d = u.shape
    tm = tri.shape[0]
    nt = t // tm
    return pl.pallas_call(
        functools.partial(_router_kernel, ng=ng, ne=ne),
        out_shape=(jax.ShapeDtypeStruct((t, LANES), F32), jax.ShapeDtypeStruct((nt, SUBLANES, LANES), F32)),
        grid=(nt,),
        in_specs=[
            pl.BlockSpec((tm, d), lambda i: (i, 0)),
            pl.BlockSpec((d, LANES), lambda i: (0, 0)),
            pl.BlockSpec((tm, tm), lambda i: (0, 0)),
            pl.BlockSpec((LANES, LANES), lambda i: (0, 0)),
        ],
        out_specs=(pl.BlockSpec((tm, LANES), lambda i: (i, 0)),
                   pl.BlockSpec((None, SUBLANES, LANES), lambda i: (i, 0, 0))),
        compiler_params=_cparams(("parallel",)),
        name="moe_router",
    )(u, wr_bf, tri, upper)


def _pack_halves(lo, hi):
    lo_bits = lax.shift_right_logical(pltpu.bitcast(lo, U32), jnp.uint32(16))
    hi_bits = pltpu.bitcast(hi, U32) & jnp.uint32(0xFFFF0000)
    return hi_bits | lo_bits


def _unpack_halves(w):
    lo = pltpu.bitcast(lax.shift_left(w, jnp.uint32(16)), F32)
    hi = pltpu.bitcast(w & jnp.uint32(0xFFFF0000), F32)
    return lo.astype(BF16), hi.astype(BF16)


def _round_bf16(x):
    return x.astype(BF16).astype(F32)


def _group_copy(src, src_g, dst, dst_g, sem):
    g8 = lambda g: pl.ds(pl.multiple_of(g * SUBLANES, SUBLANES), SUBLANES)
    return pltpu.make_async_copy(src.at[g8(src_g), :], dst.at[g8(dst_g), :], sem)


def _for_groups(n, fn):
    def body(g, c):
        fn(g)
        return c
    lax.fori_loop(0, n, body, 0)


def _dispatch_kernel(gdst_ref, ngt_ref, pad0_ref, npad_ref, tail_ref, u_ref, pos_ref, xs_out, loc, zeros, sem,
                     zsem, *, tm, nl, ne):
    i = pl.program_id(0)
    nt = pl.num_programs(0)
    slot = i % 2
    nlg = nl // SUBLANES

    def copy(step, s, g):
        return _group_copy(loc.at[s], g, xs_out, gdst_ref[step * nlg + g], sem.at[s])

    def zero_copy(e, g):
        return _group_copy(zeros, 0, xs_out, pad0_ref[e] + g, zsem)

    def zero_block(t):
        first = pl.multiple_of(tail_ref[0] + t * MOE_BM, MOE_BM)
        return pltpu.make_async_copy(zeros, xs_out.at[pl.ds(first, MOE_BM), :], zsem)

    @pl.when(i == 0)
    def _():
        zeros[...] = jnp.zeros(zeros.shape, zeros.dtype)
        for e in range(ne):
            _for_groups(npad_ref[e], lambda g, e=e: zero_copy(e, g).start())
        _for_groups(tail_ref[1], lambda t: zero_block(t).start())

    @pl.when(i >= 2)
    def _():
        _for_groups(ngt_ref[i - 2], lambda g: copy(i - 2, slot, g).wait())

    pos1 = pos_ref[0:1, :]
    pos2 = pos_ref[1:2, :]
    half = u_ref.shape[1] // 2
    u = u_ref[...]
    rc = 256
    for r in range(0, nl, rc):
        p = lax.broadcasted_iota(jnp.int32, (rc, tm), 0) + r
        onehot = jnp.where((p == pos1) | (p == pos2), 1.0, 0.0).astype(BF16)
        rows = _dot(onehot, u)
        loc[slot, r:r + rc, :] = _pack_halves(rows[:, :half], rows[:, half:])
    _for_groups(ngt_ref[i], lambda g: copy(i, slot, g).start())

    @pl.when(i == nt - 1)
    def _():
        @pl.when(i >= 1)
        def _():
            _for_groups(ngt_ref[i - 1], lambda g: copy(i - 1, 1 - slot, g).wait())

        _for_groups(ngt_ref[i], lambda g: copy(i, slot, g).wait())
        for e in range(ne):
            _for_groups(npad_ref[e], lambda g, e=e: zero_copy(e, g).wait())
        _for_groups(tail_ref[1], lambda t: zero_block(t).wait())


def _dispatch_call(u, pos_rows, gdst, ngt, pad0, npad, tail, nrows, tm, nl):
    t, d = u.shape
    ne = npad.shape[0]
    grid_spec = pltpu.PrefetchScalarGridSpec(
        num_scalar_prefetch=5,
        grid=(t // tm,),
        in_specs=[
            pl.BlockSpec((tm, d), lambda i, *_: (i, 0)),
            pl.BlockSpec((None, SUBLANES, tm), lambda i, *_: (i, 0, 0)),
        ],
        out_specs=pl.BlockSpec(memory_space=pl.ANY),
        scratch_shapes=[pltpu.VMEM((2, nl, d // 2), U32), pltpu.VMEM((MOE_BM, d // 2), U32),
                        pltpu.SemaphoreType.DMA((2,)), pltpu.SemaphoreType.DMA(())],
    )
    return pl.pallas_call(
        functools.partial(_dispatch_kernel, tm=tm, nl=nl, ne=ne),
        out_shape=jax.ShapeDtypeStruct((nrows, d // 2), U32),
        grid_spec=grid_spec,
        compiler_params=_cparams(("arbitrary",)),
        name="moe_dispatch",
    )(gdst, ngt, pad0, npad, tail, u, pos_rows)


def _expert_kernel(row0_ref, nblk_ref, tail_ref, x_hbm, w13_ref, w2_ref, y_hbm, xbuf, ybuf, w13_bf, w2_bf,
                   xsem, ysem, *, hid):
    e = pl.program_id(0)
    n = nblk_ref[e]
    row0 = row0_ref[e]
    half = xbuf.shape[2]

    def rows(first_row):
        return pl.ds(pl.multiple_of(first_row, MOE_BM), MOE_BM)

    def x_copy(c, s):
        return pltpu.make_async_copy(x_hbm.at[rows(row0 + c * MOE_BM), :], xbuf.at[s], xsem.at[s])

    def y_copy(first_row, s):
        return pltpu.make_async_copy(ybuf.at[s], y_hbm.at[rows(first_row), :], ysem.at[s])

    @pl.when(n > 0)
    def _():
        x_copy(0, 0).start(priority=1)
        w13_bf[...] = w13_ref[...].astype(BF16)
        w2_bf[...] = w2_ref[...].astype(BF16)

        def block(c, carry):
            s = c % 2
            x_copy(c, s).wait()

            @pl.when(c + 1 < n)
            def _():
                x_copy(c + 1, 1 - s).start(priority=1)

            @pl.when(c >= 2)
            def _():
                y_copy(row0 + (c - 2) * MOE_BM, s).wait()

            x_lo, x_hi = _unpack_halves(xbuf[s])
            hb = _dot(x_lo, w13_bf[:half, :]) + _dot(x_hi, w13_bf[half:, :])
            a = hb[:, :hid]
            act = (a * _sigmoid(a)) * hb[:, hid:]
            y = _round_bf16(_dot(act.astype(BF16), w2_bf[...]))
            ybuf[s] = _pack_halves(y[:, :half], y[:, half:])
            y_copy(row0 + c * MOE_BM, s).start(priority=1)
            return carry

        lax.fori_loop(0, n, block, 0)

        @pl.when(n >= 2)
        def _():
            y_copy(row0 + (n - 2) * MOE_BM, n % 2).wait()

        y_copy(row0 + (n - 1) * MOE_BM, (n - 1) % 2).wait()

    @pl.when(e == pl.num_programs(0) - 1)
    def _():
        ybuf[0] = jnp.zeros(ybuf.shape[1:], ybuf.dtype)
        _for_groups(tail_ref[1], lambda t: y_copy(tail_ref[0] + t * MOE_BM, 0).start())
        _for_groups(tail_ref[1], lambda t: y_copy(tail_ref[0] + t * MOE_BM, 0).wait())


def _expert_call(xs, row0, nblk, tail, w13, w2, l):
    r, half = xs.shape
    d = 2 * half
    ne = w13.shape[1]
    hid = w2.shape[2]
    grid_spec = pltpu.PrefetchScalarGridSpec(
        num_scalar_prefetch=3,
        grid=(ne,),
        in_specs=[
            pl.BlockSpec(memory_space=pl.ANY),
            pl.BlockSpec((None, None, d, 2 * hid), lambda e, *_: (l, e, 0, 0)),
            pl.BlockSpec((None, None, hid, d), lambda e, *_: (l, e, 0, 0)),
        ],
        out_specs=pl.BlockSpec(memory_space=pl.ANY),
        scratch_shapes=[
            pltpu.VMEM((2, MOE_BM, half), U32), pltpu.VMEM((2, MOE_BM, half), U32),
            pltpu.VMEM((d, 2 * hid), BF16), pltpu.VMEM((hid, d), BF16),
            pltpu.SemaphoreType.DMA((2,)), pltpu.SemaphoreType.DMA((2,)),
        ],
    )
    return pl.pallas_call(
        functools.partial(_expert_kernel, hid=hid),
        out_shape=jax.ShapeDtypeStruct((r, half), U32),
        grid_spec=grid_spec,
        compiler_params=_cparams(("arbitrary",)),
        name="moe_experts",
    )(row0, nblk, tail, xs, w13, w2)


def _combine_kernel(gdst_ref, ngt_ref, x_ref, rt_ref, gt_ref, nw_ref, sh_ref, sc_ref, ys_ref, out_a, out_b,
                    loc, sem, *, lay, tm, nl, final):
    i = pl.program_id(0)
    nt = pl.num_programs(0)
    slot = i % 2
    nlg = nl // SUBLANES

    def copy(step, s, g):
        return _group_copy(ys_ref, gdst_ref[step * nlg + g], loc.at[s], g, sem.at[s])

    @pl.when(i == 0)
    def _():
        loc[...] = jnp.zeros(loc.shape, loc.dtype)
        _for_groups(ngt_ref[0], lambda g: copy(0, 0, g).start())

    _for_groups(ngt_ref[i], lambda g: copy(i, slot, g).wait())

    @pl.when(i + 1 < nt)
    def _():
        _for_groups(ngt_ref[i + 1], lambda g: copy(i + 1, 1 - slot, g).start())

    rt = rt_ref[...]
    gate1, gate2 = rt[:, 2:3], rt[:, 3:4]
    pos1, pos2 = rt[:, 4:5].astype(jnp.int32), rt[:, 5:6].astype(jnp.int32)
    half = loc.shape[2]
    rc = 256
    mix_lo = jnp.zeros((tm, half), F32)
    mix_hi = jnp.zeros((tm, half), F32)
    for r in range(0, nl, rc):
        p = lax.broadcasted_iota(jnp.int32, (tm, rc), 1) + r
        wgt = (jnp.where(p == pos1, gate1, 0.0) + jnp.where(p == pos2, gate2, 0.0)).astype(BF16)
        y_lo, y_hi = _unpack_halves(loc[slot, r:r + rc, :])
        mix_lo = mix_lo + _dot(wgt, y_lo)
        mix_hi = mix_hi + _dot(wgt, y_hi)
    grp = lay.group(i * tm)
    gate = gt_ref[pl.ds(grp, 1), :]
    x_lo = x_ref[:, :half] + gate[:, :half] * mix_lo
    x_hi = x_ref[:, half:] + gate[:, half:] * mix_hi
    ms = (jnp.sum(x_lo * x_lo, axis=-1, keepdims=True) + jnp.sum(x_hi * x_hi, axis=-1, keepdims=True)) / (2 * half)
    inv = lax.rsqrt(ms + EPS)
    nw = nw_ref[...]
    if final:
        y_lo = x_lo * inv * nw[:, :half]
        y_hi = x_hi * inv * nw[:, half:]
        for ref, cond in ((out_a, i * tm < lay.tp), (out_b, i * tm >= lay.tp)):
            @pl.when(cond)
            def _(ref=ref):
                ref[:, :half] = y_lo
                ref[:, half:] = y_hi
    else:
        out_a[:, :half] = x_lo
        out_a[:, half:] = x_hi
        sh = sh_ref[pl.ds(grp, 1), :]
        sc = sc_ref[pl.ds(grp, 1), :]
        out_b[:, :half] = (x_lo * inv * nw[:, :half] * (1.0 + sc[:, :half]) + sh[:, :half]).astype(BF16)
        out_b[:, half:] = (x_hi * inv * nw[:, half:] * (1.0 + sc[:, half:]) + sh[:, half:]).astype(BF16)


def _combine_call(lay, x, route, mods, l, mod_next, gdst, ngt, ys, tm, nl, final):
    t, d = x.shape
    rows = pl.BlockSpec((tm, d), lambda i, *_: (i, 0))
    if final:
        ntp = lay.tp // tm
        out_specs = (pl.BlockSpec((tm, d), lambda i, *_: (jnp.minimum(i, ntp - 1), 0)),
                     pl.BlockSpec((tm, d), lambda i, *_: (jnp.maximum(i - ntp, 0), 0)))
        out_shape = (jax.ShapeDtypeStruct((lay.tp, d), F32), jax.ShapeDtypeStruct((lay.ts, d), F32))
    else:
        out_specs = (rows, rows)
        out_shape = (jax.ShapeDtypeStruct(x.shape, F32), jax.ShapeDtypeStruct(x.shape, BF16))
    grid_spec = pltpu.PrefetchScalarGridSpec(
        num_scalar_prefetch=2,
        grid=(t // tm,),
        in_specs=[
            rows,
            pl.BlockSpec((tm, LANES), lambda i, *_: (i, 0)),
            lay.mod_spec(l, 5),
        ] + mod_next.specs + [pl.BlockSpec(memory_space=pl.ANY)],
        out_specs=out_specs,
        scratch_shapes=[pltpu.VMEM((2, nl, d // 2), U32), pltpu.SemaphoreType.DMA((2,))],
    )
    return pl.pallas_call(
        functools.partial(_combine_kernel, lay=lay, tm=tm, nl=nl, final=final),
        out_shape=out_shape,
        grid_spec=grid_spec,
        compiler_params=_cparams(("arbitrary",)),
        name="moe_combine_final" if final else "moe_combine",
    )(gdst, ngt, x, route, mods, *mod_next.args, ys)


def _moe_layer(lay, x, u2, mods, l, mod_next, final, wr_bf, tri_r, upper, w13, w2, ng, ne):
    t, d = x.shape
    tm = tri_r.shape[0]
    nt = t // tm
    bmg = MOE_BM // SUBLANES
    nl = -(-(MOE_TOP_K * tm + ne * (SUBLANES - 1)) // 256) * 256
    nlg = nl // SUBLANES
    route, counts = _router_call(u2, wr_bf, tri_r, upper, ng, ne)

    c8 = counts[:, 0, :ne].astype(jnp.int32)
    lend = jnp.cumsum(c8, axis=1)
    lstart = lend - c8
    ngt = lend[:, -1].astype(jnp.int32)
    tot = jnp.sum(c8, axis=0)
    padded = (tot + bmg - 1) // bmg * bmg
    gend = jnp.cumsum(padded)
    gbase = (gend - padded)[None, :] + jnp.cumsum(c8, axis=0) - c8
    nb = -(-(MOE_TOP_K * t + nt * ne * (SUBLANES - 1)) // MOE_BM) + ne
    g = jnp.arange(nlg, dtype=jnp.int32)[None, :, None]
    owner = (g >= lstart[:, None, :]) & (g < lend[:, None, :])
    gdst = g[:, :, 0] + jnp.sum(jnp.where(owner, (gbase - lstart)[:, None, :], 0), axis=-1)
    gdst = gdst.reshape(nt * nlg).astype(jnp.int32)
    row0 = ((gend - padded) * SUBLANES).astype(jnp.int32)
    nblk = (padded // bmg).astype(jnp.int32)
    tail = jnp.stack([gend[-1] * SUBLANES, nb - gend[-1] // bmg]).astype(jnp.int32)
    pos = route[:, 2 * MOE_TOP_K:3 * MOE_TOP_K].astype(jnp.int32).reshape(nt, tm, MOE_TOP_K)
    pos_rows = jnp.full((nt, SUBLANES, tm), -1, jnp.int32).at[:, :MOE_TOP_K, :].set(jnp.swapaxes(pos, 1, 2))

    pad0 = (gend - padded + tot).astype(jnp.int32)
    npad = (padded - tot).astype(jnp.int32)
    xs = _dispatch_call(u2, pos_rows, gdst, ngt, pad0, npad, tail, nb * MOE_BM, tm, nl)
    ys = _expert_call(xs, row0, nblk, tail, w13, w2, l)
    return _combine_call(lay, x, route, mods, l, mod_next, gdst, ngt, ys, tm, nl, final)


def _lower_tri(n, strict):
    r = np.arange(n)
    m = (r[None, :] < r[:, None]) if strict else (r[None, :] <= r[:, None])
    return jnp.asarray(m.astype(np.float32), dtype=BF16)


def kernel(x_prompt, x_sample, state_C, state_n, state_m, c, c_ctx, ada_w, ada_b, norm1_w, norm2_w, m_w_in, m_b_gate, m_head_norm_w, m_w_out, f_w_out, cv_w_pw1, cv_b_pw1, cv_w_dw, cv_b_dw, cv_ln_w, cv_ln_b, cv_w_pw2, cv_b_pw2, r_w_group, r_w_expert, e_w13, e_w2, final_norm_w):
    nbp, sp, d = x_prompt.shape
    nbs, ss, _ = x_sample.shape
    assert ss % GRID_W == 0
    lay = _Layout(nbp, sp, nbs, ss, d)
    depth = ada_w.shape[0]
    nh, dh = state_C.shape[3], state_C.shape[4]
    di = nh * dh
    ng = r_w_group.shape[2]
    ne = r_w_expert.shape[2]
    assert ng == MOE_GROUPS and ng + ne <= LANES and 4 * nh <= LANES and MOE_TOP_K == 2

    cv = jnp.zeros((lay.ngp, d), F32).at[0].set(c_ctx).at[1:1 + nbs].set(c)
    mods = _ada_call(cv, ada_w, ada_b)

    tri_l = _lower_tri(MLSTM_L, strict=False)
    tri_r = _lower_tri(lay.row_tile(512), strict=True)
    upper = _lower_tri(LANES, strict=True).T
    row = lambda a: a.reshape(1, -1)
    w_in_t = jnp.swapaxes(m_w_in, 1, 2)
    tn = 1024
    assert di % tn == 0
    nkb = di // tn

    x, u1 = _prep_call(lay, x_prompt.reshape(lay.tp, d), x_sample.reshape(lay.ts, d),
                       _Mod(lay, mods, norm1_w[0], 0, 0))
    y = None
    qvos, kts, gpts = [], [], []
    for l in range(depth):
        j, kind = l // N_MIXERS, l % N_MIXERS
        mod2 = _Mod(lay, mods, norm2_w[l], l, 3)
        if kind == 0:
            wg = jnp.zeros((LANES, d), F32).at[:4 * nh].set(w_in_t[j, 4 * di:])
            bg = jnp.zeros((1, LANES), F32).at[0, :4 * nh].set(m_b_gate[j])
            gp = _gates_call(lay, x, _Mod(lay, mods, norm1_w[l], l, 0), wg, bg, tri_l, nh)
            gpt = gp[:, :4 * nh].T
            qvo_blocks = list(range(nkb)) + list(range(2 * nkb, 4 * nkb))
            qvo = _proj_call(lay, u1, w_in_t, j, qvo_blocks, tn, transposed=False)
            kt = _proj_call(lay, u1, w_in_t, j, list(range(nkb, 2 * nkb)), tn, transposed=True)
            hw = row(m_head_norm_w[j])
            hg = jnp.zeros((lay.t, di), BF16)
            hg = _mlstm_call(lay, hg, qvo, kt, gp, gpt, hw, nh, dh, prompt=True)
            n0 = state_n[:, j].reshape(nbs, 2 * nh, dh)
            m0 = jnp.broadcast_to(state_m[:, j].reshape(nbs, 2 * nh, 1), (nbs, 2 * nh, LANES))
            hg = _mlstm_call(lay, hg, qvo, kt, gp, gpt, hw, nh, dh, prompt=False, state=(state_C, j, n0, m0))
            x, u2 = _mm_res_call(lay, hg, m_w_out[j].astype(BF16), x, mods, l, mod2)
            qvos.append(qvo)
            kts.append(kt)
            gpts.append(gpt)
        elif kind == 1:
            wo = f_w_out[j].astype(BF16)
            u2 = jnp.zeros((lay.t, d), BF16)
            x, u2 = _fnet_call(lay, x, u1, u2, mods, l, wo, mod2, prompt=True)
            x, u2 = _fnet_call(lay, x, u1, u2, mods, l, wo, mod2, prompt=False)
        else:
            glu = _glu_call(lay, u1, cv_w_pw1[j].astype(BF16), row(cv_b_pw1[j]))
            x, u2 = _conv_call(lay, glu, cv_w_dw[j], cv_b_dw[j], cv_ln_w[j], cv_ln_b[j], cv_w_pw2[j].astype(BF16),
                               cv_b_pw2[j], x, mods, l, mod2)
        wr = jnp.zeros((d, LANES), F32).at[:, :ng].set(r_w_group[l]).at[:, ng:ng + ne].set(r_w_expert[l])
        final = l + 1 == depth
        mod_next = _Mod(lay, mods, final_norm_w, l, 0) if final else _Mod(lay, mods, norm1_w[l + 1], l + 1, 0)
        outs = _moe_layer(lay, x, u2, mods, l, mod_next, final, wr.astype(BF16), tri_r, upper, e_w13, e_w2, ng, ne)
        if final:
            y = outs
        else:
            x, u1 = outs

    y_prompt = y[0].reshape(nbp, sp, d)
    y_sample = y[1].reshape(nbs, ss, d)
    new_c, new_n, new_m = _state_call(lay, qvos, kts, gpts, nh, dh)
    return (y_prompt, y_sample, new_c, new_n, new_m)
```

```python
import functools
import math

import numpy as np
import jax
import jax.numpy as jnp
from jax import lax
from jax.experimental import pallas as pl
from jax.experimental.pallas import tpu as pltpu

F32 = jnp.float32
BF16 = jnp.bfloat16
U32 = jnp.uint32
EPS = 1e-6
GRID_W = 64
N_MIXERS = 3
FNET_GROUPS = 4
CONV_WIDTH = 31
MOE_GROUPS = 4
MOE_TOP_K = 2

LANES = 128
SUBLANES = 8
MLSTM_L = 256
MOE_BM = 256
CONV_HALO = 16
VMEM_LIMIT = 56 * 1024 * 1024


def _cparams(sem, vmem=VMEM_LIMIT):
    return pltpu.CompilerParams(dimension_semantics=sem, vmem_limit_bytes=vmem)


def _dot(a, b):
    return jnp.dot(a, b, preferred_element_type=F32)


def _dot_nt(a, b):
    return lax.dot_general(a, b, (((1,), (1,)), ((), ())), preferred_element_type=F32)


def _rms(x, w):
    return x * lax.rsqrt(jnp.mean(x * x, axis=-1, keepdims=True) + EPS) * w


def _modulate(x, w, shift, scale):
    return _rms(x, w) * (1.0 + scale) + shift


def _sigmoid(x):
    return 1.0 / (1.0 + jnp.exp(-x))


def _log_sigmoid(x):
    return jnp.minimum(x, 0.0) - jnp.log(1.0 + jnp.exp(-jnp.abs(x)))


def _split2(x):
    hi = x.astype(BF16)
    return hi, (x - hi.astype(F32)).astype(BF16)


def _split3(x):
    hi = x.astype(BF16)
    r1 = x - hi.astype(F32)
    mid = r1.astype(BF16)
    return hi, mid, (r1 - mid.astype(F32)).astype(BF16)


class _Layout:
    def __init__(self, nbp, sp, nbs, ss, d):
        self.nbp, self.sp, self.nbs, self.ss, self.d = nbp, sp, nbs, ss, d
        self.tp, self.ts = nbp * sp, nbs * ss
        self.t = self.tp + self.ts
        assert self.tp % ss == 0, "latent sequences must start on a block boundary of their own length"
        self.ngp = -(-(1 + nbs) // SUBLANES) * SUBLANES

    def group(self, row0):
        return jnp.where(row0 < self.tp, 0, 1 + (row0 - self.tp) // self.ss)

    def row_tile(self, want):
        tm = math.gcd(math.gcd(self.tp, self.ss), want)
        assert tm % SUBLANES == 0
        return tm

    def mod_spec(self, l, chunk):
        return pl.BlockSpec((None, self.ngp, self.d), lambda *_: (l, 0, chunk))

    def row_spec(self):
        return pl.BlockSpec((1, self.d), lambda *_: (0, 0))


class _Mod:
    def __init__(self, lay, mods, nw, l, c_shift):
        self.args = (nw.reshape(1, -1), mods, mods)
        self.specs = [lay.row_spec(), lay.mod_spec(l, c_shift), lay.mod_spec(l, c_shift + 1)]


def _mod_value(x, nw_ref, sh_ref, sc_ref, grp):
    return _modulate(x, nw_ref[...], sh_ref[pl.ds(grp, 1), :], sc_ref[pl.ds(grp, 1), :])


def _ada_kernel(cv_ref, w_ref, b_ref, o_ref):
    s = cv_ref[...]
    s = s * _sigmoid(s)
    o_ref[...] = _dot(s.astype(BF16), w_ref[...].astype(BF16)) + b_ref[...]


def _ada_call(cv, ada_w, ada_b):
    depth, d, n = ada_w.shape
    ngp = cv.shape[0]
    tn = min(n, 2048)
    return pl.pallas_call(
        _ada_kernel,
        out_shape=jax.ShapeDtypeStruct((depth, ngp, n), F32),
        grid=(depth, n // tn),
        in_specs=[
            pl.BlockSpec((ngp, d), lambda l, j: (0, 0)),
            pl.BlockSpec((None, d, tn), lambda l, j: (l, 0, j)),
            pl.BlockSpec((None, 1, tn), lambda l, j: (l, 0, j)),
        ],
        out_specs=pl.BlockSpec((None, ngp, tn), lambda l, j: (l, 0, j)),
        compiler_params=_cparams(("parallel", "parallel")),
        name="ada_mods",
    )(cv, ada_w, ada_b.reshape(depth, 1, n))


def _prep_kernel(xp_ref, xs_ref, nw_ref, sh_ref, sc_ref, x_ref, u_ref, *, lay, tm):
    i = pl.program_id(0)
    grp = lay.group(i * tm)
    for src, cond in ((xp_ref, i * tm < lay.tp), (xs_ref, i * tm >= lay.tp)):
        @pl.when(cond)
        def _(src=src):
            x = src[...]
            x_ref[...] = x
            u_ref[...] = _mod_value(x, nw_ref, sh_ref, sc_ref, grp).astype(BF16)


def _prep_call(lay, xp, xs, mod):
    d = lay.d
    tm = lay.row_tile(256)
    ntp = lay.tp // tm
    rows = pl.BlockSpec((tm, d), lambda i: (i, 0))
    return pl.pallas_call(
        functools.partial(_prep_kernel, lay=lay, tm=tm),
        out_shape=(jax.ShapeDtypeStruct((lay.t, d), F32), jax.ShapeDtypeStruct((lay.t, d), BF16)),
        grid=(lay.t // tm,),
        in_specs=[
            pl.BlockSpec((tm, d), lambda i: (jnp.minimum(i, ntp - 1), 0)),
            pl.BlockSpec((tm, d), lambda i: (jnp.maximum(i - ntp, 0), 0)),
        ] + mod.specs,
        out_specs=(rows, rows),
        compiler_params=_cparams(("arbitrary",)),
        name="prep_modulate",
    )(xp, xs, *mod.args)


def _gates_kernel(x_ref, nw_ref, sh_ref, sc_ref, wh_ref, wl_ref, b_ref, tri_ref, o_ref, *, lay, nh, tm):
    l = tri_ref.shape[0]
    grp = lay.group(pl.program_id(0) * tm)
    wh = wh_ref[...]
    wl = wl_ref[...]
    tri = tri_ref[...]
    lane = lax.broadcasted_iota(jnp.int32, (l, LANES), 1)
    is_f = ((lane >= nh) & (lane < 2 * nh)) | ((lane >= 3 * nh) & (lane < 4 * nh))
    is_a = (lane < nh) | ((lane >= 2 * nh) & (lane < 3 * nh))
    for r in range(0, tm, l):
        u = _mod_value(x_ref[r:r + l, :], nw_ref, sh_ref, sc_ref, grp)
        u_hi, u_lo = _split2(u)
        g = _dot_nt(u_hi, wh) + _dot_nt(u_lo, wh) + _dot_nt(u_hi, wl) + b_ref[...]
        lf = jnp.where(is_f, _log_sigmoid(g), 0.0)
        hi, mid, lo = _split3(lf)
        prefix = _dot(tri, hi) + _dot(tri, mid) + _dot(tri, lo)
        suffix = jnp.sum(lf, axis=0, keepdims=True) - prefix + lf
        b = jnp.where(lane < 2 * nh, prefix, suffix)
        a = g - pltpu.roll(b, LANES - nh, 1)
        o_ref[r:r + l, :] = jnp.where(is_a, a, b)


def _gates_call(lay, x, mod, wg, bias, tri, nh):
    t, d = x.shape
    l = tri.shape[0]
    tm = lay.row_tile(1024)
    assert tm % l == 0
    wh, wl = _split2(wg)
    return pl.pallas_call(
        functools.partial(_gates_kernel, lay=lay, nh=nh, tm=tm),
        out_shape=jax.ShapeDtypeStruct((t, LANES), F32),
        grid=(t // tm,),
        in_specs=[pl.BlockSpec((tm, d), lambda i: (i, 0))] + mod.specs + [
            pl.BlockSpec((LANES, d), lambda i: (0, 0)),
            pl.BlockSpec((LANES, d), lambda i: (0, 0)),
            pl.BlockSpec((1, LANES), lambda i: (0, 0)),
            pl.BlockSpec((l, l), lambda i: (0, 0)),
        ],
        out_specs=pl.BlockSpec((tm, LANES), lambda i: (i, 0)),
        compiler_params=_cparams(("parallel",)),
        name="mlstm_gates",
    )(x, *mod.args, wh, wl, bias, tri)


def _proj_kernel(u_ref, w_ref, o_ref, w_bf, *, transposed):
    @pl.when(pl.program_id(1) == 0)
    def _():
        w_bf[...] = w_ref[...].astype(BF16)

    if transposed:
        o_ref[...] = _dot_nt(w_bf[...], u_ref[...]).astype(BF16)
    else:
        o_ref[...] = _dot_nt(u_ref[...], w_bf[...]).astype(BF16)


def _proj_call(lay, u, w_in_t, jl, blocks, tn, transposed):
    t, d = u.shape
    tm = lay.row_tile(1024)
    nb = len(blocks)
    first, gap_at, gap = blocks[0], None, 0
    for idx in range(1, nb):
        if blocks[idx] != blocks[idx - 1] + 1:
            assert gap_at is None
            gap_at, gap = idx, blocks[idx] - blocks[idx - 1] - 1
    wblk = (lambda j: first + j) if gap_at is None else (lambda j: first + j + jnp.where(j >= gap_at, gap, 0))
    if transposed:
        out_shape = jax.ShapeDtypeStruct((nb * tn, t), BF16)
        out_spec = pl.BlockSpec((tn, tm), lambda j, i: (j, i))
    else:
        out_shape = jax.ShapeDtypeStruct((t, nb * tn), BF16)
        out_spec = pl.BlockSpec((tm, tn), lambda j, i: (i, j))
    return pl.pallas_call(
        functools.partial(_proj_kernel, transposed=transposed),
        out_shape=out_shape,
        grid=(nb, t // tm),
        in_specs=[
            pl.BlockSpec((tm, d), lambda j, i: (i, 0)),
            pl.BlockSpec((None, tn, d), lambda j, i: (jl, wblk(j), 0)),
        ],
        out_specs=out_spec,
        scratch_shapes=[pltpu.VMEM((tn, d), BF16)],
        compiler_params=_cparams(("parallel", "arbitrary")),
        name="mlstm_proj_t" if transposed else "mlstm_proj",
    )(u, w_in_t)


def _col(tile, c):
    lane = lax.broadcasted_iota(jnp.int32, tile.shape, 1)
    return jnp.sum(jnp.where(lane == c, tile, 0.0), axis=-1, keepdims=True)


def _dir_masks(l):
    r = lax.broadcasted_iota(jnp.int32, (l, l), 0)
    c = lax.broadcasted_iota(jnp.int32, (l, l), 1)
    return c <= r, c >= r


def _head_epilogue(h, hw, o):
    hn = h * lax.rsqrt(jnp.mean(h * h, axis=-1, keepdims=True) + EPS) * hw
    return (hn * _sigmoid(o.astype(F32))).astype(BF16)


def _row_times_kt(w_row, kt):
    hi, lo = _split2(w_row)
    sub = lax.broadcasted_iota(jnp.int32, (SUBLANES, w_row.shape[1]), 0)
    stacked = jnp.where(sub == 0, hi.astype(F32), jnp.where(sub == 1, lo.astype(F32), 0.0)).astype(BF16)
    res = _dot_nt(stacked, kt)
    return res[0:1, :] + res[1:2, :]


def _mlstm_single_kernel(q_ref, kt_ref, v_ref, o_ref, gp_ref, gpt_ref, hw_ref, out_ref, *, nh, dh, scale):
    l = q_ref.shape[0]
    gp = gp_ref[...]
    masks = _dir_masks(l)
    for h in range(nh):
        cols = slice(h * dh, (h + 1) * dh)
        qk = _dot(q_ref[:, cols], kt_ref[cols, :])
        p = None
        for d in range(2):
            a_r = gpt_ref[2 * nh * d + h:2 * nh * d + h + 1, :]
            b_c = gp[:, 2 * nh * d + nh + h:2 * nh * d + nh + h + 1]
            g = jnp.where(masks[d], a_r, -jnp.inf)
            m = jnp.maximum(jnp.max(g, axis=-1, keepdims=True), 0.0)
            s = qk * jnp.exp(g - m) * scale
            den = jnp.sum(s, axis=-1, keepdims=True)
            inv = 1.0 / jnp.maximum(jnp.abs(den), jnp.exp(-(b_c + m)))
            p = s * inv if p is None else p + s * inv
        hh = _dot(p.astype(BF16), v_ref[:, cols])
        out_ref[:, cols] = _head_epilogue(hh, hw_ref[:, cols], o_ref[:, cols])


def _mlstm_multi_kernel(q_ref, kt_ref, v_ref, o_ref, gp_ref, gpt_ref, hw_ref, c0_ref, n0_ref, m0_ref,
                        out_ref, cst, cbf, *, nh, nc, l, scale):
    h = pl.program_id(1)
    masks = _dir_masks(l)
    m_in = [[None] * nc for _ in range(2)]
    n_in = [[None] * nc for _ in range(2)]
    for d in range(2):
        cst[...] = c0_ref[d]
        n = n0_ref[pl.ds(d * nh + h, 1), :]
        m = m0_ref[pl.ds(d * nh + h, 1), 0:1]
        order = list(range(nc)) if d == 0 else list(range(nc - 1, -1, -1))
        for step, c in enumerate(order):
            m_in[d][c], n_in[d][c] = m, n
            cbf[d, c] = cst[...].astype(BF16)
            if step + 1 < nc:
                r0 = c * l
                a_r = gpt_ref[pl.ds(2 * nh * d + h, 1), r0:r0 + l]
                b_r = gpt_ref[pl.ds(2 * nh * d + nh + h, 1), r0:r0 + l]
                m_last = jnp.maximum(jnp.max(a_r, axis=-1, keepdims=True), m)
                b_end = b_r[:, l - 1:l] if d == 0 else b_r[:, 0:1]
                decay = jnp.exp(m - m_last)
                w_end = jnp.exp(a_r - m_last)
                kt = kt_ref[:, r0:r0 + l]
                cst[...] = decay * cst[...] + _dot((kt.astype(F32) * w_end).astype(BF16), v_ref[r0:r0 + l, :])
                n = decay * n + _row_times_kt(w_end, kt)
                m = b_end + m_last
    hw = hw_ref[...]
    for c in range(nc):
        r0 = c * l
        q = q_ref[r0:r0 + l, :]
        v = v_ref[r0:r0 + l, :]
        qk = _dot(q, kt_ref[:, r0:r0 + l])
        qf = q.astype(F32)
        gp = gp_ref[r0:r0 + l, :]
        p = inter = None
        for d in range(2):
            m, n = m_in[d][c], n_in[d][c]
            a_r = gpt_ref[pl.ds(2 * nh * d + h, 1), r0:r0 + l]
            b_c = _col(gp, 2 * nh * d + nh + h)
            g = jnp.where(masks[d], a_r, -jnp.inf)
            mt = jnp.maximum(jnp.max(g, axis=-1, keepdims=True), m)
            s = qk * jnp.exp(g - mt) * scale
            w_prev = jnp.exp(m - mt) * scale
            den = jnp.sum(s, axis=-1, keepdims=True) + w_prev * jnp.sum(qf * n, axis=-1, keepdims=True)
            inv = 1.0 / jnp.maximum(jnp.abs(den), jnp.exp(-(b_c + mt)))
            term = (w_prev * inv) * _dot(q, cbf[d, c])
            p = s * inv if p is None else p + s * inv
            inter = term if inter is None else inter + term
        hh = _dot(p.astype(BF16), v) + inter
        out_ref[r0:r0 + l, :] = _head_epilogue(hh, hw, o_ref[r0:r0 + l, :])


def _mlstm_call(lay, hg, qvo, kt, gp, gpt, hw, nh, dh, prompt, state=None):
    nb, s = (lay.nbp, lay.sp) if prompt else (lay.nbs, lay.ss)
    rb0 = 0 if prompt else lay.tp // s
    scale = dh ** -0.5
    common_in = [
        pl.BlockSpec((s, dh), lambda b, h: (rb0 + b, h)),
        pl.BlockSpec((dh, s), lambda b, h: (h, rb0 + b)),
        pl.BlockSpec((s, dh), lambda b, h: (rb0 + b, nh + h)),
        pl.BlockSpec((s, dh), lambda b, h: (rb0 + b, 2 * nh + h)),
        pl.BlockSpec((s, LANES), lambda b, h: (rb0 + b, 0)),
        pl.BlockSpec((4 * nh, s), lambda b, h: (0, rb0 + b)),
        pl.BlockSpec((1, dh), lambda b, h: (0, h)),
    ]
    hg_spec = pl.BlockSpec((s, dh), lambda b, h: (rb0 + b, h))
    any_spec = pl.BlockSpec(memory_space=pl.ANY)
    if prompt:
        assert s == MLSTM_L
        di = nh * dh
        kern = functools.partial(_mlstm_single_kernel, nh=nh, dh=dh, scale=scale)

        def body(hg_any, *refs):
            kern(*refs)

        return pl.pallas_call(
            body,
            out_shape=jax.ShapeDtypeStruct(hg.shape, hg.dtype),
            grid=(nb,),
            in_specs=[
                any_spec,
                pl.BlockSpec((s, di), lambda b: (rb0 + b, 0)),
                pl.BlockSpec((di, s), lambda b: (0, rb0 + b)),
                pl.BlockSpec((s, di), lambda b: (rb0 + b, 1)),
                pl.BlockSpec((s, di), lambda b: (rb0 + b, 2)),
                pl.BlockSpec((s, LANES), lambda b: (rb0 + b, 0)),
                pl.BlockSpec((4 * nh, s), lambda b: (0, rb0 + b)),
                pl.BlockSpec((1, di), lambda b: (0, 0)),
            ],
            out_specs=pl.BlockSpec((s, di), lambda b: (rb0 + b, 0)),
            input_output_aliases={0: 0},
            compiler_params=_cparams(("parallel",)),
            name="mlstm_prompt",
        )(hg, qvo, kt, qvo, qvo, gp, gpt, hw)
    state_c, jl, n0, m0 = state
    nc = s // MLSTM_L
    kern = functools.partial(_mlstm_multi_kernel, nh=nh, nc=nc, l=MLSTM_L, scale=scale)

    def body(hg_any, *refs):
        kern(*refs)

    return pl.pallas_call(
        body,
        out_shape=jax.ShapeDtypeStruct(hg.shape, hg.dtype),
        grid=(nb, nh),
        in_specs=[any_spec] + common_in + [
            pl.BlockSpec((None, None, 2, None, dh, dh), lambda b, h: (b, jl, 0, h, 0, 0)),
            pl.BlockSpec((None, 2 * nh, dh), lambda b, h: (b, 0, 0)),
            pl.BlockSpec((None, 2 * nh, LANES), lambda b, h: (b, 0, 0)),
        ],
        out_specs=hg_spec,
        scratch_shapes=[pltpu.VMEM((dh, dh), F32), pltpu.VMEM((2, nc, dh, dh), BF16)],
        input_output_aliases={0: 0},
        compiler_params=_cparams(("parallel", "parallel")),
        name="mlstm_latent",
    )(hg, qvo, kt, qvo, qvo, gp, gpt, hw, state_c, n0, m0)


def _state_kernel(*refs, nl, nh, dh):
    ins, (c_ref, n_ref, m_ref) = refs[:3 * nl], refs[3 * nl:]
    lyr = pl.program_id(0)
    for jl in range(nl):
        kt_ref, v_ref, gpt_ref = ins[3 * jl:3 * jl + 3]

        @pl.when(lyr == jl)
        def _(kt_ref=kt_ref, v_ref=v_ref, gpt_ref=gpt_ref):
            l = v_ref.shape[0]
            sub = lax.broadcasted_iota(jnp.int32, m_ref.shape, 0)
            lane = lax.broadcasted_iota(jnp.int32, m_ref.shape, 1)
            m_all = jnp.zeros(m_ref.shape, F32)
            for h in range(nh):
                kt = kt_ref[h * dh:(h + 1) * dh, :]
                ktf = kt.astype(F32)
                v = v_ref[:, h * dh:(h + 1) * dh]
                for d in range(2):
                    a_r = gpt_ref[2 * nh * d + h:2 * nh * d + h + 1, :]
                    b_r = gpt_ref[2 * nh * d + nh + h:2 * nh * d + nh + h + 1, :]
                    m_last = jnp.maximum(jnp.max(a_r, axis=-1, keepdims=True), 0.0)
                    b_end = b_r[:, l - 1:l] if d == 0 else b_r[:, 0:1]
                    w_end = jnp.exp(a_r - m_last)
                    c_ref[d, h] = _dot((ktf * w_end).astype(BF16), v)
                    n_ref[d, h:h + 1, :] = _row_times_kt(w_end, kt)
                    m_all = jnp.where((sub == d) & (lane == h), b_end + m_last, m_all)
            m_ref[...] = m_all


def _state_call(lay, qvos, kts, gpts, nh, dh):
    nl = len(qvos)
    nbp, s = lay.nbp, lay.sp
    di = nh * dh
    assert s == MLSTM_L

    def pick(jl):
        return lambda lyr, b: jnp.where(lyr == jl, b, jnp.where(lyr < jl, 0, nbp - 1))

    in_specs, args = [], []
    for jl in range(nl):
        pb = pick(jl)
        in_specs.append(pl.BlockSpec((di, s), lambda lyr, b, pb=pb: (0, pb(lyr, b))))
        in_specs.append(pl.BlockSpec((s, di), lambda lyr, b, pb=pb: (pb(lyr, b), 1)))
        in_specs.append(pl.BlockSpec((4 * nh, s), lambda lyr, b, pb=pb: (0, pb(lyr, b))))
        args += [kts[jl], qvos[jl], gpts[jl]]
    return pl.pallas_call(
        functools.partial(_state_kernel, nl=nl, nh=nh, dh=dh),
        out_shape=(
            jax.ShapeDtypeStruct((nbp, nl, 2, nh, dh, dh), F32),
            jax.ShapeDtypeStruct((nbp, nl, 2, nh, dh), F32),
            jax.ShapeDtypeStruct((nbp, nl, 2, nh), F32),
        ),
        grid=(nl, nbp),
        in_specs=in_specs,
        out_specs=(
            pl.BlockSpec((None, None, 2, nh, dh, dh), lambda lyr, b: (b, lyr, 0, 0, 0, 0)),
            pl.BlockSpec((None, None, 2, nh, dh), lambda lyr, b: (b, lyr, 0, 0, 0)),
            pl.BlockSpec((None, None, 2, nh), lambda lyr, b: (b, lyr, 0, 0)),
        ),
        compiler_params=_cparams(("arbitrary", "arbitrary")),
        name="mlstm_prompt_state",
    )(*args)


def _mm_res_kernel(a_ref, w_ref, x_ref, g_ref, nw_ref, sh_ref, sc_ref, o_ref, u_ref, *, lay, tm):
    grp = lay.group(pl.program_id(0) * tm)
    gate = g_ref[pl.ds(grp, 1), :]
    w = w_ref[...]
    rc = min(tm, 256)
    for r in range(0, tm, rc):
        xn = x_ref[r:r + rc, :] + gate * _dot(a_ref[r:r + rc, :], w)
        o_ref[r:r + rc, :] = xn
        u_ref[r:r + rc, :] = _mod_value(xn, nw_ref, sh_ref, sc_ref, grp).astype(BF16)


def _mm_res_call(lay, a, w_bf, x, mods, l, mod2):
    t, kdim = a.shape
    d = x.shape[1]
    tm = lay.row_tile(512)
    return pl.pallas_call(
        functools.partial(_mm_res_kernel, lay=lay, tm=tm),
        out_shape=(jax.ShapeDtypeStruct(x.shape, F32), jax.ShapeDtypeStruct(x.shape, BF16)),
        grid=(t // tm,),
        in_specs=[
            pl.BlockSpec((tm, kdim), lambda i: (i, 0)),
            pl.BlockSpec((kdim, d), lambda i: (0, 0)),
            pl.BlockSpec((tm, d), lambda i: (i, 0)),
            lay.mod_spec(l, 2),
        ] + mod2.specs,
        out_specs=(pl.BlockSpec((tm, d), lambda i: (i, 0)), pl.BlockSpec((tm, d), lambda i: (i, 0))),
        compiler_params=_cparams(("parallel",)),
        name="mm_residual",
    )(a, w_bf, x, mods, *mod2.args)


def _fnet_kernel(x_ref, u_ref, u2_any, gt_ref, wc_ref, ds_ref, wo_ref, nw_ref, sh_ref, sc_ref, o_ref, u2_ref,
                 ab_scr, *, lay, row_base, groups, norm):
    s, d = x_ref.shape
    cg = d // groups
    grp = lay.group(row_base + pl.program_id(0) * s)
    wc = wc_ref[...]
    rc = min(s, 256)
    for g in range(groups):
        for r in range(0, s, rc):
            ab = _dot(u_ref[r:r + rc, g * cg:(g + 1) * cg], wc)
            ab_scr[r:r + rc, g * cg:(g + 1) * cg] = ab[:, :cg].astype(BF16)
            ab_scr[s + r:s + r + rc, g * cg:(g + 1) * cg] = ab[:, cg:].astype(BF16)
    gate = gt_ref[pl.ds(grp, 1), :]
    wo = wo_ref[...]
    for r in range(0, s, rc):
        y = _dot(ds_ref[r:r + rc, :], ab_scr[...]) * norm
        xn = x_ref[r:r + rc, :] + gate * _dot(y.astype(BF16), wo)
        o_ref[r:r + rc, :] = xn
        u2_ref[r:r + rc, :] = _mod_value(xn, nw_ref, sh_ref, sc_ref, grp).astype(BF16)


def _dft_mats(s, cg):
    kc = np.arange(cg)
    ang_c = 2.0 * np.pi * np.outer(kc, kc) / cg
    wc = np.concatenate([np.cos(ang_c), np.sin(ang_c)], axis=1)
    ks = np.arange(s)
    ang_s = 2.0 * np.pi * np.outer(ks, ks) / s
    ds = np.concatenate([np.cos(ang_s), -np.sin(ang_s)], axis=1)
    return jnp.asarray(wc, dtype=BF16), jnp.asarray(ds, dtype=BF16)


def _fnet_call(lay, x, u1, u2, mods, l, wo_bf, mod2, prompt):
    nb, s = (lay.nbp, lay.sp) if prompt else (lay.nbs, lay.ss)
    rb0 = 0 if prompt else lay.tp // s
    d = lay.d
    cg = d // FNET_GROUPS
    wc, ds = _dft_mats(s, cg)
    kern = functools.partial(_fnet_kernel, lay=lay, row_base=rb0 * s, groups=FNET_GROUPS,
                             norm=1.0 / math.sqrt(s * cg))
    blk = pl.BlockSpec((s, d), lambda b: (rb0 + b, 0))
    return pl.pallas_call(
        kern,
        out_shape=(jax.ShapeDtypeStruct(x.shape, F32), jax.ShapeDtypeStruct(u2.shape, BF16)),
        grid=(nb,),
        in_specs=[
            blk,
            blk,
            pl.BlockSpec(memory_space=pl.ANY),
            lay.mod_spec(l, 2),
            pl.BlockSpec((cg, 2 * cg), lambda b: (0, 0)),
            pl.BlockSpec((s, 2 * s), lambda b: (0, 0)),
            pl.BlockSpec((d, d), lambda b: (0, 0)),
        ] + mod2.specs,
        out_specs=(blk, blk),
        scratch_shapes=[pltpu.VMEM((2 * s, d), BF16)],
        input_output_aliases={0: 0, 2: 1},
        compiler_params=_cparams(("parallel",)),
        name="fnet_prompt" if prompt else "fnet_latent",
    )(x, u1, u2, mods, wc, ds, wo_bf, *mod2.args)


def _glu_kernel(u_ref, wa_ref, wg_ref, ba_ref, bg_ref, o_ref):
    u = u_ref[...]
    a = _dot(u, wa_ref[...]) + ba_ref[...]
    g = _dot(u, wg_ref[...]) + bg_ref[...]
    o_ref[...] = a * _sigmoid(g)


def _glu_call(lay, u, w_bf, bias):
    t, d = u.shape
    cd = w_bf.shape[1] // 2
    tm = lay.row_tile(1024)
    tn = 512
    nj = cd // tn
    return pl.pallas_call(
        _glu_kernel,
        out_shape=jax.ShapeDtypeStruct((t, cd), F32),
        grid=(t // tm, nj),
        in_specs=[
            pl.BlockSpec((tm, d), lambda i, j: (i, 0)),
            pl.BlockSpec((d, tn), lambda i, j: (0, j)),
            pl.BlockSpec((d, tn), lambda i, j: (0, nj + j)),
            pl.BlockSpec((1, tn), lambda i, j: (0, j)),
            pl.BlockSpec((1, tn), lambda i, j: (0, nj + j)),
        ],
        out_specs=pl.BlockSpec((tm, tn), lambda i, j: (i, j)),
        compiler_params=_cparams(("parallel", "parallel")),
        name="conv_glu",
    )(u, w_bf, w_bf, bias, bias)


def _conv_kernel(c_ref, p_ref, n_ref, wd_ref, bd_ref, lw_ref, lb_ref, w2_ref, b2_ref, x_ref, gt_ref,
                 nw_ref, sh_ref, sc_ref, o_ref, u2_ref, pad, conv, act, *, lay, rb, width):
    i = pl.program_id(0)
    row0 = i * rb
    grp = lay.group(row0)
    seq = jnp.where(row0 < lay.tp, lay.sp, lay.ss)
    pos = jnp.where(row0 < lay.tp, row0 % lay.sp, (row0 - lay.tp) % lay.ss)
    has_prev = (pos != 0).astype(F32)
    has_next = (pos + rb != seq).astype(F32)
    hl = CONV_HALO
    half = width // 2
    cd = c_ref.shape[1]
    span = pad.shape[1]
    pad[0, 0:hl, :] = p_ref[...] * has_prev
    pad[0, hl:hl + rb, :] = c_ref[...]
    pad[0, hl + rb:hl + rb + hl, :] = n_ref[...] * has_next
    for s in range(1, SUBLANES):
        pad[s, 0:span - SUBLANES, :] = pad[0, s:s + span - SUBLANES, :]
    ngrp = 8
    sub = ngrp * SUBLANES
    lanes = 2 * LANES
    assert rb % sub == 0 and cd % lanes == 0

    def conv_block(blk, carry):
        r0 = pl.multiple_of(blk * sub, sub)
        for c0 in range(0, cd, lanes):
            bias = bd_ref[:, c0:c0 + lanes]
            accs = [jnp.zeros((SUBLANES, lanes), F32) + bias for _ in range(ngrp)]
            for k in sorted(range(width), key=lambda k: ((hl - half + k) % SUBLANES, k)):
                q, s = divmod(hl - half + k, SUBLANES)
                wk = wd_ref[k, :, c0:c0 + lanes]
                for gi in range(ngrp):
                    win = pad[s, pl.ds(r0 + (q + gi) * SUBLANES, SUBLANES), c0:c0 + lanes]
                    accs[gi] = accs[gi] + win * wk
            conv[pl.ds(r0, sub), c0:c0 + lanes] = jnp.concatenate(accs, axis=0)
        return carry

    lax.fori_loop(0, rb // sub, conv_block, 0)
    lw = lw_ref[...]
    lb = lb_ref[...]
    lsub = min(rb, 16 * SUBLANES)

    def ln_block(blk, carry):
        r0 = pl.multiple_of(blk * lsub, lsub)
        acc = conv[pl.ds(r0, lsub), :]
        mu = jnp.mean(acc, axis=-1, keepdims=True)
        cen = acc - mu
        var = jnp.mean(cen * cen, axis=-1, keepdims=True)
        y = cen * lax.rsqrt(var + EPS) * lw + lb
        act[pl.ds(r0, lsub), :] = (y * _sigmoid(y)).astype(BF16)
        return carry

    lax.fori_loop(0, rb // lsub, ln_block, 0)
    xn = x_ref[...] + gt_ref[pl.ds(grp, 1), :] * (_dot(act[...], w2_ref[...]) + b2_ref[...])
    o_ref[...] = xn
    u2_ref[...] = _mod_value(xn, nw_ref, sh_ref, sc_ref, grp).astype(BF16)


def _conv_call(lay, glu, wd, bd, lw, lb, w2_bf, b2, x, mods, l, mod2):
    t, cd = glu.shape
    d = x.shape[1]
    rb = lay.row_tile(256)
    hl = CONV_HALO
    assert CONV_WIDTH // 2 <= hl and rb % hl == 0
    nhb = t // hl
    per = rb // hl
    wd_p = jnp.broadcast_to(wd[:, None, :], (CONV_WIDTH, SUBLANES, cd))
    row = lambda a: a.reshape(1, -1)
    rows = pl.BlockSpec((rb, d), lambda i: (i, 0))
    return pl.pallas_call(
        functools.partial(_conv_kernel, lay=lay, rb=rb, width=CONV_WIDTH),
        out_shape=(jax.ShapeDtypeStruct(x.shape, F32), jax.ShapeDtypeStruct(x.shape, BF16)),
        grid=(t // rb,),
        in_specs=[
            pl.BlockSpec((rb, cd), lambda i: (i, 0)),
            pl.BlockSpec((hl, cd), lambda i: (jnp.maximum(i * per - 1, 0), 0)),
            pl.BlockSpec((hl, cd), lambda i: (jnp.minimum((i + 1) * per, nhb - 1), 0)),
            pl.BlockSpec(wd_p.shape, lambda i: (0, 0, 0)),
            pl.BlockSpec((1, cd), lambda i: (0, 0)),
            pl.BlockSpec((1, cd), lambda i: (0, 0)),
            pl.BlockSpec((1, cd), lambda i: (0, 0)),
            pl.BlockSpec((cd, d), lambda i: (0, 0)),
            pl.BlockSpec((1, d), lambda i: (0, 0)),
            rows,
            lay.mod_spec(l, 2),
        ] + mod2.specs,
        out_specs=(rows, rows),
        scratch_shapes=[pltpu.VMEM((SUBLANES, rb + 2 * hl, cd), F32), pltpu.VMEM((rb, cd), F32),
                        pltpu.VMEM((rb, cd), BF16)],
        compiler_params=_cparams(("parallel",)),
        name="conv_dw_ln_pw2",
    )(glu, glu, glu, wd_p, row(bd), row(lw), row(lb), w2_bf, row(b2), x, mods, *mod2.args)


def _router_kernel(u_ref, wr_ref, tri_ref, upper_ref, o_ref, cnt_ref, *, ng, ne):
    logits = _dot(u_ref[...], wr_ref[...])
    lane = lax.broadcasted_iota(jnp.int32, logits.shape, 1)
    big = jnp.int32(4 * LANES)
    neg = -jnp.inf
    epg = ne // ng

    gl = jnp.where(lane < ng, logits, neg)
    gmax = jnp.max(gl, axis=-1, keepdims=True)
    gidx = jnp.min(jnp.where(gl == gmax, lane, big), axis=-1, keepdims=True)
    g_p = 1.0 / jnp.sum(jnp.where(lane < ng, jnp.exp(logits - gmax), 0.0), axis=-1, keepdims=True)

    lo = ng + gidx * epg
    el = jnp.where((lane >= lo) & (lane < lo + epg), logits, neg)
    v1 = jnp.max(el, axis=-1, keepdims=True)
    i1 = jnp.min(jnp.where(el == v1, lane, big), axis=-1, keepdims=True)
    el2 = jnp.where(lane == i1, neg, el)
    v2 = jnp.max(el2, axis=-1, keepdims=True)
    i2 = jnp.min(jnp.where(el2 == v2, lane, big), axis=-1, keepdims=True)
    e1 = i1 - ng
    e2 = i2 - ng
    tt = jnp.exp(v2 - v1)
    p1 = 1.0 / (1.0 + tt)
    gate1 = p1 * g_p
    gate2 = (tt * p1) * g_p

    oh1 = lane == e1
    oh2 = lane == e2
    oh = jnp.where(oh1 | oh2, 1.0, 0.0)
    groups = jnp.floor((jnp.sum(oh, axis=0, keepdims=True) + (SUBLANES - 1)) * (1.0 / SUBLANES))
    groups8 = jnp.broadcast_to(groups, (SUBLANES, LANES))
    start = SUBLANES * _dot(groups8.astype(BF16), upper_ref[...])[0:1, :]
    prefix = _dot(tri_ref[...], oh.astype(BF16)) + start
    pos1 = jnp.sum(jnp.where(oh1, prefix, 0.0), axis=-1, keepdims=True)
    pos2 = jnp.sum(jnp.where(oh2, prefix, 0.0), axis=-1, keepdims=True)
    cnt_ref[...] = groups8

    out = jnp.where(lane == 0, e1.astype(F32), 0.0)
    out = jnp.where(lane == 1, e2.astype(F32), out)
    out = jnp.where(lane == 2, gate1, out)
    out = jnp.where(lane == 3, gate2, out)
    out = jnp.where(lane == 4, pos1, out)
    out = jnp.where(lane == 5, pos2, out)
    o_ref[...] = out


def _router_call(u, wr_bf, tri, upper, ng, ne):
    t, d = u.shape
    tm = tri.shape[0]
    nt = t // tm
    return pl.pallas_call(
        functools.partial(_router_kernel, ng=ng, ne=ne),
        out_shape=(jax.ShapeDtypeStruct((t, LANES), F32), jax.ShapeDtypeStruct((nt, SUBLANES, LANES), F32)),
        grid=(nt,),
        in_specs=[
            pl.BlockSpec((tm, d), lambda i: (i, 0)),
            pl.BlockSpec((d, LANES), lambda i: (0, 0)),
            pl.BlockSpec((tm, tm), lambda i: (0, 0)),
            pl.BlockSpec((LANES, LANES), lambda i: (0, 0)),
        ],
        out_specs=(pl.BlockSpec((tm, LANES), lambda i: (i, 0)),
                   pl.BlockSpec((None, SUBLANES, LANES), lambda i: (i, 0, 0))),
        compiler_params=_cparams(("parallel",)),
        name="moe_router",
    )(u, wr_bf, tri, upper)


def _pack_halves(lo, hi):
    lo_bits = lax.shift_right_logical(pltpu.bitcast(lo, U32), jnp.uint32(16))
    hi_bits = pltpu.bitcast(hi, U32) & jnp.uint32(0xFFFF0000)
    return hi_bits | lo_bits


def _unpack_halves(w):
    lo = pltpu.bitcast(lax.shift_left(w, jnp.uint32(16)), F32)
    hi = pltpu.bitcast(w & jnp.uint32(0xFFFF0000), F32)
    return lo.astype(BF16), hi.astype(BF16)


def _round_bf16(x):
    return x.astype(BF16).astype(F32)


def _group_copy(src, src_g, dst, dst_g, sem):
    g8 = lambda g: pl.ds(pl.multiple_of(g * SUBLANES, SUBLANES), SUBLANES)
    return pltpu.make_async_copy(src.at[g8(src_g), :], dst.at[g8(dst_g), :], sem)


def _for_groups(n, fn):
    def body(g, c):
        fn(g)
        return c
    lax.fori_loop(0, n, body, 0)


def _dispatch_kernel(gdst_ref, ngt_ref, pad0_ref, npad_ref, tail_ref, u_ref, pos_ref, xs_out, loc, zeros, sem,
                     zsem, *, tm, nl, ne):
    i = pl.program_id(0)
    nt = pl.num_programs(0)
    slot = i % 2
    nlg = nl // SUBLANES

    def copy(step, s, g):
        return _group_copy(loc.at[s], g, xs_out, gdst_ref[step * nlg + g], sem.at[s])

    def zero_copy(e, g):
        return _group_copy(zeros, 0, xs_out, pad0_ref[e] + g, zsem)

    def zero_block(t):
        first = pl.multiple_of(tail_ref[0] + t * MOE_BM, MOE_BM)
        return pltpu.make_async_copy(zeros, xs_out.at[pl.ds(first, MOE_BM), :], zsem)

    @pl.when(i == 0)
    def _():
        zeros[...] = jnp.zeros(zeros.shape, zeros.dtype)
        for e in range(ne):
            _for_groups(npad_ref[e], lambda g, e=e: zero_copy(e, g).start())
        _for_groups(tail_ref[1], lambda t: zero_block(t).start())

    @pl.when(i >= 2)
    def _():
        _for_groups(ngt_ref[i - 2], lambda g: copy(i - 2, slot, g).wait())

    pos1 = pos_ref[0:1, :]
    pos2 = pos_ref[1:2, :]
    half = u_ref.shape[1] // 2
    u = u_ref[...]
    rc = 256
    for r in range(0, nl, rc):
        p = lax.broadcasted_iota(jnp.int32, (rc, tm), 0) + r
        onehot = jnp.where((p == pos1) | (p == pos2), 1.0, 0.0).astype(BF16)
        rows = _dot(onehot, u)
        loc[slot, r:r + rc, :] = _pack_halves(rows[:, :half], rows[:, half:])
    _for_groups(ngt_ref[i], lambda g: copy(i, slot, g).start())

    @pl.when(i == nt - 1)
    def _():
        @pl.when(i >= 1)
        def _():
            _for_groups(ngt_ref[i - 1], lambda g: copy(i - 1, 1 - slot, g).wait())

        _for_groups(ngt_ref[i], lambda g: copy(i, slot, g).wait())
        for e in range(ne):
            _for_groups(npad_ref[e], lambda g, e=e: zero_copy(e, g).wait())
        _for_groups(tail_ref[1], lambda t: zero_block(t).wait())


def _dispatch_call(u, pos_rows, gdst, ngt, pad0, npad, tail, nrows, tm, nl):
    t, d = u.shape
    ne = npad.shape[0]
    grid_spec = pltpu.PrefetchScalarGridSpec(
        num_scalar_prefetch=5,
        grid=(t // tm,),
        in_specs=[
            pl.BlockSpec((tm, d), lambda i, *_: (i, 0)),
            pl.BlockSpec((None, SUBLANES, tm), lambda i, *_: (i, 0, 0)),
        ],
        out_specs=pl.BlockSpec(memory_space=pl.ANY),
        scratch_shapes=[pltpu.VMEM((2, nl, d // 2), U32), pltpu.VMEM((MOE_BM, d // 2), U32),
                        pltpu.SemaphoreType.DMA((2,)), pltpu.SemaphoreType.DMA(())],
    )
    return pl.pallas_call(
        functools.partial(_dispatch_kernel, tm=tm, nl=nl, ne=ne),
        out_shape=jax.ShapeDtypeStruct((nrows, d // 2), U32),
        grid_spec=grid_spec,
        compiler_params=_cparams(("arbitrary",)),
        name="moe_dispatch",
    )(gdst, ngt, pad0, npad, tail, u, pos_rows)


def _expert_kernel(row0_ref, nblk_ref, tail_ref, x_hbm, w13_ref, w2_ref, y_hbm, xbuf, ybuf, w13_bf, w2_bf,
                   xsem, ysem, *, hid):
    e = pl.program_id(0)
    n = nblk_ref[e]
    g0 = row0_ref[e] // MOE_BM
    total = tail_ref[0] // MOE_BM
    nx, ny = xbuf.shape[0], ybuf.shape[0]
    ahead = nx - 1
    half = xbuf.shape[2]

    def rows(g):
        return pl.ds(pl.multiple_of(g * MOE_BM, MOE_BM), MOE_BM)

    def x_copy(g):
        return pltpu.make_async_copy(x_hbm.at[rows(g), :], xbuf.at[g % nx], xsem.at[g % nx])

    def y_copy(g, s):
        return pltpu.make_async_copy(ybuf.at[s], y_hbm.at[rows(g), :], ysem.at[s])

    @pl.when(e == 0)
    def _():
        for g in range(ahead):
            @pl.when(g < total)
            def _(g=g):
                x_copy(g).start()

    @pl.when(n > 0)
    def _():
        w13_bf[...] = w13_ref[...].astype(BF16)
        w2_bf[...] = w2_ref[...].astype(BF16)

        def block(c, carry):
            g = g0 + c
            x_copy(g).wait()

            @pl.when(g + ahead < total)
            def _():
                x_copy(g + ahead).start()

            @pl.when(g >= ny)
            def _():
                y_copy(g - ny, g % ny).wait()

            x_lo, x_hi = _unpack_halves(xbuf[g % nx])
            hb = _dot(x_lo, w13_bf[:half, :]) + _dot(x_hi, w13_bf[half:, :])
            a = hb[:, :hid]
            act = (a * _sigmoid(a)) * hb[:, hid:]
            y = _round_bf16(_dot(act.astype(BF16), w2_bf[...]))
            ybuf[g % ny] = _pack_halves(y[:, :half], y[:, half:])
            y_copy(g, g % ny).start()
            return carry

        lax.fori_loop(0, n, block, 0)

    @pl.when(e == pl.num_programs(0) - 1)
    def _():
        for j in range(ny):
            @pl.when(total - ny + j >= 0)
            def _(j=j):
                g = total - ny + j
                y_copy(g, g % ny).wait()

        ybuf[0] = jnp.zeros(ybuf.shape[1:], ybuf.dtype)
        _for_groups(tail_ref[1], lambda t: y_copy(total + t, 0).start())
        _for_groups(tail_ref[1], lambda t: y_copy(total + t, 0).wait())


def _expert_call(xs, row0, nblk, tail, w13, w2, l):
    r, half = xs.shape
    d = 2 * half
    ne = w13.shape[1]
    hid = w2.shape[2]
    nx, ny = 5, 3
    grid_spec = pltpu.PrefetchScalarGridSpec(
        num_scalar_prefetch=3,
        grid=(ne,),
        in_specs=[
            pl.BlockSpec(memory_space=pl.ANY),
            pl.BlockSpec((None, None, d, 2 * hid), lambda e, *_: (l, e, 0, 0)),
            pl.BlockSpec((None, None, hid, d), lambda e, *_: (l, e, 0, 0)),
        ],
        out_specs=pl.BlockSpec(memory_space=pl.ANY),
        scratch_shapes=[
            pltpu.VMEM((nx, MOE_BM, half), U32), pltpu.VMEM((ny, MOE_BM, half), U32),
            pltpu.VMEM((d, 2 * hid), BF16), pltpu.VMEM((hid, d), BF16),
            pltpu.SemaphoreType.DMA((nx,)), pltpu.SemaphoreType.DMA((ny,)),
        ],
    )
    return pl.pallas_call(
        functools.partial(_expert_kernel, hid=hid),
        out_shape=jax.ShapeDtypeStruct((r, half), U32),
        grid_spec=grid_spec,
        compiler_params=_cparams(("arbitrary",)),
        name="moe_experts",
    )(row0, nblk, tail, xs, w13, w2)


def _combine_kernel(gdst_ref, ngt_ref, x_ref, rt_ref, gt_ref, nw_ref, sh_ref, sc_ref, ys_ref, out_a, out_b,
                    loc, sem, *, lay, tm, nl, final):
    i = pl.program_id(0)
    nt = pl.num_programs(0)
    slot = i % 2
    nlg = nl // SUBLANES

    def copy(step, s, g):
        return _group_copy(ys_ref, gdst_ref[step * nlg + g], loc.at[s], g, sem.at[s])

    @pl.when(i == 0)
    def _():
        loc[...] = jnp.zeros(loc.shape, loc.dtype)
        _for_groups(ngt_ref[0], lambda g: copy(0, 0, g).start())

    _for_groups(ngt_ref[i], lambda g: copy(i, slot, g).wait())

    @pl.when(i + 1 < nt)
    def _():
        _for_groups(ngt_ref[i + 1], lambda g: copy(i + 1, 1 - slot, g).start())

    rt = rt_ref[...]
    gate1, gate2 = rt[:, 2:3], rt[:, 3:4]
    pos1, pos2 = rt[:, 4:5].astype(jnp.int32), rt[:, 5:6].astype(jnp.int32)
    half = loc.shape[2]
    rc = 256
    mix_lo = jnp.zeros((tm, half), F32)
    mix_hi = jnp.zeros((tm, half), F32)
    for r in range(0, nl, rc):
        p = lax.broadcasted_iota(jnp.int32, (tm, rc), 1) + r
        wgt = (jnp.where(p == pos1, gate1, 0.0) + jnp.where(p == pos2, gate2, 0.0)).astype(BF16)
        y_lo, y_hi = _unpack_halves(loc[slot, r:r + rc, :])
        mix_lo = mix_lo + _dot(wgt, y_lo)
        mix_hi = mix_hi + _dot(wgt, y_hi)
    grp = lay.group(i * tm)
    gate = gt_ref[pl.ds(grp, 1), :]
    x_lo = x_ref[:, :half] + gate[:, :half] * mix_lo
    x_hi = x_ref[:, half:] + gate[:, half:] * mix_hi
    ms = (jnp.sum(x_lo * x_lo, axis=-1, keepdims=True) + jnp.sum(x_hi * x_hi, axis=-1, keepdims=True)) / (2 * half)
    inv = lax.rsqrt(ms + EPS)
    nw = nw_ref[...]
    if final:
        y_lo = x_lo * inv * nw[:, :half]
        y_hi = x_hi * inv * nw[:, half:]
        for ref, cond in ((out_a, i * tm < lay.tp), (out_b, i * tm >= lay.tp)):
            @pl.when(cond)
            def _(ref=ref):
                ref[:, :half] = y_lo
                ref[:, half:] = y_hi
    else:
        out_a[:, :half] = x_lo
        out_a[:, half:] = x_hi
        sh = sh_ref[pl.ds(grp, 1), :]
        sc = sc_ref[pl.ds(grp, 1), :]
        out_b[:, :half] = (x_lo * inv * nw[:, :half] * (1.0 + sc[:, :half]) + sh[:, :half]).astype(BF16)
        out_b[:, half:] = (x_hi * inv * nw[:, half:] * (1.0 + sc[:, half:]) + sh[:, half:]).astype(BF16)


def _combine_call(lay, x, route, mods, l, mod_next, gdst, ngt, ys, tm, nl, final):
    t, d = x.shape
    rows = pl.BlockSpec((tm, d), lambda i, *_: (i, 0))
    if final:
        ntp = lay.tp // tm
        out_specs = (pl.BlockSpec((tm, d), lambda i, *_: (jnp.minimum(i, ntp - 1), 0)),
                     pl.BlockSpec((tm, d), lambda i, *_: (jnp.maximum(i - ntp, 0), 0)))
        out_shape = (jax.ShapeDtypeStruct((lay.tp, d), F32), jax.ShapeDtypeStruct((lay.ts, d), F32))
    else:
        out_specs = (rows, rows)
        out_shape = (jax.ShapeDtypeStruct(x.shape, F32), jax.ShapeDtypeStruct(x.shape, BF16))
    grid_spec = pltpu.PrefetchScalarGridSpec(
        num_scalar_prefetch=2,
        grid=(t // tm,),
        in_specs=[
            rows,
            pl.BlockSpec((tm, LANES), lambda i, *_: (i, 0)),
            lay.mod_spec(l, 5),
        ] + mod_next.specs + [pl.BlockSpec(memory_space=pl.ANY)],
        out_specs=out_specs,
        scratch_shapes=[pltpu.VMEM((2, nl, d // 2), U32), pltpu.SemaphoreType.DMA((2,))],
    )
    return pl.pallas_call(
        functools.partial(_combine_kernel, lay=lay, tm=tm, nl=nl, final=final),
        out_shape=out_shape,
        grid_spec=grid_spec,
        compiler_params=_cparams(("arbitrary",)),
        name="moe_combine_final" if final else "moe_combine",
    )(gdst, ngt, x, route, mods, *mod_next.args, ys)


def _moe_layer(lay, x, u2, mods, l, mod_next, final, wr_bf, tri_r, upper, w13, w2, ng, ne):
    t, d = x.shape
    tm = tri_r.shape[0]
    nt = t // tm
    bmg = MOE_BM // SUBLANES
    nl = -(-(MOE_TOP_K * tm + ne * (SUBLANES - 1)) // 256) * 256
    nlg = nl // SUBLANES
    route, counts = _router_call(u2, wr_bf, tri_r, upper, ng, ne)

    c8 = counts[:, 0, :ne].astype(jnp.int32)
    lend = jnp.cumsum(c8, axis=1)
    lstart = lend - c8
    ngt = lend[:, -1].astype(jnp.int32)
    tot = jnp.sum(c8, axis=0)
    padded = (tot + bmg - 1) // bmg * bmg
    gend = jnp.cumsum(padded)
    gbase = (gend - padded)[None, :] + jnp.cumsum(c8, axis=0) - c8
    nb = -(-(MOE_TOP_K * t + nt * ne * (SUBLANES - 1)) // MOE_BM) + ne
    g = jnp.arange(nlg, dtype=jnp.int32)[None, :, None]
    owner = (g >= lstart[:, None, :]) & (g < lend[:, None, :])
    gdst = g[:, :, 0] + jnp.sum(jnp.where(owner, (gbase - lstart)[:, None, :], 0), axis=-1)
    gdst = gdst.reshape(nt * nlg).astype(jnp.int32)
    row0 = ((gend - padded) * SUBLANES).astype(jnp.int32)
    nblk = (padded // bmg).astype(jnp.int32)
    tail = jnp.stack([gend[-1] * SUBLANES, nb - gend[-1] // bmg]).astype(jnp.int32)
    pos = route[:, 2 * MOE_TOP_K:3 * MOE_TOP_K].astype(jnp.int32).reshape(nt, tm, MOE_TOP_K)
    pos_rows = jnp.full((nt, SUBLANES, tm), -1, jnp.int32).at[:, :MOE_TOP_K, :].set(jnp.swapaxes(pos, 1, 2))

    pad0 = (gend - padded + tot).astype(jnp.int32)
    npad = (padded - tot).astype(jnp.int32)
    xs = _dispatch_call(u2, pos_rows, gdst, ngt, pad0, npad, tail, nb * MOE_BM, tm, nl)
    ys = _expert_call(xs, row0, nblk, tail, w13, w2, l)
    return _combine_call(lay, x, route, mods, l, mod_next, gdst, ngt, ys, tm, nl, final)


def _lower_tri(n, strict):
    r = np.arange(n)
    m = (r[None, :] < r[:, None]) if strict else (r[None, :] <= r[:, None])
    return jnp.asarray(m.astype(np.float32), dtype=BF16)


def kernel(x_prompt, x_sample, state_C, state_n, state_m, c, c_ctx, ada_w, ada_b, norm1_w, norm2_w, m_w_in, m_b_gate, m_head_norm_w, m_w_out, f_w_out, cv_w_pw1, cv_b_pw1, cv_w_dw, cv_b_dw, cv_ln_w, cv_ln_b, cv_w_pw2, cv_b_pw2, r_w_group, r_w_expert, e_w13, e_w2, final_norm_w):
    nbp, sp, d = x_prompt.shape
    nbs, ss, _ = x_sample.shape
    assert ss % GRID_W == 0
    lay = _Layout(nbp, sp, nbs, ss, d)
    depth = ada_w.shape[0]
    nh, dh = state_C.shape[3], state_C.shape[4]
    di = nh * dh
    ng = r_w_group.shape[2]
    ne = r_w_expert.shape[2]
    assert ng == MOE_GROUPS and ng + ne <= LANES and 4 * nh <= LANES and MOE_TOP_K == 2

    cv = jnp.zeros((lay.ngp, d), F32).at[0].set(c_ctx).at[1:1 + nbs].set(c)
    mods = _ada_call(cv, ada_w, ada_b)

    tri_l = _lower_tri(MLSTM_L, strict=False)
    tri_r = _lower_tri(lay.row_tile(512), strict=True)
    upper = _lower_tri(LANES, strict=True).T
    row = lambda a: a.reshape(1, -1)
    w_in_t = jnp.swapaxes(m_w_in, 1, 2)
    tn = 1024
    assert di % tn == 0
    nkb = di // tn

    x, u1 = _prep_call(lay, x_prompt.reshape(lay.tp, d), x_sample.reshape(lay.ts, d),
                       _Mod(lay, mods, norm1_w[0], 0, 0))
    y = None
    qvos, kts, gpts = [], [], []
    for l in range(depth):
        j, kind = l // N_MIXERS, l % N_MIXERS
        mod2 = _Mod(lay, mods, norm2_w[l], l, 3)
        if kind == 0:
            wg = jnp.zeros((LANES, d), F32).at[:4 * nh].set(w_in_t[j, 4 * di:])
            bg = jnp.zeros((1, LANES), F32).at[0, :4 * nh].set(m_b_gate[j])
            gp = _gates_call(lay, x, _Mod(lay, mods, norm1_w[l], l, 0), wg, bg, tri_l, nh)
            gpt = gp[:, :4 * nh].T
            qvo_blocks = list(range(nkb)) + list(range(2 * nkb, 4 * nkb))
            qvo = _proj_call(lay, u1, w_in_t, j, qvo_blocks, tn, transposed=False)
            kt = _proj_call(lay, u1, w_in_t, j, list(range(nkb, 2 * nkb)), tn, transposed=True)
            hw = row(m_head_norm_w[j])
            hg = jnp.zeros((lay.t, di), BF16)
            hg = _mlstm_call(lay, hg, qvo, kt, gp, gpt, hw, nh, dh, prompt=True)
            n0 = state_n[:, j].reshape(nbs, 2 * nh, dh)
            m0 = jnp.broadcast_to(state_m[:, j].reshape(nbs, 2 * nh, 1), (nbs, 2 * nh, LANES))
            hg = _mlstm_call(lay, hg, qvo, kt, gp, gpt, hw, nh, dh, prompt=False, state=(state_C, j, n0, m0))
            x, u2 = _mm_res_call(lay, hg, m_w_out[j].astype(BF16), x, mods, l, mod2)
            qvos.append(qvo)
            kts.append(kt)
            gpts.append(gpt)
        elif kind == 1:
            wo = f_w_out[j].astype(BF16)
            u2 = jnp.zeros((lay.t, d), BF16)
            x, u2 = _fnet_call(lay, x, u1, u2, mods, l, wo, mod2, prompt=True)
            x, u2 = _fnet_call(lay, x, u1, u2, mods, l, wo, mod2, prompt=False)
        else:
            glu = _glu_call(lay, u1, cv_w_pw1[j].astype(BF16), row(cv_b_pw1[j]))
            x, u2 = _conv_call(lay, glu, cv_w_dw[j], cv_b_dw[j], cv_ln_w[j], cv_ln_b[j], cv_w_pw2[j].astype(BF16),
                               cv_b_pw2[j], x, mods, l, mod2)
        wr = jnp.zeros((d, LANES), F32).at[:, :ng].set(r_w_group[l]).at[:, ng:ng + ne].set(r_w_expert[l])
        final = l + 1 == depth
        mod_next = _Mod(lay, mods, final_norm_w, l, 0) if final else _Mod(lay, mods, norm1_w[l + 1], l + 1, 0)
        outs = _moe_layer(lay, x, u2, mods, l, mod_next, final, wr.astype(BF16), tri_r, upper, e_w13, e_w2, ng, ne)
        if final:
            y = outs
        else:
            x, u1 = outs

    y_prompt = y[0].reshape(nbp, sp, d)
    y_sample = y[1].reshape(nbs, ss, d)
    new_c, new_n, new_m = _state_call(lay, qvos, kts, gpts, nh, dh)
    return (y_prompt, y_sample, new_c, new_n, new_m)
```

```python
import functools
import math

import numpy as np
import jax
import jax.numpy as jnp
from jax import lax
from jax.experimental import pallas as pl
from jax.experimental.pallas import tpu as pltpu

F32 = jnp.float32
BF16 = jnp.bfloat16
U32 = jnp.uint32
EPS = 1e-6
GRID_W = 64
N_MIXERS = 3
FNET_GROUPS = 4
CONV_WIDTH = 31
MOE_GROUPS = 4
MOE_TOP_K = 2

LANES = 128
SUBLANES = 8
MLSTM_L = 256
MOE_BM = 256
CONV_HALO = 16
VMEM_LIMIT = 56 * 1024 * 1024


def _cparams(sem, vmem=VMEM_LIMIT):
    return pltpu.CompilerParams(dimension_semantics=sem, vmem_limit_bytes=vmem)


def _dot(a, b):
    return jnp.dot(a, b, preferred_element_type=F32)


def _dot_nt(a, b):
    return lax.dot_general(a, b, (((1,), (1,)), ((), ())), preferred_element_type=F32)


def _rms(x, w):
    return x * lax.rsqrt(jnp.mean(x * x, axis=-1, keepdims=True) + EPS) * w


def _modulate(x, w, shift, scale):
    return _rms(x, w) * (1.0 + scale) + shift


def _sigmoid(x):
    return 1.0 / (1.0 + jnp.exp(-x))


def _log_sigmoid(x):
    return jnp.minimum(x, 0.0) - jnp.log(1.0 + jnp.exp(-jnp.abs(x)))


def _split2(x):
    hi = x.astype(BF16)
    return hi, (x - hi.astype(F32)).astype(BF16)


def _split3(x):
    hi = x.astype(BF16)
    r1 = x - hi.astype(F32)
    mid = r1.astype(BF16)
    return hi, mid, (r1 - mid.astype(F32)).astype(BF16)


class _Layout:
    def __init__(self, nbp, sp, nbs, ss, d):
        self.nbp, self.sp, self.nbs, self.ss, self.d = nbp, sp, nbs, ss, d
        self.tp, self.ts = nbp * sp, nbs * ss
        self.t = self.tp + self.ts
        assert self.tp % ss == 0, "latent sequences must start on a block boundary of their own length"
        self.ngp = -(-(1 + nbs) // SUBLANES) * SUBLANES

    def group(self, row0):
        return jnp.where(row0 < self.tp, 0, 1 + (row0 - self.tp) // self.ss)

    def row_tile(self, want):
        tm = math.gcd(math.gcd(self.tp, self.ss), want)
        assert tm % SUBLANES == 0
        return tm

    def mod_spec(self, l, chunk):
        return pl.BlockSpec((None, self.ngp, self.d), lambda *_: (l, 0, chunk))

    def row_spec(self):
        return pl.BlockSpec((1, self.d), lambda *_: (0, 0))


class _Mod:
    def __init__(self, lay, mods, nw, l, c_shift):
        self.args = (nw.reshape(1, -1), mods, mods)
        self.specs = [lay.row_spec(), lay.mod_spec(l, c_shift), lay.mod_spec(l, c_shift + 1)]


def _mod_value(x, nw_ref, sh_ref, sc_ref, grp):
    return _modulate(x, nw_ref[...], sh_ref[pl.ds(grp, 1), :], sc_ref[pl.ds(grp, 1), :])


def _ada_kernel(cv_ref, w_ref, b_ref, o_ref):
    s = cv_ref[...]
    s = s * _sigmoid(s)
    o_ref[...] = _dot(s.astype(BF16), w_ref[...].astype(BF16)) + b_ref[...]


def _ada_call(cv, ada_w, ada_b):
    depth, d, n = ada_w.shape
    ngp = cv.shape[0]
    tn = min(n, 2048)
    return pl.pallas_call(
        _ada_kernel,
        out_shape=jax.ShapeDtypeStruct((depth, ngp, n), F32),
        grid=(depth, n // tn),
        in_specs=[
            pl.BlockSpec((ngp, d), lambda l, j: (0, 0)),
            pl.BlockSpec((None, d, tn), lambda l, j: (l, 0, j)),
            pl.BlockSpec((None, 1, tn), lambda l, j: (l, 0, j)),
        ],
        out_specs=pl.BlockSpec((None, ngp, tn), lambda l, j: (l, 0, j)),
        compiler_params=_cparams(("parallel", "parallel")),
        name="ada_mods",
    )(cv, ada_w, ada_b.reshape(depth, 1, n))


def _prep_kernel(xp_ref, xs_ref, nw_ref, sh_ref, sc_ref, x_ref, u_ref, *, lay, tm):
    i = pl.program_id(0)
    grp = lay.group(i * tm)
    for src, cond in ((xp_ref, i * tm < lay.tp), (xs_ref, i * tm >= lay.tp)):
        @pl.when(cond)
        def _(src=src):
            rc = min(tm, 256)
            for r in range(0, tm, rc):
                x = src[r:r + rc, :]
                x_ref[r:r + rc, :] = x
                u_ref[r:r + rc, :] = _mod_value(x, nw_ref, sh_ref, sc_ref, grp).astype(BF16)


def _prep_call(lay, xp, xs, mod):
    d = lay.d
    tm = lay.row_tile(1024)
    ntp = lay.tp // tm
    rows = pl.BlockSpec((tm, d), lambda i: (i, 0))
    return pl.pallas_call(
        functools.partial(_prep_kernel, lay=lay, tm=tm),
        out_shape=(jax.ShapeDtypeStruct((lay.t, d), F32), jax.ShapeDtypeStruct((lay.t, d), BF16)),
        grid=(lay.t // tm,),
        in_specs=[
            pl.BlockSpec((tm, d), lambda i: (jnp.minimum(i, ntp - 1), 0)),
            pl.BlockSpec((tm, d), lambda i: (jnp.maximum(i - ntp, 0), 0)),
        ] + mod.specs,
        out_specs=(rows, rows),
        compiler_params=_cparams(("arbitrary",)),
        name="prep_modulate",
    )(xp, xs, *mod.args)


def _gates_kernel(x_ref, nw_ref, sh_ref, sc_ref, wh_ref, wl_ref, b_ref, tri_ref, o_ref, *, lay, nh, tm):
    l = tri_ref.shape[0]
    grp = lay.group(pl.program_id(0) * tm)
    wh = wh_ref[...]
    wl = wl_ref[...]
    tri = tri_ref[...]
    lane = lax.broadcasted_iota(jnp.int32, (l, LANES), 1)
    is_f = ((lane >= nh) & (lane < 2 * nh)) | ((lane >= 3 * nh) & (lane < 4 * nh))
    is_a = (lane < nh) | ((lane >= 2 * nh) & (lane < 3 * nh))
    for r in range(0, tm, l):
        u = _mod_value(x_ref[r:r + l, :], nw_ref, sh_ref, sc_ref, grp)
        u_hi, u_lo = _split2(u)
        g = _dot_nt(u_hi, wh) + _dot_nt(u_lo, wh) + _dot_nt(u_hi, wl) + b_ref[...]
        lf = jnp.where(is_f, _log_sigmoid(g), 0.0)
        hi, mid, lo = _split3(lf)
        prefix = _dot(tri, hi) + _dot(tri, mid) + _dot(tri, lo)
        suffix = jnp.sum(lf, axis=0, keepdims=True) - prefix + lf
        b = jnp.where(lane < 2 * nh, prefix, suffix)
        a = g - pltpu.roll(b, LANES - nh, 1)
        o_ref[r:r + l, :] = jnp.where(is_a, a, b)


def _gates_call(lay, x, mod, wg, bias, tri, nh):
    t, d = x.shape
    l = tri.shape[0]
    tm = lay.row_tile(1024)
    assert tm % l == 0
    wh, wl = _split2(wg)
    return pl.pallas_call(
        functools.partial(_gates_kernel, lay=lay, nh=nh, tm=tm),
        out_shape=jax.ShapeDtypeStruct((t, LANES), F32),
        grid=(t // tm,),
        in_specs=[pl.BlockSpec((tm, d), lambda i: (i, 0))] + mod.specs + [
            pl.BlockSpec((LANES, d), lambda i: (0, 0)),
            pl.BlockSpec((LANES, d), lambda i: (0, 0)),
            pl.BlockSpec((1, LANES), lambda i: (0, 0)),
            pl.BlockSpec((l, l), lambda i: (0, 0)),
        ],
        out_specs=pl.BlockSpec((tm, LANES), lambda i: (i, 0)),
        compiler_params=_cparams(("parallel",)),
        name="mlstm_gates",
    )(x, *mod.args, wh, wl, bias, tri)


def _proj_kernel(u_ref, w_ref, o_ref, w_bf, *, transposed):
    @pl.when(pl.program_id(1) == 0)
    def _():
        w_bf[...] = w_ref[...].astype(BF16)

    if transposed:
        o_ref[...] = _dot_nt(w_bf[...], u_ref[...]).astype(BF16)
    else:
        o_ref[...] = _dot_nt(u_ref[...], w_bf[...]).astype(BF16)


def _proj_call(lay, u, w_in_t, jl, blocks, tn, transposed):
    t, d = u.shape
    tm = lay.row_tile(1024)
    nb = len(blocks)
    first, gap_at, gap = blocks[0], None, 0
    for idx in range(1, nb):
        if blocks[idx] != blocks[idx - 1] + 1:
            assert gap_at is None
            gap_at, gap = idx, blocks[idx] - blocks[idx - 1] - 1
    wblk = (lambda j: first + j) if gap_at is None else (lambda j: first + j + jnp.where(j >= gap_at, gap, 0))
    if transposed:
        out_shape = jax.ShapeDtypeStruct((nb * tn, t), BF16)
        out_spec = pl.BlockSpec((tn, tm), lambda j, i: (j, i))
    else:
        out_shape = jax.ShapeDtypeStruct((t, nb * tn), BF16)
        out_spec = pl.BlockSpec((tm, tn), lambda j, i: (i, j))
    return pl.pallas_call(
        functools.partial(_proj_kernel, transposed=transposed),
        out_shape=out_shape,
        grid=(nb, t // tm),
        in_specs=[
            pl.BlockSpec((tm, d), lambda j, i: (i, 0)),
            pl.BlockSpec((None, tn, d), lambda j, i: (jl, wblk(j), 0)),
        ],
        out_specs=out_spec,
        scratch_shapes=[pltpu.VMEM((tn, d), BF16)],
        compiler_params=_cparams(("parallel", "arbitrary")),
        name="mlstm_proj_t" if transposed else "mlstm_proj",
    )(u, w_in_t)


def _col(tile, c):
    lane = lax.broadcasted_iota(jnp.int32, tile.shape, 1)
    return jnp.sum(jnp.where(lane == c, tile, 0.0), axis=-1, keepdims=True)


def _dir_masks(l):
    r = lax.broadcasted_iota(jnp.int32, (l, l), 0)
    c = lax.broadcasted_iota(jnp.int32, (l, l), 1)
    return c <= r, c >= r


def _head_epilogue(h, hw, o):
    hn = h * lax.rsqrt(jnp.mean(h * h, axis=-1, keepdims=True) + EPS) * hw
    return (hn * _sigmoid(o.astype(F32))).astype(BF16)


def _row_times_kt(w_row, kt):
    hi, lo = _split2(w_row)
    sub = lax.broadcasted_iota(jnp.int32, (SUBLANES, w_row.shape[1]), 0)
    stacked = jnp.where(sub == 0, hi.astype(F32), jnp.where(sub == 1, lo.astype(F32), 0.0)).astype(BF16)
    res = _dot_nt(stacked, kt)
    return res[0:1, :] + res[1:2, :]


def _mlstm_single_kernel(q_ref, kt_ref, v_ref, o_ref, gp_ref, gpt_ref, hw_ref, out_ref, *, nh, dh, scale):
    l = q_ref.shape[0]
    gp = gp_ref[...]
    masks = _dir_masks(l)
    for h in range(nh):
        cols = slice(h * dh, (h + 1) * dh)
        qk = _dot(q_ref[:, cols], kt_ref[cols, :])
        p = None
        for d in range(2):
            a_r = gpt_ref[2 * nh * d + h:2 * nh * d + h + 1, :]
            b_c = gp[:, 2 * nh * d + nh + h:2 * nh * d + nh + h + 1]
            g = jnp.where(masks[d], a_r, -jnp.inf)
            m = jnp.maximum(jnp.max(g, axis=-1, keepdims=True), 0.0)
            s = qk * jnp.exp(g - m) * scale
            den = jnp.sum(s, axis=-1, keepdims=True)
            inv = 1.0 / jnp.maximum(jnp.abs(den), jnp.exp(-(b_c + m)))
            p = s * inv if p is None else p + s * inv
        hh = _dot(p.astype(BF16), v_ref[:, cols])
        out_ref[:, cols] = _head_epilogue(hh, hw_ref[:, cols], o_ref[:, cols])


def _mlstm_multi_kernel(q_ref, kt_ref, v_ref, o_ref, gp_ref, gpt_ref, hw_ref, c0_ref, n0_ref, m0_ref,
                        out_ref, cst, cbf, *, nh, nc, l, scale):
    h = pl.program_id(1)
    masks = _dir_masks(l)
    m_in = [[None] * nc for _ in range(2)]
    n_in = [[None] * nc for _ in range(2)]
    for d in range(2):
        cst[...] = c0_ref[d]
        n = n0_ref[pl.ds(d * nh + h, 1), :]
        m = m0_ref[pl.ds(d * nh + h, 1), 0:1]
        order = list(range(nc)) if d == 0 else list(range(nc - 1, -1, -1))
        for step, c in enumerate(order):
            m_in[d][c], n_in[d][c] = m, n
            cbf[d, c] = cst[...].astype(BF16)
            if step + 1 < nc:
                r0 = c * l
                a_r = gpt_ref[pl.ds(2 * nh * d + h, 1), r0:r0 + l]
                b_r = gpt_ref[pl.ds(2 * nh * d + nh + h, 1), r0:r0 + l]
                m_last = jnp.maximum(jnp.max(a_r, axis=-1, keepdims=True), m)
                b_end = b_r[:, l - 1:l] if d == 0 else b_r[:, 0:1]
                decay = jnp.exp(m - m_last)
                w_end = jnp.exp(a_r - m_last)
                kt = kt_ref[:, r0:r0 + l]
                cst[...] = decay * cst[...] + _dot((kt.astype(F32) * w_end).astype(BF16), v_ref[r0:r0 + l, :])
                n = decay * n + _row_times_kt(w_end, kt)
                m = b_end + m_last
    hw = hw_ref[...]
    for c in range(nc):
        r0 = c * l
        q = q_ref[r0:r0 + l, :]
        v = v_ref[r0:r0 + l, :]
        qk = _dot(q, kt_ref[:, r0:r0 + l])
        qf = q.astype(F32)
        gp = gp_ref[r0:r0 + l, :]
        p = inter = None
        for d in range(2):
            m, n = m_in[d][c], n_in[d][c]
            a_r = gpt_ref[pl.ds(2 * nh * d + h, 1), r0:r0 + l]
            b_c = _col(gp, 2 * nh * d + nh + h)
            g = jnp.where(masks[d], a_r, -jnp.inf)
            mt = jnp.maximum(jnp.max(g, axis=-1, keepdims=True), m)
            s = qk * jnp.exp(g - mt) * scale
            w_prev = jnp.exp(m - mt) * scale
            den = jnp.sum(s, axis=-1, keepdims=True) + w_prev * jnp.sum(qf * n, axis=-1, keepdims=True)
            inv = 1.0 / jnp.maximum(jnp.abs(den), jnp.exp(-(b_c + mt)))
            term = (w_prev * inv) * _dot(q, cbf[d, c])
            p = s * inv if p is None else p + s * inv
            inter = term if inter is None else inter + term
        hh = _dot(p.astype(BF16), v) + inter
        out_ref[r0:r0 + l, :] = _head_epilogue(hh, hw, o_ref[r0:r0 + l, :])


def _mlstm_call(lay, hg, qvo, kt, gp, gpt, hw, nh, dh, prompt, state=None):
    nb, s = (lay.nbp, lay.sp) if prompt else (lay.nbs, lay.ss)
    rb0 = 0 if prompt else lay.tp // s
    scale = dh ** -0.5
    common_in = [
        pl.BlockSpec((s, dh), lambda b, h: (rb0 + b, h)),
        pl.BlockSpec((dh, s), lambda b, h: (h, rb0 + b)),
        pl.BlockSpec((s, dh), lambda b, h: (rb0 + b, nh + h)),
        pl.BlockSpec((s, dh), lambda b, h: (rb0 + b, 2 * nh + h)),
        pl.BlockSpec((s, LANES), lambda b, h: (rb0 + b, 0)),
        pl.BlockSpec((4 * nh, s), lambda b, h: (0, rb0 + b)),
        pl.BlockSpec((1, dh), lambda b, h: (0, h)),
    ]
    hg_spec = pl.BlockSpec((s, dh), lambda b, h: (rb0 + b, h))
    any_spec = pl.BlockSpec(memory_space=pl.ANY)
    if prompt:
        assert s == MLSTM_L
        di = nh * dh
        kern = functools.partial(_mlstm_single_kernel, nh=nh, dh=dh, scale=scale)

        def body(hg_any, *refs):
            kern(*refs)

        return pl.pallas_call(
            body,
            out_shape=jax.ShapeDtypeStruct(hg.shape, hg.dtype),
            grid=(nb,),
            in_specs=[
                any_spec,
                pl.BlockSpec((s, di), lambda b: (rb0 + b, 0)),
                pl.BlockSpec((di, s), lambda b: (0, rb0 + b)),
                pl.BlockSpec((s, di), lambda b: (rb0 + b, 1)),
                pl.BlockSpec((s, di), lambda b: (rb0 + b, 2)),
                pl.BlockSpec((s, LANES), lambda b: (rb0 + b, 0)),
                pl.BlockSpec((4 * nh, s), lambda b: (0, rb0 + b)),
                pl.BlockSpec((1, di), lambda b: (0, 0)),
            ],
            out_specs=pl.BlockSpec((s, di), lambda b: (rb0 + b, 0)),
            input_output_aliases={0: 0},
            compiler_params=_cparams(("parallel",)),
            name="mlstm_prompt",
        )(hg, qvo, kt, qvo, qvo, gp, gpt, hw)
    state_c, jl, n0, m0 = state
    nc = s // MLSTM_L
    kern = functools.partial(_mlstm_multi_kernel, nh=nh, nc=nc, l=MLSTM_L, scale=scale)

    def body(hg_any, *refs):
        kern(*refs)

    return pl.pallas_call(
        body,
        out_shape=jax.ShapeDtypeStruct(hg.shape, hg.dtype),
        grid=(nb, nh),
        in_specs=[any_spec] + common_in + [
            pl.BlockSpec((None, None, 2, None, dh, dh), lambda b, h: (b, jl, 0, h, 0, 0)),
            pl.BlockSpec((None, 2 * nh, dh), lambda b, h: (b, 0, 0)),
            pl.BlockSpec((None, 2 * nh, LANES), lambda b, h: (b, 0, 0)),
        ],
        out_specs=hg_spec,
        scratch_shapes=[pltpu.VMEM((dh, dh), F32), pltpu.VMEM((2, nc, dh, dh), BF16)],
        input_output_aliases={0: 0},
        compiler_params=_cparams(("parallel", "parallel")),
        name="mlstm_latent",
    )(hg, qvo, kt, qvo, qvo, gp, gpt, hw, state_c, n0, m0)


def _state_kernel(*refs, nl, nh, dh):
    ins, (c_ref, n_ref, m_ref) = refs[:3 * nl], refs[3 * nl:]
    lyr = pl.program_id(0)
    for jl in range(nl):
        kt_ref, v_ref, gpt_ref = ins[3 * jl:3 * jl + 3]

        @pl.when(lyr == jl)
        def _(kt_ref=kt_ref, v_ref=v_ref, gpt_ref=gpt_ref):
            l = v_ref.shape[0]
            sub = lax.broadcasted_iota(jnp.int32, m_ref.shape, 0)
            lane = lax.broadcasted_iota(jnp.int32, m_ref.shape, 1)
            m_all = jnp.zeros(m_ref.shape, F32)
            for h in range(nh):
                kt = kt_ref[h * dh:(h + 1) * dh, :]
                ktf = kt.astype(F32)
                v = v_ref[:, h * dh:(h + 1) * dh]
                for d in range(2):
                    a_r = gpt_ref[2 * nh * d + h:2 * nh * d + h + 1, :]
                    b_r = gpt_ref[2 * nh * d + nh + h:2 * nh * d + nh + h + 1, :]
                    m_last = jnp.maximum(jnp.max(a_r, axis=-1, keepdims=True), 0.0)
                    b_end = b_r[:, l - 1:l] if d == 0 else b_r[:, 0:1]
                    w_end = jnp.exp(a_r - m_last)
                    c_ref[d, h] = _dot((ktf * w_end).astype(BF16), v)
                    n_ref[d, h:h + 1, :] = _row_times_kt(w_end, kt)
                    m_all = jnp.where((sub == d) & (lane == h), b_end + m_last, m_all)
            m_ref[...] = m_all


def _state_call(lay, qvos, kts, gpts, nh, dh):
    nl = len(qvos)
    nbp, s = lay.nbp, lay.sp
    di = nh * dh
    assert s == MLSTM_L

    def pick(jl):
        return lambda lyr, b: jnp.where(lyr == jl, b, jnp.where(lyr < jl, 0, nbp - 1))

    in_specs, args = [], []
    for jl in range(nl):
        pb = pick(jl)
        in_specs.append(pl.BlockSpec((di, s), lambda lyr, b, pb=pb: (0, pb(lyr, b))))
        in_specs.append(pl.BlockSpec((s, di), lambda lyr, b, pb=pb: (pb(lyr, b), 1)))
        in_specs.append(pl.BlockSpec((4 * nh, s), lambda lyr, b, pb=pb: (0, pb(lyr, b))))
        args += [kts[jl], qvos[jl], gpts[jl]]
    return pl.pallas_call(
        functools.partial(_state_kernel, nl=nl, nh=nh, dh=dh),
        out_shape=(
            jax.ShapeDtypeStruct((nbp, nl, 2, nh, dh, dh), F32),
            jax.ShapeDtypeStruct((nbp, nl, 2, nh, dh), F32),
            jax.ShapeDtypeStruct((nbp, nl, 2, nh), F32),
        ),
        grid=(nl, nbp),
        in_specs=in_specs,
        out_specs=(
            pl.BlockSpec((None, None, 2, nh, dh, dh), lambda lyr, b: (b, lyr, 0, 0, 0, 0)),
            pl.BlockSpec((None, None, 2, nh, dh), lambda lyr, b: (b, lyr, 0, 0, 0)),
            pl.BlockSpec((None, None, 2, nh), lambda lyr, b: (b, lyr, 0, 0)),
        ),
        compiler_params=_cparams(("arbitrary", "arbitrary")),
        name="mlstm_prompt_state",
    )(*args)


def _mm_res_kernel(a_ref, w_ref, x_ref, g_ref, nw_ref, sh_ref, sc_ref, o_ref, u_ref, *, lay, tm):
    grp = lay.group(pl.program_id(0) * tm)
    gate = g_ref[pl.ds(grp, 1), :]
    w = w_ref[...]
    rc = min(tm, 256)
    for r in range(0, tm, rc):
        xn = x_ref[r:r + rc, :] + gate * _dot(a_ref[r:r + rc, :], w)
        o_ref[r:r + rc, :] = xn
        u_ref[r:r + rc, :] = _mod_value(xn, nw_ref, sh_ref, sc_ref, grp).astype(BF16)


def _mm_res_call(lay, a, w_bf, x, mods, l, mod2):
    t, kdim = a.shape
    d = x.shape[1]
    tm = lay.row_tile(512)
    return pl.pallas_call(
        functools.partial(_mm_res_kernel, lay=lay, tm=tm),
        out_shape=(jax.ShapeDtypeStruct(x.shape, F32), jax.ShapeDtypeStruct(x.shape, BF16)),
        grid=(t // tm,),
        in_specs=[
            pl.BlockSpec((tm, kdim), lambda i: (i, 0)),
            pl.BlockSpec((kdim, d), lambda i: (0, 0)),
            pl.BlockSpec((tm, d), lambda i: (i, 0)),
            lay.mod_spec(l, 2),
        ] + mod2.specs,
        out_specs=(pl.BlockSpec((tm, d), lambda i: (i, 0)), pl.BlockSpec((tm, d), lambda i: (i, 0))),
        compiler_params=_cparams(("parallel",)),
        name="mm_residual",
    )(a, w_bf, x, mods, *mod2.args)


def _fnet_kernel(x_ref, u_ref, u2_any, gt_ref, wc_ref, ds_ref, wo_ref, nw_ref, sh_ref, sc_ref, o_ref, u2_ref,
                 ab_scr, *, lay, row_base, groups, norm):
    s, d = x_ref.shape
    cg = d // groups
    grp = lay.group(row_base + pl.program_id(0) * s)
    wc = wc_ref[...]
    rc = min(s, 256)
    for g in range(groups):
        for r in range(0, s, rc):
            ab = _dot(u_ref[r:r + rc, g * cg:(g + 1) * cg], wc)
            ab_scr[r:r + rc, g * cg:(g + 1) * cg] = ab[:, :cg].astype(BF16)
            ab_scr[s + r:s + r + rc, g * cg:(g + 1) * cg] = ab[:, cg:].astype(BF16)
    gate = gt_ref[pl.ds(grp, 1), :]
    wo = wo_ref[...]
    for r in range(0, s, rc):
        y = _dot(ds_ref[r:r + rc, :], ab_scr[...]) * norm
        xn = x_ref[r:r + rc, :] + gate * _dot(y.astype(BF16), wo)
        o_ref[r:r + rc, :] = xn
        u2_ref[r:r + rc, :] = _mod_value(xn, nw_ref, sh_ref, sc_ref, grp).astype(BF16)


def _dft_mats(s, cg):
    kc = np.arange(cg)
    ang_c = 2.0 * np.pi * np.outer(kc, kc) / cg
    wc = np.concatenate([np.cos(ang_c), np.sin(ang_c)], axis=1)
    ks = np.arange(s)
    ang_s = 2.0 * np.pi * np.outer(ks, ks) / s
    ds = np.concatenate([np.cos(ang_s), -np.sin(ang_s)], axis=1)
    return jnp.asarray(wc, dtype=BF16), jnp.asarray(ds, dtype=BF16)


def _fnet_call(lay, x, u1, u2, mods, l, wo_bf, mod2, prompt):
    nb, s = (lay.nbp, lay.sp) if prompt else (lay.nbs, lay.ss)
    rb0 = 0 if prompt else lay.tp // s
    d = lay.d
    cg = d // FNET_GROUPS
    wc, ds = _dft_mats(s, cg)
    kern = functools.partial(_fnet_kernel, lay=lay, row_base=rb0 * s, groups=FNET_GROUPS,
                             norm=1.0 / math.sqrt(s * cg))
    blk = pl.BlockSpec((s, d), lambda b: (rb0 + b, 0))
    return pl.pallas_call(
        kern,
        out_shape=(jax.ShapeDtypeStruct(x.shape, F32), jax.ShapeDtypeStruct(u2.shape, BF16)),
        grid=(nb,),
        in_specs=[
            blk,
            blk,
            pl.BlockSpec(memory_space=pl.ANY),
            lay.mod_spec(l, 2),
            pl.BlockSpec((cg, 2 * cg), lambda b: (0, 0)),
            pl.BlockSpec((s, 2 * s), lambda b: (0, 0)),
            pl.BlockSpec((d, d), lambda b: (0, 0)),
        ] + mod2.specs,
        out_specs=(blk, blk),
        scratch_shapes=[pltpu.VMEM((2 * s, d), BF16)],
        input_output_aliases={0: 0, 2: 1},
        compiler_params=_cparams(("parallel",)),
        name="fnet_prompt" if prompt else "fnet_latent",
    )(x, u1, u2, mods, wc, ds, wo_bf, *mod2.args)


def _glu_kernel(u_ref, wa_ref, wg_ref, ba_ref, bg_ref, o_ref):
    u = u_ref[...]
    a = _dot(u, wa_ref[...]) + ba_ref[...]
    g = _dot(u, wg_ref[...]) + bg_ref[...]
    o_ref[...] = a * _sigmoid(g)


def _glu_call(lay, u, w_bf, bias):
    t, d = u.shape
    cd = w_bf.shape[1] // 2
    tm = lay.row_tile(1024)
    tn = 512
    nj = cd // tn
    return pl.pallas_call(
        _glu_kernel,
        out_shape=jax.ShapeDtypeStruct((t, cd), F32),
        grid=(t // tm, nj),
        in_specs=[
            pl.BlockSpec((tm, d), lambda i, j: (i, 0)),
            pl.BlockSpec((d, tn), lambda i, j: (0, j)),
            pl.BlockSpec((d, tn), lambda i, j: (0, nj + j)),
            pl.BlockSpec((1, tn), lambda i, j: (0, j)),
            pl.BlockSpec((1, tn), lambda i, j: (0, nj + j)),
        ],
        out_specs=pl.BlockSpec((tm, tn), lambda i, j: (i, j)),
        compiler_params=_cparams(("parallel", "parallel")),
        name="conv_glu",
    )(u, w_bf, w_bf, bias, bias)


def _conv_kernel(c_ref, p_ref, n_ref, wd_ref, bd_ref, lw_ref, lb_ref, w2_ref, b2_ref, x_ref, gt_ref,
                 nw_ref, sh_ref, sc_ref, o_ref, u2_ref, pad, conv, act, *, lay, rb, width):
    i = pl.program_id(0)
    row0 = i * rb
    grp = lay.group(row0)
    seq = jnp.where(row0 < lay.tp, lay.sp, lay.ss)
    pos = jnp.where(row0 < lay.tp, row0 % lay.sp, (row0 - lay.tp) % lay.ss)
    has_prev = (pos != 0).astype(F32)
    has_next = (pos + rb != seq).astype(F32)
    hl = CONV_HALO
    half = width // 2
    cd = c_ref.shape[1]
    span = pad.shape[1]
    pad[0, 0:hl, :] = p_ref[...] * has_prev
    pad[0, hl:hl + rb, :] = c_ref[...]
    pad[0, hl + rb:hl + rb + hl, :] = n_ref[...] * has_next
    for s in range(1, SUBLANES):
        pad[s, 0:span - SUBLANES, :] = pad[0, s:s + span - SUBLANES, :]
    ngrp = 8
    sub = ngrp * SUBLANES
    lanes = 2 * LANES
    assert rb % sub == 0 and cd % lanes == 0

    def conv_block(blk, carry):
        r0 = pl.multiple_of(blk * sub, sub)
        for c0 in range(0, cd, lanes):
            bias = bd_ref[:, c0:c0 + lanes]
            accs = [jnp.zeros((SUBLANES, lanes), F32) + bias for _ in range(ngrp)]
            for k in sorted(range(width), key=lambda k: ((hl - half + k) % SUBLANES, k)):
                q, s = divmod(hl - half + k, SUBLANES)
                wk = wd_ref[k, :, c0:c0 + lanes]
                for gi in range(ngrp):
                    win = pad[s, pl.ds(r0 + (q + gi) * SUBLANES, SUBLANES), c0:c0 + lanes]
                    accs[gi] = accs[gi] + win * wk
            conv[pl.ds(r0, sub), c0:c0 + lanes] = jnp.concatenate(accs, axis=0)
        return carry

    lax.fori_loop(0, rb // sub, conv_block, 0)
    lw = lw_ref[...]
    lb = lb_ref[...]
    lsub = min(rb, 16 * SUBLANES)

    def ln_block(blk, carry):
        r0 = pl.multiple_of(blk * lsub, lsub)
        acc = conv[pl.ds(r0, lsub), :]
        mu = jnp.mean(acc, axis=-1, keepdims=True)
        cen = acc - mu
        var = jnp.mean(cen * cen, axis=-1, keepdims=True)
        y = cen * lax.rsqrt(var + EPS) * lw + lb
        act[pl.ds(r0, lsub), :] = (y * _sigmoid(y)).astype(BF16)
        return carry

    lax.fori_loop(0, rb // lsub, ln_block, 0)
    xn = x_ref[...] + gt_ref[pl.ds(grp, 1), :] * (_dot(act[...], w2_ref[...]) + b2_ref[...])
    o_ref[...] = xn
    u2_ref[...] = _mod_value(xn, nw_ref, sh_ref, sc_ref, grp).astype(BF16)


def _conv_call(lay, glu, wd, bd, lw, lb, w2_bf, b2, x, mods, l, mod2):
    t, cd = glu.shape
    d = x.shape[1]
    rb = lay.row_tile(256)
    hl = CONV_HALO
    assert CONV_WIDTH // 2 <= hl and rb % hl == 0
    nhb = t // hl
    per = rb // hl
    wd_p = jnp.broadcast_to(wd[:, None, :], (CONV_WIDTH, SUBLANES, cd))
    row = lambda a: a.reshape(1, -1)
    rows = pl.BlockSpec((rb, d), lambda i: (i, 0))
    return pl.pallas_call(
        functools.partial(_conv_kernel, lay=lay, rb=rb, width=CONV_WIDTH),
        out_shape=(jax.ShapeDtypeStruct(x.shape, F32), jax.ShapeDtypeStruct(x.shape, BF16)),
        grid=(t // rb,),
        in_specs=[
            pl.BlockSpec((rb, cd), lambda i: (i, 0)),
            pl.BlockSpec((hl, cd), lambda i: (jnp.maximum(i * per - 1, 0), 0)),
            pl.BlockSpec((hl, cd), lambda i: (jnp.minimum((i + 1) * per, nhb - 1), 0)),
            pl.BlockSpec(wd_p.shape, lambda i: (0, 0, 0)),
            pl.BlockSpec((1, cd), lambda i: (0, 0)),
            pl.BlockSpec((1, cd), lambda i: (0, 0)),
            pl.BlockSpec((1, cd), lambda i: (0, 0)),
            pl.BlockSpec((cd, d), lambda i: (0, 0)),
            pl.BlockSpec((1, d), lambda i: (0, 0)),
            rows,
            lay.mod_spec(l, 2),
        ] + mod2.specs,
        out_specs=(rows, rows),
        scratch_shapes=[pltpu.VMEM((SUBLANES, rb + 2 * hl, cd), F32), pltpu.VMEM((rb, cd), F32),
                        pltpu.VMEM((rb, cd), BF16)],
        compiler_params=_cparams(("parallel",)),
        name="conv_dw_ln_pw2",
    )(glu, glu, glu, wd_p, row(bd), row(lw), row(lb), w2_bf, row(b2), x, mods, *mod2.args)


def _router_kernel(u_ref, wr_ref, tri_ref, upper_ref, o_ref, cnt_ref, *, ng, ne):
    logits = _dot(u_ref[...], wr_ref[...])
    lane = lax.broadcasted_iota(jnp.int32, logits.shape, 1)
    big = jnp.int32(4 * LANES)
    neg = -jnp.inf
    epg = ne // ng

    gl = jnp.where(lane < ng, logits, neg)
    gmax = jnp.max(gl, axis=-1, keepdims=True)
    gidx = jnp.min(jnp.where(gl == gmax, lane, big), axis=-1, keepdims=True)
    g_p = 1.0 / jnp.sum(jnp.where(lane < ng, jnp.exp(logits - gmax), 0.0), axis=-1, keepdims=True)

    lo = ng + gidx * epg
    el = jnp.where((lane >= lo) & (lane < lo + epg), logits, neg)
    v1 = jnp.max(el, axis=-1, keepdims=True)
    i1 = jnp.min(jnp.where(el == v1, lane, big), axis=-1, keepdims=True)
    el2 = jnp.where(lane == i1, neg, el)
    v2 = jnp.max(el2, axis=-1, keepdims=True)
    i2 = jnp.min(jnp.where(el2 == v2, lane, big), axis=-1, keepdims=True)
    e1 = i1 - ng
    e2 = i2 - ng
    tt = jnp.exp(v2 - v1)
    p1 = 1.0 / (1.0 + tt)
    gate1 = p1 * g_p
    gate2 = (tt * p1) * g_p

    oh1 = lane == e1
    oh2 = lane == e2
    oh = jnp.where(oh1 | oh2, 1.0, 0.0)
    groups = jnp.floor((jnp.sum(oh, axis=0, keepdims=True) + (SUBLANES - 1)) * (1.0 / SUBLANES))
    groups8 = jnp.broadcast_to(groups, (SUBLANES, LANES))
    start = SUBLANES * _dot(groups8.astype(BF16), upper_ref[...])[0:1, :]
    prefix = _dot(tri_ref[...], oh.astype(BF16)) + start
    pos1 = jnp.sum(jnp.where(oh1, prefix, 0.0), axis=-1, keepdims=True)
    pos2 = jnp.sum(jnp.where(oh2, prefix, 0.0), axis=-1, keepdims=True)
    cnt_ref[...] = groups8

    out = jnp.where(lane == 0, e1.astype(F32), 0.0)
    out = jnp.where(lane == 1, e2.astype(F32), out)
    out = jnp.where(lane == 2, gate1, out)
    out = jnp.where(lane == 3, gate2, out)
    out = jnp.where(lane == 4, pos1, out)
    out = jnp.where(lane == 5, pos2, out)
    o_ref[...] = out


def _router_call(u, wr_bf, tri, upper, ng, ne):
    t, d = u.shape
    tm = tri.shape[0]
    nt = t // tm
    return pl.pallas_call(
        functools.partial(_router_kernel, ng=ng, ne=ne),
        out_shape=(jax.ShapeDtypeStruct((t, LANES), F32), jax.ShapeDtypeStruct((nt, SUBLANES, LANES), F32)),
        grid=(nt,),
        in_specs=[
            pl.BlockSpec((tm, d), lambda i: (i, 0)),
            pl.BlockSpec((d, LANES), lambda i: (0, 0)),
            pl.BlockSpec((tm, tm), lambda i: (0, 0)),
            pl.BlockSpec((LANES, LANES), lambda i: (0, 0)),
        ],
        out_specs=(pl.BlockSpec((tm, LANES), lambda i: (i, 0)),
                   pl.BlockSpec((None, SUBLANES, LANES), lambda i: (i, 0, 0))),
        compiler_params=_cparams(("parallel",)),
        name="moe_router",
    )(u, wr_bf, tri, upper)


def _pack_halves(lo, hi):
    lo_bits = lax.shift_right_logical(pltpu.bitcast(lo, U32), jnp.uint32(16))
    hi_bits = pltpu.bitcast(hi, U32) & jnp.uint32(0xFFFF0000)
    return hi_bits | lo_bits


def _unpack_halves(w):
    lo = pltpu.bitcast(lax.shift_left(w, jnp.uint32(16)), F32)
    hi = pltpu.bitcast(w & jnp.uint32(0xFFFF0000), F32)
    return lo.astype(BF16), hi.astype(BF16)


def _round_bf16(x):
    return x.astype(BF16).astype(F32)


def _group_copy(src, src_g, dst, dst_g, sem):
    g8 = lambda g: pl.ds(pl.multiple_of(g * SUBLANES, SUBLANES), SUBLANES)
    return pltpu.make_async_copy(src.at[g8(src_g), :], dst.at[g8(dst_g), :], sem)


def _for_groups(n, fn, unroll=4):
    def body_many(i, c):
        for j in range(unroll):
            fn(i * unroll + j)
        return c

    def body_one(g, c):
        fn(g)
        return c

    full = lax.div(n, jnp.int32(unroll))
    lax.fori_loop(0, full, body_many, 0)
    lax.fori_loop(full * unroll, n, body_one, 0)


def _dispatch_kernel(gdst_ref, ngt_ref, pad0_ref, npad_ref, tail_ref, u_ref, pos_ref, xs_out, loc, zeros, sem,
                     zsem, *, tm, nl, ne):
    i = pl.program_id(0)
    nt = pl.num_programs(0)
    slot = i % 2
    nlg = nl // SUBLANES

    def copy(step, s, g):
        return _group_copy(loc.at[s], g, xs_out, gdst_ref[step * nlg + g], sem.at[s])

    def group_wait(s):
        _group_copy(loc.at[s], 0, xs_out, 0, sem.at[s]).wait()

    def zero_copy(e, g):
        return _group_copy(zeros, 0, xs_out, pad0_ref[e] + g, zsem)

    def zero_block(t):
        first = pl.multiple_of(tail_ref[0] + t * MOE_BM, MOE_BM)
        return pltpu.make_async_copy(zeros, xs_out.at[pl.ds(first, MOE_BM), :], zsem)

    @pl.when(i == 0)
    def _():
        zeros[...] = jnp.zeros(zeros.shape, zeros.dtype)
        for e in range(ne):
            _for_groups(npad_ref[e], lambda g, e=e: zero_copy(e, g).start())
        _for_groups(tail_ref[1], lambda t: zero_block(t).start())

    @pl.when(i >= 2)
    def _():
        _for_groups(ngt_ref[i - 2], lambda g: group_wait(slot))

    pos1 = pos_ref[0:1, :]
    pos2 = pos_ref[1:2, :]
    half = u_ref.shape[1] // 2
    u = u_ref[...]
    rc = 256
    for r in range(0, nl, rc):
        p = lax.broadcasted_iota(jnp.int32, (rc, tm), 0) + r
        onehot = jnp.where((p == pos1) | (p == pos2), 1.0, 0.0).astype(BF16)
        rows = _dot(onehot, u)
        loc[slot, r:r + rc, :] = _pack_halves(rows[:, :half], rows[:, half:])
    _for_groups(ngt_ref[i], lambda g: copy(i, slot, g).start())

    @pl.when(i == nt - 1)
    def _():
        @pl.when(i >= 1)
        def _():
            _for_groups(ngt_ref[i - 1], lambda g: group_wait(1 - slot))

        _for_groups(ngt_ref[i], lambda g: group_wait(slot))
        for e in range(ne):
            _for_groups(npad_ref[e], lambda g: _group_copy(zeros, 0, xs_out, 0, zsem).wait())
        _for_groups(tail_ref[1], lambda t: pltpu.make_async_copy(zeros, xs_out.at[0:MOE_BM, :], zsem).wait())


def _dispatch_call(u, pos_rows, gdst, ngt, pad0, npad, tail, nrows, tm, nl):
    t, d = u.shape
    ne = npad.shape[0]
    grid_spec = pltpu.PrefetchScalarGridSpec(
        num_scalar_prefetch=5,
        grid=(t // tm,),
        in_specs=[
            pl.BlockSpec((tm, d), lambda i, *_: (i, 0)),
            pl.BlockSpec((None, SUBLANES, tm), lambda i, *_: (i, 0, 0)),
        ],
        out_specs=pl.BlockSpec(memory_space=pl.ANY),
        scratch_shapes=[pltpu.VMEM((2, nl, d // 2), U32), pltpu.VMEM((MOE_BM, d // 2), U32),
                        pltpu.SemaphoreType.DMA((2,)), pltpu.SemaphoreType.DMA(())],
    )
    return pl.pallas_call(
        functools.partial(_dispatch_kernel, tm=tm, nl=nl, ne=ne),
        out_shape=jax.ShapeDtypeStruct((nrows, d // 2), U32),
        grid_spec=grid_spec,
        compiler_params=_cparams(("arbitrary",)),
        name="moe_dispatch",
    )(gdst, ngt, pad0, npad, tail, u, pos_rows)


def _expert_kernel(row0_ref, nblk_ref, tail_ref, x_hbm, w13_ref, w2_ref, y_hbm, xbuf, ybuf, w13_bf, w2_bf,
                   xsem, ysem, *, hid):
    e = pl.program_id(0)
    n = nblk_ref[e]
    g0 = row0_ref[e] // MOE_BM
    total = tail_ref[0] // MOE_BM
    nx, ny = xbuf.shape[0], ybuf.shape[0]
    ahead = nx - 1
    half = xbuf.shape[2]

    def rows(g):
        return pl.ds(pl.multiple_of(g * MOE_BM, MOE_BM), MOE_BM)

    def x_copy(g):
        return pltpu.make_async_copy(x_hbm.at[rows(g), :], xbuf.at[g % nx], xsem.at[g % nx])

    def y_copy(g, s):
        return pltpu.make_async_copy(ybuf.at[s], y_hbm.at[rows(g), :], ysem.at[s])

    @pl.when(e == 0)
    def _():
        for g in range(ahead):
            @pl.when(g < total)
            def _(g=g):
                x_copy(g).start()

    @pl.when(n > 0)
    def _():
        w13_bf[...] = w13_ref[...].astype(BF16)
        w2_bf[...] = w2_ref[...].astype(BF16)

        def block(c, carry):
            g = g0 + c
            x_copy(g).wait()

            @pl.when(g + ahead < total)
            def _():
                x_copy(g + ahead).start()

            @pl.when(g >= ny)
            def _():
                y_copy(g - ny, g % ny).wait()

            x_lo, x_hi = _unpack_halves(xbuf[g % nx])
            hb = _dot(x_lo, w13_bf[:half, :]) + _dot(x_hi, w13_bf[half:, :])
            a = hb[:, :hid]
            act = (a * _sigmoid(a)) * hb[:, hid:]
            y = _round_bf16(_dot(act.astype(BF16), w2_bf[...]))
            ybuf[g % ny] = _pack_halves(y[:, :half], y[:, half:])
            y_copy(g, g % ny).start()
            return carry

        lax.fori_loop(0, n, block, 0)

    @pl.when(e == pl.num_programs(0) - 1)
    def _():
        for j in range(ny):
            @pl.when(total - ny + j >= 0)
            def _(j=j):
                g = total - ny + j
                y_copy(g, g % ny).wait()

        ybuf[0] = jnp.zeros(ybuf.shape[1:], ybuf.dtype)
        _for_groups(tail_ref[1], lambda t: y_copy(total + t, 0).start())
        _for_groups(tail_ref[1], lambda t: y_copy(total + t, 0).wait())


def _expert_call(xs, row0, nblk, tail, w13, w2, l):
    r, half = xs.shape
    d = 2 * half
    ne = w13.shape[1]
    hid = w2.shape[2]
    nx, ny = 5, 3
    grid_spec = pltpu.PrefetchScalarGridSpec(
        num_scalar_prefetch=3,
        grid=(ne,),
        in_specs=[
            pl.BlockSpec(memory_space=pl.ANY),
            pl.BlockSpec((None, None, d, 2 * hid), lambda e, *_: (l, e, 0, 0)),
            pl.BlockSpec((None, None, hid, d), lambda e, *_: (l, e, 0, 0)),
        ],
        out_specs=pl.BlockSpec(memory_space=pl.ANY),
        scratch_shapes=[
            pltpu.VMEM((nx, MOE_BM, half), U32), pltpu.VMEM((ny, MOE_BM, half), U32),
            pltpu.VMEM((d, 2 * hid), BF16), pltpu.VMEM((hid, d), BF16),
            pltpu.SemaphoreType.DMA((nx,)), pltpu.SemaphoreType.DMA((ny,)),
        ],
    )
    return pl.pallas_call(
        functools.partial(_expert_kernel, hid=hid),
        out_shape=jax.ShapeDtypeStruct((r, half), U32),
        grid_spec=grid_spec,
        compiler_params=_cparams(("arbitrary",)),
        name="moe_experts",
    )(row0, nblk, tail, xs, w13, w2)


def _combine_kernel(gdst_ref, ngt_ref, x_ref, rt_ref, gt_ref, nw_ref, sh_ref, sc_ref, ys_ref, out_a, out_b,
                    loc, sem, *, lay, tm, nl, final):
    i = pl.program_id(0)
    nt = pl.num_programs(0)
    slot = i % 2
    nlg = nl // SUBLANES

    def copy(step, s, g):
        return _group_copy(ys_ref, gdst_ref[step * nlg + g], loc.at[s], g, sem.at[s])

    @pl.when(i == 0)
    def _():
        loc[...] = jnp.zeros(loc.shape, loc.dtype)
        _for_groups(ngt_ref[0], lambda g: copy(0, 0, g).start())

    _for_groups(ngt_ref[i], lambda g: _group_copy(ys_ref, 0, loc.at[slot], 0, sem.at[slot]).wait())

    @pl.when(i + 1 < nt)
    def _():
        _for_groups(ngt_ref[i + 1], lambda g: copy(i + 1, 1 - slot, g).start())

    rt = rt_ref[...]
    gate1, gate2 = rt[:, 2:3], rt[:, 3:4]
    pos1, pos2 = rt[:, 4:5].astype(jnp.int32), rt[:, 5:6].astype(jnp.int32)
    half = loc.shape[2]
    rc = 256
    mix_lo = jnp.zeros((tm, half), F32)
    mix_hi = jnp.zeros((tm, half), F32)
    for r in range(0, nl, rc):
        p = lax.broadcasted_iota(jnp.int32, (tm, rc), 1) + r
        wgt = (jnp.where(p == pos1, gate1, 0.0) + jnp.where(p == pos2, gate2, 0.0)).astype(BF16)
        y_lo, y_hi = _unpack_halves(loc[slot, r:r + rc, :])
        mix_lo = mix_lo + _dot(wgt, y_lo)
        mix_hi = mix_hi + _dot(wgt, y_hi)
    grp = lay.group(i * tm)
    gate = gt_ref[pl.ds(grp, 1), :]
    x_lo = x_ref[:, :half] + gate[:, :half] * mix_lo
    x_hi = x_ref[:, half:] + gate[:, half:] * mix_hi
    ms = (jnp.sum(x_lo * x_lo, axis=-1, keepdims=True) + jnp.sum(x_hi * x_hi, axis=-1, keepdims=True)) / (2 * half)
    inv = lax.rsqrt(ms + EPS)
    nw = nw_ref[...]
    if final:
        y_lo = x_lo * inv * nw[:, :half]
        y_hi = x_hi * inv * nw[:, half:]
        for ref, cond in ((out_a, i * tm < lay.tp), (out_b, i * tm >= lay.tp)):
            @pl.when(cond)
            def _(ref=ref):
                ref[:, :half] = y_lo
                ref[:, half:] = y_hi
    else:
        out_a[:, :half] = x_lo
        out_a[:, half:] = x_hi
        sh = sh_ref[pl.ds(grp, 1), :]
        sc = sc_ref[pl.ds(grp, 1), :]
        out_b[:, :half] = (x_lo * inv * nw[:, :half] * (1.0 + sc[:, :half]) + sh[:, :half]).astype(BF16)
        out_b[:, half:] = (x_hi * inv * nw[:, half:] * (1.0 + sc[:, half:]) + sh[:, half:]).astype(BF16)


def _combine_call(lay, x, route, mods, l, mod_next, gdst, ngt, ys, tm, nl, final):
    t, d = x.shape
    rows = pl.BlockSpec((tm, d), lambda i, *_: (i, 0))
    if final:
        ntp = lay.tp // tm
        out_specs = (pl.BlockSpec((tm, d), lambda i, *_: (jnp.minimum(i, ntp - 1), 0)),
                     pl.BlockSpec((tm, d), lambda i, *_: (jnp.maximum(i - ntp, 0), 0)))
        out_shape = (jax.ShapeDtypeStruct((lay.tp, d), F32), jax.ShapeDtypeStruct((lay.ts, d), F32))
    else:
        out_specs = (rows, rows)
        out_shape = (jax.ShapeDtypeStruct(x.shape, F32), jax.ShapeDtypeStruct(x.shape, BF16))
    grid_spec = pltpu.PrefetchScalarGridSpec(
        num_scalar_prefetch=2,
        grid=(t // tm,),
        in_specs=[
            rows,
            pl.BlockSpec((tm, LANES), lambda i, *_: (i, 0)),
            lay.mod_spec(l, 5),
        ] + mod_next.specs + [pl.BlockSpec(memory_space=pl.ANY)],
        out_specs=out_specs,
        scratch_shapes=[pltpu.VMEM((2, nl, d // 2), U32), pltpu.SemaphoreType.DMA((2,))],
    )
    return pl.pallas_call(
        functools.partial(_combine_kernel, lay=lay, tm=tm, nl=nl, final=final),
        out_shape=out_shape,
        grid_spec=grid_spec,
        compiler_params=_cparams(("arbitrary",)),
        name="moe_combine_final" if final else "moe_combine",
    )(gdst, ngt, x, route, mods, *mod_next.args, ys)


def _moe_layer(lay, x, u2, mods, l, mod_next, final, wr_bf, tri_r, upper, w13, w2, ng, ne):
    t, d = x.shape
    tm = tri_r.shape[0]
    nt = t // tm
    bmg = MOE_BM // SUBLANES
    nl = -(-(MOE_TOP_K * tm + ne * (SUBLANES - 1)) // 256) * 256
    nlg = nl // SUBLANES
    route, counts = _router_call(u2, wr_bf, tri_r, upper, ng, ne)

    c8 = counts[:, 0, :ne].astype(jnp.int32)
    lend = jnp.cumsum(c8, axis=1)
    lstart = lend - c8
    ngt = lend[:, -1].astype(jnp.int32)
    tot = jnp.sum(c8, axis=0)
    padded = (tot + bmg - 1) // bmg * bmg
    gend = jnp.cumsum(padded)
    gbase = (gend - padded)[None, :] + jnp.cumsum(c8, axis=0) - c8
    nb = -(-(MOE_TOP_K * t + nt * ne * (SUBLANES - 1)) // MOE_BM) + ne
    g = jnp.arange(nlg, dtype=jnp.int32)[None, :, None]
    owner = (g >= lstart[:, None, :]) & (g < lend[:, None, :])
    gdst = g[:, :, 0] + jnp.sum(jnp.where(owner, (gbase - lstart)[:, None, :], 0), axis=-1)
    gdst = gdst.reshape(nt * nlg).astype(jnp.int32)
    row0 = ((gend - padded) * SUBLANES).astype(jnp.int32)
    nblk = (padded // bmg).astype(jnp.int32)
    tail = jnp.stack([gend[-1] * SUBLANES, nb - gend[-1] // bmg]).astype(jnp.int32)
    pos = route[:, 2 * MOE_TOP_K:3 * MOE_TOP_K].astype(jnp.int32).reshape(nt, tm, MOE_TOP_K)
    pos_rows = jnp.full((nt, SUBLANES, tm), -1, jnp.int32).at[:, :MOE_TOP_K, :].set(jnp.swapaxes(pos, 1, 2))

    pad0 = (gend - padded + tot).astype(jnp.int32)
    npad = (padded - tot).astype(jnp.int32)
    xs = _dispatch_call(u2, pos_rows, gdst, ngt, pad0, npad, tail, nb * MOE_BM, tm, nl)
    ys = _expert_call(xs, row0, nblk, tail, w13, w2, l)
    return _combine_call(lay, x, route, mods, l, mod_next, gdst, ngt, ys, tm, nl, final)


def _lower_tri(n, strict):
    r = np.arange(n)
    m = (r[None, :] < r[:, None]) if strict else (r[None, :] <= r[:, None])
    return jnp.asarray(m.astype(np.float32), dtype=BF16)


def kernel(x_prompt, x_sample, state_C, state_n, state_m, c, c_ctx, ada_w, ada_b, norm1_w, norm2_w, m_w_in, m_b_gate, m_head_norm_w, m_w_out, f_w_out, cv_w_pw1, cv_b_pw1, cv_w_dw, cv_b_dw, cv_ln_w, cv_ln_b, cv_w_pw2, cv_b_pw2, r_w_group, r_w_expert, e_w13, e_w2, final_norm_w):
    nbp, sp, d = x_prompt.shape
    nbs, ss, _ = x_sample.shape
    assert ss % GRID_W == 0
    lay = _Layout(nbp, sp, nbs, ss, d)
    depth = ada_w.shape[0]
    nh, dh = state_C.shape[3], state_C.shape[4]
    di = nh * dh
    ng = r_w_group.shape[2]
    ne = r_w_expert.shape[2]
    assert ng == MOE_GROUPS and ng + ne <= LANES and 4 * nh <= LANES and MOE_TOP_K == 2

    cv = jnp.zeros((lay.ngp, d), F32).at[0].set(c_ctx).at[1:1 + nbs].set(c)
    mods = _ada_call(cv, ada_w, ada_b)

    tri_l = _lower_tri(MLSTM_L, strict=False)
    tri_r = _lower_tri(lay.row_tile(512), strict=True)
    upper = _lower_tri(LANES, strict=True).T
    row = lambda a: a.reshape(1, -1)
    w_in_t = jnp.swapaxes(m_w_in, 1, 2)
    tn = 1024
    assert di % tn == 0
    nkb = di // tn

    x, u1 = _prep_call(lay, x_prompt.reshape(lay.tp, d), x_sample.reshape(lay.ts, d),
                       _Mod(lay, mods, norm1_w[0], 0, 0))
    y = None
    qvos, kts, gpts = [], [], []
    for l in range(depth):
        j, kind = l // N_MIXERS, l % N_MIXERS
        mod2 = _Mod(lay, mods, norm2_w[l], l, 3)
        if kind == 0:
            wg = jnp.zeros((LANES, d), F32).at[:4 * nh].set(w_in_t[j, 4 * di:])
            bg = jnp.zeros((1, LANES), F32).at[0, :4 * nh].set(m_b_gate[j])
            gp = _gates_call(lay, x, _Mod(lay, mods, norm1_w[l], l, 0), wg, bg, tri_l, nh)
            gpt = gp[:, :4 * nh].T
            qvo_blocks = list(range(nkb)) + list(range(2 * nkb, 4 * nkb))
            qvo = _proj_call(lay, u1, w_in_t, j, qvo_blocks, tn, transposed=False)
            kt = _proj_call(lay, u1, w_in_t, j, list(range(nkb, 2 * nkb)), tn, transposed=True)
            hw = row(m_head_norm_w[j])
            hg = jnp.zeros((lay.t, di), BF16)
            hg = _mlstm_call(lay, hg, qvo, kt, gp, gpt, hw, nh, dh, prompt=True)
            n0 = state_n[:, j].reshape(nbs, 2 * nh, dh)
            m0 = jnp.broadcast_to(state_m[:, j].reshape(nbs, 2 * nh, 1), (nbs, 2 * nh, LANES))
            hg = _mlstm_call(lay, hg, qvo, kt, gp, gpt, hw, nh, dh, prompt=False, state=(state_C, j, n0, m0))
            x, u2 = _mm_res_call(lay, hg, m_w_out[j].astype(BF16), x, mods, l, mod2)
            qvos.append(qvo)
            kts.append(kt)
            gpts.append(gpt)
        elif kind == 1:
            wo = f_w_out[j].astype(BF16)
            u2 = jnp.zeros((lay.t, d), BF16)
            x, u2 = _fnet_call(lay, x, u1, u2, mods, l, wo, mod2, prompt=True)
            x, u2 = _fnet_call(lay, x, u1, u2, mods, l, wo, mod2, prompt=False)
        else:
            glu = _glu_call(lay, u1, cv_w_pw1[j].astype(BF16), row(cv_b_pw1[j]))
            x, u2 = _conv_call(lay, glu, cv_w_dw[j], cv_b_dw[j], cv_ln_w[j], cv_ln_b[j], cv_w_pw2[j].astype(BF16),
                               cv_b_pw2[j], x, mods, l, mod2)
        wr = jnp.zeros((d, LANES), F32).at[:, :ng].set(r_w_group[l]).at[:, ng:ng + ne].set(r_w_expert[l])
        final = l + 1 == depth
        mod_next = _Mod(lay, mods, final_norm_w, l, 0) if final else _Mod(lay, mods, norm1_w[l + 1], l + 1, 0)
        outs = _moe_layer(lay, x, u2, mods, l, mod_next, final, wr.astype(BF16), tri_r, upper, e_w13, e_w2, ng, ne)
        if final:
            y = outs
        else:
            x, u1 = outs

    y_prompt = y[0].reshape(nbp, sp, d)
    y_sample = y[1].reshape(nbs, ss, d)
    new_c, new_n, new_m = _state_call(lay, qvos, kts, gpts, nh, dh)
    return (y_prompt, y_sample, new_c, new_n, new_m)
```

```python
import functools
import math

import numpy as np
import jax
import jax.numpy as jnp
from jax import lax
from jax.experimental import pallas as pl
from jax.experimental.pallas import tpu as pltpu

F32 = jnp.float32
BF16 = jnp.bfloat16
U32 = jnp.uint32
EPS = 1e-6
GRID_W = 64
N_MIXERS = 3
FNET_GROUPS = 4
CONV_WIDTH = 31
MOE_GROUPS = 4
MOE_TOP_K = 2

LANES = 128
SUBLANES = 8
MLSTM_L = 256
MOE_BM = 256
CONV_HALO = 16
VMEM_LIMIT = 56 * 1024 * 1024


def _cparams(sem, vmem=VMEM_LIMIT):
    return pltpu.CompilerParams(dimension_semantics=sem, vmem_limit_bytes=vmem)


def _dot(a, b):
    return jnp.dot(a, b, preferred_element_type=F32)


def _dot_nt(a, b):
    return lax.dot_general(a, b, (((1,), (1,)), ((), ())), preferred_element_type=F32)


def _rms(x, w):
    return x * lax.rsqrt(jnp.mean(x * x, axis=-1, keepdims=True) + EPS) * w


def _modulate(x, w, shift, scale):
    return _rms(x, w) * (1.0 + scale) + shift


def _sigmoid(x):
    return 1.0 / (1.0 + jnp.exp(-x))


def _log_sigmoid(x):
    return jnp.minimum(x, 0.0) - jnp.log(1.0 + jnp.exp(-jnp.abs(x)))


def _split2(x):
    hi = x.astype(BF16)
    return hi, (x - hi.astype(F32)).astype(BF16)


def _split3(x):
    hi = x.astype(BF16)
    r1 = x - hi.astype(F32)
    mid = r1.astype(BF16)
    return hi, mid, (r1 - mid.astype(F32)).astype(BF16)


class _Layout:
    def __init__(self, nbp, sp, nbs, ss, d):
        self.nbp, self.sp, self.nbs, self.ss, self.d = nbp, sp, nbs, ss, d
        self.tp, self.ts = nbp * sp, nbs * ss
        self.t = self.tp + self.ts
        assert self.tp % ss == 0, "latent sequences must start on a block boundary of their own length"
        self.ngp = -(-(1 + nbs) // SUBLANES) * SUBLANES

    def group(self, row0):
        return jnp.where(row0 < self.tp, 0, 1 + (row0 - self.tp) // self.ss)

    def row_tile(self, want):
        tm = math.gcd(math.gcd(self.tp, self.ss), want)
        assert tm % SUBLANES == 0
        return tm

    def mod_spec(self, l, chunk):
        return pl.BlockSpec((None, self.ngp, self.d), lambda *_: (l, 0, chunk))

    def row_spec(self):
        return pl.BlockSpec((1, self.d), lambda *_: (0, 0))


class _Mod:
    def __init__(self, lay, mods, nw, l, c_shift):
        self.args = (nw.reshape(1, -1), mods, mods)
        self.specs = [lay.row_spec(), lay.mod_spec(l, c_shift), lay.mod_spec(l, c_shift + 1)]


def _mod_value(x, nw_ref, sh_ref, sc_ref, grp):
    return _modulate(x, nw_ref[...], sh_ref[pl.ds(grp, 1), :], sc_ref[pl.ds(grp, 1), :])


def _ada_kernel(cv_ref, w_ref, b_ref, o_ref):
    s = cv_ref[...]
    s = s * _sigmoid(s)
    o_ref[...] = _dot(s.astype(BF16), w_ref[...].astype(BF16)) + b_ref[...]


def _ada_call(cv, ada_w, ada_b):
    depth, d, n = ada_w.shape
    ngp = cv.shape[0]
    tn = min(n, 2048)
    return pl.pallas_call(
        _ada_kernel,
        out_shape=jax.ShapeDtypeStruct((depth, ngp, n), F32),
        grid=(depth, n // tn),
        in_specs=[
            pl.BlockSpec((ngp, d), lambda l, j: (0, 0)),
            pl.BlockSpec((None, d, tn), lambda l, j: (l, 0, j)),
            pl.BlockSpec((None, 1, tn), lambda l, j: (l, 0, j)),
        ],
        out_specs=pl.BlockSpec((None, ngp, tn), lambda l, j: (l, 0, j)),
        compiler_params=_cparams(("parallel", "parallel")),
        name="ada_mods",
    )(cv, ada_w, ada_b.reshape(depth, 1, n))


def _prep_kernel(xp_ref, xs_ref, nw_ref, sh_ref, sc_ref, x_ref, u_ref, *, lay, tm):
    i = pl.program_id(0)
    grp = lay.group(i * tm)
    for src, cond in ((xp_ref, i * tm < lay.tp), (xs_ref, i * tm >= lay.tp)):
        @pl.when(cond)
        def _(src=src):
            rc = min(tm, 256)
            for r in range(0, tm, rc):
                x = src[r:r + rc, :]
                x_ref[r:r + rc, :] = x
                u_ref[r:r + rc, :] = _mod_value(x, nw_ref, sh_ref, sc_ref, grp).astype(BF16)


def _prep_call(lay, xp, xs, mod):
    d = lay.d
    tm = lay.row_tile(1024)
    ntp = lay.tp // tm
    rows = pl.BlockSpec((tm, d), lambda i: (i, 0))
    return pl.pallas_call(
        functools.partial(_prep_kernel, lay=lay, tm=tm),
        out_shape=(jax.ShapeDtypeStruct((lay.t, d), F32), jax.ShapeDtypeStruct((lay.t, d), BF16)),
        grid=(lay.t // tm,),
        in_specs=[
            pl.BlockSpec((tm, d), lambda i: (jnp.minimum(i, ntp - 1), 0)),
            pl.BlockSpec((tm, d), lambda i: (jnp.maximum(i - ntp, 0), 0)),
        ] + mod.specs,
        out_specs=(rows, rows),
        compiler_params=_cparams(("arbitrary",)),
        name="prep_modulate",
    )(xp, xs, *mod.args)


def _gates_kernel(x_ref, nw_ref, sh_ref, sc_ref, wh_ref, wl_ref, b_ref, tri_ref, o_ref, *, lay, nh, tm):
    l = tri_ref.shape[0]
    grp = lay.group(pl.program_id(0) * tm)
    wh = wh_ref[...]
    wl = wl_ref[...]
    tri = tri_ref[...]
    lane = lax.broadcasted_iota(jnp.int32, (l, LANES), 1)
    is_f = ((lane >= nh) & (lane < 2 * nh)) | ((lane >= 3 * nh) & (lane < 4 * nh))
    is_a = (lane < nh) | ((lane >= 2 * nh) & (lane < 3 * nh))
    for r in range(0, tm, l):
        u = _mod_value(x_ref[r:r + l, :], nw_ref, sh_ref, sc_ref, grp)
        u_hi, u_lo = _split2(u)
        g = _dot_nt(u_hi, wh) + _dot_nt(u_lo, wh) + _dot_nt(u_hi, wl) + b_ref[...]
        lf = jnp.where(is_f, _log_sigmoid(g), 0.0)
        hi, mid, lo = _split3(lf)
        prefix = _dot(tri, hi) + _dot(tri, mid) + _dot(tri, lo)
        suffix = jnp.sum(lf, axis=0, keepdims=True) - prefix + lf
        b = jnp.where(lane < 2 * nh, prefix, suffix)
        a = g - pltpu.roll(b, LANES - nh, 1)
        o_ref[r:r + l, :] = jnp.where(is_a, a, b)


def _gates_call(lay, x, mod, wg, bias, tri, nh):
    t, d = x.shape
    l = tri.shape[0]
    tm = lay.row_tile(1024)
    assert tm % l == 0
    wh, wl = _split2(wg)
    return pl.pallas_call(
        functools.partial(_gates_kernel, lay=lay, nh=nh, tm=tm),
        out_shape=jax.ShapeDtypeStruct((t, LANES), F32),
        grid=(t // tm,),
        in_specs=[pl.BlockSpec((tm, d), lambda i: (i, 0))] + mod.specs + [
            pl.BlockSpec((LANES, d), lambda i: (0, 0)),
            pl.BlockSpec((LANES, d), lambda i: (0, 0)),
            pl.BlockSpec((1, LANES), lambda i: (0, 0)),
            pl.BlockSpec((l, l), lambda i: (0, 0)),
        ],
        out_specs=pl.BlockSpec((tm, LANES), lambda i: (i, 0)),
        compiler_params=_cparams(("parallel",)),
        name="mlstm_gates",
    )(x, *mod.args, wh, wl, bias, tri)


def _proj_kernel(u_ref, w_ref, o_ref, w_bf, *, transposed):
    @pl.when(pl.program_id(1) == 0)
    def _():
        w_bf[...] = w_ref[...].astype(BF16)

    if transposed:
        o_ref[...] = _dot_nt(w_bf[...], u_ref[...]).astype(BF16)
    else:
        o_ref[...] = _dot_nt(u_ref[...], w_bf[...]).astype(BF16)


def _proj_call(lay, u, w_in_t, jl, blocks, tn, transposed):
    t, d = u.shape
    tm = lay.row_tile(1024)
    nb = len(blocks)
    first, gap_at, gap = blocks[0], None, 0
    for idx in range(1, nb):
        if blocks[idx] != blocks[idx - 1] + 1:
            assert gap_at is None
            gap_at, gap = idx, blocks[idx] - blocks[idx - 1] - 1
    wblk = (lambda j: first + j) if gap_at is None else (lambda j: first + j + jnp.where(j >= gap_at, gap, 0))
    if transposed:
        out_shape = jax.ShapeDtypeStruct((nb * tn, t), BF16)
        out_spec = pl.BlockSpec((tn, tm), lambda j, i: (j, i))
    else:
        out_shape = jax.ShapeDtypeStruct((t, nb * tn), BF16)
        out_spec = pl.BlockSpec((tm, tn), lambda j, i: (i, j))
    return pl.pallas_call(
        functools.partial(_proj_kernel, transposed=transposed),
        out_shape=out_shape,
        grid=(nb, t // tm),
        in_specs=[
            pl.BlockSpec((tm, d), lambda j, i: (i, 0)),
            pl.BlockSpec((None, tn, d), lambda j, i: (jl, wblk(j), 0)),
        ],
        out_specs=out_spec,
        scratch_shapes=[pltpu.VMEM((tn, d), BF16)],
        compiler_params=_cparams(("parallel", "arbitrary")),
        name="mlstm_proj_t" if transposed else "mlstm_proj",
    )(u, w_in_t)


def _col(tile, c):
    lane = lax.broadcasted_iota(jnp.int32, tile.shape, 1)
    return jnp.sum(jnp.where(lane == c, tile, 0.0), axis=-1, keepdims=True)


def _dir_masks(l):
    r = lax.broadcasted_iota(jnp.int32, (l, l), 0)
    c = lax.broadcasted_iota(jnp.int32, (l, l), 1)
    return c <= r, c >= r


def _head_epilogue(h, hw, o):
    hn = h * lax.rsqrt(jnp.mean(h * h, axis=-1, keepdims=True) + EPS) * hw
    return (hn * _sigmoid(o.astype(F32))).astype(BF16)


def _row_times_kt(w_row, kt):
    hi, lo = _split2(w_row)
    sub = lax.broadcasted_iota(jnp.int32, (SUBLANES, w_row.shape[1]), 0)
    stacked = jnp.where(sub == 0, hi.astype(F32), jnp.where(sub == 1, lo.astype(F32), 0.0)).astype(BF16)
    res = _dot_nt(stacked, kt)
    return res[0:1, :] + res[1:2, :]


def _mlstm_single_kernel(q_ref, kt_ref, v_ref, o_ref, gp_ref, gpt_ref, hw_ref, out_ref, *, nh, dh, scale):
    l = q_ref.shape[0]
    gp = gp_ref[...]
    masks = _dir_masks(l)
    for h in range(nh):
        cols = slice(h * dh, (h + 1) * dh)
        qk = _dot(q_ref[:, cols], kt_ref[cols, :])
        p = None
        for d in range(2):
            a_r = gpt_ref[2 * nh * d + h:2 * nh * d + h + 1, :]
            b_c = gp[:, 2 * nh * d + nh + h:2 * nh * d + nh + h + 1]
            g = jnp.where(masks[d], a_r, -jnp.inf)
            m = jnp.maximum(jnp.max(g, axis=-1, keepdims=True), 0.0)
            s = qk * jnp.exp(g - m) * scale
            den = jnp.sum(s, axis=-1, keepdims=True)
            inv = 1.0 / jnp.maximum(jnp.abs(den), jnp.exp(-(b_c + m)))
            p = s * inv if p is None else p + s * inv
        hh = _dot(p.astype(BF16), v_ref[:, cols])
        out_ref[:, cols] = _head_epilogue(hh, hw_ref[:, cols], o_ref[:, cols])


def _mlstm_multi_kernel(q_ref, kt_ref, v_ref, o_ref, gp_ref, gpt_ref, hw_ref, c0_ref, n0_ref, m0_ref,
                        out_ref, cst, cbf, *, nh, nc, l, scale):
    h = pl.program_id(1)
    masks = _dir_masks(l)
    m_in = [[None] * nc for _ in range(2)]
    n_in = [[None] * nc for _ in range(2)]
    for d in range(2):
        cst[...] = c0_ref[d]
        n = n0_ref[pl.ds(d * nh + h, 1), :]
        m = m0_ref[pl.ds(d * nh + h, 1), 0:1]
        order = list(range(nc)) if d == 0 else list(range(nc - 1, -1, -1))
        for step, c in enumerate(order):
            m_in[d][c], n_in[d][c] = m, n
            cbf[d, c] = cst[...].astype(BF16)
            if step + 1 < nc:
                r0 = c * l
                a_r = gpt_ref[pl.ds(2 * nh * d + h, 1), r0:r0 + l]
                b_r = gpt_ref[pl.ds(2 * nh * d + nh + h, 1), r0:r0 + l]
                m_last = jnp.maximum(jnp.max(a_r, axis=-1, keepdims=True), m)
                b_end = b_r[:, l - 1:l] if d == 0 else b_r[:, 0:1]
                decay = jnp.exp(m - m_last)
                w_end = jnp.exp(a_r - m_last)
                kt = kt_ref[:, r0:r0 + l]
                cst[...] = decay * cst[...] + _dot((kt.astype(F32) * w_end).astype(BF16), v_ref[r0:r0 + l, :])
                n = decay * n + _row_times_kt(w_end, kt)
                m = b_end + m_last
    hw = hw_ref[...]
    for c in range(nc):
        r0 = c * l
        q = q_ref[r0:r0 + l, :]
        v = v_ref[r0:r0 + l, :]
        qk = _dot(q, kt_ref[:, r0:r0 + l])
        qf = q.astype(F32)
        gp = gp_ref[r0:r0 + l, :]
        p = inter = None
        for d in range(2):
            m, n = m_in[d][c], n_in[d][c]
            a_r = gpt_ref[pl.ds(2 * nh * d + h, 1), r0:r0 + l]
            b_c = _col(gp, 2 * nh * d + nh + h)
            g = jnp.where(masks[d], a_r, -jnp.inf)
            mt = jnp.maximum(jnp.max(g, axis=-1, keepdims=True), m)
            s = qk * jnp.exp(g - mt) * scale
            w_prev = jnp.exp(m - mt) * scale
            den = jnp.sum(s, axis=-1, keepdims=True) + w_prev * jnp.sum(qf * n, axis=-1, keepdims=True)
            inv = 1.0 / jnp.maximum(jnp.abs(den), jnp.exp(-(b_c + mt)))
            term = (w_prev * inv) * _dot(q, cbf[d, c])
            p = s * inv if p is None else p + s * inv
            inter = term if inter is None else inter + term
        hh = _dot(p.astype(BF16), v) + inter
        out_ref[r0:r0 + l, :] = _head_epilogue(hh, hw, o_ref[r0:r0 + l, :])


def _mlstm_call(lay, hg, qvo, kt, gp, gpt, hw, nh, dh, prompt, state=None):
    nb, s = (lay.nbp, lay.sp) if prompt else (lay.nbs, lay.ss)
    rb0 = 0 if prompt else lay.tp // s
    scale = dh ** -0.5
    common_in = [
        pl.BlockSpec((s, dh), lambda b, h: (rb0 + b, h)),
        pl.BlockSpec((dh, s), lambda b, h: (h, rb0 + b)),
        pl.BlockSpec((s, dh), lambda b, h: (rb0 + b, nh + h)),
        pl.BlockSpec((s, dh), lambda b, h: (rb0 + b, 2 * nh + h)),
        pl.BlockSpec((s, LANES), lambda b, h: (rb0 + b, 0)),
        pl.BlockSpec((4 * nh, s), lambda b, h: (0, rb0 + b)),
        pl.BlockSpec((1, dh), lambda b, h: (0, h)),
    ]
    hg_spec = pl.BlockSpec((s, dh), lambda b, h: (rb0 + b, h))
    any_spec = pl.BlockSpec(memory_space=pl.ANY)
    if prompt:
        assert s == MLSTM_L
        di = nh * dh
        kern = functools.partial(_mlstm_single_kernel, nh=nh, dh=dh, scale=scale)

        def body(hg_any, *refs):
            kern(*refs)

        return pl.pallas_call(
            body,
            out_shape=jax.ShapeDtypeStruct(hg.shape, hg.dtype),
            grid=(nb,),
            in_specs=[
                any_spec,
                pl.BlockSpec((s, di), lambda b: (rb0 + b, 0)),
                pl.BlockSpec((di, s), lambda b: (0, rb0 + b)),
                pl.BlockSpec((s, di), lambda b: (rb0 + b, 1)),
                pl.BlockSpec((s, di), lambda b: (rb0 + b, 2)),
                pl.BlockSpec((s, LANES), lambda b: (rb0 + b, 0)),
                pl.BlockSpec((4 * nh, s), lambda b: (0, rb0 + b)),
                pl.BlockSpec((1, di), lambda b: (0, 0)),
            ],
            out_specs=pl.BlockSpec((s, di), lambda b: (rb0 + b, 0)),
            input_output_aliases={0: 0},
            compiler_params=_cparams(("parallel",)),
            name="mlstm_prompt",
        )(hg, qvo, kt, qvo, qvo, gp, gpt, hw)
    state_c, jl, n0, m0 = state
    nc = s // MLSTM_L
    kern = functools.partial(_mlstm_multi_kernel, nh=nh, nc=nc, l=MLSTM_L, scale=scale)

    def body(hg_any, *refs):
        kern(*refs)

    return pl.pallas_call(
        body,
        out_shape=jax.ShapeDtypeStruct(hg.shape, hg.dtype),
        grid=(nb, nh),
        in_specs=[any_spec] + common_in + [
            pl.BlockSpec((None, None, 2, None, dh, dh), lambda b, h: (b, jl, 0, h, 0, 0)),
            pl.BlockSpec((None, 2 * nh, dh), lambda b, h: (b, 0, 0)),
            pl.BlockSpec((None, 2 * nh, LANES), lambda b, h: (b, 0, 0)),
        ],
        out_specs=hg_spec,
        scratch_shapes=[pltpu.VMEM((dh, dh), F32), pltpu.VMEM((2, nc, dh, dh), BF16)],
        input_output_aliases={0: 0},
        compiler_params=_cparams(("parallel", "parallel")),
        name="mlstm_latent",
    )(hg, qvo, kt, qvo, qvo, gp, gpt, hw, state_c, n0, m0)


def _state_kernel(*refs, nl, nh, dh):
    ins, (c_ref, n_ref, m_ref) = refs[:3 * nl], refs[3 * nl:]
    lyr = pl.program_id(0)
    for jl in range(nl):
        kt_ref, v_ref, gpt_ref = ins[3 * jl:3 * jl + 3]

        @pl.when(lyr == jl)
        def _(kt_ref=kt_ref, v_ref=v_ref, gpt_ref=gpt_ref):
            l = v_ref.shape[0]
            sub = lax.broadcasted_iota(jnp.int32, m_ref.shape, 0)
            lane = lax.broadcasted_iota(jnp.int32, m_ref.shape, 1)
            m_all = jnp.zeros(m_ref.shape, F32)
            for h in range(nh):
                kt = kt_ref[h * dh:(h + 1) * dh, :]
                ktf = kt.astype(F32)
                v = v_ref[:, h * dh:(h + 1) * dh]
                for d in range(2):
                    a_r = gpt_ref[2 * nh * d + h:2 * nh * d + h + 1, :]
                    b_r = gpt_ref[2 * nh * d + nh + h:2 * nh * d + nh + h + 1, :]
                    m_last = jnp.maximum(jnp.max(a_r, axis=-1, keepdims=True), 0.0)
                    b_end = b_r[:, l - 1:l] if d == 0 else b_r[:, 0:1]
                    w_end = jnp.exp(a_r - m_last)
                    c_ref[d, h] = _dot((ktf * w_end).astype(BF16), v)
                    n_ref[d, h:h + 1, :] = _row_times_kt(w_end, kt)
                    m_all = jnp.where((sub == d) & (lane == h), b_end + m_last, m_all)
            m_ref[...] = m_all


def _state_call(lay, qvos, kts, gpts, nh, dh):
    nl = len(qvos)
    nbp, s = lay.nbp, lay.sp
    di = nh * dh
    assert s == MLSTM_L

    def pick(jl):
        return lambda lyr, b: jnp.where(lyr == jl, b, jnp.where(lyr < jl, 0, nbp - 1))

    in_specs, args = [], []
    for jl in range(nl):
        pb = pick(jl)
        in_specs.append(pl.BlockSpec((di, s), lambda lyr, b, pb=pb: (0, pb(lyr, b))))
        in_specs.append(pl.BlockSpec((s, di), lambda lyr, b, pb=pb: (pb(lyr, b), 1)))
        in_specs.append(pl.BlockSpec((4 * nh, s), lambda lyr, b, pb=pb: (0, pb(lyr, b))))
        args += [kts[jl], qvos[jl], gpts[jl]]
    return pl.pallas_call(
        functools.partial(_state_kernel, nl=nl, nh=nh, dh=dh),
        out_shape=(
            jax.ShapeDtypeStruct((nbp, nl, 2, nh, dh, dh), F32),
            jax.ShapeDtypeStruct((nbp, nl, 2, nh, dh), F32),
            jax.ShapeDtypeStruct((nbp, nl, 2, nh), F32),
        ),
        grid=(nl, nbp),
        in_specs=in_specs,
        out_specs=(
            pl.BlockSpec((None, None, 2, nh, dh, dh), lambda lyr, b: (b, lyr, 0, 0, 0, 0)),
            pl.BlockSpec((None, None, 2, nh, dh), lambda lyr, b: (b, lyr, 0, 0, 0)),
            pl.BlockSpec((None, None, 2, nh), lambda lyr, b: (b, lyr, 0, 0)),
        ),
        compiler_params=_cparams(("arbitrary", "arbitrary")),
        name="mlstm_prompt_state",
    )(*args)


def _mm_res_kernel(a_ref, w_ref, x_ref, g_ref, nw_ref, sh_ref, sc_ref, o_ref, u_ref, *, lay, tm):
    grp = lay.group(pl.program_id(0) * tm)
    gate = g_ref[pl.ds(grp, 1), :]
    w = w_ref[...]
    rc = min(tm, 256)
    for r in range(0, tm, rc):
        xn = x_ref[r:r + rc, :] + gate * _dot(a_ref[r:r + rc, :], w)
        o_ref[r:r + rc, :] = xn
        u_ref[r:r + rc, :] = _mod_value(xn, nw_ref, sh_ref, sc_ref, grp).astype(BF16)


def _mm_res_call(lay, a, w_bf, x, mods, l, mod2):
    t, kdim = a.shape
    d = x.shape[1]
    tm = lay.row_tile(512)
    return pl.pallas_call(
        functools.partial(_mm_res_kernel, lay=lay, tm=tm),
        out_shape=(jax.ShapeDtypeStruct(x.shape, F32), jax.ShapeDtypeStruct(x.shape, BF16)),
        grid=(t // tm,),
        in_specs=[
            pl.BlockSpec((tm, kdim), lambda i: (i, 0)),
            pl.BlockSpec((kdim, d), lambda i: (0, 0)),
            pl.BlockSpec((tm, d), lambda i: (i, 0)),
            lay.mod_spec(l, 2),
        ] + mod2.specs,
        out_specs=(pl.BlockSpec((tm, d), lambda i: (i, 0)), pl.BlockSpec((tm, d), lambda i: (i, 0))),
        compiler_params=_cparams(("parallel",)),
        name="mm_residual",
    )(a, w_bf, x, mods, *mod2.args)


def _fnet_kernel(x_ref, u_ref, u2_any, gt_ref, wc_ref, ds_ref, wo_ref, nw_ref, sh_ref, sc_ref, o_ref, u2_ref,
                 ab_scr, *, lay, row_base, groups, norm):
    s, d = x_ref.shape
    cg = d // groups
    grp = lay.group(row_base + pl.program_id(0) * s)
    wc = wc_ref[...]
    rc = min(s, 256)
    for g in range(groups):
        for r in range(0, s, rc):
            ab = _dot(u_ref[r:r + rc, g * cg:(g + 1) * cg], wc)
            ab_scr[r:r + rc, g * cg:(g + 1) * cg] = ab[:, :cg].astype(BF16)
            ab_scr[s + r:s + r + rc, g * cg:(g + 1) * cg] = ab[:, cg:].astype(BF16)
    gate = gt_ref[pl.ds(grp, 1), :]
    wo = wo_ref[...]
    for r in range(0, s, rc):
        y = _dot(ds_ref[r:r + rc, :], ab_scr[...]) * norm
        xn = x_ref[r:r + rc, :] + gate * _dot(y.astype(BF16), wo)
        o_ref[r:r + rc, :] = xn
        u2_ref[r:r + rc, :] = _mod_value(xn, nw_ref, sh_ref, sc_ref, grp).astype(BF16)


def _dft_mats(s, cg):
    kc = np.arange(cg)
    ang_c = 2.0 * np.pi * np.outer(kc, kc) / cg
    wc = np.concatenate([np.cos(ang_c), np.sin(ang_c)], axis=1)
    ks = np.arange(s)
    ang_s = 2.0 * np.pi * np.outer(ks, ks) / s
    ds = np.concatenate([np.cos(ang_s), -np.sin(ang_s)], axis=1)
    return jnp.asarray(wc, dtype=BF16), jnp.asarray(ds, dtype=BF16)


def _fnet_call(lay, x, u1, u2, mods, l, wo_bf, mod2, prompt):
    nb, s = (lay.nbp, lay.sp) if prompt else (lay.nbs, lay.ss)
    rb0 = 0 if prompt else lay.tp // s
    d = lay.d
    cg = d // FNET_GROUPS
    wc, ds = _dft_mats(s, cg)
    kern = functools.partial(_fnet_kernel, lay=lay, row_base=rb0 * s, groups=FNET_GROUPS,
                             norm=1.0 / math.sqrt(s * cg))
    blk = pl.BlockSpec((s, d), lambda b: (rb0 + b, 0))
    return pl.pallas_call(
        kern,
        out_shape=(jax.ShapeDtypeStruct(x.shape, F32), jax.ShapeDtypeStruct(u2.shape, BF16)),
        grid=(nb,),
        in_specs=[
            blk,
            blk,
            pl.BlockSpec(memory_space=pl.ANY),
            lay.mod_spec(l, 2),
            pl.BlockSpec((cg, 2 * cg), lambda b: (0, 0)),
            pl.BlockSpec((s, 2 * s), lambda b: (0, 0)),
            pl.BlockSpec((d, d), lambda b: (0, 0)),
        ] + mod2.specs,
        out_specs=(blk, blk),
        scratch_shapes=[pltpu.VMEM((2 * s, d), BF16)],
        input_output_aliases={0: 0, 2: 1},
        compiler_params=_cparams(("parallel",)),
        name="fnet_prompt" if prompt else "fnet_latent",
    )(x, u1, u2, mods, wc, ds, wo_bf, *mod2.args)


def _glu_kernel(u_ref, wa_ref, wg_ref, ba_ref, bg_ref, o_ref):
    u = u_ref[...]
    a = _dot(u, wa_ref[...]) + ba_ref[...]
    g = _dot(u, wg_ref[...]) + bg_ref[...]
    o_ref[...] = a * _sigmoid(g)


def _glu_call(lay, u, w_bf, bias):
    t, d = u.shape
    cd = w_bf.shape[1] // 2
    tm = lay.row_tile(1024)
    tn = 512
    nj = cd // tn
    return pl.pallas_call(
        _glu_kernel,
        out_shape=jax.ShapeDtypeStruct((t, cd), F32),
        grid=(t // tm, nj),
        in_specs=[
            pl.BlockSpec((tm, d), lambda i, j: (i, 0)),
            pl.BlockSpec((d, tn), lambda i, j: (0, j)),
            pl.BlockSpec((d, tn), lambda i, j: (0, nj + j)),
            pl.BlockSpec((1, tn), lambda i, j: (0, j)),
            pl.BlockSpec((1, tn), lambda i, j: (0, nj + j)),
        ],
        out_specs=pl.BlockSpec((tm, tn), lambda i, j: (i, j)),
        compiler_params=_cparams(("parallel", "parallel")),
        name="conv_glu",
    )(u, w_bf, w_bf, bias, bias)


def _conv_kernel(c_ref, p_ref, n_ref, wd_ref, bd_ref, lw_ref, lb_ref, w2_ref, b2_ref, x_ref, gt_ref,
                 nw_ref, sh_ref, sc_ref, o_ref, u2_ref, pad, conv, act, *, lay, rb, width):
    i = pl.program_id(0)
    row0 = i * rb
    grp = lay.group(row0)
    seq = jnp.where(row0 < lay.tp, lay.sp, lay.ss)
    pos = jnp.where(row0 < lay.tp, row0 % lay.sp, (row0 - lay.tp) % lay.ss)
    has_prev = (pos != 0).astype(F32)
    has_next = (pos + rb != seq).astype(F32)
    hl = CONV_HALO
    half = width // 2
    cd = c_ref.shape[1]
    span = pad.shape[1]
    pad[0, 0:hl, :] = p_ref[...] * has_prev
    pad[0, hl:hl + rb, :] = c_ref[...]
    pad[0, hl + rb:hl + rb + hl, :] = n_ref[...] * has_next
    for s in range(1, SUBLANES):
        pad[s, 0:span - SUBLANES, :] = pad[0, s:s + span - SUBLANES, :]
    ngrp = 8
    sub = ngrp * SUBLANES
    lanes = 2 * LANES
    assert rb % sub == 0 and cd % lanes == 0

    def conv_block(blk, carry):
        r0 = pl.multiple_of(blk * sub, sub)
        for c0 in range(0, cd, lanes):
            bias = bd_ref[:, c0:c0 + lanes]
            accs = [jnp.zeros((SUBLANES, lanes), F32) + bias for _ in range(ngrp)]
            for k in sorted(range(width), key=lambda k: ((hl - half + k) % SUBLANES, k)):
                q, s = divmod(hl - half + k, SUBLANES)
                wk = wd_ref[k, :, c0:c0 + lanes]
                for gi in range(ngrp):
                    win = pad[s, pl.ds(r0 + (q + gi) * SUBLANES, SUBLANES), c0:c0 + lanes]
                    accs[gi] = accs[gi] + win * wk
            conv[pl.ds(r0, sub), c0:c0 + lanes] = jnp.concatenate(accs, axis=0)
        return carry

    lax.fori_loop(0, rb // sub, conv_block, 0)
    lw = lw_ref[...]
    lb = lb_ref[...]
    lsub = min(rb, 16 * SUBLANES)

    def ln_block(blk, carry):
        r0 = pl.multiple_of(blk * lsub, lsub)
        acc = conv[pl.ds(r0, lsub), :]
        mu = jnp.mean(acc, axis=-1, keepdims=True)
        cen = acc - mu
        var = jnp.mean(cen * cen, axis=-1, keepdims=True)
        y = cen * lax.rsqrt(var + EPS) * lw + lb
        act[pl.ds(r0, lsub), :] = (y * _sigmoid(y)).astype(BF16)
        return carry

    lax.fori_loop(0, rb // lsub, ln_block, 0)
    xn = x_ref[...] + gt_ref[pl.ds(grp, 1), :] * (_dot(act[...], w2_ref[...]) + b2_ref[...])
    o_ref[...] = xn
    u2_ref[...] = _mod_value(xn, nw_ref, sh_ref, sc_ref, grp).astype(BF16)


def _conv_call(lay, glu, wd, bd, lw, lb, w2_bf, b2, x, mods, l, mod2):
    t, cd = glu.shape
    d = x.shape[1]
    rb = lay.row_tile(256)
    hl = CONV_HALO
    assert CONV_WIDTH // 2 <= hl and rb % hl == 0
    nhb = t // hl
    per = rb // hl
    wd_p = jnp.broadcast_to(wd[:, None, :], (CONV_WIDTH, SUBLANES, cd))
    row = lambda a: a.reshape(1, -1)
    rows = pl.BlockSpec((rb, d), lambda i: (i, 0))
    return pl.pallas_call(
        functools.partial(_conv_kernel, lay=lay, rb=rb, width=CONV_WIDTH),
        out_shape=(jax.ShapeDtypeStruct(x.shape, F32), jax.ShapeDtypeStruct(x.shape, BF16)),
        grid=(t // rb,),
        in_specs=[
            pl.BlockSpec((rb, cd), lambda i: (i, 0)),
            pl.BlockSpec((hl, cd), lambda i: (jnp.maximum(i * per - 1, 0), 0)),
            pl.BlockSpec((hl, cd), lambda i: (jnp.minimum((i + 1) * per, nhb - 1), 0)),
            pl.BlockSpec(wd_p.shape, lambda i: (0, 0, 0)),
            pl.BlockSpec((1, cd), lambda i: (0, 0)),
            pl.BlockSpec((1, cd), lambda i: (0, 0)),
            pl.BlockSpec((1, cd), lambda i: (0, 0)),
            pl.BlockSpec((cd, d), lambda i: (0, 0)),
            pl.BlockSpec((1, d), lambda i: (0, 0)),
            rows,
            lay.mod_spec(l, 2),
        ] + mod2.specs,
        out_specs=(rows, rows),
        scratch_shapes=[pltpu.VMEM((SUBLANES, rb + 2 * hl, cd), F32), pltpu.VMEM((rb, cd), F32),
                        pltpu.VMEM((rb, cd), BF16)],
        compiler_params=_cparams(("parallel",)),
        name="conv_dw_ln_pw2",
    )(glu, glu, glu, wd_p, row(bd), row(lw), row(lb), w2_bf, row(b2), x, mods, *mod2.args)


def _router_kernel(u_ref, wr_ref, upper_ref, ltri_ref, o_ref, cnt_ref, *, ng, ne):
    tm = u_ref.shape[0]
    logits = _dot_nt(wr_ref[...], u_ref[...])
    neg = -jnp.inf
    row = lax.broadcasted_iota(jnp.int32, (SUBLANES, tm), 0)

    gl = jnp.where(row < ng, logits[0:SUBLANES, :], neg)
    gmax = jnp.max(gl, axis=0, keepdims=True)
    gidx = jnp.min(jnp.where(gl == gmax, row, SUBLANES), axis=0, keepdims=True)
    g_p = 1.0 / jnp.sum(jnp.where(row < ng, jnp.exp(gl - gmax), 0.0), axis=0, keepdims=True)

    sel = logits[SUBLANES:2 * SUBLANES, :]
    for g in range(1, ng):
        sel = jnp.where(gidx == g, logits[(1 + g) * SUBLANES:(2 + g) * SUBLANES, :], sel)
    v1 = jnp.max(sel, axis=0, keepdims=True)
    i1 = jnp.min(jnp.where(sel == v1, row, SUBLANES), axis=0, keepdims=True)
    sel2 = jnp.where(row == i1, neg, sel)
    v2 = jnp.max(sel2, axis=0, keepdims=True)
    i2 = jnp.min(jnp.where(sel2 == v2, row, SUBLANES), axis=0, keepdims=True)
    e1 = gidx * SUBLANES + i1
    e2 = gidx * SUBLANES + i2
    tt = jnp.exp(v2 - v1)
    p1 = 1.0 / (1.0 + tt)
    gate1 = p1 * g_p
    gate2 = (tt * p1) * g_p

    rowe = lax.broadcasted_iota(jnp.int32, (ne, tm), 0)
    oh1 = rowe == e1
    oh2 = rowe == e2
    oh = jnp.where(oh1 | oh2, 1.0, 0.0)
    groups = jnp.floor((jnp.sum(oh, axis=1, keepdims=True) + (SUBLANES - 1)) * (1.0 / SUBLANES))
    groups_b = jnp.broadcast_to(groups, (ne, LANES))
    start = SUBLANES * _dot(ltri_ref[...], groups_b.astype(BF16))[:, 0:1]
    prefix = _dot(oh.astype(BF16), upper_ref[...]) + start
    pos1 = jnp.sum(jnp.where(oh1, prefix, 0.0), axis=0, keepdims=True)
    pos2 = jnp.sum(jnp.where(oh2, prefix, 0.0), axis=0, keepdims=True)
    cnt_ref[...] = groups_b

    out = jnp.where(row == 0, e1.astype(F32), 0.0)
    out = jnp.where(row == 1, e2.astype(F32), out)
    out = jnp.where(row == 2, gate1, out)
    out = jnp.where(row == 3, gate2, out)
    out = jnp.where(row == 4, pos1, out)
    out = jnp.where(row == 5, pos2, out)
    o_ref[...] = out


def _router_call(u, wr_t, upper, ltri, ng, ne):
    t, d = u.shape
    tm = upper.shape[0]
    nt = t // tm
    assert ne // ng == SUBLANES and ng <= SUBLANES and SUBLANES + ne <= LANES
    return pl.pallas_call(
        functools.partial(_router_kernel, ng=ng, ne=ne),
        out_shape=(jax.ShapeDtypeStruct((nt, SUBLANES, tm), F32), jax.ShapeDtypeStruct((nt, ne, LANES), F32)),
        grid=(nt,),
        in_specs=[
            pl.BlockSpec((tm, d), lambda i: (i, 0)),
            pl.BlockSpec((LANES, d), lambda i: (0, 0)),
            pl.BlockSpec((tm, tm), lambda i: (0, 0)),
            pl.BlockSpec((ne, ne), lambda i: (0, 0)),
        ],
        out_specs=(pl.BlockSpec((None, SUBLANES, tm), lambda i: (i, 0, 0)),
                   pl.BlockSpec((None, ne, LANES), lambda i: (i, 0, 0))),
        compiler_params=_cparams(("parallel",)),
        name="moe_router",
    )(u, wr_t, upper, ltri)


def _pack_halves(lo, hi):
    lo_bits = lax.shift_right_logical(pltpu.bitcast(lo, U32), jnp.uint32(16))
    hi_bits = pltpu.bitcast(hi, U32) & jnp.uint32(0xFFFF0000)
    return hi_bits | lo_bits


def _unpack_halves(w):
    lo = pltpu.bitcast(lax.shift_left(w, jnp.uint32(16)), F32)
    hi = pltpu.bitcast(w & jnp.uint32(0xFFFF0000), F32)
    return lo.astype(BF16), hi.astype(BF16)


def _round_bf16(x):
    return x.astype(BF16).astype(F32)


def _group_copy(src, src_g, dst, dst_g, sem):
    g8 = lambda g: pl.ds(pl.multiple_of(g * SUBLANES, SUBLANES), SUBLANES)
    return pltpu.make_async_copy(src.at[g8(src_g), :], dst.at[g8(dst_g), :], sem)


def _for_groups(n, fn, unroll=4):
    def body_many(i, c):
        for j in range(unroll):
            fn(i * unroll + j)
        return c

    def body_one(g, c):
        fn(g)
        return c

    full = lax.div(n, jnp.int32(unroll))
    lax.fori_loop(0, full, body_many, 0)
    lax.fori_loop(full * unroll, n, body_one, 0)


def _dispatch_kernel(gdst_ref, ngt_ref, pad0_ref, npad_ref, tail_ref, u_ref, pos_ref, xs_out, loc, zeros, sem,
                     zsem, *, tm, nl, ne):
    i = pl.program_id(0)
    nt = pl.num_programs(0)
    slot = i % 2
    nlg = nl // SUBLANES

    def copy(step, s, g):
        return _group_copy(loc.at[s], g, xs_out, gdst_ref[step * nlg + g], sem.at[s])

    def group_wait(s):
        _group_copy(loc.at[s], 0, xs_out, 0, sem.at[s]).wait()

    def zero_copy(e, g):
        return _group_copy(zeros, 0, xs_out, pad0_ref[e] + g, zsem)

    def zero_block(t):
        first = pl.multiple_of(tail_ref[0] + t * MOE_BM, MOE_BM)
        return pltpu.make_async_copy(zeros, xs_out.at[pl.ds(first, MOE_BM), :], zsem)

    @pl.when(i == 0)
    def _():
        zeros[...] = jnp.zeros(zeros.shape, zeros.dtype)
        for e in range(ne):
            _for_groups(npad_ref[e], lambda g, e=e: zero_copy(e, g).start())
        _for_groups(tail_ref[1], lambda t: zero_block(t).start())

    @pl.when(i >= 2)
    def _():
        _for_groups(ngt_ref[i - 2], lambda g: group_wait(slot))

    pos1 = pos_ref[4:5, :].astype(jnp.int32)
    pos2 = pos_ref[5:6, :].astype(jnp.int32)
    half = u_ref.shape[1] // 2
    u = u_ref[...]
    rc = 256
    for r in range(0, nl, rc):
        p = lax.broadcasted_iota(jnp.int32, (rc, tm), 0) + r
        onehot = jnp.where((p == pos1) | (p == pos2), 1.0, 0.0).astype(BF16)
        rows = _dot(onehot, u)
        loc[slot, r:r + rc, :] = _pack_halves(rows[:, :half], rows[:, half:])
    _for_groups(ngt_ref[i], lambda g: copy(i, slot, g).start())

    @pl.when(i == nt - 1)
    def _():
        @pl.when(i >= 1)
        def _():
            _for_groups(ngt_ref[i - 1], lambda g: group_wait(1 - slot))

        _for_groups(ngt_ref[i], lambda g: group_wait(slot))
        for e in range(ne):
            _for_groups(npad_ref[e], lambda g: _group_copy(zeros, 0, xs_out, 0, zsem).wait())
        _for_groups(tail_ref[1], lambda t: pltpu.make_async_copy(zeros, xs_out.at[0:MOE_BM, :], zsem).wait())


def _dispatch_call(u, pos_rows, gdst, ngt, pad0, npad, tail, nrows, tm, nl):
    t, d = u.shape
    ne = npad.shape[0]
    grid_spec = pltpu.PrefetchScalarGridSpec(
        num_scalar_prefetch=5,
        grid=(t // tm,),
        in_specs=[
            pl.BlockSpec((tm, d), lambda i, *_: (i, 0)),
            pl.BlockSpec((None, SUBLANES, tm), lambda i, *_: (i, 0, 0)),
        ],
        out_specs=pl.BlockSpec(memory_space=pl.ANY),
        scratch_shapes=[pltpu.VMEM((2, nl, d // 2), U32), pltpu.VMEM((MOE_BM, d // 2), U32),
                        pltpu.SemaphoreType.DMA((2,)), pltpu.SemaphoreType.DMA(())],
    )
    return pl.pallas_call(
        functools.partial(_dispatch_kernel, tm=tm, nl=nl, ne=ne),
        out_shape=jax.ShapeDtypeStruct((nrows, d // 2), U32),
        grid_spec=grid_spec,
        compiler_params=_cparams(("arbitrary",)),
        name="moe_dispatch",
    )(gdst, ngt, pad0, npad, tail, u, pos_rows)


def _expert_kernel(row0_ref, nblk_ref, tail_ref, x_hbm, w13_ref, w2_ref, y_hbm, xbuf, ybuf, w13_bf, w2_bf,
                   xsem, ysem, *, hid):
    e = pl.program_id(0)
    n = nblk_ref[e]
    g0 = row0_ref[e] // MOE_BM
    total = tail_ref[0] // MOE_BM
    nx, ny = xbuf.shape[0], ybuf.shape[0]
    ahead = nx - 1
    half = xbuf.shape[2]

    def rows(g):
        return pl.ds(pl.multiple_of(g * MOE_BM, MOE_BM), MOE_BM)

    def x_copy(g):
        return pltpu.make_async_copy(x_hbm.at[rows(g), :], xbuf.at[g % nx], xsem.at[g % nx])

    def y_copy(g, s):
        return pltpu.make_async_copy(ybuf.at[s], y_hbm.at[rows(g), :], ysem.at[s])

    @pl.when(e == 0)
    def _():
        for g in range(ahead):
            @pl.when(g < total)
            def _(g=g):
                x_copy(g).start()

    @pl.when(n > 0)
    def _():
        w13_bf[...] = w13_ref[...].astype(BF16)
        w2_bf[...] = w2_ref[...].astype(BF16)

        def block(c, carry):
            g = g0 + c
            x_copy(g).wait()

            @pl.when(g + ahead < total)
            def _():
                x_copy(g + ahead).start()

            @pl.when(g >= ny)
            def _():
                y_copy(g - ny, g % ny).wait()

            x_lo, x_hi = _unpack_halves(xbuf[g % nx])
            hb = _dot(x_lo, w13_bf[:half, :]) + _dot(x_hi, w13_bf[half:, :])
            a = hb[:, :hid]
            act = (a * _sigmoid(a)) * hb[:, hid:]
            y = _round_bf16(_dot(act.astype(BF16), w2_bf[...]))
            ybuf[g % ny] = _pack_halves(y[:, :half], y[:, half:])
            y_copy(g, g % ny).start()
            return carry

        lax.fori_loop(0, n, block, 0)

    @pl.when(e == pl.num_programs(0) - 1)
    def _():
        for j in range(ny):
            @pl.when(total - ny + j >= 0)
            def _(j=j):
                g = total - ny + j
                y_copy(g, g % ny).wait()

        ybuf[0] = jnp.zeros(ybuf.shape[1:], ybuf.dtype)
        _for_groups(tail_ref[1], lambda t: y_copy(total + t, 0).start())
        _for_groups(tail_ref[1], lambda t: y_copy(total + t, 0).wait())


def _expert_call(xs, row0, nblk, tail, w13, w2, l):
    r, half = xs.shape
    d = 2 * half
    ne = w13.shape[1]
    hid = w2.shape[2]
    nx, ny = 5, 3
    grid_spec = pltpu.PrefetchScalarGridSpec(
        num_scalar_prefetch=3,
        grid=(ne,),
        in_specs=[
            pl.BlockSpec(memory_space=pl.ANY),
            pl.BlockSpec((None, None, d, 2 * hid), lambda e, *_: (l, e, 0, 0)),
            pl.BlockSpec((None, None, hid, d), lambda e, *_: (l, e, 0, 0)),
        ],
        out_specs=pl.BlockSpec(memory_space=pl.ANY),
        scratch_shapes=[
            pltpu.VMEM((nx, MOE_BM, half), U32), pltpu.VMEM((ny, MOE_BM, half), U32),
            pltpu.VMEM((d, 2 * hid), BF16), pltpu.VMEM((hid, d), BF16),
            pltpu.SemaphoreType.DMA((nx,)), pltpu.SemaphoreType.DMA((ny,)),
        ],
    )
    return pl.pallas_call(
        functools.partial(_expert_kernel, hid=hid),
        out_shape=jax.ShapeDtypeStruct((r, half), U32),
        grid_spec=grid_spec,
        compiler_params=_cparams(("arbitrary",)),
        name="moe_experts",
    )(row0, nblk, tail, xs, w13, w2)


def _combine_kernel(gdst_ref, ngt_ref, x_ref, rt_ref, gt_ref, nw_ref, sh_ref, sc_ref, ys_ref, out_a, out_b,
                    loc, sem, *, lay, tm, nl, final):
    i = pl.program_id(0)
    nt = pl.num_programs(0)
    slot = i % 2
    nlg = nl // SUBLANES

    def copy(step, s, g):
        return _group_copy(ys_ref, gdst_ref[step * nlg + g], loc.at[s], g, sem.at[s])

    @pl.when(i == 0)
    def _():
        loc[...] = jnp.zeros(loc.shape, loc.dtype)
        _for_groups(ngt_ref[0], lambda g: copy(0, 0, g).start())

    _for_groups(ngt_ref[i], lambda g: _group_copy(ys_ref, 0, loc.at[slot], 0, sem.at[slot]).wait())

    @pl.when(i + 1 < nt)
    def _():
        _for_groups(ngt_ref[i + 1], lambda g: copy(i + 1, 1 - slot, g).start())

    rt = rt_ref[...]
    gate1, gate2 = rt[:, 2:3], rt[:, 3:4]
    pos1, pos2 = rt[:, 4:5].astype(jnp.int32), rt[:, 5:6].astype(jnp.int32)
    half = loc.shape[2]
    rc = 256
    mix_lo = jnp.zeros((tm, half), F32)
    mix_hi = jnp.zeros((tm, half), F32)
    for r in range(0, nl, rc):
        p = lax.broadcasted_iota(jnp.int32, (tm, rc), 1) + r
        wgt = (jnp.where(p == pos1, gate1, 0.0) + jnp.where(p == pos2, gate2, 0.0)).astype(BF16)
        y_lo, y_hi = _unpack_halves(loc[slot, r:r + rc, :])
        mix_lo = mix_lo + _dot(wgt, y_lo)
        mix_hi = mix_hi + _dot(wgt, y_hi)
    grp = lay.group(i * tm)
    gate = gt_ref[pl.ds(grp, 1), :]
    x_lo = x_ref[:, :half] + gate[:, :half] * mix_lo
    x_hi = x_ref[:, half:] + gate[:, half:] * mix_hi
    ms = (jnp.sum(x_lo * x_lo, axis=-1, keepdims=True) + jnp.sum(x_hi * x_hi, axis=-1, keepdims=True)) / (2 * half)
    inv = lax.rsqrt(ms + EPS)
    nw = nw_ref[...]
    if final:
        y_lo = x_lo * inv * nw[:, :half]
        y_hi = x_hi * inv * nw[:, half:]
        for ref, cond in ((out_a, i * tm < lay.tp), (out_b, i * tm >= lay.tp)):
            @pl.when(cond)
            def _(ref=ref):
                ref[:, :half] = y_lo
                ref[:, half:] = y_hi
    else:
        out_a[:, :half] = x_lo
        out_a[:, half:] = x_hi
        sh = sh_ref[pl.ds(grp, 1), :]
        sc = sc_ref[pl.ds(grp, 1), :]
        out_b[:, :half] = (x_lo * inv * nw[:, :half] * (1.0 + sc[:, :half]) + sh[:, :half]).astype(BF16)
        out_b[:, half:] = (x_hi * inv * nw[:, half:] * (1.0 + sc[:, half:]) + sh[:, half:]).astype(BF16)


def _combine_call(lay, x, route, mods, l, mod_next, gdst, ngt, ys, tm, nl, final):
    t, d = x.shape
    rows = pl.BlockSpec((tm, d), lambda i, *_: (i, 0))
    if final:
        ntp = lay.tp // tm
        out_specs = (pl.BlockSpec((tm, d), lambda i, *_: (jnp.minimum(i, ntp - 1), 0)),
                     pl.BlockSpec((tm, d), lambda i, *_: (jnp.maximum(i - ntp, 0), 0)))
        out_shape = (jax.ShapeDtypeStruct((lay.tp, d), F32), jax.ShapeDtypeStruct((lay.ts, d), F32))
    else:
        out_specs = (rows, rows)
        out_shape = (jax.ShapeDtypeStruct(x.shape, F32), jax.ShapeDtypeStruct(x.shape, BF16))
    grid_spec = pltpu.PrefetchScalarGridSpec(
        num_scalar_prefetch=2,
        grid=(t // tm,),
        in_specs=[
            rows,
            pl.BlockSpec((tm, SUBLANES), lambda i, *_: (i, 0)),
            lay.mod_spec(l, 5),
        ] + mod_next.specs + [pl.BlockSpec(memory_space=pl.ANY)],
        out_specs=out_specs,
        scratch_shapes=[pltpu.VMEM((2, nl, d // 2), U32), pltpu.SemaphoreType.DMA((2,))],
    )
    return pl.pallas_call(
        functools.partial(_combine_kernel, lay=lay, tm=tm, nl=nl, final=final),
        out_shape=out_shape,
        grid_spec=grid_spec,
        compiler_params=_cparams(("arbitrary",)),
        name="moe_combine_final" if final else "moe_combine",
    )(gdst, ngt, x, route, mods, *mod_next.args, ys)


def _moe_layer(lay, x, u2, mods, l, mod_next, final, wr_t, upper, ltri, w13, w2, ng, ne):
    t, d = x.shape
    tm = upper.shape[0]
    nt = t // tm
    bmg = MOE_BM // SUBLANES
    nl = -(-(MOE_TOP_K * tm + ne * (SUBLANES - 1)) // 256) * 256
    nlg = nl // SUBLANES
    route_t, counts = _router_call(u2, wr_t, upper, ltri, ng, ne)
    route = jnp.swapaxes(route_t, 1, 2).reshape(t, SUBLANES)

    c8 = counts[:, :, 0].astype(jnp.int32)
    lend = jnp.cumsum(c8, axis=1)
    lstart = lend - c8
    ngt = lend[:, -1].astype(jnp.int32)
    tot = jnp.sum(c8, axis=0)
    padded = (tot + bmg - 1) // bmg * bmg
    gend = jnp.cumsum(padded)
    gbase = (gend - padded)[None, :] + jnp.cumsum(c8, axis=0) - c8
    nb = -(-(MOE_TOP_K * t + nt * ne * (SUBLANES - 1)) // MOE_BM) + ne
    g = jnp.arange(nlg, dtype=jnp.int32)[None, :, None]
    owner = (g >= lstart[:, None, :]) & (g < lend[:, None, :])
    gdst = g[:, :, 0] + jnp.sum(jnp.where(owner, (gbase - lstart)[:, None, :], 0), axis=-1)
    gdst = gdst.reshape(nt * nlg).astype(jnp.int32)
    row0 = ((gend - padded) * SUBLANES).astype(jnp.int32)
    nblk = (padded // bmg).astype(jnp.int32)
    tail = jnp.stack([gend[-1] * SUBLANES, nb - gend[-1] // bmg]).astype(jnp.int32)

    pad0 = (gend - padded + tot).astype(jnp.int32)
    npad = (padded - tot).astype(jnp.int32)
    xs = _dispatch_call(u2, route_t, gdst, ngt, pad0, npad, tail, nb * MOE_BM, tm, nl)
    ys = _expert_call(xs, row0, nblk, tail, w13, w2, l)
    return _combine_call(lay, x, route, mods, l, mod_next, gdst, ngt, ys, tm, nl, final)


def _lower_tri(n, strict):
    r = np.arange(n)
    m = (r[None, :] < r[:, None]) if strict else (r[None, :] <= r[:, None])
    return jnp.asarray(m.astype(np.float32), dtype=BF16)


def kernel(x_prompt, x_sample, state_C, state_n, state_m, c, c_ctx, ada_w, ada_b, norm1_w, norm2_w, m_w_in, m_b_gate, m_head_norm_w, m_w_out, f_w_out, cv_w_pw1, cv_b_pw1, cv_w_dw, cv_b_dw, cv_ln_w, cv_ln_b, cv_w_pw2, cv_b_pw2, r_w_group, r_w_expert, e_w13, e_w2, final_norm_w):
    nbp, sp, d = x_prompt.shape
    nbs, ss, _ = x_sample.shape
    assert ss % GRID_W == 0
    lay = _Layout(nbp, sp, nbs, ss, d)
    depth = ada_w.shape[0]
    nh, dh = state_C.shape[3], state_C.shape[4]
    di = nh * dh
    ng = r_w_group.shape[2]
    ne = r_w_expert.shape[2]
    assert ng == MOE_GROUPS and ng + ne <= LANES and 4 * nh <= LANES and MOE_TOP_K == 2

    cv = jnp.zeros((lay.ngp, d), F32).at[0].set(c_ctx).at[1:1 + nbs].set(c)
    mods = _ada_call(cv, ada_w, ada_b)

    tri_l = _lower_tri(MLSTM_L, strict=False)
    upper = _lower_tri(lay.row_tile(512), strict=True).T
    ltri = _lower_tri(ne, strict=True)
    row = lambda a: a.reshape(1, -1)
    w_in_t = jnp.swapaxes(m_w_in, 1, 2)
    tn = 1024
    assert di % tn == 0
    nkb = di // tn

    x, u1 = _prep_call(lay, x_prompt.reshape(lay.tp, d), x_sample.reshape(lay.ts, d),
                       _Mod(lay, mods, norm1_w[0], 0, 0))
    y = None
    qvos, kts, gpts = [], [], []
    for l in range(depth):
        j, kind = l // N_MIXERS, l % N_MIXERS
        mod2 = _Mod(lay, mods, norm2_w[l], l, 3)
        if kind == 0:
            wg = jnp.zeros((LANES, d), F32).at[:4 * nh].set(w_in_t[j, 4 * di:])
            bg = jnp.zeros((1, LANES), F32).at[0, :4 * nh].set(m_b_gate[j])
            gp = _gates_call(lay, x, _Mod(lay, mods, norm1_w[l], l, 0), wg, bg, tri_l, nh)
            gpt = gp[:, :4 * nh].T
            qvo_blocks = list(range(nkb)) + list(range(2 * nkb, 4 * nkb))
            qvo = _proj_call(lay, u1, w_in_t, j, qvo_blocks, tn, transposed=False)
            kt = _proj_call(lay, u1, w_in_t, j, list(range(nkb, 2 * nkb)), tn, transposed=True)
            hw = row(m_head_norm_w[j])
            hg = jnp.zeros((lay.t, di), BF16)
            hg = _mlstm_call(lay, hg, qvo, kt, gp, gpt, hw, nh, dh, prompt=True)
            n0 = state_n[:, j].reshape(nbs, 2 * nh, dh)
            m0 = jnp.broadcast_to(state_m[:, j].reshape(nbs, 2 * nh, 1), (nbs, 2 * nh, LANES))
            hg = _mlstm_call(lay, hg, qvo, kt, gp, gpt, hw, nh, dh, prompt=False, state=(state_C, j, n0, m0))
            x, u2 = _mm_res_call(lay, hg, m_w_out[j].astype(BF16), x, mods, l, mod2)
            qvos.append(qvo)
            kts.append(kt)
            gpts.append(gpt)
        elif kind == 1:
            wo = f_w_out[j].astype(BF16)
            u2 = jnp.zeros((lay.t, d), BF16)
            x, u2 = _fnet_call(lay, x, u1, u2, mods, l, wo, mod2, prompt=True)
            x, u2 = _fnet_call(lay, x, u1, u2, mods, l, wo, mod2, prompt=False)
        else:
            glu = _glu_call(lay, u1, cv_w_pw1[j].astype(BF16), row(cv_b_pw1[j]))
            x, u2 = _conv_call(lay, glu, cv_w_dw[j], cv_b_dw[j], cv_ln_w[j], cv_ln_b[j], cv_w_pw2[j].astype(BF16),
                               cv_b_pw2[j], x, mods, l, mod2)
        wr = jnp.zeros((LANES, d), F32).at[:ng].set(r_w_group[l].T).at[SUBLANES:SUBLANES + ne].set(r_w_expert[l].T)
        final = l + 1 == depth
        mod_next = _Mod(lay, mods, final_norm_w, l, 0) if final else _Mod(lay, mods, norm1_w[l + 1], l + 1, 0)
        outs = _moe_layer(lay, x, u2, mods, l, mod_next, final, wr.astype(BF16), upper, ltri, e_w13, e_w2, ng, ne)
        if final:
            y = outs
        else:
            x, u1 = outs

    y_prompt = y[0].reshape(nbp, sp, d)
    y_sample = y[1].reshape(nbs, ss, d)
    new_c, new_n, new_m = _state_call(lay, qvos, kts, gpts, nh, dh)
    return (y_prompt, y_sample, new_c, new_n, new_m)
```

```python
import functools
import math

import numpy as np
import jax
import jax.numpy as jnp
from jax import lax
from jax.experimental import pallas as pl
from jax.experimental.pallas import tpu as pltpu

F32 = jnp.float32
BF16 = jnp.bfloat16
U32 = jnp.uint32
EPS = 1e-6
GRID_W = 64
N_MIXERS = 3
FNET_GROUPS = 4
CONV_WIDTH = 31
MOE_GROUPS = 4
MOE_TOP_K = 2

LANES = 128
SUBLANES = 8
MLSTM_L = 256
MOE_BM = 256
CONV_HALO = 16
VMEM_LIMIT = 56 * 1024 * 1024


def _cparams(sem, vmem=VMEM_LIMIT):
    return pltpu.CompilerParams(dimension_semantics=sem, vmem_limit_bytes=vmem)


def _dot(a, b):
    return jnp.dot(a, b, preferred_element_type=F32)


def _dot_nt(a, b):
    return lax.dot_general(a, b, (((1,), (1,)), ((), ())), preferred_element_type=F32)


def _rms(x, w):
    return x * lax.rsqrt(jnp.mean(x * x, axis=-1, keepdims=True) + EPS) * w


def _modulate(x, w, shift, scale):
    return _rms(x, w) * (1.0 + scale) + shift


def _sigmoid(x):
    return 1.0 / (1.0 + jnp.exp(-x))


def _log_sigmoid(x):
    return jnp.minimum(x, 0.0) - jnp.log(1.0 + jnp.exp(-jnp.abs(x)))


def _split2(x):
    hi = x.astype(BF16)
    return hi, (x - hi.astype(F32)).astype(BF16)


def _split3(x):
    hi = x.astype(BF16)
    r1 = x - hi.astype(F32)
    mid = r1.astype(BF16)
    return hi, mid, (r1 - mid.astype(F32)).astype(BF16)


class _Layout:
    def __init__(self, nbp, sp, nbs, ss, d):
        self.nbp, self.sp, self.nbs, self.ss, self.d = nbp, sp, nbs, ss, d
        self.tp, self.ts = nbp * sp, nbs * ss
        self.t = self.tp + self.ts
        assert self.tp % ss == 0, "latent sequences must start on a block boundary of their own length"
        self.ngp = -(-(1 + nbs) // SUBLANES) * SUBLANES

    def group(self, row0):
        return jnp.where(row0 < self.tp, 0, 1 + (row0 - self.tp) // self.ss)

    def row_tile(self, want):
        tm = math.gcd(math.gcd(self.tp, self.ss), want)
        assert tm % SUBLANES == 0
        return tm

    def mod_spec(self, l, chunk):
        return pl.BlockSpec((None, self.ngp, self.d), lambda *_: (l, 0, chunk))

    def row_spec(self):
        return pl.BlockSpec((1, self.d), lambda *_: (0, 0))


class _Mod:
    def __init__(self, lay, mods, nw, l, c_shift):
        self.args = (nw.reshape(1, -1), mods, mods)
        self.specs = [lay.row_spec(), lay.mod_spec(l, c_shift), lay.mod_spec(l, c_shift + 1)]


def _mod_value(x, nw_ref, sh_ref, sc_ref, grp):
    return _modulate(x, nw_ref[...], sh_ref[pl.ds(grp, 1), :], sc_ref[pl.ds(grp, 1), :])


def _ada_kernel(cv_ref, w_ref, b_ref, o_ref):
    s = cv_ref[...]
    s = s * _sigmoid(s)
    o_ref[...] = _dot(s.astype(BF16), w_ref[...].astype(BF16)) + b_ref[...]


def _ada_call(cv, ada_w, ada_b):
    depth, d, n = ada_w.shape
    ngp = cv.shape[0]
    tn = min(n, 2048)
    return pl.pallas_call(
        _ada_kernel,
        out_shape=jax.ShapeDtypeStruct((depth, ngp, n), F32),
        grid=(depth, n // tn),
        in_specs=[
            pl.BlockSpec((ngp, d), lambda l, j: (0, 0)),
            pl.BlockSpec((None, d, tn), lambda l, j: (l, 0, j)),
            pl.BlockSpec((None, 1, tn), lambda l, j: (l, 0, j)),
        ],
        out_specs=pl.BlockSpec((None, ngp, tn), lambda l, j: (l, 0, j)),
        compiler_params=_cparams(("parallel", "parallel")),
        name="ada_mods",
    )(cv, ada_w, ada_b.reshape(depth, 1, n))


def _prep_kernel(xp_ref, xs_ref, nw_ref, sh_ref, sc_ref, x_ref, u_ref, *, lay, tm):
    i = pl.program_id(0)
    grp = lay.group(i * tm)
    for src, cond in ((xp_ref, i * tm < lay.tp), (xs_ref, i * tm >= lay.tp)):
        @pl.when(cond)
        def _(src=src):
            rc = min(tm, 256)
            for r in range(0, tm, rc):
                x = src[r:r + rc, :]
                x_ref[r:r + rc, :] = x
                u_ref[r:r + rc, :] = _mod_value(x, nw_ref, sh_ref, sc_ref, grp).astype(BF16)


def _prep_call(lay, xp, xs, mod):
    d = lay.d
    tm = lay.row_tile(1024)
    ntp = lay.tp // tm
    rows = pl.BlockSpec((tm, d), lambda i: (i, 0))
    return pl.pallas_call(
        functools.partial(_prep_kernel, lay=lay, tm=tm),
        out_shape=(jax.ShapeDtypeStruct((lay.t, d), F32), jax.ShapeDtypeStruct((lay.t, d), BF16)),
        grid=(lay.t // tm,),
        in_specs=[
            pl.BlockSpec((tm, d), lambda i: (jnp.minimum(i, ntp - 1), 0)),
            pl.BlockSpec((tm, d), lambda i: (jnp.maximum(i - ntp, 0), 0)),
        ] + mod.specs,
        out_specs=(rows, rows),
        compiler_params=_cparams(("arbitrary",)),
        name="prep_modulate",
    )(xp, xs, *mod.args)


def _gates_kernel(x_ref, nw_ref, sh_ref, sc_ref, wh_ref, wl_ref, b_ref, tri_ref, o_ref, *, lay, nh, tm):
    l = tri_ref.shape[0]
    grp = lay.group(pl.program_id(0) * tm)
    wh = wh_ref[...]
    wl = wl_ref[...]
    tri = tri_ref[...]
    lane = lax.broadcasted_iota(jnp.int32, (l, LANES), 1)
    is_f = ((lane >= nh) & (lane < 2 * nh)) | ((lane >= 3 * nh) & (lane < 4 * nh))
    is_a = (lane < nh) | ((lane >= 2 * nh) & (lane < 3 * nh))
    for r in range(0, tm, l):
        u = _mod_value(x_ref[r:r + l, :], nw_ref, sh_ref, sc_ref, grp)
        u_hi, u_lo = _split2(u)
        g = _dot_nt(u_hi, wh) + _dot_nt(u_lo, wh) + _dot_nt(u_hi, wl) + b_ref[...]
        lf = jnp.where(is_f, _log_sigmoid(g), 0.0)
        hi, mid, lo = _split3(lf)
        prefix = _dot(tri, hi) + _dot(tri, mid) + _dot(tri, lo)
        suffix = jnp.sum(lf, axis=0, keepdims=True) - prefix + lf
        b = jnp.where(lane < 2 * nh, prefix, suffix)
        a = g - pltpu.roll(b, LANES - nh, 1)
        o_ref[r:r + l, :] = jnp.where(is_a, a, b)


def _gates_call(lay, x, mod, wg, bias, tri, nh):
    t, d = x.shape
    l = tri.shape[0]
    tm = lay.row_tile(1024)
    assert tm % l == 0
    wh, wl = _split2(wg)
    return pl.pallas_call(
        functools.partial(_gates_kernel, lay=lay, nh=nh, tm=tm),
        out_shape=jax.ShapeDtypeStruct((t, LANES), F32),
        grid=(t // tm,),
        in_specs=[pl.BlockSpec((tm, d), lambda i: (i, 0))] + mod.specs + [
            pl.BlockSpec((LANES, d), lambda i: (0, 0)),
            pl.BlockSpec((LANES, d), lambda i: (0, 0)),
            pl.BlockSpec((1, LANES), lambda i: (0, 0)),
            pl.BlockSpec((l, l), lambda i: (0, 0)),
        ],
        out_specs=pl.BlockSpec((tm, LANES), lambda i: (i, 0)),
        compiler_params=_cparams(("parallel",)),
        name="mlstm_gates",
    )(x, *mod.args, wh, wl, bias, tri)


def _proj_kernel(u_ref, w_ref, o_ref, w_bf, *, transposed):
    @pl.when(pl.program_id(1) == 0)
    def _():
        w_bf[...] = w_ref[...].astype(BF16)

    if transposed:
        o_ref[...] = _dot_nt(w_bf[...], u_ref[...]).astype(BF16)
    else:
        o_ref[...] = _dot_nt(u_ref[...], w_bf[...]).astype(BF16)


def _proj_call(lay, u, w_in_t, jl, blocks, tn, transposed):
    t, d = u.shape
    tm = 2048 if t % 2048 == 0 else lay.row_tile(1024)
    nb = len(blocks)
    first, gap_at, gap = blocks[0], None, 0
    for idx in range(1, nb):
        if blocks[idx] != blocks[idx - 1] + 1:
            assert gap_at is None
            gap_at, gap = idx, blocks[idx] - blocks[idx - 1] - 1
    wblk = (lambda j: first + j) if gap_at is None else (lambda j: first + j + jnp.where(j >= gap_at, gap, 0))
    if transposed:
        out_shape = jax.ShapeDtypeStruct((nb * tn, t), BF16)
        out_spec = pl.BlockSpec((tn, tm), lambda j, i: (j, i))
    else:
        out_shape = jax.ShapeDtypeStruct((t, nb * tn), BF16)
        out_spec = pl.BlockSpec((tm, tn), lambda j, i: (i, j))
    return pl.pallas_call(
        functools.partial(_proj_kernel, transposed=transposed),
        out_shape=out_shape,
        grid=(nb, t // tm),
        in_specs=[
            pl.BlockSpec((tm, d), lambda j, i: (i, 0)),
            pl.BlockSpec((None, tn, d), lambda j, i: (jl, wblk(j), 0)),
        ],
        out_specs=out_spec,
        scratch_shapes=[pltpu.VMEM((tn, d), BF16)],
        compiler_params=_cparams(("parallel", "arbitrary")),
        name="mlstm_proj_t" if transposed else "mlstm_proj",
    )(u, w_in_t)


def _col(tile, c):
    lane = lax.broadcasted_iota(jnp.int32, tile.shape, 1)
    return jnp.sum(jnp.where(lane == c, tile, 0.0), axis=-1, keepdims=True)


def _dir_masks(l):
    r = lax.broadcasted_iota(jnp.int32, (l, l), 0)
    c = lax.broadcasted_iota(jnp.int32, (l, l), 1)
    return c <= r, c >= r


def _head_epilogue(h, hw, o):
    hn = h * lax.rsqrt(jnp.mean(h * h, axis=-1, keepdims=True) + EPS) * hw
    return (hn * _sigmoid(o.astype(F32))).astype(BF16)


def _row_times_kt(w_row, kt):
    hi, lo = _split2(w_row)
    sub = lax.broadcasted_iota(jnp.int32, (SUBLANES, w_row.shape[1]), 0)
    stacked = jnp.where(sub == 0, hi.astype(F32), jnp.where(sub == 1, lo.astype(F32), 0.0)).astype(BF16)
    res = _dot_nt(stacked, kt)
    return res[0:1, :] + res[1:2, :]


def _mlstm_single_kernel(q_ref, kt_ref, v_ref, o_ref, gp_ref, gpt_ref, hw_ref, out_ref, *, nh, dh, scale):
    l = q_ref.shape[0]
    gp = gp_ref[...]
    masks = _dir_masks(l)
    for h in range(nh):
        cols = slice(h * dh, (h + 1) * dh)
        qk = _dot(q_ref[:, cols], kt_ref[cols, :])
        p = None
        for d in range(2):
            a_r = gpt_ref[2 * nh * d + h:2 * nh * d + h + 1, :]
            b_c = gp[:, 2 * nh * d + nh + h:2 * nh * d + nh + h + 1]
            g = jnp.where(masks[d], a_r, -jnp.inf)
            m = jnp.maximum(jnp.max(g, axis=-1, keepdims=True), 0.0)
            s = qk * jnp.exp(g - m) * scale
            den = jnp.sum(s, axis=-1, keepdims=True)
            inv = 1.0 / jnp.maximum(jnp.abs(den), jnp.exp(-(b_c + m)))
            p = s * inv if p is None else p + s * inv
        hh = _dot(p.astype(BF16), v_ref[:, cols])
        out_ref[:, cols] = _head_epilogue(hh, hw_ref[:, cols], o_ref[:, cols])


def _mlstm_multi_kernel(q_ref, kt_ref, v_ref, o_ref, gp_ref, gpt_ref, hw_ref, c0_ref, n0_ref, m0_ref,
                        out_ref, cst, cbf, *, nh, nc, l, scale):
    h = pl.program_id(1)
    masks = _dir_masks(l)
    m_in = [[None] * nc for _ in range(2)]
    n_in = [[None] * nc for _ in range(2)]
    for d in range(2):
        cst[...] = c0_ref[d]
        n = n0_ref[pl.ds(d * nh + h, 1), :]
        m = m0_ref[pl.ds(d * nh + h, 1), 0:1]
        order = list(range(nc)) if d == 0 else list(range(nc - 1, -1, -1))
        for step, c in enumerate(order):
            m_in[d][c], n_in[d][c] = m, n
            cbf[d, c] = cst[...].astype(BF16)
            if step + 1 < nc:
                r0 = c * l
                a_r = gpt_ref[pl.ds(2 * nh * d + h, 1), r0:r0 + l]
                b_r = gpt_ref[pl.ds(2 * nh * d + nh + h, 1), r0:r0 + l]
                m_last = jnp.maximum(jnp.max(a_r, axis=-1, keepdims=True), m)
                b_end = b_r[:, l - 1:l] if d == 0 else b_r[:, 0:1]
                decay = jnp.exp(m - m_last)
                w_end = jnp.exp(a_r - m_last)
                kt = kt_ref[:, r0:r0 + l]
                cst[...] = decay * cst[...] + _dot((kt.astype(F32) * w_end).astype(BF16), v_ref[r0:r0 + l, :])
                n = decay * n + _row_times_kt(w_end, kt)
                m = b_end + m_last
    hw = hw_ref[...]
    for c in range(nc):
        r0 = c * l
        q = q_ref[r0:r0 + l, :]
        v = v_ref[r0:r0 + l, :]
        qk = _dot(q, kt_ref[:, r0:r0 + l])
        qf = q.astype(F32)
        gp = gp_ref[r0:r0 + l, :]
        p = inter = None
        for d in range(2):
            m, n = m_in[d][c], n_in[d][c]
            a_r = gpt_ref[pl.ds(2 * nh * d + h, 1), r0:r0 + l]
            b_c = _col(gp, 2 * nh * d + nh + h)
            g = jnp.where(masks[d], a_r, -jnp.inf)
            mt = jnp.maximum(jnp.max(g, axis=-1, keepdims=True), m)
            s = qk * jnp.exp(g - mt) * scale
            w_prev = jnp.exp(m - mt) * scale
            den = jnp.sum(s, axis=-1, keepdims=True) + w_prev * jnp.sum(qf * n, axis=-1, keepdims=True)
            inv = 1.0 / jnp.maximum(jnp.abs(den), jnp.exp(-(b_c + mt)))
            term = (w_prev * inv) * _dot(q, cbf[d, c])
            p = s * inv if p is None else p + s * inv
            inter = term if inter is None else inter + term
        hh = _dot(p.astype(BF16), v) + inter
        out_ref[r0:r0 + l, :] = _head_epilogue(hh, hw, o_ref[r0:r0 + l, :])


def _mlstm_call(lay, qvo, kt, gp, gpt, hw, nh, dh, prompt, state=None):
    nb, s = (lay.nbp, lay.sp) if prompt else (lay.nbs, lay.ss)
    rb0 = 0 if prompt else lay.tp // s
    scale = dh ** -0.5
    di = nh * dh
    common_in = [
        pl.BlockSpec((s, dh), lambda b, h: (rb0 + b, h)),
        pl.BlockSpec((dh, s), lambda b, h: (h, rb0 + b)),
        pl.BlockSpec((s, dh), lambda b, h: (rb0 + b, nh + h)),
        pl.BlockSpec((s, dh), lambda b, h: (rb0 + b, 2 * nh + h)),
        pl.BlockSpec((s, LANES), lambda b, h: (rb0 + b, 0)),
        pl.BlockSpec((4 * nh, s), lambda b, h: (0, rb0 + b)),
        pl.BlockSpec((1, dh), lambda b, h: (0, h)),
    ]
    if prompt:
        assert s == MLSTM_L
        return pl.pallas_call(
            functools.partial(_mlstm_single_kernel, nh=nh, dh=dh, scale=scale),
            out_shape=jax.ShapeDtypeStruct((nb * s, di), BF16),
            grid=(nb,),
            in_specs=[
                pl.BlockSpec((s, di), lambda b: (rb0 + b, 0)),
                pl.BlockSpec((di, s), lambda b: (0, rb0 + b)),
                pl.BlockSpec((s, di), lambda b: (rb0 + b, 1)),
                pl.BlockSpec((s, di), lambda b: (rb0 + b, 2)),
                pl.BlockSpec((s, LANES), lambda b: (rb0 + b, 0)),
                pl.BlockSpec((4 * nh, s), lambda b: (0, rb0 + b)),
                pl.BlockSpec((1, di), lambda b: (0, 0)),
            ],
            out_specs=pl.BlockSpec((s, di), lambda b: (b, 0)),
            compiler_params=_cparams(("parallel",)),
            name="mlstm_prompt",
        )(qvo, kt, qvo, qvo, gp, gpt, hw)
    state_c, jl, n0, m0 = state
    nc = s // MLSTM_L
    return pl.pallas_call(
        functools.partial(_mlstm_multi_kernel, nh=nh, nc=nc, l=MLSTM_L, scale=scale),
        out_shape=jax.ShapeDtypeStruct((nb * s, di), BF16),
        grid=(nb, nh),
        in_specs=common_in + [
            pl.BlockSpec((None, None, 2, None, dh, dh), lambda b, h: (b, jl, 0, h, 0, 0)),
            pl.BlockSpec((None, 2 * nh, dh), lambda b, h: (b, 0, 0)),
            pl.BlockSpec((None, 2 * nh, LANES), lambda b, h: (b, 0, 0)),
        ],
        out_specs=pl.BlockSpec((s, dh), lambda b, h: (b, h)),
        scratch_shapes=[pltpu.VMEM((dh, dh), F32), pltpu.VMEM((2, nc, dh, dh), BF16)],
        compiler_params=_cparams(("parallel", "parallel")),
        name="mlstm_latent",
    )(qvo, kt, qvo, qvo, gp, gpt, hw, state_c, n0, m0)


def _state_kernel(*refs, nl, nh, dh):
    ins, (c_ref, n_ref, m_ref) = refs[:3 * nl], refs[3 * nl:]
    lyr = pl.program_id(0)
    for jl in range(nl):
        kt_ref, v_ref, gpt_ref = ins[3 * jl:3 * jl + 3]

        @pl.when(lyr == jl)
        def _(kt_ref=kt_ref, v_ref=v_ref, gpt_ref=gpt_ref):
            l = v_ref.shape[0]
            sub = lax.broadcasted_iota(jnp.int32, m_ref.shape, 0)
            lane = lax.broadcasted_iota(jnp.int32, m_ref.shape, 1)
            m_all = jnp.zeros(m_ref.shape, F32)
            for h in range(nh):
                kt = kt_ref[h * dh:(h + 1) * dh, :]
                ktf = kt.astype(F32)
                v = v_ref[:, h * dh:(h + 1) * dh]
                for d in range(2):
                    a_r = gpt_ref[2 * nh * d + h:2 * nh * d + h + 1, :]
                    b_r = gpt_ref[2 * nh * d + nh + h:2 * nh * d + nh + h + 1, :]
                    m_last = jnp.maximum(jnp.max(a_r, axis=-1, keepdims=True), 0.0)
                    b_end = b_r[:, l - 1:l] if d == 0 else b_r[:, 0:1]
                    w_end = jnp.exp(a_r - m_last)
                    c_ref[d, h] = _dot((ktf * w_end).astype(BF16), v)
                    n_ref[d, h:h + 1, :] = _row_times_kt(w_end, kt)
                    m_all = jnp.where((sub == d) & (lane == h), b_end + m_last, m_all)
            m_ref[...] = m_all


def _state_call(lay, qvos, kts, gpts, nh, dh):
    nl = len(qvos)
    nbp, s = lay.nbp, lay.sp
    di = nh * dh
    assert s == MLSTM_L

    def pick(jl):
        return lambda lyr, b: jnp.where(lyr == jl, b, jnp.where(lyr < jl, 0, nbp - 1))

    in_specs, args = [], []
    for jl in range(nl):
        pb = pick(jl)
        in_specs.append(pl.BlockSpec((di, s), lambda lyr, b, pb=pb: (0, pb(lyr, b))))
        in_specs.append(pl.BlockSpec((s, di), lambda lyr, b, pb=pb: (pb(lyr, b), 1)))
        in_specs.append(pl.BlockSpec((4 * nh, s), lambda lyr, b, pb=pb: (0, pb(lyr, b))))
        args += [kts[jl], qvos[jl], gpts[jl]]
    return pl.pallas_call(
        functools.partial(_state_kernel, nl=nl, nh=nh, dh=dh),
        out_shape=(
            jax.ShapeDtypeStruct((nbp, nl, 2, nh, dh, dh), F32),
            jax.ShapeDtypeStruct((nbp, nl, 2, nh, dh), F32),
            jax.ShapeDtypeStruct((nbp, nl, 2, nh), F32),
        ),
        grid=(nl, nbp),
        in_specs=in_specs,
        out_specs=(
            pl.BlockSpec((None, None, 2, nh, dh, dh), lambda lyr, b: (b, lyr, 0, 0, 0, 0)),
            pl.BlockSpec((None, None, 2, nh, dh), lambda lyr, b: (b, lyr, 0, 0, 0)),
            pl.BlockSpec((None, None, 2, nh), lambda lyr, b: (b, lyr, 0, 0)),
        ),
        compiler_params=_cparams(("arbitrary", "arbitrary")),
        name="mlstm_prompt_state",
    )(*args)


def _mm_res_kernel(ap_ref, as_ref, w_ref, x_ref, g_ref, nw_ref, sh_ref, sc_ref, o_ref, u_ref, *, lay, tm):
    i = pl.program_id(0)
    grp = lay.group(i * tm)
    gate = g_ref[pl.ds(grp, 1), :]
    w = w_ref[...]
    rc = min(tm, 256)
    for a_ref, cond in ((ap_ref, i * tm < lay.tp), (as_ref, i * tm >= lay.tp)):
        @pl.when(cond)
        def _(a_ref=a_ref):
            for r in range(0, tm, rc):
                xn = x_ref[r:r + rc, :] + gate * _dot(a_ref[r:r + rc, :], w)
                o_ref[r:r + rc, :] = xn
                u_ref[r:r + rc, :] = _mod_value(xn, nw_ref, sh_ref, sc_ref, grp).astype(BF16)


def _mm_res_call(lay, a_p, a_s, w_bf, x, mods, l, mod2):
    kdim = a_p.shape[1]
    t, d = x.shape
    tm = lay.row_tile(512)
    ntp = lay.tp // tm
    return pl.pallas_call(
        functools.partial(_mm_res_kernel, lay=lay, tm=tm),
        out_shape=(jax.ShapeDtypeStruct(x.shape, F32), jax.ShapeDtypeStruct(x.shape, BF16)),
        grid=(t // tm,),
        in_specs=[
            pl.BlockSpec((tm, kdim), lambda i: (jnp.minimum(i, ntp - 1), 0)),
            pl.BlockSpec((tm, kdim), lambda i: (jnp.maximum(i - ntp, 0), 0)),
            pl.BlockSpec((kdim, d), lambda i: (0, 0)),
            pl.BlockSpec((tm, d), lambda i: (i, 0)),
            lay.mod_spec(l, 2),
        ] + mod2.specs,
        out_specs=(pl.BlockSpec((tm, d), lambda i: (i, 0)), pl.BlockSpec((tm, d), lambda i: (i, 0))),
        compiler_params=_cparams(("arbitrary",)),
        name="mm_residual",
    )(a_p, a_s, w_bf, x, mods, *mod2.args)


def _fnet_kernel(x_ref, u_ref, u2_any, gt_ref, wc_ref, ds_ref, wo_ref, nw_ref, sh_ref, sc_ref, o_ref, u2_ref,
                 ab_scr, *, lay, row_base, groups, norm):
    s, d = x_ref.shape
    cg = d // groups
    grp = lay.group(row_base + pl.program_id(0) * s)
    wc = wc_ref[...]
    rc = min(s, 256)
    for g in range(groups):
        for r in range(0, s, rc):
            ab = _dot(u_ref[r:r + rc, g * cg:(g + 1) * cg], wc)
            ab_scr[r:r + rc, g * cg:(g + 1) * cg] = ab[:, :cg].astype(BF16)
            ab_scr[s + r:s + r + rc, g * cg:(g + 1) * cg] = ab[:, cg:].astype(BF16)
    gate = gt_ref[pl.ds(grp, 1), :]
    wo = wo_ref[...]
    for r in range(0, s, rc):
        y = _dot(ds_ref[r:r + rc, :], ab_scr[...]) * norm
        xn = x_ref[r:r + rc, :] + gate * _dot(y.astype(BF16), wo)
        o_ref[r:r + rc, :] = xn
        u2_ref[r:r + rc, :] = _mod_value(xn, nw_ref, sh_ref, sc_ref, grp).astype(BF16)


def _dft_mats(s, cg):
    kc = np.arange(cg)
    ang_c = 2.0 * np.pi * np.outer(kc, kc) / cg
    wc = np.concatenate([np.cos(ang_c), np.sin(ang_c)], axis=1)
    ks = np.arange(s)
    ang_s = 2.0 * np.pi * np.outer(ks, ks) / s
    ds = np.concatenate([np.cos(ang_s), -np.sin(ang_s)], axis=1)
    return jnp.asarray(wc, dtype=BF16), jnp.asarray(ds, dtype=BF16)


def _fnet_call(lay, x, u1, u2, mods, l, wo_bf, mod2, prompt):
    nb, s = (lay.nbp, lay.sp) if prompt else (lay.nbs, lay.ss)
    rb0 = 0 if prompt else lay.tp // s
    d = lay.d
    cg = d // FNET_GROUPS
    wc, ds = _dft_mats(s, cg)
    kern = functools.partial(_fnet_kernel, lay=lay, row_base=rb0 * s, groups=FNET_GROUPS,
                             norm=1.0 / math.sqrt(s * cg))
    blk = pl.BlockSpec((s, d), lambda b: (rb0 + b, 0))
    return pl.pallas_call(
        kern,
        out_shape=(jax.ShapeDtypeStruct(x.shape, F32), jax.ShapeDtypeStruct(u2.shape, BF16)),
        grid=(nb,),
        in_specs=[
            blk,
            blk,
            pl.BlockSpec(memory_space=pl.ANY),
            lay.mod_spec(l, 2),
            pl.BlockSpec((cg, 2 * cg), lambda b: (0, 0)),
            pl.BlockSpec((s, 2 * s), lambda b: (0, 0)),
            pl.BlockSpec((d, d), lambda b: (0, 0)),
        ] + mod2.specs,
        out_specs=(blk, blk),
        scratch_shapes=[pltpu.VMEM((2 * s, d), BF16)],
        input_output_aliases={0: 0, 2: 1},
        compiler_params=_cparams(("parallel",)),
        name="fnet_prompt" if prompt else "fnet_latent",
    )(x, u1, u2, mods, wc, ds, wo_bf, *mod2.args)


def _glu_kernel(u_ref, wa_ref, wg_ref, ba_ref, bg_ref, o_ref):
    u = u_ref[...]
    a = _dot(u, wa_ref[...]) + ba_ref[...]
    g = _dot(u, wg_ref[...]) + bg_ref[...]
    o_ref[...] = a * _sigmoid(g)


def _glu_call(lay, u, w_bf, bias):
    t, d = u.shape
    cd = w_bf.shape[1] // 2
    tm = lay.row_tile(1024)
    tn = 512
    nj = cd // tn
    return pl.pallas_call(
        _glu_kernel,
        out_shape=jax.ShapeDtypeStruct((t, cd), F32),
        grid=(t // tm, nj),
        in_specs=[
            pl.BlockSpec((tm, d), lambda i, j: (i, 0)),
            pl.BlockSpec((d, tn), lambda i, j: (0, j)),
            pl.BlockSpec((d, tn), lambda i, j: (0, nj + j)),
            pl.BlockSpec((1, tn), lambda i, j: (0, j)),
            pl.BlockSpec((1, tn), lambda i, j: (0, nj + j)),
        ],
        out_specs=pl.BlockSpec((tm, tn), lambda i, j: (i, j)),
        compiler_params=_cparams(("parallel", "parallel")),
        name="conv_glu",
    )(u, w_bf, w_bf, bias, bias)


def _conv_kernel(c_ref, p_ref, n_ref, wd_ref, bd_ref, lw_ref, lb_ref, w2_ref, b2_ref, x_ref, gt_ref,
                 nw_ref, sh_ref, sc_ref, o_ref, u2_ref, pad, conv, act, *, lay, rb, width):
    i = pl.program_id(0)
    row0 = i * rb
    grp = lay.group(row0)
    seq = jnp.where(row0 < lay.tp, lay.sp, lay.ss)
    pos = jnp.where(row0 < lay.tp, row0 % lay.sp, (row0 - lay.tp) % lay.ss)
    has_prev = (pos != 0).astype(F32)
    has_next = (pos + rb != seq).astype(F32)
    hl = CONV_HALO
    half = width // 2
    cd = c_ref.shape[1]
    span = pad.shape[1]
    pad[0, 0:hl, :] = p_ref[...] * has_prev
    pad[0, hl:hl + rb, :] = c_ref[...]
    pad[0, hl + rb:hl + rb + hl, :] = n_ref[...] * has_next
    for s in range(1, SUBLANES):
        pad[s, 0:span - SUBLANES, :] = pad[0, s:s + span - SUBLANES, :]
    ngrp = 8
    sub = ngrp * SUBLANES
    lanes = 2 * LANES
    assert rb % sub == 0 and cd % lanes == 0

    def conv_block(blk, carry):
        r0 = pl.multiple_of(blk * sub, sub)
        for c0 in range(0, cd, lanes):
            bias = bd_ref[:, c0:c0 + lanes]
            accs = [jnp.zeros((SUBLANES, lanes), F32) + bias for _ in range(ngrp)]
            for k in sorted(range(width), key=lambda k: ((hl - half + k) % SUBLANES, k)):
                q, s = divmod(hl - half + k, SUBLANES)
                wk = wd_ref[k, :, c0:c0 + lanes]
                for gi in range(ngrp):
                    win = pad[s, pl.ds(r0 + (q + gi) * SUBLANES, SUBLANES), c0:c0 + lanes]
                    accs[gi] = accs[gi] + win * wk
            conv[pl.ds(r0, sub), c0:c0 + lanes] = jnp.concatenate(accs, axis=0)
        return carry

    lax.fori_loop(0, rb // sub, conv_block, 0)
    lw = lw_ref[...]
    lb = lb_ref[...]
    lsub = min(rb, 16 * SUBLANES)

    def ln_block(blk, carry):
        r0 = pl.multiple_of(blk * lsub, lsub)
        acc = conv[pl.ds(r0, lsub), :]
        mu = jnp.mean(acc, axis=-1, keepdims=True)
        cen = acc - mu
        var = jnp.mean(cen * cen, axis=-1, keepdims=True)
        y = cen * lax.rsqrt(var + EPS) * lw + lb
        act[pl.ds(r0, lsub), :] = (y * _sigmoid(y)).astype(BF16)
        return carry

    lax.fori_loop(0, rb // lsub, ln_block, 0)
    xn = x_ref[...] + gt_ref[pl.ds(grp, 1), :] * (_dot(act[...], w2_ref[...]) + b2_ref[...])
    o_ref[...] = xn
    u2_ref[...] = _mod_value(xn, nw_ref, sh_ref, sc_ref, grp).astype(BF16)


def _conv_call(lay, glu, wd, bd, lw, lb, w2_bf, b2, x, mods, l, mod2):
    t, cd = glu.shape
    d = x.shape[1]
    rb = lay.row_tile(256)
    hl = CONV_HALO
    assert CONV_WIDTH // 2 <= hl and rb % hl == 0
    nhb = t // hl
    per = rb // hl
    wd_p = jnp.broadcast_to(wd[:, None, :], (CONV_WIDTH, SUBLANES, cd))
    row = lambda a: a.reshape(1, -1)
    rows = pl.BlockSpec((rb, d), lambda i: (i, 0))
    return pl.pallas_call(
        functools.partial(_conv_kernel, lay=lay, rb=rb, width=CONV_WIDTH),
        out_shape=(jax.ShapeDtypeStruct(x.shape, F32), jax.ShapeDtypeStruct(x.shape, BF16)),
        grid=(t // rb,),
        in_specs=[
            pl.BlockSpec((rb, cd), lambda i: (i, 0)),
            pl.BlockSpec((hl, cd), lambda i: (jnp.maximum(i * per - 1, 0), 0)),
            pl.BlockSpec((hl, cd), lambda i: (jnp.minimum((i + 1) * per, nhb - 1), 0)),
            pl.BlockSpec(wd_p.shape, lambda i: (0, 0, 0)),
            pl.BlockSpec((1, cd), lambda i: (0, 0)),
            pl.BlockSpec((1, cd), lambda i: (0, 0)),
            pl.BlockSpec((1, cd), lambda i: (0, 0)),
            pl.BlockSpec((cd, d), lambda i: (0, 0)),
            pl.BlockSpec((1, d), lambda i: (0, 0)),
            rows,
            lay.mod_spec(l, 2),
        ] + mod2.specs,
        out_specs=(rows, rows),
        scratch_shapes=[pltpu.VMEM((SUBLANES, rb + 2 * hl, cd), F32), pltpu.VMEM((rb, cd), F32),
                        pltpu.VMEM((rb, cd), BF16)],
        compiler_params=_cparams(("parallel",)),
        name="conv_dw_ln_pw2",
    )(glu, glu, glu, wd_p, row(bd), row(lw), row(lb), w2_bf, row(b2), x, mods, *mod2.args)


def _router_kernel(u_ref, wr_ref, upper_ref, ltri_ref, o_ref, cnt_ref, *, ng, ne):
    for j in range(o_ref.shape[0]):
        _route_tile(u_ref, wr_ref, upper_ref, ltri_ref, o_ref, cnt_ref, j, ng, ne)


def _route_tile(u_ref, wr_ref, upper_ref, ltri_ref, o_ref, cnt_ref, j, ng, ne):
    tm = upper_ref.shape[0]
    logits = _dot_nt(wr_ref[...], u_ref[j * tm:(j + 1) * tm, :])
    neg = -jnp.inf
    row = lax.broadcasted_iota(jnp.int32, (SUBLANES, tm), 0)

    gl = jnp.where(row < ng, logits[0:SUBLANES, :], neg)
    gmax = jnp.max(gl, axis=0, keepdims=True)
    gidx = jnp.min(jnp.where(gl == gmax, row, SUBLANES), axis=0, keepdims=True)
    g_p = 1.0 / jnp.sum(jnp.where(row < ng, jnp.exp(gl - gmax), 0.0), axis=0, keepdims=True)

    sel = logits[SUBLANES:2 * SUBLANES, :]
    for g in range(1, ng):
        sel = jnp.where(gidx == g, logits[(1 + g) * SUBLANES:(2 + g) * SUBLANES, :], sel)
    v1 = jnp.max(sel, axis=0, keepdims=True)
    i1 = jnp.min(jnp.where(sel == v1, row, SUBLANES), axis=0, keepdims=True)
    sel2 = jnp.where(row == i1, neg, sel)
    v2 = jnp.max(sel2, axis=0, keepdims=True)
    i2 = jnp.min(jnp.where(sel2 == v2, row, SUBLANES), axis=0, keepdims=True)
    e1 = gidx * SUBLANES + i1
    e2 = gidx * SUBLANES + i2
    tt = jnp.exp(v2 - v1)
    p1 = 1.0 / (1.0 + tt)
    gate1 = p1 * g_p
    gate2 = (tt * p1) * g_p

    rowe = lax.broadcasted_iota(jnp.int32, (ne, tm), 0)
    oh1 = rowe == e1
    oh2 = rowe == e2
    oh = jnp.where(oh1 | oh2, 1.0, 0.0)
    groups = jnp.floor((jnp.sum(oh, axis=1, keepdims=True) + (SUBLANES - 1)) * (1.0 / SUBLANES))
    groups_b = jnp.broadcast_to(groups, (ne, LANES))
    start = SUBLANES * _dot(ltri_ref[...], groups_b.astype(BF16))[:, 0:1]
    prefix = _dot(oh.astype(BF16), upper_ref[...]) + start
    pos1 = jnp.sum(jnp.where(oh1, prefix, 0.0), axis=0, keepdims=True)
    pos2 = jnp.sum(jnp.where(oh2, prefix, 0.0), axis=0, keepdims=True)
    cnt_ref[j] = groups_b

    out = jnp.where(row == 0, e1.astype(F32), 0.0)
    out = jnp.where(row == 1, e2.astype(F32), out)
    out = jnp.where(row == 2, gate1, out)
    out = jnp.where(row == 3, gate2, out)
    out = jnp.where(row == 4, pos1, out)
    out = jnp.where(row == 5, pos2, out)
    o_ref[j] = out


def _router_call(u, wr_t, upper, ltri, ng, ne):
    t, d = u.shape
    tm = upper.shape[0]
    nt = t // tm
    per = 2 if nt % 2 == 0 else 1
    assert ne // ng == SUBLANES and ng <= SUBLANES and SUBLANES + ne <= LANES
    return pl.pallas_call(
        functools.partial(_router_kernel, ng=ng, ne=ne),
        out_shape=(jax.ShapeDtypeStruct((nt, SUBLANES, tm), F32), jax.ShapeDtypeStruct((nt, ne, LANES), F32)),
        grid=(nt // per,),
        in_specs=[
            pl.BlockSpec((per * tm, d), lambda i: (i, 0)),
            pl.BlockSpec((LANES, d), lambda i: (0, 0)),
            pl.BlockSpec((tm, tm), lambda i: (0, 0)),
            pl.BlockSpec((ne, ne), lambda i: (0, 0)),
        ],
        out_specs=(pl.BlockSpec((per, SUBLANES, tm), lambda i: (i, 0, 0)),
                   pl.BlockSpec((per, ne, LANES), lambda i: (i, 0, 0))),
        compiler_params=_cparams(("parallel",)),
        name="moe_router",
    )(u, wr_t, upper, ltri)


def _pack_halves(lo, hi):
    lo_bits = lax.shift_right_logical(pltpu.bitcast(lo, U32), jnp.uint32(16))
    hi_bits = pltpu.bitcast(hi, U32) & jnp.uint32(0xFFFF0000)
    return hi_bits | lo_bits


def _unpack_halves(w):
    lo = pltpu.bitcast(lax.shift_left(w, jnp.uint32(16)), F32)
    hi = pltpu.bitcast(w & jnp.uint32(0xFFFF0000), F32)
    return lo.astype(BF16), hi.astype(BF16)


def _round_bf16(x):
    return x.astype(BF16).astype(F32)


def _group_copy(src, src_g, dst, dst_g, sem):
    g8 = lambda g: pl.ds(pl.multiple_of(g * SUBLANES, SUBLANES), SUBLANES)
    return pltpu.make_async_copy(src.at[g8(src_g), :], dst.at[g8(dst_g), :], sem)


def _for_groups(n, fn, unroll=4):
    def body_many(i, c):
        for j in range(unroll):
            fn(i * unroll + j)
        return c

    def body_one(g, c):
        fn(g)
        return c

    full = lax.div(n, jnp.int32(unroll))
    lax.fori_loop(0, full, body_many, 0)
    lax.fori_loop(full * unroll, n, body_one, 0)


def _dispatch_kernel(gdst_ref, ngt_ref, pad0_ref, npad_ref, tail_ref, u_ref, pos_ref, xs_out, loc, zeros, sem,
                     zsem, *, tm, nl, ne):
    i = pl.program_id(0)
    nt = pl.num_programs(0)
    slot = i % 2
    nlg = nl // SUBLANES

    def copy(step, s, g):
        return _group_copy(loc.at[s], g, xs_out, gdst_ref[step * nlg + g], sem.at[s])

    def group_wait(s):
        _group_copy(loc.at[s], 0, xs_out, 0, sem.at[s]).wait()

    def zero_copy(e, g):
        return _group_copy(zeros, 0, xs_out, pad0_ref[e] + g, zsem)

    def zero_block(t):
        first = pl.multiple_of(tail_ref[0] + t * MOE_BM, MOE_BM)
        return pltpu.make_async_copy(zeros, xs_out.at[pl.ds(first, MOE_BM), :], zsem)

    @pl.when(i == 0)
    def _():
        zeros[...] = jnp.zeros(zeros.shape, zeros.dtype)
        for e in range(ne):
            _for_groups(npad_ref[e], lambda g, e=e: zero_copy(e, g).start())
        _for_groups(tail_ref[1], lambda t: zero_block(t).start())

    @pl.when(i >= 2)
    def _():
        _for_groups(ngt_ref[i - 2], lambda g: group_wait(slot))

    pos1 = pos_ref[4:5, :].astype(jnp.int32)
    pos2 = pos_ref[5:6, :].astype(jnp.int32)
    half = u_ref.shape[1] // 2
    u = u_ref[...]
    rc = 256
    for r in range(0, nl, rc):
        p = lax.broadcasted_iota(jnp.int32, (rc, tm), 0) + r
        onehot = jnp.where((p == pos1) | (p == pos2), 1.0, 0.0).astype(BF16)
        rows = _dot(onehot, u)
        loc[slot, r:r + rc, :] = _pack_halves(rows[:, :half], rows[:, half:])
    _for_groups(ngt_ref[i], lambda g: copy(i, slot, g).start())

    @pl.when(i == nt - 1)
    def _():
        @pl.when(i >= 1)
        def _():
            _for_groups(ngt_ref[i - 1], lambda g: group_wait(1 - slot))

        _for_groups(ngt_ref[i], lambda g: group_wait(slot))
        for e in range(ne):
            _for_groups(npad_ref[e], lambda g: _group_copy(zeros, 0, xs_out, 0, zsem).wait())
        _for_groups(tail_ref[1], lambda t: pltpu.make_async_copy(zeros, xs_out.at[0:MOE_BM, :], zsem).wait())


def _dispatch_call(u, pos_rows, gdst, ngt, pad0, npad, tail, nrows, tm, nl):
    t, d = u.shape
    ne = npad.shape[0]
    grid_spec = pltpu.PrefetchScalarGridSpec(
        num_scalar_prefetch=5,
        grid=(t // tm,),
        in_specs=[
            pl.BlockSpec((tm, d), lambda i, *_: (i, 0)),
            pl.BlockSpec((None, SUBLANES, tm), lambda i, *_: (i, 0, 0)),
        ],
        out_specs=pl.BlockSpec(memory_space=pl.ANY),
        scratch_shapes=[pltpu.VMEM((2, nl, d // 2), U32), pltpu.VMEM((MOE_BM, d // 2), U32),
                        pltpu.SemaphoreType.DMA((2,)), pltpu.SemaphoreType.DMA(())],
    )
    return pl.pallas_call(
        functools.partial(_dispatch_kernel, tm=tm, nl=nl, ne=ne),
        out_shape=jax.ShapeDtypeStruct((nrows, d // 2), U32),
        grid_spec=grid_spec,
        compiler_params=_cparams(("arbitrary",)),
        name="moe_dispatch",
    )(gdst, ngt, pad0, npad, tail, u, pos_rows)


def _expert_kernel(row0_ref, nblk_ref, tail_ref, x_hbm, w13_ref, w2_ref, y_hbm, xbuf, ybuf, w13_bf, w2_bf,
                   xsem, ysem, *, hid):
    e = pl.program_id(0)
    n = nblk_ref[e]
    g0 = row0_ref[e] // MOE_BM
    total = tail_ref[0] // MOE_BM
    nx, ny = xbuf.shape[0], ybuf.shape[0]
    ahead = nx - 1
    half = xbuf.shape[2]

    def rows(g):
        return pl.ds(pl.multiple_of(g * MOE_BM, MOE_BM), MOE_BM)

    def x_copy(g):
        return pltpu.make_async_copy(x_hbm.at[rows(g), :], xbuf.at[g % nx], xsem.at[g % nx])

    def y_copy(g, s):
        return pltpu.make_async_copy(ybuf.at[s], y_hbm.at[rows(g), :], ysem.at[s])

    @pl.when(e == 0)
    def _():
        for g in range(ahead):
            @pl.when(g < total)
            def _(g=g):
                x_copy(g).start()

    @pl.when(n > 0)
    def _():
        w13_bf[...] = w13_ref[...].astype(BF16)
        w2_bf[...] = w2_ref[...].astype(BF16)

        def block(c, carry):
            g = g0 + c
            x_copy(g).wait()

            @pl.when(g + ahead < total)
            def _():
                x_copy(g + ahead).start()

            @pl.when(g >= ny)
            def _():
                y_copy(g - ny, g % ny).wait()

            x_lo, x_hi = _unpack_halves(xbuf[g % nx])
            hb = _dot(x_lo, w13_bf[:half, :]) + _dot(x_hi, w13_bf[half:, :])
            a = hb[:, :hid]
            act = (a * _sigmoid(a)) * hb[:, hid:]
            y = _round_bf16(_dot(act.astype(BF16), w2_bf[...]))
            ybuf[g % ny] = _pack_halves(y[:, :half], y[:, half:])
            y_copy(g, g % ny).start()
            return carry

        lax.fori_loop(0, n, block, 0)

    @pl.when(e == pl.num_programs(0) - 1)
    def _():
        for j in range(ny):
            @pl.when(total - ny + j >= 0)
            def _(j=j):
                g = total - ny + j
                y_copy(g, g % ny).wait()

        ybuf[0] = jnp.zeros(ybuf.shape[1:], ybuf.dtype)
        _for_groups(tail_ref[1], lambda t: y_copy(total + t, 0).start())
        _for_groups(tail_ref[1], lambda t: y_copy(total + t, 0).wait())


def _expert_call(xs, row0, nblk, tail, w13, w2, l):
    r, half = xs.shape
    d = 2 * half
    ne = w13.shape[1]
    hid = w2.shape[2]
    nx, ny = 5, 3
    grid_spec = pltpu.PrefetchScalarGridSpec(
        num_scalar_prefetch=3,
        grid=(ne,),
        in_specs=[
            pl.BlockSpec(memory_space=pl.ANY),
            pl.BlockSpec((None, None, d, 2 * hid), lambda e, *_: (l, e, 0, 0)),
            pl.BlockSpec((None, None, hid, d), lambda e, *_: (l, e, 0, 0)),
        ],
        out_specs=pl.BlockSpec(memory_space=pl.ANY),
        scratch_shapes=[
            pltpu.VMEM((nx, MOE_BM, half), U32), pltpu.VMEM((ny, MOE_BM, half), U32),
            pltpu.VMEM((d, 2 * hid), BF16), pltpu.VMEM((hid, d), BF16),
            pltpu.SemaphoreType.DMA((nx,)), pltpu.SemaphoreType.DMA((ny,)),
        ],
    )
    return pl.pallas_call(
        functools.partial(_expert_kernel, hid=hid),
        out_shape=jax.ShapeDtypeStruct((r, half), U32),
        grid_spec=grid_spec,
        compiler_params=_cparams(("arbitrary",)),
        name="moe_experts",
    )(row0, nblk, tail, xs, w13, w2)


def _combine_kernel(gdst_ref, ngt_ref, x_ref, rt_ref, gt_ref, nw_ref, sh_ref, sc_ref, ys_ref, out_a, out_b,
                    loc, sem, *, lay, tm, nl, final):
    i = pl.program_id(0)
    nt = pl.num_programs(0)
    slot = i % 2
    nlg = nl // SUBLANES

    def copy(step, s, g):
        return _group_copy(ys_ref, gdst_ref[step * nlg + g], loc.at[s], g, sem.at[s])

    @pl.when(i == 0)
    def _():
        loc[...] = jnp.zeros(loc.shape, loc.dtype)
        _for_groups(ngt_ref[0], lambda g: copy(0, 0, g).start())

    _for_groups(ngt_ref[i], lambda g: _group_copy(ys_ref, 0, loc.at[slot], 0, sem.at[slot]).wait())

    @pl.when(i + 1 < nt)
    def _():
        _for_groups(ngt_ref[i + 1], lambda g: copy(i + 1, 1 - slot, g).start())

    rt = rt_ref[...]
    gate1, gate2 = rt[:, 2:3], rt[:, 3:4]
    pos1, pos2 = rt[:, 4:5].astype(jnp.int32), rt[:, 5:6].astype(jnp.int32)
    half = loc.shape[2]
    rc = 256
    mix_lo = jnp.zeros((tm, half), F32)
    mix_hi = jnp.zeros((tm, half), F32)
    for r in range(0, nl, rc):
        p = lax.broadcasted_iota(jnp.int32, (tm, rc), 1) + r
        wgt = (jnp.where(p == pos1, gate1, 0.0) + jnp.where(p == pos2, gate2, 0.0)).astype(BF16)
        y_lo, y_hi = _unpack_halves(loc[slot, r:r + rc, :])
        mix_lo = mix_lo + _dot(wgt, y_lo)
        mix_hi = mix_hi + _dot(wgt, y_hi)
    grp = lay.group(i * tm)
    gate = gt_ref[pl.ds(grp, 1), :]
    x_lo = x_ref[:, :half] + gate[:, :half] * mix_lo
    x_hi = x_ref[:, half:] + gate[:, half:] * mix_hi
    ms = (jnp.sum(x_lo * x_lo, axis=-1, keepdims=True) + jnp.sum(x_hi * x_hi, axis=-1, keepdims=True)) / (2 * half)
    inv = lax.rsqrt(ms + EPS)
    nw = nw_ref[...]
    if final:
        y_lo = x_lo * inv * nw[:, :half]
        y_hi = x_hi * inv * nw[:, half:]
        for ref, cond in ((out_a, i * tm < lay.tp), (out_b, i * tm >= lay.tp)):
            @pl.when(cond)
            def _(ref=ref):
                ref[:, :half] = y_lo
                ref[:, half:] = y_hi
    else:
        out_a[:, :half] = x_lo
        out_a[:, half:] = x_hi
        sh = sh_ref[pl.ds(grp, 1), :]
        sc = sc_ref[pl.ds(grp, 1), :]
        out_b[:, :half] = (x_lo * inv * nw[:, :half] * (1.0 + sc[:, :half]) + sh[:, :half]).astype(BF16)
        out_b[:, half:] = (x_hi * inv * nw[:, half:] * (1.0 + sc[:, half:]) + sh[:, half:]).astype(BF16)


def _combine_call(lay, x, route, mods, l, mod_next, gdst, ngt, ys, tm, nl, final):
    t, d = x.shape
    rows = pl.BlockSpec((tm, d), lambda i, *_: (i, 0))
    if final:
        ntp = lay.tp // tm
        out_specs = (pl.BlockSpec((tm, d), lambda i, *_: (jnp.minimum(i, ntp - 1), 0)),
                     pl.BlockSpec((tm, d), lambda i, *_: (jnp.maximum(i - ntp, 0), 0)))
        out_shape = (jax.ShapeDtypeStruct((lay.tp, d), F32), jax.ShapeDtypeStruct((lay.ts, d), F32))
    else:
        out_specs = (rows, rows)
        out_shape = (jax.ShapeDtypeStruct(x.shape, F32), jax.ShapeDtypeStruct(x.shape, BF16))
    grid_spec = pltpu.PrefetchScalarGridSpec(
        num_scalar_prefetch=2,
        grid=(t // tm,),
        in_specs=[
            rows,
            pl.BlockSpec((tm, SUBLANES), lambda i, *_: (i, 0)),
            lay.mod_spec(l, 5),
        ] + mod_next.specs + [pl.BlockSpec(memory_space=pl.ANY)],
        out_specs=out_specs,
        scratch_shapes=[pltpu.VMEM((2, nl, d // 2), U32), pltpu.SemaphoreType.DMA((2,))],
    )
    return pl.pallas_call(
        functools.partial(_combine_kernel, lay=lay, tm=tm, nl=nl, final=final),
        out_shape=out_shape,
        grid_spec=grid_spec,
        compiler_params=_cparams(("arbitrary",)),
        name="moe_combine_final" if final else "moe_combine",
    )(gdst, ngt, x, route, mods, *mod_next.args, ys)


def _moe_layer(lay, x, u2, mods, l, mod_next, final, wr_t, upper, ltri, w13, w2, ng, ne):
    t, d = x.shape
    tm = upper.shape[0]
    nt = t // tm
    bmg = MOE_BM // SUBLANES
    nl = -(-(MOE_TOP_K * tm + ne * (SUBLANES - 1)) // 256) * 256
    nlg = nl // SUBLANES
    route_t, counts = _router_call(u2, wr_t, upper, ltri, ng, ne)
    route = jnp.swapaxes(route_t, 1, 2).reshape(t, SUBLANES)

    c8 = counts[:, :, 0].astype(jnp.int32)
    lend = jnp.cumsum(c8, axis=1)
    lstart = lend - c8
    ngt = lend[:, -1].astype(jnp.int32)
    tot = jnp.sum(c8, axis=0)
    padded = (tot + bmg - 1) // bmg * bmg
    gend = jnp.cumsum(padded)
    gbase = (gend - padded)[None, :] + jnp.cumsum(c8, axis=0) - c8
    nb = -(-(MOE_TOP_K * t + nt * ne * (SUBLANES - 1)) // MOE_BM) + ne
    g = jnp.arange(nlg, dtype=jnp.int32)[None, :, None]
    owner = (g >= lstart[:, None, :]) & (g < lend[:, None, :])
    gdst = g[:, :, 0] + jnp.sum(jnp.where(owner, (gbase - lstart)[:, None, :], 0), axis=-1)
    gdst = gdst.reshape(nt * nlg).astype(jnp.int32)
    row0 = ((gend - padded) * SUBLANES).astype(jnp.int32)
    nblk = (padded // bmg).astype(jnp.int32)
    tail = jnp.stack([gend[-1] * SUBLANES, nb - gend[-1] // bmg]).astype(jnp.int32)

    pad0 = (gend - padded + tot).astype(jnp.int32)
    npad = (padded - tot).astype(jnp.int32)
    xs = _dispatch_call(u2, route_t, gdst, ngt, pad0, npad, tail, nb * MOE_BM, tm, nl)
    ys = _expert_call(xs, row0, nblk, tail, w13, w2, l)
    return _combine_call(lay, x, route, mods, l, mod_next, gdst, ngt, ys, tm, nl, final)


def _lower_tri(n, strict):
    r = np.arange(n)
    m = (r[None, :] < r[:, None]) if strict else (r[None, :] <= r[:, None])
    return jnp.asarray(m.astype(np.float32), dtype=BF16)


def kernel(x_prompt, x_sample, state_C, state_n, state_m, c, c_ctx, ada_w, ada_b, norm1_w, norm2_w, m_w_in, m_b_gate, m_head_norm_w, m_w_out, f_w_out, cv_w_pw1, cv_b_pw1, cv_w_dw, cv_b_dw, cv_ln_w, cv_ln_b, cv_w_pw2, cv_b_pw2, r_w_group, r_w_expert, e_w13, e_w2, final_norm_w):
    nbp, sp, d = x_prompt.shape
    nbs, ss, _ = x_sample.shape
    assert ss % GRID_W == 0
    lay = _Layout(nbp, sp, nbs, ss, d)
    depth = ada_w.shape[0]
    nh, dh = state_C.shape[3], state_C.shape[4]
    di = nh * dh
    ng = r_w_group.shape[2]
    ne = r_w_expert.shape[2]
    assert ng == MOE_GROUPS and ng + ne <= LANES and 4 * nh <= LANES and MOE_TOP_K == 2

    cv = jnp.zeros((lay.ngp, d), F32).at[0].set(c_ctx).at[1:1 + nbs].set(c)
    mods = _ada_call(cv, ada_w, ada_b)

    tri_l = _lower_tri(MLSTM_L, strict=False)
    upper = _lower_tri(lay.row_tile(512), strict=True).T
    ltri = _lower_tri(ne, strict=True)
    row = lambda a: a.reshape(1, -1)
    w_in_t = jnp.swapaxes(m_w_in, 1, 2)
    tn = 1024
    assert di % tn == 0
    nkb = di // tn

    x, u1 = _prep_call(lay, x_prompt.reshape(lay.tp, d), x_sample.reshape(lay.ts, d),
                       _Mod(lay, mods, norm1_w[0], 0, 0))
    y = None
    qvos, kts, gpts = [], [], []
    for l in range(depth):
        j, kind = l // N_MIXERS, l % N_MIXERS
        mod2 = _Mod(lay, mods, norm2_w[l], l, 3)
        if kind == 0:
            wg = jnp.zeros((LANES, d), F32).at[:4 * nh].set(w_in_t[j, 4 * di:])
            bg = jnp.zeros((1, LANES), F32).at[0, :4 * nh].set(m_b_gate[j])
            gp = _gates_call(lay, x, _Mod(lay, mods, norm1_w[l], l, 0), wg, bg, tri_l, nh)
            gpt = gp[:, :4 * nh].T
            qvo_blocks = list(range(nkb)) + list(range(2 * nkb, 4 * nkb))
            qvo = _proj_call(lay, u1, w_in_t, j, qvo_blocks, tn, transposed=False)
            kt = _proj_call(lay, u1, w_in_t, j, list(range(nkb, 2 * nkb)), tn, transposed=True)
            hw = row(m_head_norm_w[j])
            hg_p = _mlstm_call(lay, qvo, kt, gp, gpt, hw, nh, dh, prompt=True)
            n0 = state_n[:, j].reshape(nbs, 2 * nh, dh)
            m0 = jnp.broadcast_to(state_m[:, j].reshape(nbs, 2 * nh, 1), (nbs, 2 * nh, LANES))
            hg_s = _mlstm_call(lay, qvo, kt, gp, gpt, hw, nh, dh, prompt=False, state=(state_C, j, n0, m0))
            x, u2 = _mm_res_call(lay, hg_p, hg_s, m_w_out[j].astype(BF16), x, mods, l, mod2)
            qvos.append(qvo)
            kts.append(kt)
            gpts.append(gpt)
        elif kind == 1:
            wo = f_w_out[j].astype(BF16)
            u2 = jnp.zeros((lay.t, d), BF16)
            x, u2 = _fnet_call(lay, x, u1, u2, mods, l, wo, mod2, prompt=True)
            x, u2 = _fnet_call(lay, x, u1, u2, mods, l, wo, mod2, prompt=False)
        else:
            glu = _glu_call(lay, u1, cv_w_pw1[j].astype(BF16), row(cv_b_pw1[j]))
            x, u2 = _conv_call(lay, glu, cv_w_dw[j], cv_b_dw[j], cv_ln_w[j], cv_ln_b[j], cv_w_pw2[j].astype(BF16),
                               cv_b_pw2[j], x, mods, l, mod2)
        wr = jnp.zeros((LANES, d), F32).at[:ng].set(r_w_group[l].T).at[SUBLANES:SUBLANES + ne].set(r_w_expert[l].T)
        final = l + 1 == depth
        mod_next = _Mod(lay, mods, final_norm_w, l, 0) if final else _Mod(lay, mods, norm1_w[l + 1], l + 1, 0)
        outs = _moe_layer(lay, x, u2, mods, l, mod_next, final, wr.astype(BF16), upper, ltri, e_w13, e_w2, ng, ne)
        if final:
            y = outs
        else:
            x, u1 = outs

    y_prompt = y[0].reshape(nbp, sp, d)
    y_sample = y[1].reshape(nbs, ss, d)
    new_c, new_n, new_m = _state_call(lay, qvos, kts, gpts, nh, dh)
    return (y_prompt, y_sample, new_c, new_n, new_m)
```

```python
import functools
import math

import numpy as np
import jax
import jax.numpy as jnp
from jax import lax
from jax.experimental import pallas as pl
from jax.experimental.pallas import tpu as pltpu

F32 = jnp.float32
BF16 = jnp.bfloat16
U32 = jnp.uint32
EPS = 1e-6
GRID_W = 64
N_MIXERS = 3
FNET_GROUPS = 4
CONV_WIDTH = 31
MOE_GROUPS = 4
MOE_TOP_K = 2

LANES = 128
SUBLANES = 8
MLSTM_L = 256
MOE_BM = 256
CONV_HALO = 16
VMEM_LIMIT = 56 * 1024 * 1024


def _cparams(sem, vmem=VMEM_LIMIT):
    return pltpu.CompilerParams(dimension_semantics=sem, vmem_limit_bytes=vmem)


def _dot(a, b):
    return jnp.dot(a, b, preferred_element_type=F32)


def _dot_nt(a, b):
    return lax.dot_general(a, b, (((1,), (1,)), ((), ())), preferred_element_type=F32)


def _rms(x, w):
    return x * lax.rsqrt(jnp.mean(x * x, axis=-1, keepdims=True) + EPS) * w


def _modulate(x, w, shift, scale):
    return _rms(x, w) * (1.0 + scale) + shift


def _sigmoid(x):
    return 1.0 / (1.0 + jnp.exp(-x))


def _log_sigmoid(x):
    return jnp.minimum(x, 0.0) - jnp.log(1.0 + jnp.exp(-jnp.abs(x)))


def _split2(x):
    hi = x.astype(BF16)
    return hi, (x - hi.astype(F32)).astype(BF16)


def _split3(x):
    hi = x.astype(BF16)
    r1 = x - hi.astype(F32)
    mid = r1.astype(BF16)
    return hi, mid, (r1 - mid.astype(F32)).astype(BF16)


class _Layout:
    def __init__(self, nbp, sp, nbs, ss, d):
        self.nbp, self.sp, self.nbs, self.ss, self.d = nbp, sp, nbs, ss, d
        self.tp, self.ts = nbp * sp, nbs * ss
        self.t = self.tp + self.ts
        assert self.tp % ss == 0, "latent sequences must start on a block boundary of their own length"
        self.ngp = -(-(1 + nbs) // SUBLANES) * SUBLANES

    def group(self, row0):
        return jnp.where(row0 < self.tp, 0, 1 + (row0 - self.tp) // self.ss)

    def row_tile(self, want):
        tm = math.gcd(math.gcd(self.tp, self.ss), want)
        assert tm % SUBLANES == 0
        return tm

    def mod_spec(self, l, chunk):
        return pl.BlockSpec((None, self.ngp, self.d), lambda *_: (l, 0, chunk))

    def row_spec(self):
        return pl.BlockSpec((1, self.d), lambda *_: (0, 0))


class _Mod:
    def __init__(self, lay, mods, nw, l, c_shift):
        self.args = (nw.reshape(1, -1), mods, mods)
        self.specs = [lay.row_spec(), lay.mod_spec(l, c_shift), lay.mod_spec(l, c_shift + 1)]


def _mod_value(x, nw_ref, sh_ref, sc_ref, grp):
    return _modulate(x, nw_ref[...], sh_ref[pl.ds(grp, 1), :], sc_ref[pl.ds(grp, 1), :])


def _ada_kernel(cv_ref, w_ref, b_ref, o_ref):
    s = cv_ref[...]
    s = s * _sigmoid(s)
    o_ref[...] = _dot(s.astype(BF16), w_ref[...].astype(BF16)) + b_ref[...]


def _ada_call(cv, ada_w, ada_b):
    depth, d, n = ada_w.shape
    ngp = cv.shape[0]
    tn = min(n, 2048)
    return pl.pallas_call(
        _ada_kernel,
        out_shape=jax.ShapeDtypeStruct((depth, ngp, n), F32),
        grid=(depth, n // tn),
        in_specs=[
            pl.BlockSpec((ngp, d), lambda l, j: (0, 0)),
            pl.BlockSpec((None, d, tn), lambda l, j: (l, 0, j)),
            pl.BlockSpec((None, 1, tn), lambda l, j: (l, 0, j)),
        ],
        out_specs=pl.BlockSpec((None, ngp, tn), lambda l, j: (l, 0, j)),
        compiler_params=_cparams(("parallel", "parallel")),
        name="ada_mods",
    )(cv, ada_w, ada_b.reshape(depth, 1, n))


def _gate_raw(u, wh, wl):
    u_hi, u_lo = _split2(u)
    return _dot_nt(u_hi, wh) + _dot_nt(u_lo, wh) + _dot_nt(u_hi, wl), u_hi


def _prep_kernel(xp_ref, xs_ref, nw_ref, sh_ref, sc_ref, wh_ref, wl_ref, x_ref, u_ref, g_ref, *, lay, tm):
    i = pl.program_id(0)
    grp = lay.group(i * tm)
    for src, cond in ((xp_ref, i * tm < lay.tp), (xs_ref, i * tm >= lay.tp)):
        @pl.when(cond)
        def _(src=src):
            rc = min(tm, 256)
            for r in range(0, tm, rc):
                x = src[r:r + rc, :]
                x_ref[r:r + rc, :] = x
                g, u_hi = _gate_raw(_mod_value(x, nw_ref, sh_ref, sc_ref, grp), wh_ref[...], wl_ref[...])
                u_ref[r:r + rc, :] = u_hi
                g_ref[r:r + rc, :] = g


def _prep_call(lay, xp, xs, mod, gate_w):
    d = lay.d
    tm = lay.row_tile(1024)
    ntp = lay.tp // tm
    rows = pl.BlockSpec((tm, d), lambda i: (i, 0))
    wspec = pl.BlockSpec((LANES, d), lambda i: (0, 0))
    return pl.pallas_call(
        functools.partial(_prep_kernel, lay=lay, tm=tm),
        out_shape=(jax.ShapeDtypeStruct((lay.t, d), F32), jax.ShapeDtypeStruct((lay.t, d), BF16),
                   jax.ShapeDtypeStruct((lay.t, LANES), F32)),
        grid=(lay.t // tm,),
        in_specs=[
            pl.BlockSpec((tm, d), lambda i: (jnp.minimum(i, ntp - 1), 0)),
            pl.BlockSpec((tm, d), lambda i: (jnp.maximum(i - ntp, 0), 0)),
        ] + mod.specs + [wspec, wspec],
        out_specs=(rows, rows, pl.BlockSpec((tm, LANES), lambda i: (i, 0))),
        compiler_params=_cparams(("arbitrary",)),
        name="prep_modulate",
    )(xp, xs, *mod.args, *gate_w)


def _gates_kernel(g_ref, b_ref, tri_ref, o_ref, *, nh, tm):
    l = tri_ref.shape[0]
    tri = tri_ref[...]
    lane = lax.broadcasted_iota(jnp.int32, (l, LANES), 1)
    is_f = ((lane >= nh) & (lane < 2 * nh)) | ((lane >= 3 * nh) & (lane < 4 * nh))
    is_a = (lane < nh) | ((lane >= 2 * nh) & (lane < 3 * nh))
    for r in range(0, tm, l):
        g = g_ref[r:r + l, :] + b_ref[...]
        lf = jnp.where(is_f, _log_sigmoid(g), 0.0)
        hi, mid, lo = _split3(lf)
        prefix = _dot(tri, hi) + _dot(tri, mid) + _dot(tri, lo)
        suffix = jnp.sum(lf, axis=0, keepdims=True) - prefix + lf
        b = jnp.where(lane < 2 * nh, prefix, suffix)
        a = g - pltpu.roll(b, LANES - nh, 1)
        o_ref[r:r + l, :] = jnp.where(is_a, a, b)


def _gates_call(lay, graw, bias, tri, nh):
    t = graw.shape[0]
    l = tri.shape[0]
    tm = lay.row_tile(2048)
    assert tm % l == 0
    return pl.pallas_call(
        functools.partial(_gates_kernel, nh=nh, tm=tm),
        out_shape=jax.ShapeDtypeStruct((t, LANES), F32),
        grid=(t // tm,),
        in_specs=[
            pl.BlockSpec((tm, LANES), lambda i: (i, 0)),
            pl.BlockSpec((1, LANES), lambda i: (0, 0)),
            pl.BlockSpec((l, l), lambda i: (0, 0)),
        ],
        out_specs=pl.BlockSpec((tm, LANES), lambda i: (i, 0)),
        compiler_params=_cparams(("parallel",)),
        name="mlstm_gates",
    )(graw, bias, tri)


def _proj_kernel(u_ref, w_ref, o_ref, w_bf, *, transposed):
    @pl.when(pl.program_id(1) == 0)
    def _():
        w_bf[...] = w_ref[...].astype(BF16)

    if transposed:
        o_ref[...] = _dot_nt(w_bf[...], u_ref[...]).astype(BF16)
    else:
        o_ref[...] = _dot_nt(u_ref[...], w_bf[...]).astype(BF16)


def _proj_call(lay, u, w_in_t, jl, blocks, tn, transposed):
    t, d = u.shape
    tm = 2048 if t % 2048 == 0 else lay.row_tile(1024)
    nb = len(blocks)
    first, gap_at, gap = blocks[0], None, 0
    for idx in range(1, nb):
        if blocks[idx] != blocks[idx - 1] + 1:
            assert gap_at is None
            gap_at, gap = idx, blocks[idx] - blocks[idx - 1] - 1
    wblk = (lambda j: first + j) if gap_at is None else (lambda j: first + j + jnp.where(j >= gap_at, gap, 0))
    if transposed:
        out_shape = jax.ShapeDtypeStruct((nb * tn, t), BF16)
        out_spec = pl.BlockSpec((tn, tm), lambda j, i: (j, i))
    else:
        out_shape = jax.ShapeDtypeStruct((t, nb * tn), BF16)
        out_spec = pl.BlockSpec((tm, tn), lambda j, i: (i, j))
    return pl.pallas_call(
        functools.partial(_proj_kernel, transposed=transposed),
        out_shape=out_shape,
        grid=(nb, t // tm),
        in_specs=[
            pl.BlockSpec((tm, d), lambda j, i: (i, 0)),
            pl.BlockSpec((None, tn, d), lambda j, i: (jl, wblk(j), 0)),
        ],
        out_specs=out_spec,
        scratch_shapes=[pltpu.VMEM((tn, d), BF16)],
        compiler_params=_cparams(("parallel", "arbitrary")),
        name="mlstm_proj_t" if transposed else "mlstm_proj",
    )(u, w_in_t)


def _col(tile, c):
    lane = lax.broadcasted_iota(jnp.int32, tile.shape, 1)
    return jnp.sum(jnp.where(lane == c, tile, 0.0), axis=-1, keepdims=True)


def _dir_masks(l):
    r = lax.broadcasted_iota(jnp.int32, (l, l), 0)
    c = lax.broadcasted_iota(jnp.int32, (l, l), 1)
    return c <= r, c >= r


def _head_epilogue(h, hw, o):
    hn = h * lax.rsqrt(jnp.mean(h * h, axis=-1, keepdims=True) + EPS) * hw
    return (hn * _sigmoid(o.astype(F32))).astype(BF16)


def _row_times_kt(w_row, kt):
    hi, lo = _split2(w_row)
    sub = lax.broadcasted_iota(jnp.int32, (SUBLANES, w_row.shape[1]), 0)
    stacked = jnp.where(sub == 0, hi.astype(F32), jnp.where(sub == 1, lo.astype(F32), 0.0)).astype(BF16)
    res = _dot_nt(stacked, kt)
    return res[0:1, :] + res[1:2, :]


def _mlstm_single_kernel(q_ref, kt_ref, v_ref, o_ref, gp_ref, gpt_ref, hw_ref, out_ref, *, nh, dh, scale):
    l = q_ref.shape[0]
    gp = gp_ref[...]
    masks = _dir_masks(l)
    for h in range(nh):
        cols = slice(h * dh, (h + 1) * dh)
        qk = _dot(q_ref[:, cols], kt_ref[cols, :])
        p = None
        for d in range(2):
            a_r = gpt_ref[2 * nh * d + h:2 * nh * d + h + 1, :]
            b_c = gp[:, 2 * nh * d + nh + h:2 * nh * d + nh + h + 1]
            g = jnp.where(masks[d], a_r, -jnp.inf)
            m = jnp.maximum(jnp.max(g, axis=-1, keepdims=True), 0.0)
            s = qk * jnp.exp(g - m) * scale
            den = jnp.sum(s, axis=-1, keepdims=True)
            inv = 1.0 / jnp.maximum(jnp.abs(den), jnp.exp(-(b_c + m)))
            p = s * inv if p is None else p + s * inv
        hh = _dot(p.astype(BF16), v_ref[:, cols])
        out_ref[:, cols] = _head_epilogue(hh, hw_ref[:, cols], o_ref[:, cols])


def _mlstm_multi_kernel(q_ref, kt_ref, v_ref, o_ref, gp_ref, gpt_ref, hw_ref, c0_ref, n0_ref, m0_ref,
                        out_ref, cst, cbf, *, nh, nc, l, scale):
    h = pl.program_id(1)
    masks = _dir_masks(l)
    m_in = [[None] * nc for _ in range(2)]
    n_in = [[None] * nc for _ in range(2)]
    for d in range(2):
        cst[...] = c0_ref[d]
        n = n0_ref[pl.ds(d * nh + h, 1), :]
        m = m0_ref[pl.ds(d * nh + h, 1), 0:1]
        order = list(range(nc)) if d == 0 else list(range(nc - 1, -1, -1))
        for step, c in enumerate(order):
            m_in[d][c], n_in[d][c] = m, n
            cbf[d, c] = cst[...].astype(BF16)
            if step + 1 < nc:
                r0 = c * l
                a_r = gpt_ref[pl.ds(2 * nh * d + h, 1), r0:r0 + l]
                b_r = gpt_ref[pl.ds(2 * nh * d + nh + h, 1), r0:r0 + l]
                m_last = jnp.maximum(jnp.max(a_r, axis=-1, keepdims=True), m)
                b_end = b_r[:, l - 1:l] if d == 0 else b_r[:, 0:1]
                decay = jnp.exp(m - m_last)
                w_end = jnp.exp(a_r - m_last)
                kt = kt_ref[:, r0:r0 + l]
                cst[...] = decay * cst[...] + _dot((kt.astype(F32) * w_end).astype(BF16), v_ref[r0:r0 + l, :])
                n = decay * n + _row_times_kt(w_end, kt)
                m = b_end + m_last
    hw = hw_ref[...]
    for c in range(nc):
        r0 = c * l
        q = q_ref[r0:r0 + l, :]
        v = v_ref[r0:r0 + l, :]
        qk = _dot(q, kt_ref[:, r0:r0 + l])
        qf = q.astype(F32)
        gp = gp_ref[r0:r0 + l, :]
        p = inter = None
        for d in range(2):
            m, n = m_in[d][c], n_in[d][c]
            a_r = gpt_ref[pl.ds(2 * nh * d + h, 1), r0:r0 + l]
            b_c = _col(gp, 2 * nh * d + nh + h)
            g = jnp.where(masks[d], a_r, -jnp.inf)
            mt = jnp.maximum(jnp.max(g, axis=-1, keepdims=True), m)
            s = qk * jnp.exp(g - mt) * scale
            w_prev = jnp.exp(m - mt) * scale
            den = jnp.sum(s, axis=-1, keepdims=True) + w_prev * jnp.sum(qf * n, axis=-1, keepdims=True)
            inv = 1.0 / jnp.maximum(jnp.abs(den), jnp.exp(-(b_c + mt)))
            term = (w_prev * inv) * _dot(q, cbf[d, c])
            p = s * inv if p is None else p + s * inv
            inter = term if inter is None else inter + term
        hh = _dot(p.astype(BF16), v) + inter
        out_ref[r0:r0 + l, :] = _head_epilogue(hh, hw, o_ref[r0:r0 + l, :])


def _mlstm_call(lay, qvo, kt, gp, gpt, hw, nh, dh, prompt, state=None):
    nb, s = (lay.nbp, lay.sp) if prompt else (lay.nbs, lay.ss)
    rb0 = 0 if prompt else lay.tp // s
    scale = dh ** -0.5
    di = nh * dh
    common_in = [
        pl.BlockSpec((s, dh), lambda b, h: (rb0 + b, h)),
        pl.BlockSpec((dh, s), lambda b, h: (h, rb0 + b)),
        pl.BlockSpec((s, dh), lambda b, h: (rb0 + b, nh + h)),
        pl.BlockSpec((s, dh), lambda b, h: (rb0 + b, 2 * nh + h)),
        pl.BlockSpec((s, LANES), lambda b, h: (rb0 + b, 0)),
        pl.BlockSpec((4 * nh, s), lambda b, h: (0, rb0 + b)),
        pl.BlockSpec((1, dh), lambda b, h: (0, h)),
    ]
    if prompt:
        assert s == MLSTM_L
        return pl.pallas_call(
            functools.partial(_mlstm_single_kernel, nh=nh, dh=dh, scale=scale),
            out_shape=jax.ShapeDtypeStruct((nb * s, di), BF16),
            grid=(nb,),
            in_specs=[
                pl.BlockSpec((s, di), lambda b: (rb0 + b, 0)),
                pl.BlockSpec((di, s), lambda b: (0, rb0 + b)),
                pl.BlockSpec((s, di), lambda b: (rb0 + b, 1)),
                pl.BlockSpec((s, di), lambda b: (rb0 + b, 2)),
                pl.BlockSpec((s, LANES), lambda b: (rb0 + b, 0)),
                pl.BlockSpec((4 * nh, s), lambda b: (0, rb0 + b)),
                pl.BlockSpec((1, di), lambda b: (0, 0)),
            ],
            out_specs=pl.BlockSpec((s, di), lambda b: (b, 0)),
            compiler_params=_cparams(("parallel",)),
            name="mlstm_prompt",
        )(qvo, kt, qvo, qvo, gp, gpt, hw)
    state_c, jl, n0, m0 = state
    nc = s // MLSTM_L
    return pl.pallas_call(
        functools.partial(_mlstm_multi_kernel, nh=nh, nc=nc, l=MLSTM_L, scale=scale),
        out_shape=jax.ShapeDtypeStruct((nb * s, di), BF16),
        grid=(nb, nh),
        in_specs=common_in + [
            pl.BlockSpec((None, None, 2, None, dh, dh), lambda b, h: (b, jl, 0, h, 0, 0)),
            pl.BlockSpec((None, 2 * nh, dh), lambda b, h: (b, 0, 0)),
            pl.BlockSpec((None, 2 * nh, LANES), lambda b, h: (b, 0, 0)),
        ],
        out_specs=pl.BlockSpec((s, dh), lambda b, h: (b, h)),
        scratch_shapes=[pltpu.VMEM((dh, dh), F32), pltpu.VMEM((2, nc, dh, dh), BF16)],
        compiler_params=_cparams(("parallel", "parallel")),
        name="mlstm_latent",
    )(qvo, kt, qvo, qvo, gp, gpt, hw, state_c, n0, m0)


def _state_kernel(*refs, nl, nh, dh):
    ins, (c_ref, n_ref, m_ref) = refs[:3 * nl], refs[3 * nl:]
    lyr = pl.program_id(0)
    for jl in range(nl):
        kt_ref, v_ref, gpt_ref = ins[3 * jl:3 * jl + 3]

        @pl.when(lyr == jl)
        def _(kt_ref=kt_ref, v_ref=v_ref, gpt_ref=gpt_ref):
            l = v_ref.shape[0]
            sub = lax.broadcasted_iota(jnp.int32, m_ref.shape, 0)
            lane = lax.broadcasted_iota(jnp.int32, m_ref.shape, 1)
            m_all = jnp.zeros(m_ref.shape, F32)
            for h in range(nh):
                kt = kt_ref[h * dh:(h + 1) * dh, :]
                ktf = kt.astype(F32)
                v = v_ref[:, h * dh:(h + 1) * dh]
                for d in range(2):
                    a_r = gpt_ref[2 * nh * d + h:2 * nh * d + h + 1, :]
                    b_r = gpt_ref[2 * nh * d + nh + h:2 * nh * d + nh + h + 1, :]
                    m_last = jnp.maximum(jnp.max(a_r, axis=-1, keepdims=True), 0.0)
                    b_end = b_r[:, l - 1:l] if d == 0 else b_r[:, 0:1]
                    w_end = jnp.exp(a_r - m_last)
                    c_ref[d, h] = _dot((ktf * w_end).astype(BF16), v)
                    n_ref[d, h:h + 1, :] = _row_times_kt(w_end, kt)
                    m_all = jnp.where((sub == d) & (lane == h), b_end + m_last, m_all)
            m_ref[...] = m_all


def _state_call(lay, qvos, kts, gpts, nh, dh):
    nl = len(qvos)
    nbp, s = lay.nbp, lay.sp
    di = nh * dh
    assert s == MLSTM_L

    def pick(jl):
        return lambda lyr, b: jnp.where(lyr == jl, b, jnp.where(lyr < jl, 0, nbp - 1))

    in_specs, args = [], []
    for jl in range(nl):
        pb = pick(jl)
        in_specs.append(pl.BlockSpec((di, s), lambda lyr, b, pb=pb: (0, pb(lyr, b))))
        in_specs.append(pl.BlockSpec((s, di), lambda lyr, b, pb=pb: (pb(lyr, b), 1)))
        in_specs.append(pl.BlockSpec((4 * nh, s), lambda lyr, b, pb=pb: (0, pb(lyr, b))))
        args += [kts[jl], qvos[jl], gpts[jl]]
    return pl.pallas_call(
        functools.partial(_state_kernel, nl=nl, nh=nh, dh=dh),
        out_shape=(
            jax.ShapeDtypeStruct((nbp, nl, 2, nh, dh, dh), F32),
            jax.ShapeDtypeStruct((nbp, nl, 2, nh, dh), F32),
            jax.ShapeDtypeStruct((nbp, nl, 2, nh), F32),
        ),
        grid=(nl, nbp),
        in_specs=in_specs,
        out_specs=(
            pl.BlockSpec((None, None, 2, nh, dh, dh), lambda lyr, b: (b, lyr, 0, 0, 0, 0)),
            pl.BlockSpec((None, None, 2, nh, dh), lambda lyr, b: (b, lyr, 0, 0, 0)),
            pl.BlockSpec((None, None, 2, nh), lambda lyr, b: (b, lyr, 0, 0)),
        ),
        compiler_params=_cparams(("arbitrary", "arbitrary")),
        name="mlstm_prompt_state",
    )(*args)


def _mm_res_kernel(ap_ref, as_ref, w_ref, x_ref, g_ref, nw_ref, sh_ref, sc_ref, o_ref, u_ref, *, lay, tm):
    i = pl.program_id(0)
    grp = lay.group(i * tm)
    gate = g_ref[pl.ds(grp, 1), :]
    w = w_ref[...]
    rc = min(tm, 256)
    for a_ref, cond in ((ap_ref, i * tm < lay.tp), (as_ref, i * tm >= lay.tp)):
        @pl.when(cond)
        def _(a_ref=a_ref):
            for r in range(0, tm, rc):
                xn = x_ref[r:r + rc, :] + gate * _dot(a_ref[r:r + rc, :], w)
                o_ref[r:r + rc, :] = xn
                u_ref[r:r + rc, :] = _mod_value(xn, nw_ref, sh_ref, sc_ref, grp).astype(BF16)


def _mm_res_call(lay, a_p, a_s, w_bf, x, mods, l, mod2):
    kdim = a_p.shape[1]
    t, d = x.shape
    tm = lay.row_tile(512)
    ntp = lay.tp // tm
    return pl.pallas_call(
        functools.partial(_mm_res_kernel, lay=lay, tm=tm),
        out_shape=(jax.ShapeDtypeStruct(x.shape, F32), jax.ShapeDtypeStruct(x.shape, BF16)),
        grid=(t // tm,),
        in_specs=[
            pl.BlockSpec((tm, kdim), lambda i: (jnp.minimum(i, ntp - 1), 0)),
            pl.BlockSpec((tm, kdim), lambda i: (jnp.maximum(i - ntp, 0), 0)),
            pl.BlockSpec((kdim, d), lambda i: (0, 0)),
            pl.BlockSpec((tm, d), lambda i: (i, 0)),
            lay.mod_spec(l, 2),
        ] + mod2.specs,
        out_specs=(pl.BlockSpec((tm, d), lambda i: (i, 0)), pl.BlockSpec((tm, d), lambda i: (i, 0))),
        compiler_params=_cparams(("arbitrary",)),
        name="mm_residual",
    )(a_p, a_s, w_bf, x, mods, *mod2.args)


def _fnet_kernel(x_ref, u_ref, u2_any, gt_ref, wc_ref, ds_ref, wo_ref, nw_ref, sh_ref, sc_ref, o_ref, u2_ref,
                 ab_scr, *, lay, row_base, groups, norm):
    s, d = x_ref.shape
    cg = d // groups
    grp = lay.group(row_base + pl.program_id(0) * s)
    wc = wc_ref[...]
    rc = min(s, 256)
    for g in range(groups):
        for r in range(0, s, rc):
            ab = _dot(u_ref[r:r + rc, g * cg:(g + 1) * cg], wc)
            ab_scr[r:r + rc, g * cg:(g + 1) * cg] = ab[:, :cg].astype(BF16)
            ab_scr[s + r:s + r + rc, g * cg:(g + 1) * cg] = ab[:, cg:].astype(BF16)
    gate = gt_ref[pl.ds(grp, 1), :]
    wo = wo_ref[...]
    for r in range(0, s, rc):
        y = _dot(ds_ref[r:r + rc, :], ab_scr[...]) * norm
        xn = x_ref[r:r + rc, :] + gate * _dot(y.astype(BF16), wo)
        o_ref[r:r + rc, :] = xn
        u2_ref[r:r + rc, :] = _mod_value(xn, nw_ref, sh_ref, sc_ref, grp).astype(BF16)


def _dft_mats(s, cg):
    kc = np.arange(cg)
    ang_c = 2.0 * np.pi * np.outer(kc, kc) / cg
    wc = np.concatenate([np.cos(ang_c), np.sin(ang_c)], axis=1)
    ks = np.arange(s)
    ang_s = 2.0 * np.pi * np.outer(ks, ks) / s
    ds = np.concatenate([np.cos(ang_s), -np.sin(ang_s)], axis=1)
    return jnp.asarray(wc, dtype=BF16), jnp.asarray(ds, dtype=BF16)


def _fnet_call(lay, x, u1, u2, mods, l, wo_bf, mod2, prompt):
    nb, s = (lay.nbp, lay.sp) if prompt else (lay.nbs, lay.ss)
    rb0 = 0 if prompt else lay.tp // s
    d = lay.d
    cg = d // FNET_GROUPS
    wc, ds = _dft_mats(s, cg)
    kern = functools.partial(_fnet_kernel, lay=lay, row_base=rb0 * s, groups=FNET_GROUPS,
                             norm=1.0 / math.sqrt(s * cg))
    blk = pl.BlockSpec((s, d), lambda b: (rb0 + b, 0))
    return pl.pallas_call(
        kern,
        out_shape=(jax.ShapeDtypeStruct(x.shape, F32), jax.ShapeDtypeStruct(u2.shape, BF16)),
        grid=(nb,),
        in_specs=[
            blk,
            blk,
            pl.BlockSpec(memory_space=pl.ANY),
            lay.mod_spec(l, 2),
            pl.BlockSpec((cg, 2 * cg), lambda b: (0, 0)),
            pl.BlockSpec((s, 2 * s), lambda b: (0, 0)),
            pl.BlockSpec((d, d), lambda b: (0, 0)),
        ] + mod2.specs,
        out_specs=(blk, blk),
        scratch_shapes=[pltpu.VMEM((2 * s, d), BF16)],
        input_output_aliases={0: 0, 2: 1},
        compiler_params=_cparams(("parallel",)),
        name="fnet_prompt" if prompt else "fnet_latent",
    )(x, u1, u2, mods, wc, ds, wo_bf, *mod2.args)


def _glu_kernel(u_ref, wa_ref, wg_ref, ba_ref, bg_ref, o_ref):
    u = u_ref[...]
    a = _dot(u, wa_ref[...]) + ba_ref[...]
    g = _dot(u, wg_ref[...]) + bg_ref[...]
    o_ref[...] = a * _sigmoid(g)


def _glu_call(lay, u, w_bf, bias):
    t, d = u.shape
    cd = w_bf.shape[1] // 2
    tm = lay.row_tile(1024)
    tn = 512
    nj = cd // tn
    return pl.pallas_call(
        _glu_kernel,
        out_shape=jax.ShapeDtypeStruct((t, cd), F32),
        grid=(t // tm, nj),
        in_specs=[
            pl.BlockSpec((tm, d), lambda i, j: (i, 0)),
            pl.BlockSpec((d, tn), lambda i, j: (0, j)),
            pl.BlockSpec((d, tn), lambda i, j: (0, nj + j)),
            pl.BlockSpec((1, tn), lambda i, j: (0, j)),
            pl.BlockSpec((1, tn), lambda i, j: (0, nj + j)),
        ],
        out_specs=pl.BlockSpec((tm, tn), lambda i, j: (i, j)),
        compiler_params=_cparams(("parallel", "parallel")),
        name="conv_glu",
    )(u, w_bf, w_bf, bias, bias)


def _conv_kernel(c_ref, p_ref, n_ref, wd_ref, bd_ref, lw_ref, lb_ref, w2_ref, b2_ref, x_ref, gt_ref,
                 nw_ref, sh_ref, sc_ref, o_ref, u2_ref, pad, conv, act, *, lay, rb, width):
    i = pl.program_id(0)
    row0 = i * rb
    grp = lay.group(row0)
    seq = jnp.where(row0 < lay.tp, lay.sp, lay.ss)
    pos = jnp.where(row0 < lay.tp, row0 % lay.sp, (row0 - lay.tp) % lay.ss)
    has_prev = (pos != 0).astype(F32)
    has_next = (pos + rb != seq).astype(F32)
    hl = CONV_HALO
    half = width // 2
    cd = c_ref.shape[1]
    span = pad.shape[1]
    pad[0, 0:hl, :] = p_ref[...] * has_prev
    pad[0, hl:hl + rb, :] = c_ref[...]
    pad[0, hl + rb:hl + rb + hl, :] = n_ref[...] * has_next
    for s in range(1, SUBLANES):
        pad[s, 0:span - SUBLANES, :] = pad[0, s:s + span - SUBLANES, :]
    ngrp = 8
    sub = ngrp * SUBLANES
    lanes = 2 * LANES
    assert rb % sub == 0 and cd % lanes == 0

    def conv_block(blk, carry):
        r0 = pl.multiple_of(blk * sub, sub)
        for c0 in range(0, cd, lanes):
            bias = bd_ref[:, c0:c0 + lanes]
            accs = [jnp.zeros((SUBLANES, lanes), F32) + bias for _ in range(ngrp)]
            for k in sorted(range(width), key=lambda k: ((hl - half + k) % SUBLANES, k)):
                q, s = divmod(hl - half + k, SUBLANES)
                wk = wd_ref[k, :, c0:c0 + lanes]
                for gi in range(ngrp):
                    win = pad[s, pl.ds(r0 + (q + gi) * SUBLANES, SUBLANES), c0:c0 + lanes]
                    accs[gi] = accs[gi] + win * wk
            conv[pl.ds(r0, sub), c0:c0 + lanes] = jnp.concatenate(accs, axis=0)
        return carry

    lax.fori_loop(0, rb // sub, conv_block, 0)
    lw = lw_ref[...]
    lb = lb_ref[...]
    lsub = min(rb, 16 * SUBLANES)

    def ln_block(blk, carry):
        r0 = pl.multiple_of(blk * lsub, lsub)
        acc = conv[pl.ds(r0, lsub), :]
        mu = jnp.mean(acc, axis=-1, keepdims=True)
        cen = acc - mu
        var = jnp.mean(cen * cen, axis=-1, keepdims=True)
        y = cen * lax.rsqrt(var + EPS) * lw + lb
        act[pl.ds(r0, lsub), :] = (y * _sigmoid(y)).astype(BF16)
        return carry

    lax.fori_loop(0, rb // lsub, ln_block, 0)
    xn = x_ref[...] + gt_ref[pl.ds(grp, 1), :] * (_dot(act[...], w2_ref[...]) + b2_ref[...])
    o_ref[...] = xn
    u2_ref[...] = _mod_value(xn, nw_ref, sh_ref, sc_ref, grp).astype(BF16)


def _conv_call(lay, glu, wd, bd, lw, lb, w2_bf, b2, x, mods, l, mod2):
    t, cd = glu.shape
    d = x.shape[1]
    rb = lay.row_tile(256)
    hl = CONV_HALO
    assert CONV_WIDTH // 2 <= hl and rb % hl == 0
    nhb = t // hl
    per = rb // hl
    wd_p = jnp.broadcast_to(wd[:, None, :], (CONV_WIDTH, SUBLANES, cd))
    row = lambda a: a.reshape(1, -1)
    rows = pl.BlockSpec((rb, d), lambda i: (i, 0))
    return pl.pallas_call(
        functools.partial(_conv_kernel, lay=lay, rb=rb, width=CONV_WIDTH),
        out_shape=(jax.ShapeDtypeStruct(x.shape, F32), jax.ShapeDtypeStruct(x.shape, BF16)),
        grid=(t // rb,),
        in_specs=[
            pl.BlockSpec((rb, cd), lambda i: (i, 0)),
            pl.BlockSpec((hl, cd), lambda i: (jnp.maximum(i * per - 1, 0), 0)),
            pl.BlockSpec((hl, cd), lambda i: (jnp.minimum((i + 1) * per, nhb - 1), 0)),
            pl.BlockSpec(wd_p.shape, lambda i: (0, 0, 0)),
            pl.BlockSpec((1, cd), lambda i: (0, 0)),
            pl.BlockSpec((1, cd), lambda i: (0, 0)),
            pl.BlockSpec((1, cd), lambda i: (0, 0)),
            pl.BlockSpec((cd, d), lambda i: (0, 0)),
            pl.BlockSpec((1, d), lambda i: (0, 0)),
            rows,
            lay.mod_spec(l, 2),
        ] + mod2.specs,
        out_specs=(rows, rows),
        scratch_shapes=[pltpu.VMEM((SUBLANES, rb + 2 * hl, cd), F32), pltpu.VMEM((rb, cd), F32),
                        pltpu.VMEM((rb, cd), BF16)],
        compiler_params=_cparams(("parallel",)),
        name="conv_dw_ln_pw2",
    )(glu, glu, glu, wd_p, row(bd), row(lw), row(lb), w2_bf, row(b2), x, mods, *mod2.args)


def _router_kernel(u_ref, wr_ref, upper_ref, ltri_ref, o_ref, cnt_ref, *, ng, ne):
    for j in range(o_ref.shape[0]):
        _route_tile(u_ref, wr_ref, upper_ref, ltri_ref, o_ref, cnt_ref, j, ng, ne)


def _route_tile(u_ref, wr_ref, upper_ref, ltri_ref, o_ref, cnt_ref, j, ng, ne):
    tm = upper_ref.shape[0]
    logits = _dot_nt(wr_ref[...], u_ref[j * tm:(j + 1) * tm, :])
    neg = -jnp.inf
    row = lax.broadcasted_iota(jnp.int32, (SUBLANES, tm), 0)

    gl = jnp.where(row < ng, logits[0:SUBLANES, :], neg)
    gmax = jnp.max(gl, axis=0, keepdims=True)
    gidx = jnp.min(jnp.where(gl == gmax, row, SUBLANES), axis=0, keepdims=True)
    g_p = 1.0 / jnp.sum(jnp.where(row < ng, jnp.exp(gl - gmax), 0.0), axis=0, keepdims=True)

    sel = logits[SUBLANES:2 * SUBLANES, :]
    for g in range(1, ng):
        sel = jnp.where(gidx == g, logits[(1 + g) * SUBLANES:(2 + g) * SUBLANES, :], sel)
    v1 = jnp.max(sel, axis=0, keepdims=True)
    i1 = jnp.min(jnp.where(sel == v1, row, SUBLANES), axis=0, keepdims=True)
    sel2 = jnp.where(row == i1, neg, sel)
    v2 = jnp.max(sel2, axis=0, keepdims=True)
    i2 = jnp.min(jnp.where(sel2 == v2, row, SUBLANES), axis=0, keepdims=True)
    e1 = gidx * SUBLANES + i1
    e2 = gidx * SUBLANES + i2
    tt = jnp.exp(v2 - v1)
    p1 = 1.0 / (1.0 + tt)
    gate1 = p1 * g_p
    gate2 = (tt * p1) * g_p

    rowe = lax.broadcasted_iota(jnp.int32, (ne, tm), 0)
    oh1 = rowe == e1
    oh2 = rowe == e2
    oh = jnp.where(oh1 | oh2, 1.0, 0.0)
    groups = jnp.floor((jnp.sum(oh, axis=1, keepdims=True) + (SUBLANES - 1)) * (1.0 / SUBLANES))
    groups_b = jnp.broadcast_to(groups, (ne, LANES))
    start = SUBLANES * _dot(ltri_ref[...], groups_b.astype(BF16))[:, 0:1]
    prefix = _dot(oh.astype(BF16), upper_ref[...]) + start
    pos1 = jnp.sum(jnp.where(oh1, prefix, 0.0), axis=0, keepdims=True)
    pos2 = jnp.sum(jnp.where(oh2, prefix, 0.0), axis=0, keepdims=True)
    cnt_ref[j] = groups_b

    out = jnp.where(row == 0, e1.astype(F32), 0.0)
    out = jnp.where(row == 1, e2.astype(F32), out)
    out = jnp.where(row == 2, gate1, out)
    out = jnp.where(row == 3, gate2, out)
    out = jnp.where(row == 4, pos1, out)
    out = jnp.where(row == 5, pos2, out)
    o_ref[j] = out


def _router_call(u, wr_t, upper, ltri, ng, ne):
    t, d = u.shape
    tm = upper.shape[0]
    nt = t // tm
    per = 2 if nt % 2 == 0 else 1
    assert ne // ng == SUBLANES and ng <= SUBLANES and SUBLANES + ne <= LANES
    return pl.pallas_call(
        functools.partial(_router_kernel, ng=ng, ne=ne),
        out_shape=(jax.ShapeDtypeStruct((nt, SUBLANES, tm), F32), jax.ShapeDtypeStruct((nt, ne, LANES), F32)),
        grid=(nt // per,),
        in_specs=[
            pl.BlockSpec((per * tm, d), lambda i: (i, 0)),
            pl.BlockSpec((LANES, d), lambda i: (0, 0)),
            pl.BlockSpec((tm, tm), lambda i: (0, 0)),
            pl.BlockSpec((ne, ne), lambda i: (0, 0)),
        ],
        out_specs=(pl.BlockSpec((per, SUBLANES, tm), lambda i: (i, 0, 0)),
                   pl.BlockSpec((per, ne, LANES), lambda i: (i, 0, 0))),
        compiler_params=_cparams(("parallel",)),
        name="moe_router",
    )(u, wr_t, upper, ltri)


def _pack_halves(lo, hi):
    lo_bits = lax.shift_right_logical(pltpu.bitcast(lo, U32), jnp.uint32(16))
    hi_bits = pltpu.bitcast(hi, U32) & jnp.uint32(0xFFFF0000)
    return hi_bits | lo_bits


def _unpack_halves(w):
    lo = pltpu.bitcast(lax.shift_left(w, jnp.uint32(16)), F32)
    hi = pltpu.bitcast(w & jnp.uint32(0xFFFF0000), F32)
    return lo.astype(BF16), hi.astype(BF16)


def _round_bf16(x):
    return x.astype(BF16).astype(F32)


def _group_copy(src, src_g, dst, dst_g, sem):
    g8 = lambda g: pl.ds(pl.multiple_of(g * SUBLANES, SUBLANES), SUBLANES)
    return pltpu.make_async_copy(src.at[g8(src_g), :], dst.at[g8(dst_g), :], sem)


def _for_groups(n, fn, unroll=4):
    def body_many(i, c):
        for j in range(unroll):
            fn(i * unroll + j)
        return c

    def body_one(g, c):
        fn(g)
        return c

    full = lax.div(n, jnp.int32(unroll))
    lax.fori_loop(0, full, body_many, 0)
    lax.fori_loop(full * unroll, n, body_one, 0)


def _dispatch_kernel(gdst_ref, ngt_ref, pad0_ref, npad_ref, tail_ref, u_ref, pos_ref, xs_out, loc, zeros, sem,
                     zsem, *, tm, nl, ne):
    i = pl.program_id(0)
    nt = pl.num_programs(0)
    slot = i % 2
    nlg = nl // SUBLANES

    def copy(step, s, g):
        return _group_copy(loc.at[s], g, xs_out, gdst_ref[step * nlg + g], sem.at[s])

    def group_wait(s):
        _group_copy(loc.at[s], 0, xs_out, 0, sem.at[s]).wait()

    def zero_copy(e, g):
        return _group_copy(zeros, 0, xs_out, pad0_ref[e] + g, zsem)

    def zero_block(t):
        first = pl.multiple_of(tail_ref[0] + t * MOE_BM, MOE_BM)
        return pltpu.make_async_copy(zeros, xs_out.at[pl.ds(first, MOE_BM), :], zsem)

    @pl.when(i == 0)
    def _():
        zeros[...] = jnp.zeros(zeros.shape, zeros.dtype)
        for e in range(ne):
            _for_groups(npad_ref[e], lambda g, e=e: zero_copy(e, g).start())
        _for_groups(tail_ref[1], lambda t: zero_block(t).start())

    @pl.when(i >= 2)
    def _():
        _for_groups(ngt_ref[i - 2], lambda g: group_wait(slot))

    pos1 = pos_ref[4:5, :].astype(jnp.int32)
    pos2 = pos_ref[5:6, :].astype(jnp.int32)
    half = u_ref.shape[1] // 2
    u = u_ref[...]
    rc = 256
    for r in range(0, nl, rc):
        p = lax.broadcasted_iota(jnp.int32, (rc, tm), 0) + r
        onehot = jnp.where((p == pos1) | (p == pos2), 1.0, 0.0).astype(BF16)
        rows = _dot(onehot, u)
        loc[slot, r:r + rc, :] = _pack_halves(rows[:, :half], rows[:, half:])
    _for_groups(ngt_ref[i], lambda g: copy(i, slot, g).start())

    @pl.when(i == nt - 1)
    def _():
        @pl.when(i >= 1)
        def _():
            _for_groups(ngt_ref[i - 1], lambda g: group_wait(1 - slot))

        _for_groups(ngt_ref[i], lambda g: group_wait(slot))
        for e in range(ne):
            _for_groups(npad_ref[e], lambda g: _group_copy(zeros, 0, xs_out, 0, zsem).wait())
        _for_groups(tail_ref[1], lambda t: pltpu.make_async_copy(zeros, xs_out.at[0:MOE_BM, :], zsem).wait())


def _dispatch_call(u, pos_rows, gdst, ngt, pad0, npad, tail, nrows, tm, nl):
    t, d = u.shape
    ne = npad.shape[0]
    grid_spec = pltpu.PrefetchScalarGridSpec(
        num_scalar_prefetch=5,
        grid=(t // tm,),
        in_specs=[
            pl.BlockSpec((tm, d), lambda i, *_: (i, 0)),
            pl.BlockSpec((None, SUBLANES, tm), lambda i, *_: (i, 0, 0)),
        ],
        out_specs=pl.BlockSpec(memory_space=pl.ANY),
        scratch_shapes=[pltpu.VMEM((2, nl, d // 2), U32), pltpu.VMEM((MOE_BM, d // 2), U32),
                        pltpu.SemaphoreType.DMA((2,)), pltpu.SemaphoreType.DMA(())],
    )
    return pl.pallas_call(
        functools.partial(_dispatch_kernel, tm=tm, nl=nl, ne=ne),
        out_shape=jax.ShapeDtypeStruct((nrows, d // 2), U32),
        grid_spec=grid_spec,
        compiler_params=_cparams(("arbitrary",)),
        name="moe_dispatch",
    )(gdst, ngt, pad0, npad, tail, u, pos_rows)


def _expert_kernel(row0_ref, nblk_ref, tail_ref, x_hbm, w13_ref, w2_ref, y_hbm, xbuf, ybuf, w13_bf, w2_bf,
                   xsem, ysem, *, hid):
    e = pl.program_id(0)
    n = nblk_ref[e]
    g0 = row0_ref[e] // MOE_BM
    total = tail_ref[0] // MOE_BM
    nx, ny = xbuf.shape[0], ybuf.shape[0]
    ahead = nx - 1
    half = xbuf.shape[2]

    def rows(g):
        return pl.ds(pl.multiple_of(g * MOE_BM, MOE_BM), MOE_BM)

    def x_copy(g):
        return pltpu.make_async_copy(x_hbm.at[rows(g), :], xbuf.at[g % nx], xsem.at[g % nx])

    def y_copy(g, s):
        return pltpu.make_async_copy(ybuf.at[s], y_hbm.at[rows(g), :], ysem.at[s])

    @pl.when(e == 0)
    def _():
        for g in range(ahead):
            @pl.when(g < total)
            def _(g=g):
                x_copy(g).start()

    @pl.when(n > 0)
    def _():
        w13_bf[...] = w13_ref[...].astype(BF16)
        w2_bf[...] = w2_ref[...].astype(BF16)

        def block(c, carry):
            g = g0 + c
            x_copy(g).wait()

            @pl.when(g + ahead < total)
            def _():
                x_copy(g + ahead).start()

            @pl.when(g >= ny)
            def _():
                y_copy(g - ny, g % ny).wait()

            x_lo, x_hi = _unpack_halves(xbuf[g % nx])
            hb = _dot(x_lo, w13_bf[:half, :]) + _dot(x_hi, w13_bf[half:, :])
            a = hb[:, :hid]
            act = (a * _sigmoid(a)) * hb[:, hid:]
            y = _round_bf16(_dot(act.astype(BF16), w2_bf[...]))
            ybuf[g % ny] = _pack_halves(y[:, :half], y[:, half:])
            y_copy(g, g % ny).start()
            return carry

        lax.fori_loop(0, n, block, 0)

    @pl.when(e == pl.num_programs(0) - 1)
    def _():
        for j in range(ny):
            @pl.when(total - ny + j >= 0)
            def _(j=j):
                g = total - ny + j
                y_copy(g, g % ny).wait()

        ybuf[0] = jnp.zeros(ybuf.shape[1:], ybuf.dtype)
        _for_groups(tail_ref[1], lambda t: y_copy(total + t, 0).start())
        _for_groups(tail_ref[1], lambda t: y_copy(total + t, 0).wait())


def _expert_call(xs, row0, nblk, tail, w13, w2, l):
    r, half = xs.shape
    d = 2 * half
    ne = w13.shape[1]
    hid = w2.shape[2]
    nx, ny = 5, 3
    grid_spec = pltpu.PrefetchScalarGridSpec(
        num_scalar_prefetch=3,
        grid=(ne,),
        in_specs=[
            pl.BlockSpec(memory_space=pl.ANY),
            pl.BlockSpec((None, None, d, 2 * hid), lambda e, *_: (l, e, 0, 0)),
            pl.BlockSpec((None, None, hid, d), lambda e, *_: (l, e, 0, 0)),
        ],
        out_specs=pl.BlockSpec(memory_space=pl.ANY),
        scratch_shapes=[
            pltpu.VMEM((nx, MOE_BM, half), U32), pltpu.VMEM((ny, MOE_BM, half), U32),
            pltpu.VMEM((d, 2 * hid), BF16), pltpu.VMEM((hid, d), BF16),
            pltpu.SemaphoreType.DMA((nx,)), pltpu.SemaphoreType.DMA((ny,)),
        ],
    )
    return pl.pallas_call(
        functools.partial(_expert_kernel, hid=hid),
        out_shape=jax.ShapeDtypeStruct((r, half), U32),
        grid_spec=grid_spec,
        compiler_params=_cparams(("arbitrary",)),
        name="moe_experts",
    )(row0, nblk, tail, xs, w13, w2)


def _combine_kernel(gdst_ref, ngt_ref, x_ref, rt_ref, gt_ref, nw_ref, sh_ref, sc_ref, ys_ref, *rest,
                    lay, tm, nl, final, gated):
    if gated:
        wh_ref, wl_ref, out_a, out_b, out_g, loc, sem = rest
    else:
        out_a, out_b, loc, sem = rest
    i = pl.program_id(0)
    nt = pl.num_programs(0)
    slot = i % 2
    nlg = nl // SUBLANES

    def copy(step, s, g):
        return _group_copy(ys_ref, gdst_ref[step * nlg + g], loc.at[s], g, sem.at[s])

    @pl.when(i == 0)
    def _():
        loc[...] = jnp.zeros(loc.shape, loc.dtype)
        _for_groups(ngt_ref[0], lambda g: copy(0, 0, g).start())

    _for_groups(ngt_ref[i], lambda g: _group_copy(ys_ref, 0, loc.at[slot], 0, sem.at[slot]).wait())

    @pl.when(i + 1 < nt)
    def _():
        _for_groups(ngt_ref[i + 1], lambda g: copy(i + 1, 1 - slot, g).start())

    rt = rt_ref[...]
    gate1, gate2 = rt[:, 2:3], rt[:, 3:4]
    pos1, pos2 = rt[:, 4:5].astype(jnp.int32), rt[:, 5:6].astype(jnp.int32)
    half = loc.shape[2]
    rc = 256
    mix_lo = jnp.zeros((tm, half), F32)
    mix_hi = jnp.zeros((tm, half), F32)
    for r in range(0, nl, rc):
        p = lax.broadcasted_iota(jnp.int32, (tm, rc), 1) + r
        wgt = (jnp.where(p == pos1, gate1, 0.0) + jnp.where(p == pos2, gate2, 0.0)).astype(BF16)
        y_lo, y_hi = _unpack_halves(loc[slot, r:r + rc, :])
        mix_lo = mix_lo + _dot(wgt, y_lo)
        mix_hi = mix_hi + _dot(wgt, y_hi)
    grp = lay.group(i * tm)
    gate = gt_ref[pl.ds(grp, 1), :]
    x_lo = x_ref[:, :half] + gate[:, :half] * mix_lo
    x_hi = x_ref[:, half:] + gate[:, half:] * mix_hi
    ms = (jnp.sum(x_lo * x_lo, axis=-1, keepdims=True) + jnp.sum(x_hi * x_hi, axis=-1, keepdims=True)) / (2 * half)
    inv = lax.rsqrt(ms + EPS)
    nw = nw_ref[...]
    if final:
        y_lo = x_lo * inv * nw[:, :half]
        y_hi = x_hi * inv * nw[:, half:]
        for ref, cond in ((out_a, i * tm < lay.tp), (out_b, i * tm >= lay.tp)):
            @pl.when(cond)
            def _(ref=ref):
                ref[:, :half] = y_lo
                ref[:, half:] = y_hi
    else:
        out_a[:, :half] = x_lo
        out_a[:, half:] = x_hi
        sh = sh_ref[pl.ds(grp, 1), :]
        sc = sc_ref[pl.ds(grp, 1), :]
        u_lo = x_lo * inv * nw[:, :half] * (1.0 + sc[:, :half]) + sh[:, :half]
        u_hi = x_hi * inv * nw[:, half:] * (1.0 + sc[:, half:]) + sh[:, half:]
        if gated:
            (a_hi, a_lo), (b_hi, b_lo) = _split2(u_lo), _split2(u_hi)
            wh, wl = wh_ref[...], wl_ref[...]
            out_g[...] = (_dot_nt(a_hi, wh[:, :half]) + _dot_nt(b_hi, wh[:, half:])
                          + _dot_nt(a_lo, wh[:, :half]) + _dot_nt(b_lo, wh[:, half:])
                          + _dot_nt(a_hi, wl[:, :half]) + _dot_nt(b_hi, wl[:, half:]))
            out_b[:, :half] = a_hi
            out_b[:, half:] = b_hi
        else:
            out_b[:, :half] = u_lo.astype(BF16)
            out_b[:, half:] = u_hi.astype(BF16)


def _combine_call(lay, x, route, mods, l, mod_next, gdst, ngt, ys, tm, nl, final, gate_w=None):
    t, d = x.shape
    rows = pl.BlockSpec((tm, d), lambda i, *_: (i, 0))
    gated = gate_w is not None
    extra_in, extra_args = [], ()
    if final:
        ntp = lay.tp // tm
        out_specs = (pl.BlockSpec((tm, d), lambda i, *_: (jnp.minimum(i, ntp - 1), 0)),
                     pl.BlockSpec((tm, d), lambda i, *_: (jnp.maximum(i - ntp, 0), 0)))
        out_shape = (jax.ShapeDtypeStruct((lay.tp, d), F32), jax.ShapeDtypeStruct((lay.ts, d), F32))
    else:
        out_specs = (rows, rows)
        out_shape = (jax.ShapeDtypeStruct(x.shape, F32), jax.ShapeDtypeStruct(x.shape, BF16))
        if gated:
            wspec = pl.BlockSpec((LANES, d), lambda i, *_: (0, 0))
            extra_in, extra_args = [wspec, wspec], tuple(gate_w)
            out_specs += (pl.BlockSpec((tm, LANES), lambda i, *_: (i, 0)),)
            out_shape += (jax.ShapeDtypeStruct((t, LANES), F32),)
    grid_spec = pltpu.PrefetchScalarGridSpec(
        num_scalar_prefetch=2,
        grid=(t // tm,),
        in_specs=[
            rows,
            pl.BlockSpec((tm, SUBLANES), lambda i, *_: (i, 0)),
            lay.mod_spec(l, 5),
        ] + mod_next.specs + [pl.BlockSpec(memory_space=pl.ANY)] + extra_in,
        out_specs=out_specs,
        scratch_shapes=[pltpu.VMEM((2, nl, d // 2), U32), pltpu.SemaphoreType.DMA((2,))],
    )
    return pl.pallas_call(
        functools.partial(_combine_kernel, lay=lay, tm=tm, nl=nl, final=final, gated=gated),
        out_shape=out_shape,
        grid_spec=grid_spec,
        compiler_params=_cparams(("arbitrary",)),
        name="moe_combine_final" if final else "moe_combine",
    )(gdst, ngt, x, route, mods, *mod_next.args, ys, *extra_args)


def _moe_layer(lay, x, u2, mods, l, mod_next, final, wr_t, upper, ltri, w13, w2, ng, ne, gate_w=None):
    t, d = x.shape
    tm = upper.shape[0]
    nt = t // tm
    bmg = MOE_BM // SUBLANES
    nl = -(-(MOE_TOP_K * tm + ne * (SUBLANES - 1)) // 256) * 256
    nlg = nl // SUBLANES
    route_t, counts = _router_call(u2, wr_t, upper, ltri, ng, ne)
    route = jnp.swapaxes(route_t, 1, 2).reshape(t, SUBLANES)

    c8 = counts[:, :, 0].astype(jnp.int32)
    lend = jnp.cumsum(c8, axis=1)
    lstart = lend - c8
    ngt = lend[:, -1].astype(jnp.int32)
    tot = jnp.sum(c8, axis=0)
    padded = (tot + bmg - 1) // bmg * bmg
    gend = jnp.cumsum(padded)
    gbase = (gend - padded)[None, :] + jnp.cumsum(c8, axis=0) - c8
    nb = -(-(MOE_TOP_K * t + nt * ne * (SUBLANES - 1)) // MOE_BM) + ne
    g = jnp.arange(nlg, dtype=jnp.int32)[None, :, None]
    owner = (g >= lstart[:, None, :]) & (g < lend[:, None, :])
    gdst = g[:, :, 0] + jnp.sum(jnp.where(owner, (gbase - lstart)[:, None, :], 0), axis=-1)
    gdst = gdst.reshape(nt * nlg).astype(jnp.int32)
    row0 = ((gend - padded) * SUBLANES).astype(jnp.int32)
    nblk = (padded // bmg).astype(jnp.int32)
    tail = jnp.stack([gend[-1] * SUBLANES, nb - gend[-1] // bmg]).astype(jnp.int32)

    pad0 = (gend - padded + tot).astype(jnp.int32)
    npad = (padded - tot).astype(jnp.int32)
    xs = _dispatch_call(u2, route_t, gdst, ngt, pad0, npad, tail, nb * MOE_BM, tm, nl)
    ys = _expert_call(xs, row0, nblk, tail, w13, w2, l)
    return _combine_call(lay, x, route, mods, l, mod_next, gdst, ngt, ys, tm, nl, final, gate_w)


def _lower_tri(n, strict):
    r = np.arange(n)
    m = (r[None, :] < r[:, None]) if strict else (r[None, :] <= r[:, None])
    return jnp.asarray(m.astype(np.float32), dtype=BF16)


def kernel(x_prompt, x_sample, state_C, state_n, state_m, c, c_ctx, ada_w, ada_b, norm1_w, norm2_w, m_w_in, m_b_gate, m_head_norm_w, m_w_out, f_w_out, cv_w_pw1, cv_b_pw1, cv_w_dw, cv_b_dw, cv_ln_w, cv_ln_b, cv_w_pw2, cv_b_pw2, r_w_group, r_w_expert, e_w13, e_w2, final_norm_w):
    nbp, sp, d = x_prompt.shape
    nbs, ss, _ = x_sample.shape
    assert ss % GRID_W == 0
    lay = _Layout(nbp, sp, nbs, ss, d)
    depth = ada_w.shape[0]
    nh, dh = state_C.shape[3], state_C.shape[4]
    di = nh * dh
    ng = r_w_group.shape[2]
    ne = r_w_expert.shape[2]
    assert ng == MOE_GROUPS and ng + ne <= LANES and 4 * nh <= LANES and MOE_TOP_K == 2

    cv = jnp.zeros((lay.ngp, d), F32).at[0].set(c_ctx).at[1:1 + nbs].set(c)
    mods = _ada_call(cv, ada_w, ada_b)

    tri_l = _lower_tri(MLSTM_L, strict=False)
    upper = _lower_tri(lay.row_tile(512), strict=True).T
    ltri = _lower_tri(ne, strict=True)
    row = lambda a: a.reshape(1, -1)
    w_in_t = jnp.swapaxes(m_w_in, 1, 2)
    tn = 1024
    assert di % tn == 0
    nkb = di // tn

    def gate_weights(j):
        wg = jnp.zeros((LANES, d), F32).at[:4 * nh].set(w_in_t[j, 4 * di:])
        return _split2(wg)

    x, u1, graw = _prep_call(lay, x_prompt.reshape(lay.tp, d), x_sample.reshape(lay.ts, d),
                             _Mod(lay, mods, norm1_w[0], 0, 0), gate_weights(0))
    y = None
    qvos, kts, gpts = [], [], []
    for l in range(depth):
        j, kind = l // N_MIXERS, l % N_MIXERS
        mod2 = _Mod(lay, mods, norm2_w[l], l, 3)
        if kind == 0:
            bg = jnp.zeros((1, LANES), F32).at[0, :4 * nh].set(m_b_gate[j])
            gp = _gates_call(lay, graw, bg, tri_l, nh)
            gpt = gp[:, :4 * nh].T
            qvo_blocks = list(range(nkb)) + list(range(2 * nkb, 4 * nkb))
            qvo = _proj_call(lay, u1, w_in_t, j, qvo_blocks, tn, transposed=False)
            kt = _proj_call(lay, u1, w_in_t, j, list(range(nkb, 2 * nkb)), tn, transposed=True)
            hw = row(m_head_norm_w[j])
            hg_p = _mlstm_call(lay, qvo, kt, gp, gpt, hw, nh, dh, prompt=True)
            n0 = state_n[:, j].reshape(nbs, 2 * nh, dh)
            m0 = jnp.broadcast_to(state_m[:, j].reshape(nbs, 2 * nh, 1), (nbs, 2 * nh, LANES))
            hg_s = _mlstm_call(lay, qvo, kt, gp, gpt, hw, nh, dh, prompt=False, state=(state_C, j, n0, m0))
            x, u2 = _mm_res_call(lay, hg_p, hg_s, m_w_out[j].astype(BF16), x, mods, l, mod2)
            qvos.append(qvo)
            kts.append(kt)
            gpts.append(gpt)
        elif kind == 1:
            wo = f_w_out[j].astype(BF16)
            u2 = jnp.zeros((lay.t, d), BF16)
            x, u2 = _fnet_call(lay, x, u1, u2, mods, l, wo, mod2, prompt=True)
            x, u2 = _fnet_call(lay, x, u1, u2, mods, l, wo, mod2, prompt=False)
        else:
            glu = _glu_call(lay, u1, cv_w_pw1[j].astype(BF16), row(cv_b_pw1[j]))
            x, u2 = _conv_call(lay, glu, cv_w_dw[j], cv_b_dw[j], cv_ln_w[j], cv_ln_b[j], cv_w_pw2[j].astype(BF16),
                               cv_b_pw2[j], x, mods, l, mod2)
        wr = jnp.zeros((LANES, d), F32).at[:ng].set(r_w_group[l].T).at[SUBLANES:SUBLANES + ne].set(r_w_expert[l].T)
        final = l + 1 == depth
        mod_next = _Mod(lay, mods, final_norm_w, l, 0) if final else _Mod(lay, mods, norm1_w[l + 1], l + 1, 0)
        next_mlstm = not final and (l + 1) % N_MIXERS == 0
        outs = _moe_layer(lay, x, u2, mods, l, mod_next, final, wr.astype(BF16), upper, ltri, e_w13, e_w2, ng, ne,
                          gate_weights((l + 1) // N_MIXERS) if next_mlstm else None)
        if final:
            y = outs
        elif next_mlstm:
            x, u1, graw = outs
        else:
            x, u1 = outs

    y_prompt = y[0].reshape(nbp, sp, d)
    y_sample = y[1].reshape(nbs, ss, d)
    new_c, new_n, new_m = _state_call(lay, qvos, kts, gpts, nh, dh)
    return (y_prompt, y_sample, new_c, new_n, new_m)
```

```python
import functools
import math

import numpy as np
import jax
import jax.numpy as jnp
from jax import lax
from jax.experimental import pallas as pl
from jax.experimental.pallas import tpu as pltpu

F32 = jnp.float32
BF16 = jnp.bfloat16
U32 = jnp.uint32
EPS = 1e-6
GRID_W = 64
N_MIXERS = 3
FNET_GROUPS = 4
CONV_WIDTH = 31
MOE_GROUPS = 4
MOE_TOP_K = 2

LANES = 128
SUBLANES = 8
MLSTM_L = 256
MOE_BM = 256
CONV_HALO = 16
VMEM_LIMIT = 56 * 1024 * 1024


def _cparams(sem, vmem=VMEM_LIMIT):
    return pltpu.CompilerParams(dimension_semantics=sem, vmem_limit_bytes=vmem)


def _dot(a, b):
    return jnp.dot(a, b, preferred_element_type=F32)


def _dot_nt(a, b):
    return lax.dot_general(a, b, (((1,), (1,)), ((), ())), preferred_element_type=F32)


def _rms(x, w):
    return x * lax.rsqrt(jnp.mean(x * x, axis=-1, keepdims=True) + EPS) * w


def _modulate(x, w, shift, scale):
    return _rms(x, w) * (1.0 + scale) + shift


def _sigmoid(x):
    return 1.0 / (1.0 + jnp.exp(-x))


def _log_sigmoid(x):
    return jnp.minimum(x, 0.0) - jnp.log(1.0 + jnp.exp(-jnp.abs(x)))


def _split2(x):
    hi = x.astype(BF16)
    return hi, (x - hi.astype(F32)).astype(BF16)


def _split3(x):
    hi = x.astype(BF16)
    r1 = x - hi.astype(F32)
    mid = r1.astype(BF16)
    return hi, mid, (r1 - mid.astype(F32)).astype(BF16)


class _Layout:
    def __init__(self, nbp, sp, nbs, ss, d):
        self.nbp, self.sp, self.nbs, self.ss, self.d = nbp, sp, nbs, ss, d
        self.tp, self.ts = nbp * sp, nbs * ss
        self.t = self.tp + self.ts
        assert self.tp % ss == 0, "latent sequences must start on a block boundary of their own length"
        self.ngp = -(-(1 + nbs) // SUBLANES) * SUBLANES

    def group(self, row0):
        return jnp.where(row0 < self.tp, 0, 1 + (row0 - self.tp) // self.ss)

    def row_tile(self, want):
        tm = math.gcd(math.gcd(self.tp, self.ss), want)
        assert tm % SUBLANES == 0
        return tm

    def mod_spec(self, l, chunk):
        return pl.BlockSpec((None, self.ngp, self.d), lambda *_: (l, 0, chunk))

    def row_spec(self):
        return pl.BlockSpec((1, self.d), lambda *_: (0, 0))


class _Mod:
    def __init__(self, lay, mods, nw, l, c_shift):
        self.args = (nw.reshape(1, -1), mods, mods)
        self.specs = [lay.row_spec(), lay.mod_spec(l, c_shift), lay.mod_spec(l, c_shift + 1)]


def _mod_value(x, nw_ref, sh_ref, sc_ref, grp):
    return _modulate(x, nw_ref[...], sh_ref[pl.ds(grp, 1), :], sc_ref[pl.ds(grp, 1), :])


def _ada_kernel(cv_ref, w_ref, b_ref, o_ref):
    s = cv_ref[...]
    s = s * _sigmoid(s)
    o_ref[...] = _dot(s.astype(BF16), w_ref[...].astype(BF16)) + b_ref[...]


def _ada_call(cv, ada_w, ada_b):
    depth, d, n = ada_w.shape
    ngp = cv.shape[0]
    tn = min(n, 2048)
    return pl.pallas_call(
        _ada_kernel,
        out_shape=jax.ShapeDtypeStruct((depth, ngp, n), F32),
        grid=(depth, n // tn),
        in_specs=[
            pl.BlockSpec((ngp, d), lambda l, j: (0, 0)),
            pl.BlockSpec((None, d, tn), lambda l, j: (l, 0, j)),
            pl.BlockSpec((None, 1, tn), lambda l, j: (l, 0, j)),
        ],
        out_specs=pl.BlockSpec((None, ngp, tn), lambda l, j: (l, 0, j)),
        compiler_params=_cparams(("parallel", "parallel")),
        name="ada_mods",
    )(cv, ada_w, ada_b.reshape(depth, 1, n))


def _gate_raw(u, wh, wl):
    u_hi, u_lo = _split2(u)
    return _dot_nt(u_hi, wh) + _dot_nt(u_lo, wh) + _dot_nt(u_hi, wl), u_hi


def _prep_kernel(xp_ref, xs_ref, nw_ref, sh_ref, sc_ref, wh_ref, wl_ref, x_ref, u_ref, g_ref, *, lay, tm):
    i = pl.program_id(0)
    grp = lay.group(i * tm)
    for src, cond in ((xp_ref, i * tm < lay.tp), (xs_ref, i * tm >= lay.tp)):
        @pl.when(cond)
        def _(src=src):
            rc = min(tm, 256)
            for r in range(0, tm, rc):
                x = src[r:r + rc, :]
                x_ref[r:r + rc, :] = x
                g, u_hi = _gate_raw(_mod_value(x, nw_ref, sh_ref, sc_ref, grp), wh_ref[...], wl_ref[...])
                u_ref[r:r + rc, :] = u_hi
                g_ref[r:r + rc, :] = g


def _prep_call(lay, xp, xs, mod, gate_w):
    d = lay.d
    tm = lay.row_tile(1024)
    ntp = lay.tp // tm
    rows = pl.BlockSpec((tm, d), lambda i: (i, 0))
    wspec = pl.BlockSpec((LANES, d), lambda i: (0, 0))
    return pl.pallas_call(
        functools.partial(_prep_kernel, lay=lay, tm=tm),
        out_shape=(jax.ShapeDtypeStruct((lay.t, d), F32), jax.ShapeDtypeStruct((lay.t, d), BF16),
                   jax.ShapeDtypeStruct((lay.t, LANES), F32)),
        grid=(lay.t // tm,),
        in_specs=[
            pl.BlockSpec((tm, d), lambda i: (jnp.minimum(i, ntp - 1), 0)),
            pl.BlockSpec((tm, d), lambda i: (jnp.maximum(i - ntp, 0), 0)),
        ] + mod.specs + [wspec, wspec],
        out_specs=(rows, rows, pl.BlockSpec((tm, LANES), lambda i: (i, 0))),
        compiler_params=_cparams(("arbitrary",)),
        name="prep_modulate",
    )(xp, xs, *mod.args, *gate_w)


def _gates_kernel(g_ref, b_ref, tri_ref, o_ref, *, nh, tm):
    l = tri_ref.shape[0]
    tri = tri_ref[...]
    lane = lax.broadcasted_iota(jnp.int32, (l, LANES), 1)
    is_f = ((lane >= nh) & (lane < 2 * nh)) | ((lane >= 3 * nh) & (lane < 4 * nh))
    is_a = (lane < nh) | ((lane >= 2 * nh) & (lane < 3 * nh))
    for r in range(0, tm, l):
        g = g_ref[r:r + l, :] + b_ref[...]
        lf = jnp.where(is_f, _log_sigmoid(g), 0.0)
        hi, mid, lo = _split3(lf)
        prefix = _dot(tri, hi) + _dot(tri, mid) + _dot(tri, lo)
        suffix = jnp.sum(lf, axis=0, keepdims=True) - prefix + lf
        b = jnp.where(lane < 2 * nh, prefix, suffix)
        a = g - pltpu.roll(b, LANES - nh, 1)
        o_ref[r:r + l, :] = jnp.where(is_a, a, b)


def _gates_call(lay, graw, bias, tri, nh):
    t = graw.shape[0]
    l = tri.shape[0]
    tm = lay.row_tile(2048)
    assert tm % l == 0
    return pl.pallas_call(
        functools.partial(_gates_kernel, nh=nh, tm=tm),
        out_shape=jax.ShapeDtypeStruct((t, LANES), F32),
        grid=(t // tm,),
        in_specs=[
            pl.BlockSpec((tm, LANES), lambda i: (i, 0)),
            pl.BlockSpec((1, LANES), lambda i: (0, 0)),
            pl.BlockSpec((l, l), lambda i: (0, 0)),
        ],
        out_specs=pl.BlockSpec((tm, LANES), lambda i: (i, 0)),
        compiler_params=_cparams(("parallel",)),
        name="mlstm_gates",
    )(graw, bias, tri)


def _proj_kernel(u_ref, w_ref, o_ref, w_bf, *, transposed):
    @pl.when(pl.program_id(1) == 0)
    def _():
        w_bf[...] = w_ref[...].astype(BF16)

    if transposed:
        o_ref[...] = _dot_nt(w_bf[...], u_ref[...]).astype(BF16)
    else:
        o_ref[...] = _dot_nt(u_ref[...], w_bf[...]).astype(BF16)


def _proj_call(lay, u, w_in_t, jl, blocks, tn, transposed):
    t, d = u.shape
    tm = 2048 if t % 2048 == 0 else lay.row_tile(1024)
    nb = len(blocks)
    first, gap_at, gap = blocks[0], None, 0
    for idx in range(1, nb):
        if blocks[idx] != blocks[idx - 1] + 1:
            assert gap_at is None
            gap_at, gap = idx, blocks[idx] - blocks[idx - 1] - 1
    wblk = (lambda j: first + j) if gap_at is None else (lambda j: first + j + jnp.where(j >= gap_at, gap, 0))
    if transposed:
        out_shape = jax.ShapeDtypeStruct((nb * tn, t), BF16)
        out_spec = pl.BlockSpec((tn, tm), lambda j, i: (j, i))
    else:
        out_shape = jax.ShapeDtypeStruct((t, nb * tn), BF16)
        out_spec = pl.BlockSpec((tm, tn), lambda j, i: (i, j))
    return pl.pallas_call(
        functools.partial(_proj_kernel, transposed=transposed),
        out_shape=out_shape,
        grid=(nb, t // tm),
        in_specs=[
            pl.BlockSpec((tm, d), lambda j, i: (i, 0)),
            pl.BlockSpec((None, tn, d), lambda j, i: (jl, wblk(j), 0)),
        ],
        out_specs=out_spec,
        scratch_shapes=[pltpu.VMEM((tn, d), BF16)],
        compiler_params=_cparams(("parallel", "arbitrary")),
        name="mlstm_proj_t" if transposed else "mlstm_proj",
    )(u, w_in_t)


def _col(tile, c):
    lane = lax.broadcasted_iota(jnp.int32, tile.shape, 1)
    return jnp.sum(jnp.where(lane == c, tile, 0.0), axis=-1, keepdims=True)


def _dir_masks(l):
    r = lax.broadcasted_iota(jnp.int32, (l, l), 0)
    c = lax.broadcasted_iota(jnp.int32, (l, l), 1)
    return c <= r, c >= r


def _head_epilogue(h, hw, o):
    hn = h * lax.rsqrt(jnp.mean(h * h, axis=-1, keepdims=True) + EPS) * hw
    return (hn * _sigmoid(o.astype(F32))).astype(BF16)


def _row_times_kt(w_row, kt):
    hi, lo = _split2(w_row)
    sub = lax.broadcasted_iota(jnp.int32, (SUBLANES, w_row.shape[1]), 0)
    stacked = jnp.where(sub == 0, hi.astype(F32), jnp.where(sub == 1, lo.astype(F32), 0.0)).astype(BF16)
    res = _dot_nt(stacked, kt)
    return res[0:1, :] + res[1:2, :]


def _mlstm_single_kernel(q_ref, kt_ref, v_ref, o_ref, gp_ref, gpt_ref, hw_ref, out_ref, *, nh, dh, scale):
    l = q_ref.shape[0]
    gp = gp_ref[...]
    masks = _dir_masks(l)
    for h in range(nh):
        cols = slice(h * dh, (h + 1) * dh)
        qk = _dot(q_ref[:, cols], kt_ref[cols, :])
        p = None
        for d in range(2):
            a_r = gpt_ref[2 * nh * d + h:2 * nh * d + h + 1, :]
            b_c = gp[:, 2 * nh * d + nh + h:2 * nh * d + nh + h + 1]
            g = jnp.where(masks[d], a_r, -jnp.inf)
            m = jnp.maximum(jnp.max(g, axis=-1, keepdims=True), 0.0)
            s = qk * jnp.exp(g - m)
            den = scale * jnp.sum(s, axis=-1, keepdims=True)
            inv = scale / jnp.maximum(jnp.abs(den), jnp.exp(-(b_c + m)))
            p = s * inv if p is None else p + s * inv
        hh = _dot(p.astype(BF16), v_ref[:, cols])
        out_ref[:, cols] = _head_epilogue(hh, hw_ref[:, cols], o_ref[:, cols])


def _mlstm_multi_kernel(q_ref, kt_ref, v_ref, o_ref, gp_ref, gpt_ref, hw_ref, c0_ref, n0_ref, m0_ref,
                        out_ref, cst, cbf, *, nh, nc, l, scale):
    h = pl.program_id(1)
    masks = _dir_masks(l)
    m_in = [[None] * nc for _ in range(2)]
    n_in = [[None] * nc for _ in range(2)]
    for d in range(2):
        cst[...] = c0_ref[d]
        n = n0_ref[pl.ds(d * nh + h, 1), :]
        m = m0_ref[pl.ds(d * nh + h, 1), 0:1]
        order = list(range(nc)) if d == 0 else list(range(nc - 1, -1, -1))
        for step, c in enumerate(order):
            m_in[d][c], n_in[d][c] = m, n
            cbf[d, c] = cst[...].astype(BF16)
            if step + 1 < nc:
                r0 = c * l
                a_r = gpt_ref[pl.ds(2 * nh * d + h, 1), r0:r0 + l]
                b_r = gpt_ref[pl.ds(2 * nh * d + nh + h, 1), r0:r0 + l]
                m_last = jnp.maximum(jnp.max(a_r, axis=-1, keepdims=True), m)
                b_end = b_r[:, l - 1:l] if d == 0 else b_r[:, 0:1]
                decay = jnp.exp(m - m_last)
                w_end = jnp.exp(a_r - m_last)
                kt = kt_ref[:, r0:r0 + l]
                cst[...] = decay * cst[...] + _dot((kt.astype(F32) * w_end).astype(BF16), v_ref[r0:r0 + l, :])
                n = decay * n + _row_times_kt(w_end, kt)
                m = b_end + m_last
    hw = hw_ref[...]
    for c in range(nc):
        r0 = c * l
        q = q_ref[r0:r0 + l, :]
        v = v_ref[r0:r0 + l, :]
        qk = _dot(q, kt_ref[:, r0:r0 + l])
        qf = q.astype(F32)
        gp = gp_ref[r0:r0 + l, :]
        p = inter = None
        for d in range(2):
            m, n = m_in[d][c], n_in[d][c]
            a_r = gpt_ref[pl.ds(2 * nh * d + h, 1), r0:r0 + l]
            b_c = _col(gp, 2 * nh * d + nh + h)
            g = jnp.where(masks[d], a_r, -jnp.inf)
            mt = jnp.maximum(jnp.max(g, axis=-1, keepdims=True), m)
            s = qk * jnp.exp(g - mt)
            w_prev = jnp.exp(m - mt)
            den = scale * (jnp.sum(s, axis=-1, keepdims=True) + w_prev * jnp.sum(qf * n, axis=-1, keepdims=True))
            inv = scale / jnp.maximum(jnp.abs(den), jnp.exp(-(b_c + mt)))
            term = (w_prev * inv) * _dot(q, cbf[d, c])
            p = s * inv if p is None else p + s * inv
            inter = term if inter is None else inter + term
        hh = _dot(p.astype(BF16), v) + inter
        out_ref[r0:r0 + l, :] = _head_epilogue(hh, hw, o_ref[r0:r0 + l, :])


def _mlstm_call(lay, qvo, kt, gp, gpt, hw, nh, dh, prompt, state=None):
    nb, s = (lay.nbp, lay.sp) if prompt else (lay.nbs, lay.ss)
    rb0 = 0 if prompt else lay.tp // s
    scale = dh ** -0.5
    di = nh * dh
    common_in = [
        pl.BlockSpec((s, dh), lambda b, h: (rb0 + b, h)),
        pl.BlockSpec((dh, s), lambda b, h: (h, rb0 + b)),
        pl.BlockSpec((s, dh), lambda b, h: (rb0 + b, nh + h)),
        pl.BlockSpec((s, dh), lambda b, h: (rb0 + b, 2 * nh + h)),
        pl.BlockSpec((s, LANES), lambda b, h: (rb0 + b, 0)),
        pl.BlockSpec((4 * nh, s), lambda b, h: (0, rb0 + b)),
        pl.BlockSpec((1, dh), lambda b, h: (0, h)),
    ]
    if prompt:
        assert s == MLSTM_L
        return pl.pallas_call(
            functools.partial(_mlstm_single_kernel, nh=nh, dh=dh, scale=scale),
            out_shape=jax.ShapeDtypeStruct((nb * s, di), BF16),
            grid=(nb,),
            in_specs=[
                pl.BlockSpec((s, di), lambda b: (rb0 + b, 0)),
                pl.BlockSpec((di, s), lambda b: (0, rb0 + b)),
                pl.BlockSpec((s, di), lambda b: (rb0 + b, 1)),
                pl.BlockSpec((s, di), lambda b: (rb0 + b, 2)),
                pl.BlockSpec((s, LANES), lambda b: (rb0 + b, 0)),
                pl.BlockSpec((4 * nh, s), lambda b: (0, rb0 + b)),
                pl.BlockSpec((1, di), lambda b: (0, 0)),
            ],
            out_specs=pl.BlockSpec((s, di), lambda b: (b, 0)),
            compiler_params=_cparams(("parallel",)),
            name="mlstm_prompt",
        )(qvo, kt, qvo, qvo, gp, gpt, hw)
    state_c, jl, n0, m0 = state
    nc = s // MLSTM_L
    return pl.pallas_call(
        functools.partial(_mlstm_multi_kernel, nh=nh, nc=nc, l=MLSTM_L, scale=scale),
        out_shape=jax.ShapeDtypeStruct((nb * s, di), BF16),
        grid=(nb, nh),
        in_specs=common_in + [
            pl.BlockSpec((None, None, 2, None, dh, dh), lambda b, h: (b, jl, 0, h, 0, 0)),
            pl.BlockSpec((None, 2 * nh, dh), lambda b, h: (b, 0, 0)),
            pl.BlockSpec((None, 2 * nh, LANES), lambda b, h: (b, 0, 0)),
        ],
        out_specs=pl.BlockSpec((s, dh), lambda b, h: (b, h)),
        scratch_shapes=[pltpu.VMEM((dh, dh), F32), pltpu.VMEM((2, nc, dh, dh), BF16)],
        compiler_params=_cparams(("parallel", "parallel")),
        name="mlstm_latent",
    )(qvo, kt, qvo, qvo, gp, gpt, hw, state_c, n0, m0)


def _state_kernel(*refs, nl, nh, dh):
    ins, (c_ref, n_ref, m_ref) = refs[:3 * nl], refs[3 * nl:]
    lyr = pl.program_id(0)
    for jl in range(nl):
        kt_ref, v_ref, gpt_ref = ins[3 * jl:3 * jl + 3]

        @pl.when(lyr == jl)
        def _(kt_ref=kt_ref, v_ref=v_ref, gpt_ref=gpt_ref):
            l = v_ref.shape[0]
            sub = lax.broadcasted_iota(jnp.int32, m_ref.shape, 0)
            lane = lax.broadcasted_iota(jnp.int32, m_ref.shape, 1)
            m_all = jnp.zeros(m_ref.shape, F32)
            for h in range(nh):
                kt = kt_ref[h * dh:(h + 1) * dh, :]
                ktf = kt.astype(F32)
                v = v_ref[:, h * dh:(h + 1) * dh]
                for d in range(2):
                    a_r = gpt_ref[2 * nh * d + h:2 * nh * d + h + 1, :]
                    b_r = gpt_ref[2 * nh * d + nh + h:2 * nh * d + nh + h + 1, :]
                    m_last = jnp.maximum(jnp.max(a_r, axis=-1, keepdims=True), 0.0)
                    b_end = b_r[:, l - 1:l] if d == 0 else b_r[:, 0:1]
                    w_end = jnp.exp(a_r - m_last)
                    c_ref[d, h] = _dot((ktf * w_end).astype(BF16), v)
                    n_ref[d, h:h + 1, :] = _row_times_kt(w_end, kt)
                    m_all = jnp.where((sub == d) & (lane == h), b_end + m_last, m_all)
            m_ref[...] = m_all


def _state_call(lay, qvos, kts, gpts, nh, dh):
    nl = len(qvos)
    nbp, s = lay.nbp, lay.sp
    di = nh * dh
    assert s == MLSTM_L

    def pick(jl):
        return lambda lyr, b: jnp.where(lyr == jl, b, jnp.where(lyr < jl, 0, nbp - 1))

    in_specs, args = [], []
    for jl in range(nl):
        pb = pick(jl)
        in_specs.append(pl.BlockSpec((di, s), lambda lyr, b, pb=pb: (0, pb(lyr, b))))
        in_specs.append(pl.BlockSpec((s, di), lambda lyr, b, pb=pb: (pb(lyr, b), 1)))
        in_specs.append(pl.BlockSpec((4 * nh, s), lambda lyr, b, pb=pb: (0, pb(lyr, b))))
        args += [kts[jl], qvos[jl], gpts[jl]]
    return pl.pallas_call(
        functools.partial(_state_kernel, nl=nl, nh=nh, dh=dh),
        out_shape=(
            jax.ShapeDtypeStruct((nbp, nl, 2, nh, dh, dh), F32),
            jax.ShapeDtypeStruct((nbp, nl, 2, nh, dh), F32),
            jax.ShapeDtypeStruct((nbp, nl, 2, nh), F32),
        ),
        grid=(nl, nbp),
        in_specs=in_specs,
        out_specs=(
            pl.BlockSpec((None, None, 2, nh, dh, dh), lambda lyr, b: (b, lyr, 0, 0, 0, 0)),
            pl.BlockSpec((None, None, 2, nh, dh), lambda lyr, b: (b, lyr, 0, 0, 0)),
            pl.BlockSpec((None, None, 2, nh), lambda lyr, b: (b, lyr, 0, 0)),
        ),
        compiler_params=_cparams(("arbitrary", "arbitrary")),
        name="mlstm_prompt_state",
    )(*args)


def _mm_res_kernel(ap_ref, as_ref, w_ref, x_ref, g_ref, nw_ref, sh_ref, sc_ref, o_ref, u_ref, *, lay, tm):
    i = pl.program_id(0)
    grp = lay.group(i * tm)
    gate = g_ref[pl.ds(grp, 1), :]
    w = w_ref[...]
    rc = min(tm, 256)
    for a_ref, cond in ((ap_ref, i * tm < lay.tp), (as_ref, i * tm >= lay.tp)):
        @pl.when(cond)
        def _(a_ref=a_ref):
            for r in range(0, tm, rc):
                xn = x_ref[r:r + rc, :] + gate * _dot(a_ref[r:r + rc, :], w)
                o_ref[r:r + rc, :] = xn
                u_ref[r:r + rc, :] = _mod_value(xn, nw_ref, sh_ref, sc_ref, grp).astype(BF16)


def _mm_res_call(lay, a_p, a_s, w_bf, x, mods, l, mod2):
    kdim = a_p.shape[1]
    t, d = x.shape
    tm = lay.row_tile(512)
    ntp = lay.tp // tm
    return pl.pallas_call(
        functools.partial(_mm_res_kernel, lay=lay, tm=tm),
        out_shape=(jax.ShapeDtypeStruct(x.shape, F32), jax.ShapeDtypeStruct(x.shape, BF16)),
        grid=(t // tm,),
        in_specs=[
            pl.BlockSpec((tm, kdim), lambda i: (jnp.minimum(i, ntp - 1), 0)),
            pl.BlockSpec((tm, kdim), lambda i: (jnp.maximum(i - ntp, 0), 0)),
            pl.BlockSpec((kdim, d), lambda i: (0, 0)),
            pl.BlockSpec((tm, d), lambda i: (i, 0)),
            lay.mod_spec(l, 2),
        ] + mod2.specs,
        out_specs=(pl.BlockSpec((tm, d), lambda i: (i, 0)), pl.BlockSpec((tm, d), lambda i: (i, 0))),
        compiler_params=_cparams(("arbitrary",)),
        name="mm_residual",
    )(a_p, a_s, w_bf, x, mods, *mod2.args)


def _fnet_kernel(x_ref, u_ref, u2_any, gt_ref, wc_ref, ds_ref, wo_ref, nw_ref, sh_ref, sc_ref, o_ref, u2_ref,
                 ab_scr, *, lay, row_base, groups, norm):
    s, d = x_ref.shape
    cg = d // groups
    grp = lay.group(row_base + pl.program_id(0) * s)
    wc = wc_ref[...]
    rc = min(s, 256)
    for g in range(groups):
        for r in range(0, s, rc):
            ab = _dot(u_ref[r:r + rc, g * cg:(g + 1) * cg], wc)
            ab_scr[r:r + rc, g * cg:(g + 1) * cg] = ab[:, :cg].astype(BF16)
            ab_scr[s + r:s + r + rc, g * cg:(g + 1) * cg] = ab[:, cg:].astype(BF16)
    gate = gt_ref[pl.ds(grp, 1), :]
    wo = wo_ref[...]
    for r in range(0, s, rc):
        y = _dot(ds_ref[r:r + rc, :], ab_scr[...]) * norm
        xn = x_ref[r:r + rc, :] + gate * _dot(y.astype(BF16), wo)
        o_ref[r:r + rc, :] = xn
        u2_ref[r:r + rc, :] = _mod_value(xn, nw_ref, sh_ref, sc_ref, grp).astype(BF16)


def _dft_mats(s, cg):
    kc = np.arange(cg)
    ang_c = 2.0 * np.pi * np.outer(kc, kc) / cg
    wc = np.concatenate([np.cos(ang_c), np.sin(ang_c)], axis=1)
    ks = np.arange(s)
    ang_s = 2.0 * np.pi * np.outer(ks, ks) / s
    ds = np.concatenate([np.cos(ang_s), -np.sin(ang_s)], axis=1)
    return jnp.asarray(wc, dtype=BF16), jnp.asarray(ds, dtype=BF16)


def _fnet_call(lay, x, u1, u2, mods, l, wo_bf, mod2, prompt):
    nb, s = (lay.nbp, lay.sp) if prompt else (lay.nbs, lay.ss)
    rb0 = 0 if prompt else lay.tp // s
    d = lay.d
    cg = d // FNET_GROUPS
    wc, ds = _dft_mats(s, cg)
    kern = functools.partial(_fnet_kernel, lay=lay, row_base=rb0 * s, groups=FNET_GROUPS,
                             norm=1.0 / math.sqrt(s * cg))
    blk = pl.BlockSpec((s, d), lambda b: (rb0 + b, 0))
    return pl.pallas_call(
        kern,
        out_shape=(jax.ShapeDtypeStruct(x.shape, F32), jax.ShapeDtypeStruct(u2.shape, BF16)),
        grid=(nb,),
        in_specs=[
            blk,
            blk,
            pl.BlockSpec(memory_space=pl.ANY),
            lay.mod_spec(l, 2),
            pl.BlockSpec((cg, 2 * cg), lambda b: (0, 0)),
            pl.BlockSpec((s, 2 * s), lambda b: (0, 0)),
            pl.BlockSpec((d, d), lambda b: (0, 0)),
        ] + mod2.specs,
        out_specs=(blk, blk),
        scratch_shapes=[pltpu.VMEM((2 * s, d), BF16)],
        input_output_aliases={0: 0, 2: 1},
        compiler_params=_cparams(("parallel",)),
        name="fnet_prompt" if prompt else "fnet_latent",
    )(x, u1, u2, mods, wc, ds, wo_bf, *mod2.args)


def _glu_kernel(u_ref, wa_ref, wg_ref, ba_ref, bg_ref, o_ref):
    u = u_ref[...]
    a = _dot(u, wa_ref[...]) + ba_ref[...]
    g = _dot(u, wg_ref[...]) + bg_ref[...]
    o_ref[...] = a * _sigmoid(g)


def _glu_call(lay, u, w_bf, bias):
    t, d = u.shape
    cd = w_bf.shape[1] // 2
    tm = lay.row_tile(1024)
    tn = 512
    nj = cd // tn
    return pl.pallas_call(
        _glu_kernel,
        out_shape=jax.ShapeDtypeStruct((t, cd), F32),
        grid=(t // tm, nj),
        in_specs=[
            pl.BlockSpec((tm, d), lambda i, j: (i, 0)),
            pl.BlockSpec((d, tn), lambda i, j: (0, j)),
            pl.BlockSpec((d, tn), lambda i, j: (0, nj + j)),
            pl.BlockSpec((1, tn), lambda i, j: (0, j)),
            pl.BlockSpec((1, tn), lambda i, j: (0, nj + j)),
        ],
        out_specs=pl.BlockSpec((tm, tn), lambda i, j: (i, j)),
        compiler_params=_cparams(("parallel", "parallel")),
        name="conv_glu",
    )(u, w_bf, w_bf, bias, bias)


def _conv_kernel(c_ref, p_ref, n_ref, wd_ref, bd_ref, lw_ref, lb_ref, w2_ref, b2_ref, x_ref, gt_ref,
                 nw_ref, sh_ref, sc_ref, o_ref, u2_ref, pad, conv, act, *, lay, rb, width):
    i = pl.program_id(0)
    row0 = i * rb
    grp = lay.group(row0)
    seq = jnp.where(row0 < lay.tp, lay.sp, lay.ss)
    pos = jnp.where(row0 < lay.tp, row0 % lay.sp, (row0 - lay.tp) % lay.ss)
    has_prev = (pos != 0).astype(F32)
    has_next = (pos + rb != seq).astype(F32)
    hl = CONV_HALO
    half = width // 2
    cd = c_ref.shape[1]
    span = pad.shape[1]
    pad[0, 0:hl, :] = p_ref[...] * has_prev
    pad[0, hl:hl + rb, :] = c_ref[...]
    pad[0, hl + rb:hl + rb + hl, :] = n_ref[...] * has_next
    for s in range(1, SUBLANES):
        pad[s, 0:span - SUBLANES, :] = pad[0, s:s + span - SUBLANES, :]
    ngrp = 8
    sub = ngrp * SUBLANES
    lanes = 2 * LANES
    assert rb % sub == 0 and cd % lanes == 0

    def conv_block(blk, carry):
        r0 = pl.multiple_of(blk * sub, sub)
        for c0 in range(0, cd, lanes):
            bias = bd_ref[:, c0:c0 + lanes]
            accs = [jnp.zeros((SUBLANES, lanes), F32) + bias for _ in range(ngrp)]
            for k in sorted(range(width), key=lambda k: ((hl - half + k) % SUBLANES, k)):
                q, s = divmod(hl - half + k, SUBLANES)
                wk = wd_ref[k, :, c0:c0 + lanes]
                for gi in range(ngrp):
                    win = pad[s, pl.ds(r0 + (q + gi) * SUBLANES, SUBLANES), c0:c0 + lanes]
                    accs[gi] = accs[gi] + win * wk
            conv[pl.ds(r0, sub), c0:c0 + lanes] = jnp.concatenate(accs, axis=0)
        return carry

    lax.fori_loop(0, rb // sub, conv_block, 0)
    lw = lw_ref[...]
    lb = lb_ref[...]
    lsub = min(rb, 16 * SUBLANES)

    def ln_block(blk, carry):
        r0 = pl.multiple_of(blk * lsub, lsub)
        acc = conv[pl.ds(r0, lsub), :]
        mu = jnp.mean(acc, axis=-1, keepdims=True)
        cen = acc - mu
        var = jnp.mean(cen * cen, axis=-1, keepdims=True)
        y = cen * lax.rsqrt(var + EPS) * lw + lb
        act[pl.ds(r0, lsub), :] = (y * _sigmoid(y)).astype(BF16)
        return carry

    lax.fori_loop(0, rb // lsub, ln_block, 0)
    xn = x_ref[...] + gt_ref[pl.ds(grp, 1), :] * (_dot(act[...], w2_ref[...]) + b2_ref[...])
    o_ref[...] = xn
    u2_ref[...] = _mod_value(xn, nw_ref, sh_ref, sc_ref, grp).astype(BF16)


def _conv_call(lay, glu, wd, bd, lw, lb, w2_bf, b2, x, mods, l, mod2):
    t, cd = glu.shape
    d = x.shape[1]
    rb = lay.row_tile(256)
    hl = CONV_HALO
    assert CONV_WIDTH // 2 <= hl and rb % hl == 0
    nhb = t // hl
    per = rb // hl
    wd_p = jnp.broadcast_to(wd[:, None, :], (CONV_WIDTH, SUBLANES, cd))
    row = lambda a: a.reshape(1, -1)
    rows = pl.BlockSpec((rb, d), lambda i: (i, 0))
    return pl.pallas_call(
        functools.partial(_conv_kernel, lay=lay, rb=rb, width=CONV_WIDTH),
        out_shape=(jax.ShapeDtypeStruct(x.shape, F32), jax.ShapeDtypeStruct(x.shape, BF16)),
        grid=(t // rb,),
        in_specs=[
            pl.BlockSpec((rb, cd), lambda i: (i, 0)),
            pl.BlockSpec((hl, cd), lambda i: (jnp.maximum(i * per - 1, 0), 0)),
            pl.BlockSpec((hl, cd), lambda i: (jnp.minimum((i + 1) * per, nhb - 1), 0)),
            pl.BlockSpec(wd_p.shape, lambda i: (0, 0, 0)),
            pl.BlockSpec((1, cd), lambda i: (0, 0)),
            pl.BlockSpec((1, cd), lambda i: (0, 0)),
            pl.BlockSpec((1, cd), lambda i: (0, 0)),
            pl.BlockSpec((cd, d), lambda i: (0, 0)),
            pl.BlockSpec((1, d), lambda i: (0, 0)),
            rows,
            lay.mod_spec(l, 2),
        ] + mod2.specs,
        out_specs=(rows, rows),
        scratch_shapes=[pltpu.VMEM((SUBLANES, rb + 2 * hl, cd), F32), pltpu.VMEM((rb, cd), F32),
                        pltpu.VMEM((rb, cd), BF16)],
        compiler_params=_cparams(("parallel",)),
        name="conv_dw_ln_pw2",
    )(glu, glu, glu, wd_p, row(bd), row(lw), row(lb), w2_bf, row(b2), x, mods, *mod2.args)


def _router_kernel(u_ref, wr_ref, upper_ref, ltri_ref, o_ref, cnt_ref, *, ng, ne):
    for j in range(o_ref.shape[0]):
        _route_tile(u_ref, wr_ref, upper_ref, ltri_ref, o_ref, cnt_ref, j, ng, ne)


def _route_tile(u_ref, wr_ref, upper_ref, ltri_ref, o_ref, cnt_ref, j, ng, ne):
    tm = upper_ref.shape[0]
    logits = _dot_nt(wr_ref[...], u_ref[j * tm:(j + 1) * tm, :])
    neg = -jnp.inf
    row = lax.broadcasted_iota(jnp.int32, (SUBLANES, tm), 0)

    gl = jnp.where(row < ng, logits[0:SUBLANES, :], neg)
    gmax = jnp.max(gl, axis=0, keepdims=True)
    gidx = jnp.min(jnp.where(gl == gmax, row, SUBLANES), axis=0, keepdims=True)
    g_p = 1.0 / jnp.sum(jnp.where(row < ng, jnp.exp(gl - gmax), 0.0), axis=0, keepdims=True)

    sel = logits[SUBLANES:2 * SUBLANES, :]
    for g in range(1, ng):
        sel = jnp.where(gidx == g, logits[(1 + g) * SUBLANES:(2 + g) * SUBLANES, :], sel)
    v1 = jnp.max(sel, axis=0, keepdims=True)
    i1 = jnp.min(jnp.where(sel == v1, row, SUBLANES), axis=0, keepdims=True)
    sel2 = jnp.where(row == i1, neg, sel)
    v2 = jnp.max(sel2, axis=0, keepdims=True)
    i2 = jnp.min(jnp.where(sel2 == v2, row, SUBLANES), axis=0, keepdims=True)
    e1 = gidx * SUBLANES + i1
    e2 = gidx * SUBLANES + i2
    tt = jnp.exp(v2 - v1)
    p1 = 1.0 / (1.0 + tt)
    gate1 = p1 * g_p
    gate2 = (tt * p1) * g_p

    rowe = lax.broadcasted_iota(jnp.int32, (ne, tm), 0)
    oh1 = rowe == e1
    oh2 = rowe == e2
    oh = jnp.where(oh1 | oh2, 1.0, 0.0)
    groups = jnp.floor((jnp.sum(oh, axis=1, keepdims=True) + (SUBLANES - 1)) * (1.0 / SUBLANES))
    groups_b = jnp.broadcast_to(groups, (ne, LANES))
    start = SUBLANES * _dot(ltri_ref[...], groups_b.astype(BF16))[:, 0:1]
    prefix = _dot(oh.astype(BF16), upper_ref[...]) + start
    pos1 = jnp.sum(jnp.where(oh1, prefix, 0.0), axis=0, keepdims=True)
    pos2 = jnp.sum(jnp.where(oh2, prefix, 0.0), axis=0, keepdims=True)
    cnt_ref[j] = groups_b

    out = jnp.where(row == 0, e1.astype(F32), 0.0)
    out = jnp.where(row == 1, e2.astype(F32), out)
    out = jnp.where(row == 2, gate1, out)
    out = jnp.where(row == 3, gate2, out)
    out = jnp.where(row == 4, pos1, out)
    out = jnp.where(row == 5, pos2, out)
    o_ref[j] = out


def _router_call(u, wr_t, upper, ltri, ng, ne):
    t, d = u.shape
    tm = upper.shape[0]
    nt = t // tm
    per = 2 if nt % 2 == 0 else 1
    assert ne // ng == SUBLANES and ng <= SUBLANES and SUBLANES + ne <= LANES
    return pl.pallas_call(
        functools.partial(_router_kernel, ng=ng, ne=ne),
        out_shape=(jax.ShapeDtypeStruct((nt, SUBLANES, tm), F32), jax.ShapeDtypeStruct((nt, ne, LANES), F32)),
        grid=(nt // per,),
        in_specs=[
            pl.BlockSpec((per * tm, d), lambda i: (i, 0)),
            pl.BlockSpec((LANES, d), lambda i: (0, 0)),
            pl.BlockSpec((tm, tm), lambda i: (0, 0)),
            pl.BlockSpec((ne, ne), lambda i: (0, 0)),
        ],
        out_specs=(pl.BlockSpec((per, SUBLANES, tm), lambda i: (i, 0, 0)),
                   pl.BlockSpec((per, ne, LANES), lambda i: (i, 0, 0))),
        compiler_params=_cparams(("parallel",)),
        name="moe_router",
    )(u, wr_t, upper, ltri)


def _pack_halves(lo, hi):
    lo_bits = lax.shift_right_logical(pltpu.bitcast(lo, U32), jnp.uint32(16))
    hi_bits = pltpu.bitcast(hi, U32) & jnp.uint32(0xFFFF0000)
    return hi_bits | lo_bits


def _unpack_halves(w):
    lo = pltpu.bitcast(lax.shift_left(w, jnp.uint32(16)), F32)
    hi = pltpu.bitcast(w & jnp.uint32(0xFFFF0000), F32)
    return lo.astype(BF16), hi.astype(BF16)


def _round_bf16(x):
    return x.astype(BF16).astype(F32)


def _group_copy(src, src_g, dst, dst_g, sem):
    g8 = lambda g: pl.ds(pl.multiple_of(g * SUBLANES, SUBLANES), SUBLANES)
    return pltpu.make_async_copy(src.at[g8(src_g), :], dst.at[g8(dst_g), :], sem)


def _for_groups(n, fn, unroll=4):
    def body_many(i, c):
        for j in range(unroll):
            fn(i * unroll + j)
        return c

    def body_one(g, c):
        fn(g)
        return c

    full = lax.div(n, jnp.int32(unroll))
    lax.fori_loop(0, full, body_many, 0)
    lax.fori_loop(full * unroll, n, body_one, 0)


def _dispatch_kernel(gdst_ref, ngt_ref, pad0_ref, npad_ref, tail_ref, u_ref, pos_ref, xs_out, loc, zeros, sem,
                     zsem, *, tm, nl, ne):
    i = pl.program_id(0)
    nt = pl.num_programs(0)
    slot = i % 2
    nlg = nl // SUBLANES

    def copy(step, s, g):
        return _group_copy(loc.at[s], g, xs_out, gdst_ref[step * nlg + g], sem.at[s])

    def group_wait(s):
        _group_copy(loc.at[s], 0, xs_out, 0, sem.at[s]).wait()

    def zero_copy(e, g):
        return _group_copy(zeros, 0, xs_out, pad0_ref[e] + g, zsem)

    def zero_block(t):
        first = pl.multiple_of(tail_ref[0] + t * MOE_BM, MOE_BM)
        return pltpu.make_async_copy(zeros, xs_out.at[pl.ds(first, MOE_BM), :], zsem)

    @pl.when(i == 0)
    def _():
        zeros[...] = jnp.zeros(zeros.shape, zeros.dtype)
        for e in range(ne):
            _for_groups(npad_ref[e], lambda g, e=e: zero_copy(e, g).start())
        _for_groups(tail_ref[1], lambda t: zero_block(t).start())

    @pl.when(i >= 2)
    def _():
        _for_groups(ngt_ref[i - 2], lambda g: group_wait(slot))

    pos1 = pos_ref[4:5, :].astype(jnp.int32)
    pos2 = pos_ref[5:6, :].astype(jnp.int32)
    half = u_ref.shape[1] // 2
    u = u_ref[...]
    rc = 256
    for r in range(0, nl, rc):
        p = lax.broadcasted_iota(jnp.int32, (rc, tm), 0) + r
        onehot = jnp.where((p == pos1) | (p == pos2), 1.0, 0.0).astype(BF16)
        rows = _dot(onehot, u)
        loc[slot, r:r + rc, :] = _pack_halves(rows[:, :half], rows[:, half:])
    _for_groups(ngt_ref[i], lambda g: copy(i, slot, g).start())

    @pl.when(i == nt - 1)
    def _():
        @pl.when(i >= 1)
        def _():
            _for_groups(ngt_ref[i - 1], lambda g: group_wait(1 - slot))

        _for_groups(ngt_ref[i], lambda g: group_wait(slot))
        for e in range(ne):
            _for_groups(npad_ref[e], lambda g: _group_copy(zeros, 0, xs_out, 0, zsem).wait())
        _for_groups(tail_ref[1], lambda t: pltpu.make_async_copy(zeros, xs_out.at[0:MOE_BM, :], zsem).wait())


def _dispatch_call(u, pos_rows, gdst, ngt, pad0, npad, tail, nrows, tm, nl):
    t, d = u.shape
    ne = npad.shape[0]
    grid_spec = pltpu.PrefetchScalarGridSpec(
        num_scalar_prefetch=5,
        grid=(t // tm,),
        in_specs=[
            pl.BlockSpec((tm, d), lambda i, *_: (i, 0)),
            pl.BlockSpec((None, SUBLANES, tm), lambda i, *_: (i, 0, 0)),
        ],
        out_specs=pl.BlockSpec(memory_space=pl.ANY),
        scratch_shapes=[pltpu.VMEM((2, nl, d // 2), U32), pltpu.VMEM((MOE_BM, d // 2), U32),
                        pltpu.SemaphoreType.DMA((2,)), pltpu.SemaphoreType.DMA(())],
    )
    return pl.pallas_call(
        functools.partial(_dispatch_kernel, tm=tm, nl=nl, ne=ne),
        out_shape=jax.ShapeDtypeStruct((nrows, d // 2), U32),
        grid_spec=grid_spec,
        compiler_params=_cparams(("arbitrary",)),
        name="moe_dispatch",
    )(gdst, ngt, pad0, npad, tail, u, pos_rows)


def _expert_kernel(row0_ref, nblk_ref, tail_ref, x_hbm, w13_ref, w2_ref, y_hbm, xbuf, ybuf, w13_bf, w2_bf,
                   xsem, ysem, *, hid):
    e = pl.program_id(0)
    n = nblk_ref[e]
    g0 = row0_ref[e] // MOE_BM
    total = tail_ref[0] // MOE_BM
    nx, ny = xbuf.shape[0], ybuf.shape[0]
    ahead = nx - 1
    half = xbuf.shape[2]

    def rows(g):
        return pl.ds(pl.multiple_of(g * MOE_BM, MOE_BM), MOE_BM)

    def x_copy(g):
        return pltpu.make_async_copy(x_hbm.at[rows(g), :], xbuf.at[g % nx], xsem.at[g % nx])

    def y_copy(g, s):
        return pltpu.make_async_copy(ybuf.at[s], y_hbm.at[rows(g), :], ysem.at[s])

    @pl.when(e == 0)
    def _():
        for g in range(ahead):
            @pl.when(g < total)
            def _(g=g):
                x_copy(g).start()

    @pl.when(n > 0)
    def _():
        w13_bf[...] = w13_ref[...].astype(BF16)
        w2_bf[...] = w2_ref[...].astype(BF16)

        def block(c, carry):
            g = g0 + c
            x_copy(g).wait()

            @pl.when(g + ahead < total)
            def _():
                x_copy(g + ahead).start()

            @pl.when(g >= ny)
            def _():
                y_copy(g - ny, g % ny).wait()

            x_lo, x_hi = _unpack_halves(xbuf[g % nx])
            hb = _dot(x_lo, w13_bf[:half, :]) + _dot(x_hi, w13_bf[half:, :])
            a = hb[:, :hid]
            act = (a * _sigmoid(a)) * hb[:, hid:]
            y = _round_bf16(_dot(act.astype(BF16), w2_bf[...]))
            ybuf[g % ny] = _pack_halves(y[:, :half], y[:, half:])
            y_copy(g, g % ny).start()
            return carry

        lax.fori_loop(0, n, block, 0)

    @pl.when(e == pl.num_programs(0) - 1)
    def _():
        for j in range(ny):
            @pl.when(total - ny + j >= 0)
            def _(j=j):
                g = total - ny + j
                y_copy(g, g % ny).wait()

        ybuf[0] = jnp.zeros(ybuf.shape[1:], ybuf.dtype)
        _for_groups(tail_ref[1], lambda t: y_copy(total + t, 0).start())
        _for_groups(tail_ref[1], lambda t: y_copy(total + t, 0).wait())


def _expert_call(xs, row0, nblk, tail, w13, w2, l):
    r, half = xs.shape
    d = 2 * half
    ne = w13.shape[1]
    hid = w2.shape[2]
    nx, ny = 5, 3
    grid_spec = pltpu.PrefetchScalarGridSpec(
        num_scalar_prefetch=3,
        grid=(ne,),
        in_specs=[
            pl.BlockSpec(memory_space=pl.ANY),
            pl.BlockSpec((None, None, d, 2 * hid), lambda e, *_: (l, e, 0, 0)),
            pl.BlockSpec((None, None, hid, d), lambda e, *_: (l, e, 0, 0)),
        ],
        out_specs=pl.BlockSpec(memory_space=pl.ANY),
        scratch_shapes=[
            pltpu.VMEM((nx, MOE_BM, half), U32), pltpu.VMEM((ny, MOE_BM, half), U32),
            pltpu.VMEM((d, 2 * hid), BF16), pltpu.VMEM((hid, d), BF16),
            pltpu.SemaphoreType.DMA((nx,)), pltpu.SemaphoreType.DMA((ny,)),
        ],
    )
    return pl.pallas_call(
        functools.partial(_expert_kernel, hid=hid),
        out_shape=jax.ShapeDtypeStruct((r, half), U32),
        grid_spec=grid_spec,
        compiler_params=_cparams(("arbitrary",)),
        name="moe_experts",
    )(row0, nblk, tail, xs, w13, w2)


def _combine_kernel(gdst_ref, ngt_ref, x_ref, rt_ref, gt_ref, nw_ref, sh_ref, sc_ref, ys_ref, *rest,
                    lay, tm, nl, final, gated):
    if gated:
        wh_ref, wl_ref, out_a, out_b, out_g, loc, sem = rest
    else:
        out_a, out_b, loc, sem = rest
    i = pl.program_id(0)
    nt = pl.num_programs(0)
    slot = i % 2
    nlg = nl // SUBLANES

    def copy(step, s, g):
        return _group_copy(ys_ref, gdst_ref[step * nlg + g], loc.at[s], g, sem.at[s])

    @pl.when(i == 0)
    def _():
        loc[...] = jnp.zeros(loc.shape, loc.dtype)
        _for_groups(ngt_ref[0], lambda g: copy(0, 0, g).start())

    _for_groups(ngt_ref[i], lambda g: _group_copy(ys_ref, 0, loc.at[slot], 0, sem.at[slot]).wait())

    @pl.when(i + 1 < nt)
    def _():
        _for_groups(ngt_ref[i + 1], lambda g: copy(i + 1, 1 - slot, g).start())

    rt = rt_ref[...]
    gate1, gate2 = rt[:, 2:3], rt[:, 3:4]
    pos1, pos2 = rt[:, 4:5].astype(jnp.int32), rt[:, 5:6].astype(jnp.int32)
    half = loc.shape[2]
    rc = 256
    mix_lo = jnp.zeros((tm, half), F32)
    mix_hi = jnp.zeros((tm, half), F32)
    for r in range(0, nl, rc):
        p = lax.broadcasted_iota(jnp.int32, (tm, rc), 1) + r
        wgt = (jnp.where(p == pos1, gate1, 0.0) + jnp.where(p == pos2, gate2, 0.0)).astype(BF16)
        y_lo, y_hi = _unpack_halves(loc[slot, r:r + rc, :])
        mix_lo = mix_lo + _dot(wgt, y_lo)
        mix_hi = mix_hi + _dot(wgt, y_hi)
    grp = lay.group(i * tm)
    gate = gt_ref[pl.ds(grp, 1), :]
    nw = nw_ref[...]
    sh = sh_ref[pl.ds(grp, 1), :]
    sc = sc_ref[pl.ds(grp, 1), :]
    er = 128 if gated else tm
    for r in range(0, tm, er):
        rs = slice(r, r + er)
        x_lo = x_ref[rs, :half] + gate[:, :half] * mix_lo[rs]
        x_hi = x_ref[rs, half:] + gate[:, half:] * mix_hi[rs]
        ms = (jnp.sum(x_lo * x_lo, axis=-1, keepdims=True)
              + jnp.sum(x_hi * x_hi, axis=-1, keepdims=True)) / (2 * half)
        inv = lax.rsqrt(ms + EPS)
        if final:
            y_lo = x_lo * inv * nw[:, :half]
            y_hi = x_hi * inv * nw[:, half:]
            for ref, cond in ((out_a, i * tm < lay.tp), (out_b, i * tm >= lay.tp)):
                @pl.when(cond)
                def _(ref=ref, y_lo=y_lo, y_hi=y_hi):
                    ref[rs, :half] = y_lo
                    ref[rs, half:] = y_hi
            continue
        out_a[rs, :half] = x_lo
        out_a[rs, half:] = x_hi
        u_lo = x_lo * inv * nw[:, :half] * (1.0 + sc[:, :half]) + sh[:, :half]
        u_hi = x_hi * inv * nw[:, half:] * (1.0 + sc[:, half:]) + sh[:, half:]
        if gated:
            (a_hi, a_lo), (b_hi, b_lo) = _split2(u_lo), _split2(u_hi)
            wh, wl = wh_ref[...], wl_ref[...]
            out_g[rs, :] = (_dot_nt(a_hi, wh[:, :half]) + _dot_nt(b_hi, wh[:, half:])
                            + _dot_nt(a_lo, wh[:, :half]) + _dot_nt(b_lo, wh[:, half:])
                            + _dot_nt(a_hi, wl[:, :half]) + _dot_nt(b_hi, wl[:, half:]))
            out_b[rs, :half] = a_hi
            out_b[rs, half:] = b_hi
        else:
            out_b[rs, :half] = u_lo.astype(BF16)
            out_b[rs, half:] = u_hi.astype(BF16)


def _combine_call(lay, x, route, mods, l, mod_next, gdst, ngt, ys, tm, nl, final, gate_w=None):
    t, d = x.shape
    rows = pl.BlockSpec((tm, d), lambda i, *_: (i, 0))
    gated = gate_w is not None
    extra_in, extra_args = [], ()
    if final:
        ntp = lay.tp // tm
        out_specs = (pl.BlockSpec((tm, d), lambda i, *_: (jnp.minimum(i, ntp - 1), 0)),
                     pl.BlockSpec((tm, d), lambda i, *_: (jnp.maximum(i - ntp, 0), 0)))
        out_shape = (jax.ShapeDtypeStruct((lay.tp, d), F32), jax.ShapeDtypeStruct((lay.ts, d), F32))
    else:
        out_specs = (rows, rows)
        out_shape = (jax.ShapeDtypeStruct(x.shape, F32), jax.ShapeDtypeStruct(x.shape, BF16))
        if gated:
            wspec = pl.BlockSpec((LANES, d), lambda i, *_: (0, 0))
            extra_in, extra_args = [wspec, wspec], tuple(gate_w)
            out_specs += (pl.BlockSpec((tm, LANES), lambda i, *_: (i, 0)),)
            out_shape += (jax.ShapeDtypeStruct((t, LANES), F32),)
    grid_spec = pltpu.PrefetchScalarGridSpec(
        num_scalar_prefetch=2,
        grid=(t // tm,),
        in_specs=[
            rows,
            pl.BlockSpec((tm, SUBLANES), lambda i, *_: (i, 0)),
            lay.mod_spec(l, 5),
        ] + mod_next.specs + [pl.BlockSpec(memory_space=pl.ANY)] + extra_in,
        out_specs=out_specs,
        scratch_shapes=[pltpu.VMEM((2, nl, d // 2), U32), pltpu.SemaphoreType.DMA((2,))],
    )
    return pl.pallas_call(
        functools.partial(_combine_kernel, lay=lay, tm=tm, nl=nl, final=final, gated=gated),
        out_shape=out_shape,
        grid_spec=grid_spec,
        compiler_params=_cparams(("arbitrary",)),
        name="moe_combine_final" if final else "moe_combine",
    )(gdst, ngt, x, route, mods, *mod_next.args, ys, *extra_args)


def _moe_layer(lay, x, u2, mods, l, mod_next, final, wr_t, upper, ltri, w13, w2, ng, ne, gate_w=None):
    t, d = x.shape
    tm = upper.shape[0]
    nt = t // tm
    bmg = MOE_BM // SUBLANES
    nl = -(-(MOE_TOP_K * tm + ne * (SUBLANES - 1)) // 256) * 256
    nlg = nl // SUBLANES
    route_t, counts = _router_call(u2, wr_t, upper, ltri, ng, ne)
    route = jnp.swapaxes(route_t, 1, 2).reshape(t, SUBLANES)

    c8 = counts[:, :, 0].astype(jnp.int32)
    lend = jnp.cumsum(c8, axis=1)
    lstart = lend - c8
    ngt = lend[:, -1].astype(jnp.int32)
    tot = jnp.sum(c8, axis=0)
    padded = (tot + bmg - 1) // bmg * bmg
    gend = jnp.cumsum(padded)
    gbase = (gend - padded)[None, :] + jnp.cumsum(c8, axis=0) - c8
    nb = -(-(MOE_TOP_K * t + nt * ne * (SUBLANES - 1)) // MOE_BM) + ne
    g = jnp.arange(nlg, dtype=jnp.int32)[None, :, None]
    owner = (g >= lstart[:, None, :]) & (g < lend[:, None, :])
    gdst = g[:, :, 0] + jnp.sum(jnp.where(owner, (gbase - lstart)[:, None, :], 0), axis=-1)
    gdst = gdst.reshape(nt * nlg).astype(jnp.int32)
    row0 = ((gend - padded) * SUBLANES).astype(jnp.int32)
    nblk = (padded // bmg).astype(jnp.int32)
    tail = jnp.stack([gend[-1] * SUBLANES, nb - gend[-1] // bmg]).astype(jnp.int32)

    pad0 = (gend - padded + tot).astype(jnp.int32)
    npad = (padded - tot).astype(jnp.int32)
    xs = _dispatch_call(u2, route_t, gdst, ngt, pad0, npad, tail, nb * MOE_BM, tm, nl)
    ys = _expert_call(xs, row0, nblk, tail, w13, w2, l)
    return _combine_call(lay, x, route, mods, l, mod_next, gdst, ngt, ys, tm, nl, final, gate_w)


def _lower_tri(n, strict):
    r = np.arange(n)
    m = (r[None, :] < r[:, None]) if strict else (r[None, :] <= r[:, None])
    return jnp.asarray(m.astype(np.float32), dtype=BF16)


def kernel(x_prompt, x_sample, state_C, state_n, state_m, c, c_ctx, ada_w, ada_b, norm1_w, norm2_w, m_w_in, m_b_gate, m_head_norm_w, m_w_out, f_w_out, cv_w_pw1, cv_b_pw1, cv_w_dw, cv_b_dw, cv_ln_w, cv_ln_b, cv_w_pw2, cv_b_pw2, r_w_group, r_w_expert, e_w13, e_w2, final_norm_w):
    nbp, sp, d = x_prompt.shape
    nbs, ss, _ = x_sample.shape
    assert ss % GRID_W == 0
    lay = _Layout(nbp, sp, nbs, ss, d)
    depth = ada_w.shape[0]
    nh, dh = state_C.shape[3], state_C.shape[4]
    di = nh * dh
    ng = r_w_group.shape[2]
    ne = r_w_expert.shape[2]
    assert ng == MOE_GROUPS and ng + ne <= LANES and 4 * nh <= LANES and MOE_TOP_K == 2

    cv = jnp.zeros((lay.ngp, d), F32).at[0].set(c_ctx).at[1:1 + nbs].set(c)
    mods = _ada_call(cv, ada_w, ada_b)

    tri_l = _lower_tri(MLSTM_L, strict=False)
    upper = _lower_tri(lay.row_tile(512), strict=True).T
    ltri = _lower_tri(ne, strict=True)
    row = lambda a: a.reshape(1, -1)
    w_in_t = jnp.swapaxes(m_w_in, 1, 2)
    tn = 1024
    assert di % tn == 0
    nkb = di // tn

    def gate_weights(j):
        wg = jnp.zeros((LANES, d), F32).at[:4 * nh].set(w_in_t[j, 4 * di:])
        return _split2(wg)

    x, u1, graw = _prep_call(lay, x_prompt.reshape(lay.tp, d), x_sample.reshape(lay.ts, d),
                             _Mod(lay, mods, norm1_w[0], 0, 0), gate_weights(0))
    y = None
    qvos, kts, gpts = [], [], []
    for l in range(depth):
        j, kind = l // N_MIXERS, l % N_MIXERS
        mod2 = _Mod(lay, mods, norm2_w[l], l, 3)
        if kind == 0:
            bg = jnp.zeros((1, LANES), F32).at[0, :4 * nh].set(m_b_gate[j])
            gp = _gates_call(lay, graw, bg, tri_l, nh)
            gpt = gp[:, :4 * nh].T
            qvo_blocks = list(range(nkb)) + list(range(2 * nkb, 4 * nkb))
            qvo = _proj_call(lay, u1, w_in_t, j, qvo_blocks, tn, transposed=False)
            kt = _proj_call(lay, u1, w_in_t, j, list(range(nkb, 2 * nkb)), tn, transposed=True)
            hw = row(m_head_norm_w[j])
            hg_p = _mlstm_call(lay, qvo, kt, gp, gpt, hw, nh, dh, prompt=True)
            n0 = state_n[:, j].reshape(nbs, 2 * nh, dh)
            m0 = jnp.broadcast_to(state_m[:, j].reshape(nbs, 2 * nh, 1), (nbs, 2 * nh, LANES))
            hg_s = _mlstm_call(lay, qvo, kt, gp, gpt, hw, nh, dh, prompt=False, state=(state_C, j, n0, m0))
            x, u2 = _mm_res_call(lay, hg_p, hg_s, m_w_out[j].astype(BF16), x, mods, l, mod2)
            qvos.append(qvo)
            kts.append(kt)
            gpts.append(gpt)
        elif kind == 1:
            wo = f_w_out[j].astype(BF16)
            u2 = jnp.zeros((lay.t, d), BF16)
            x, u2 = _fnet_call(lay, x, u1, u2, mods, l, wo, mod2, prompt=True)
            x, u2 = _fnet_call(lay, x, u1, u2, mods, l, wo, mod2, prompt=False)
        else:
            glu = _glu_call(lay, u1, cv_w_pw1[j].astype(BF16), row(cv_b_pw1[j]))
            x, u2 = _conv_call(lay, glu, cv_w_dw[j], cv_b_dw[j], cv_ln_w[j], cv_ln_b[j], cv_w_pw2[j].astype(BF16),
                               cv_b_pw2[j], x, mods, l, mod2)
        wr = jnp.zeros((LANES, d), F32).at[:ng].set(r_w_group[l].T).at[SUBLANES:SUBLANES + ne].set(r_w_expert[l].T)
        final = l + 1 == depth
        mod_next = _Mod(lay, mods, final_norm_w, l, 0) if final else _Mod(lay, mods, norm1_w[l + 1], l + 1, 0)
        next_mlstm = not final and (l + 1) % N_MIXERS == 0
        outs = _moe_layer(lay, x, u2, mods, l, mod_next, final, wr.astype(BF16), upper, ltri, e_w13, e_w2, ng, ne,
                          gate_weights((l + 1) // N_MIXERS) if next_mlstm else None)
        if final:
            y = outs
        elif next_mlstm:
            x, u1, graw = outs
        else:
            x, u1 = outs

    y_prompt = y[0].reshape(nbp, sp, d)
    y_sample = y[1].reshape(nbs, ss, d)
    new_c, new_n, new_m = _state_call(lay, qvos, kts, gpts, nh, dh)
    return (y_prompt, y_sample, new_c, new_n, new_m)
```

```python
import functools
import math

import numpy as np
import jax
import jax.numpy as jnp
from jax import lax
from jax.experimental import pallas as pl
from jax.experimental.pallas import tpu as pltpu

F32 = jnp.float32
BF16 = jnp.bfloat16
U32 = jnp.uint32
EPS = 1e-6
GRID_W = 64
N_MIXERS = 3
FNET_GROUPS = 4
CONV_WIDTH = 31
MOE_GROUPS = 4
MOE_TOP_K = 2

LANES = 128
SUBLANES = 8
MLSTM_L = 256
MOE_BM = 256
CONV_HALO = 16
VMEM_LIMIT = 56 * 1024 * 1024


def _cparams(sem, vmem=VMEM_LIMIT):
    return pltpu.CompilerParams(dimension_semantics=sem, vmem_limit_bytes=vmem)


def _dot(a, b):
    return jnp.dot(a, b, preferred_element_type=F32)


def _dot_nt(a, b):
    return lax.dot_general(a, b, (((1,), (1,)), ((), ())), preferred_element_type=F32)


def _rms(x, w):
    return x * lax.rsqrt(jnp.mean(x * x, axis=-1, keepdims=True) + EPS) * w


def _modulate(x, w, shift, scale):
    return _rms(x, w) * (1.0 + scale) + shift


def _sigmoid(x):
    return 1.0 / (1.0 + jnp.exp(-x))


def _log_sigmoid(x):
    return jnp.minimum(x, 0.0) - jnp.log(1.0 + jnp.exp(-jnp.abs(x)))


def _split2(x):
    hi = x.astype(BF16)
    return hi, (x - hi.astype(F32)).astype(BF16)


def _split3(x):
    hi = x.astype(BF16)
    r1 = x - hi.astype(F32)
    mid = r1.astype(BF16)
    return hi, mid, (r1 - mid.astype(F32)).astype(BF16)


class _Layout:
    def __init__(self, nbp, sp, nbs, ss, d):
        self.nbp, self.sp, self.nbs, self.ss, self.d = nbp, sp, nbs, ss, d
        self.tp, self.ts = nbp * sp, nbs * ss
        self.t = self.tp + self.ts
        assert self.tp % ss == 0, "latent sequences must start on a block boundary of their own length"
        self.ngp = -(-(1 + nbs) // SUBLANES) * SUBLANES

    def group(self, row0):
        return jnp.where(row0 < self.tp, 0, 1 + (row0 - self.tp) // self.ss)

    def row_tile(self, want):
        tm = math.gcd(math.gcd(self.tp, self.ss), want)
        assert tm % SUBLANES == 0
        return tm

    def mod_spec(self, l, chunk):
        return pl.BlockSpec((None, self.ngp, self.d), lambda *_: (l, 0, chunk))

    def row_spec(self):
        return pl.BlockSpec((1, self.d), lambda *_: (0, 0))


class _Mod:
    def __init__(self, lay, mods, nw, l, c_shift):
        self.args = (nw.reshape(1, -1), mods, mods)
        self.specs = [lay.row_spec(), lay.mod_spec(l, c_shift), lay.mod_spec(l, c_shift + 1)]


def _mod_value(x, nw_ref, sh_ref, sc_ref, grp):
    return _modulate(x, nw_ref[...], sh_ref[pl.ds(grp, 1), :], sc_ref[pl.ds(grp, 1), :])


def _ada_kernel(cv_ref, w_ref, b_ref, o_ref):
    s = cv_ref[...]
    s = s * _sigmoid(s)
    o_ref[...] = _dot(s.astype(BF16), w_ref[...].astype(BF16)) + b_ref[...]


def _ada_call(cv, ada_w, ada_b):
    depth, d, n = ada_w.shape
    ngp = cv.shape[0]
    tn = min(n, 2048)
    return pl.pallas_call(
        _ada_kernel,
        out_shape=jax.ShapeDtypeStruct((depth, ngp, n), F32),
        grid=(depth, n // tn),
        in_specs=[
            pl.BlockSpec((ngp, d), lambda l, j: (0, 0)),
            pl.BlockSpec((None, d, tn), lambda l, j: (l, 0, j)),
            pl.BlockSpec((None, 1, tn), lambda l, j: (l, 0, j)),
        ],
        out_specs=pl.BlockSpec((None, ngp, tn), lambda l, j: (l, 0, j)),
        compiler_params=_cparams(("parallel", "parallel")),
        name="ada_mods",
    )(cv, ada_w, ada_b.reshape(depth, 1, n))


def _gate_raw(u, wh, wl):
    u_hi, u_lo = _split2(u)
    return _dot_nt(u_hi, wh) + _dot_nt(u_lo, wh) + _dot_nt(u_hi, wl), u_hi


def _prep_kernel(xp_ref, xs_ref, nw_ref, sh_ref, sc_ref, wh_ref, wl_ref, x_ref, u_ref, g_ref, *, lay, tm):
    i = pl.program_id(0)
    grp = lay.group(i * tm)
    for src, cond in ((xp_ref, i * tm < lay.tp), (xs_ref, i * tm >= lay.tp)):
        @pl.when(cond)
        def _(src=src):
            rc = min(tm, 256)
            for r in range(0, tm, rc):
                x = src[r:r + rc, :]
                x_ref[r:r + rc, :] = x
                g, u_hi = _gate_raw(_mod_value(x, nw_ref, sh_ref, sc_ref, grp), wh_ref[...], wl_ref[...])
                u_ref[r:r + rc, :] = u_hi
                g_ref[r:r + rc, :] = g


def _prep_call(lay, xp, xs, mod, gate_w):
    d = lay.d
    tm = lay.row_tile(1024)
    ntp = lay.tp // tm
    rows = pl.BlockSpec((tm, d), lambda i: (i, 0))
    wspec = pl.BlockSpec((LANES, d), lambda i: (0, 0))
    return pl.pallas_call(
        functools.partial(_prep_kernel, lay=lay, tm=tm),
        out_shape=(jax.ShapeDtypeStruct((lay.t, d), F32), jax.ShapeDtypeStruct((lay.t, d), BF16),
                   jax.ShapeDtypeStruct((lay.t, LANES), F32)),
        grid=(lay.t // tm,),
        in_specs=[
            pl.BlockSpec((tm, d), lambda i: (jnp.minimum(i, ntp - 1), 0)),
            pl.BlockSpec((tm, d), lambda i: (jnp.maximum(i - ntp, 0), 0)),
        ] + mod.specs + [wspec, wspec],
        out_specs=(rows, rows, pl.BlockSpec((tm, LANES), lambda i: (i, 0))),
        compiler_params=_cparams(("arbitrary",)),
        name="prep_modulate",
    )(xp, xs, *mod.args, *gate_w)


def _gates_kernel(g_ref, b_ref, tri_ref, o_ref, *, nh, tm):
    l = tri_ref.shape[0]
    tri = tri_ref[...]
    lane = lax.broadcasted_iota(jnp.int32, (l, LANES), 1)
    is_f = ((lane >= nh) & (lane < 2 * nh)) | ((lane >= 3 * nh) & (lane < 4 * nh))
    is_a = (lane < nh) | ((lane >= 2 * nh) & (lane < 3 * nh))
    for r in range(0, tm, l):
        g = g_ref[r:r + l, :] + b_ref[...]
        lf = jnp.where(is_f, _log_sigmoid(g), 0.0)
        hi, mid, lo = _split3(lf)
        prefix = _dot(tri, hi) + _dot(tri, mid) + _dot(tri, lo)
        suffix = jnp.sum(lf, axis=0, keepdims=True) - prefix + lf
        b = jnp.where(lane < 2 * nh, prefix, suffix)
        a = g - pltpu.roll(b, LANES - nh, 1)
        o_ref[r:r + l, :] = jnp.where(is_a, a, b)


def _gates_call(lay, graw, bias, tri, nh):
    t = graw.shape[0]
    l = tri.shape[0]
    tm = lay.row_tile(2048)
    assert tm % l == 0
    return pl.pallas_call(
        functools.partial(_gates_kernel, nh=nh, tm=tm),
        out_shape=jax.ShapeDtypeStruct((t, LANES), F32),
        grid=(t // tm,),
        in_specs=[
            pl.BlockSpec((tm, LANES), lambda i: (i, 0)),
            pl.BlockSpec((1, LANES), lambda i: (0, 0)),
            pl.BlockSpec((l, l), lambda i: (0, 0)),
        ],
        out_specs=pl.BlockSpec((tm, LANES), lambda i: (i, 0)),
        compiler_params=_cparams(("parallel",)),
        name="mlstm_gates",
    )(graw, bias, tri)


def _proj_kernel(u_ref, w_ref, o_ref, w_bf, *, transposed):
    @pl.when(pl.program_id(1) == 0)
    def _():
        w_bf[...] = w_ref[...].astype(BF16)

    if transposed:
        o_ref[...] = _dot_nt(w_bf[...], u_ref[...]).astype(BF16)
    else:
        o_ref[...] = _dot_nt(u_ref[...], w_bf[...]).astype(BF16)


def _proj_call(lay, u, w_in_t, jl, blocks, tn, transposed):
    t, d = u.shape
    tm = 2048 if t % 2048 == 0 else lay.row_tile(1024)
    nb = len(blocks)
    first, gap_at, gap = blocks[0], None, 0
    for idx in range(1, nb):
        if blocks[idx] != blocks[idx - 1] + 1:
            assert gap_at is None
            gap_at, gap = idx, blocks[idx] - blocks[idx - 1] - 1
    wblk = (lambda j: first + j) if gap_at is None else (lambda j: first + j + jnp.where(j >= gap_at, gap, 0))
    if transposed:
        out_shape = jax.ShapeDtypeStruct((nb * tn, t), BF16)
        out_spec = pl.BlockSpec((tn, tm), lambda j, i: (j, i))
    else:
        out_shape = jax.ShapeDtypeStruct((t, nb * tn), BF16)
        out_spec = pl.BlockSpec((tm, tn), lambda j, i: (i, j))
    return pl.pallas_call(
        functools.partial(_proj_kernel, transposed=transposed),
        out_shape=out_shape,
        grid=(nb, t // tm),
        in_specs=[
            pl.BlockSpec((tm, d), lambda j, i: (i, 0)),
            pl.BlockSpec((None, tn, d), lambda j, i: (jl, wblk(j), 0)),
        ],
        out_specs=out_spec,
        scratch_shapes=[pltpu.VMEM((tn, d), BF16)],
        compiler_params=_cparams(("parallel", "arbitrary")),
        name="mlstm_proj_t" if transposed else "mlstm_proj",
    )(u, w_in_t)


def _col(tile, c):
    lane = lax.broadcasted_iota(jnp.int32, tile.shape, 1)
    return jnp.sum(jnp.where(lane == c, tile, 0.0), axis=-1, keepdims=True)


def _dir_masks(l):
    r = lax.broadcasted_iota(jnp.int32, (l, l), 0)
    c = lax.broadcasted_iota(jnp.int32, (l, l), 1)
    return c <= r, c >= r


def _head_epilogue(h, hw, o):
    hn = h * lax.rsqrt(jnp.mean(h * h, axis=-1, keepdims=True) + EPS) * hw
    return (hn * _sigmoid(o.astype(F32))).astype(BF16)


def _row_times_kt(w_row, kt):
    hi, lo = _split2(w_row)
    sub = lax.broadcasted_iota(jnp.int32, (SUBLANES, w_row.shape[1]), 0)
    stacked = jnp.where(sub == 0, hi.astype(F32), jnp.where(sub == 1, lo.astype(F32), 0.0)).astype(BF16)
    res = _dot_nt(stacked, kt)
    return res[0:1, :] + res[1:2, :]


def _mlstm_single_kernel(q_ref, kt_ref, v_ref, o_ref, gp_ref, gpt_ref, hw_ref, out_ref, *, nh, dh, scale):
    l = q_ref.shape[0]
    gp = gp_ref[...]
    masks = _dir_masks(l)
    for h in range(nh):
        cols = slice(h * dh, (h + 1) * dh)
        qk = _dot(q_ref[:, cols], kt_ref[cols, :])
        p = None
        for d in range(2):
            a_r = gpt_ref[2 * nh * d + h:2 * nh * d + h + 1, :]
            b_c = gp[:, 2 * nh * d + nh + h:2 * nh * d + nh + h + 1]
            g = jnp.where(masks[d], a_r, -jnp.inf)
            m = jnp.maximum(jnp.max(g, axis=-1, keepdims=True), 0.0)
            s = qk * jnp.exp(g - m)
            den = scale * jnp.sum(s, axis=-1, keepdims=True)
            inv = scale / jnp.maximum(jnp.abs(den), jnp.exp(-(b_c + m)))
            p = s * inv if p is None else p + s * inv
        hh = _dot(p.astype(BF16), v_ref[:, cols])
        out_ref[:, cols] = _head_epilogue(hh, hw_ref[:, cols], o_ref[:, cols])


def _mlstm_multi_kernel(q_ref, kt_ref, v_ref, o_ref, gp_ref, gpt_ref, hw_ref, c0_ref, n0_ref, m0_ref,
                        out_ref, cst, cbf, *, nh, nc, l, scale):
    h = pl.program_id(1)
    masks = _dir_masks(l)
    m_in = [[None] * nc for _ in range(2)]
    n_in = [[None] * nc for _ in range(2)]
    for d in range(2):
        cst[...] = c0_ref[d]
        n = n0_ref[pl.ds(d * nh + h, 1), :]
        m = m0_ref[pl.ds(d * nh + h, 1), 0:1]
        order = list(range(nc)) if d == 0 else list(range(nc - 1, -1, -1))
        for step, c in enumerate(order):
            m_in[d][c], n_in[d][c] = m, n
            cbf[d, c] = cst[...].astype(BF16)
            if step + 1 < nc:
                r0 = c * l
                a_r = gpt_ref[pl.ds(2 * nh * d + h, 1), r0:r0 + l]
                b_r = gpt_ref[pl.ds(2 * nh * d + nh + h, 1), r0:r0 + l]
                m_last = jnp.maximum(jnp.max(a_r, axis=-1, keepdims=True), m)
                b_end = b_r[:, l - 1:l] if d == 0 else b_r[:, 0:1]
                decay = jnp.exp(m - m_last)
                w_end = jnp.exp(a_r - m_last)
                kt = kt_ref[:, r0:r0 + l]
                cst[...] = decay * cst[...] + _dot((kt.astype(F32) * w_end).astype(BF16), v_ref[r0:r0 + l, :])
                n = decay * n + _row_times_kt(w_end, kt)
                m = b_end + m_last
    hw = hw_ref[...]
    for c in range(nc):
        r0 = c * l
        q = q_ref[r0:r0 + l, :]
        v = v_ref[r0:r0 + l, :]
        qk = _dot(q, kt_ref[:, r0:r0 + l])
        qf = q.astype(F32)
        gp = gp_ref[r0:r0 + l, :]
        p = inter = None
        for d in range(2):
            m, n = m_in[d][c], n_in[d][c]
            a_r = gpt_ref[pl.ds(2 * nh * d + h, 1), r0:r0 + l]
            b_c = _col(gp, 2 * nh * d + nh + h)
            g = jnp.where(masks[d], a_r, -jnp.inf)
            mt = jnp.maximum(jnp.max(g, axis=-1, keepdims=True), m)
            s = qk * jnp.exp(g - mt)
            w_prev = jnp.exp(m - mt)
            den = scale * (jnp.sum(s, axis=-1, keepdims=True) + w_prev * jnp.sum(qf * n, axis=-1, keepdims=True))
            inv = scale / jnp.maximum(jnp.abs(den), jnp.exp(-(b_c + mt)))
            term = (w_prev * inv) * _dot(q, cbf[d, c])
            p = s * inv if p is None else p + s * inv
            inter = term if inter is None else inter + term
        hh = _dot(p.astype(BF16), v) + inter
        out_ref[r0:r0 + l, :] = _head_epilogue(hh, hw, o_ref[r0:r0 + l, :])


def _mlstm_call(lay, qvo, kt, gp, gpt, hw, nh, dh, prompt, state=None):
    nb, s = (lay.nbp, lay.sp) if prompt else (lay.nbs, lay.ss)
    rb0 = 0 if prompt else lay.tp // s
    scale = dh ** -0.5
    di = nh * dh
    common_in = [
        pl.BlockSpec((s, dh), lambda b, h: (rb0 + b, h)),
        pl.BlockSpec((dh, s), lambda b, h: (h, rb0 + b)),
        pl.BlockSpec((s, dh), lambda b, h: (rb0 + b, nh + h)),
        pl.BlockSpec((s, dh), lambda b, h: (rb0 + b, 2 * nh + h)),
        pl.BlockSpec((s, LANES), lambda b, h: (rb0 + b, 0)),
        pl.BlockSpec((4 * nh, s), lambda b, h: (0, rb0 + b)),
        pl.BlockSpec((1, dh), lambda b, h: (0, h)),
    ]
    if prompt:
        assert s == MLSTM_L
        return pl.pallas_call(
            functools.partial(_mlstm_single_kernel, nh=nh, dh=dh, scale=scale),
            out_shape=jax.ShapeDtypeStruct((nb * s, di), BF16),
            grid=(nb,),
            in_specs=[
                pl.BlockSpec((s, di), lambda b: (rb0 + b, 0)),
                pl.BlockSpec((di, s), lambda b: (0, rb0 + b)),
                pl.BlockSpec((s, di), lambda b: (rb0 + b, 1)),
                pl.BlockSpec((s, di), lambda b: (rb0 + b, 2)),
                pl.BlockSpec((s, LANES), lambda b: (rb0 + b, 0)),
                pl.BlockSpec((4 * nh, s), lambda b: (0, rb0 + b)),
                pl.BlockSpec((1, di), lambda b: (0, 0)),
            ],
            out_specs=pl.BlockSpec((s, di), lambda b: (b, 0)),
            compiler_params=_cparams(("parallel",)),
            name="mlstm_prompt",
        )(qvo, kt, qvo, qvo, gp, gpt, hw)
    state_c, jl, n0, m0 = state
    nc = s // MLSTM_L
    return pl.pallas_call(
        functools.partial(_mlstm_multi_kernel, nh=nh, nc=nc, l=MLSTM_L, scale=scale),
        out_shape=jax.ShapeDtypeStruct((nb * s, di), BF16),
        grid=(nb, nh),
        in_specs=common_in + [
            pl.BlockSpec((None, None, 2, None, dh, dh), lambda b, h: (b, jl, 0, h, 0, 0)),
            pl.BlockSpec((None, 2 * nh, dh), lambda b, h: (b, 0, 0)),
            pl.BlockSpec((None, 2 * nh, LANES), lambda b, h: (b, 0, 0)),
        ],
        out_specs=pl.BlockSpec((s, dh), lambda b, h: (b, h)),
        scratch_shapes=[pltpu.VMEM((dh, dh), F32), pltpu.VMEM((2, nc, dh, dh), BF16)],
        compiler_params=_cparams(("parallel", "parallel")),
        name="mlstm_latent",
    )(qvo, kt, qvo, qvo, gp, gpt, hw, state_c, n0, m0)


def _state_kernel(*refs, nl, nh, dh):
    ins, (c_ref, n_ref, m_ref) = refs[:3 * nl], refs[3 * nl:]
    lyr = pl.program_id(0)
    for jl in range(nl):
        kt_ref, v_ref, gpt_ref = ins[3 * jl:3 * jl + 3]

        @pl.when(lyr == jl)
        def _(kt_ref=kt_ref, v_ref=v_ref, gpt_ref=gpt_ref):
            l = v_ref.shape[0]
            sub = lax.broadcasted_iota(jnp.int32, m_ref.shape, 0)
            lane = lax.broadcasted_iota(jnp.int32, m_ref.shape, 1)
            m_all = jnp.zeros(m_ref.shape, F32)
            for h in range(nh):
                kt = kt_ref[h * dh:(h + 1) * dh, :]
                ktf = kt.astype(F32)
                v = v_ref[:, h * dh:(h + 1) * dh]
                for d in range(2):
                    a_r = gpt_ref[2 * nh * d + h:2 * nh * d + h + 1, :]
                    b_r = gpt_ref[2 * nh * d + nh + h:2 * nh * d + nh + h + 1, :]
                    m_last = jnp.maximum(jnp.max(a_r, axis=-1, keepdims=True), 0.0)
                    b_end = b_r[:, l - 1:l] if d == 0 else b_r[:, 0:1]
                    w_end = jnp.exp(a_r - m_last)
                    c_ref[d, h] = _dot((ktf * w_end).astype(BF16), v)
                    n_ref[d, h:h + 1, :] = _row_times_kt(w_end, kt)
                    m_all = jnp.where((sub == d) & (lane == h), b_end + m_last, m_all)
            m_ref[...] = m_all


def _state_call(lay, qvos, kts, gpts, nh, dh):
    nl = len(qvos)
    nbp, s = lay.nbp, lay.sp
    di = nh * dh
    assert s == MLSTM_L

    def pick(jl):
        return lambda lyr, b: jnp.where(lyr == jl, b, jnp.where(lyr < jl, 0, nbp - 1))

    in_specs, args = [], []
    for jl in range(nl):
        pb = pick(jl)
        in_specs.append(pl.BlockSpec((di, s), lambda lyr, b, pb=pb: (0, pb(lyr, b))))
        in_specs.append(pl.BlockSpec((s, di), lambda lyr, b, pb=pb: (pb(lyr, b), 1)))
        in_specs.append(pl.BlockSpec((4 * nh, s), lambda lyr, b, pb=pb: (0, pb(lyr, b))))
        args += [kts[jl], qvos[jl], gpts[jl]]
    return pl.pallas_call(
        functools.partial(_state_kernel, nl=nl, nh=nh, dh=dh),
        out_shape=(
            jax.ShapeDtypeStruct((nbp, nl, 2, nh, dh, dh), F32),
            jax.ShapeDtypeStruct((nbp, nl, 2, nh, dh), F32),
            jax.ShapeDtypeStruct((nbp, nl, 2, nh), F32),
        ),
        grid=(nl, nbp),
        in_specs=in_specs,
        out_specs=(
            pl.BlockSpec((None, None, 2, nh, dh, dh), lambda lyr, b: (b, lyr, 0, 0, 0, 0)),
            pl.BlockSpec((None, None, 2, nh, dh), lambda lyr, b: (b, lyr, 0, 0, 0)),
            pl.BlockSpec((None, None, 2, nh), lambda lyr, b: (b, lyr, 0, 0)),
        ),
        compiler_params=_cparams(("arbitrary", "arbitrary")),
        name="mlstm_prompt_state",
    )(*args)


def _mm_res_kernel(ap_ref, as_ref, w_ref, x_ref, g_ref, nw_ref, sh_ref, sc_ref, o_ref, u_ref, *, lay, tm):
    i = pl.program_id(0)
    grp = lay.group(i * tm)
    gate = g_ref[pl.ds(grp, 1), :]
    w = w_ref[...]
    rc = min(tm, 256)
    for a_ref, cond in ((ap_ref, i * tm < lay.tp), (as_ref, i * tm >= lay.tp)):
        @pl.when(cond)
        def _(a_ref=a_ref):
            for r in range(0, tm, rc):
                xn = x_ref[r:r + rc, :] + gate * _dot(a_ref[r:r + rc, :], w)
                o_ref[r:r + rc, :] = xn
                u_ref[r:r + rc, :] = _mod_value(xn, nw_ref, sh_ref, sc_ref, grp).astype(BF16)


def _mm_res_call(lay, a_p, a_s, w_bf, x, mods, l, mod2):
    kdim = a_p.shape[1]
    t, d = x.shape
    tm = lay.row_tile(512)
    ntp = lay.tp // tm
    return pl.pallas_call(
        functools.partial(_mm_res_kernel, lay=lay, tm=tm),
        out_shape=(jax.ShapeDtypeStruct(x.shape, F32), jax.ShapeDtypeStruct(x.shape, BF16)),
        grid=(t // tm,),
        in_specs=[
            pl.BlockSpec((tm, kdim), lambda i: (jnp.minimum(i, ntp - 1), 0)),
            pl.BlockSpec((tm, kdim), lambda i: (jnp.maximum(i - ntp, 0), 0)),
            pl.BlockSpec((kdim, d), lambda i: (0, 0)),
            pl.BlockSpec((tm, d), lambda i: (i, 0)),
            lay.mod_spec(l, 2),
        ] + mod2.specs,
        out_specs=(pl.BlockSpec((tm, d), lambda i: (i, 0)), pl.BlockSpec((tm, d), lambda i: (i, 0))),
        compiler_params=_cparams(("arbitrary",)),
        name="mm_residual",
    )(a_p, a_s, w_bf, x, mods, *mod2.args)


def _fnet_kernel(x_ref, u_ref, u2_any, gt_ref, wc_ref, ds_ref, wo_ref, nw_ref, sh_ref, sc_ref, o_ref, u2_ref,
                 ab_scr, *, lay, row_base, groups, norm):
    s, d = x_ref.shape
    cg = d // groups
    grp = lay.group(row_base + pl.program_id(0) * s)
    wc = wc_ref[...]
    rc = min(s, 256)
    for g in range(groups):
        for r in range(0, s, rc):
            ab = _dot(u_ref[r:r + rc, g * cg:(g + 1) * cg], wc)
            ab_scr[r:r + rc, g * cg:(g + 1) * cg] = ab[:, :cg].astype(BF16)
            ab_scr[s + r:s + r + rc, g * cg:(g + 1) * cg] = ab[:, cg:].astype(BF16)
    gate = gt_ref[pl.ds(grp, 1), :]
    wo = wo_ref[...]
    for r in range(0, s, rc):
        y = _dot(ds_ref[r:r + rc, :], ab_scr[...]) * norm
        xn = x_ref[r:r + rc, :] + gate * _dot(y.astype(BF16), wo)
        o_ref[r:r + rc, :] = xn
        u2_ref[r:r + rc, :] = _mod_value(xn, nw_ref, sh_ref, sc_ref, grp).astype(BF16)


def _dft_mats(s, cg):
    kc = np.arange(cg)
    ang_c = 2.0 * np.pi * np.outer(kc, kc) / cg
    wc = np.concatenate([np.cos(ang_c), np.sin(ang_c)], axis=1)
    ks = np.arange(s)
    ang_s = 2.0 * np.pi * np.outer(ks, ks) / s
    ds = np.concatenate([np.cos(ang_s), -np.sin(ang_s)], axis=1)
    return jnp.asarray(wc, dtype=BF16), jnp.asarray(ds, dtype=BF16)


def _fnet_call(lay, x, u1, u2, mods, l, wo_bf, mod2, prompt):
    nb, s = (lay.nbp, lay.sp) if prompt else (lay.nbs, lay.ss)
    rb0 = 0 if prompt else lay.tp // s
    d = lay.d
    cg = d // FNET_GROUPS
    wc, ds = _dft_mats(s, cg)
    kern = functools.partial(_fnet_kernel, lay=lay, row_base=rb0 * s, groups=FNET_GROUPS,
                             norm=1.0 / math.sqrt(s * cg))
    blk = pl.BlockSpec((s, d), lambda b: (rb0 + b, 0))
    return pl.pallas_call(
        kern,
        out_shape=(jax.ShapeDtypeStruct(x.shape, F32), jax.ShapeDtypeStruct(u2.shape, BF16)),
        grid=(nb,),
        in_specs=[
            blk,
            blk,
            pl.BlockSpec(memory_space=pl.ANY),
            lay.mod_spec(l, 2),
            pl.BlockSpec((cg, 2 * cg), lambda b: (0, 0)),
            pl.BlockSpec((s, 2 * s), lambda b: (0, 0)),
            pl.BlockSpec((d, d), lambda b: (0, 0)),
        ] + mod2.specs,
        out_specs=(blk, blk),
        scratch_shapes=[pltpu.VMEM((2 * s, d), BF16)],
        input_output_aliases={0: 0, 2: 1},
        compiler_params=_cparams(("parallel",)),
        name="fnet_prompt" if prompt else "fnet_latent",
    )(x, u1, u2, mods, wc, ds, wo_bf, *mod2.args)


def _glu_kernel(u_ref, wa_ref, wg_ref, ba_ref, bg_ref, o_ref):
    u = u_ref[...]
    a = _dot(u, wa_ref[...]) + ba_ref[...]
    g = _dot(u, wg_ref[...]) + bg_ref[...]
    o_ref[...] = a * _sigmoid(g)


def _glu_call(lay, u, w_bf, bias):
    t, d = u.shape
    cd = w_bf.shape[1] // 2
    tm = lay.row_tile(1024)
    tn = 512
    nj = cd // tn
    return pl.pallas_call(
        _glu_kernel,
        out_shape=jax.ShapeDtypeStruct((t, cd), F32),
        grid=(t // tm, nj),
        in_specs=[
            pl.BlockSpec((tm, d), lambda i, j: (i, 0)),
            pl.BlockSpec((d, tn), lambda i, j: (0, j)),
            pl.BlockSpec((d, tn), lambda i, j: (0, nj + j)),
            pl.BlockSpec((1, tn), lambda i, j: (0, j)),
            pl.BlockSpec((1, tn), lambda i, j: (0, nj + j)),
        ],
        out_specs=pl.BlockSpec((tm, tn), lambda i, j: (i, j)),
        compiler_params=_cparams(("parallel", "parallel")),
        name="conv_glu",
    )(u, w_bf, w_bf, bias, bias)


def _conv_kernel(c_ref, p_ref, n_ref, wd_ref, bd_ref, lw_ref, lb_ref, w2_ref, b2_ref, x_ref, gt_ref,
                 nw_ref, sh_ref, sc_ref, o_ref, u2_ref, pad, conv, act, *, lay, rb, width):
    i = pl.program_id(0)
    row0 = i * rb
    grp = lay.group(row0)
    seq = jnp.where(row0 < lay.tp, lay.sp, lay.ss)
    pos = jnp.where(row0 < lay.tp, row0 % lay.sp, (row0 - lay.tp) % lay.ss)
    has_prev = (pos != 0).astype(F32)
    has_next = (pos + rb != seq).astype(F32)
    hl = CONV_HALO
    half = width // 2
    cd = c_ref.shape[1]
    span = pad.shape[1]
    pad[0, 0:hl, :] = p_ref[...] * has_prev
    pad[0, hl:hl + rb, :] = c_ref[...]
    pad[0, hl + rb:hl + rb + hl, :] = n_ref[...] * has_next
    for s in range(1, SUBLANES):
        pad[s, 0:span - SUBLANES, :] = pad[0, s:s + span - SUBLANES, :]
    ngrp = 8
    sub = ngrp * SUBLANES
    lanes = 2 * LANES
    assert rb % sub == 0 and cd % lanes == 0

    def conv_block(blk, carry):
        r0 = pl.multiple_of(blk * sub, sub)
        for c0 in range(0, cd, lanes):
            bias = bd_ref[:, c0:c0 + lanes]
            accs = [jnp.zeros((SUBLANES, lanes), F32) + bias for _ in range(ngrp)]
            for k in sorted(range(width), key=lambda k: ((hl - half + k) % SUBLANES, k)):
                q, s = divmod(hl - half + k, SUBLANES)
                wk = wd_ref[k, :, c0:c0 + lanes]
                for gi in range(ngrp):
                    win = pad[s, pl.ds(r0 + (q + gi) * SUBLANES, SUBLANES), c0:c0 + lanes]
                    accs[gi] = accs[gi] + win * wk
            conv[pl.ds(r0, sub), c0:c0 + lanes] = jnp.concatenate(accs, axis=0)
        return carry

    lax.fori_loop(0, rb // sub, conv_block, 0)
    lw = lw_ref[...]
    lb = lb_ref[...]
    lsub = min(rb, 16 * SUBLANES)

    def ln_block(blk, carry):
        r0 = pl.multiple_of(blk * lsub, lsub)
        acc = conv[pl.ds(r0, lsub), :]
        mu = jnp.mean(acc, axis=-1, keepdims=True)
        cen = acc - mu
        var = jnp.mean(cen * cen, axis=-1, keepdims=True)
        y = cen * lax.rsqrt(var + EPS) * lw + lb
        act[pl.ds(r0, lsub), :] = (y * _sigmoid(y)).astype(BF16)
        return carry

    lax.fori_loop(0, rb // lsub, ln_block, 0)
    xn = x_ref[...] + gt_ref[pl.ds(grp, 1), :] * (_dot(act[...], w2_ref[...]) + b2_ref[...])
    o_ref[...] = xn
    u2_ref[...] = _mod_value(xn, nw_ref, sh_ref, sc_ref, grp).astype(BF16)


def _conv_call(lay, glu, wd, bd, lw, lb, w2_bf, b2, x, mods, l, mod2):
    t, cd = glu.shape
    d = x.shape[1]
    rb = lay.row_tile(256)
    hl = CONV_HALO
    assert CONV_WIDTH // 2 <= hl and rb % hl == 0
    nhb = t // hl
    per = rb // hl
    wd_p = jnp.broadcast_to(wd[:, None, :], (CONV_WIDTH, SUBLANES, cd))
    row = lambda a: a.reshape(1, -1)
    rows = pl.BlockSpec((rb, d), lambda i: (i, 0))
    return pl.pallas_call(
        functools.partial(_conv_kernel, lay=lay, rb=rb, width=CONV_WIDTH),
        out_shape=(jax.ShapeDtypeStruct(x.shape, F32), jax.ShapeDtypeStruct(x.shape, BF16)),
        grid=(t // rb,),
        in_specs=[
            pl.BlockSpec((rb, cd), lambda i: (i, 0)),
            pl.BlockSpec((hl, cd), lambda i: (jnp.maximum(i * per - 1, 0), 0)),
            pl.BlockSpec((hl, cd), lambda i: (jnp.minimum((i + 1) * per, nhb - 1), 0)),
            pl.BlockSpec(wd_p.shape, lambda i: (0, 0, 0)),
            pl.BlockSpec((1, cd), lambda i: (0, 0)),
            pl.BlockSpec((1, cd), lambda i: (0, 0)),
            pl.BlockSpec((1, cd), lambda i: (0, 0)),
            pl.BlockSpec((cd, d), lambda i: (0, 0)),
            pl.BlockSpec((1, d), lambda i: (0, 0)),
            rows,
            lay.mod_spec(l, 2),
        ] + mod2.specs,
        out_specs=(rows, rows),
        scratch_shapes=[pltpu.VMEM((SUBLANES, rb + 2 * hl, cd), F32), pltpu.VMEM((rb, cd), F32),
                        pltpu.VMEM((rb, cd), BF16)],
        compiler_params=_cparams(("parallel",)),
        name="conv_dw_ln_pw2",
    )(glu, glu, glu, wd_p, row(bd), row(lw), row(lb), w2_bf, row(b2), x, mods, *mod2.args)


def _router_kernel(u_ref, wr_ref, upper_ref, ltri_ref, o_ref, cnt_ref, *, ng, ne):
    for j in range(o_ref.shape[0]):
        _route_tile(u_ref, wr_ref, upper_ref, ltri_ref, o_ref, cnt_ref, j, ng, ne)


def _route_tile(u_ref, wr_ref, upper_ref, ltri_ref, o_ref, cnt_ref, j, ng, ne):
    tm = upper_ref.shape[0]
    logits = _dot_nt(wr_ref[...], u_ref[j * tm:(j + 1) * tm, :])
    neg = -jnp.inf
    row = lax.broadcasted_iota(jnp.int32, (SUBLANES, tm), 0)

    gl = jnp.where(row < ng, logits[0:SUBLANES, :], neg)
    gmax = jnp.max(gl, axis=0, keepdims=True)
    gidx = jnp.min(jnp.where(gl == gmax, row, SUBLANES), axis=0, keepdims=True)
    g_p = 1.0 / jnp.sum(jnp.where(row < ng, jnp.exp(gl - gmax), 0.0), axis=0, keepdims=True)

    sel = logits[SUBLANES:2 * SUBLANES, :]
    for g in range(1, ng):
        sel = jnp.where(gidx == g, logits[(1 + g) * SUBLANES:(2 + g) * SUBLANES, :], sel)
    v1 = jnp.max(sel, axis=0, keepdims=True)
    i1 = jnp.min(jnp.where(sel == v1, row, SUBLANES), axis=0, keepdims=True)
    sel2 = jnp.where(row == i1, neg, sel)
    v2 = jnp.max(sel2, axis=0, keepdims=True)
    i2 = jnp.min(jnp.where(sel2 == v2, row, SUBLANES), axis=0, keepdims=True)
    e1 = gidx * SUBLANES + i1
    e2 = gidx * SUBLANES + i2
    tt = jnp.exp(v2 - v1)
    p1 = 1.0 / (1.0 + tt)
    gate1 = p1 * g_p
    gate2 = (tt * p1) * g_p

    rowe = lax.broadcasted_iota(jnp.int32, (ne, tm), 0)
    oh1 = rowe == e1
    oh2 = rowe == e2
    oh = jnp.where(oh1 | oh2, 1.0, 0.0)
    groups = jnp.floor((jnp.sum(oh, axis=1, keepdims=True) + (SUBLANES - 1)) * (1.0 / SUBLANES))
    groups_b = jnp.broadcast_to(groups, (ne, LANES))
    start = SUBLANES * _dot(ltri_ref[...], groups_b.astype(BF16))[:, 0:1]
    prefix = _dot(oh.astype(BF16), upper_ref[...]) + start
    pos1 = jnp.sum(jnp.where(oh1, prefix, 0.0), axis=0, keepdims=True)
    pos2 = jnp.sum(jnp.where(oh2, prefix, 0.0), axis=0, keepdims=True)
    cnt_ref[j] = groups_b

    out = jnp.where(row == 0, e1.astype(F32), 0.0)
    out = jnp.where(row == 1, e2.astype(F32), out)
    out = jnp.where(row == 2, gate1, out)
    out = jnp.where(row == 3, gate2, out)
    out = jnp.where(row == 4, pos1, out)
    out = jnp.where(row == 5, pos2, out)
    o_ref[j] = out


def _router_call(u, wr_t, upper, ltri, ng, ne):
    t, d = u.shape
    tm = upper.shape[0]
    nt = t // tm
    per = 2 if nt % 2 == 0 else 1
    assert ne // ng == SUBLANES and ng <= SUBLANES and SUBLANES + ne <= LANES
    return pl.pallas_call(
        functools.partial(_router_kernel, ng=ng, ne=ne),
        out_shape=(jax.ShapeDtypeStruct((nt, SUBLANES, tm), F32), jax.ShapeDtypeStruct((nt, ne, LANES), F32)),
        grid=(nt // per,),
        in_specs=[
            pl.BlockSpec((per * tm, d), lambda i: (i, 0)),
            pl.BlockSpec((LANES, d), lambda i: (0, 0)),
            pl.BlockSpec((tm, tm), lambda i: (0, 0)),
            pl.BlockSpec((ne, ne), lambda i: (0, 0)),
        ],
        out_specs=(pl.BlockSpec((per, SUBLANES, tm), lambda i: (i, 0, 0)),
                   pl.BlockSpec((per, ne, LANES), lambda i: (i, 0, 0))),
        compiler_params=_cparams(("parallel",)),
        name="moe_router",
    )(u, wr_t, upper, ltri)


def _pack_halves(lo, hi):
    lo_bits = lax.shift_right_logical(pltpu.bitcast(lo, U32), jnp.uint32(16))
    hi_bits = pltpu.bitcast(hi, U32) & jnp.uint32(0xFFFF0000)
    return hi_bits | lo_bits


def _unpack_halves(w):
    lo = pltpu.bitcast(lax.shift_left(w, jnp.uint32(16)), F32)
    hi = pltpu.bitcast(w & jnp.uint32(0xFFFF0000), F32)
    return lo.astype(BF16), hi.astype(BF16)


def _round_bf16(x):
    return x.astype(BF16).astype(F32)


def _group_copy(src, src_g, dst, dst_g, sem):
    g8 = lambda g: pl.ds(pl.multiple_of(g * SUBLANES, SUBLANES), SUBLANES)
    return pltpu.make_async_copy(src.at[g8(src_g), :], dst.at[g8(dst_g), :], sem)


def _for_groups(n, fn, unroll=4):
    def body_many(i, c):
        for j in range(unroll):
            fn(i * unroll + j)
        return c

    def body_one(g, c):
        fn(g)
        return c

    full = lax.div(n, jnp.int32(unroll))
    lax.fori_loop(0, full, body_many, 0)
    lax.fori_loop(full * unroll, n, body_one, 0)


def _dispatch_kernel(gdst_ref, ngt_ref, pad0_ref, npad_ref, tail_ref, u_ref, pos_ref, xs_out, loc, zeros, sem,
                     zsem, *, tm, nl, ne):
    i = pl.program_id(0)
    nt = pl.num_programs(0)
    slot = i % 2
    nlg = nl // SUBLANES

    def copy(step, s, g):
        return _group_copy(loc.at[s], g, xs_out, gdst_ref[step * nlg + g], sem.at[s])

    def group_wait(s):
        _group_copy(loc.at[s], 0, xs_out, 0, sem.at[s]).wait()

    def zero_pad(e, start):
        off = pad0_ref[e]
        size = MOE_BM // SUBLANES // 2
        while size:
            bit = (npad_ref[e] & size) != 0
            nrow = size * SUBLANES

            @pl.when(bit)
            def _(off=off, nrow=nrow):
                first = pl.multiple_of(off * SUBLANES, SUBLANES) if start else 0
                c = pltpu.make_async_copy(zeros.at[0:nrow, :], xs_out.at[pl.ds(first, nrow), :], zsem)
                c.start() if start else c.wait()

            off = off + jnp.where(bit, size, 0)
            size //= 2

    def zero_block(t):
        first = pl.multiple_of(tail_ref[0] + t * MOE_BM, MOE_BM)
        return pltpu.make_async_copy(zeros, xs_out.at[pl.ds(first, MOE_BM), :], zsem)

    @pl.when(i == 0)
    def _():
        zeros[...] = jnp.zeros(zeros.shape, zeros.dtype)
        for e in range(ne):
            zero_pad(e, True)
        _for_groups(tail_ref[1], lambda t: zero_block(t).start())

    @pl.when(i >= 2)
    def _():
        _for_groups(ngt_ref[i - 2], lambda g: group_wait(slot))

    pos1 = pos_ref[4:5, :].astype(jnp.int32)
    pos2 = pos_ref[5:6, :].astype(jnp.int32)
    half = u_ref.shape[1] // 2
    u = u_ref[...]
    rc = 256
    for r in range(0, nl, rc):
        p = lax.broadcasted_iota(jnp.int32, (rc, tm), 0) + r
        onehot = jnp.where((p == pos1) | (p == pos2), 1.0, 0.0).astype(BF16)
        rows = _dot(onehot, u)
        loc[slot, r:r + rc, :] = _pack_halves(rows[:, :half], rows[:, half:])
    _for_groups(ngt_ref[i], lambda g: copy(i, slot, g).start())

    @pl.when(i == nt - 1)
    def _():
        @pl.when(i >= 1)
        def _():
            _for_groups(ngt_ref[i - 1], lambda g: group_wait(1 - slot))

        _for_groups(ngt_ref[i], lambda g: group_wait(slot))
        for e in range(ne):
            zero_pad(e, False)
        _for_groups(tail_ref[1], lambda t: pltpu.make_async_copy(zeros, xs_out.at[0:MOE_BM, :], zsem).wait())


def _dispatch_call(u, pos_rows, gdst, ngt, pad0, npad, tail, nrows, tm, nl):
    t, d = u.shape
    ne = npad.shape[0]
    grid_spec = pltpu.PrefetchScalarGridSpec(
        num_scalar_prefetch=5,
        grid=(t // tm,),
        in_specs=[
            pl.BlockSpec((tm, d), lambda i, *_: (i, 0)),
            pl.BlockSpec((None, SUBLANES, tm), lambda i, *_: (i, 0, 0)),
        ],
        out_specs=pl.BlockSpec(memory_space=pl.ANY),
        scratch_shapes=[pltpu.VMEM((2, nl, d // 2), U32), pltpu.VMEM((MOE_BM, d // 2), U32),
                        pltpu.SemaphoreType.DMA((2,)), pltpu.SemaphoreType.DMA(())],
    )
    return pl.pallas_call(
        functools.partial(_dispatch_kernel, tm=tm, nl=nl, ne=ne),
        out_shape=jax.ShapeDtypeStruct((nrows, d // 2), U32),
        grid_spec=grid_spec,
        compiler_params=_cparams(("arbitrary",)),
        name="moe_dispatch",
    )(gdst, ngt, pad0, npad, tail, u, pos_rows)


def _expert_kernel(row0_ref, nblk_ref, rows_ref, tail_ref, x_hbm, w13_ref, w2_ref, y_hbm, xbuf, ybuf, w13_bf, w2_bf,
                   xsem, ysem, *, hid):
    e = pl.program_id(0)
    n = nblk_ref[e]
    g0 = row0_ref[e] // MOE_BM
    total = tail_ref[0] // MOE_BM
    nx, ny = xbuf.shape[0], ybuf.shape[0]
    ahead = nx - 1
    half = xbuf.shape[2]

    def rows(g):
        return pl.ds(pl.multiple_of(g * MOE_BM, MOE_BM), MOE_BM)

    def x_copy(g):
        return pltpu.make_async_copy(x_hbm.at[rows(g), :], xbuf.at[g % nx], xsem.at[g % nx])

    def y_copy(g, s):
        return pltpu.make_async_copy(ybuf.at[s], y_hbm.at[rows(g), :], ysem.at[s])

    @pl.when(e == 0)
    def _():
        for g in range(ahead):
            @pl.when(g < total)
            def _(g=g):
                x_copy(g).start()

    @pl.when(n > 0)
    def _():
        w13_bf[...] = w13_ref[...].astype(BF16)
        w2_bf[...] = w2_ref[...].astype(BF16)

        def block(c, carry):
            g = g0 + c
            x_copy(g).wait()

            @pl.when(g + ahead < total)
            def _():
                x_copy(g + ahead).start()

            @pl.when(g >= ny)
            def _():
                y_copy(g - ny, g % ny).wait()

            def compute(nrows):
                x_lo, x_hi = _unpack_halves(xbuf[g % nx, 0:nrows, :])
                hb = _dot(x_lo, w13_bf[:half, :]) + _dot(x_hi, w13_bf[half:, :])
                a = hb[:, :hid]
                act = (a * _sigmoid(a)) * hb[:, hid:]
                y = _round_bf16(_dot(act.astype(BF16), w2_bf[...]))
                ybuf[g % ny, 0:nrows, :] = _pack_halves(y[:, :half], y[:, half:])
                if nrows < MOE_BM:
                    ybuf[g % ny, nrows:, :] = jnp.zeros((MOE_BM - nrows, half), ybuf.dtype)

            valid = rows_ref[e] - c * MOE_BM
            for nrows, cond in ((MOE_BM, valid > MOE_BM // 2), (MOE_BM // 2, valid <= MOE_BM // 2)):
                @pl.when(cond)
                def _(nrows=nrows):
                    compute(nrows)

            y_copy(g, g % ny).start()
            return carry

        lax.fori_loop(0, n, block, 0)

    @pl.when(e == pl.num_programs(0) - 1)
    def _():
        for j in range(ny):
            @pl.when(total - ny + j >= 0)
            def _(j=j):
                g = total - ny + j
                y_copy(g, g % ny).wait()

        ybuf[0] = jnp.zeros(ybuf.shape[1:], ybuf.dtype)
        _for_groups(tail_ref[1], lambda t: y_copy(total + t, 0).start())
        _for_groups(tail_ref[1], lambda t: y_copy(total + t, 0).wait())


def _expert_call(xs, row0, nblk, nrows, tail, w13, w2, l):
    r, half = xs.shape
    d = 2 * half
    ne = w13.shape[1]
    hid = w2.shape[2]
    nx, ny = 5, 3
    grid_spec = pltpu.PrefetchScalarGridSpec(
        num_scalar_prefetch=4,
        grid=(ne,),
        in_specs=[
            pl.BlockSpec(memory_space=pl.ANY),
            pl.BlockSpec((None, None, d, 2 * hid), lambda e, *_: (l, e, 0, 0)),
            pl.BlockSpec((None, None, hid, d), lambda e, *_: (l, e, 0, 0)),
        ],
        out_specs=pl.BlockSpec(memory_space=pl.ANY),
        scratch_shapes=[
            pltpu.VMEM((nx, MOE_BM, half), U32), pltpu.VMEM((ny, MOE_BM, half), U32),
            pltpu.VMEM((d, 2 * hid), BF16), pltpu.VMEM((hid, d), BF16),
            pltpu.SemaphoreType.DMA((nx,)), pltpu.SemaphoreType.DMA((ny,)),
        ],
    )
    return pl.pallas_call(
        functools.partial(_expert_kernel, hid=hid),
        out_shape=jax.ShapeDtypeStruct((r, half), U32),
        grid_spec=grid_spec,
        compiler_params=_cparams(("arbitrary",)),
        name="moe_experts",
    )(row0, nblk, nrows, tail, xs, w13, w2)


def _combine_kernel(gdst_ref, ngt_ref, x_ref, rt_ref, gt_ref, nw_ref, sh_ref, sc_ref, ys_ref, *rest,
                    lay, tm, nl, final, gated):
    if gated:
        wh_ref, wl_ref, out_a, out_b, out_g, loc, sem = rest
    else:
        out_a, out_b, loc, sem = rest
    i = pl.program_id(0)
    nt = pl.num_programs(0)
    slot = i % 2
    nlg = nl // SUBLANES

    def copy(step, s, g):
        return _group_copy(ys_ref, gdst_ref[step * nlg + g], loc.at[s], g, sem.at[s])

    @pl.when(i == 0)
    def _():
        loc[...] = jnp.zeros(loc.shape, loc.dtype)
        _for_groups(ngt_ref[0], lambda g: copy(0, 0, g).start())

    _for_groups(ngt_ref[i], lambda g: _group_copy(ys_ref, 0, loc.at[slot], 0, sem.at[slot]).wait())

    @pl.when(i + 1 < nt)
    def _():
        _for_groups(ngt_ref[i + 1], lambda g: copy(i + 1, 1 - slot, g).start())

    rt = rt_ref[...]
    gate1, gate2 = rt[:, 2:3], rt[:, 3:4]
    pos1, pos2 = rt[:, 4:5].astype(jnp.int32), rt[:, 5:6].astype(jnp.int32)
    half = loc.shape[2]
    rc = 256
    mix_lo = jnp.zeros((tm, half), F32)
    mix_hi = jnp.zeros((tm, half), F32)
    for r in range(0, nl, rc):
        p = lax.broadcasted_iota(jnp.int32, (tm, rc), 1) + r
        wgt = (jnp.where(p == pos1, gate1, 0.0) + jnp.where(p == pos2, gate2, 0.0)).astype(BF16)
        y_lo, y_hi = _unpack_halves(loc[slot, r:r + rc, :])
        mix_lo = mix_lo + _dot(wgt, y_lo)
        mix_hi = mix_hi + _dot(wgt, y_hi)
    grp = lay.group(i * tm)
    gate = gt_ref[pl.ds(grp, 1), :]
    nw = nw_ref[...]
    sh = sh_ref[pl.ds(grp, 1), :]
    sc = sc_ref[pl.ds(grp, 1), :]
    er = 128 if gated else tm
    for r in range(0, tm, er):
        rs = slice(r, r + er)
        x_lo = x_ref[rs, :half] + gate[:, :half] * mix_lo[rs]
        x_hi = x_ref[rs, half:] + gate[:, half:] * mix_hi[rs]
        ms = (jnp.sum(x_lo * x_lo, axis=-1, keepdims=True)
              + jnp.sum(x_hi * x_hi, axis=-1, keepdims=True)) / (2 * half)
        inv = lax.rsqrt(ms + EPS)
        if final:
            y_lo = x_lo * inv * nw[:, :half]
            y_hi = x_hi * inv * nw[:, half:]
            for ref, cond in ((out_a, i * tm < lay.tp), (out_b, i * tm >= lay.tp)):
                @pl.when(cond)
                def _(ref=ref, y_lo=y_lo, y_hi=y_hi):
                    ref[rs, :half] = y_lo
                    ref[rs, half:] = y_hi
            continue
        out_a[rs, :half] = x_lo
        out_a[rs, half:] = x_hi
        u_lo = x_lo * inv * nw[:, :half] * (1.0 + sc[:, :half]) + sh[:, :half]
        u_hi = x_hi * inv * nw[:, half:] * (1.0 + sc[:, half:]) + sh[:, half:]
        if gated:
            (a_hi, a_lo), (b_hi, b_lo) = _split2(u_lo), _split2(u_hi)
            wh, wl = wh_ref[...], wl_ref[...]
            out_g[rs, :] = (_dot_nt(a_hi, wh[:, :half]) + _dot_nt(b_hi, wh[:, half:])
                            + _dot_nt(a_lo, wh[:, :half]) + _dot_nt(b_lo, wh[:, half:])
                            + _dot_nt(a_hi, wl[:, :half]) + _dot_nt(b_hi, wl[:, half:]))
            out_b[rs, :half] = a_hi
            out_b[rs, half:] = b_hi
        else:
            out_b[rs, :half] = u_lo.astype(BF16)
            out_b[rs, half:] = u_hi.astype(BF16)


def _combine_call(lay, x, route, mods, l, mod_next, gdst, ngt, ys, tm, nl, final, gate_w=None):
    t, d = x.shape
    rows = pl.BlockSpec((tm, d), lambda i, *_: (i, 0))
    gated = gate_w is not None
    extra_in, extra_args = [], ()
    if final:
        ntp = lay.tp // tm
        out_specs = (pl.BlockSpec((tm, d), lambda i, *_: (jnp.minimum(i, ntp - 1), 0)),
                     pl.BlockSpec((tm, d), lambda i, *_: (jnp.maximum(i - ntp, 0), 0)))
        out_shape = (jax.ShapeDtypeStruct((lay.tp, d), F32), jax.ShapeDtypeStruct((lay.ts, d), F32))
    else:
        out_specs = (rows, rows)
        out_shape = (jax.ShapeDtypeStruct(x.shape, F32), jax.ShapeDtypeStruct(x.shape, BF16))
        if gated:
            wspec = pl.BlockSpec((LANES, d), lambda i, *_: (0, 0))
            extra_in, extra_args = [wspec, wspec], tuple(gate_w)
            out_specs += (pl.BlockSpec((tm, LANES), lambda i, *_: (i, 0)),)
            out_shape += (jax.ShapeDtypeStruct((t, LANES), F32),)
    grid_spec = pltpu.PrefetchScalarGridSpec(
        num_scalar_prefetch=2,
        grid=(t // tm,),
        in_specs=[
            rows,
            pl.BlockSpec((tm, SUBLANES), lambda i, *_: (i, 0)),
            lay.mod_spec(l, 5),
        ] + mod_next.specs + [pl.BlockSpec(memory_space=pl.ANY)] + extra_in,
        out_specs=out_specs,
        scratch_shapes=[pltpu.VMEM((2, nl, d // 2), U32), pltpu.SemaphoreType.DMA((2,))],
    )
    return pl.pallas_call(
        functools.partial(_combine_kernel, lay=lay, tm=tm, nl=nl, final=final, gated=gated),
        out_shape=out_shape,
        grid_spec=grid_spec,
        compiler_params=_cparams(("arbitrary",)),
        name="moe_combine_final" if final else "moe_combine",
    )(gdst, ngt, x, route, mods, *mod_next.args, ys, *extra_args)


def _moe_layer(lay, x, u2, mods, l, mod_next, final, wr_t, upper, ltri, w13, w2, ng, ne, gate_w=None):
    t, d = x.shape
    tm = upper.shape[0]
    nt = t // tm
    bmg = MOE_BM // SUBLANES
    nl = -(-(MOE_TOP_K * tm + ne * (SUBLANES - 1)) // 256) * 256
    nlg = nl // SUBLANES
    route_t, counts = _router_call(u2, wr_t, upper, ltri, ng, ne)
    route = jnp.swapaxes(route_t, 1, 2).reshape(t, SUBLANES)

    c8 = counts[:, :, 0].astype(jnp.int32)
    lend = jnp.cumsum(c8, axis=1)
    lstart = lend - c8
    ngt = lend[:, -1].astype(jnp.int32)
    tot = jnp.sum(c8, axis=0)
    padded = (tot + bmg - 1) // bmg * bmg
    gend = jnp.cumsum(padded)
    gbase = (gend - padded)[None, :] + jnp.cumsum(c8, axis=0) - c8
    nb = -(-(MOE_TOP_K * t + nt * ne * (SUBLANES - 1)) // MOE_BM) + ne
    g = jnp.arange(nlg, dtype=jnp.int32)[None, :, None]
    owner = (g >= lstart[:, None, :]) & (g < lend[:, None, :])
    gdst = g[:, :, 0] + jnp.sum(jnp.where(owner, (gbase - lstart)[:, None, :], 0), axis=-1)
    gdst = gdst.reshape(nt * nlg).astype(jnp.int32)
    row0 = ((gend - padded) * SUBLANES).astype(jnp.int32)
    nblk = (padded // bmg).astype(jnp.int32)
    tail = jnp.stack([gend[-1] * SUBLANES, nb - gend[-1] // bmg]).astype(jnp.int32)

    pad0 = (gend - padded + tot).astype(jnp.int32)
    npad = (padded - tot).astype(jnp.int32)
    xs = _dispatch_call(u2, route_t, gdst, ngt, pad0, npad, tail, nb * MOE_BM, tm, nl)
    ys = _expert_call(xs, row0, nblk, (tot * SUBLANES).astype(jnp.int32), tail, w13, w2, l)
    return _combine_call(lay, x, route, mods, l, mod_next, gdst, ngt, ys, tm, nl, final, gate_w)


def _lower_tri(n, strict):
    r = np.arange(n)
    m = (r[None, :] < r[:, None]) if strict else (r[None, :] <= r[:, None])
    return jnp.asarray(m.astype(np.float32), dtype=BF16)


def kernel(x_prompt, x_sample, state_C, state_n, state_m, c, c_ctx, ada_w, ada_b, norm1_w, norm2_w, m_w_in, m_b_gate, m_head_norm_w, m_w_out, f_w_out, cv_w_pw1, cv_b_pw1, cv_w_dw, cv_b_dw, cv_ln_w, cv_ln_b, cv_w_pw2, cv_b_pw2, r_w_group, r_w_expert, e_w13, e_w2, final_norm_w):
    nbp, sp, d = x_prompt.shape
    nbs, ss, _ = x_sample.shape
    assert ss % GRID_W == 0
    lay = _Layout(nbp, sp, nbs, ss, d)
    depth = ada_w.shape[0]
    nh, dh = state_C.shape[3], state_C.shape[4]
    di = nh * dh
    ng = r_w_group.shape[2]
    ne = r_w_expert.shape[2]
    assert ng == MOE_GROUPS and ng + ne <= LANES and 4 * nh <= LANES and MOE_TOP_K == 2

    cv = jnp.zeros((lay.ngp, d), F32).at[0].set(c_ctx).at[1:1 + nbs].set(c)
    mods = _ada_call(cv, ada_w, ada_b)

    tri_l = _lower_tri(MLSTM_L, strict=False)
    upper = _lower_tri(lay.row_tile(512), strict=True).T
    ltri = _lower_tri(ne, strict=True)
    row = lambda a: a.reshape(1, -1)
    w_in_t = jnp.swapaxes(m_w_in, 1, 2)
    tn = 1024
    assert di % tn == 0
    nkb = di // tn

    def gate_weights(j):
        wg = jnp.zeros((LANES, d), F32).at[:4 * nh].set(w_in_t[j, 4 * di:])
        return _split2(wg)

    x, u1, graw = _prep_call(lay, x_prompt.reshape(lay.tp, d), x_sample.reshape(lay.ts, d),
                             _Mod(lay, mods, norm1_w[0], 0, 0), gate_weights(0))
    y = None
    qvos, kts, gpts = [], [], []
    for l in range(depth):
        j, kind = l // N_MIXERS, l % N_MIXERS
        mod2 = _Mod(lay, mods, norm2_w[l], l, 3)
        if kind == 0:
            bg = jnp.zeros((1, LANES), F32).at[0, :4 * nh].set(m_b_gate[j])
            gp = _gates_call(lay, graw, bg, tri_l, nh)
            gpt = gp[:, :4 * nh].T
            qvo_blocks = list(range(nkb)) + list(range(2 * nkb, 4 * nkb))
            qvo = _proj_call(lay, u1, w_in_t, j, qvo_blocks, tn, transposed=False)
            kt = _proj_call(lay, u1, w_in_t, j, list(range(nkb, 2 * nkb)), tn, transposed=True)
            hw = row(m_head_norm_w[j])
            hg_p = _mlstm_call(lay, qvo, kt, gp, gpt, hw, nh, dh, prompt=True)
            n0 = state_n[:, j].reshape(nbs, 2 * nh, dh)
            m0 = jnp.broadcast_to(state_m[:, j].reshape(nbs, 2 * nh, 1), (nbs, 2 * nh, LANES))
            hg_s = _mlstm_call(lay, qvo, kt, gp, gpt, hw, nh, dh, prompt=False, state=(state_C, j, n0, m0))
            x, u2 = _mm_res_call(lay, hg_p, hg_s, m_w_out[j].astype(BF16), x, mods, l, mod2)
            qvos.append(qvo)
            kts.append(kt)
            gpts.append(gpt)
        elif kind == 1:
            wo = f_w_out[j].astype(BF16)
            u2 = jnp.zeros((lay.t, d), BF16)
            x, u2 = _fnet_call(lay, x, u1, u2, mods, l, wo, mod2, prompt=True)
            x, u2 = _fnet_call(lay, x, u1, u2, mods, l, wo, mod2, prompt=False)
        else:
            glu = _glu_call(lay, u1, cv_w_pw1[j].astype(BF16), row(cv_b_pw1[j]))
            x, u2 = _conv_call(lay, glu, cv_w_dw[j], cv_b_dw[j], cv_ln_w[j], cv_ln_b[j], cv_w_pw2[j].astype(BF16),
                               cv_b_pw2[j], x, mods, l, mod2)
        wr = jnp.zeros((LANES, d), F32).at[:ng].set(r_w_group[l].T).at[SUBLANES:SUBLANES + ne].set(r_w_expert[l].T)
        final = l + 1 == depth
        mod_next = _Mod(lay, mods, final_norm_w, l, 0) if final else _Mod(lay, mods, norm1_w[l + 1], l + 1, 0)
        next_mlstm = not final and (l + 1) % N_MIXERS == 0
        outs = _moe_layer(lay, x, u2, mods, l, mod_next, final, wr.astype(BF16), upper, ltri, e_w13, e_w2, ng, ne,
                          gate_weights((l + 1) // N_MIXERS) if next_mlstm else None)
        if final:
            y = outs
        elif next_mlstm:
            x, u1, graw = outs
        else:
            x, u1 = outs

    y_prompt = y[0].reshape(nbp, sp, d)
    y_sample = y[1].reshape(nbs, ss, d)
    new_c, new_n, new_m = _state_call(lay, qvos, kts, gpts, nh, dh)
    return (y_prompt, y_sample, new_c, new_n, new_m)
```

```python
import functools
import math

import numpy as np
import jax
import jax.numpy as jnp
from jax import lax
from jax.experimental import pallas as pl
from jax.experimental.pallas import tpu as pltpu

F32 = jnp.float32
BF16 = jnp.bfloat16
U32 = jnp.uint32
EPS = 1e-6
GRID_W = 64
N_MIXERS = 3
FNET_GROUPS = 4
CONV_WIDTH = 31
MOE_GROUPS = 4
MOE_TOP_K = 2

LANES = 128
SUBLANES = 8
MLSTM_L = 256
MOE_BM = 256
CONV_HALO = 16
VMEM_LIMIT = 56 * 1024 * 1024


def _cparams(sem, vmem=VMEM_LIMIT):
    return pltpu.CompilerParams(dimension_semantics=sem, vmem_limit_bytes=vmem)


def _dot(a, b):
    return jnp.dot(a, b, preferred_element_type=F32)


def _dot_nt(a, b):
    return lax.dot_general(a, b, (((1,), (1,)), ((), ())), preferred_element_type=F32)


def _rms(x, w):
    return x * lax.rsqrt(jnp.mean(x * x, axis=-1, keepdims=True) + EPS) * w


def _modulate(x, w, shift, scale):
    return _rms(x, w) * (1.0 + scale) + shift


def _sigmoid(x):
    return 1.0 / (1.0 + jnp.exp(-x))


def _log_sigmoid(x):
    return jnp.minimum(x, 0.0) - jnp.log(1.0 + jnp.exp(-jnp.abs(x)))


def _split2(x):
    hi = x.astype(BF16)
    return hi, (x - hi.astype(F32)).astype(BF16)


def _split3(x):
    hi = x.astype(BF16)
    r1 = x - hi.astype(F32)
    mid = r1.astype(BF16)
    return hi, mid, (r1 - mid.astype(F32)).astype(BF16)


class _Layout:
    def __init__(self, nbp, sp, nbs, ss, d):
        self.nbp, self.sp, self.nbs, self.ss, self.d = nbp, sp, nbs, ss, d
        self.tp, self.ts = nbp * sp, nbs * ss
        self.t = self.tp + self.ts
        assert self.tp % ss == 0, "latent sequences must start on a block boundary of their own length"
        self.ngp = -(-(1 + nbs) // SUBLANES) * SUBLANES

    def group(self, row0):
        return jnp.where(row0 < self.tp, 0, 1 + (row0 - self.tp) // self.ss)

    def row_tile(self, want):
        tm = math.gcd(math.gcd(self.tp, self.ss), want)
        assert tm % SUBLANES == 0
        return tm

    def mod_spec(self, l, chunk):
        return pl.BlockSpec((None, self.ngp, self.d), lambda *_: (l, 0, chunk))

    def row_spec(self):
        return pl.BlockSpec((1, self.d), lambda *_: (0, 0))


class _Mod:
    def __init__(self, lay, mods, nw, l, c_shift):
        self.args = (nw.reshape(1, -1), mods, mods)
        self.specs = [lay.row_spec(), lay.mod_spec(l, c_shift), lay.mod_spec(l, c_shift + 1)]


def _mod_value(x, nw_ref, sh_ref, sc_ref, grp):
    return _modulate(x, nw_ref[...], sh_ref[pl.ds(grp, 1), :], sc_ref[pl.ds(grp, 1), :])


def _ada_kernel(cv_ref, w_ref, b_ref, o_ref):
    s = cv_ref[...]
    s = s * _sigmoid(s)
    o_ref[...] = _dot(s.astype(BF16), w_ref[...].astype(BF16)) + b_ref[...]


def _ada_call(cv, ada_w, ada_b):
    depth, d, n = ada_w.shape
    ngp = cv.shape[0]
    tn = min(n, 2048)
    return pl.pallas_call(
        _ada_kernel,
        out_shape=jax.ShapeDtypeStruct((depth, ngp, n), F32),
        grid=(depth, n // tn),
        in_specs=[
            pl.BlockSpec((ngp, d), lambda l, j: (0, 0)),
            pl.BlockSpec((None, d, tn), lambda l, j: (l, 0, j)),
            pl.BlockSpec((None, 1, tn), lambda l, j: (l, 0, j)),
        ],
        out_specs=pl.BlockSpec((None, ngp, tn), lambda l, j: (l, 0, j)),
        compiler_params=_cparams(("parallel", "parallel")),
        name="ada_mods",
    )(cv, ada_w, ada_b.reshape(depth, 1, n))


def _gate_raw(u, wh, wl):
    u_hi, u_lo = _split2(u)
    return _dot_nt(u_hi, wh) + _dot_nt(u_lo, wh) + _dot_nt(u_hi, wl), u_hi


def _prep_kernel(xp_ref, xs_ref, nw_ref, sh_ref, sc_ref, wh_ref, wl_ref, x_ref, u_ref, g_ref, *, lay, tm):
    i = pl.program_id(0)
    grp = lay.group(i * tm)
    for src, cond in ((xp_ref, i * tm < lay.tp), (xs_ref, i * tm >= lay.tp)):
        @pl.when(cond)
        def _(src=src):
            rc = min(tm, 256)
            for r in range(0, tm, rc):
                x = src[r:r + rc, :]
                x_ref[r:r + rc, :] = x
                g, u_hi = _gate_raw(_mod_value(x, nw_ref, sh_ref, sc_ref, grp), wh_ref[...], wl_ref[...])
                u_ref[r:r + rc, :] = u_hi
                g_ref[r:r + rc, :] = g


def _prep_call(lay, xp, xs, mod, gate_w):
    d = lay.d
    tm = lay.row_tile(1024)
    ntp = lay.tp // tm
    rows = pl.BlockSpec((tm, d), lambda i: (i, 0))
    wspec = pl.BlockSpec((LANES, d), lambda i: (0, 0))
    return pl.pallas_call(
        functools.partial(_prep_kernel, lay=lay, tm=tm),
        out_shape=(jax.ShapeDtypeStruct((lay.t, d), F32), jax.ShapeDtypeStruct((lay.t, d), BF16),
                   jax.ShapeDtypeStruct((lay.t, LANES), F32)),
        grid=(lay.t // tm,),
        in_specs=[
            pl.BlockSpec((tm, d), lambda i: (jnp.minimum(i, ntp - 1), 0)),
            pl.BlockSpec((tm, d), lambda i: (jnp.maximum(i - ntp, 0), 0)),
        ] + mod.specs + [wspec, wspec],
        out_specs=(rows, rows, pl.BlockSpec((tm, LANES), lambda i: (i, 0))),
        compiler_params=_cparams(("arbitrary",)),
        name="prep_modulate",
    )(xp, xs, *mod.args, *gate_w)


def _gates_kernel(g_ref, b_ref, tri_ref, o_ref, *, nh, tm):
    l = tri_ref.shape[0]
    tri = tri_ref[...]
    lane = lax.broadcasted_iota(jnp.int32, (l, LANES), 1)
    is_f = ((lane >= nh) & (lane < 2 * nh)) | ((lane >= 3 * nh) & (lane < 4 * nh))
    is_a = (lane < nh) | ((lane >= 2 * nh) & (lane < 3 * nh))
    for r in range(0, tm, l):
        g = g_ref[r:r + l, :] + b_ref[...]
        lf = jnp.where(is_f, _log_sigmoid(g), 0.0)
        hi, mid, lo = _split3(lf)
        prefix = _dot(tri, hi) + _dot(tri, mid) + _dot(tri, lo)
        suffix = jnp.sum(lf, axis=0, keepdims=True) - prefix + lf
        b = jnp.where(lane < 2 * nh, prefix, suffix)
        a = g - pltpu.roll(b, LANES - nh, 1)
        o_ref[r:r + l, :] = jnp.where(is_a, a, b)


def _gates_call(lay, graw, bias, tri, nh):
    t = graw.shape[0]
    l = tri.shape[0]
    tm = lay.row_tile(2048)
    assert tm % l == 0
    return pl.pallas_call(
        functools.partial(_gates_kernel, nh=nh, tm=tm),
        out_shape=jax.ShapeDtypeStruct((t, LANES), F32),
        grid=(t // tm,),
        in_specs=[
            pl.BlockSpec((tm, LANES), lambda i: (i, 0)),
            pl.BlockSpec((1, LANES), lambda i: (0, 0)),
            pl.BlockSpec((l, l), lambda i: (0, 0)),
        ],
        out_specs=pl.BlockSpec((tm, LANES), lambda i: (i, 0)),
        compiler_params=_cparams(("parallel",)),
        name="mlstm_gates",
    )(graw, bias, tri)


def _proj_kernel(u_ref, w_ref, o_ref, w_bf, *, transposed):
    @pl.when(pl.program_id(1) == 0)
    def _():
        w_bf[...] = w_ref[...].astype(BF16)

    if transposed:
        o_ref[...] = _dot_nt(w_bf[...], u_ref[...]).astype(BF16)
    else:
        o_ref[...] = _dot_nt(u_ref[...], w_bf[...]).astype(BF16)


def _proj_call(lay, u, w_in_t, jl, blocks, tn, transposed):
    t, d = u.shape
    tm = 2048 if t % 2048 == 0 else lay.row_tile(1024)
    nb = len(blocks)
    first, gap_at, gap = blocks[0], None, 0
    for idx in range(1, nb):
        if blocks[idx] != blocks[idx - 1] + 1:
            assert gap_at is None
            gap_at, gap = idx, blocks[idx] - blocks[idx - 1] - 1
    wblk = (lambda j: first + j) if gap_at is None else (lambda j: first + j + jnp.where(j >= gap_at, gap, 0))
    if transposed:
        out_shape = jax.ShapeDtypeStruct((nb * tn, t), BF16)
        out_spec = pl.BlockSpec((tn, tm), lambda j, i: (j, i))
    else:
        out_shape = jax.ShapeDtypeStruct((t, nb * tn), BF16)
        out_spec = pl.BlockSpec((tm, tn), lambda j, i: (i, j))
    return pl.pallas_call(
        functools.partial(_proj_kernel, transposed=transposed),
        out_shape=out_shape,
        grid=(nb, t // tm),
        in_specs=[
            pl.BlockSpec((tm, d), lambda j, i: (i, 0)),
            pl.BlockSpec((None, tn, d), lambda j, i: (jl, wblk(j), 0)),
        ],
        out_specs=out_spec,
        scratch_shapes=[pltpu.VMEM((tn, d), BF16)],
        compiler_params=_cparams(("parallel", "arbitrary")),
        name="mlstm_proj_t" if transposed else "mlstm_proj",
    )(u, w_in_t)


def _col(tile, c):
    lane = lax.broadcasted_iota(jnp.int32, tile.shape, 1)
    return jnp.sum(jnp.where(lane == c, tile, 0.0), axis=-1, keepdims=True)


def _dir_masks(l):
    r = lax.broadcasted_iota(jnp.int32, (l, l), 0)
    c = lax.broadcasted_iota(jnp.int32, (l, l), 1)
    return c <= r, c >= r


def _head_epilogue(h, hw, o):
    hn = h * lax.rsqrt(jnp.mean(h * h, axis=-1, keepdims=True) + EPS) * hw
    return (hn * _sigmoid(o.astype(F32))).astype(BF16)


def _row_times_kt(w_row, kt):
    hi, lo = _split2(w_row)
    sub = lax.broadcasted_iota(jnp.int32, (SUBLANES, w_row.shape[1]), 0)
    stacked = jnp.where(sub == 0, hi.astype(F32), jnp.where(sub == 1, lo.astype(F32), 0.0)).astype(BF16)
    res = _dot_nt(stacked, kt)
    return res[0:1, :] + res[1:2, :]


def _mlstm_single_kernel(q_ref, kt_ref, v_ref, o_ref, gp_ref, gpt_ref, hw_ref, out_ref, *, nh, dh, scale):
    l = q_ref.shape[0]
    gp = gp_ref[...]
    masks = _dir_masks(l)
    for h in range(nh):
        cols = slice(h * dh, (h + 1) * dh)
        qk = _dot(q_ref[:, cols], kt_ref[cols, :])
        p = None
        for d in range(2):
            a_r = gpt_ref[2 * nh * d + h:2 * nh * d + h + 1, :]
            b_c = gp[:, 2 * nh * d + nh + h:2 * nh * d + nh + h + 1]
            g = jnp.where(masks[d], a_r, -jnp.inf)
            m = jnp.maximum(jnp.max(g, axis=-1, keepdims=True), 0.0)
            s = qk * jnp.exp(g - m)
            den = scale * jnp.sum(s, axis=-1, keepdims=True)
            inv = scale / jnp.maximum(jnp.abs(den), jnp.exp(-(b_c + m)))
            p = s * inv if p is None else p + s * inv
        hh = _dot(p.astype(BF16), v_ref[:, cols])
        out_ref[:, cols] = _head_epilogue(hh, hw_ref[:, cols], o_ref[:, cols])


def _mlstm_multi_kernel(q_ref, kt_ref, v_ref, o_ref, gp_ref, gpt_ref, hw_ref, c0_ref, n0_ref, m0_ref,
                        out_ref, cst, cbf, *, nh, nc, l, scale):
    h = pl.program_id(1)
    masks = _dir_masks(l)
    m_in = [[None] * nc for _ in range(2)]
    n_in = [[None] * nc for _ in range(2)]
    for d in range(2):
        cst[...] = c0_ref[d]
        n = n0_ref[pl.ds(d * nh + h, 1), :]
        m = m0_ref[pl.ds(d * nh + h, 1), 0:1]
        order = list(range(nc)) if d == 0 else list(range(nc - 1, -1, -1))
        for step, c in enumerate(order):
            m_in[d][c], n_in[d][c] = m, n
            cbf[d, c] = cst[...].astype(BF16)
            if step + 1 < nc:
                r0 = c * l
                a_r = gpt_ref[pl.ds(2 * nh * d + h, 1), r0:r0 + l]
                b_r = gpt_ref[pl.ds(2 * nh * d + nh + h, 1), r0:r0 + l]
                m_last = jnp.maximum(jnp.max(a_r, axis=-1, keepdims=True), m)
                b_end = b_r[:, l - 1:l] if d == 0 else b_r[:, 0:1]
                decay = jnp.exp(m - m_last)
                w_end = jnp.exp(a_r - m_last)
                kt = kt_ref[:, r0:r0 + l]
                cst[...] = decay * cst[...] + _dot((kt.astype(F32) * w_end).astype(BF16), v_ref[r0:r0 + l, :])
                n = decay * n + _row_times_kt(w_end, kt)
                m = b_end + m_last
    hw = hw_ref[...]
    for c in range(nc):
        r0 = c * l
        q = q_ref[r0:r0 + l, :]
        v = v_ref[r0:r0 + l, :]
        qk = _dot(q, kt_ref[:, r0:r0 + l])
        qf = q.astype(F32)
        gp = gp_ref[r0:r0 + l, :]
        p = inter = None
        for d in range(2):
            m, n = m_in[d][c], n_in[d][c]
            a_r = gpt_ref[pl.ds(2 * nh * d + h, 1), r0:r0 + l]
            b_c = _col(gp, 2 * nh * d + nh + h)
            g = jnp.where(masks[d], a_r, -jnp.inf)
            mt = jnp.maximum(jnp.max(g, axis=-1, keepdims=True), m)
            s = qk * jnp.exp(g - mt)
            w_prev = jnp.exp(m - mt)
            den = scale * (jnp.sum(s, axis=-1, keepdims=True) + w_prev * jnp.sum(qf * n, axis=-1, keepdims=True))
            inv = scale / jnp.maximum(jnp.abs(den), jnp.exp(-(b_c + mt)))
            term = (w_prev * inv) * _dot(q, cbf[d, c])
            p = s * inv if p is None else p + s * inv
            inter = term if inter is None else inter + term
        hh = _dot(p.astype(BF16), v) + inter
        out_ref[r0:r0 + l, :] = _head_epilogue(hh, hw, o_ref[r0:r0 + l, :])


def _mlstm_call(lay, qvo, kt, gp, gpt, hw, nh, dh, prompt, state=None):
    nb, s = (lay.nbp, lay.sp) if prompt else (lay.nbs, lay.ss)
    rb0 = 0 if prompt else lay.tp // s
    scale = dh ** -0.5
    di = nh * dh
    common_in = [
        pl.BlockSpec((s, dh), lambda b, h: (rb0 + b, h)),
        pl.BlockSpec((dh, s), lambda b, h: (h, rb0 + b)),
        pl.BlockSpec((s, dh), lambda b, h: (rb0 + b, nh + h)),
        pl.BlockSpec((s, dh), lambda b, h: (rb0 + b, 2 * nh + h)),
        pl.BlockSpec((s, LANES), lambda b, h: (rb0 + b, 0)),
        pl.BlockSpec((4 * nh, s), lambda b, h: (0, rb0 + b)),
        pl.BlockSpec((1, dh), lambda b, h: (0, h)),
    ]
    if prompt:
        assert s == MLSTM_L
        return pl.pallas_call(
            functools.partial(_mlstm_single_kernel, nh=nh, dh=dh, scale=scale),
            out_shape=jax.ShapeDtypeStruct((nb * s, di), BF16),
            grid=(nb,),
            in_specs=[
                pl.BlockSpec((s, di), lambda b: (rb0 + b, 0)),
                pl.BlockSpec((di, s), lambda b: (0, rb0 + b)),
                pl.BlockSpec((s, di), lambda b: (rb0 + b, 1)),
                pl.BlockSpec((s, di), lambda b: (rb0 + b, 2)),
                pl.BlockSpec((s, LANES), lambda b: (rb0 + b, 0)),
                pl.BlockSpec((4 * nh, s), lambda b: (0, rb0 + b)),
                pl.BlockSpec((1, di), lambda b: (0, 0)),
            ],
            out_specs=pl.BlockSpec((s, di), lambda b: (b, 0)),
            compiler_params=_cparams(("parallel",)),
            name="mlstm_prompt",
        )(qvo, kt, qvo, qvo, gp, gpt, hw)
    state_c, jl, n0, m0 = state
    nc = s // MLSTM_L
    return pl.pallas_call(
        functools.partial(_mlstm_multi_kernel, nh=nh, nc=nc, l=MLSTM_L, scale=scale),
        out_shape=jax.ShapeDtypeStruct((nb * s, di), BF16),
        grid=(nb, nh),
        in_specs=common_in + [
            pl.BlockSpec((None, None, 2, None, dh, dh), lambda b, h: (b, jl, 0, h, 0, 0)),
            pl.BlockSpec((None, 2 * nh, dh), lambda b, h: (b, 0, 0)),
            pl.BlockSpec((None, 2 * nh, LANES), lambda b, h: (b, 0, 0)),
        ],
        out_specs=pl.BlockSpec((s, dh), lambda b, h: (b, h)),
        scratch_shapes=[pltpu.VMEM((dh, dh), F32), pltpu.VMEM((2, nc, dh, dh), BF16)],
        compiler_params=_cparams(("parallel", "parallel")),
        name="mlstm_latent",
    )(qvo, kt, qvo, qvo, gp, gpt, hw, state_c, n0, m0)


def _state_kernel(*refs, nl, nh, dh):
    ins, (c_ref, n_ref, m_ref) = refs[:3 * nl], refs[3 * nl:]
    lyr = pl.program_id(0)
    for jl in range(nl):
        kt_ref, v_ref, gpt_ref = ins[3 * jl:3 * jl + 3]

        @pl.when(lyr == jl)
        def _(kt_ref=kt_ref, v_ref=v_ref, gpt_ref=gpt_ref):
            l = v_ref.shape[0]
            sub = lax.broadcasted_iota(jnp.int32, m_ref.shape, 0)
            lane = lax.broadcasted_iota(jnp.int32, m_ref.shape, 1)
            m_all = jnp.zeros(m_ref.shape, F32)
            for h in range(nh):
                kt = kt_ref[h * dh:(h + 1) * dh, :]
                ktf = kt.astype(F32)
                v = v_ref[:, h * dh:(h + 1) * dh]
                for d in range(2):
                    a_r = gpt_ref[2 * nh * d + h:2 * nh * d + h + 1, :]
                    b_r = gpt_ref[2 * nh * d + nh + h:2 * nh * d + nh + h + 1, :]
                    m_last = jnp.maximum(jnp.max(a_r, axis=-1, keepdims=True), 0.0)
                    b_end = b_r[:, l - 1:l] if d == 0 else b_r[:, 0:1]
                    w_end = jnp.exp(a_r - m_last)
                    c_ref[d, h] = _dot((ktf * w_end).astype(BF16), v)
                    n_ref[d, h:h + 1, :] = _row_times_kt(w_end, kt)
                    m_all = jnp.where((sub == d) & (lane == h), b_end + m_last, m_all)
            m_ref[...] = m_all


def _state_call(lay, qvos, kts, gpts, nh, dh):
    nl = len(qvos)
    nbp, s = lay.nbp, lay.sp
    di = nh * dh
    assert s == MLSTM_L

    def pick(jl):
        return lambda lyr, b: jnp.where(lyr == jl, b, jnp.where(lyr < jl, 0, nbp - 1))

    in_specs, args = [], []
    for jl in range(nl):
        pb = pick(jl)
        in_specs.append(pl.BlockSpec((di, s), lambda lyr, b, pb=pb: (0, pb(lyr, b))))
        in_specs.append(pl.BlockSpec((s, di), lambda lyr, b, pb=pb: (pb(lyr, b), 1)))
        in_specs.append(pl.BlockSpec((4 * nh, s), lambda lyr, b, pb=pb: (0, pb(lyr, b))))
        args += [kts[jl], qvos[jl], gpts[jl]]
    return pl.pallas_call(
        functools.partial(_state_kernel, nl=nl, nh=nh, dh=dh),
        out_shape=(
            jax.ShapeDtypeStruct((nbp, nl, 2, nh, dh, dh), F32),
            jax.ShapeDtypeStruct((nbp, nl, 2, nh, dh), F32),
            jax.ShapeDtypeStruct((nbp, nl, 2, nh), F32),
        ),
        grid=(nl, nbp),
        in_specs=in_specs,
        out_specs=(
            pl.BlockSpec((None, None, 2, nh, dh, dh), lambda lyr, b: (b, lyr, 0, 0, 0, 0)),
            pl.BlockSpec((None, None, 2, nh, dh), lambda lyr, b: (b, lyr, 0, 0, 0)),
            pl.BlockSpec((None, None, 2, nh), lambda lyr, b: (b, lyr, 0, 0)),
        ),
        compiler_params=_cparams(("arbitrary", "arbitrary")),
        name="mlstm_prompt_state",
    )(*args)


def _mm_res_kernel(ap_ref, as_ref, w_ref, x_ref, g_ref, nw_ref, sh_ref, sc_ref, o_ref, u_ref, *, lay, tm):
    i = pl.program_id(0)
    grp = lay.group(i * tm)
    gate = g_ref[pl.ds(grp, 1), :]
    w = w_ref[...]
    rc = min(tm, 256)
    for a_ref, cond in ((ap_ref, i * tm < lay.tp), (as_ref, i * tm >= lay.tp)):
        @pl.when(cond)
        def _(a_ref=a_ref):
            for r in range(0, tm, rc):
                xn = x_ref[r:r + rc, :] + gate * _dot(a_ref[r:r + rc, :], w)
                o_ref[r:r + rc, :] = xn
                u_ref[r:r + rc, :] = _mod_value(xn, nw_ref, sh_ref, sc_ref, grp).astype(BF16)


def _mm_res_call(lay, a_p, a_s, w_bf, x, mods, l, mod2):
    kdim = a_p.shape[1]
    t, d = x.shape
    tm = lay.row_tile(1024)
    ntp = lay.tp // tm
    return pl.pallas_call(
        functools.partial(_mm_res_kernel, lay=lay, tm=tm),
        out_shape=(jax.ShapeDtypeStruct(x.shape, F32), jax.ShapeDtypeStruct(x.shape, BF16)),
        grid=(t // tm,),
        in_specs=[
            pl.BlockSpec((tm, kdim), lambda i: (jnp.minimum(i, ntp - 1), 0)),
            pl.BlockSpec((tm, kdim), lambda i: (jnp.maximum(i - ntp, 0), 0)),
            pl.BlockSpec((kdim, d), lambda i: (0, 0)),
            pl.BlockSpec((tm, d), lambda i: (i, 0)),
            lay.mod_spec(l, 2),
        ] + mod2.specs,
        out_specs=(pl.BlockSpec((tm, d), lambda i: (i, 0)), pl.BlockSpec((tm, d), lambda i: (i, 0))),
        compiler_params=_cparams(("arbitrary",)),
        name="mm_residual",
    )(a_p, a_s, w_bf, x, mods, *mod2.args)


def _fnet_kernel(x_ref, u_ref, u2_any, gt_ref, wc_ref, ds_ref, wo_ref, nw_ref, sh_ref, sc_ref, o_ref, u2_ref,
                 ab_scr, *, lay, row_base, groups, norm):
    s, d = x_ref.shape
    cg = d // groups
    grp = lay.group(row_base + pl.program_id(0) * s)
    wc = wc_ref[...]
    rc = min(s, 256)
    for g in range(groups):
        for r in range(0, s, rc):
            ab = _dot(u_ref[r:r + rc, g * cg:(g + 1) * cg], wc)
            ab_scr[r:r + rc, g * cg:(g + 1) * cg] = ab[:, :cg].astype(BF16)
            ab_scr[s + r:s + r + rc, g * cg:(g + 1) * cg] = ab[:, cg:].astype(BF16)
    gate = gt_ref[pl.ds(grp, 1), :]
    wo = wo_ref[...]
    for r in range(0, s, rc):
        y = _dot(ds_ref[r:r + rc, :], ab_scr[...]) * norm
        xn = x_ref[r:r + rc, :] + gate * _dot(y.astype(BF16), wo)
        o_ref[r:r + rc, :] = xn
        u2_ref[r:r + rc, :] = _mod_value(xn, nw_ref, sh_ref, sc_ref, grp).astype(BF16)


def _dft_mats(s, cg):
    kc = np.arange(cg)
    ang_c = 2.0 * np.pi * np.outer(kc, kc) / cg
    wc = np.concatenate([np.cos(ang_c), np.sin(ang_c)], axis=1)
    ks = np.arange(s)
    ang_s = 2.0 * np.pi * np.outer(ks, ks) / s
    ds = np.concatenate([np.cos(ang_s), -np.sin(ang_s)], axis=1)
    return jnp.asarray(wc, dtype=BF16), jnp.asarray(ds, dtype=BF16)


def _fnet_call(lay, x, u1, u2, mods, l, wo_bf, mod2, prompt):
    nb, s = (lay.nbp, lay.sp) if prompt else (lay.nbs, lay.ss)
    rb0 = 0 if prompt else lay.tp // s
    d = lay.d
    cg = d // FNET_GROUPS
    wc, ds = _dft_mats(s, cg)
    kern = functools.partial(_fnet_kernel, lay=lay, row_base=rb0 * s, groups=FNET_GROUPS,
                             norm=1.0 / math.sqrt(s * cg))
    blk = pl.BlockSpec((s, d), lambda b: (rb0 + b, 0))
    return pl.pallas_call(
        kern,
        out_shape=(jax.ShapeDtypeStruct(x.shape, F32), jax.ShapeDtypeStruct(u2.shape, BF16)),
        grid=(nb,),
        in_specs=[
            blk,
            blk,
            pl.BlockSpec(memory_space=pl.ANY),
            lay.mod_spec(l, 2),
            pl.BlockSpec((cg, 2 * cg), lambda b: (0, 0)),
            pl.BlockSpec((s, 2 * s), lambda b: (0, 0)),
            pl.BlockSpec((d, d), lambda b: (0, 0)),
        ] + mod2.specs,
        out_specs=(blk, blk),
        scratch_shapes=[pltpu.VMEM((2 * s, d), BF16)],
        input_output_aliases={0: 0, 2: 1},
        compiler_params=_cparams(("parallel",)),
        name="fnet_prompt" if prompt else "fnet_latent",
    )(x, u1, u2, mods, wc, ds, wo_bf, *mod2.args)


def _glu_kernel(u_ref, wa_ref, wg_ref, ba_ref, bg_ref, o_ref):
    u = u_ref[...]
    a = _dot(u, wa_ref[...]) + ba_ref[...]
    g = _dot(u, wg_ref[...]) + bg_ref[...]
    o_ref[...] = a * _sigmoid(g)


def _glu_call(lay, u, w_bf, bias):
    t, d = u.shape
    cd = w_bf.shape[1] // 2
    tm = lay.row_tile(1024)
    tn = min(cd, 1024)
    nj = cd // tn
    return pl.pallas_call(
        _glu_kernel,
        out_shape=jax.ShapeDtypeStruct((t, cd), F32),
        grid=(t // tm, nj),
        in_specs=[
            pl.BlockSpec((tm, d), lambda i, j: (i, 0)),
            pl.BlockSpec((d, tn), lambda i, j: (0, j)),
            pl.BlockSpec((d, tn), lambda i, j: (0, nj + j)),
            pl.BlockSpec((1, tn), lambda i, j: (0, j)),
            pl.BlockSpec((1, tn), lambda i, j: (0, nj + j)),
        ],
        out_specs=pl.BlockSpec((tm, tn), lambda i, j: (i, j)),
        compiler_params=_cparams(("parallel", "parallel")),
        name="conv_glu",
    )(u, w_bf, w_bf, bias, bias)


def _conv_kernel(c_ref, p_ref, n_ref, wd_ref, bd_ref, lw_ref, lb_ref, w2_ref, b2_ref, x_ref, gt_ref,
                 nw_ref, sh_ref, sc_ref, o_ref, u2_ref, pad, conv, act, *, lay, rb, width):
    i = pl.program_id(0)
    row0 = i * rb
    grp = lay.group(row0)
    seq = jnp.where(row0 < lay.tp, lay.sp, lay.ss)
    pos = jnp.where(row0 < lay.tp, row0 % lay.sp, (row0 - lay.tp) % lay.ss)
    has_prev = (pos != 0).astype(F32)
    has_next = (pos + rb != seq).astype(F32)
    hl = CONV_HALO
    half = width // 2
    cd = c_ref.shape[1]
    span = pad.shape[1]
    pad[0, 0:hl, :] = p_ref[...] * has_prev
    pad[0, hl:hl + rb, :] = c_ref[...]
    pad[0, hl + rb:hl + rb + hl, :] = n_ref[...] * has_next
    for s in range(1, SUBLANES):
        pad[s, 0:span - SUBLANES, :] = pad[0, s:s + span - SUBLANES, :]
    ngrp = 8
    sub = ngrp * SUBLANES
    lanes = 2 * LANES
    assert rb % sub == 0 and cd % lanes == 0

    def conv_block(blk, carry):
        r0 = pl.multiple_of(blk * sub, sub)
        for c0 in range(0, cd, lanes):
            bias = bd_ref[:, c0:c0 + lanes]
            accs = [jnp.zeros((SUBLANES, lanes), F32) + bias for _ in range(ngrp)]
            for k in sorted(range(width), key=lambda k: ((hl - half + k) % SUBLANES, k)):
                q, s = divmod(hl - half + k, SUBLANES)
                wk = wd_ref[k, :, c0:c0 + lanes]
                for gi in range(ngrp):
                    win = pad[s, pl.ds(r0 + (q + gi) * SUBLANES, SUBLANES), c0:c0 + lanes]
                    accs[gi] = accs[gi] + win * wk
            conv[pl.ds(r0, sub), c0:c0 + lanes] = jnp.concatenate(accs, axis=0)
        return carry

    lax.fori_loop(0, rb // sub, conv_block, 0)
    lw = lw_ref[...]
    lb = lb_ref[...]
    lsub = min(rb, 16 * SUBLANES)

    def ln_block(blk, carry):
        r0 = pl.multiple_of(blk * lsub, lsub)
        acc = conv[pl.ds(r0, lsub), :]
        mu = jnp.mean(acc, axis=-1, keepdims=True)
        cen = acc - mu
        var = jnp.mean(cen * cen, axis=-1, keepdims=True)
        y = cen * lax.rsqrt(var + EPS) * lw + lb
        act[pl.ds(r0, lsub), :] = (y * _sigmoid(y)).astype(BF16)
        return carry

    lax.fori_loop(0, rb // lsub, ln_block, 0)
    xn = x_ref[...] + gt_ref[pl.ds(grp, 1), :] * (_dot(act[...], w2_ref[...]) + b2_ref[...])
    o_ref[...] = xn
    u2_ref[...] = _mod_value(xn, nw_ref, sh_ref, sc_ref, grp).astype(BF16)


def _conv_call(lay, glu, wd, bd, lw, lb, w2_bf, b2, x, mods, l, mod2):
    t, cd = glu.shape
    d = x.shape[1]
    rb = lay.row_tile(256)
    hl = CONV_HALO
    assert CONV_WIDTH // 2 <= hl and rb % hl == 0
    nhb = t // hl
    per = rb // hl
    wd_p = jnp.broadcast_to(wd[:, None, :], (CONV_WIDTH, SUBLANES, cd))
    row = lambda a: a.reshape(1, -1)
    rows = pl.BlockSpec((rb, d), lambda i: (i, 0))
    return pl.pallas_call(
        functools.partial(_conv_kernel, lay=lay, rb=rb, width=CONV_WIDTH),
        out_shape=(jax.ShapeDtypeStruct(x.shape, F32), jax.ShapeDtypeStruct(x.shape, BF16)),
        grid=(t // rb,),
        in_specs=[
            pl.BlockSpec((rb, cd), lambda i: (i, 0)),
            pl.BlockSpec((hl, cd), lambda i: (jnp.maximum(i * per - 1, 0), 0)),
            pl.BlockSpec((hl, cd), lambda i: (jnp.minimum((i + 1) * per, nhb - 1), 0)),
            pl.BlockSpec(wd_p.shape, lambda i: (0, 0, 0)),
            pl.BlockSpec((1, cd), lambda i: (0, 0)),
            pl.BlockSpec((1, cd), lambda i: (0, 0)),
            pl.BlockSpec((1, cd), lambda i: (0, 0)),
            pl.BlockSpec((cd, d), lambda i: (0, 0)),
            pl.BlockSpec((1, d), lambda i: (0, 0)),
            rows,
            lay.mod_spec(l, 2),
        ] + mod2.specs,
        out_specs=(rows, rows),
        scratch_shapes=[pltpu.VMEM((SUBLANES, rb + 2 * hl, cd), F32), pltpu.VMEM((rb, cd), F32),
                        pltpu.VMEM((rb, cd), BF16)],
        compiler_params=_cparams(("parallel",)),
        name="conv_dw_ln_pw2",
    )(glu, glu, glu, wd_p, row(bd), row(lw), row(lb), w2_bf, row(b2), x, mods, *mod2.args)


def _router_kernel(u_ref, wr_ref, upper_ref, ltri_ref, o_ref, cnt_ref, *, ng, ne):
    for j in range(o_ref.shape[0]):
        _route_tile(u_ref, wr_ref, upper_ref, ltri_ref, o_ref, cnt_ref, j, ng, ne)


def _route_tile(u_ref, wr_ref, upper_ref, ltri_ref, o_ref, cnt_ref, j, ng, ne):
    tm = upper_ref.shape[0]
    logits = _dot_nt(wr_ref[...], u_ref[j * tm:(j + 1) * tm, :])
    neg = -jnp.inf
    row = lax.broadcasted_iota(jnp.int32, (SUBLANES, tm), 0)

    gl = jnp.where(row < ng, logits[0:SUBLANES, :], neg)
    gmax = jnp.max(gl, axis=0, keepdims=True)
    gidx = jnp.min(jnp.where(gl == gmax, row, SUBLANES), axis=0, keepdims=True)
    g_p = 1.0 / jnp.sum(jnp.where(row < ng, jnp.exp(gl - gmax), 0.0), axis=0, keepdims=True)

    sel = logits[SUBLANES:2 * SUBLANES, :]
    for g in range(1, ng):
        sel = jnp.where(gidx == g, logits[(1 + g) * SUBLANES:(2 + g) * SUBLANES, :], sel)
    v1 = jnp.max(sel, axis=0, keepdims=True)
    i1 = jnp.min(jnp.where(sel == v1, row, SUBLANES), axis=0, keepdims=True)
    sel2 = jnp.where(row == i1, neg, sel)
    v2 = jnp.max(sel2, axis=0, keepdims=True)
    i2 = jnp.min(jnp.where(sel2 == v2, row, SUBLANES), axis=0, keepdims=True)
    e1 = gidx * SUBLANES + i1
    e2 = gidx * SUBLANES + i2
    tt = jnp.exp(v2 - v1)
    p1 = 1.0 / (1.0 + tt)
    gate1 = p1 * g_p
    gate2 = (tt * p1) * g_p

    rowe = lax.broadcasted_iota(jnp.int32, (ne, tm), 0)
    oh1 = rowe == e1
    oh2 = rowe == e2
    oh = jnp.where(oh1 | oh2, 1.0, 0.0)
    groups = jnp.floor((jnp.sum(oh, axis=1, keepdims=True) + (SUBLANES - 1)) * (1.0 / SUBLANES))
    groups_b = jnp.broadcast_to(groups, (ne, LANES))
    start = SUBLANES * _dot(ltri_ref[...], groups_b.astype(BF16))[:, 0:1]
    prefix = _dot(oh.astype(BF16), upper_ref[...]) + start
    pos1 = jnp.sum(jnp.where(oh1, prefix, 0.0), axis=0, keepdims=True)
    pos2 = jnp.sum(jnp.where(oh2, prefix, 0.0), axis=0, keepdims=True)
    cnt_ref[j] = groups_b

    out = jnp.where(row == 0, e1.astype(F32), 0.0)
    out = jnp.where(row == 1, e2.astype(F32), out)
    out = jnp.where(row == 2, gate1, out)
    out = jnp.where(row == 3, gate2, out)
    out = jnp.where(row == 4, pos1, out)
    out = jnp.where(row == 5, pos2, out)
    o_ref[j] = out


def _router_call(u, wr_t, upper, ltri, ng, ne):
    t, d = u.shape
    tm = upper.shape[0]
    nt = t // tm
    per = 2 if nt % 2 == 0 else 1
    assert ne // ng == SUBLANES and ng <= SUBLANES and SUBLANES + ne <= LANES
    return pl.pallas_call(
        functools.partial(_router_kernel, ng=ng, ne=ne),
        out_shape=(jax.ShapeDtypeStruct((nt, SUBLANES, tm), F32), jax.ShapeDtypeStruct((nt, ne, LANES), F32)),
        grid=(nt // per,),
        in_specs=[
            pl.BlockSpec((per * tm, d), lambda i: (i, 0)),
            pl.BlockSpec((LANES, d), lambda i: (0, 0)),
            pl.BlockSpec((tm, tm), lambda i: (0, 0)),
            pl.BlockSpec((ne, ne), lambda i: (0, 0)),
        ],
        out_specs=(pl.BlockSpec((per, SUBLANES, tm), lambda i: (i, 0, 0)),
                   pl.BlockSpec((per, ne, LANES), lambda i: (i, 0, 0))),
        compiler_params=_cparams(("parallel",)),
        name="moe_router",
    )(u, wr_t, upper, ltri)


def _pack_halves(lo, hi):
    lo_bits = lax.shift_right_logical(pltpu.bitcast(lo, U32), jnp.uint32(16))
    hi_bits = pltpu.bitcast(hi, U32) & jnp.uint32(0xFFFF0000)
    return hi_bits | lo_bits


def _unpack_halves(w):
    lo = pltpu.bitcast(lax.shift_left(w, jnp.uint32(16)), F32)
    hi = pltpu.bitcast(w & jnp.uint32(0xFFFF0000), F32)
    return lo.astype(BF16), hi.astype(BF16)


def _round_bf16(x):
    return x.astype(BF16).astype(F32)


def _group_copy(src, src_g, dst, dst_g, sem):
    g8 = lambda g: pl.ds(pl.multiple_of(g * SUBLANES, SUBLANES), SUBLANES)
    return pltpu.make_async_copy(src.at[g8(src_g), :], dst.at[g8(dst_g), :], sem)


def _for_groups(n, fn, unroll=4):
    def body_many(i, c):
        for j in range(unroll):
            fn(i * unroll + j)
        return c

    def body_one(g, c):
        fn(g)
        return c

    full = lax.div(n, jnp.int32(unroll))
    lax.fori_loop(0, full, body_many, 0)
    lax.fori_loop(full * unroll, n, body_one, 0)


def _dispatch_kernel(gdst_ref, ngt_ref, pad0_ref, npad_ref, tail_ref, u_ref, pos_ref, xs_out, loc, zeros, sem,
                     zsem, *, tm, nl, ne):
    i = pl.program_id(0)
    nt = pl.num_programs(0)
    slot = i % 2
    nlg = nl // SUBLANES

    def copy(step, s, g):
        return _group_copy(loc.at[s], g, xs_out, gdst_ref[step * nlg + g], sem.at[s])

    def group_wait(s):
        _group_copy(loc.at[s], 0, xs_out, 0, sem.at[s]).wait()

    def zero_pad(e, start):
        if start:
            _for_groups(npad_ref[e], lambda g: _group_copy(zeros, 0, xs_out, pad0_ref[e] + g, zsem).start())
        else:
            _for_groups(npad_ref[e], lambda g: _group_copy(zeros, 0, xs_out, 0, zsem).wait())

    def zero_block(t):
        first = pl.multiple_of(tail_ref[0] + t * MOE_BM, MOE_BM)
        return pltpu.make_async_copy(zeros, xs_out.at[pl.ds(first, MOE_BM), :], zsem)

    @pl.when(i == 0)
    def _():
        zeros[...] = jnp.zeros(zeros.shape, zeros.dtype)
        for e in range(ne):
            zero_pad(e, True)
        _for_groups(tail_ref[1], lambda t: zero_block(t).start())

    @pl.when(i >= 2)
    def _():
        _for_groups(ngt_ref[i - 2], lambda g: group_wait(slot))

    pos1 = pos_ref[4:5, :].astype(jnp.int32)
    pos2 = pos_ref[5:6, :].astype(jnp.int32)
    half = u_ref.shape[1] // 2
    u = u_ref[...]
    rc = 256
    for r in range(0, nl, rc):
        p = lax.broadcasted_iota(jnp.int32, (rc, tm), 0) + r
        onehot = jnp.where((p == pos1) | (p == pos2), 1.0, 0.0).astype(BF16)
        rows = _dot(onehot, u)
        loc[slot, r:r + rc, :] = _pack_halves(rows[:, :half], rows[:, half:])
    _for_groups(ngt_ref[i], lambda g: copy(i, slot, g).start())

    @pl.when(i == nt - 1)
    def _():
        @pl.when(i >= 1)
        def _():
            _for_groups(ngt_ref[i - 1], lambda g: group_wait(1 - slot))

        _for_groups(ngt_ref[i], lambda g: group_wait(slot))
        for e in range(ne):
            zero_pad(e, False)
        _for_groups(tail_ref[1], lambda t: pltpu.make_async_copy(zeros, xs_out.at[0:MOE_BM, :], zsem).wait())


def _dispatch_call(u, pos_rows, gdst, ngt, pad0, npad, tail, nrows, tm, nl):
    t, d = u.shape
    ne = npad.shape[0]
    grid_spec = pltpu.PrefetchScalarGridSpec(
        num_scalar_prefetch=5,
        grid=(t // tm,),
        in_specs=[
            pl.BlockSpec((tm, d), lambda i, *_: (i, 0)),
            pl.BlockSpec((None, SUBLANES, tm), lambda i, *_: (i, 0, 0)),
        ],
        out_specs=pl.BlockSpec(memory_space=pl.ANY),
        scratch_shapes=[pltpu.VMEM((2, nl, d // 2), U32), pltpu.VMEM((MOE_BM, d // 2), U32),
                        pltpu.SemaphoreType.DMA((2,)), pltpu.SemaphoreType.DMA(())],
    )
    return pl.pallas_call(
        functools.partial(_dispatch_kernel, tm=tm, nl=nl, ne=ne),
        out_shape=jax.ShapeDtypeStruct((nrows, d // 2), U32),
        grid_spec=grid_spec,
        compiler_params=_cparams(("arbitrary",)),
        name="moe_dispatch",
    )(gdst, ngt, pad0, npad, tail, u, pos_rows)


def _expert_kernel(row0_ref, nblk_ref, rows_ref, tail_ref, x_hbm, w13_ref, w2_ref, y_hbm, xbuf, ybuf, w13_bf, w2_bf,
                   xsem, ysem, *, hid):
    e = pl.program_id(0)
    n = nblk_ref[e]
    g0 = row0_ref[e] // MOE_BM
    total = tail_ref[0] // MOE_BM
    nx, ny = xbuf.shape[0], ybuf.shape[0]
    ahead = nx - 1
    half = xbuf.shape[2]

    def rows(g):
        return pl.ds(pl.multiple_of(g * MOE_BM, MOE_BM), MOE_BM)

    def x_copy(g):
        return pltpu.make_async_copy(x_hbm.at[rows(g), :], xbuf.at[g % nx], xsem.at[g % nx])

    def y_copy(g, s):
        return pltpu.make_async_copy(ybuf.at[s], y_hbm.at[rows(g), :], ysem.at[s])

    @pl.when(e == 0)
    def _():
        for g in range(ahead):
            @pl.when(g < total)
            def _(g=g):
                x_copy(g).start()

    @pl.when(n > 0)
    def _():
        w13_bf[...] = w13_ref[...].astype(BF16)
        w2_bf[...] = w2_ref[...].astype(BF16)

        def block(c, carry):
            g = g0 + c
            x_copy(g).wait()

            @pl.when(g + ahead < total)
            def _():
                x_copy(g + ahead).start()

            @pl.when(g >= ny)
            def _():
                y_copy(g - ny, g % ny).wait()

            def compute(nrows):
                x_lo, x_hi = _unpack_halves(xbuf[g % nx, 0:nrows, :])
                hb = _dot(x_lo, w13_bf[:half, :]) + _dot(x_hi, w13_bf[half:, :])
                a = hb[:, :hid]
                act = (a * _sigmoid(a)) * hb[:, hid:]
                y = _round_bf16(_dot(act.astype(BF16), w2_bf[...]))
                ybuf[g % ny, 0:nrows, :] = _pack_halves(y[:, :half], y[:, half:])
                if nrows < MOE_BM:
                    ybuf[g % ny, nrows:, :] = jnp.zeros((MOE_BM - nrows, half), ybuf.dtype)

            valid = rows_ref[e] - c * MOE_BM
            for nrows, cond in ((MOE_BM, valid > MOE_BM // 2), (MOE_BM // 2, valid <= MOE_BM // 2)):
                @pl.when(cond)
                def _(nrows=nrows):
                    compute(nrows)

            y_copy(g, g % ny).start()
            return carry

        lax.fori_loop(0, n, block, 0)

    @pl.when(e == pl.num_programs(0) - 1)
    def _():
        for j in range(ny):
            @pl.when(total - ny + j >= 0)
            def _(j=j):
                g = total - ny + j
                y_copy(g, g % ny).wait()

        ybuf[0] = jnp.zeros(ybuf.shape[1:], ybuf.dtype)
        _for_groups(tail_ref[1], lambda t: y_copy(total + t, 0).start())
        _for_groups(tail_ref[1], lambda t: y_copy(total + t, 0).wait())


def _expert_call(xs, row0, nblk, nrows, tail, w13, w2, l):
    r, half = xs.shape
    d = 2 * half
    ne = w13.shape[1]
    hid = w2.shape[2]
    nx, ny = 5, 3
    grid_spec = pltpu.PrefetchScalarGridSpec(
        num_scalar_prefetch=4,
        grid=(ne,),
        in_specs=[
            pl.BlockSpec(memory_space=pl.ANY),
            pl.BlockSpec((None, None, d, 2 * hid), lambda e, *_: (l, e, 0, 0)),
            pl.BlockSpec((None, None, hid, d), lambda e, *_: (l, e, 0, 0)),
        ],
        out_specs=pl.BlockSpec(memory_space=pl.ANY),
        scratch_shapes=[
            pltpu.VMEM((nx, MOE_BM, half), U32), pltpu.VMEM((ny, MOE_BM, half), U32),
            pltpu.VMEM((d, 2 * hid), BF16), pltpu.VMEM((hid, d), BF16),
            pltpu.SemaphoreType.DMA((nx,)), pltpu.SemaphoreType.DMA((ny,)),
        ],
    )
    return pl.pallas_call(
        functools.partial(_expert_kernel, hid=hid),
        out_shape=jax.ShapeDtypeStruct((r, half), U32),
        grid_spec=grid_spec,
        compiler_params=_cparams(("arbitrary",)),
        name="moe_experts",
    )(row0, nblk, nrows, tail, xs, w13, w2)


def _combine_kernel(gdst_ref, ngt_ref, x_ref, rt_ref, gt_ref, nw_ref, sh_ref, sc_ref, ys_ref, *rest,
                    lay, tm, nl, final, gated):
    if gated:
        wh_ref, wl_ref, out_a, out_b, out_g, loc, sem = rest
    else:
        out_a, out_b, loc, sem = rest
    i = pl.program_id(0)
    nt = pl.num_programs(0)
    slot = i % 2
    nlg = nl // SUBLANES

    def copy(step, s, g):
        return _group_copy(ys_ref, gdst_ref[step * nlg + g], loc.at[s], g, sem.at[s])

    @pl.when(i == 0)
    def _():
        loc[...] = jnp.zeros(loc.shape, loc.dtype)
        _for_groups(ngt_ref[0], lambda g: copy(0, 0, g).start())

    _for_groups(ngt_ref[i], lambda g: _group_copy(ys_ref, 0, loc.at[slot], 0, sem.at[slot]).wait())

    @pl.when(i + 1 < nt)
    def _():
        _for_groups(ngt_ref[i + 1], lambda g: copy(i + 1, 1 - slot, g).start())

    rt = rt_ref[...]
    gate1, gate2 = rt[:, 2:3], rt[:, 3:4]
    pos1, pos2 = rt[:, 4:5].astype(jnp.int32), rt[:, 5:6].astype(jnp.int32)
    half = loc.shape[2]
    rc = 256
    mix_lo = jnp.zeros((tm, half), F32)
    mix_hi = jnp.zeros((tm, half), F32)
    for r in range(0, nl, rc):
        p = lax.broadcasted_iota(jnp.int32, (tm, rc), 1) + r
        wgt = (jnp.where(p == pos1, gate1, 0.0) + jnp.where(p == pos2, gate2, 0.0)).astype(BF16)
        y_lo, y_hi = _unpack_halves(loc[slot, r:r + rc, :])
        mix_lo = mix_lo + _dot(wgt, y_lo)
        mix_hi = mix_hi + _dot(wgt, y_hi)
    grp = lay.group(i * tm)
    gate = gt_ref[pl.ds(grp, 1), :]
    nw = nw_ref[...]
    sh = sh_ref[pl.ds(grp, 1), :]
    sc = sc_ref[pl.ds(grp, 1), :]
    er = 128 if gated else tm
    for r in range(0, tm, er):
        rs = slice(r, r + er)
        x_lo = x_ref[rs, :half] + gate[:, :half] * mix_lo[rs]
        x_hi = x_ref[rs, half:] + gate[:, half:] * mix_hi[rs]
        ms = (jnp.sum(x_lo * x_lo, axis=-1, keepdims=True)
              + jnp.sum(x_hi * x_hi, axis=-1, keepdims=True)) / (2 * half)
        inv = lax.rsqrt(ms + EPS)
        if final:
            y_lo = x_lo * inv * nw[:, :half]
            y_hi = x_hi * inv * nw[:, half:]
            for ref, cond in ((out_a, i * tm < lay.tp), (out_b, i * tm >= lay.tp)):
                @pl.when(cond)
                def _(ref=ref, y_lo=y_lo, y_hi=y_hi):
                    ref[rs, :half] = y_lo
                    ref[rs, half:] = y_hi
            continue
        out_a[rs, :half] = x_lo
        out_a[rs, half:] = x_hi
        u_lo = x_lo * inv * nw[:, :half] * (1.0 + sc[:, :half]) + sh[:, :half]
        u_hi = x_hi * inv * nw[:, half:] * (1.0 + sc[:, half:]) + sh[:, half:]
        if gated:
            (a_hi, a_lo), (b_hi, b_lo) = _split2(u_lo), _split2(u_hi)
            wh, wl = wh_ref[...], wl_ref[...]
            out_g[rs, :] = (_dot_nt(a_hi, wh[:, :half]) + _dot_nt(b_hi, wh[:, half:])
                            + _dot_nt(a_lo, wh[:, :half]) + _dot_nt(b_lo, wh[:, half:])
                            + _dot_nt(a_hi, wl[:, :half]) + _dot_nt(b_hi, wl[:, half:]))
            out_b[rs, :half] = a_hi
            out_b[rs, half:] = b_hi
        else:
            out_b[rs, :half] = u_lo.astype(BF16)
            out_b[rs, half:] = u_hi.astype(BF16)


def _combine_call(lay, x, route, mods, l, mod_next, gdst, ngt, ys, tm, nl, final, gate_w=None):
    t, d = x.shape
    rows = pl.BlockSpec((tm, d), lambda i, *_: (i, 0))
    gated = gate_w is not None
    extra_in, extra_args = [], ()
    if final:
        ntp = lay.tp // tm
        out_specs = (pl.BlockSpec((tm, d), lambda i, *_: (jnp.minimum(i, ntp - 1), 0)),
                     pl.BlockSpec((tm, d), lambda i, *_: (jnp.maximum(i - ntp, 0), 0)))
        out_shape = (jax.ShapeDtypeStruct((lay.tp, d), F32), jax.ShapeDtypeStruct((lay.ts, d), F32))
    else:
        out_specs = (rows, rows)
        out_shape = (jax.ShapeDtypeStruct(x.shape, F32), jax.ShapeDtypeStruct(x.shape, BF16))
        if gated:
            wspec = pl.BlockSpec((LANES, d), lambda i, *_: (0, 0))
            extra_in, extra_args = [wspec, wspec], tuple(gate_w)
            out_specs += (pl.BlockSpec((tm, LANES), lambda i, *_: (i, 0)),)
            out_shape += (jax.ShapeDtypeStruct((t, LANES), F32),)
    grid_spec = pltpu.PrefetchScalarGridSpec(
        num_scalar_prefetch=2,
        grid=(t // tm,),
        in_specs=[
            rows,
            pl.BlockSpec((tm, SUBLANES), lambda i, *_: (i, 0)),
            lay.mod_spec(l, 5),
        ] + mod_next.specs + [pl.BlockSpec(memory_space=pl.ANY)] + extra_in,
        out_specs=out_specs,
        scratch_shapes=[pltpu.VMEM((2, nl, d // 2), U32), pltpu.SemaphoreType.DMA((2,))],
    )
    return pl.pallas_call(
        functools.partial(_combine_kernel, lay=lay, tm=tm, nl=nl, final=final, gated=gated),
        out_shape=out_shape,
        grid_spec=grid_spec,
        compiler_params=_cparams(("arbitrary",)),
        name="moe_combine_final" if final else "moe_combine",
    )(gdst, ngt, x, route, mods, *mod_next.args, ys, *extra_args)


def _moe_layer(lay, x, u2, mods, l, mod_next, final, wr_t, upper, ltri, w13, w2, ng, ne, gate_w=None):
    t, d = x.shape
    tm = upper.shape[0]
    nt = t // tm
    bmg = MOE_BM // SUBLANES
    nl = -(-(MOE_TOP_K * tm + ne * (SUBLANES - 1)) // 256) * 256
    nlg = nl // SUBLANES
    route_t, counts = _router_call(u2, wr_t, upper, ltri, ng, ne)
    route = jnp.swapaxes(route_t, 1, 2).reshape(t, SUBLANES)

    c8 = counts[:, :, 0].astype(jnp.int32)
    lend = jnp.cumsum(c8, axis=1)
    lstart = lend - c8
    ngt = lend[:, -1].astype(jnp.int32)
    tot = jnp.sum(c8, axis=0)
    padded = (tot + bmg - 1) // bmg * bmg
    gend = jnp.cumsum(padded)
    gbase = (gend - padded)[None, :] + jnp.cumsum(c8, axis=0) - c8
    nb = -(-(MOE_TOP_K * t + nt * ne * (SUBLANES - 1)) // MOE_BM) + ne
    g = jnp.arange(nlg, dtype=jnp.int32)[None, :, None]
    owner = (g >= lstart[:, None, :]) & (g < lend[:, None, :])
    gdst = g[:, :, 0] + jnp.sum(jnp.where(owner, (gbase - lstart)[:, None, :], 0), axis=-1)
    gdst = gdst.reshape(nt * nlg).astype(jnp.int32)
    row0 = ((gend - padded) * SUBLANES).astype(jnp.int32)
    nblk = (padded // bmg).astype(jnp.int32)
    tail = jnp.stack([gend[-1] * SUBLANES, nb - gend[-1] // bmg]).astype(jnp.int32)

    pad0 = (gend - padded + tot).astype(jnp.int32)
    npad = (padded - tot).astype(jnp.int32)
    xs = _dispatch_call(u2, route_t, gdst, ngt, pad0, npad, tail, nb * MOE_BM, tm, nl)
    ys = _expert_call(xs, row0, nblk, (tot * SUBLANES).astype(jnp.int32), tail, w13, w2, l)
    return _combine_call(lay, x, route, mods, l, mod_next, gdst, ngt, ys, tm, nl, final, gate_w)


def _lower_tri(n, strict):
    r = np.arange(n)
    m = (r[None, :] < r[:, None]) if strict else (r[None, :] <= r[:, None])
    return jnp.asarray(m.astype(np.float32), dtype=BF16)


def kernel(x_prompt, x_sample, state_C, state_n, state_m, c, c_ctx, ada_w, ada_b, norm1_w, norm2_w, m_w_in, m_b_gate, m_head_norm_w, m_w_out, f_w_out, cv_w_pw1, cv_b_pw1, cv_w_dw, cv_b_dw, cv_ln_w, cv_ln_b, cv_w_pw2, cv_b_pw2, r_w_group, r_w_expert, e_w13, e_w2, final_norm_w):
    nbp, sp, d = x_prompt.shape
    nbs, ss, _ = x_sample.shape
    assert ss % GRID_W == 0
    lay = _Layout(nbp, sp, nbs, ss, d)
    depth = ada_w.shape[0]
    nh, dh = state_C.shape[3], state_C.shape[4]
    di = nh * dh
    ng = r_w_group.shape[2]
    ne = r_w_expert.shape[2]
    assert ng == MOE_GROUPS and ng + ne <= LANES and 4 * nh <= LANES and MOE_TOP_K == 2

    cv = jnp.zeros((lay.ngp, d), F32).at[0].set(c_ctx).at[1:1 + nbs].set(c)
    mods = _ada_call(cv, ada_w, ada_b)

    tri_l = _lower_tri(MLSTM_L, strict=False)
    upper = _lower_tri(lay.row_tile(512), strict=True).T
    ltri = _lower_tri(ne, strict=True)
    row = lambda a: a.reshape(1, -1)
    w_in_t = jnp.swapaxes(m_w_in, 1, 2)
    tn = 1024
    assert di % tn == 0
    nkb = di // tn

    def gate_weights(j):
        wg = jnp.zeros((LANES, d), F32).at[:4 * nh].set(w_in_t[j, 4 * di:])
        return _split2(wg)

    x, u1, graw = _prep_call(lay, x_prompt.reshape(lay.tp, d), x_sample.reshape(lay.ts, d),
                             _Mod(lay, mods, norm1_w[0], 0, 0), gate_weights(0))
    y = None
    qvos, kts, gpts = [], [], []
    for l in range(depth):
        j, kind = l // N_MIXERS, l % N_MIXERS
        mod2 = _Mod(lay, mods, norm2_w[l], l, 3)
        if kind == 0:
            bg = jnp.zeros((1, LANES), F32).at[0, :4 * nh].set(m_b_gate[j])
            gp = _gates_call(lay, graw, bg, tri_l, nh)
            gpt = gp[:, :4 * nh].T
            qvo_blocks = list(range(nkb)) + list(range(2 * nkb, 4 * nkb))
            qvo = _proj_call(lay, u1, w_in_t, j, qvo_blocks, tn, transposed=False)
            kt = _proj_call(lay, u1, w_in_t, j, list(range(nkb, 2 * nkb)), tn, transposed=True)
            hw = row(m_head_norm_w[j])
            hg_p = _mlstm_call(lay, qvo, kt, gp, gpt, hw, nh, dh, prompt=True)
            n0 = state_n[:, j].reshape(nbs, 2 * nh, dh)
            m0 = jnp.broadcast_to(state_m[:, j].reshape(nbs, 2 * nh, 1), (nbs, 2 * nh, LANES))
            hg_s = _mlstm_call(lay, qvo, kt, gp, gpt, hw, nh, dh, prompt=False, state=(state_C, j, n0, m0))
            x, u2 = _mm_res_call(lay, hg_p, hg_s, m_w_out[j].astype(BF16), x, mods, l, mod2)
            qvos.append(qvo)
            kts.append(kt)
            gpts.append(gpt)
        elif kind == 1:
            wo = f_w_out[j].astype(BF16)
            u2 = jnp.zeros((lay.t, d), BF16)
            x, u2 = _fnet_call(lay, x, u1, u2, mods, l, wo, mod2, prompt=True)
            x, u2 = _fnet_call(lay, x, u1, u2, mods, l, wo, mod2, prompt=False)
        else:
            glu = _glu_call(lay, u1, cv_w_pw1[j].astype(BF16), row(cv_b_pw1[j]))
            x, u2 = _conv_call(lay, glu, cv_w_dw[j], cv_b_dw[j], cv_ln_w[j], cv_ln_b[j], cv_w_pw2[j].astype(BF16),
                               cv_b_pw2[j], x, mods, l, mod2)
        wr = jnp.zeros((LANES, d), F32).at[:ng].set(r_w_group[l].T).at[SUBLANES:SUBLANES + ne].set(r_w_expert[l].T)
        final = l + 1 == depth
        mod_next = _Mod(lay, mods, final_norm_w, l, 0) if final else _Mod(lay, mods, norm1_w[l + 1], l + 1, 0)
        next_mlstm = not final and (l + 1) % N_MIXERS == 0
        outs = _moe_layer(lay, x, u2, mods, l, mod_next, final, wr.astype(BF16), upper, ltri, e_w13, e_w2, ng, ne,
                          gate_weights((l + 1) // N_MIXERS) if next_mlstm else None)
        if final:
            y = outs
        elif next_mlstm:
            x, u1, graw = outs
        else:
            x, u1 = outs

    y_prompt = y[0].reshape(nbp, sp, d)
    y_sample = y[1].reshape(nbs, ss, d)
    new_c, new_n, new_m = _state_call(lay, qvos, kts, gpts, nh, dh)
    return (y_prompt, y_sample, new_c, new_n, new_m)
```

```python
import functools
import math

import numpy as np
import jax
import jax.numpy as jnp
from jax import lax
from jax.experimental import pallas as pl
from jax.experimental.pallas import tpu as pltpu

F32 = jnp.float32
BF16 = jnp.bfloat16
U32 = jnp.uint32
EPS = 1e-6
GRID_W = 64
N_MIXERS = 3
FNET_GROUPS = 4
CONV_WIDTH = 31
MOE_GROUPS = 4
MOE_TOP_K = 2

LANES = 128
SUBLANES = 8
MLSTM_L = 256
MOE_BM = 256
CONV_HALO = 16
VMEM_LIMIT = 56 * 1024 * 1024


def _cparams(sem, vmem=VMEM_LIMIT):
    return pltpu.CompilerParams(dimension_semantics=sem, vmem_limit_bytes=vmem)


def _dot(a, b):
    return jnp.dot(a, b, preferred_element_type=F32)


def _dot_nt(a, b):
    return lax.dot_general(a, b, (((1,), (1,)), ((), ())), preferred_element_type=F32)


def _rms(x, w):
    return x * lax.rsqrt(jnp.mean(x * x, axis=-1, keepdims=True) + EPS) * w


def _modulate(x, w, shift, scale):
    return _rms(x, w) * (1.0 + scale) + shift


def _sigmoid(x):
    return 1.0 / (1.0 + jnp.exp(-x))


def _log_sigmoid(x):
    return jnp.minimum(x, 0.0) - jnp.log(1.0 + jnp.exp(-jnp.abs(x)))


def _split2(x):
    hi = x.astype(BF16)
    return hi, (x - hi.astype(F32)).astype(BF16)


def _split3(x):
    hi = x.astype(BF16)
    r1 = x - hi.astype(F32)
    mid = r1.astype(BF16)
    return hi, mid, (r1 - mid.astype(F32)).astype(BF16)


class _Layout:
    def __init__(self, nbp, sp, nbs, ss, d):
        self.nbp, self.sp, self.nbs, self.ss, self.d = nbp, sp, nbs, ss, d
        self.tp, self.ts = nbp * sp, nbs * ss
        self.t = self.tp + self.ts
        assert self.tp % ss == 0, "latent sequences must start on a block boundary of their own length"
        self.ngp = -(-(1 + nbs) // SUBLANES) * SUBLANES

    def group(self, row0):
        return jnp.where(row0 < self.tp, 0, 1 + (row0 - self.tp) // self.ss)

    def row_tile(self, want):
        tm = math.gcd(math.gcd(self.tp, self.ss), want)
        assert tm % SUBLANES == 0
        return tm

    def mod_spec(self, l, chunk):
        return pl.BlockSpec((None, self.ngp, self.d), lambda *_: (l, 0, chunk))

    def row_spec(self):
        return pl.BlockSpec((1, self.d), lambda *_: (0, 0))


class _Mod:
    def __init__(self, lay, mods, nw, l, c_shift):
        self.args = (nw.reshape(1, -1), mods, mods)
        self.specs = [lay.row_spec(), lay.mod_spec(l, c_shift), lay.mod_spec(l, c_shift + 1)]


def _mod_value(x, nw_ref, sh_ref, sc_ref, grp):
    return _modulate(x, nw_ref[...], sh_ref[pl.ds(grp, 1), :], sc_ref[pl.ds(grp, 1), :])


def _ada_kernel(cv_ref, w_ref, b_ref, o_ref):
    s = cv_ref[...]
    s = s * _sigmoid(s)
    o_ref[...] = _dot(s.astype(BF16), w_ref[...].astype(BF16)) + b_ref[...]


def _ada_call(cv, ada_w, ada_b):
    depth, d, n = ada_w.shape
    ngp = cv.shape[0]
    tn = min(n, 2048)
    return pl.pallas_call(
        _ada_kernel,
        out_shape=jax.ShapeDtypeStruct((depth, ngp, n), F32),
        grid=(depth, n // tn),
        in_specs=[
            pl.BlockSpec((ngp, d), lambda l, j: (0, 0)),
            pl.BlockSpec((None, d, tn), lambda l, j: (l, 0, j)),
            pl.BlockSpec((None, 1, tn), lambda l, j: (l, 0, j)),
        ],
        out_specs=pl.BlockSpec((None, ngp, tn), lambda l, j: (l, 0, j)),
        compiler_params=_cparams(("parallel", "parallel")),
        name="ada_mods",
    )(cv, ada_w, ada_b.reshape(depth, 1, n))


def _gate_raw(u, wh, wl):
    u_hi, u_lo = _split2(u)
    return _dot_nt(u_hi, wh) + _dot_nt(u_lo, wh) + _dot_nt(u_hi, wl), u_hi


def _prep_kernel(xp_ref, xs_ref, nw_ref, sh_ref, sc_ref, wh_ref, wl_ref, x_ref, u_ref, g_ref, *, lay, tm):
    i = pl.program_id(0)
    grp = lay.group(i * tm)
    for src, cond in ((xp_ref, i * tm < lay.tp), (xs_ref, i * tm >= lay.tp)):
        @pl.when(cond)
        def _(src=src):
            rc = min(tm, 256)
            for r in range(0, tm, rc):
                x = src[r:r + rc, :]
                x_ref[r:r + rc, :] = x
                g, u_hi = _gate_raw(_mod_value(x, nw_ref, sh_ref, sc_ref, grp), wh_ref[...], wl_ref[...])
                u_ref[r:r + rc, :] = u_hi
                g_ref[r:r + rc, :] = g


def _prep_call(lay, xp, xs, mod, gate_w):
    d = lay.d
    tm = lay.row_tile(1024)
    ntp = lay.tp // tm
    rows = pl.BlockSpec((tm, d), lambda i: (i, 0))
    wspec = pl.BlockSpec((LANES, d), lambda i: (0, 0))
    return pl.pallas_call(
        functools.partial(_prep_kernel, lay=lay, tm=tm),
        out_shape=(jax.ShapeDtypeStruct((lay.t, d), F32), jax.ShapeDtypeStruct((lay.t, d), BF16),
                   jax.ShapeDtypeStruct((lay.t, LANES), F32)),
        grid=(lay.t // tm,),
        in_specs=[
            pl.BlockSpec((tm, d), lambda i: (jnp.minimum(i, ntp - 1), 0)),
            pl.BlockSpec((tm, d), lambda i: (jnp.maximum(i - ntp, 0), 0)),
        ] + mod.specs + [wspec, wspec],
        out_specs=(rows, rows, pl.BlockSpec((tm, LANES), lambda i: (i, 0))),
        compiler_params=_cparams(("arbitrary",)),
        name="prep_modulate",
    )(xp, xs, *mod.args, *gate_w)


def _gates_kernel(g_ref, b_ref, tri_ref, o_ref, *, nh, tm):
    l = tri_ref.shape[0]
    tri = tri_ref[...]
    lane = lax.broadcasted_iota(jnp.int32, (l, LANES), 1)
    is_f = ((lane >= nh) & (lane < 2 * nh)) | ((lane >= 3 * nh) & (lane < 4 * nh))
    is_a = (lane < nh) | ((lane >= 2 * nh) & (lane < 3 * nh))
    for r in range(0, tm, l):
        g = g_ref[r:r + l, :] + b_ref[...]
        lf = jnp.where(is_f, _log_sigmoid(g), 0.0)
        hi, mid, lo = _split3(lf)
        prefix = _dot(tri, hi) + _dot(tri, mid) + _dot(tri, lo)
        suffix = jnp.sum(lf, axis=0, keepdims=True) - prefix + lf
        b = jnp.where(lane < 2 * nh, prefix, suffix)
        a = g - pltpu.roll(b, LANES - nh, 1)
        o_ref[r:r + l, :] = jnp.where(is_a, a, b)


def _gates_call(lay, graw, bias, tri, nh):
    t = graw.shape[0]
    l = tri.shape[0]
    tm = lay.row_tile(2048)
    assert tm % l == 0
    return pl.pallas_call(
        functools.partial(_gates_kernel, nh=nh, tm=tm),
        out_shape=jax.ShapeDtypeStruct((t, LANES), F32),
        grid=(t // tm,),
        in_specs=[
            pl.BlockSpec((tm, LANES), lambda i: (i, 0)),
            pl.BlockSpec((1, LANES), lambda i: (0, 0)),
            pl.BlockSpec((l, l), lambda i: (0, 0)),
        ],
        out_specs=pl.BlockSpec((tm, LANES), lambda i: (i, 0)),
        compiler_params=_cparams(("parallel",)),
        name="mlstm_gates",
    )(graw, bias, tri)


def _proj_kernel(u_ref, w_ref, o_ref, w_bf, *, transposed):
    @pl.when(pl.program_id(1) == 0)
    def _():
        w_bf[...] = w_ref[...].astype(BF16)

    if transposed:
        o_ref[...] = _dot_nt(w_bf[...], u_ref[...]).astype(BF16)
    else:
        o_ref[...] = _dot_nt(u_ref[...], w_bf[...]).astype(BF16)


def _proj_call(lay, u, w_in_t, jl, blocks, tn, transposed):
    t, d = u.shape
    tm = 2048 if t % 2048 == 0 else lay.row_tile(1024)
    nb = len(blocks)
    first, gap_at, gap = blocks[0], None, 0
    for idx in range(1, nb):
        if blocks[idx] != blocks[idx - 1] + 1:
            assert gap_at is None
            gap_at, gap = idx, blocks[idx] - blocks[idx - 1] - 1
    wblk = (lambda j: first + j) if gap_at is None else (lambda j: first + j + jnp.where(j >= gap_at, gap, 0))
    if transposed:
        out_shape = jax.ShapeDtypeStruct((nb * tn, t), BF16)
        out_spec = pl.BlockSpec((tn, tm), lambda j, i: (j, i))
    else:
        out_shape = jax.ShapeDtypeStruct((t, nb * tn), BF16)
        out_spec = pl.BlockSpec((tm, tn), lambda j, i: (i, j))
    return pl.pallas_call(
        functools.partial(_proj_kernel, transposed=transposed),
        out_shape=out_shape,
        grid=(nb, t // tm),
        in_specs=[
            pl.BlockSpec((tm, d), lambda j, i: (i, 0)),
            pl.BlockSpec((None, tn, d), lambda j, i: (jl, wblk(j), 0)),
        ],
        out_specs=out_spec,
        scratch_shapes=[pltpu.VMEM((tn, d), BF16)],
        compiler_params=_cparams(("parallel", "arbitrary")),
        name="mlstm_proj_t" if transposed else "mlstm_proj",
    )(u, w_in_t)


def _col(tile, c):
    lane = lax.broadcasted_iota(jnp.int32, tile.shape, 1)
    return jnp.sum(jnp.where(lane == c, tile, 0.0), axis=-1, keepdims=True)


def _dir_masks(l):
    r = lax.broadcasted_iota(jnp.int32, (l, l), 0)
    c = lax.broadcasted_iota(jnp.int32, (l, l), 1)
    return c <= r, c >= r


def _head_epilogue(h, hw, o):
    hn = h * lax.rsqrt(jnp.mean(h * h, axis=-1, keepdims=True) + EPS) * hw
    return (hn * _sigmoid(o.astype(F32))).astype(BF16)


def _row_times_kt(w_row, kt):
    hi, lo = _split2(w_row)
    sub = lax.broadcasted_iota(jnp.int32, (SUBLANES, w_row.shape[1]), 0)
    stacked = jnp.where(sub == 0, hi.astype(F32), jnp.where(sub == 1, lo.astype(F32), 0.0)).astype(BF16)
    res = _dot_nt(stacked, kt)
    return res[0:1, :] + res[1:2, :]


def _mlstm_single_kernel(q_ref, kt_ref, v_ref, o_ref, gp_ref, gpt_ref, hw_ref, out_ref, *, nh, dh, l, scale):
    masks = _dir_masks(l)
    for r0 in range(0, q_ref.shape[0], l):
        rows = slice(r0, r0 + l)
        gp = gp_ref[rows, :]
        for h in range(nh):
            cols = slice(h * dh, (h + 1) * dh)
            qk = _dot(q_ref[rows, cols], kt_ref[cols, rows])
            p = None
            for d in range(2):
                a_r = gpt_ref[2 * nh * d + h:2 * nh * d + h + 1, rows]
                b_c = gp[:, 2 * nh * d + nh + h:2 * nh * d + nh + h + 1]
                g = jnp.where(masks[d], a_r, -jnp.inf)
                m = jnp.maximum(jnp.max(g, axis=-1, keepdims=True), 0.0)
                s = qk * jnp.exp(g - m)
                den = scale * jnp.sum(s, axis=-1, keepdims=True)
                inv = scale / jnp.maximum(jnp.abs(den), jnp.exp(-(b_c + m)))
                p = s * inv if p is None else p + s * inv
            hh = _dot(p.astype(BF16), v_ref[rows, cols])
            out_ref[rows, cols] = _head_epilogue(hh, hw_ref[:, cols], o_ref[rows, cols])


def _mlstm_multi_kernel(q_ref, kt_ref, v_ref, o_ref, gp_ref, gpt_ref, hw_ref, c0_ref, n0_ref, m0_ref,
                        out_ref, cst, cbf, *, nh, nc, l, scale):
    h = pl.program_id(1)
    masks = _dir_masks(l)
    m_in = [[None] * nc for _ in range(2)]
    n_in = [[None] * nc for _ in range(2)]
    for d in range(2):
        cst[...] = c0_ref[d]
        n = n0_ref[pl.ds(d * nh + h, 1), :]
        m = m0_ref[pl.ds(d * nh + h, 1), 0:1]
        order = list(range(nc)) if d == 0 else list(range(nc - 1, -1, -1))
        for step, c in enumerate(order):
            m_in[d][c], n_in[d][c] = m, n
            cbf[d, c] = cst[...].astype(BF16)
            if step + 1 < nc:
                r0 = c * l
                a_r = gpt_ref[pl.ds(2 * nh * d + h, 1), r0:r0 + l]
                b_r = gpt_ref[pl.ds(2 * nh * d + nh + h, 1), r0:r0 + l]
                m_last = jnp.maximum(jnp.max(a_r, axis=-1, keepdims=True), m)
                b_end = b_r[:, l - 1:l] if d == 0 else b_r[:, 0:1]
                decay = jnp.exp(m - m_last)
                w_end = jnp.exp(a_r - m_last)
                kt = kt_ref[:, r0:r0 + l]
                cst[...] = decay * cst[...] + _dot((kt.astype(F32) * w_end).astype(BF16), v_ref[r0:r0 + l, :])
                n = decay * n + _row_times_kt(w_end, kt)
                m = b_end + m_last
    hw = hw_ref[...]
    for c in range(nc):
        r0 = c * l
        q = q_ref[r0:r0 + l, :]
        v = v_ref[r0:r0 + l, :]
        qk = _dot(q, kt_ref[:, r0:r0 + l])
        qf = q.astype(F32)
        gp = gp_ref[r0:r0 + l, :]
        p = inter = None
        for d in range(2):
            m, n = m_in[d][c], n_in[d][c]
            a_r = gpt_ref[pl.ds(2 * nh * d + h, 1), r0:r0 + l]
            b_c = _col(gp, 2 * nh * d + nh + h)
            g = jnp.where(masks[d], a_r, -jnp.inf)
            mt = jnp.maximum(jnp.max(g, axis=-1, keepdims=True), m)
            s = qk * jnp.exp(g - mt)
            w_prev = jnp.exp(m - mt)
            den = scale * (jnp.sum(s, axis=-1, keepdims=True) + w_prev * jnp.sum(qf * n, axis=-1, keepdims=True))
            inv = scale / jnp.maximum(jnp.abs(den), jnp.exp(-(b_c + mt)))
            term = (w_prev * inv) * _dot(q, cbf[d, c])
            p = s * inv if p is None else p + s * inv
            inter = term if inter is None else inter + term
        hh = _dot(p.astype(BF16), v) + inter
        out_ref[r0:r0 + l, :] = _head_epilogue(hh, hw, o_ref[r0:r0 + l, :])


def _mlstm_call(lay, qvo, kt, gp, gpt, hw, nh, dh, prompt, state=None):
    nb, s = (lay.nbp, lay.sp) if prompt else (lay.nbs, lay.ss)
    rb0 = 0 if prompt else lay.tp // s
    scale = dh ** -0.5
    di = nh * dh
    common_in = [
        pl.BlockSpec((s, dh), lambda b, h: (rb0 + b, h)),
        pl.BlockSpec((dh, s), lambda b, h: (h, rb0 + b)),
        pl.BlockSpec((s, dh), lambda b, h: (rb0 + b, nh + h)),
        pl.BlockSpec((s, dh), lambda b, h: (rb0 + b, 2 * nh + h)),
        pl.BlockSpec((s, LANES), lambda b, h: (rb0 + b, 0)),
        pl.BlockSpec((4 * nh, s), lambda b, h: (0, rb0 + b)),
        pl.BlockSpec((1, dh), lambda b, h: (0, h)),
    ]
    if prompt:
        assert s == MLSTM_L and rb0 == 0
        per = 2 if nb % 2 == 0 else 1
        rows = per * s
        return pl.pallas_call(
            functools.partial(_mlstm_single_kernel, nh=nh, dh=dh, l=s, scale=scale),
            out_shape=jax.ShapeDtypeStruct((nb * s, di), BF16),
            grid=(nb // per,),
            in_specs=[
                pl.BlockSpec((rows, di), lambda b: (b, 0)),
                pl.BlockSpec((di, rows), lambda b: (0, b)),
                pl.BlockSpec((rows, di), lambda b: (b, 1)),
                pl.BlockSpec((rows, di), lambda b: (b, 2)),
                pl.BlockSpec((rows, LANES), lambda b: (b, 0)),
                pl.BlockSpec((4 * nh, rows), lambda b: (0, b)),
                pl.BlockSpec((1, di), lambda b: (0, 0)),
            ],
            out_specs=pl.BlockSpec((rows, di), lambda b: (b, 0)),
            compiler_params=_cparams(("parallel",)),
            name="mlstm_prompt",
        )(qvo, kt, qvo, qvo, gp, gpt, hw)
    state_c, jl, n0, m0 = state
    nc = s // MLSTM_L
    return pl.pallas_call(
        functools.partial(_mlstm_multi_kernel, nh=nh, nc=nc, l=MLSTM_L, scale=scale),
        out_shape=jax.ShapeDtypeStruct((nb * s, di), BF16),
        grid=(nb, nh),
        in_specs=common_in + [
            pl.BlockSpec((None, None, 2, None, dh, dh), lambda b, h: (b, jl, 0, h, 0, 0)),
            pl.BlockSpec((None, 2 * nh, dh), lambda b, h: (b, 0, 0)),
            pl.BlockSpec((None, 2 * nh, LANES), lambda b, h: (b, 0, 0)),
        ],
        out_specs=pl.BlockSpec((s, dh), lambda b, h: (b, h)),
        scratch_shapes=[pltpu.VMEM((dh, dh), F32), pltpu.VMEM((2, nc, dh, dh), BF16)],
        compiler_params=_cparams(("parallel", "parallel")),
        name="mlstm_latent",
    )(qvo, kt, qvo, qvo, gp, gpt, hw, state_c, n0, m0)


def _state_kernel(*refs, nl, nh, dh):
    ins, (c_ref, n_ref, m_ref) = refs[:3 * nl], refs[3 * nl:]
    lyr = pl.program_id(0)
    for jl in range(nl):
        kt_ref, v_ref, gpt_ref = ins[3 * jl:3 * jl + 3]

        @pl.when(lyr == jl)
        def _(kt_ref=kt_ref, v_ref=v_ref, gpt_ref=gpt_ref):
            l = v_ref.shape[0]
            sub = lax.broadcasted_iota(jnp.int32, m_ref.shape, 0)
            lane = lax.broadcasted_iota(jnp.int32, m_ref.shape, 1)
            m_all = jnp.zeros(m_ref.shape, F32)
            for h in range(nh):
                kt = kt_ref[h * dh:(h + 1) * dh, :]
                ktf = kt.astype(F32)
                v = v_ref[:, h * dh:(h + 1) * dh]
                for d in range(2):
                    a_r = gpt_ref[2 * nh * d + h:2 * nh * d + h + 1, :]
                    b_r = gpt_ref[2 * nh * d + nh + h:2 * nh * d + nh + h + 1, :]
                    m_last = jnp.maximum(jnp.max(a_r, axis=-1, keepdims=True), 0.0)
                    b_end = b_r[:, l - 1:l] if d == 0 else b_r[:, 0:1]
                    w_end = jnp.exp(a_r - m_last)
                    c_ref[d, h] = _dot((ktf * w_end).astype(BF16), v)
                    n_ref[d, h:h + 1, :] = _row_times_kt(w_end, kt)
                    m_all = jnp.where((sub == d) & (lane == h), b_end + m_last, m_all)
            m_ref[...] = m_all


def _state_call(lay, qvos, kts, gpts, nh, dh):
    nl = len(qvos)
    nbp, s = lay.nbp, lay.sp
    di = nh * dh
    assert s == MLSTM_L

    def pick(jl):
        return lambda lyr, b: jnp.where(lyr == jl, b, jnp.where(lyr < jl, 0, nbp - 1))

    in_specs, args = [], []
    for jl in range(nl):
        pb = pick(jl)
        in_specs.append(pl.BlockSpec((di, s), lambda lyr, b, pb=pb: (0, pb(lyr, b))))
        in_specs.append(pl.BlockSpec((s, di), lambda lyr, b, pb=pb: (pb(lyr, b), 1)))
        in_specs.append(pl.BlockSpec((4 * nh, s), lambda lyr, b, pb=pb: (0, pb(lyr, b))))
        args += [kts[jl], qvos[jl], gpts[jl]]
    return pl.pallas_call(
        functools.partial(_state_kernel, nl=nl, nh=nh, dh=dh),
        out_shape=(
            jax.ShapeDtypeStruct((nbp, nl, 2, nh, dh, dh), F32),
            jax.ShapeDtypeStruct((nbp, nl, 2, nh, dh), F32),
            jax.ShapeDtypeStruct((nbp, nl, 2, nh), F32),
        ),
        grid=(nl, nbp),
        in_specs=in_specs,
        out_specs=(
            pl.BlockSpec((None, None, 2, nh, dh, dh), lambda lyr, b: (b, lyr, 0, 0, 0, 0)),
            pl.BlockSpec((None, None, 2, nh, dh), lambda lyr, b: (b, lyr, 0, 0, 0)),
            pl.BlockSpec((None, None, 2, nh), lambda lyr, b: (b, lyr, 0, 0)),
        ),
        compiler_params=_cparams(("arbitrary", "arbitrary")),
        name="mlstm_prompt_state",
    )(*args)


def _mm_res_kernel(ap_ref, as_ref, w_ref, x_ref, g_ref, nw_ref, sh_ref, sc_ref, o_ref, u_ref, *, lay, tm):
    i = pl.program_id(0)
    grp = lay.group(i * tm)
    gate = g_ref[pl.ds(grp, 1), :]
    w = w_ref[...]
    rc = min(tm, 256)
    for a_ref, cond in ((ap_ref, i * tm < lay.tp), (as_ref, i * tm >= lay.tp)):
        @pl.when(cond)
        def _(a_ref=a_ref):
            for r in range(0, tm, rc):
                xn = x_ref[r:r + rc, :] + gate * _dot(a_ref[r:r + rc, :], w)
                o_ref[r:r + rc, :] = xn
                u_ref[r:r + rc, :] = _mod_value(xn, nw_ref, sh_ref, sc_ref, grp).astype(BF16)


def _mm_res_call(lay, a_p, a_s, w_bf, x, mods, l, mod2):
    kdim = a_p.shape[1]
    t, d = x.shape
    tm = lay.row_tile(1024)
    ntp = lay.tp // tm
    return pl.pallas_call(
        functools.partial(_mm_res_kernel, lay=lay, tm=tm),
        out_shape=(jax.ShapeDtypeStruct(x.shape, F32), jax.ShapeDtypeStruct(x.shape, BF16)),
        grid=(t // tm,),
        in_specs=[
            pl.BlockSpec((tm, kdim), lambda i: (jnp.minimum(i, ntp - 1), 0)),
            pl.BlockSpec((tm, kdim), lambda i: (jnp.maximum(i - ntp, 0), 0)),
            pl.BlockSpec((kdim, d), lambda i: (0, 0)),
            pl.BlockSpec((tm, d), lambda i: (i, 0)),
            lay.mod_spec(l, 2),
        ] + mod2.specs,
        out_specs=(pl.BlockSpec((tm, d), lambda i: (i, 0)), pl.BlockSpec((tm, d), lambda i: (i, 0))),
        compiler_params=_cparams(("arbitrary",)),
        name="mm_residual",
    )(a_p, a_s, w_bf, x, mods, *mod2.args)


def _fnet_kernel(x_ref, u_ref, u2_any, gt_ref, wc_ref, ds_ref, wo_ref, nw_ref, sh_ref, sc_ref, o_ref, u2_ref,
                 ab_scr, *, lay, row_base, groups, norm):
    s, d = x_ref.shape
    cg = d // groups
    grp = lay.group(row_base + pl.program_id(0) * s)
    wc = wc_ref[...]
    rc = min(s, 256)
    for g in range(groups):
        for r in range(0, s, rc):
            ab = _dot(u_ref[r:r + rc, g * cg:(g + 1) * cg], wc)
            ab_scr[r:r + rc, g * cg:(g + 1) * cg] = ab[:, :cg].astype(BF16)
            ab_scr[s + r:s + r + rc, g * cg:(g + 1) * cg] = ab[:, cg:].astype(BF16)
    gate = gt_ref[pl.ds(grp, 1), :]
    wo = wo_ref[...]
    for r in range(0, s, rc):
        y = _dot(ds_ref[r:r + rc, :], ab_scr[...]) * norm
        xn = x_ref[r:r + rc, :] + gate * _dot(y.astype(BF16), wo)
        o_ref[r:r + rc, :] = xn
        u2_ref[r:r + rc, :] = _mod_value(xn, nw_ref, sh_ref, sc_ref, grp).astype(BF16)


def _dft_mats(s, cg):
    kc = np.arange(cg)
    ang_c = 2.0 * np.pi * np.outer(kc, kc) / cg
    wc = np.concatenate([np.cos(ang_c), np.sin(ang_c)], axis=1)
    ks = np.arange(s)
    ang_s = 2.0 * np.pi * np.outer(ks, ks) / s
    ds = np.concatenate([np.cos(ang_s), -np.sin(ang_s)], axis=1)
    return jnp.asarray(wc, dtype=BF16), jnp.asarray(ds, dtype=BF16)


def _fnet_call(lay, x, u1, u2, mods, l, wo_bf, mod2, prompt):
    nb, s = (lay.nbp, lay.sp) if prompt else (lay.nbs, lay.ss)
    rb0 = 0 if prompt else lay.tp // s
    d = lay.d
    cg = d // FNET_GROUPS
    wc, ds = _dft_mats(s, cg)
    kern = functools.partial(_fnet_kernel, lay=lay, row_base=rb0 * s, groups=FNET_GROUPS,
                             norm=1.0 / math.sqrt(s * cg))
    blk = pl.BlockSpec((s, d), lambda b: (rb0 + b, 0))
    return pl.pallas_call(
        kern,
        out_shape=(jax.ShapeDtypeStruct(x.shape, F32), jax.ShapeDtypeStruct(u2.shape, BF16)),
        grid=(nb,),
        in_specs=[
            blk,
            blk,
            pl.BlockSpec(memory_space=pl.ANY),
            lay.mod_spec(l, 2),
            pl.BlockSpec((cg, 2 * cg), lambda b: (0, 0)),
            pl.BlockSpec((s, 2 * s), lambda b: (0, 0)),
            pl.BlockSpec((d, d), lambda b: (0, 0)),
        ] + mod2.specs,
        out_specs=(blk, blk),
        scratch_shapes=[pltpu.VMEM((2 * s, d), BF16)],
        input_output_aliases={0: 0, 2: 1},
        compiler_params=_cparams(("parallel",)),
        name="fnet_prompt" if prompt else "fnet_latent",
    )(x, u1, u2, mods, wc, ds, wo_bf, *mod2.args)


def _glu_kernel(u_ref, wa_ref, wg_ref, ba_ref, bg_ref, o_ref):
    u = u_ref[...]
    a = _dot(u, wa_ref[...]) + ba_ref[...]
    g = _dot(u, wg_ref[...]) + bg_ref[...]
    o_ref[...] = a * _sigmoid(g)


def _glu_call(lay, u, w_bf, bias):
    t, d = u.shape
    cd = w_bf.shape[1] // 2
    tm = lay.row_tile(1024)
    tn = min(cd, 1024)
    nj = cd // tn
    return pl.pallas_call(
        _glu_kernel,
        out_shape=jax.ShapeDtypeStruct((t, cd), F32),
        grid=(t // tm, nj),
        in_specs=[
            pl.BlockSpec((tm, d), lambda i, j: (i, 0)),
            pl.BlockSpec((d, tn), lambda i, j: (0, j)),
            pl.BlockSpec((d, tn), lambda i, j: (0, nj + j)),
            pl.BlockSpec((1, tn), lambda i, j: (0, j)),
            pl.BlockSpec((1, tn), lambda i, j: (0, nj + j)),
        ],
        out_specs=pl.BlockSpec((tm, tn), lambda i, j: (i, j)),
        compiler_params=_cparams(("parallel", "parallel")),
        name="conv_glu",
    )(u, w_bf, w_bf, bias, bias)


def _conv_kernel(c_ref, p_ref, n_ref, wd_ref, bd_ref, lw_ref, lb_ref, w2_ref, b2_ref, x_ref, gt_ref,
                 nw_ref, sh_ref, sc_ref, o_ref, u2_ref, pad, conv, act, *, lay, rb, width):
    i = pl.program_id(0)
    row0 = i * rb
    grp = lay.group(row0)
    seq = jnp.where(row0 < lay.tp, lay.sp, lay.ss)
    pos = jnp.where(row0 < lay.tp, row0 % lay.sp, (row0 - lay.tp) % lay.ss)
    has_prev = (pos != 0).astype(F32)
    has_next = (pos + rb != seq).astype(F32)
    hl = CONV_HALO
    half = width // 2
    cd = c_ref.shape[1]
    span = pad.shape[1]
    pad[0, 0:hl, :] = p_ref[...] * has_prev
    pad[0, hl:hl + rb, :] = c_ref[...]
    pad[0, hl + rb:hl + rb + hl, :] = n_ref[...] * has_next
    for s in range(1, SUBLANES):
        pad[s, 0:span - SUBLANES, :] = pad[0, s:s + span - SUBLANES, :]
    ngrp = 8
    sub = ngrp * SUBLANES
    lanes = 2 * LANES
    assert rb % sub == 0 and cd % lanes == 0

    def conv_block(blk, carry):
        r0 = pl.multiple_of(blk * sub, sub)
        for c0 in range(0, cd, lanes):
            bias = bd_ref[:, c0:c0 + lanes]
            accs = [jnp.zeros((SUBLANES, lanes), F32) + bias for _ in range(ngrp)]
            for k in sorted(range(width), key=lambda k: ((hl - half + k) % SUBLANES, k)):
                q, s = divmod(hl - half + k, SUBLANES)
                wk = wd_ref[k, :, c0:c0 + lanes]
                for gi in range(ngrp):
                    win = pad[s, pl.ds(r0 + (q + gi) * SUBLANES, SUBLANES), c0:c0 + lanes]
                    accs[gi] = accs[gi] + win * wk
            conv[pl.ds(r0, sub), c0:c0 + lanes] = jnp.concatenate(accs, axis=0)
        return carry

    lax.fori_loop(0, rb // sub, conv_block, 0)
    lw = lw_ref[...]
    lb = lb_ref[...]
    lsub = min(rb, 16 * SUBLANES)

    def ln_block(blk, carry):
        r0 = pl.multiple_of(blk * lsub, lsub)
        acc = conv[pl.ds(r0, lsub), :]
        mu = jnp.mean(acc, axis=-1, keepdims=True)
        cen = acc - mu
        var = jnp.mean(cen * cen, axis=-1, keepdims=True)
        y = cen * lax.rsqrt(var + EPS) * lw + lb
        act[pl.ds(r0, lsub), :] = (y * _sigmoid(y)).astype(BF16)
        return carry

    lax.fori_loop(0, rb // lsub, ln_block, 0)
    xn = x_ref[...] + gt_ref[pl.ds(grp, 1), :] * (_dot(act[...], w2_ref[...]) + b2_ref[...])
    o_ref[...] = xn
    u2_ref[...] = _mod_value(xn, nw_ref, sh_ref, sc_ref, grp).astype(BF16)


def _conv_call(lay, glu, wd, bd, lw, lb, w2_bf, b2, x, mods, l, mod2):
    t, cd = glu.shape
    d = x.shape[1]
    rb = lay.row_tile(256)
    hl = CONV_HALO
    assert CONV_WIDTH // 2 <= hl and rb % hl == 0
    nhb = t // hl
    per = rb // hl
    wd_p = jnp.broadcast_to(wd[:, None, :], (CONV_WIDTH, SUBLANES, cd))
    row = lambda a: a.reshape(1, -1)
    rows = pl.BlockSpec((rb, d), lambda i: (i, 0))
    return pl.pallas_call(
        functools.partial(_conv_kernel, lay=lay, rb=rb, width=CONV_WIDTH),
        out_shape=(jax.ShapeDtypeStruct(x.shape, F32), jax.ShapeDtypeStruct(x.shape, BF16)),
        grid=(t // rb,),
        in_specs=[
            pl.BlockSpec((rb, cd), lambda i: (i, 0)),
            pl.BlockSpec((hl, cd), lambda i: (jnp.maximum(i * per - 1, 0), 0)),
            pl.BlockSpec((hl, cd), lambda i: (jnp.minimum((i + 1) * per, nhb - 1), 0)),
            pl.BlockSpec(wd_p.shape, lambda i: (0, 0, 0)),
            pl.BlockSpec((1, cd), lambda i: (0, 0)),
            pl.BlockSpec((1, cd), lambda i: (0, 0)),
            pl.BlockSpec((1, cd), lambda i: (0, 0)),
            pl.BlockSpec((cd, d), lambda i: (0, 0)),
            pl.BlockSpec((1, d), lambda i: (0, 0)),
            rows,
            lay.mod_spec(l, 2),
        ] + mod2.specs,
        out_specs=(rows, rows),
        scratch_shapes=[pltpu.VMEM((SUBLANES, rb + 2 * hl, cd), F32), pltpu.VMEM((rb, cd), F32),
                        pltpu.VMEM((rb, cd), BF16)],
        compiler_params=_cparams(("parallel",)),
        name="conv_dw_ln_pw2",
    )(glu, glu, glu, wd_p, row(bd), row(lw), row(lb), w2_bf, row(b2), x, mods, *mod2.args)


def _router_kernel(u_ref, wr_ref, upper_ref, ltri_ref, o_ref, cnt_ref, *, ng, ne):
    for j in range(o_ref.shape[0]):
        _route_tile(u_ref, wr_ref, upper_ref, ltri_ref, o_ref, cnt_ref, j, ng, ne)


def _route_tile(u_ref, wr_ref, upper_ref, ltri_ref, o_ref, cnt_ref, j, ng, ne):
    tm = upper_ref.shape[0]
    logits = _dot_nt(wr_ref[...], u_ref[j * tm:(j + 1) * tm, :])
    neg = -jnp.inf
    row = lax.broadcasted_iota(jnp.int32, (SUBLANES, tm), 0)

    gl = jnp.where(row < ng, logits[0:SUBLANES, :], neg)
    gmax = jnp.max(gl, axis=0, keepdims=True)
    gidx = jnp.min(jnp.where(gl == gmax, row, SUBLANES), axis=0, keepdims=True)
    g_p = 1.0 / jnp.sum(jnp.where(row < ng, jnp.exp(gl - gmax), 0.0), axis=0, keepdims=True)

    sel = logits[SUBLANES:2 * SUBLANES, :]
    for g in range(1, ng):
        sel = jnp.where(gidx == g, logits[(1 + g) * SUBLANES:(2 + g) * SUBLANES, :], sel)
    v1 = jnp.max(sel, axis=0, keepdims=True)
    i1 = jnp.min(jnp.where(sel == v1, row, SUBLANES), axis=0, keepdims=True)
    sel2 = jnp.where(row == i1, neg, sel)
    v2 = jnp.max(sel2, axis=0, keepdims=True)
    i2 = jnp.min(jnp.where(sel2 == v2, row, SUBLANES), axis=0, keepdims=True)
    e1 = gidx * SUBLANES + i1
    e2 = gidx * SUBLANES + i2
    tt = jnp.exp(v2 - v1)
    p1 = 1.0 / (1.0 + tt)
    gate1 = p1 * g_p
    gate2 = (tt * p1) * g_p

    rowe = lax.broadcasted_iota(jnp.int32, (ne, tm), 0)
    oh1 = rowe == e1
    oh2 = rowe == e2
    oh = jnp.where(oh1 | oh2, 1.0, 0.0)
    groups = jnp.floor((jnp.sum(oh, axis=1, keepdims=True) + (SUBLANES - 1)) * (1.0 / SUBLANES))
    groups_b = jnp.broadcast_to(groups, (ne, LANES))
    start = SUBLANES * _dot(ltri_ref[...], groups_b.astype(BF16))[:, 0:1]
    prefix = _dot(oh.astype(BF16), upper_ref[...]) + start
    pos1 = jnp.sum(jnp.where(oh1, prefix, 0.0), axis=0, keepdims=True)
    pos2 = jnp.sum(jnp.where(oh2, prefix, 0.0), axis=0, keepdims=True)
    cnt_ref[j] = groups_b

    out = jnp.where(row == 0, e1.astype(F32), 0.0)
    out = jnp.where(row == 1, e2.astype(F32), out)
    out = jnp.where(row == 2, gate1, out)
    out = jnp.where(row == 3, gate2, out)
    out = jnp.where(row == 4, pos1, out)
    out = jnp.where(row == 5, pos2, out)
    o_ref[j] = out


def _router_call(u, wr_t, upper, ltri, ng, ne):
    t, d = u.shape
    tm = upper.shape[0]
    nt = t // tm
    per = 2 if nt % 2 == 0 else 1
    assert ne // ng == SUBLANES and ng <= SUBLANES and SUBLANES + ne <= LANES
    return pl.pallas_call(
        functools.partial(_router_kernel, ng=ng, ne=ne),
        out_shape=(jax.ShapeDtypeStruct((nt, SUBLANES, tm), F32), jax.ShapeDtypeStruct((nt, ne, LANES), F32)),
        grid=(nt // per,),
        in_specs=[
            pl.BlockSpec((per * tm, d), lambda i: (i, 0)),
            pl.BlockSpec((LANES, d), lambda i: (0, 0)),
            pl.BlockSpec((tm, tm), lambda i: (0, 0)),
            pl.BlockSpec((ne, ne), lambda i: (0, 0)),
        ],
        out_specs=(pl.BlockSpec((per, SUBLANES, tm), lambda i: (i, 0, 0)),
                   pl.BlockSpec((per, ne, LANES), lambda i: (i, 0, 0))),
        compiler_params=_cparams(("parallel",)),
        name="moe_router",
    )(u, wr_t, upper, ltri)


def _pack_halves(lo, hi):
    lo_bits = lax.shift_right_logical(pltpu.bitcast(lo, U32), jnp.uint32(16))
    hi_bits = pltpu.bitcast(hi, U32) & jnp.uint32(0xFFFF0000)
    return hi_bits | lo_bits


def _unpack_halves(w):
    lo = pltpu.bitcast(lax.shift_left(w, jnp.uint32(16)), F32)
    hi = pltpu.bitcast(w & jnp.uint32(0xFFFF0000), F32)
    return lo.astype(BF16), hi.astype(BF16)


def _round_bf16(x):
    return x.astype(BF16).astype(F32)


def _group_copy(src, src_g, dst, dst_g, sem):
    g8 = lambda g: pl.ds(pl.multiple_of(g * SUBLANES, SUBLANES), SUBLANES)
    return pltpu.make_async_copy(src.at[g8(src_g), :], dst.at[g8(dst_g), :], sem)


def _for_groups(n, fn, unroll=4):
    def body_many(i, c):
        for j in range(unroll):
            fn(i * unroll + j)
        return c

    def body_one(g, c):
        fn(g)
        return c

    full = lax.div(n, jnp.int32(unroll))
    lax.fori_loop(0, full, body_many, 0)
    lax.fori_loop(full * unroll, n, body_one, 0)


def _dispatch_kernel(gdst_ref, ngt_ref, pad0_ref, npad_ref, tail_ref, u_ref, pos_ref, xs_out, loc, zeros, sem,
                     zsem, *, tm, nl, ne):
    i = pl.program_id(0)
    nt = pl.num_programs(0)
    slot = i % 2
    nlg = nl // SUBLANES

    def copy(step, s, g):
        return _group_copy(loc.at[s], g, xs_out, gdst_ref[step * nlg + g], sem.at[s])

    def group_wait(s):
        _group_copy(loc.at[s], 0, xs_out, 0, sem.at[s]).wait()

    def zero_pad(e, start):
        if start:
            _for_groups(npad_ref[e], lambda g: _group_copy(zeros, 0, xs_out, pad0_ref[e] + g, zsem).start())
        else:
            _for_groups(npad_ref[e], lambda g: _group_copy(zeros, 0, xs_out, 0, zsem).wait())

    def zero_block(t):
        first = pl.multiple_of(tail_ref[0] + t * MOE_BM, MOE_BM)
        return pltpu.make_async_copy(zeros, xs_out.at[pl.ds(first, MOE_BM), :], zsem)

    @pl.when(i == 0)
    def _():
        zeros[...] = jnp.zeros(zeros.shape, zeros.dtype)
        for e in range(ne):
            zero_pad(e, True)
        _for_groups(tail_ref[1], lambda t: zero_block(t).start())

    @pl.when(i >= 2)
    def _():
        _for_groups(ngt_ref[i - 2], lambda g: group_wait(slot))

    pos1 = pos_ref[4:5, :].astype(jnp.int32)
    pos2 = pos_ref[5:6, :].astype(jnp.int32)
    half = u_ref.shape[1] // 2
    u = u_ref[...]
    rc = 256
    for r in range(0, nl, rc):
        p = lax.broadcasted_iota(jnp.int32, (rc, tm), 0) + r
        onehot = jnp.where((p == pos1) | (p == pos2), 1.0, 0.0).astype(BF16)
        rows = _dot(onehot, u)
        loc[slot, r:r + rc, :] = _pack_halves(rows[:, :half], rows[:, half:])
    _for_groups(ngt_ref[i], lambda g: copy(i, slot, g).start())

    @pl.when(i == nt - 1)
    def _():
        @pl.when(i >= 1)
        def _():
            _for_groups(ngt_ref[i - 1], lambda g: group_wait(1 - slot))

        _for_groups(ngt_ref[i], lambda g: group_wait(slot))
        for e in range(ne):
            zero_pad(e, False)
        _for_groups(tail_ref[1], lambda t: pltpu.make_async_copy(zeros, xs_out.at[0:MOE_BM, :], zsem).wait())


def _dispatch_call(u, pos_rows, gdst, ngt, pad0, npad, tail, nrows, tm, nl):
    t, d = u.shape
    ne = npad.shape[0]
    grid_spec = pltpu.PrefetchScalarGridSpec(
        num_scalar_prefetch=5,
        grid=(t // tm,),
        in_specs=[
            pl.BlockSpec((tm, d), lambda i, *_: (i, 0)),
            pl.BlockSpec((None, SUBLANES, tm), lambda i, *_: (i, 0, 0)),
        ],
        out_specs=pl.BlockSpec(memory_space=pl.ANY),
        scratch_shapes=[pltpu.VMEM((2, nl, d // 2), U32), pltpu.VMEM((MOE_BM, d // 2), U32),
                        pltpu.SemaphoreType.DMA((2,)), pltpu.SemaphoreType.DMA(())],
    )
    return pl.pallas_call(
        functools.partial(_dispatch_kernel, tm=tm, nl=nl, ne=ne),
        out_shape=jax.ShapeDtypeStruct((nrows, d // 2), U32),
        grid_spec=grid_spec,
        compiler_params=_cparams(("arbitrary",)),
        name="moe_dispatch",
    )(gdst, ngt, pad0, npad, tail, u, pos_rows)


def _expert_kernel(row0_ref, nblk_ref, rows_ref, tail_ref, x_hbm, w13_ref, w2_ref, y_hbm, xbuf, ybuf, w13_bf, w2_bf,
                   xsem, ysem, *, hid):
    e = pl.program_id(0)
    n = nblk_ref[e]
    g0 = row0_ref[e] // MOE_BM
    total = tail_ref[0] // MOE_BM
    nx, ny = xbuf.shape[0], ybuf.shape[0]
    ahead = nx - 1
    half = xbuf.shape[2]

    def rows(g):
        return pl.ds(pl.multiple_of(g * MOE_BM, MOE_BM), MOE_BM)

    def x_copy(g):
        return pltpu.make_async_copy(x_hbm.at[rows(g), :], xbuf.at[g % nx], xsem.at[g % nx])

    def y_copy(g, s):
        return pltpu.make_async_copy(ybuf.at[s], y_hbm.at[rows(g), :], ysem.at[s])

    @pl.when(e == 0)
    def _():
        for g in range(ahead):
            @pl.when(g < total)
            def _(g=g):
                x_copy(g).start()

    @pl.when(n > 0)
    def _():
        w13_bf[...] = w13_ref[...].astype(BF16)
        w2_bf[...] = w2_ref[...].astype(BF16)

        def block(c, carry):
            g = g0 + c
            x_copy(g).wait()

            @pl.when(g + ahead < total)
            def _():
                x_copy(g + ahead).start()

            @pl.when(g >= ny)
            def _():
                y_copy(g - ny, g % ny).wait()

            def compute(nrows):
                x_lo, x_hi = _unpack_halves(xbuf[g % nx, 0:nrows, :])
                hb = _dot(x_lo, w13_bf[:half, :]) + _dot(x_hi, w13_bf[half:, :])
                a = hb[:, :hid]
                act = (a * _sigmoid(a)) * hb[:, hid:]
                y = _round_bf16(_dot(act.astype(BF16), w2_bf[...]))
                ybuf[g % ny, 0:nrows, :] = _pack_halves(y[:, :half], y[:, half:])
                if nrows < MOE_BM:
                    ybuf[g % ny, nrows:, :] = jnp.zeros((MOE_BM - nrows, half), ybuf.dtype)

            valid = rows_ref[e] - c * MOE_BM
            for nrows, cond in ((MOE_BM, valid > MOE_BM // 2), (MOE_BM // 2, valid <= MOE_BM // 2)):
                @pl.when(cond)
                def _(nrows=nrows):
                    compute(nrows)

            y_copy(g, g % ny).start()
            return carry

        lax.fori_loop(0, n, block, 0)

    @pl.when(e == pl.num_programs(0) - 1)
    def _():
        for j in range(ny):
            @pl.when(total - ny + j >= 0)
            def _(j=j):
                g = total - ny + j
                y_copy(g, g % ny).wait()

        ybuf[0] = jnp.zeros(ybuf.shape[1:], ybuf.dtype)
        _for_groups(tail_ref[1], lambda t: y_copy(total + t, 0).start())
        _for_groups(tail_ref[1], lambda t: y_copy(total + t, 0).wait())


def _expert_call(xs, row0, nblk, nrows, tail, w13, w2, l):
    r, half = xs.shape
    d = 2 * half
    ne = w13.shape[1]
    hid = w2.shape[2]
    nx, ny = 5, 3
    grid_spec = pltpu.PrefetchScalarGridSpec(
        num_scalar_prefetch=4,
        grid=(ne,),
        in_specs=[
            pl.BlockSpec(memory_space=pl.ANY),
            pl.BlockSpec((None, None, d, 2 * hid), lambda e, *_: (l, e, 0, 0)),
            pl.BlockSpec((None, None, hid, d), lambda e, *_: (l, e, 0, 0)),
        ],
        out_specs=pl.BlockSpec(memory_space=pl.ANY),
        scratch_shapes=[
            pltpu.VMEM((nx, MOE_BM, half), U32), pltpu.VMEM((ny, MOE_BM, half), U32),
            pltpu.VMEM((d, 2 * hid), BF16), pltpu.VMEM((hid, d), BF16),
            pltpu.SemaphoreType.DMA((nx,)), pltpu.SemaphoreType.DMA((ny,)),
        ],
    )
    return pl.pallas_call(
        functools.partial(_expert_kernel, hid=hid),
        out_shape=jax.ShapeDtypeStruct((r, half), U32),
        grid_spec=grid_spec,
        compiler_params=_cparams(("arbitrary",)),
        name="moe_experts",
    )(row0, nblk, nrows, tail, xs, w13, w2)


def _combine_kernel(gdst_ref, ngt_ref, x_ref, rt_ref, gt_ref, nw_ref, sh_ref, sc_ref, ys_ref, *rest,
                    lay, tm, nl, final, gated):
    if gated:
        wh_ref, wl_ref, out_a, out_b, out_g, loc, sem = rest
    else:
        out_a, out_b, loc, sem = rest
    i = pl.program_id(0)
    nt = pl.num_programs(0)
    slot = i % 2
    nlg = nl // SUBLANES

    def copy(step, s, g):
        return _group_copy(ys_ref, gdst_ref[step * nlg + g], loc.at[s], g, sem.at[s])

    @pl.when(i == 0)
    def _():
        loc[...] = jnp.zeros(loc.shape, loc.dtype)
        _for_groups(ngt_ref[0], lambda g: copy(0, 0, g).start())

    _for_groups(ngt_ref[i], lambda g: _group_copy(ys_ref, 0, loc.at[slot], 0, sem.at[slot]).wait())

    @pl.when(i + 1 < nt)
    def _():
        _for_groups(ngt_ref[i + 1], lambda g: copy(i + 1, 1 - slot, g).start())

    rt = rt_ref[...]
    gate1, gate2 = rt[:, 2:3], rt[:, 3:4]
    pos1, pos2 = rt[:, 4:5].astype(jnp.int32), rt[:, 5:6].astype(jnp.int32)
    half = loc.shape[2]
    rc = 256
    mix_lo = jnp.zeros((tm, half), F32)
    mix_hi = jnp.zeros((tm, half), F32)
    for r in range(0, nl, rc):
        p = lax.broadcasted_iota(jnp.int32, (tm, rc), 1) + r
        wgt = (jnp.where(p == pos1, gate1, 0.0) + jnp.where(p == pos2, gate2, 0.0)).astype(BF16)
        y_lo, y_hi = _unpack_halves(loc[slot, r:r + rc, :])
        mix_lo = mix_lo + _dot(wgt, y_lo)
        mix_hi = mix_hi + _dot(wgt, y_hi)
    grp = lay.group(i * tm)
    gate = gt_ref[pl.ds(grp, 1), :]
    nw = nw_ref[...]
    sh = sh_ref[pl.ds(grp, 1), :]
    sc = sc_ref[pl.ds(grp, 1), :]
    er = 128 if gated else tm
    for r in range(0, tm, er):
        rs = slice(r, r + er)
        x_lo = x_ref[rs, :half] + gate[:, :half] * mix_lo[rs]
        x_hi = x_ref[rs, half:] + gate[:, half:] * mix_hi[rs]
        ms = (jnp.sum(x_lo * x_lo, axis=-1, keepdims=True)
              + jnp.sum(x_hi * x_hi, axis=-1, keepdims=True)) / (2 * half)
        inv = lax.rsqrt(ms + EPS)
        if final:
            y_lo = x_lo * inv * nw[:, :half]
            y_hi = x_hi * inv * nw[:, half:]
            for ref, cond in ((out_a, i * tm < lay.tp), (out_b, i * tm >= lay.tp)):
                @pl.when(cond)
                def _(ref=ref, y_lo=y_lo, y_hi=y_hi):
                    ref[rs, :half] = y_lo
                    ref[rs, half:] = y_hi
            continue
        out_a[rs, :half] = x_lo
        out_a[rs, half:] = x_hi
        u_lo = x_lo * inv * nw[:, :half] * (1.0 + sc[:, :half]) + sh[:, :half]
        u_hi = x_hi * inv * nw[:, half:] * (1.0 + sc[:, half:]) + sh[:, half:]
        if gated:
            (a_hi, a_lo), (b_hi, b_lo) = _split2(u_lo), _split2(u_hi)
            wh, wl = wh_ref[...], wl_ref[...]
            out_g[rs, :] = (_dot_nt(a_hi, wh[:, :half]) + _dot_nt(b_hi, wh[:, half:])
                            + _dot_nt(a_lo, wh[:, :half]) + _dot_nt(b_lo, wh[:, half:])
                            + _dot_nt(a_hi, wl[:, :half]) + _dot_nt(b_hi, wl[:, half:]))
            out_b[rs, :half] = a_hi
            out_b[rs, half:] = b_hi
        else:
            out_b[rs, :half] = u_lo.astype(BF16)
            out_b[rs, half:] = u_hi.astype(BF16)


def _combine_call(lay, x, route, mods, l, mod_next, gdst, ngt, ys, tm, nl, final, gate_w=None):
    t, d = x.shape
    rows = pl.BlockSpec((tm, d), lambda i, *_: (i, 0))
    gated = gate_w is not None
    extra_in, extra_args = [], ()
    if final:
        ntp = lay.tp // tm
        out_specs = (pl.BlockSpec((tm, d), lambda i, *_: (jnp.minimum(i, ntp - 1), 0)),
                     pl.BlockSpec((tm, d), lambda i, *_: (jnp.maximum(i - ntp, 0), 0)))
        out_shape = (jax.ShapeDtypeStruct((lay.tp, d), F32), jax.ShapeDtypeStruct((lay.ts, d), F32))
    else:
        out_specs = (rows, rows)
        out_shape = (jax.ShapeDtypeStruct(x.shape, F32), jax.ShapeDtypeStruct(x.shape, BF16))
        if gated:
            wspec = pl.BlockSpec((LANES, d), lambda i, *_: (0, 0))
            extra_in, extra_args = [wspec, wspec], tuple(gate_w)
            out_specs += (pl.BlockSpec((tm, LANES), lambda i, *_: (i, 0)),)
            out_shape += (jax.ShapeDtypeStruct((t, LANES), F32),)
    grid_spec = pltpu.PrefetchScalarGridSpec(
        num_scalar_prefetch=2,
        grid=(t // tm,),
        in_specs=[
            rows,
            pl.BlockSpec((tm, SUBLANES), lambda i, *_: (i, 0)),
            lay.mod_spec(l, 5),
        ] + mod_next.specs + [pl.BlockSpec(memory_space=pl.ANY)] + extra_in,
        out_specs=out_specs,
        scratch_shapes=[pltpu.VMEM((2, nl, d // 2), U32), pltpu.SemaphoreType.DMA((2,))],
    )
    return pl.pallas_call(
        functools.partial(_combine_kernel, lay=lay, tm=tm, nl=nl, final=final, gated=gated),
        out_shape=out_shape,
        grid_spec=grid_spec,
        compiler_params=_cparams(("arbitrary",)),
        name="moe_combine_final" if final else "moe_combine",
    )(gdst, ngt, x, route, mods, *mod_next.args, ys, *extra_args)


def _moe_layer(lay, x, u2, mods, l, mod_next, final, wr_t, upper, ltri, w13, w2, ng, ne, gate_w=None):
    t, d = x.shape
    tm = upper.shape[0]
    nt = t // tm
    bmg = MOE_BM // SUBLANES
    nl = -(-(MOE_TOP_K * tm + ne * (SUBLANES - 1)) // 256) * 256
    nlg = nl // SUBLANES
    route_t, counts = _router_call(u2, wr_t, upper, ltri, ng, ne)
    route = jnp.swapaxes(route_t, 1, 2).reshape(t, SUBLANES)

    c8 = counts[:, :, 0].astype(jnp.int32)
    lend = jnp.cumsum(c8, axis=1)
    lstart = lend - c8
    ngt = lend[:, -1].astype(jnp.int32)
    tot = jnp.sum(c8, axis=0)
    padded = (tot + bmg - 1) // bmg * bmg
    gend = jnp.cumsum(padded)
    gbase = (gend - padded)[None, :] + jnp.cumsum(c8, axis=0) - c8
    nb = -(-(MOE_TOP_K * t + nt * ne * (SUBLANES - 1)) // MOE_BM) + ne
    g = jnp.arange(nlg, dtype=jnp.int32)[None, :, None]
    owner = (g >= lstart[:, None, :]) & (g < lend[:, None, :])
    gdst = g[:, :, 0] + jnp.sum(jnp.where(owner, (gbase - lstart)[:, None, :], 0), axis=-1)
    gdst = gdst.reshape(nt * nlg).astype(jnp.int32)
    row0 = ((gend - padded) * SUBLANES).astype(jnp.int32)
    nblk = (padded // bmg).astype(jnp.int32)
    tail = jnp.stack([gend[-1] * SUBLANES, nb - gend[-1] // bmg]).astype(jnp.int32)

    pad0 = (gend - padded + tot).astype(jnp.int32)
    npad = (padded - tot).astype(jnp.int32)
    xs = _dispatch_call(u2, route_t, gdst, ngt, pad0, npad, tail, nb * MOE_BM, tm, nl)
    ys = _expert_call(xs, row0, nblk, (tot * SUBLANES).astype(jnp.int32), tail, w13, w2, l)
    return _combine_call(lay, x, route, mods, l, mod_next, gdst, ngt, ys, tm, nl, final, gate_w)


def _lower_tri(n, strict):
    r = np.arange(n)
    m = (r[None, :] < r[:, None]) if strict else (r[None, :] <= r[:, None])
    return jnp.asarray(m.astype(np.float32), dtype=BF16)


def kernel(x_prompt, x_sample, state_C, state_n, state_m, c, c_ctx, ada_w, ada_b, norm1_w, norm2_w, m_w_in, m_b_gate, m_head_norm_w, m_w_out, f_w_out, cv_w_pw1, cv_b_pw1, cv_w_dw, cv_b_dw, cv_ln_w, cv_ln_b, cv_w_pw2, cv_b_pw2, r_w_group, r_w_expert, e_w13, e_w2, final_norm_w):
    nbp, sp, d = x_prompt.shape
    nbs, ss, _ = x_sample.shape
    assert ss % GRID_W == 0
    lay = _Layout(nbp, sp, nbs, ss, d)
    depth = ada_w.shape[0]
    nh, dh = state_C.shape[3], state_C.shape[4]
    di = nh * dh
    ng = r_w_group.shape[2]
    ne = r_w_expert.shape[2]
    assert ng == MOE_GROUPS and ng + ne <= LANES and 4 * nh <= LANES and MOE_TOP_K == 2

    cv = jnp.zeros((lay.ngp, d), F32).at[0].set(c_ctx).at[1:1 + nbs].set(c)
    mods = _ada_call(cv, ada_w, ada_b)

    tri_l = _lower_tri(MLSTM_L, strict=False)
    upper = _lower_tri(lay.row_tile(512), strict=True).T
    ltri = _lower_tri(ne, strict=True)
    row = lambda a: a.reshape(1, -1)
    w_in_t = jnp.swapaxes(m_w_in, 1, 2)
    tn = 1024
    assert di % tn == 0
    nkb = di // tn

    def gate_weights(j):
        wg = jnp.zeros((LANES, d), F32).at[:4 * nh].set(w_in_t[j, 4 * di:])
        return _split2(wg)

    x, u1, graw = _prep_call(lay, x_prompt.reshape(lay.tp, d), x_sample.reshape(lay.ts, d),
                             _Mod(lay, mods, norm1_w[0], 0, 0), gate_weights(0))
    y = None
    qvos, kts, gpts = [], [], []
    for l in range(depth):
        j, kind = l // N_MIXERS, l % N_MIXERS
        mod2 = _Mod(lay, mods, norm2_w[l], l, 3)
        if kind == 0:
            bg = jnp.zeros((1, LANES), F32).at[0, :4 * nh].set(m_b_gate[j])
            gp = _gates_call(lay, graw, bg, tri_l, nh)
            gpt = gp[:, :4 * nh].T
            qvo_blocks = list(range(nkb)) + list(range(2 * nkb, 4 * nkb))
            qvo = _proj_call(lay, u1, w_in_t, j, qvo_blocks, tn, transposed=False)
            kt = _proj_call(lay, u1, w_in_t, j, list(range(nkb, 2 * nkb)), tn, transposed=True)
            hw = row(m_head_norm_w[j])
            hg_p = _mlstm_call(lay, qvo, kt, gp, gpt, hw, nh, dh, prompt=True)
            n0 = state_n[:, j].reshape(nbs, 2 * nh, dh)
            m0 = jnp.broadcast_to(state_m[:, j].reshape(nbs, 2 * nh, 1), (nbs, 2 * nh, LANES))
            hg_s = _mlstm_call(lay, qvo, kt, gp, gpt, hw, nh, dh, prompt=False, state=(state_C, j, n0, m0))
            x, u2 = _mm_res_call(lay, hg_p, hg_s, m_w_out[j].astype(BF16), x, mods, l, mod2)
            qvos.append(qvo)
            kts.append(kt)
            gpts.append(gpt)
        elif kind == 1:
            wo = f_w_out[j].astype(BF16)
            u2 = jnp.zeros((lay.t, d), BF16)
            x, u2 = _fnet_call(lay, x, u1, u2, mods, l, wo, mod2, prompt=True)
            x, u2 = _fnet_call(lay, x, u1, u2, mods, l, wo, mod2, prompt=False)
        else:
            glu = _glu_call(lay, u1, cv_w_pw1[j].astype(BF16), row(cv_b_pw1[j]))
            x, u2 = _conv_call(lay, glu, cv_w_dw[j], cv_b_dw[j], cv_ln_w[j], cv_ln_b[j], cv_w_pw2[j].astype(BF16),
                               cv_b_pw2[j], x, mods, l, mod2)
        wr = jnp.zeros((LANES, d), F32).at[:ng].set(r_w_group[l].T).at[SUBLANES:SUBLANES + ne].set(r_w_expert[l].T)
        final = l + 1 == depth
        mod_next = _Mod(lay, mods, final_norm_w, l, 0) if final else _Mod(lay, mods, norm1_w[l + 1], l + 1, 0)
        next_mlstm = not final and (l + 1) % N_MIXERS == 0
        outs = _moe_layer(lay, x, u2, mods, l, mod_next, final, wr.astype(BF16), upper, ltri, e_w13, e_w2, ng, ne,
                          gate_weights((l + 1) // N_MIXERS) if next_mlstm else None)
        if final:
            y = outs
        elif next_mlstm:
            x, u1, graw = outs
        else:
            x, u1 = outs

    y_prompt = y[0].reshape(nbp, sp, d)
    y_sample = y[1].reshape(nbs, ss, d)
    new_c, new_n, new_m = _state_call(lay, qvos, kts, gpts, nh, dh)
    return (y_prompt, y_sample, new_c, new_n, new_m)
```

```python
import functools
import math

import numpy as np
import jax
import jax.numpy as jnp
from jax import lax
from jax.experimental import pallas as pl
from jax.experimental.pallas import tpu as pltpu

F32 = jnp.float32
BF16 = jnp.bfloat16
U32 = jnp.uint32
EPS = 1e-6
GRID_W = 64
N_MIXERS = 3
FNET_GROUPS = 4
CONV_WIDTH = 31
MOE_GROUPS = 4
MOE_TOP_K = 2

LANES = 128
SUBLANES = 8
MLSTM_L = 256
MOE_BM = 256
CONV_HALO = 16
VMEM_LIMIT = 56 * 1024 * 1024


def _cparams(sem, vmem=VMEM_LIMIT):
    return pltpu.CompilerParams(dimension_semantics=sem, vmem_limit_bytes=vmem)


def _dot(a, b):
    return jnp.dot(a, b, preferred_element_type=F32)


def _dot_nt(a, b):
    return lax.dot_general(a, b, (((1,), (1,)), ((), ())), preferred_element_type=F32)


def _rms(x, w):
    return x * lax.rsqrt(jnp.mean(x * x, axis=-1, keepdims=True) + EPS) * w


def _modulate(x, w, shift, scale):
    return _rms(x, w) * (1.0 + scale) + shift


def _sigmoid(x):
    return 1.0 / (1.0 + jnp.exp(-x))


def _log_sigmoid(x):
    return jnp.minimum(x, 0.0) - jnp.log(1.0 + jnp.exp(-jnp.abs(x)))


def _split2(x):
    hi = x.astype(BF16)
    return hi, (x - hi.astype(F32)).astype(BF16)


def _split3(x):
    hi = x.astype(BF16)
    r1 = x - hi.astype(F32)
    mid = r1.astype(BF16)
    return hi, mid, (r1 - mid.astype(F32)).astype(BF16)


class _Layout:
    def __init__(self, nbp, sp, nbs, ss, d):
        self.nbp, self.sp, self.nbs, self.ss, self.d = nbp, sp, nbs, ss, d
        self.tp, self.ts = nbp * sp, nbs * ss
        self.t = self.tp + self.ts
        assert self.tp % ss == 0, "latent sequences must start on a block boundary of their own length"
        self.ngp = -(-(1 + nbs) // SUBLANES) * SUBLANES

    def group(self, row0):
        return jnp.where(row0 < self.tp, 0, 1 + (row0 - self.tp) // self.ss)

    def row_tile(self, want):
        tm = math.gcd(math.gcd(self.tp, self.ss), want)
        assert tm % SUBLANES == 0
        return tm

    def mod_spec(self, l, chunk):
        return pl.BlockSpec((None, self.ngp, self.d), lambda *_: (l, 0, chunk))

    def row_spec(self):
        return pl.BlockSpec((1, self.d), lambda *_: (0, 0))


class _Mod:
    def __init__(self, lay, mods, nw, l, c_shift):
        self.args = (nw.reshape(1, -1), mods, mods)
        self.specs = [lay.row_spec(), lay.mod_spec(l, c_shift), lay.mod_spec(l, c_shift + 1)]


def _mod_value(x, nw_ref, sh_ref, sc_ref, grp):
    return _modulate(x, nw_ref[...], sh_ref[pl.ds(grp, 1), :], sc_ref[pl.ds(grp, 1), :])


def _ada_kernel(cv_ref, w_ref, b_ref, o_ref):
    s = cv_ref[...]
    s = s * _sigmoid(s)
    o_ref[...] = _dot(s.astype(BF16), w_ref[...].astype(BF16)) + b_ref[...]


def _ada_call(cv, ada_w, ada_b):
    depth, d, n = ada_w.shape
    ngp = cv.shape[0]
    tn = min(n, 2048)
    return pl.pallas_call(
        _ada_kernel,
        out_shape=jax.ShapeDtypeStruct((depth, ngp, n), F32),
        grid=(depth, n // tn),
        in_specs=[
            pl.BlockSpec((ngp, d), lambda l, j: (0, 0)),
            pl.BlockSpec((None, d, tn), lambda l, j: (l, 0, j)),
            pl.BlockSpec((None, 1, tn), lambda l, j: (l, 0, j)),
        ],
        out_specs=pl.BlockSpec((None, ngp, tn), lambda l, j: (l, 0, j)),
        compiler_params=_cparams(("parallel", "parallel")),
        name="ada_mods",
    )(cv, ada_w, ada_b.reshape(depth, 1, n))


def _gate_raw(u, wh, wl):
    u_hi, u_lo = _split2(u)
    return _dot_nt(u_hi, wh) + _dot_nt(u_lo, wh) + _dot_nt(u_hi, wl), u_hi


def _prep_kernel(xp_ref, xs_ref, nw_ref, sh_ref, sc_ref, wh_ref, wl_ref, x_ref, u_ref, g_ref, *, lay, tm):
    i = pl.program_id(0)
    grp = lay.group(i * tm)
    for src, cond in ((xp_ref, i * tm < lay.tp), (xs_ref, i * tm >= lay.tp)):
        @pl.when(cond)
        def _(src=src):
            rc = min(tm, 256)
            for r in range(0, tm, rc):
                x = src[r:r + rc, :]
                x_ref[r:r + rc, :] = x
                g, u_hi = _gate_raw(_mod_value(x, nw_ref, sh_ref, sc_ref, grp), wh_ref[...], wl_ref[...])
                u_ref[r:r + rc, :] = u_hi
                g_ref[r:r + rc, :] = g


def _prep_call(lay, xp, xs, mod, gate_w):
    d = lay.d
    tm = lay.row_tile(1024)
    ntp = lay.tp // tm
    rows = pl.BlockSpec((tm, d), lambda i: (i, 0))
    wspec = pl.BlockSpec((LANES, d), lambda i: (0, 0))
    return pl.pallas_call(
        functools.partial(_prep_kernel, lay=lay, tm=tm),
        out_shape=(jax.ShapeDtypeStruct((lay.t, d), F32), jax.ShapeDtypeStruct((lay.t, d), BF16),
                   jax.ShapeDtypeStruct((lay.t, LANES), F32)),
        grid=(lay.t // tm,),
        in_specs=[
            pl.BlockSpec((tm, d), lambda i: (jnp.minimum(i, ntp - 1), 0)),
            pl.BlockSpec((tm, d), lambda i: (jnp.maximum(i - ntp, 0), 0)),
        ] + mod.specs + [wspec, wspec],
        out_specs=(rows, rows, pl.BlockSpec((tm, LANES), lambda i: (i, 0))),
        compiler_params=_cparams(("arbitrary",)),
        name="prep_modulate",
    )(xp, xs, *mod.args, *gate_w)


def _gates_kernel(g_ref, b_ref, tri_ref, o_ref, *, nh, tm):
    l = tri_ref.shape[0]
    tri = tri_ref[...]
    lane = lax.broadcasted_iota(jnp.int32, (l, LANES), 1)
    is_f = ((lane >= nh) & (lane < 2 * nh)) | ((lane >= 3 * nh) & (lane < 4 * nh))
    is_a = (lane < nh) | ((lane >= 2 * nh) & (lane < 3 * nh))
    for r in range(0, tm, l):
        g = g_ref[r:r + l, :] + b_ref[...]
        lf = jnp.where(is_f, _log_sigmoid(g), 0.0)
        hi, mid, lo = _split3(lf)
        prefix = _dot(tri, hi) + _dot(tri, mid) + _dot(tri, lo)
        suffix = jnp.sum(lf, axis=0, keepdims=True) - prefix + lf
        b = jnp.where(lane < 2 * nh, prefix, suffix)
        a = g - pltpu.roll(b, LANES - nh, 1)
        o_ref[r:r + l, :] = jnp.where(is_a, a, b)


def _gates_call(lay, graw, bias, tri, nh):
    t = graw.shape[0]
    l = tri.shape[0]
    tm = lay.row_tile(2048)
    assert tm % l == 0
    return pl.pallas_call(
        functools.partial(_gates_kernel, nh=nh, tm=tm),
        out_shape=jax.ShapeDtypeStruct((t, LANES), F32),
        grid=(t // tm,),
        in_specs=[
            pl.BlockSpec((tm, LANES), lambda i: (i, 0)),
            pl.BlockSpec((1, LANES), lambda i: (0, 0)),
            pl.BlockSpec((l, l), lambda i: (0, 0)),
        ],
        out_specs=pl.BlockSpec((tm, LANES), lambda i: (i, 0)),
        compiler_params=_cparams(("parallel",)),
        name="mlstm_gates",
    )(graw, bias, tri)


def _proj_kernel(u_ref, w_ref, o_ref, w_bf, *, transposed):
    @pl.when(pl.program_id(1) == 0)
    def _():
        w_bf[...] = w_ref[...].astype(BF16)

    if transposed:
        o_ref[...] = _dot_nt(w_bf[...], u_ref[...]).astype(BF16)
    else:
        o_ref[...] = _dot_nt(u_ref[...], w_bf[...]).astype(BF16)


def _proj_call(lay, u, w_in_t, jl, blocks, tn, transposed):
    t, d = u.shape
    tm = 2048 if t % 2048 == 0 else lay.row_tile(1024)
    nb = len(blocks)
    first, gap_at, gap = blocks[0], None, 0
    for idx in range(1, nb):
        if blocks[idx] != blocks[idx - 1] + 1:
            assert gap_at is None
            gap_at, gap = idx, blocks[idx] - blocks[idx - 1] - 1
    wblk = (lambda j: first + j) if gap_at is None else (lambda j: first + j + jnp.where(j >= gap_at, gap, 0))
    if transposed:
        out_shape = jax.ShapeDtypeStruct((nb * tn, t), BF16)
        out_spec = pl.BlockSpec((tn, tm), lambda j, i: (j, i))
    else:
        out_shape = jax.ShapeDtypeStruct((t, nb * tn), BF16)
        out_spec = pl.BlockSpec((tm, tn), lambda j, i: (i, j))
    return pl.pallas_call(
        functools.partial(_proj_kernel, transposed=transposed),
        out_shape=out_shape,
        grid=(nb, t // tm),
        in_specs=[
            pl.BlockSpec((tm, d), lambda j, i: (i, 0)),
            pl.BlockSpec((None, tn, d), lambda j, i: (jl, wblk(j), 0)),
        ],
        out_specs=out_spec,
        scratch_shapes=[pltpu.VMEM((tn, d), BF16)],
        compiler_params=_cparams(("parallel", "arbitrary")),
        name="mlstm_proj_t" if transposed else "mlstm_proj",
    )(u, w_in_t)


def _col(tile, c):
    lane = lax.broadcasted_iota(jnp.int32, tile.shape, 1)
    return jnp.sum(jnp.where(lane == c, tile, 0.0), axis=-1, keepdims=True)


def _dir_masks(l):
    r = lax.broadcasted_iota(jnp.int32, (l, l), 0)
    c = lax.broadcasted_iota(jnp.int32, (l, l), 1)
    return c <= r, c >= r


def _head_epilogue(h, hw, o):
    hn = h * lax.rsqrt(jnp.mean(h * h, axis=-1, keepdims=True) + EPS) * hw
    return (hn * _sigmoid(o.astype(F32))).astype(BF16)


def _row_times_kt(w_row, kt):
    hi, lo = _split2(w_row)
    sub = lax.broadcasted_iota(jnp.int32, (SUBLANES, w_row.shape[1]), 0)
    stacked = jnp.where(sub == 0, hi.astype(F32), jnp.where(sub == 1, lo.astype(F32), 0.0)).astype(BF16)
    res = _dot_nt(stacked, kt)
    return res[0:1, :] + res[1:2, :]


def _mlstm_single_kernel(q_ref, kt_ref, v_ref, o_ref, gp_ref, gpt_ref, hw_ref, out_ref, *, nh, dh, scale):
    l = q_ref.shape[0]
    gp = gp_ref[...]
    masks = _dir_masks(l)
    for h in range(nh):
        cols = slice(h * dh, (h + 1) * dh)
        qk = _dot(q_ref[:, cols], kt_ref[cols, :])
        p = None
        for d in range(2):
            a_r = gpt_ref[2 * nh * d + h:2 * nh * d + h + 1, :]
            b_c = gp[:, 2 * nh * d + nh + h:2 * nh * d + nh + h + 1]
            g = jnp.where(masks[d], a_r, -jnp.inf)
            m = jnp.maximum(jnp.max(g, axis=-1, keepdims=True), 0.0)
            s = qk * jnp.exp(g - m)
            den = scale * jnp.sum(s, axis=-1, keepdims=True)
            inv = scale / jnp.maximum(jnp.abs(den), jnp.exp(-(b_c + m)))
            p = s * inv if p is None else p + s * inv
        hh = _dot(p.astype(BF16), v_ref[:, cols])
        out_ref[:, cols] = _head_epilogue(hh, hw_ref[:, cols], o_ref[:, cols])


def _mlstm_multi_kernel(q_ref, kt_ref, v_ref, o_ref, gp_ref, gpt_ref, hw_ref, c0_ref, n0_ref, m0_ref,
                        out_ref, cst, cbf, *, nh, nc, l, scale):
    h = pl.program_id(1)
    masks = _dir_masks(l)
    m_in = [[None] * nc for _ in range(2)]
    n_in = [[None] * nc for _ in range(2)]
    for d in range(2):
        cst[...] = c0_ref[d]
        n = n0_ref[pl.ds(d * nh + h, 1), :]
        m = m0_ref[pl.ds(d * nh + h, 1), 0:1]
        order = list(range(nc)) if d == 0 else list(range(nc - 1, -1, -1))
        for step, c in enumerate(order):
            m_in[d][c], n_in[d][c] = m, n
            cbf[d, c] = cst[...].astype(BF16)
            if step + 1 < nc:
                r0 = c * l
                a_r = gpt_ref[pl.ds(2 * nh * d + h, 1), r0:r0 + l]
                b_r = gpt_ref[pl.ds(2 * nh * d + nh + h, 1), r0:r0 + l]
                m_last = jnp.maximum(jnp.max(a_r, axis=-1, keepdims=True), m)
                b_end = b_r[:, l - 1:l] if d == 0 else b_r[:, 0:1]
                decay = jnp.exp(m - m_last)
                w_end = jnp.exp(a_r - m_last)
                kt = kt_ref[:, r0:r0 + l]
                cst[...] = decay * cst[...] + _dot((kt.astype(F32) * w_end).astype(BF16), v_ref[r0:r0 + l, :])
                n = decay * n + _row_times_kt(w_end, kt)
                m = b_end + m_last
    hw = hw_ref[...]
    for c in range(nc):
        r0 = c * l
        q = q_ref[r0:r0 + l, :]
        v = v_ref[r0:r0 + l, :]
        qk = _dot(q, kt_ref[:, r0:r0 + l])
        qf = q.astype(F32)
        gp = gp_ref[r0:r0 + l, :]
        p = inter = None
        for d in range(2):
            m, n = m_in[d][c], n_in[d][c]
            a_r = gpt_ref[pl.ds(2 * nh * d + h, 1), r0:r0 + l]
            b_c = _col(gp, 2 * nh * d + nh + h)
            g = jnp.where(masks[d], a_r, -jnp.inf)
            mt = jnp.maximum(jnp.max(g, axis=-1, keepdims=True), m)
            s = qk * jnp.exp(g - mt)
            w_prev = jnp.exp(m - mt)
            den = scale * (jnp.sum(s, axis=-1, keepdims=True) + w_prev * jnp.sum(qf * n, axis=-1, keepdims=True))
            inv = scale / jnp.maximum(jnp.abs(den), jnp.exp(-(b_c + mt)))
            term = (w_prev * inv) * _dot(q, cbf[d, c])
            p = s * inv if p is None else p + s * inv
            inter = term if inter is None else inter + term
        hh = _dot(p.astype(BF16), v) + inter
        out_ref[r0:r0 + l, :] = _head_epilogue(hh, hw, o_ref[r0:r0 + l, :])


def _mlstm_call(lay, qvo, kt, gp, gpt, hw, nh, dh, prompt, state=None):
    nb, s = (lay.nbp, lay.sp) if prompt else (lay.nbs, lay.ss)
    rb0 = 0 if prompt else lay.tp // s
    scale = dh ** -0.5
    di = nh * dh
    common_in = [
        pl.BlockSpec((s, dh), lambda b, h: (rb0 + b, h)),
        pl.BlockSpec((dh, s), lambda b, h: (h, rb0 + b)),
        pl.BlockSpec((s, dh), lambda b, h: (rb0 + b, nh + h)),
        pl.BlockSpec((s, dh), lambda b, h: (rb0 + b, 2 * nh + h)),
        pl.BlockSpec((s, LANES), lambda b, h: (rb0 + b, 0)),
        pl.BlockSpec((4 * nh, s), lambda b, h: (0, rb0 + b)),
        pl.BlockSpec((1, dh), lambda b, h: (0, h)),
    ]
    if prompt:
        assert s == MLSTM_L
        return pl.pallas_call(
            functools.partial(_mlstm_single_kernel, nh=nh, dh=dh, scale=scale),
            out_shape=jax.ShapeDtypeStruct((nb * s, di), BF16),
            grid=(nb,),
            in_specs=[
                pl.BlockSpec((s, di), lambda b: (rb0 + b, 0)),
                pl.BlockSpec((di, s), lambda b: (0, rb0 + b)),
                pl.BlockSpec((s, di), lambda b: (rb0 + b, 1)),
                pl.BlockSpec((s, di), lambda b: (rb0 + b, 2)),
                pl.BlockSpec((s, LANES), lambda b: (rb0 + b, 0)),
                pl.BlockSpec((4 * nh, s), lambda b: (0, rb0 + b)),
                pl.BlockSpec((1, di), lambda b: (0, 0)),
            ],
            out_specs=pl.BlockSpec((s, di), lambda b: (b, 0)),
            compiler_params=_cparams(("parallel",)),
            name="mlstm_prompt",
        )(qvo, kt, qvo, qvo, gp, gpt, hw)
    state_c, jl, n0, m0 = state
    nc = s // MLSTM_L
    return pl.pallas_call(
        functools.partial(_mlstm_multi_kernel, nh=nh, nc=nc, l=MLSTM_L, scale=scale),
        out_shape=jax.ShapeDtypeStruct((nb * s, di), BF16),
        grid=(nb, nh),
        in_specs=common_in + [
            pl.BlockSpec((None, None, 2, None, dh, dh), lambda b, h: (b, jl, 0, h, 0, 0)),
            pl.BlockSpec((None, 2 * nh, dh), lambda b, h: (b, 0, 0)),
            pl.BlockSpec((None, 2 * nh, LANES), lambda b, h: (b, 0, 0)),
        ],
        out_specs=pl.BlockSpec((s, dh), lambda b, h: (b, h)),
        scratch_shapes=[pltpu.VMEM((dh, dh), F32), pltpu.VMEM((2, nc, dh, dh), BF16)],
        compiler_params=_cparams(("parallel", "parallel")),
        name="mlstm_latent",
    )(qvo, kt, qvo, qvo, gp, gpt, hw, state_c, n0, m0)


def _state_kernel(*refs, nl, nh, dh):
    ins, (c_ref, n_ref, m_ref) = refs[:3 * nl], refs[3 * nl:]
    lyr = pl.program_id(0)
    for jl in range(nl):
        kt_ref, v_ref, gpt_ref = ins[3 * jl:3 * jl + 3]

        @pl.when(lyr == jl)
        def _(kt_ref=kt_ref, v_ref=v_ref, gpt_ref=gpt_ref):
            l = v_ref.shape[0]
            sub = lax.broadcasted_iota(jnp.int32, m_ref.shape, 0)
            lane = lax.broadcasted_iota(jnp.int32, m_ref.shape, 1)
            m_all = jnp.zeros(m_ref.shape, F32)
            for h in range(nh):
                kt = kt_ref[h * dh:(h + 1) * dh, :]
                ktf = kt.astype(F32)
                v = v_ref[:, h * dh:(h + 1) * dh]
                for d in range(2):
                    a_r = gpt_ref[2 * nh * d + h:2 * nh * d + h + 1, :]
                    b_r = gpt_ref[2 * nh * d + nh + h:2 * nh * d + nh + h + 1, :]
                    m_last = jnp.maximum(jnp.max(a_r, axis=-1, keepdims=True), 0.0)
                    b_end = b_r[:, l - 1:l] if d == 0 else b_r[:, 0:1]
                    w_end = jnp.exp(a_r - m_last)
                    c_ref[d, h] = _dot((ktf * w_end).astype(BF16), v)
                    n_ref[d, h:h + 1, :] = _row_times_kt(w_end, kt)
                    m_all = jnp.where((sub == d) & (lane == h), b_end + m_last, m_all)
            m_ref[...] = m_all


def _state_call(lay, qvos, kts, gpts, nh, dh):
    nl = len(qvos)
    nbp, s = lay.nbp, lay.sp
    di = nh * dh
    assert s == MLSTM_L

    def pick(jl):
        return lambda lyr, b: jnp.where(lyr == jl, b, jnp.where(lyr < jl, 0, nbp - 1))

    in_specs, args = [], []
    for jl in range(nl):
        pb = pick(jl)
        in_specs.append(pl.BlockSpec((di, s), lambda lyr, b, pb=pb: (0, pb(lyr, b))))
        in_specs.append(pl.BlockSpec((s, di), lambda lyr, b, pb=pb: (pb(lyr, b), 1)))
        in_specs.append(pl.BlockSpec((4 * nh, s), lambda lyr, b, pb=pb: (0, pb(lyr, b))))
        args += [kts[jl], qvos[jl], gpts[jl]]
    return pl.pallas_call(
        functools.partial(_state_kernel, nl=nl, nh=nh, dh=dh),
        out_shape=(
            jax.ShapeDtypeStruct((nbp, nl, 2, nh, dh, dh), F32),
            jax.ShapeDtypeStruct((nbp, nl, 2, nh, dh), F32),
            jax.ShapeDtypeStruct((nbp, nl, 2, nh), F32),
        ),
        grid=(nl, nbp),
        in_specs=in_specs,
        out_specs=(
            pl.BlockSpec((None, None, 2, nh, dh, dh), lambda lyr, b: (b, lyr, 0, 0, 0, 0)),
            pl.BlockSpec((None, None, 2, nh, dh), lambda lyr, b: (b, lyr, 0, 0, 0)),
            pl.BlockSpec((None, None, 2, nh), lambda lyr, b: (b, lyr, 0, 0)),
        ),
        compiler_params=_cparams(("arbitrary", "arbitrary")),
        name="mlstm_prompt_state",
    )(*args)


def _mm_res_kernel(ap_ref, as_ref, w_ref, x_ref, g_ref, nw_ref, sh_ref, sc_ref, o_ref, u_ref, *, lay, tm):
    i = pl.program_id(0)
    grp = lay.group(i * tm)
    gate = g_ref[pl.ds(grp, 1), :]
    w = w_ref[...]
    rc = min(tm, 256)
    for a_ref, cond in ((ap_ref, i * tm < lay.tp), (as_ref, i * tm >= lay.tp)):
        @pl.when(cond)
        def _(a_ref=a_ref):
            for r in range(0, tm, rc):
                xn = x_ref[r:r + rc, :] + gate * _dot(a_ref[r:r + rc, :], w)
                o_ref[r:r + rc, :] = xn
                u_ref[r:r + rc, :] = _mod_value(xn, nw_ref, sh_ref, sc_ref, grp).astype(BF16)


def _mm_res_call(lay, a_p, a_s, w_bf, x, mods, l, mod2):
    kdim = a_p.shape[1]
    t, d = x.shape
    tm = lay.row_tile(1024)
    ntp = lay.tp // tm
    return pl.pallas_call(
        functools.partial(_mm_res_kernel, lay=lay, tm=tm),
        out_shape=(jax.ShapeDtypeStruct(x.shape, F32), jax.ShapeDtypeStruct(x.shape, BF16)),
        grid=(t // tm,),
        in_specs=[
            pl.BlockSpec((tm, kdim), lambda i: (jnp.minimum(i, ntp - 1), 0)),
            pl.BlockSpec((tm, kdim), lambda i: (jnp.maximum(i - ntp, 0), 0)),
            pl.BlockSpec((kdim, d), lambda i: (0, 0)),
            pl.BlockSpec((tm, d), lambda i: (i, 0)),
            lay.mod_spec(l, 2),
        ] + mod2.specs,
        out_specs=(pl.BlockSpec((tm, d), lambda i: (i, 0)), pl.BlockSpec((tm, d), lambda i: (i, 0))),
        compiler_params=_cparams(("arbitrary",)),
        name="mm_residual",
    )(a_p, a_s, w_bf, x, mods, *mod2.args)


def _fnet_kernel(x_ref, u_ref, u2_any, gt_ref, wc_ref, ds_ref, wo_ref, nw_ref, sh_ref, sc_ref, o_ref, u2_ref,
                 ab_scr, *, lay, row_base, groups, norm):
    s, d = x_ref.shape
    cg = d // groups
    grp = lay.group(row_base + pl.program_id(0) * s)
    wc = wc_ref[...]
    rc = min(s, 256)
    for g in range(groups):
        for r in range(0, s, rc):
            ab = _dot(u_ref[r:r + rc, g * cg:(g + 1) * cg], wc)
            ab_scr[r:r + rc, g * cg:(g + 1) * cg] = ab[:, :cg].astype(BF16)
            ab_scr[s + r:s + r + rc, g * cg:(g + 1) * cg] = ab[:, cg:].astype(BF16)
    gate = gt_ref[pl.ds(grp, 1), :]
    wo = wo_ref[...]
    for r in range(0, s, rc):
        y = _dot(ds_ref[r:r + rc, :], ab_scr[...]) * norm
        xn = x_ref[r:r + rc, :] + gate * _dot(y.astype(BF16), wo)
        o_ref[r:r + rc, :] = xn
        u2_ref[r:r + rc, :] = _mod_value(xn, nw_ref, sh_ref, sc_ref, grp).astype(BF16)


def _dft_mats(s, cg):
    kc = np.arange(cg)
    ang_c = 2.0 * np.pi * np.outer(kc, kc) / cg
    wc = np.concatenate([np.cos(ang_c), np.sin(ang_c)], axis=1)
    ks = np.arange(s)
    ang_s = 2.0 * np.pi * np.outer(ks, ks) / s
    ds = np.concatenate([np.cos(ang_s), -np.sin(ang_s)], axis=1)
    return jnp.asarray(wc, dtype=BF16), jnp.asarray(ds, dtype=BF16)


def _fnet_call(lay, x, u1, u2, mods, l, wo_bf, mod2, prompt):
    nb, s = (lay.nbp, lay.sp) if prompt else (lay.nbs, lay.ss)
    rb0 = 0 if prompt else lay.tp // s
    d = lay.d
    cg = d // FNET_GROUPS
    wc, ds = _dft_mats(s, cg)
    kern = functools.partial(_fnet_kernel, lay=lay, row_base=rb0 * s, groups=FNET_GROUPS,
                             norm=1.0 / math.sqrt(s * cg))
    blk = pl.BlockSpec((s, d), lambda b: (rb0 + b, 0))
    return pl.pallas_call(
        kern,
        out_shape=(jax.ShapeDtypeStruct(x.shape, F32), jax.ShapeDtypeStruct(u2.shape, BF16)),
        grid=(nb,),
        in_specs=[
            blk,
            blk,
            pl.BlockSpec(memory_space=pl.ANY),
            lay.mod_spec(l, 2),
            pl.BlockSpec((cg, 2 * cg), lambda b: (0, 0)),
            pl.BlockSpec((s, 2 * s), lambda b: (0, 0)),
            pl.BlockSpec((d, d), lambda b: (0, 0)),
        ] + mod2.specs,
        out_specs=(blk, blk),
        scratch_shapes=[pltpu.VMEM((2 * s, d), BF16)],
        input_output_aliases={0: 0, 2: 1},
        compiler_params=_cparams(("parallel",)),
        name="fnet_prompt" if prompt else "fnet_latent",
    )(x, u1, u2, mods, wc, ds, wo_bf, *mod2.args)


def _glu_kernel(u_ref, wa_ref, wg_ref, ba_ref, bg_ref, o_ref):
    u = u_ref[...]
    a = _dot(u, wa_ref[...]) + ba_ref[...]
    g = _dot(u, wg_ref[...]) + bg_ref[...]
    o_ref[...] = a * _sigmoid(g)


def _glu_call(lay, u, w_bf, bias):
    t, d = u.shape
    cd = w_bf.shape[1] // 2
    tm = lay.row_tile(1024)
    tn = min(cd, 1024)
    nj = cd // tn
    return pl.pallas_call(
        _glu_kernel,
        out_shape=jax.ShapeDtypeStruct((t, cd), F32),
        grid=(t // tm, nj),
        in_specs=[
            pl.BlockSpec((tm, d), lambda i, j: (i, 0)),
            pl.BlockSpec((d, tn), lambda i, j: (0, j)),
            pl.BlockSpec((d, tn), lambda i, j: (0, nj + j)),
            pl.BlockSpec((1, tn), lambda i, j: (0, j)),
            pl.BlockSpec((1, tn), lambda i, j: (0, nj + j)),
        ],
        out_specs=pl.BlockSpec((tm, tn), lambda i, j: (i, j)),
        compiler_params=_cparams(("parallel", "parallel")),
        name="conv_glu",
    )(u, w_bf, w_bf, bias, bias)


def _conv_kernel(c_ref, p_ref, n_ref, wd_ref, bd_ref, lw_ref, lb_ref, w2_ref, b2_ref, x_ref, gt_ref,
                 nw_ref, sh_ref, sc_ref, o_ref, u2_ref, pad, conv, act, *, lay, rb, width):
    i = pl.program_id(0)
    row0 = i * rb
    grp = lay.group(row0)
    seq = jnp.where(row0 < lay.tp, lay.sp, lay.ss)
    pos = jnp.where(row0 < lay.tp, row0 % lay.sp, (row0 - lay.tp) % lay.ss)
    has_prev = (pos != 0).astype(F32)
    has_next = (pos + rb != seq).astype(F32)
    hl = CONV_HALO
    half = width // 2
    cd = c_ref.shape[1]
    span = pad.shape[1]
    pad[0, 0:hl, :] = p_ref[...] * has_prev
    pad[0, hl:hl + rb, :] = c_ref[...]
    pad[0, hl + rb:hl + rb + hl, :] = n_ref[...] * has_next
    for s in range(1, SUBLANES):
        pad[s, 0:span - SUBLANES, :] = pad[0, s:s + span - SUBLANES, :]
    ngrp = 8
    sub = ngrp * SUBLANES
    lanes = 2 * LANES
    assert rb % sub == 0 and cd % lanes == 0

    def conv_block(blk, carry):
        r0 = pl.multiple_of(blk * sub, sub)
        for c0 in range(0, cd, lanes):
            bias = bd_ref[:, c0:c0 + lanes]
            accs = [jnp.zeros((SUBLANES, lanes), F32) + bias for _ in range(ngrp)]
            for k in sorted(range(width), key=lambda k: ((hl - half + k) % SUBLANES, k)):
                q, s = divmod(hl - half + k, SUBLANES)
                wk = wd_ref[k, :, c0:c0 + lanes]
                for gi in range(ngrp):
                    win = pad[s, pl.ds(r0 + (q + gi) * SUBLANES, SUBLANES), c0:c0 + lanes]
                    accs[gi] = accs[gi] + win * wk
            conv[pl.ds(r0, sub), c0:c0 + lanes] = jnp.concatenate(accs, axis=0)
        return carry

    lax.fori_loop(0, rb // sub, conv_block, 0)
    lw = lw_ref[...]
    lb = lb_ref[...]
    lsub = min(rb, 16 * SUBLANES)

    def ln_block(blk, carry):
        r0 = pl.multiple_of(blk * lsub, lsub)
        acc = conv[pl.ds(r0, lsub), :]
        mu = jnp.mean(acc, axis=-1, keepdims=True)
        cen = acc - mu
        var = jnp.mean(cen * cen, axis=-1, keepdims=True)
        y = cen * lax.rsqrt(var + EPS) * lw + lb
        act[pl.ds(r0, lsub), :] = (y * _sigmoid(y)).astype(BF16)
        return carry

    lax.fori_loop(0, rb // lsub, ln_block, 0)
    xn = x_ref[...] + gt_ref[pl.ds(grp, 1), :] * (_dot(act[...], w2_ref[...]) + b2_ref[...])
    o_ref[...] = xn
    u2_ref[...] = _mod_value(xn, nw_ref, sh_ref, sc_ref, grp).astype(BF16)


def _conv_call(lay, glu, wd, bd, lw, lb, w2_bf, b2, x, mods, l, mod2):
    t, cd = glu.shape
    d = x.shape[1]
    rb = lay.row_tile(256)
    hl = CONV_HALO
    assert CONV_WIDTH // 2 <= hl and rb % hl == 0
    nhb = t // hl
    per = rb // hl
    wd_p = jnp.broadcast_to(wd[:, None, :], (CONV_WIDTH, SUBLANES, cd))
    row = lambda a: a.reshape(1, -1)
    rows = pl.BlockSpec((rb, d), lambda i: (i, 0))
    return pl.pallas_call(
        functools.partial(_conv_kernel, lay=lay, rb=rb, width=CONV_WIDTH),
        out_shape=(jax.ShapeDtypeStruct(x.shape, F32), jax.ShapeDtypeStruct(x.shape, BF16)),
        grid=(t // rb,),
        in_specs=[
            pl.BlockSpec((rb, cd), lambda i: (i, 0)),
            pl.BlockSpec((hl, cd), lambda i: (jnp.maximum(i * per - 1, 0), 0)),
            pl.BlockSpec((hl, cd), lambda i: (jnp.minimum((i + 1) * per, nhb - 1), 0)),
            pl.BlockSpec(wd_p.shape, lambda i: (0, 0, 0)),
            pl.BlockSpec((1, cd), lambda i: (0, 0)),
            pl.BlockSpec((1, cd), lambda i: (0, 0)),
            pl.BlockSpec((1, cd), lambda i: (0, 0)),
            pl.BlockSpec((cd, d), lambda i: (0, 0)),
            pl.BlockSpec((1, d), lambda i: (0, 0)),
            rows,
            lay.mod_spec(l, 2),
        ] + mod2.specs,
        out_specs=(rows, rows),
        scratch_shapes=[pltpu.VMEM((SUBLANES, rb + 2 * hl, cd), F32), pltpu.VMEM((rb, cd), F32),
                        pltpu.VMEM((rb, cd), BF16)],
        compiler_params=_cparams(("parallel",)),
        name="conv_dw_ln_pw2",
    )(glu, glu, glu, wd_p, row(bd), row(lw), row(lb), w2_bf, row(b2), x, mods, *mod2.args)


def _router_kernel(u_ref, wr_ref, upper_ref, ltri_ref, o_ref, cnt_ref, *, ng, ne):
    for j in range(o_ref.shape[0]):
        _route_tile(u_ref, wr_ref, upper_ref, ltri_ref, o_ref, cnt_ref, j, ng, ne)


def _route_tile(u_ref, wr_ref, upper_ref, ltri_ref, o_ref, cnt_ref, j, ng, ne):
    tm = upper_ref.shape[0]
    logits = _dot_nt(wr_ref[...], u_ref[j * tm:(j + 1) * tm, :])
    neg = -jnp.inf
    row = lax.broadcasted_iota(jnp.int32, (SUBLANES, tm), 0)

    gl = jnp.where(row < ng, logits[0:SUBLANES, :], neg)
    gmax = jnp.max(gl, axis=0, keepdims=True)
    gidx = jnp.min(jnp.where(gl == gmax, row, SUBLANES), axis=0, keepdims=True)
    g_p = 1.0 / jnp.sum(jnp.where(row < ng, jnp.exp(gl - gmax), 0.0), axis=0, keepdims=True)

    sel = logits[SUBLANES:2 * SUBLANES, :]
    for g in range(1, ng):
        sel = jnp.where(gidx == g, logits[(1 + g) * SUBLANES:(2 + g) * SUBLANES, :], sel)
    v1 = jnp.max(sel, axis=0, keepdims=True)
    i1 = jnp.min(jnp.where(sel == v1, row, SUBLANES), axis=0, keepdims=True)
    sel2 = jnp.where(row == i1, neg, sel)
    v2 = jnp.max(sel2, axis=0, keepdims=True)
    i2 = jnp.min(jnp.where(sel2 == v2, row, SUBLANES), axis=0, keepdims=True)
    e1 = gidx * SUBLANES + i1
    e2 = gidx * SUBLANES + i2
    tt = jnp.exp(v2 - v1)
    p1 = 1.0 / (1.0 + tt)
    gate1 = p1 * g_p
    gate2 = (tt * p1) * g_p

    rowe = lax.broadcasted_iota(jnp.int32, (ne, tm), 0)
    oh1 = rowe == e1
    oh2 = rowe == e2
    oh = jnp.where(oh1 | oh2, 1.0, 0.0)
    groups = jnp.floor((jnp.sum(oh, axis=1, keepdims=True) + (SUBLANES - 1)) * (1.0 / SUBLANES))
    groups_b = jnp.broadcast_to(groups, (ne, LANES))
    start = SUBLANES * _dot(ltri_ref[...], groups_b.astype(BF16))[:, 0:1]
    prefix = _dot(oh.astype(BF16), upper_ref[...]) + start
    pos1 = jnp.sum(jnp.where(oh1, prefix, 0.0), axis=0, keepdims=True)
    pos2 = jnp.sum(jnp.where(oh2, prefix, 0.0), axis=0, keepdims=True)
    cnt_ref[j] = groups_b

    out = jnp.where(row == 0, e1.astype(F32), 0.0)
    out = jnp.where(row == 1, e2.astype(F32), out)
    out = jnp.where(row == 2, gate1, out)
    out = jnp.where(row == 3, gate2, out)
    out = jnp.where(row == 4, pos1, out)
    out = jnp.where(row == 5, pos2, out)
    o_ref[j] = out


def _router_call(u, wr_t, upper, ltri, ng, ne):
    t, d = u.shape
    tm = upper.shape[0]
    nt = t // tm
    per = 2 if nt % 2 == 0 else 1
    assert ne // ng == SUBLANES and ng <= SUBLANES and SUBLANES + ne <= LANES
    return pl.pallas_call(
        functools.partial(_router_kernel, ng=ng, ne=ne),
        out_shape=(jax.ShapeDtypeStruct((nt, SUBLANES, tm), F32), jax.ShapeDtypeStruct((nt, ne, LANES), F32)),
        grid=(nt // per,),
        in_specs=[
            pl.BlockSpec((per * tm, d), lambda i: (i, 0)),
            pl.BlockSpec((LANES, d), lambda i: (0, 0)),
            pl.BlockSpec((tm, tm), lambda i: (0, 0)),
            pl.BlockSpec((ne, ne), lambda i: (0, 0)),
        ],
        out_specs=(pl.BlockSpec((per, SUBLANES, tm), lambda i: (i, 0, 0)),
                   pl.BlockSpec((per, ne, LANES), lambda i: (i, 0, 0))),
        compiler_params=_cparams(("parallel",)),
        name="moe_router",
    )(u, wr_t, upper, ltri)


def _pack_halves(lo, hi):
    lo_bits = lax.shift_right_logical(pltpu.bitcast(lo, U32), jnp.uint32(16))
    hi_bits = pltpu.bitcast(hi, U32) & jnp.uint32(0xFFFF0000)
    return hi_bits | lo_bits


def _unpack_halves(w):
    lo = pltpu.bitcast(lax.shift_left(w, jnp.uint32(16)), F32)
    hi = pltpu.bitcast(w & jnp.uint32(0xFFFF0000), F32)
    return lo.astype(BF16), hi.astype(BF16)


def _round_bf16(x):
    return x.astype(BF16).astype(F32)


def _group_copy(src, src_g, dst, dst_g, sem):
    g8 = lambda g: pl.ds(pl.multiple_of(g * SUBLANES, SUBLANES), SUBLANES)
    return pltpu.make_async_copy(src.at[g8(src_g), :], dst.at[g8(dst_g), :], sem)


def _for_groups(n, fn, unroll=4):
    def body_many(i, c):
        for j in range(unroll):
            fn(i * unroll + j)
        return c

    def body_one(g, c):
        fn(g)
        return c

    full = lax.div(n, jnp.int32(unroll))
    lax.fori_loop(0, full, body_many, 0)
    lax.fori_loop(full * unroll, n, body_one, 0)


def _dispatch_kernel(gdst_ref, ngt_ref, pad0_ref, npad_ref, tail_ref, u_ref, pos_ref, xs_out, loc, zeros, sem,
                     zsem, *, tm, nl, ne):
    i = pl.program_id(0)
    nt = pl.num_programs(0)
    slot = i % 2
    nlg = nl // SUBLANES

    def copy(step, s, g):
        return _group_copy(loc.at[s], g, xs_out, gdst_ref[step * nlg + g], sem.at[s])

    def group_wait(s):
        _group_copy(loc.at[s], 0, xs_out, 0, sem.at[s]).wait()

    def zero_pad(e, start):
        if start:
            _for_groups(npad_ref[e], lambda g: _group_copy(zeros, 0, xs_out, pad0_ref[e] + g, zsem).start())
        else:
            _for_groups(npad_ref[e], lambda g: _group_copy(zeros, 0, xs_out, 0, zsem).wait())

    def zero_block(t):
        first = pl.multiple_of(tail_ref[0] + t * MOE_BM, MOE_BM)
        return pltpu.make_async_copy(zeros, xs_out.at[pl.ds(first, MOE_BM), :], zsem)

    @pl.when(i == 0)
    def _():
        zeros[...] = jnp.zeros(zeros.shape, zeros.dtype)
        for e in range(ne):
            zero_pad(e, True)
        _for_groups(tail_ref[1], lambda t: zero_block(t).start())

    @pl.when(i >= 2)
    def _():
        _for_groups(ngt_ref[i - 2], lambda g: group_wait(slot))

    pos1 = pos_ref[4:5, :].astype(jnp.int32)
    pos2 = pos_ref[5:6, :].astype(jnp.int32)
    half = u_ref.shape[1] // 2
    u = u_ref[...]
    rc = 256
    for r in range(0, nl, rc):
        p = lax.broadcasted_iota(jnp.int32, (rc, tm), 0) + r
        onehot = jnp.where((p == pos1) | (p == pos2), 1.0, 0.0).astype(BF16)
        rows = _dot(onehot, u)
        loc[slot, r:r + rc, :] = _pack_halves(rows[:, :half], rows[:, half:])
    _for_groups(ngt_ref[i], lambda g: copy(i, slot, g).start())

    @pl.when(i == nt - 1)
    def _():
        @pl.when(i >= 1)
        def _():
            _for_groups(ngt_ref[i - 1], lambda g: group_wait(1 - slot))

        _for_groups(ngt_ref[i], lambda g: group_wait(slot))
        for e in range(ne):
            zero_pad(e, False)
        _for_groups(tail_ref[1], lambda t: pltpu.make_async_copy(zeros, xs_out.at[0:MOE_BM, :], zsem).wait())


def _dispatch_call(u, pos_rows, gdst, ngt, pad0, npad, tail, nrows, tm, nl):
    t, d = u.shape
    ne = npad.shape[0]
    grid_spec = pltpu.PrefetchScalarGridSpec(
        num_scalar_prefetch=5,
        grid=(t // tm,),
        in_specs=[
            pl.BlockSpec((tm, d), lambda i, *_: (i, 0)),
            pl.BlockSpec((None, SUBLANES, tm), lambda i, *_: (i, 0, 0)),
        ],
        out_specs=pl.BlockSpec(memory_space=pl.ANY),
        scratch_shapes=[pltpu.VMEM((2, nl, d // 2), U32), pltpu.VMEM((MOE_BM, d // 2), U32),
                        pltpu.SemaphoreType.DMA((2,)), pltpu.SemaphoreType.DMA(())],
    )
    return pl.pallas_call(
        functools.partial(_dispatch_kernel, tm=tm, nl=nl, ne=ne),
        out_shape=jax.ShapeDtypeStruct((nrows, d // 2), U32),
        grid_spec=grid_spec,
        compiler_params=_cparams(("arbitrary",)),
        name="moe_dispatch",
    )(gdst, ngt, pad0, npad, tail, u, pos_rows)


def _expert_kernel(row0_ref, nblk_ref, rows_ref, tail_ref, x_hbm, w13_ref, w2_ref, y_hbm, xbuf, ybuf, w13_bf, w2_bf,
                   xsem, ysem, *, hid):
    e = pl.program_id(0)
    n = nblk_ref[e]
    g0 = row0_ref[e] // MOE_BM
    total = tail_ref[0] // MOE_BM
    nx, ny = xbuf.shape[0], ybuf.shape[0]
    ahead = nx - 1
    half = xbuf.shape[2]

    def rows(g):
        return pl.ds(pl.multiple_of(g * MOE_BM, MOE_BM), MOE_BM)

    def x_copy(g):
        return pltpu.make_async_copy(x_hbm.at[rows(g), :], xbuf.at[g % nx], xsem.at[g % nx])

    def y_copy(g, s):
        return pltpu.make_async_copy(ybuf.at[s], y_hbm.at[rows(g), :], ysem.at[s])

    @pl.when(e == 0)
    def _():
        for g in range(ahead):
            @pl.when(g < total)
            def _(g=g):
                x_copy(g).start()

    @pl.when(n > 0)
    def _():
        w13_bf[...] = w13_ref[...].astype(BF16)
        w2_bf[...] = w2_ref[...].astype(BF16)

        def block(c, carry):
            g = g0 + c
            x_copy(g).wait()

            @pl.when(g + ahead < total)
            def _():
                x_copy(g + ahead).start()

            @pl.when(g >= ny)
            def _():
                y_copy(g - ny, g % ny).wait()

            def compute(nrows):
                x_lo, x_hi = _unpack_halves(xbuf[g % nx, 0:nrows, :])
                hb = _dot(x_lo, w13_bf[:half, :]) + _dot(x_hi, w13_bf[half:, :])
                a = hb[:, :hid]
                act = (a * _sigmoid(a)) * hb[:, hid:]
                y = _round_bf16(_dot(act.astype(BF16), w2_bf[...]))
                ybuf[g % ny, 0:nrows, :] = _pack_halves(y[:, :half], y[:, half:])
                if nrows < MOE_BM:
                    ybuf[g % ny, nrows:, :] = jnp.zeros((MOE_BM - nrows, half), ybuf.dtype)

            valid = rows_ref[e] - c * MOE_BM
            for nrows, cond in ((MOE_BM, valid > MOE_BM // 2), (MOE_BM // 2, valid <= MOE_BM // 2)):
                @pl.when(cond)
                def _(nrows=nrows):
                    compute(nrows)

            y_copy(g, g % ny).start()
            return carry

        lax.fori_loop(0, n, block, 0)

    @pl.when(e == pl.num_programs(0) - 1)
    def _():
        for j in range(ny):
            @pl.when(total - ny + j >= 0)
            def _(j=j):
                g = total - ny + j
                y_copy(g, g % ny).wait()

        ybuf[0] = jnp.zeros(ybuf.shape[1:], ybuf.dtype)
        _for_groups(tail_ref[1], lambda t: y_copy(total + t, 0).start())
        _for_groups(tail_ref[1], lambda t: y_copy(total + t, 0).wait())


def _expert_call(xs, row0, nblk, nrows, tail, w13, w2, l):
    r, half = xs.shape
    d = 2 * half
    ne = w13.shape[1]
    hid = w2.shape[2]
    nx, ny = 6, 4
    grid_spec = pltpu.PrefetchScalarGridSpec(
        num_scalar_prefetch=4,
        grid=(ne,),
        in_specs=[
            pl.BlockSpec(memory_space=pl.ANY),
            pl.BlockSpec((None, None, d, 2 * hid), lambda e, *_: (l, e, 0, 0)),
            pl.BlockSpec((None, None, hid, d), lambda e, *_: (l, e, 0, 0)),
        ],
        out_specs=pl.BlockSpec(memory_space=pl.ANY),
        scratch_shapes=[
            pltpu.VMEM((nx, MOE_BM, half), U32), pltpu.VMEM((ny, MOE_BM, half), U32),
            pltpu.VMEM((d, 2 * hid), BF16), pltpu.VMEM((hid, d), BF16),
            pltpu.SemaphoreType.DMA((nx,)), pltpu.SemaphoreType.DMA((ny,)),
        ],
    )
    return pl.pallas_call(
        functools.partial(_expert_kernel, hid=hid),
        out_shape=jax.ShapeDtypeStruct((r, half), U32),
        grid_spec=grid_spec,
        compiler_params=_cparams(("arbitrary",)),
        name="moe_experts",
    )(row0, nblk, nrows, tail, xs, w13, w2)


def _combine_kernel(gdst_ref, ngt_ref, x_ref, rt_ref, gt_ref, nw_ref, sh_ref, sc_ref, ys_ref, *rest,
                    lay, tm, nl, final, gated):
    if gated:
        wh_ref, wl_ref, out_a, out_b, out_g, loc, sem = rest
    else:
        out_a, out_b, loc, sem = rest
    i = pl.program_id(0)
    nt = pl.num_programs(0)
    slot = i % 2
    nlg = nl // SUBLANES

    def copy(step, s, g):
        return _group_copy(ys_ref, gdst_ref[step * nlg + g], loc.at[s], g, sem.at[s])

    @pl.when(i == 0)
    def _():
        loc[...] = jnp.zeros(loc.shape, loc.dtype)
        _for_groups(ngt_ref[0], lambda g: copy(0, 0, g).start())

    _for_groups(ngt_ref[i], lambda g: _group_copy(ys_ref, 0, loc.at[slot], 0, sem.at[slot]).wait())

    @pl.when(i + 1 < nt)
    def _():
        _for_groups(ngt_ref[i + 1], lambda g: copy(i + 1, 1 - slot, g).start())

    rt = rt_ref[...]
    gate1, gate2 = rt[:, 2:3], rt[:, 3:4]
    pos1, pos2 = rt[:, 4:5].astype(jnp.int32), rt[:, 5:6].astype(jnp.int32)
    half = loc.shape[2]
    rc = 256
    mix_lo = jnp.zeros((tm, half), F32)
    mix_hi = jnp.zeros((tm, half), F32)
    for r in range(0, nl, rc):
        p = lax.broadcasted_iota(jnp.int32, (tm, rc), 1) + r
        wgt = (jnp.where(p == pos1, gate1, 0.0) + jnp.where(p == pos2, gate2, 0.0)).astype(BF16)
        y_lo, y_hi = _unpack_halves(loc[slot, r:r + rc, :])
        mix_lo = mix_lo + _dot(wgt, y_lo)
        mix_hi = mix_hi + _dot(wgt, y_hi)
    grp = lay.group(i * tm)
    gate = gt_ref[pl.ds(grp, 1), :]
    nw = nw_ref[...]
    sh = sh_ref[pl.ds(grp, 1), :]
    sc = sc_ref[pl.ds(grp, 1), :]
    er = tm if final else 128
    for r in range(0, tm, er):
        rs = slice(r, r + er)
        x_lo = x_ref[rs, :half] + gate[:, :half] * mix_lo[rs]
        x_hi = x_ref[rs, half:] + gate[:, half:] * mix_hi[rs]
        ms = (jnp.sum(x_lo * x_lo, axis=-1, keepdims=True)
              + jnp.sum(x_hi * x_hi, axis=-1, keepdims=True)) / (2 * half)
        inv = lax.rsqrt(ms + EPS)
        if final:
            y_lo = x_lo * inv * nw[:, :half]
            y_hi = x_hi * inv * nw[:, half:]
            for ref, cond in ((out_a, i * tm < lay.tp), (out_b, i * tm >= lay.tp)):
                @pl.when(cond)
                def _(ref=ref, y_lo=y_lo, y_hi=y_hi):
                    ref[rs, :half] = y_lo
                    ref[rs, half:] = y_hi
            continue
        out_a[rs, :half] = x_lo
        out_a[rs, half:] = x_hi
        u_lo = x_lo * inv * nw[:, :half] * (1.0 + sc[:, :half]) + sh[:, :half]
        u_hi = x_hi * inv * nw[:, half:] * (1.0 + sc[:, half:]) + sh[:, half:]
        if gated:
            (a_hi, a_lo), (b_hi, b_lo) = _split2(u_lo), _split2(u_hi)
            wh, wl = wh_ref[...], wl_ref[...]
            out_g[rs, :] = (_dot_nt(a_hi, wh[:, :half]) + _dot_nt(b_hi, wh[:, half:])
                            + _dot_nt(a_lo, wh[:, :half]) + _dot_nt(b_lo, wh[:, half:])
                            + _dot_nt(a_hi, wl[:, :half]) + _dot_nt(b_hi, wl[:, half:]))
            out_b[rs, :half] = a_hi
            out_b[rs, half:] = b_hi
        else:
            out_b[rs, :half] = u_lo.astype(BF16)
            out_b[rs, half:] = u_hi.astype(BF16)


def _combine_call(lay, x, route, mods, l, mod_next, gdst, ngt, ys, tm, nl, final, gate_w=None):
    t, d = x.shape
    rows = pl.BlockSpec((tm, d), lambda i, *_: (i, 0))
    gated = gate_w is not None
    extra_in, extra_args = [], ()
    if final:
        ntp = lay.tp // tm
        out_specs = (pl.BlockSpec((tm, d), lambda i, *_: (jnp.minimum(i, ntp - 1), 0)),
                     pl.BlockSpec((tm, d), lambda i, *_: (jnp.maximum(i - ntp, 0), 0)))
        out_shape = (jax.ShapeDtypeStruct((lay.tp, d), F32), jax.ShapeDtypeStruct((lay.ts, d), F32))
    else:
        out_specs = (rows, rows)
        out_shape = (jax.ShapeDtypeStruct(x.shape, F32), jax.ShapeDtypeStruct(x.shape, BF16))
        if gated:
            wspec = pl.BlockSpec((LANES, d), lambda i, *_: (0, 0))
            extra_in, extra_args = [wspec, wspec], tuple(gate_w)
            out_specs += (pl.BlockSpec((tm, LANES), lambda i, *_: (i, 0)),)
            out_shape += (jax.ShapeDtypeStruct((t, LANES), F32),)
    grid_spec = pltpu.PrefetchScalarGridSpec(
        num_scalar_prefetch=2,
        grid=(t // tm,),
        in_specs=[
            rows,
            pl.BlockSpec((tm, SUBLANES), lambda i, *_: (i, 0)),
            lay.mod_spec(l, 5),
        ] + mod_next.specs + [pl.BlockSpec(memory_space=pl.ANY)] + extra_in,
        out_specs=out_specs,
        scratch_shapes=[pltpu.VMEM((2, nl, d // 2), U32), pltpu.SemaphoreType.DMA((2,))],
    )
    return pl.pallas_call(
        functools.partial(_combine_kernel, lay=lay, tm=tm, nl=nl, final=final, gated=gated),
        out_shape=out_shape,
        grid_spec=grid_spec,
        compiler_params=_cparams(("arbitrary",)),
        name="moe_combine_final" if final else "moe_combine",
    )(gdst, ngt, x, route, mods, *mod_next.args, ys, *extra_args)


def _moe_layer(lay, x, u2, mods, l, mod_next, final, wr_t, upper, ltri, w13, w2, ng, ne, gate_w=None):
    t, d = x.shape
    tm = upper.shape[0]
    nt = t // tm
    bmg = MOE_BM // SUBLANES
    nl = -(-(MOE_TOP_K * tm + ne * (SUBLANES - 1)) // 256) * 256
    nlg = nl // SUBLANES
    route_t, counts = _router_call(u2, wr_t, upper, ltri, ng, ne)
    route = jnp.swapaxes(route_t, 1, 2).reshape(t, SUBLANES)

    c8 = counts[:, :, 0].astype(jnp.int32)
    lend = jnp.cumsum(c8, axis=1)
    lstart = lend - c8
    ngt = lend[:, -1].astype(jnp.int32)
    tot = jnp.sum(c8, axis=0)
    padded = (tot + bmg - 1) // bmg * bmg
    gend = jnp.cumsum(padded)
    gbase = (gend - padded)[None, :] + jnp.cumsum(c8, axis=0) - c8
    nb = -(-(MOE_TOP_K * t + nt * ne * (SUBLANES - 1)) // MOE_BM) + ne
    g = jnp.arange(nlg, dtype=jnp.int32)[None, :, None]
    owner = (g >= lstart[:, None, :]) & (g < lend[:, None, :])
    gdst = g[:, :, 0] + jnp.sum(jnp.where(owner, (gbase - lstart)[:, None, :], 0), axis=-1)
    gdst = gdst.reshape(nt * nlg).astype(jnp.int32)
    row0 = ((gend - padded) * SUBLANES).astype(jnp.int32)
    nblk = (padded // bmg).astype(jnp.int32)
    tail = jnp.stack([gend[-1] * SUBLANES, nb - gend[-1] // bmg]).astype(jnp.int32)

    pad0 = (gend - padded + tot).astype(jnp.int32)
    npad = (padded - tot).astype(jnp.int32)
    xs = _dispatch_call(u2, route_t, gdst, ngt, pad0, npad, tail, nb * MOE_BM, tm, nl)
    ys = _expert_call(xs, row0, nblk, (tot * SUBLANES).astype(jnp.int32), tail, w13, w2, l)
    return _combine_call(lay, x, route, mods, l, mod_next, gdst, ngt, ys, tm, nl, final, gate_w)


def _lower_tri(n, strict):
    r = np.arange(n)
    m = (r[None, :] < r[:, None]) if strict else (r[None, :] <= r[:, None])
    return jnp.asarray(m.astype(np.float32), dtype=BF16)


def kernel(x_prompt, x_sample, state_C, state_n, state_m, c, c_ctx, ada_w, ada_b, norm1_w, norm2_w, m_w_in, m_b_gate, m_head_norm_w, m_w_out, f_w_out, cv_w_pw1, cv_b_pw1, cv_w_dw, cv_b_dw, cv_ln_w, cv_ln_b, cv_w_pw2, cv_b_pw2, r_w_group, r_w_expert, e_w13, e_w2, final_norm_w):
    nbp, sp, d = x_prompt.shape
    nbs, ss, _ = x_sample.shape
    assert ss % GRID_W == 0
    lay = _Layout(nbp, sp, nbs, ss, d)
    depth = ada_w.shape[0]
    nh, dh = state_C.shape[3], state_C.shape[4]
    di = nh * dh
    ng = r_w_group.shape[2]
    ne = r_w_expert.shape[2]
    assert ng == MOE_GROUPS and ng + ne <= LANES and 4 * nh <= LANES and MOE_TOP_K == 2

    cv = jnp.zeros((lay.ngp, d), F32).at[0].set(c_ctx).at[1:1 + nbs].set(c)
    mods = _ada_call(cv, ada_w, ada_b)

    tri_l = _lower_tri(MLSTM_L, strict=False)
    upper = _lower_tri(lay.row_tile(512), strict=True).T
    ltri = _lower_tri(ne, strict=True)
    row = lambda a: a.reshape(1, -1)
    w_in_t = jnp.swapaxes(m_w_in, 1, 2)
    tn = 1024
    assert di % tn == 0
    nkb = di // tn

    def gate_weights(j):
        wg = jnp.zeros((LANES, d), F32).at[:4 * nh].set(w_in_t[j, 4 * di:])
        return _split2(wg)

    x, u1, graw = _prep_call(lay, x_prompt.reshape(lay.tp, d), x_sample.reshape(lay.ts, d),
                             _Mod(lay, mods, norm1_w[0], 0, 0), gate_weights(0))
    y = None
    qvos, kts, gpts = [], [], []
    for l in range(depth):
        j, kind = l // N_MIXERS, l % N_MIXERS
        mod2 = _Mod(lay, mods, norm2_w[l], l, 3)
        if kind == 0:
            bg = jnp.zeros((1, LANES), F32).at[0, :4 * nh].set(m_b_gate[j])
            gp = _gates_call(lay, graw, bg, tri_l, nh)
            gpt = gp[:, :4 * nh].T
            qvo_blocks = list(range(nkb)) + list(range(2 * nkb, 4 * nkb))
            qvo = _proj_call(lay, u1, w_in_t, j, qvo_blocks, tn, transposed=False)
            kt = _proj_call(lay, u1, w_in_t, j, list(range(nkb, 2 * nkb)), tn, transposed=True)
            hw = row(m_head_norm_w[j])
            hg_p = _mlstm_call(lay, qvo, kt, gp, gpt, hw, nh, dh, prompt=True)
            n0 = state_n[:, j].reshape(nbs, 2 * nh, dh)
            m0 = jnp.broadcast_to(state_m[:, j].reshape(nbs, 2 * nh, 1), (nbs, 2 * nh, LANES))
            hg_s = _mlstm_call(lay, qvo, kt, gp, gpt, hw, nh, dh, prompt=False, state=(state_C, j, n0, m0))
            x, u2 = _mm_res_call(lay, hg_p, hg_s, m_w_out[j].astype(BF16), x, mods, l, mod2)
            qvos.append(qvo)
            kts.append(kt)
            gpts.append(gpt)
        elif kind == 1:
            wo = f_w_out[j].astype(BF16)
            u2 = jnp.zeros((lay.t, d), BF16)
            x, u2 = _fnet_call(lay, x, u1, u2, mods, l, wo, mod2, prompt=True)
            x, u2 = _fnet_call(lay, x, u1, u2, mods, l, wo, mod2, prompt=False)
        else:
            glu = _glu_call(lay, u1, cv_w_pw1[j].astype(BF16), row(cv_b_pw1[j]))
            x, u2 = _conv_call(lay, glu, cv_w_dw[j], cv_b_dw[j], cv_ln_w[j], cv_ln_b[j], cv_w_pw2[j].astype(BF16),
                               cv_b_pw2[j], x, mods, l, mod2)
        wr = jnp.zeros((LANES, d), F32).at[:ng].set(r_w_group[l].T).at[SUBLANES:SUBLANES + ne].set(r_w_expert[l].T)
        final = l + 1 == depth
        mod_next = _Mod(lay, mods, final_norm_w, l, 0) if final else _Mod(lay, mods, norm1_w[l + 1], l + 1, 0)
        next_mlstm = not final and (l + 1) % N_MIXERS == 0
        outs = _moe_layer(lay, x, u2, mods, l, mod_next, final, wr.astype(BF16), upper, ltri, e_w13, e_w2, ng, ne,
                          gate_weights((l + 1) // N_MIXERS) if next_mlstm else None)
        if final:
            y = outs
        elif next_mlstm:
            x, u1, graw = outs
        else:
            x, u1 = outs

    y_prompt = y[0].reshape(nbp, sp, d)
    y_sample = y[1].reshape(nbs, ss, d)
    new_c, new_n, new_m = _state_call(lay, qvos, kts, gpts, nh, dh)
    return (y_prompt, y_sample, new_c, new_n, new_m)
```

```python
import functools
import math

import numpy as np
import jax
import jax.numpy as jnp
from jax import lax
from jax.experimental import pallas as pl
from jax.experimental.pallas import tpu as pltpu

F32 = jnp.float32
BF16 = jnp.bfloat16
U32 = jnp.uint32
EPS = 1e-6
GRID_W = 64
N_MIXERS = 3
FNET_GROUPS = 4
CONV_WIDTH = 31
MOE_GROUPS = 4
MOE_TOP_K = 2

LANES = 128
SUBLANES = 8
MLSTM_L = 256
MOE_BM = 256
CONV_HALO = 16
VMEM_LIMIT = 56 * 1024 * 1024


def _cparams(sem, vmem=VMEM_LIMIT):
    return pltpu.CompilerParams(dimension_semantics=sem, vmem_limit_bytes=vmem)


def _dot(a, b):
    return jnp.dot(a, b, preferred_element_type=F32)


def _dot_nt(a, b):
    return lax.dot_general(a, b, (((1,), (1,)), ((), ())), preferred_element_type=F32)


def _rms(x, w):
    return x * lax.rsqrt(jnp.mean(x * x, axis=-1, keepdims=True) + EPS) * w


def _modulate(x, w, shift, scale):
    return _rms(x, w) * (1.0 + scale) + shift


def _sigmoid(x):
    return 1.0 / (1.0 + jnp.exp(-x))


def _log_sigmoid(x):
    return jnp.minimum(x, 0.0) - jnp.log(1.0 + jnp.exp(-jnp.abs(x)))


def _split2(x):
    hi = x.astype(BF16)
    return hi, (x - hi.astype(F32)).astype(BF16)


def _split3(x):
    hi = x.astype(BF16)
    r1 = x - hi.astype(F32)
    mid = r1.astype(BF16)
    return hi, mid, (r1 - mid.astype(F32)).astype(BF16)


class _Layout:
    def __init__(self, nbp, sp, nbs, ss, d):
        self.nbp, self.sp, self.nbs, self.ss, self.d = nbp, sp, nbs, ss, d
        self.tp, self.ts = nbp * sp, nbs * ss
        self.t = self.tp + self.ts
        assert self.tp % ss == 0, "latent sequences must start on a block boundary of their own length"
        self.ngp = -(-(1 + nbs) // SUBLANES) * SUBLANES

    def group(self, row0):
        return jnp.where(row0 < self.tp, 0, 1 + (row0 - self.tp) // self.ss)

    def row_tile(self, want):
        tm = math.gcd(math.gcd(self.tp, self.ss), want)
        assert tm % SUBLANES == 0
        return tm

    def mod_spec(self, l, chunk):
        return pl.BlockSpec((None, self.ngp, self.d), lambda *_: (l, 0, chunk))

    def row_spec(self):
        return pl.BlockSpec((1, self.d), lambda *_: (0, 0))


class _Mod:
    def __init__(self, lay, mods, nw, l, c_shift):
        self.args = (nw.reshape(1, -1), mods, mods)
        self.specs = [lay.row_spec(), lay.mod_spec(l, c_shift), lay.mod_spec(l, c_shift + 1)]


def _mod_value(x, nw_ref, sh_ref, sc_ref, grp):
    return _modulate(x, nw_ref[...], sh_ref[pl.ds(grp, 1), :], sc_ref[pl.ds(grp, 1), :])


def _ada_kernel(cv_ref, w_ref, b_ref, o_ref):
    s = cv_ref[...]
    s = s * _sigmoid(s)
    o_ref[...] = _dot(s.astype(BF16), w_ref[...].astype(BF16)) + b_ref[...]


def _ada_call(cv, ada_w, ada_b):
    depth, d, n = ada_w.shape
    ngp = cv.shape[0]
    tn = min(n, 2048)
    return pl.pallas_call(
        _ada_kernel,
        out_shape=jax.ShapeDtypeStruct((depth, ngp, n), F32),
        grid=(depth, n // tn),
        in_specs=[
            pl.BlockSpec((ngp, d), lambda l, j: (0, 0)),
            pl.BlockSpec((None, d, tn), lambda l, j: (l, 0, j)),
            pl.BlockSpec((None, 1, tn), lambda l, j: (l, 0, j)),
        ],
        out_specs=pl.BlockSpec((None, ngp, tn), lambda l, j: (l, 0, j)),
        compiler_params=_cparams(("parallel", "parallel")),
        name="ada_mods",
    )(cv, ada_w, ada_b.reshape(depth, 1, n))


def _gate_raw(u, wh, wl):
    u_hi, u_lo = _split2(u)
    return _dot_nt(u_hi, wh) + _dot_nt(u_lo, wh) + _dot_nt(u_hi, wl), u_hi


def _prep_kernel(xp_ref, xs_ref, nw_ref, sh_ref, sc_ref, wh_ref, wl_ref, x_ref, u_ref, g_ref, *, lay, tm):
    i = pl.program_id(0)
    grp = lay.group(i * tm)
    for src, cond in ((xp_ref, i * tm < lay.tp), (xs_ref, i * tm >= lay.tp)):
        @pl.when(cond)
        def _(src=src):
            rc = min(tm, 256)
            for r in range(0, tm, rc):
                x = src[r:r + rc, :]
                x_ref[r:r + rc, :] = x
                g, u_hi = _gate_raw(_mod_value(x, nw_ref, sh_ref, sc_ref, grp), wh_ref[...], wl_ref[...])
                u_ref[r:r + rc, :] = u_hi
                g_ref[r:r + rc, :] = g


def _prep_call(lay, xp, xs, mod, gate_w):
    d = lay.d
    tm = lay.row_tile(1024)
    ntp = lay.tp // tm
    rows = pl.BlockSpec((tm, d), lambda i: (i, 0))
    wspec = pl.BlockSpec((LANES, d), lambda i: (0, 0))
    return pl.pallas_call(
        functools.partial(_prep_kernel, lay=lay, tm=tm),
        out_shape=(jax.ShapeDtypeStruct((lay.t, d), F32), jax.ShapeDtypeStruct((lay.t, d), BF16),
                   jax.ShapeDtypeStruct((lay.t, LANES), F32)),
        grid=(lay.t // tm,),
        in_specs=[
            pl.BlockSpec((tm, d), lambda i: (jnp.minimum(i, ntp - 1), 0)),
            pl.BlockSpec((tm, d), lambda i: (jnp.maximum(i - ntp, 0), 0)),
        ] + mod.specs + [wspec, wspec],
        out_specs=(rows, rows, pl.BlockSpec((tm, LANES), lambda i: (i, 0))),
        compiler_params=_cparams(("arbitrary",)),
        name="prep_modulate",
    )(xp, xs, *mod.args, *gate_w)


def _gates_kernel(g_ref, b_ref, tri_ref, o_ref, *, nh, tm):
    l = tri_ref.shape[0]
    tri = tri_ref[...]
    lane = lax.broadcasted_iota(jnp.int32, (l, LANES), 1)
    is_f = ((lane >= nh) & (lane < 2 * nh)) | ((lane >= 3 * nh) & (lane < 4 * nh))
    is_a = (lane < nh) | ((lane >= 2 * nh) & (lane < 3 * nh))
    for r in range(0, tm, l):
        g = g_ref[r:r + l, :] + b_ref[...]
        lf = jnp.where(is_f, _log_sigmoid(g), 0.0)
        hi, mid, lo = _split3(lf)
        prefix = _dot(tri, hi) + _dot(tri, mid) + _dot(tri, lo)
        suffix = jnp.sum(lf, axis=0, keepdims=True) - prefix + lf
        b = jnp.where(lane < 2 * nh, prefix, suffix)
        a = g - pltpu.roll(b, LANES - nh, 1)
        o_ref[r:r + l, :] = jnp.where(is_a, a, b)


def _gates_call(lay, graw, bias, tri, nh):
    t = graw.shape[0]
    l = tri.shape[0]
    tm = lay.row_tile(2048)
    assert tm % l == 0
    return pl.pallas_call(
        functools.partial(_gates_kernel, nh=nh, tm=tm),
        out_shape=jax.ShapeDtypeStruct((t, LANES), F32),
        grid=(t // tm,),
        in_specs=[
            pl.BlockSpec((tm, LANES), lambda i: (i, 0)),
            pl.BlockSpec((1, LANES), lambda i: (0, 0)),
            pl.BlockSpec((l, l), lambda i: (0, 0)),
        ],
        out_specs=pl.BlockSpec((tm, LANES), lambda i: (i, 0)),
        compiler_params=_cparams(("parallel",)),
        name="mlstm_gates",
    )(graw, bias, tri)


def _proj_kernel(u_ref, w_ref, o_ref, w_bf, *, transposed):
    @pl.when(pl.program_id(1) == 0)
    def _():
        w_bf[...] = w_ref[...].astype(BF16)

    if transposed:
        o_ref[...] = _dot_nt(w_bf[...], u_ref[...]).astype(BF16)
    else:
        o_ref[...] = _dot_nt(u_ref[...], w_bf[...]).astype(BF16)


def _proj_call(lay, u, w_in_t, jl, blocks, tn, transposed):
    t, d = u.shape
    tm = 2048 if t % 2048 == 0 else lay.row_tile(1024)
    nb = len(blocks)
    first, gap_at, gap = blocks[0], None, 0
    for idx in range(1, nb):
        if blocks[idx] != blocks[idx - 1] + 1:
            assert gap_at is None
            gap_at, gap = idx, blocks[idx] - blocks[idx - 1] - 1
    wblk = (lambda j: first + j) if gap_at is None else (lambda j: first + j + jnp.where(j >= gap_at, gap, 0))
    if transposed:
        out_shape = jax.ShapeDtypeStruct((nb * tn, t), BF16)
        out_spec = pl.BlockSpec((tn, tm), lambda j, i: (j, i))
    else:
        out_shape = jax.ShapeDtypeStruct((t, nb * tn), BF16)
        out_spec = pl.BlockSpec((tm, tn), lambda j, i: (i, j))
    return pl.pallas_call(
        functools.partial(_proj_kernel, transposed=transposed),
        out_shape=out_shape,
        grid=(nb, t // tm),
        in_specs=[
            pl.BlockSpec((tm, d), lambda j, i: (i, 0)),
            pl.BlockSpec((None, tn, d), lambda j, i: (jl, wblk(j), 0)),
        ],
        out_specs=out_spec,
        scratch_shapes=[pltpu.VMEM((tn, d), BF16)],
        compiler_params=_cparams(("parallel", "arbitrary")),
        name="mlstm_proj_t" if transposed else "mlstm_proj",
    )(u, w_in_t)


def _col(tile, c):
    lane = lax.broadcasted_iota(jnp.int32, tile.shape, 1)
    return jnp.sum(jnp.where(lane == c, tile, 0.0), axis=-1, keepdims=True)


def _dir_masks(l):
    r = lax.broadcasted_iota(jnp.int32, (l, l), 0)
    c = lax.broadcasted_iota(jnp.int32, (l, l), 1)
    return c <= r, c >= r


def _head_epilogue(h, hw, o):
    hn = h * lax.rsqrt(jnp.mean(h * h, axis=-1, keepdims=True) + EPS) * hw
    return (hn * _sigmoid(o.astype(F32))).astype(BF16)


def _row_times_kt(w_row, kt):
    hi, lo = _split2(w_row)
    sub = lax.broadcasted_iota(jnp.int32, (SUBLANES, w_row.shape[1]), 0)
    stacked = jnp.where(sub == 0, hi.astype(F32), jnp.where(sub == 1, lo.astype(F32), 0.0)).astype(BF16)
    res = _dot_nt(stacked, kt)
    return res[0:1, :] + res[1:2, :]


def _mlstm_single_kernel(q_ref, kt_ref, v_ref, o_ref, gp_ref, gpt_ref, hw_ref, out_ref, *, nh, dh, scale):
    l = q_ref.shape[0]
    gp = gp_ref[...]
    masks = _dir_masks(l)
    for h in range(nh):
        cols = slice(h * dh, (h + 1) * dh)
        qk = _dot(q_ref[:, cols], kt_ref[cols, :])
        p = None
        for d in range(2):
            a_r = gpt_ref[2 * nh * d + h:2 * nh * d + h + 1, :]
            b_c = gp[:, 2 * nh * d + nh + h:2 * nh * d + nh + h + 1]
            g = jnp.where(masks[d], a_r, -jnp.inf)
            m = jnp.maximum(jnp.max(g, axis=-1, keepdims=True), 0.0)
            s = qk * jnp.exp(g - m)
            den = scale * jnp.sum(s, axis=-1, keepdims=True)
            inv = scale / jnp.maximum(jnp.abs(den), jnp.exp(-(b_c + m)))
            p = s * inv if p is None else p + s * inv
        hh = _dot(p.astype(BF16), v_ref[:, cols])
        out_ref[:, cols] = _head_epilogue(hh, hw_ref[:, cols], o_ref[:, cols])


def _mlstm_multi_kernel(q_ref, kt_ref, v_ref, o_ref, gp_ref, gpt_ref, hw_ref, c0_ref, n0_ref, m0_ref,
                        out_ref, cst, cbf, *, nh, nc, l, scale):
    h = pl.program_id(1)
    masks = _dir_masks(l)
    m_in = [[None] * nc for _ in range(2)]
    n_in = [[None] * nc for _ in range(2)]
    for d in range(2):
        cst[...] = c0_ref[d]
        n = n0_ref[pl.ds(d * nh + h, 1), :]
        m = m0_ref[pl.ds(d * nh + h, 1), 0:1]
        order = list(range(nc)) if d == 0 else list(range(nc - 1, -1, -1))
        for step, c in enumerate(order):
            m_in[d][c], n_in[d][c] = m, n
            cbf[d, c] = cst[...].astype(BF16)
            if step + 1 < nc:
                r0 = c * l
                a_r = gpt_ref[pl.ds(2 * nh * d + h, 1), r0:r0 + l]
                b_r = gpt_ref[pl.ds(2 * nh * d + nh + h, 1), r0:r0 + l]
                m_last = jnp.maximum(jnp.max(a_r, axis=-1, keepdims=True), m)
                b_end = b_r[:, l - 1:l] if d == 0 else b_r[:, 0:1]
                decay = jnp.exp(m - m_last)
                w_end = jnp.exp(a_r - m_last)
                kt = kt_ref[:, r0:r0 + l]
                cst[...] = decay * cst[...] + _dot((kt.astype(F32) * w_end).astype(BF16), v_ref[r0:r0 + l, :])
                n = decay * n + _row_times_kt(w_end, kt)
                m = b_end + m_last
    hw = hw_ref[...]
    for c in range(nc):
        r0 = c * l
        q = q_ref[r0:r0 + l, :]
        v = v_ref[r0:r0 + l, :]
        qk = _dot(q, kt_ref[:, r0:r0 + l])
        qf = q.astype(F32)
        gp = gp_ref[r0:r0 + l, :]
        p = inter = None
        for d in range(2):
            m, n = m_in[d][c], n_in[d][c]
            a_r = gpt_ref[pl.ds(2 * nh * d + h, 1), r0:r0 + l]
            b_c = _col(gp, 2 * nh * d + nh + h)
            g = jnp.where(masks[d], a_r, -jnp.inf)
            mt = jnp.maximum(jnp.max(g, axis=-1, keepdims=True), m)
            s = qk * jnp.exp(g - mt)
            w_prev = jnp.exp(m - mt)
            den = scale * (jnp.sum(s, axis=-1, keepdims=True) + w_prev * jnp.sum(qf * n, axis=-1, keepdims=True))
            inv = scale / jnp.maximum(jnp.abs(den), jnp.exp(-(b_c + mt)))
            term = (w_prev * inv) * _dot(q, cbf[d, c])
            p = s * inv if p is None else p + s * inv
            inter = term if inter is None else inter + term
        hh = _dot(p.astype(BF16), v) + inter
        out_ref[r0:r0 + l, :] = _head_epilogue(hh, hw, o_ref[r0:r0 + l, :])


def _mlstm_call(lay, qvo, kt, gp, gpt, hw, nh, dh, prompt, state=None):
    nb, s = (lay.nbp, lay.sp) if prompt else (lay.nbs, lay.ss)
    rb0 = 0 if prompt else lay.tp // s
    scale = dh ** -0.5
    di = nh * dh
    common_in = [
        pl.BlockSpec((s, dh), lambda b, h: (rb0 + b, h)),
        pl.BlockSpec((dh, s), lambda b, h: (h, rb0 + b)),
        pl.BlockSpec((s, dh), lambda b, h: (rb0 + b, nh + h)),
        pl.BlockSpec((s, dh), lambda b, h: (rb0 + b, 2 * nh + h)),
        pl.BlockSpec((s, LANES), lambda b, h: (rb0 + b, 0)),
        pl.BlockSpec((4 * nh, s), lambda b, h: (0, rb0 + b)),
        pl.BlockSpec((1, dh), lambda b, h: (0, h)),
    ]
    if prompt:
        assert s == MLSTM_L
        return pl.pallas_call(
            functools.partial(_mlstm_single_kernel, nh=nh, dh=dh, scale=scale),
            out_shape=jax.ShapeDtypeStruct((nb * s, di), BF16),
            grid=(nb,),
            in_specs=[
                pl.BlockSpec((s, di), lambda b: (rb0 + b, 0)),
                pl.BlockSpec((di, s), lambda b: (0, rb0 + b)),
                pl.BlockSpec((s, di), lambda b: (rb0 + b, 1)),
                pl.BlockSpec((s, di), lambda b: (rb0 + b, 2)),
                pl.BlockSpec((s, LANES), lambda b: (rb0 + b, 0)),
                pl.BlockSpec((4 * nh, s), lambda b: (0, rb0 + b)),
                pl.BlockSpec((1, di), lambda b: (0, 0)),
            ],
            out_specs=pl.BlockSpec((s, di), lambda b: (b, 0)),
            compiler_params=_cparams(("parallel",)),
            name="mlstm_prompt",
        )(qvo, kt, qvo, qvo, gp, gpt, hw)
    state_c, jl, n0, m0 = state
    nc = s // MLSTM_L
    return pl.pallas_call(
        functools.partial(_mlstm_multi_kernel, nh=nh, nc=nc, l=MLSTM_L, scale=scale),
        out_shape=jax.ShapeDtypeStruct((nb * s, di), BF16),
        grid=(nb, nh),
        in_specs=common_in + [
            pl.BlockSpec((None, None, 2, None, dh, dh), lambda b, h: (b, jl, 0, h, 0, 0)),
            pl.BlockSpec((None, 2 * nh, dh), lambda b, h: (b, 0, 0)),
            pl.BlockSpec((None, 2 * nh, LANES), lambda b, h: (b, 0, 0)),
        ],
        out_specs=pl.BlockSpec((s, dh), lambda b, h: (b, h)),
        scratch_shapes=[pltpu.VMEM((dh, dh), F32), pltpu.VMEM((2, nc, dh, dh), BF16)],
        compiler_params=_cparams(("parallel", "parallel")),
        name="mlstm_latent",
    )(qvo, kt, qvo, qvo, gp, gpt, hw, state_c, n0, m0)


def _state_kernel(*refs, nl, nh, dh):
    ins, (c_ref, n_ref, m_ref) = refs[:3 * nl], refs[3 * nl:]
    lyr = pl.program_id(0)
    for jl in range(nl):
        kt_ref, v_ref, gpt_ref = ins[3 * jl:3 * jl + 3]

        @pl.when(lyr == jl)
        def _(kt_ref=kt_ref, v_ref=v_ref, gpt_ref=gpt_ref):
            l = v_ref.shape[0]
            sub = lax.broadcasted_iota(jnp.int32, m_ref.shape, 0)
            lane = lax.broadcasted_iota(jnp.int32, m_ref.shape, 1)
            m_all = jnp.zeros(m_ref.shape, F32)
            for h in range(nh):
                kt = kt_ref[h * dh:(h + 1) * dh, :]
                ktf = kt.astype(F32)
                v = v_ref[:, h * dh:(h + 1) * dh]
                for d in range(2):
                    a_r = gpt_ref[2 * nh * d + h:2 * nh * d + h + 1, :]
                    b_r = gpt_ref[2 * nh * d + nh + h:2 * nh * d + nh + h + 1, :]
                    m_last = jnp.maximum(jnp.max(a_r, axis=-1, keepdims=True), 0.0)
                    b_end = b_r[:, l - 1:l] if d == 0 else b_r[:, 0:1]
                    w_end = jnp.exp(a_r - m_last)
                    c_ref[d, h] = _dot((ktf * w_end).astype(BF16), v)
                    n_ref[d, h:h + 1, :] = _row_times_kt(w_end, kt)
                    m_all = jnp.where((sub == d) & (lane == h), b_end + m_last, m_all)
            m_ref[...] = m_all


def _state_call(lay, qvos, kts, gpts, nh, dh):
    nl = len(qvos)
    nbp, s = lay.nbp, lay.sp
    di = nh * dh
    assert s == MLSTM_L

    def pick(jl):
        return lambda lyr, b: jnp.where(lyr == jl, b, jnp.where(lyr < jl, 0, nbp - 1))

    in_specs, args = [], []
    for jl in range(nl):
        pb = pick(jl)
        in_specs.append(pl.BlockSpec((di, s), lambda lyr, b, pb=pb: (0, pb(lyr, b))))
        in_specs.append(pl.BlockSpec((s, di), lambda lyr, b, pb=pb: (pb(lyr, b), 1)))
        in_specs.append(pl.BlockSpec((4 * nh, s), lambda lyr, b, pb=pb: (0, pb(lyr, b))))
        args += [kts[jl], qvos[jl], gpts[jl]]
    return pl.pallas_call(
        functools.partial(_state_kernel, nl=nl, nh=nh, dh=dh),
        out_shape=(
            jax.ShapeDtypeStruct((nbp, nl, 2, nh, dh, dh), F32),
            jax.ShapeDtypeStruct((nbp, nl, 2, nh, dh), F32),
            jax.ShapeDtypeStruct((nbp, nl, 2, nh), F32),
        ),
        grid=(nl, nbp),
        in_specs=in_specs,
        out_specs=(
            pl.BlockSpec((None, None, 2, nh, dh, dh), lambda lyr, b: (b, lyr, 0, 0, 0, 0)),
            pl.BlockSpec((None, None, 2, nh, dh), lambda lyr, b: (b, lyr, 0, 0, 0)),
            pl.BlockSpec((None, None, 2, nh), lambda lyr, b: (b, lyr, 0, 0)),
        ),
        compiler_params=_cparams(("arbitrary", "arbitrary")),
        name="mlstm_prompt_state",
    )(*args)


def _mm_res_kernel(ap_ref, as_ref, w_ref, x_ref, g_ref, nw_ref, sh_ref, sc_ref, o_ref, u_ref, *, lay, tm):
    i = pl.program_id(0)
    grp = lay.group(i * tm)
    gate = g_ref[pl.ds(grp, 1), :]
    w = w_ref[...]
    rc = min(tm, 256)
    for a_ref, cond in ((ap_ref, i * tm < lay.tp), (as_ref, i * tm >= lay.tp)):
        @pl.when(cond)
        def _(a_ref=a_ref):
            for r in range(0, tm, rc):
                xn = x_ref[r:r + rc, :] + gate * _dot(a_ref[r:r + rc, :], w)
                o_ref[r:r + rc, :] = xn
                u_ref[r:r + rc, :] = _mod_value(xn, nw_ref, sh_ref, sc_ref, grp).astype(BF16)


def _mm_res_call(lay, a_p, a_s, w_bf, x, mods, l, mod2):
    kdim = a_p.shape[1]
    t, d = x.shape
    tm = lay.row_tile(1024)
    ntp = lay.tp // tm
    return pl.pallas_call(
        functools.partial(_mm_res_kernel, lay=lay, tm=tm),
        out_shape=(jax.ShapeDtypeStruct(x.shape, F32), jax.ShapeDtypeStruct(x.shape, BF16)),
        grid=(t // tm,),
        in_specs=[
            pl.BlockSpec((tm, kdim), lambda i: (jnp.minimum(i, ntp - 1), 0)),
            pl.BlockSpec((tm, kdim), lambda i: (jnp.maximum(i - ntp, 0), 0)),
            pl.BlockSpec((kdim, d), lambda i: (0, 0)),
            pl.BlockSpec((tm, d), lambda i: (i, 0)),
            lay.mod_spec(l, 2),
        ] + mod2.specs,
        out_specs=(pl.BlockSpec((tm, d), lambda i: (i, 0)), pl.BlockSpec((tm, d), lambda i: (i, 0))),
        compiler_params=_cparams(("arbitrary",)),
        name="mm_residual",
    )(a_p, a_s, w_bf, x, mods, *mod2.args)


def _fnet_kernel(x_ref, u_ref, u2_any, gt_ref, wc_ref, ds_ref, wo_ref, nw_ref, sh_ref, sc_ref, o_ref, u2_ref,
                 ab_scr, *, lay, row_base, groups, norm):
    s, d = x_ref.shape
    cg = d // groups
    grp = lay.group(row_base + pl.program_id(0) * s)
    wc = wc_ref[...]
    rc = min(s, 256)
    for g in range(groups):
        for r in range(0, s, rc):
            ab = _dot(u_ref[r:r + rc, g * cg:(g + 1) * cg], wc)
            ab_scr[r:r + rc, g * cg:(g + 1) * cg] = ab[:, :cg].astype(BF16)
            ab_scr[s + r:s + r + rc, g * cg:(g + 1) * cg] = ab[:, cg:].astype(BF16)
    gate = gt_ref[pl.ds(grp, 1), :]
    wo = wo_ref[...]
    for r in range(0, s, rc):
        y = _dot(ds_ref[r:r + rc, :], ab_scr[...]) * norm
        xn = x_ref[r:r + rc, :] + gate * _dot(y.astype(BF16), wo)
        o_ref[r:r + rc, :] = xn
        u2_ref[r:r + rc, :] = _mod_value(xn, nw_ref, sh_ref, sc_ref, grp).astype(BF16)


def _dft_mats(s, cg):
    kc = np.arange(cg)
    ang_c = 2.0 * np.pi * np.outer(kc, kc) / cg
    wc = np.concatenate([np.cos(ang_c), np.sin(ang_c)], axis=1)
    ks = np.arange(s)
    ang_s = 2.0 * np.pi * np.outer(ks, ks) / s
    ds = np.concatenate([np.cos(ang_s), -np.sin(ang_s)], axis=1)
    return jnp.asarray(wc, dtype=BF16), jnp.asarray(ds, dtype=BF16)


def _fnet_call(lay, x, u1, u2, mods, l, wo_bf, mod2, prompt):
    nb, s = (lay.nbp, lay.sp) if prompt else (lay.nbs, lay.ss)
    rb0 = 0 if prompt else lay.tp // s
    d = lay.d
    cg = d // FNET_GROUPS
    wc, ds = _dft_mats(s, cg)
    kern = functools.partial(_fnet_kernel, lay=lay, row_base=rb0 * s, groups=FNET_GROUPS,
                             norm=1.0 / math.sqrt(s * cg))
    blk = pl.BlockSpec((s, d), lambda b: (rb0 + b, 0))
    return pl.pallas_call(
        kern,
        out_shape=(jax.ShapeDtypeStruct(x.shape, F32), jax.ShapeDtypeStruct(u2.shape, BF16)),
        grid=(nb,),
        in_specs=[
            blk,
            blk,
            pl.BlockSpec(memory_space=pl.ANY),
            lay.mod_spec(l, 2),
            pl.BlockSpec((cg, 2 * cg), lambda b: (0, 0)),
            pl.BlockSpec((s, 2 * s), lambda b: (0, 0)),
            pl.BlockSpec((d, d), lambda b: (0, 0)),
        ] + mod2.specs,
        out_specs=(blk, blk),
        scratch_shapes=[pltpu.VMEM((2 * s, d), BF16)],
        input_output_aliases={0: 0, 2: 1},
        compiler_params=_cparams(("parallel",)),
        name="fnet_prompt" if prompt else "fnet_latent",
    )(x, u1, u2, mods, wc, ds, wo_bf, *mod2.args)


def _glu_kernel(u_ref, wa_ref, wg_ref, ba_ref, bg_ref, o_ref):
    u = u_ref[...]
    a = _dot(u, wa_ref[...]) + ba_ref[...]
    g = _dot(u, wg_ref[...]) + bg_ref[...]
    o_ref[...] = a * _sigmoid(g)


def _glu_call(lay, u, w_bf, bias):
    t, d = u.shape
    cd = w_bf.shape[1] // 2
    tm = lay.row_tile(1024)
    tn = min(cd, 1024)
    nj = cd // tn
    return pl.pallas_call(
        _glu_kernel,
        out_shape=jax.ShapeDtypeStruct((t, cd), F32),
        grid=(t // tm, nj),
        in_specs=[
            pl.BlockSpec((tm, d), lambda i, j: (i, 0)),
            pl.BlockSpec((d, tn), lambda i, j: (0, j)),
            pl.BlockSpec((d, tn), lambda i, j: (0, nj + j)),
            pl.BlockSpec((1, tn), lambda i, j: (0, j)),
            pl.BlockSpec((1, tn), lambda i, j: (0, nj + j)),
        ],
        out_specs=pl.BlockSpec((tm, tn), lambda i, j: (i, j)),
        compiler_params=_cparams(("parallel", "parallel")),
        name="conv_glu",
    )(u, w_bf, w_bf, bias, bias)


def _conv_kernel(c_ref, p_ref, n_ref, wd_ref, bd_ref, lw_ref, lb_ref, w2_ref, b2_ref, x_ref, gt_ref,
                 nw_ref, sh_ref, sc_ref, o_ref, u2_ref, pad, conv, act, *, lay, rb, width):
    i = pl.program_id(0)
    row0 = i * rb
    grp = lay.group(row0)
    seq = jnp.where(row0 < lay.tp, lay.sp, lay.ss)
    pos = jnp.where(row0 < lay.tp, row0 % lay.sp, (row0 - lay.tp) % lay.ss)
    has_prev = (pos != 0).astype(F32)
    has_next = (pos + rb != seq).astype(F32)
    hl = CONV_HALO
    half = width // 2
    cd = c_ref.shape[1]
    span = pad.shape[1]
    pad[0, 0:hl, :] = p_ref[...] * has_prev
    pad[0, hl:hl + rb, :] = c_ref[...]
    pad[0, hl + rb:hl + rb + hl, :] = n_ref[...] * has_next
    for s in range(1, SUBLANES):
        pad[s, 0:span - SUBLANES, :] = pad[0, s:s + span - SUBLANES, :]
    ngrp = 8
    sub = ngrp * SUBLANES
    lanes = 2 * LANES
    assert rb % sub == 0 and cd % lanes == 0

    def conv_block(blk, carry):
        r0 = pl.multiple_of(blk * sub, sub)
        for c0 in range(0, cd, lanes):
            bias = bd_ref[:, c0:c0 + lanes]
            accs = [jnp.zeros((SUBLANES, lanes), F32) + bias for _ in range(ngrp)]
            for k in sorted(range(width), key=lambda k: ((hl - half + k) % SUBLANES, k)):
                q, s = divmod(hl - half + k, SUBLANES)
                wk = wd_ref[k, :, c0:c0 + lanes]
                for gi in range(ngrp):
                    win = pad[s, pl.ds(r0 + (q + gi) * SUBLANES, SUBLANES), c0:c0 + lanes]
                    accs[gi] = accs[gi] + win * wk
            conv[pl.ds(r0, sub), c0:c0 + lanes] = jnp.concatenate(accs, axis=0)
        return carry

    lax.fori_loop(0, rb // sub, conv_block, 0)
    lw = lw_ref[...]
    lb = lb_ref[...]
    lsub = min(rb, 16 * SUBLANES)

    def ln_block(blk, carry):
        r0 = pl.multiple_of(blk * lsub, lsub)
        acc = conv[pl.ds(r0, lsub), :]
        mu = jnp.mean(acc, axis=-1, keepdims=True)
        cen = acc - mu
        var = jnp.mean(cen * cen, axis=-1, keepdims=True)
        y = cen * lax.rsqrt(var + EPS) * lw + lb
        act[pl.ds(r0, lsub), :] = (y * _sigmoid(y)).astype(BF16)
        return carry

    lax.fori_loop(0, rb // lsub, ln_block, 0)
    xn = x_ref[...] + gt_ref[pl.ds(grp, 1), :] * (_dot(act[...], w2_ref[...]) + b2_ref[...])
    o_ref[...] = xn
    u2_ref[...] = _mod_value(xn, nw_ref, sh_ref, sc_ref, grp).astype(BF16)


def _conv_call(lay, glu, wd, bd, lw, lb, w2_bf, b2, x, mods, l, mod2):
    t, cd = glu.shape
    d = x.shape[1]
    rb = lay.row_tile(256)
    hl = CONV_HALO
    assert CONV_WIDTH // 2 <= hl and rb % hl == 0
    nhb = t // hl
    per = rb // hl
    wd_p = jnp.broadcast_to(wd[:, None, :], (CONV_WIDTH, SUBLANES, cd))
    row = lambda a: a.reshape(1, -1)
    rows = pl.BlockSpec((rb, d), lambda i: (i, 0))
    return pl.pallas_call(
        functools.partial(_conv_kernel, lay=lay, rb=rb, width=CONV_WIDTH),
        out_shape=(jax.ShapeDtypeStruct(x.shape, F32), jax.ShapeDtypeStruct(x.shape, BF16)),
        grid=(t // rb,),
        in_specs=[
            pl.BlockSpec((rb, cd), lambda i: (i, 0)),
            pl.BlockSpec((hl, cd), lambda i: (jnp.maximum(i * per - 1, 0), 0)),
            pl.BlockSpec((hl, cd), lambda i: (jnp.minimum((i + 1) * per, nhb - 1), 0)),
            pl.BlockSpec(wd_p.shape, lambda i: (0, 0, 0)),
            pl.BlockSpec((1, cd), lambda i: (0, 0)),
            pl.BlockSpec((1, cd), lambda i: (0, 0)),
            pl.BlockSpec((1, cd), lambda i: (0, 0)),
            pl.BlockSpec((cd, d), lambda i: (0, 0)),
            pl.BlockSpec((1, d), lambda i: (0, 0)),
            rows,
            lay.mod_spec(l, 2),
        ] + mod2.specs,
        out_specs=(rows, rows),
        scratch_shapes=[pltpu.VMEM((SUBLANES, rb + 2 * hl, cd), F32), pltpu.VMEM((rb, cd), F32),
                        pltpu.VMEM((rb, cd), BF16)],
        compiler_params=_cparams(("parallel",)),
        name="conv_dw_ln_pw2",
    )(glu, glu, glu, wd_p, row(bd), row(lw), row(lb), w2_bf, row(b2), x, mods, *mod2.args)


def _router_kernel(u_ref, wr_ref, upper_ref, ltri_ref, o_ref, cnt_ref, *, ng, ne):
    for j in range(o_ref.shape[0]):
        _route_tile(u_ref, wr_ref, upper_ref, ltri_ref, o_ref, cnt_ref, j, ng, ne)


def _route_tile(u_ref, wr_ref, upper_ref, ltri_ref, o_ref, cnt_ref, j, ng, ne):
    tm = upper_ref.shape[0]
    logits = _dot_nt(wr_ref[...], u_ref[j * tm:(j + 1) * tm, :])
    neg = -jnp.inf
    row = lax.broadcasted_iota(jnp.int32, (SUBLANES, tm), 0)

    gl = jnp.where(row < ng, logits[0:SUBLANES, :], neg)
    gmax = jnp.max(gl, axis=0, keepdims=True)
    gidx = jnp.min(jnp.where(gl == gmax, row, SUBLANES), axis=0, keepdims=True)
    g_p = 1.0 / jnp.sum(jnp.where(row < ng, jnp.exp(gl - gmax), 0.0), axis=0, keepdims=True)

    sel = logits[SUBLANES:2 * SUBLANES, :]
    for g in range(1, ng):
        sel = jnp.where(gidx == g, logits[(1 + g) * SUBLANES:(2 + g) * SUBLANES, :], sel)
    v1 = jnp.max(sel, axis=0, keepdims=True)
    i1 = jnp.min(jnp.where(sel == v1, row, SUBLANES), axis=0, keepdims=True)
    sel2 = jnp.where(row == i1, neg, sel)
    v2 = jnp.max(sel2, axis=0, keepdims=True)
    i2 = jnp.min(jnp.where(sel2 == v2, row, SUBLANES), axis=0, keepdims=True)
    e1 = gidx * SUBLANES + i1
    e2 = gidx * SUBLANES + i2
    tt = jnp.exp(v2 - v1)
    p1 = 1.0 / (1.0 + tt)
    gate1 = p1 * g_p
    gate2 = (tt * p1) * g_p

    rowe = lax.broadcasted_iota(jnp.int32, (ne, tm), 0)
    oh1 = rowe == e1
    oh2 = rowe == e2
    oh = jnp.where(oh1 | oh2, 1.0, 0.0)
    groups = jnp.floor((jnp.sum(oh, axis=1, keepdims=True) + (SUBLANES - 1)) * (1.0 / SUBLANES))
    groups_b = jnp.broadcast_to(groups, (ne, LANES))
    start = SUBLANES * _dot(ltri_ref[...], groups_b.astype(BF16))[:, 0:1]
    prefix = _dot(oh.astype(BF16), upper_ref[...]) + start
    pos1 = jnp.sum(jnp.where(oh1, prefix, 0.0), axis=0, keepdims=True)
    pos2 = jnp.sum(jnp.where(oh2, prefix, 0.0), axis=0, keepdims=True)
    cnt_ref[j] = groups_b

    out = jnp.where(row == 0, e1.astype(F32), 0.0)
    out = jnp.where(row == 1, e2.astype(F32), out)
    out = jnp.where(row == 2, gate1, out)
    out = jnp.where(row == 3, gate2, out)
    out = jnp.where(row == 4, pos1, out)
    out = jnp.where(row == 5, pos2, out)
    o_ref[j] = out


def _router_call(u, wr_t, upper, ltri, ng, ne):
    t, d = u.shape
    tm = upper.shape[0]
    nt = t // tm
    per = 2 if nt % 2 == 0 else 1
    assert ne // ng == SUBLANES and ng <= SUBLANES and SUBLANES + ne <= LANES
    return pl.pallas_call(
        functools.partial(_router_kernel, ng=ng, ne=ne),
        out_shape=(jax.ShapeDtypeStruct((nt, SUBLANES, tm), F32), jax.ShapeDtypeStruct((nt, ne, LANES), F32)),
        grid=(nt // per,),
        in_specs=[
            pl.BlockSpec((per * tm, d), lambda i: (i, 0)),
            pl.BlockSpec((LANES, d), lambda i: (0, 0)),
            pl.BlockSpec((tm, tm), lambda i: (0, 0)),
            pl.BlockSpec((ne, ne), lambda i: (0, 0)),
        ],
        out_specs=(pl.BlockSpec((per, SUBLANES, tm), lambda i: (i, 0, 0)),
                   pl.BlockSpec((per, ne, LANES), lambda i: (i, 0, 0))),
        compiler_params=_cparams(("parallel",)),
        name="moe_router",
    )(u, wr_t, upper, ltri)


def _pack_halves(lo, hi):
    lo_bits = lax.shift_right_logical(pltpu.bitcast(lo, U32), jnp.uint32(16))
    hi_bits = pltpu.bitcast(hi, U32) & jnp.uint32(0xFFFF0000)
    return hi_bits | lo_bits


def _unpack_halves(w):
    lo = pltpu.bitcast(lax.shift_left(w, jnp.uint32(16)), F32)
    hi = pltpu.bitcast(w & jnp.uint32(0xFFFF0000), F32)
    return lo.astype(BF16), hi.astype(BF16)


def _round_bf16(x):
    return x.astype(BF16).astype(F32)


def _group_copy(src, src_g, dst, dst_g, sem):
    g8 = lambda g: pl.ds(pl.multiple_of(g * SUBLANES, SUBLANES), SUBLANES)
    return pltpu.make_async_copy(src.at[g8(src_g), :], dst.at[g8(dst_g), :], sem)


def _for_groups(n, fn, unroll=4, alternate=False):
    def call(g, j):
        if alternate:
            fn(g, j % 2)
        else:
            fn(g)

    def body_many(i, c):
        for j in range(unroll):
            call(i * unroll + j, j)
        return c

    def body_one(g, c):
        call(g, 0)
        return c

    full = lax.div(n, jnp.int32(unroll))
    lax.fori_loop(0, full, body_many, 0)
    lax.fori_loop(full * unroll, n, body_one, 0)


def _dispatch_kernel(gdst_ref, ngt_ref, pad0_ref, npad_ref, tail_ref, u_ref, pos_ref, xs_out, loc, zeros, sem,
                     zsem, *, tm, nl, ne):
    i = pl.program_id(0)
    nt = pl.num_programs(0)
    slot = i % 2
    nlg = nl // SUBLANES

    def copy(step, s, g):
        return _group_copy(loc.at[s], g, xs_out, gdst_ref[step * nlg + g], sem.at[s])

    def group_wait(s):
        _group_copy(loc.at[s], 0, xs_out, 0, sem.at[s]).wait()

    def zero_pad(e, start):
        if start:
            _for_groups(npad_ref[e], lambda g: _group_copy(zeros, 0, xs_out, pad0_ref[e] + g, zsem).start())
        else:
            _for_groups(npad_ref[e], lambda g: _group_copy(zeros, 0, xs_out, 0, zsem).wait())

    def zero_block(t):
        first = pl.multiple_of(tail_ref[0] + t * MOE_BM, MOE_BM)
        return pltpu.make_async_copy(zeros, xs_out.at[pl.ds(first, MOE_BM), :], zsem)

    @pl.when(i == 0)
    def _():
        zeros[...] = jnp.zeros(zeros.shape, zeros.dtype)
        for e in range(ne):
            zero_pad(e, True)
        _for_groups(tail_ref[1], lambda t: zero_block(t).start())

    @pl.when(i >= 2)
    def _():
        _for_groups(ngt_ref[i - 2], lambda g: group_wait(slot))

    pos1 = pos_ref[4:5, :].astype(jnp.int32)
    pos2 = pos_ref[5:6, :].astype(jnp.int32)
    half = u_ref.shape[1] // 2
    u = u_ref[...]
    rc = 256
    for r in range(0, nl, rc):
        p = lax.broadcasted_iota(jnp.int32, (rc, tm), 0) + r
        onehot = jnp.where((p == pos1) | (p == pos2), 1.0, 0.0).astype(BF16)
        rows = _dot(onehot, u)
        loc[slot, r:r + rc, :] = _pack_halves(rows[:, :half], rows[:, half:])
    _for_groups(ngt_ref[i], lambda g, pr: copy(i, slot, g).start(priority=pr), alternate=True)

    @pl.when(i == nt - 1)
    def _():
        @pl.when(i >= 1)
        def _():
            _for_groups(ngt_ref[i - 1], lambda g: group_wait(1 - slot))

        _for_groups(ngt_ref[i], lambda g: group_wait(slot))
        for e in range(ne):
            zero_pad(e, False)
        _for_groups(tail_ref[1], lambda t: pltpu.make_async_copy(zeros, xs_out.at[0:MOE_BM, :], zsem).wait())


def _dispatch_call(u, pos_rows, gdst, ngt, pad0, npad, tail, nrows, tm, nl):
    t, d = u.shape
    ne = npad.shape[0]
    grid_spec = pltpu.PrefetchScalarGridSpec(
        num_scalar_prefetch=5,
        grid=(t // tm,),
        in_specs=[
            pl.BlockSpec((tm, d), lambda i, *_: (i, 0)),
            pl.BlockSpec((None, SUBLANES, tm), lambda i, *_: (i, 0, 0)),
        ],
        out_specs=pl.BlockSpec(memory_space=pl.ANY),
        scratch_shapes=[pltpu.VMEM((2, nl, d // 2), U32), pltpu.VMEM((MOE_BM, d // 2), U32),
                        pltpu.SemaphoreType.DMA((2,)), pltpu.SemaphoreType.DMA(())],
    )
    return pl.pallas_call(
        functools.partial(_dispatch_kernel, tm=tm, nl=nl, ne=ne),
        out_shape=jax.ShapeDtypeStruct((nrows, d // 2), U32),
        grid_spec=grid_spec,
        compiler_params=_cparams(("arbitrary",)),
        name="moe_dispatch",
    )(gdst, ngt, pad0, npad, tail, u, pos_rows)


def _expert_kernel(row0_ref, nblk_ref, rows_ref, tail_ref, x_hbm, w13_ref, w2_ref, y_hbm, xbuf, ybuf, w13_bf, w2_bf,
                   xsem, ysem, *, hid):
    e = pl.program_id(0)
    n = nblk_ref[e]
    g0 = row0_ref[e] // MOE_BM
    total = tail_ref[0] // MOE_BM
    nx, ny = xbuf.shape[0], ybuf.shape[0]
    ahead = nx - 1
    half = xbuf.shape[2]

    def rows(g):
        return pl.ds(pl.multiple_of(g * MOE_BM, MOE_BM), MOE_BM)

    def x_copy(g):
        return pltpu.make_async_copy(x_hbm.at[rows(g), :], xbuf.at[g % nx], xsem.at[g % nx])

    def y_copy(g, s):
        return pltpu.make_async_copy(ybuf.at[s], y_hbm.at[rows(g), :], ysem.at[s])

    @pl.when(e == 0)
    def _():
        for g in range(ahead):
            @pl.when(g < total)
            def _(g=g):
                x_copy(g).start()

    @pl.when(n > 0)
    def _():
        w13_bf[...] = w13_ref[...].astype(BF16)
        w2_bf[...] = w2_ref[...].astype(BF16)

        def block(c, carry):
            g = g0 + c
            x_copy(g).wait()

            @pl.when(g + ahead < total)
            def _():
                x_copy(g + ahead).start()

            @pl.when(g >= ny)
            def _():
                y_copy(g - ny, g % ny).wait()

            def compute(nrows):
                x_lo, x_hi = _unpack_halves(xbuf[g % nx, 0:nrows, :])
                hb = _dot(x_lo, w13_bf[:half, :]) + _dot(x_hi, w13_bf[half:, :])
                a = hb[:, :hid]
                act = (a * _sigmoid(a)) * hb[:, hid:]
                y = _round_bf16(_dot(act.astype(BF16), w2_bf[...]))
                ybuf[g % ny, 0:nrows, :] = _pack_halves(y[:, :half], y[:, half:])
                if nrows < MOE_BM:
                    ybuf[g % ny, nrows:, :] = jnp.zeros((MOE_BM - nrows, half), ybuf.dtype)

            valid = rows_ref[e] - c * MOE_BM
            for nrows, cond in ((MOE_BM, valid > MOE_BM // 2), (MOE_BM // 2, valid <= MOE_BM // 2)):
                @pl.when(cond)
                def _(nrows=nrows):
                    compute(nrows)

            y_copy(g, g % ny).start()
            return carry

        lax.fori_loop(0, n, block, 0)

    @pl.when(e == pl.num_programs(0) - 1)
    def _():
        for j in range(ny):
            @pl.when(total - ny + j >= 0)
            def _(j=j):
                g = total - ny + j
                y_copy(g, g % ny).wait()

        ybuf[0] = jnp.zeros(ybuf.shape[1:], ybuf.dtype)
        _for_groups(tail_ref[1], lambda t: y_copy(total + t, 0).start())
        _for_groups(tail_ref[1], lambda t: y_copy(total + t, 0).wait())


def _expert_call(xs, row0, nblk, nrows, tail, w13, w2, l):
    r, half = xs.shape
    d = 2 * half
    ne = w13.shape[1]
    hid = w2.shape[2]
    nx, ny = 6, 4
    grid_spec = pltpu.PrefetchScalarGridSpec(
        num_scalar_prefetch=4,
        grid=(ne,),
        in_specs=[
            pl.BlockSpec(memory_space=pl.ANY),
            pl.BlockSpec((None, None, d, 2 * hid), lambda e, *_: (l, e, 0, 0)),
            pl.BlockSpec((None, None, hid, d), lambda e, *_: (l, e, 0, 0)),
        ],
        out_specs=pl.BlockSpec(memory_space=pl.ANY),
        scratch_shapes=[
            pltpu.VMEM((nx, MOE_BM, half), U32), pltpu.VMEM((ny, MOE_BM, half), U32),
            pltpu.VMEM((d, 2 * hid), BF16), pltpu.VMEM((hid, d), BF16),
            pltpu.SemaphoreType.DMA((nx,)), pltpu.SemaphoreType.DMA((ny,)),
        ],
    )
    return pl.pallas_call(
        functools.partial(_expert_kernel, hid=hid),
        out_shape=jax.ShapeDtypeStruct((r, half), U32),
        grid_spec=grid_spec,
        compiler_params=_cparams(("arbitrary",)),
        name="moe_experts",
    )(row0, nblk, nrows, tail, xs, w13, w2)


def _combine_kernel(gdst_ref, ngt_ref, x_ref, rt_ref, gt_ref, nw_ref, sh_ref, sc_ref, ys_ref, *rest,
                    lay, tm, nl, final, gated):
    if gated:
        wh_ref, wl_ref, out_a, out_b, out_g, loc, sem = rest
    else:
        out_a, out_b, loc, sem = rest
    i = pl.program_id(0)
    nt = pl.num_programs(0)
    slot = i % 2
    nlg = nl // SUBLANES

    def copy(step, s, g):
        return _group_copy(ys_ref, gdst_ref[step * nlg + g], loc.at[s], g, sem.at[s])

    @pl.when(i == 0)
    def _():
        loc[...] = jnp.zeros(loc.shape, loc.dtype)
        _for_groups(ngt_ref[0], lambda g, pr: copy(0, 0, g).start(priority=pr), alternate=True)

    _for_groups(ngt_ref[i], lambda g: _group_copy(ys_ref, 0, loc.at[slot], 0, sem.at[slot]).wait())

    @pl.when(i + 1 < nt)
    def _():
        _for_groups(ngt_ref[i + 1], lambda g, pr: copy(i + 1, 1 - slot, g).start(priority=pr), alternate=True)

    rt = rt_ref[...]
    gate1, gate2 = rt[:, 2:3], rt[:, 3:4]
    pos1, pos2 = rt[:, 4:5].astype(jnp.int32), rt[:, 5:6].astype(jnp.int32)
    half = loc.shape[2]
    rc = 256
    mix_lo = jnp.zeros((tm, half), F32)
    mix_hi = jnp.zeros((tm, half), F32)
    for r in range(0, nl, rc):
        p = lax.broadcasted_iota(jnp.int32, (tm, rc), 1) + r
        wgt = (jnp.where(p == pos1, gate1, 0.0) + jnp.where(p == pos2, gate2, 0.0)).astype(BF16)
        y_lo, y_hi = _unpack_halves(loc[slot, r:r + rc, :])
        mix_lo = mix_lo + _dot(wgt, y_lo)
        mix_hi = mix_hi + _dot(wgt, y_hi)
    grp = lay.group(i * tm)
    gate = gt_ref[pl.ds(grp, 1), :]
    nw = nw_ref[...]
    sh = sh_ref[pl.ds(grp, 1), :]
    sc = sc_ref[pl.ds(grp, 1), :]
    er = tm if final else 128
    for r in range(0, tm, er):
        rs = slice(r, r + er)
        x_lo = x_ref[rs, :half] + gate[:, :half] * mix_lo[rs]
        x_hi = x_ref[rs, half:] + gate[:, half:] * mix_hi[rs]
        ms = (jnp.sum(x_lo * x_lo, axis=-1, keepdims=True)
              + jnp.sum(x_hi * x_hi, axis=-1, keepdims=True)) / (2 * half)
        inv = lax.rsqrt(ms + EPS)
        if final:
            y_lo = x_lo * inv * nw[:, :half]
            y_hi = x_hi * inv * nw[:, half:]
            for ref, cond in ((out_a, i * tm < lay.tp), (out_b, i * tm >= lay.tp)):
                @pl.when(cond)
                def _(ref=ref, y_lo=y_lo, y_hi=y_hi):
                    ref[rs, :half] = y_lo
                    ref[rs, half:] = y_hi
            continue
        out_a[rs, :half] = x_lo
        out_a[rs, half:] = x_hi
        u_lo = x_lo * inv * nw[:, :half] * (1.0 + sc[:, :half]) + sh[:, :half]
        u_hi = x_hi * inv * nw[:, half:] * (1.0 + sc[:, half:]) + sh[:, half:]
        if gated:
            (a_hi, a_lo), (b_hi, b_lo) = _split2(u_lo), _split2(u_hi)
            wh, wl = wh_ref[...], wl_ref[...]
            out_g[rs, :] = (_dot_nt(a_hi, wh[:, :half]) + _dot_nt(b_hi, wh[:, half:])
                            + _dot_nt(a_lo, wh[:, :half]) + _dot_nt(b_lo, wh[:, half:])
                            + _dot_nt(a_hi, wl[:, :half]) + _dot_nt(b_hi, wl[:, half:]))
            out_b[rs, :half] = a_hi
            out_b[rs, half:] = b_hi
        else:
            out_b[rs, :half] = u_lo.astype(BF16)
            out_b[rs, half:] = u_hi.astype(BF16)


def _combine_call(lay, x, route, mods, l, mod_next, gdst, ngt, ys, tm, nl, final, gate_w=None):
    t, d = x.shape
    rows = pl.BlockSpec((tm, d), lambda i, *_: (i, 0))
    gated = gate_w is not None
    extra_in, extra_args = [], ()
    if final:
        ntp = lay.tp // tm
        out_specs = (pl.BlockSpec((tm, d), lambda i, *_: (jnp.minimum(i, ntp - 1), 0)),
                     pl.BlockSpec((tm, d), lambda i, *_: (jnp.maximum(i - ntp, 0), 0)))
        out_shape = (jax.ShapeDtypeStruct((lay.tp, d), F32), jax.ShapeDtypeStruct((lay.ts, d), F32))
    else:
        out_specs = (rows, rows)
        out_shape = (jax.ShapeDtypeStruct(x.shape, F32), jax.ShapeDtypeStruct(x.shape, BF16))
        if gated:
            wspec = pl.BlockSpec((LANES, d), lambda i, *_: (0, 0))
            extra_in, extra_args = [wspec, wspec], tuple(gate_w)
            out_specs += (pl.BlockSpec((tm, LANES), lambda i, *_: (i, 0)),)
            out_shape += (jax.ShapeDtypeStruct((t, LANES), F32),)
    grid_spec = pltpu.PrefetchScalarGridSpec(
        num_scalar_prefetch=2,
        grid=(t // tm,),
        in_specs=[
            rows,
            pl.BlockSpec((tm, SUBLANES), lambda i, *_: (i, 0)),
            lay.mod_spec(l, 5),
        ] + mod_next.specs + [pl.BlockSpec(memory_space=pl.ANY)] + extra_in,
        out_specs=out_specs,
        scratch_shapes=[pltpu.VMEM((2, nl, d // 2), U32), pltpu.SemaphoreType.DMA((2,))],
    )
    return pl.pallas_call(
        functools.partial(_combine_kernel, lay=lay, tm=tm, nl=nl, final=final, gated=gated),
        out_shape=out_shape,
        grid_spec=grid_spec,
        compiler_params=_cparams(("arbitrary",)),
        name="moe_combine_final" if final else "moe_combine",
    )(gdst, ngt, x, route, mods, *mod_next.args, ys, *extra_args)


def _moe_layer(lay, x, u2, mods, l, mod_next, final, wr_t, upper, ltri, w13, w2, ng, ne, gate_w=None):
    t, d = x.shape
    tm = upper.shape[0]
    nt = t // tm
    bmg = MOE_BM // SUBLANES
    nl = -(-(MOE_TOP_K * tm + ne * (SUBLANES - 1)) // 256) * 256
    nlg = nl // SUBLANES
    route_t, counts = _router_call(u2, wr_t, upper, ltri, ng, ne)
    route = jnp.swapaxes(route_t, 1, 2).reshape(t, SUBLANES)

    c8 = counts[:, :, 0].astype(jnp.int32)
    lend = jnp.cumsum(c8, axis=1)
    lstart = lend - c8
    ngt = lend[:, -1].astype(jnp.int32)
    tot = jnp.sum(c8, axis=0)
    padded = (tot + bmg - 1) // bmg * bmg
    gend = jnp.cumsum(padded)
    gbase = (gend - padded)[None, :] + jnp.cumsum(c8, axis=0) - c8
    nb = -(-(MOE_TOP_K * t + nt * ne * (SUBLANES - 1)) // MOE_BM) + ne
    g = jnp.arange(nlg, dtype=jnp.int32)[None, :, None]
    owner = (g >= lstart[:, None, :]) & (g < lend[:, None, :])
    gdst = g[:, :, 0] + jnp.sum(jnp.where(owner, (gbase - lstart)[:, None, :], 0), axis=-1)
    gdst = gdst.reshape(nt * nlg).astype(jnp.int32)
    row0 = ((gend - padded) * SUBLANES).astype(jnp.int32)
    nblk = (padded // bmg).astype(jnp.int32)
    tail = jnp.stack([gend[-1] * SUBLANES, nb - gend[-1] // bmg]).astype(jnp.int32)

    pad0 = (gend - padded + tot).astype(jnp.int32)
    npad = (padded - tot).astype(jnp.int32)
    xs = _dispatch_call(u2, route_t, gdst, ngt, pad0, npad, tail, nb * MOE_BM, tm, nl)
    ys = _expert_call(xs, row0, nblk, (tot * SUBLANES).astype(jnp.int32), tail, w13, w2, l)
    return _combine_call(lay, x, route, mods, l, mod_next, gdst, ngt, ys, tm, nl, final, gate_w)


def _lower_tri(n, strict):
    r = np.arange(n)
    m = (r[None, :] < r[:, None]) if strict else (r[None, :] <= r[:, None])
    return jnp.asarray(m.astype(np.float32), dtype=BF16)


def kernel(x_prompt, x_sample, state_C, state_n, state_m, c, c_ctx, ada_w, ada_b, norm1_w, norm2_w, m_w_in, m_b_gate, m_head_norm_w, m_w_out, f_w_out, cv_w_pw1, cv_b_pw1, cv_w_dw, cv_b_dw, cv_ln_w, cv_ln_b, cv_w_pw2, cv_b_pw2, r_w_group, r_w_expert, e_w13, e_w2, final_norm_w):
    nbp, sp, d = x_prompt.shape
    nbs, ss, _ = x_sample.shape
    assert ss % GRID_W == 0
    lay = _Layout(nbp, sp, nbs, ss, d)
    depth = ada_w.shape[0]
    nh, dh = state_C.shape[3], state_C.shape[4]
    di = nh * dh
    ng = r_w_group.shape[2]
    ne = r_w_expert.shape[2]
    assert ng == MOE_GROUPS and ng + ne <= LANES and 4 * nh <= LANES and MOE_TOP_K == 2

    cv = jnp.zeros((lay.ngp, d), F32).at[0].set(c_ctx).at[1:1 + nbs].set(c)
    mods = _ada_call(cv, ada_w, ada_b)

    tri_l = _lower_tri(MLSTM_L, strict=False)
    upper = _lower_tri(lay.row_tile(512), strict=True).T
    ltri = _lower_tri(ne, strict=True)
    row = lambda a: a.reshape(1, -1)
    w_in_t = jnp.swapaxes(m_w_in, 1, 2)
    tn = 1024
    assert di % tn == 0
    nkb = di // tn

    def gate_weights(j):
        wg = jnp.zeros((LANES, d), F32).at[:4 * nh].set(w_in_t[j, 4 * di:])
        return _split2(wg)

    x, u1, graw = _prep_call(lay, x_prompt.reshape(lay.tp, d), x_sample.reshape(lay.ts, d),
                             _Mod(lay, mods, norm1_w[0], 0, 0), gate_weights(0))
    y = None
    qvos, kts, gpts = [], [], []
    for l in range(depth):
        j, kind = l // N_MIXERS, l % N_MIXERS
        mod2 = _Mod(lay, mods, norm2_w[l], l, 3)
        if kind == 0:
            bg = jnp.zeros((1, LANES), F32).at[0, :4 * nh].set(m_b_gate[j])
            gp = _gates_call(lay, graw, bg, tri_l, nh)
            gpt = gp[:, :4 * nh].T
            qvo_blocks = list(range(nkb)) + list(range(2 * nkb, 4 * nkb))
            qvo = _proj_call(lay, u1, w_in_t, j, qvo_blocks, tn, transposed=False)
            kt = _proj_call(lay, u1, w_in_t, j, list(range(nkb, 2 * nkb)), tn, transposed=True)
            hw = row(m_head_norm_w[j])
            hg_p = _mlstm_call(lay, qvo, kt, gp, gpt, hw, nh, dh, prompt=True)
            n0 = state_n[:, j].reshape(nbs, 2 * nh, dh)
            m0 = jnp.broadcast_to(state_m[:, j].reshape(nbs, 2 * nh, 1), (nbs, 2 * nh, LANES))
            hg_s = _mlstm_call(lay, qvo, kt, gp, gpt, hw, nh, dh, prompt=False, state=(state_C, j, n0, m0))
            x, u2 = _mm_res_call(lay, hg_p, hg_s, m_w_out[j].astype(BF16), x, mods, l, mod2)
            qvos.append(qvo)
            kts.append(kt)
            gpts.append(gpt)
        elif kind == 1:
            wo = f_w_out[j].astype(BF16)
            u2 = jnp.zeros((lay.t, d), BF16)
            x, u2 = _fnet_call(lay, x, u1, u2, mods, l, wo, mod2, prompt=True)
            x, u2 = _fnet_call(lay, x, u1, u2, mods, l, wo, mod2, prompt=False)
        else:
            glu = _glu_call(lay, u1, cv_w_pw1[j].astype(BF16), row(cv_b_pw1[j]))
            x, u2 = _conv_call(lay, glu, cv_w_dw[j], cv_b_dw[j], cv_ln_w[j], cv_ln_b[j], cv_w_pw2[j].astype(BF16),
                               cv_b_pw2[j], x, mods, l, mod2)
        wr = jnp.zeros((LANES, d), F32).at[:ng].set(r_w_group[l].T).at[SUBLANES:SUBLANES + ne].set(r_w_expert[l].T)
        final = l + 1 == depth
        mod_next = _Mod(lay, mods, final_norm_w, l, 0) if final else _Mod(lay, mods, norm1_w[l + 1], l + 1, 0)
        next_mlstm = not final and (l + 1) % N_MIXERS == 0
        outs = _moe_layer(lay, x, u2, mods, l, mod_next, final, wr.astype(BF16), upper, ltri, e_w13, e_w2, ng, ne,
                          gate_weights((l + 1) // N_MIXERS) if next_mlstm else None)
        if final:
            y = outs
        elif next_mlstm:
            x, u1, graw = outs
        else:
            x, u1 = outs

    y_prompt = y[0].reshape(nbp, sp, d)
    y_sample = y[1].reshape(nbs, ss, d)
    new_c, new_n, new_m = _state_call(lay, qvos, kts, gpts, nh, dh)
    return (y_prompt, y_sample, new_c, new_n, new_m)
```
